```python
import math
import jax, jax.numpy as jnp
from jax import lax
import numpy as np

D_MODEL = 1024
BATCH = 32
SEQ = 256
DEPTH = 1
DEC_BATCH = 2
DEC_SEQ = 1024
PAST_LEN = 256

GRID_W = 64
N_HEADS = 6
HEAD_DIM = 64
V_DIM = 2 * HEAD_DIM
ATTN_QK_WIDTH = N_HEADS * 2 * HEAD_DIM
ATTN_V_WIDTH = N_HEADS * V_DIM
N_FOURIER_GROUPS = 4
FOURIER_GROUP_DIM = 64
FOURIER_WIDTH = N_FOURIER_GROUPS * FOURIER_GROUP_DIM
IN_WIDTH = 2 * ATTN_QK_WIDTH + ATTN_V_WIDTH + FOURIER_WIDTH + 2 * D_MODEL
N_EXPERTS = 16
CAPACITY_FACTOR = 2
D_FF = 2816
ROPE_THETA = 10000.0
Q_BLOCK = 128
EPS = 1e-6
SUBLN_EPS = 1e-5
N_MOD = 6

kernel_name = 'hybrid_diffusion_diffattn_fnet_ec'


def rmsnorm(x, g, eps=EPS):
    x32 = x.astype(jnp.float32)
    y = x32 * lax.rsqrt(jnp.mean(x32 * x32, axis=-1, keepdims=True) + eps)
    return (y * g.astype(jnp.float32)).astype(x.dtype)


def modulation(cond, w_mod, b_mod):
    m = jnp.einsum('...d,de->...e', jax.nn.silu(cond), w_mod) + b_mod
    m = m.reshape(m.shape[:-1] + (1, N_MOD, D_MODEL))
    return [m[..., i, :] for i in range(N_MOD)]


def rope_1d(x, pos):
    half = x.shape[-1] // 2
    freqs = ROPE_THETA ** (-jnp.arange(half, dtype=jnp.float32) / half)
    ang = pos.astype(jnp.float32)[:, None] * freqs
    cos, sin = jnp.cos(ang), jnp.sin(ang)
    x32 = x.astype(jnp.float32)
    x1, x2 = x32[..., :half], x32[..., half:]
    return jnp.concatenate([x1 * cos - x2 * sin, x1 * sin + x2 * cos], axis=-1).astype(x.dtype)


def axial_rope(x):
    S = x.shape[-2]
    rows = S // GRID_W
    row = jnp.repeat(jnp.arange(rows), GRID_W)
    col = jnp.tile(jnp.arange(GRID_W), rows)
    half = HEAD_DIM // 2
    return jnp.concatenate([rope_1d(x[..., :half], row), rope_1d(x[..., half:], col)], axis=-1)


def diff_attention(q, k, v, lam):
    B, H, _, Sq, DH = q.shape
    nb = Sq // Q_BLOCK
    scale = DH ** -0.5
    qb = q.reshape(B, H, 2, nb, Q_BLOCK, DH).transpose(3, 0, 1, 2, 4, 5)

    def block(qi):
        s = jnp.einsum('bhcqd,bhckd->bhcqk', qi, k).astype(jnp.float32) * scale
        p = jax.nn.softmax(s, axis=-1)
        a = p[:, :, 0] - lam * p[:, :, 1]
        return jnp.einsum('bhqk,bhkd->bhqd', a.astype(v.dtype), v)

    o = lax.map(block, qb)
    return o.transpose(1, 2, 0, 3, 4).reshape(B, H, Sq, v.shape[-1])


def token_mixer(h, w_in, lam, lam_init, g_subln, w_pa, w_pf, w_out, positional, ctx_k, ctx_v):
    B, S, _ = h.shape
    z = jnp.einsum('bsd,de->bse', h, w_in)
    splits = [ATTN_QK_WIDTH, 2 * ATTN_QK_WIDTH, 2 * ATTN_QK_WIDTH + ATTN_V_WIDTH,
              2 * ATTN_QK_WIDTH + ATTN_V_WIDTH + FOURIER_WIDTH,
              2 * ATTN_QK_WIDTH + ATTN_V_WIDTH + FOURIER_WIDTH + D_MODEL]
    q, k, v, f, ga, gf = jnp.split(z, splits, axis=-1)
    q = q.reshape(B, S, N_HEADS, 2, HEAD_DIM).transpose(0, 2, 3, 1, 4)
    k = k.reshape(B, S, N_HEADS, 2, HEAD_DIM).transpose(0, 2, 3, 1, 4)
    v = v.reshape(B, S, N_HEADS, V_DIM).transpose(0, 2, 1, 3)
    if positional:
        q = axial_rope(q)
        k = axial_rope(k)
    k_all, v_all = k, v
    if ctx_k is not None:
        k_all = jnp.concatenate([ctx_k, k], axis=3)
        v_all = jnp.concatenate([ctx_v, v], axis=2)
    o = diff_attention(q, k_all, v_all, lam)
    o = rmsnorm(o, g_subln, SUBLN_EPS) * (1.0 - lam_init)
    o = o.transpose(0, 2, 1, 3).reshape(B, S, ATTN_V_WIDTH)
    fg = f.reshape(B, S, N_FOURIER_GROUPS, FOURIER_GROUP_DIM).astype(jnp.float32)
    fm = jnp.fft.fft2(fg, axes=(1, 3), norm='ortho').real.astype(h.dtype).reshape(B, S, FOURIER_WIDTH)
    merged = jax.nn.sigmoid(ga) * jnp.einsum('bse,ed->bsd', o, w_pa) + \
        jax.nn.sigmoid(gf) * jnp.einsum('bse,ed->bsd', fm, w_pf)
    y = jnp.einsum('bsd,de->bse', merged, w_out)
    return y, k, v


def expert_choice_ffn(h, w_router, w_gate, w_up, w_down):
    B, S, D = h.shape
    T = B * S
    xt = h.reshape(T, D)
    aff = jax.nn.softmax(jnp.einsum('td,de->te', xt, w_router).astype(jnp.float32), axis=-1)
    cap = CAPACITY_FACTOR * T // N_EXPERTS
    g, idx = lax.top_k(aff.T, cap)
    xe = xt[idx]
    hid = jax.nn.silu(jnp.einsum('ecd,edf->ecf', xe, w_gate)) * jnp.einsum('ecd,edf->ecf', xe, w_up)
    ye = jnp.einsum('ecf,efd->ecd', hid, w_down) * g[..., None].astype(h.dtype)
    out = jnp.zeros_like(xt).at[idx.reshape(-1)].add(ye.reshape(-1, D))
    return out.reshape(B, S, D)


def layer(x, cond, lp, lam_init, positional, ctx_k, ctx_v):
    (w_mod, b_mod, g_pre_mix, g_post_mix, g_pre_ffn, g_post_ffn, w_in, lam_q1, lam_k1, lam_q2,
     lam_k2, g_subln, w_proj_attn, w_proj_fourier, w_out, w_router, w_gate, w_up, w_down) = lp
    sh1, sc1, gt1, sh2, sc2, gt2 = modulation(cond, w_mod, b_mod)
    lam = (jnp.exp(jnp.sum(lam_q1.astype(jnp.float32) * lam_k1.astype(jnp.float32)))
           - jnp.exp(jnp.sum(lam_q2.astype(jnp.float32) * lam_k2.astype(jnp.float32))) + lam_init)
    h = rmsnorm(x, g_pre_mix) * (1 + sc1) + sh1
    y, k, v = token_mixer(h, w_in, lam, lam_init, g_subln, w_proj_attn, w_proj_fourier, w_out,
                          positional, ctx_k, ctx_v)
    x = x + gt1 * rmsnorm(y, g_post_mix)
    h = rmsnorm(x, g_pre_ffn) * (1 + sc2) + sh2
    y = expert_choice_ffn(h, w_router, w_gate, w_up, w_down)
    x = x + gt2 * rmsnorm(y, g_post_ffn)
    return x, k, v


def setup_inputs(seed: int = 0) -> dict:
    key = jax.random.key(seed)
    ks = jax.random.split(key, 32)
    f32 = jnp.float32
    nrm = lambda k, shape, s: jax.random.normal(k, shape, f32) * s
    return {
        'x_prompt': nrm(ks[0], (BATCH, SEQ, D_MODEL), 1.0),
        'x_sample': nrm(ks[1], (DEC_BATCH, DEC_SEQ, D_MODEL), 1.0),
        'c': nrm(ks[2], (DEC_BATCH, D_MODEL), 1.0),
        'cache_k': nrm(ks[3], (DEC_BATCH, DEPTH, N_HEADS, PAST_LEN, 2 * HEAD_DIM), 1.0),
        'cache_v': nrm(ks[4], (DEC_BATCH, DEPTH, N_HEADS, PAST_LEN, V_DIM), 1.0),
        'c_ctx': nrm(ks[5], (D_MODEL,), 1.0),
        'w_mod': nrm(ks[6], (DEPTH, D_MODEL, N_MOD * D_MODEL), 0.5 * D_MODEL ** -0.5),
        'b_mod': nrm(ks[7], (DEPTH, N_MOD * D_MODEL), 0.02),
        'g_pre_mix': 1.0 + nrm(ks[8], (DEPTH, D_MODEL), 0.02),
        'g_post_mix': 1.0 + nrm(ks[9], (DEPTH, D_MODEL), 0.02),
        'g_pre_ffn': 1.0 + nrm(ks[10], (DEPTH, D_MODEL), 0.02),
        'g_post_ffn': 1.0 + nrm(ks[11], (DEPTH, D_MODEL), 0.02),
        'w_in': nrm(ks[12], (DEPTH, D_MODEL, IN_WIDTH), D_MODEL ** -0.5),
        'lam_q1': nrm(ks[13], (DEPTH, HEAD_DIM), 0.1),
        'lam_k1': nrm(ks[14], (DEPTH, HEAD_DIM), 0.1),
        'lam_q2': nrm(ks[15], (DEPTH, HEAD_DIM), 0.1),
        'lam_k2': nrm(ks[16], (DEPTH, HEAD_DIM), 0.1),
        'g_subln': 1.0 + nrm(ks[17], (DEPTH, V_DIM), 0.02),
        'w_proj_attn': nrm(ks[18], (DEPTH, ATTN_V_WIDTH, D_MODEL), ATTN_V_WIDTH ** -0.5),
        'w_proj_fourier': nrm(ks[19], (DEPTH, FOURIER_WIDTH, D_MODEL), FOURIER_WIDTH ** -0.5),
        'w_out': nrm(ks[20], (DEPTH, D_MODEL, D_MODEL), D_MODEL ** -0.5),
        'w_router': nrm(ks[21], (DEPTH, D_MODEL, N_EXPERTS), D_MODEL ** -0.5),
        'w_gate': nrm(ks[22], (DEPTH, N_EXPERTS, D_MODEL, D_FF), D_MODEL ** -0.5),
        'w_up': nrm(ks[23], (DEPTH, N_EXPERTS, D_MODEL, D_FF), D_MODEL ** -0.5),
        'w_down': nrm(ks[24], (DEPTH, N_EXPERTS, D_FF, D_MODEL), D_FF ** -0.5),
    }


def reference(x_prompt, x_sample, c, cache_k, cache_v, c_ctx, w_mod, b_mod, g_pre_mix, g_post_mix,
              g_pre_ffn, g_post_ffn, w_in, lam_q1, lam_k1, lam_q2, lam_k2, g_subln, w_proj_attn,
              w_proj_fourier, w_out, w_router, w_gate, w_up, w_down):
    xp, xs = x_prompt, x_sample
    new_k, new_v = [], []
    for l in range(DEPTH):
        lam_init = 0.8 - 0.6 * math.exp(-0.3 * l)
        lp = (w_mod[l], b_mod[l], g_pre_mix[l], g_post_mix[l], g_pre_ffn[l], g_post_ffn[l], w_in[l],
              lam_q1[l], lam_k1[l], lam_q2[l], lam_k2[l], g_subln[l], w_proj_attn[l],
              w_proj_fourier[l], w_out[l], w_router[l], w_gate[l], w_up[l], w_down[l])
        xp, kp, vp = layer(xp, c_ctx, lp, lam_init, False, None, None)
        Bp, H, _, Sp, DH = kp.shape
        new_k.append(kp.transpose(0, 1, 3, 2, 4).reshape(Bp, H, Sp, 2 * DH))
        new_v.append(vp)
        ck = cache_k[:, l]
        Bd, _, L, _ = ck.shape
        ck = ck.reshape(Bd, N_HEADS, L, 2, HEAD_DIM).transpose(0, 1, 3, 2, 4)
        xs, _, _ = layer(xs, c, lp, lam_init, True, ck, cache_v[:, l])
    new_cache_k = jnp.stack(new_k, axis=1)
    new_cache_v = jnp.stack(new_v, axis=1)
    return (xp, xs, new_cache_k, new_cache_v)
```

```python
import functools
import math

import numpy as np
import jax
import jax.numpy as jnp
from jax import lax
from jax.experimental import pallas as pl
from jax.experimental.pallas import tpu as pltpu

F32 = jnp.float32
BF16 = jnp.bfloat16
I32 = jnp.int32

D_MODEL = 1024
N_HEADS = 6
HEAD_DIM = 64
V_DIM = 128
QK_W = 768
FOUR_W = 256
FOUR_G = 64
IN_W = 4608
N_EXPERTS = 16
D_FF = 2816
N_MOD = 6
GRID_W = 64
ROPE_THETA = 10000.0
EPS = 1e-6
SUBLN_EPS = 1e-5

LANES = 128
ROW_BLOCK = 256
SLOT_TILE = 256
TOK_BLOCK = 256
FF_TILE = 256
EXT_W = D_MODEL + LANES
VMEM_LIMIT = 56 * 1024 * 1024


def _cparams(sem):
    return pltpu.CompilerParams(dimension_semantics=sem, vmem_limit_bytes=VMEM_LIMIT)


def _dot(a, b):
    return jnp.dot(a, b, preferred_element_type=F32)


def _rms(x, g, eps):
    return x * lax.rsqrt(jnp.mean(x * x, axis=-1, keepdims=True) + eps) * g


def _whole_vmem():
    return pl.BlockSpec(memory_space=pltpu.MemorySpace.VMEM)


def _mod_kernel(c_ref, w_ref, b_ref, o_ref):
    c = c_ref[...]
    s = c * jax.nn.sigmoid(c)
    o_ref[...] = _dot(s.astype(BF16), w_ref[...].astype(BF16)) + b_ref[...]


def _modulation(cond8, w_mod, b_mod):
    tn = 512
    n = N_MOD * D_MODEL
    return pl.pallas_call(
        _mod_kernel,
        grid=(n // tn,),
        in_specs=[pl.BlockSpec((8, D_MODEL), lambda j: (0, 0)),
                  pl.BlockSpec((D_MODEL, tn), lambda j: (0, j)),
                  pl.BlockSpec((1, tn), lambda j: (0, j))],
        out_specs=pl.BlockSpec((8, tn), lambda j: (0, j)),
        out_shape=jax.ShapeDtypeStruct((8, n), F32),
        compiler_params=_cparams(("arbitrary",)),
        name="modulation",
    )(cond8, w_mod, b_mod)


def _rope(z, cos, sin_signed, first_half):
    fwd = pltpu.roll(z, QK_W - 16, axis=1)
    bwd = pltpu.roll(z, 16, axis=1)
    return z * cos + jnp.where(first_half, fwd, bwd) * sin_signed


def _pre_kernel(*refs, positional, write_cache):
    it = iter(refs)
    x_ref, mod_ref, g_ref, w_ref = next(it), next(it), next(it), next(it)
    if positional:
        cos_ref, sin_ref = next(it), next(it)
    q_ref, k_ref, v_ref, f_ref, ga_ref, gf_ref = (next(it) for _ in range(6))
    if write_cache:
        kc_ref, vc_ref = next(it), next(it)

    m = mod_ref[0]
    sh1 = m[:, 0:D_MODEL]
    sc1 = m[:, D_MODEL:2 * D_MODEL]
    h = _rms(x_ref[...], g_ref[...], EPS) * (1.0 + sc1) + sh1
    hb = h.astype(BF16)

    def proj(lo, hi):
        return _dot(hb, w_ref[:, lo:hi])

    zq = proj(0, QK_W)
    zk = proj(QK_W, 2 * QK_W)
    zv = proj(2 * QK_W, 3 * QK_W)
    if positional:
        lane = lax.broadcasted_iota(I32, (1, QK_W), 1)
        first_half = (lane % 32) < 16
        cos = cos_ref[...]
        sin_signed = sin_ref[...]
        zq = _rope(zq, cos, sin_signed, first_half)
        zk = _rope(zk, cos, sin_signed, first_half)
    q_ref[...] = zq.astype(BF16)
    k_ref[...] = zk.astype(BF16)
    v_ref[...] = zv.astype(BF16)
    if write_cache:
        for hd in range(N_HEADS):
            kc_ref[0, 0, hd] = zk[:, hd * V_DIM:(hd + 1) * V_DIM]
            vc_ref[0, 0, hd] = zv[:, hd * V_DIM:(hd + 1) * V_DIM]
    f0 = 3 * QK_W
    f_ref[...] = proj(f0, f0 + FOUR_W)
    ga_ref[...] = jax.nn.sigmoid(proj(f0 + FOUR_W, f0 + FOUR_W + D_MODEL))
    gf_ref[...] = jax.nn.sigmoid(proj(f0 + FOUR_W + D_MODEL, IN_W))


def _pre_mixer(x2d, mod3, mod_row, g_pre, w_in_b, rope_tabs, seq, write_cache):
    t = x2d.shape[0]
    tm = ROW_BLOCK
    positional = rope_tabs is not None
    blocks_per_seq = seq // tm
    row = lambda i: (i, 0)
    in_specs = [pl.BlockSpec((tm, D_MODEL), row),
                pl.BlockSpec((1, 1, N_MOD * D_MODEL), lambda i: (mod_row(i), 0, 0)),
                pl.BlockSpec((1, D_MODEL), lambda i: (0, 0)),
                _whole_vmem()]
    args = [x2d, mod3, g_pre, w_in_b]
    if positional:
        in_specs += [pl.BlockSpec((tm, QK_W), lambda i: (i % blocks_per_seq, 0))] * 2
        args += list(rope_tabs)
    out_shape = [jax.ShapeDtypeStruct((t, QK_W), BF16)] * 3 + [
        jax.ShapeDtypeStruct((t, FOUR_W), F32),
        jax.ShapeDtypeStruct((t, D_MODEL), F32),
        jax.ShapeDtypeStruct((t, D_MODEL), F32)]
    out_specs = [pl.BlockSpec((tm, QK_W), row)] * 3 + [
        pl.BlockSpec((tm, FOUR_W), row),
        pl.BlockSpec((tm, D_MODEL), row),
        pl.BlockSpec((tm, D_MODEL), row)]
    if write_cache:
        assert seq == tm
        nb = t // seq
        cshape = jax.ShapeDtypeStruct((nb, 1, N_HEADS, seq, V_DIM), F32)
        cspec = pl.BlockSpec((1, 1, N_HEADS, seq, V_DIM), lambda i: (i, 0, 0, 0, 0))
        out_shape += [cshape, cshape]
        out_specs += [cspec, cspec]
    return pl.pallas_call(
        functools.partial(_pre_kernel, positional=positional, write_cache=write_cache),
        grid=(t // tm,),
        in_specs=in_specs,
        out_specs=out_specs,
        out_shape=out_shape,
        compiler_params=_cparams(("arbitrary",)),
        name="pre_mixer",
    )(*args)


def _attn_kernel(*refs, lam_init, has_ctx):
    it = iter(refs)
    lam_ref, gs_ref, q_ref, k_ref, v_ref = (next(it) for _ in range(5))
    if has_ctx:
        ck_ref, cv_ref = next(it), next(it)
    o_ref = next(it)

    lp = lam_ref[...]
    s1 = jnp.sum(lp[0:1] * lp[1:2], axis=-1, keepdims=True)
    s2 = jnp.sum(lp[2:3] * lp[3:4], axis=-1, keepdims=True)
    lam = jnp.exp(s1) - jnp.exp(s2) + lam_init
    lane = lax.broadcasted_iota(I32, (1, V_DIM), 1)
    comp1 = lane < HEAD_DIM
    scale = HEAD_DIM ** -0.5
    nt = (((1,), (1,)), ((), ()))
    for hd in range(N_HEADS):
        sl = slice(hd * V_DIM, (hd + 1) * V_DIM)
        q = q_ref[:, sl]
        k = k_ref[:, sl]
        v = v_ref[:, sl]
        if has_ctx:
            k = jnp.concatenate([ck_ref[0, 0, hd].astype(BF16), k], axis=0)
            v = jnp.concatenate([cv_ref[0, 0, hd].astype(BF16), v], axis=0)
        zero = jnp.zeros_like(q)
        q1 = jnp.where(comp1, q, zero)
        q2 = jnp.where(comp1, zero, q)
        sa = lax.dot_general(q1, k, nt, preferred_element_type=F32) * scale
        sb = lax.dot_general(q2, k, nt, preferred_element_type=F32) * scale
        ea = jnp.exp(sa - jnp.max(sa, axis=-1, keepdims=True))
        eb = jnp.exp(sb - jnp.max(sb, axis=-1, keepdims=True))
        pa = ea / jnp.sum(ea, axis=-1, keepdims=True)
        pb = eb / jnp.sum(eb, axis=-1, keepdims=True)
        a = (pa - lam * pb).astype(BF16)
        o = _dot(a, v)
        o = _rms(o, gs_ref[...], SUBLN_EPS) * (1.0 - lam_init)
        o_ref[:, sl] = o.astype(BF16)


def _attention(lam_p, g_subln, q, k, v, ctx, seq, lam_init):
    t = q.shape[0]
    tq = ROW_BLOCK
    qb = seq // tq
    has_ctx = ctx is not None
    in_specs = [pl.BlockSpec((4, HEAD_DIM), lambda b, i: (0, 0)),
                pl.BlockSpec((1, V_DIM), lambda b, i: (0, 0)),
                pl.BlockSpec((tq, QK_W), lambda b, i: (b * qb + i, 0)),
                pl.BlockSpec((seq, QK_W), lambda b, i: (b, 0)),
                pl.BlockSpec((seq, QK_W), lambda b, i: (b, 0))]
    args = [lam_p, g_subln, q, k, v]
    if has_ctx:
        past = ctx[0].shape[3]
        cspec = pl.BlockSpec((1, 1, N_HEADS, past, V_DIM), lambda b, i: (b, 0, 0, 0, 0))
        in_specs += [cspec, cspec]
        args += list(ctx)
    return pl.pallas_call(
        functools.partial(_attn_kernel, lam_init=lam_init, has_ctx=has_ctx),
        grid=(t // seq, qb),
        in_specs=in_specs,
        out_specs=pl.BlockSpec((tq, QK_W), lambda b, i: (b * qb + i, 0)),
        out_shape=jax.ShapeDtypeStruct((t, QK_W), BF16),
        compiler_params=_cparams(("arbitrary", "arbitrary")),
        name="diff_attention",
    )(*args)


def _fourier_kernel(f_ref, bc_ref, bs_ref, cs_ref, ss_ref, o_ref):
    fb = f_ref[...].astype(BF16)
    u = _dot(fb, bc_ref[...].astype(BF16)).astype(BF16)
    w = _dot(fb, bs_ref[...].astype(BF16)).astype(BF16)
    o_ref[...] = (_dot(cs_ref[...].astype(BF16), u) - _dot(ss_ref[...].astype(BF16), w)).astype(BF16)


def _dft_consts(seq):
    c = np.arange(FOUR_G)
    ang_c = 2.0 * np.pi * ((c[:, None] * c[None, :]) % FOUR_G) / FOUR_G
    eye = np.eye(FOUR_W // FOUR_G)
    bc = np.kron(eye, np.cos(ang_c)) / math.sqrt(FOUR_G)
    bs = np.kron(eye, np.sin(ang_c)) / math.sqrt(FOUR_G)
    s = np.arange(seq)
    ang_s = 2.0 * np.pi * ((s[:, None] * s[None, :]) % seq) / seq
    cs = np.cos(ang_s) / math.sqrt(seq)
    ss = np.sin(ang_s) / math.sqrt(seq)
    return tuple(jnp.asarray(a, dtype=F32) for a in (bc, bs, cs, ss))


def _fourier(f, seq):
    t = f.shape[0]
    bc, bs, cs, ss = _dft_consts(seq)
    const = lambda b: (0, 0)
    return pl.pallas_call(
        _fourier_kernel,
        grid=(t // seq,),
        in_specs=[pl.BlockSpec((seq, FOUR_W), lambda b: (b, 0)),
                  pl.BlockSpec((FOUR_W, FOUR_W), const),
                  pl.BlockSpec((FOUR_W, FOUR_W), const),
                  pl.BlockSpec((seq, seq), const),
                  pl.BlockSpec((seq, seq), const)],
        out_specs=pl.BlockSpec((seq, FOUR_W), lambda b: (b, 0)),
        out_shape=jax.ShapeDtypeStruct((t, FOUR_W), BF16),
        compiler_params=_cparams(("arbitrary",)),
        name="fourier_mix",
    )(f, bc, bs, cs, ss)


def _post_kernel(o_ref, fm_ref, ga_ref, gf_ref, x_ref, mod_ref, gpost_ref, gffn_ref,
                 wpa_ref, wpf_ref, wout_ref, wrt_ref, wrx_ref,
                 x1_ref, h2e_ref, afft_ref):
    a = _dot(o_ref[...], wpa_ref[...])
    b = _dot(fm_ref[...], wpf_ref[...])
    merged = ga_ref[...] * a + gf_ref[...] * b
    y = _dot(merged.astype(BF16), wout_ref[...])
    m = mod_ref[0]
    gt1 = m[:, 2 * D_MODEL:3 * D_MODEL]
    sh2 = m[:, 3 * D_MODEL:4 * D_MODEL]
    sc2 = m[:, 4 * D_MODEL:5 * D_MODEL]
    x1 = x_ref[...] + gt1 * _rms(y, gpost_ref[...], EPS)
    x1_ref[...] = x1
    h2 = _rms(x1, gffn_ref[...], EPS) * (1.0 + sc2) + sh2
    h2b = h2.astype(BF16)

    lt = lax.dot_general(wrt_ref[...], h2b, (((1,), (1,)), ((), ())), preferred_element_type=F32)
    et = jnp.exp(lt - jnp.max(lt, axis=0, keepdims=True))
    afft_ref[...] = et / jnp.sum(et, axis=0, keepdims=True)

    le = _dot(h2b, wrx_ref[...])
    lane = lax.broadcasted_iota(I32, (1, LANES), 1)
    first = lane < N_EXPERTS
    mx = jnp.max(jnp.where(first, le, -jnp.inf), axis=-1, keepdims=True)
    ee = jnp.exp(le - mx)
    aff = ee / jnp.sum(jnp.where(first, ee, 0.0), axis=-1, keepdims=True)
    a1 = aff.astype(BF16).astype(F32)
    r1 = aff - a1
    a2 = r1.astype(BF16).astype(F32)
    a3 = r1 - a2
    ext = jnp.where(first, a1, jnp.where(lane < 2 * N_EXPERTS, a2,
                                         jnp.where(lane < 3 * N_EXPERTS, a3, 0.0)))
    h2e_ref[:, 0:D_MODEL] = h2b
    h2e_ref[:, D_MODEL:EXT_W] = ext.astype(BF16)


def _post_mixer(o, fm, ga, gf, x2d, mod3, mod_row, g_post, g_ffn, wpa, wpf, wout, wrt, wrx):
    t = x2d.shape[0]
    tm = ROW_BLOCK
    row = lambda i: (i, 0)
    const = lambda i: (0, 0)
    return pl.pallas_call(
        _post_kernel,
        grid=(t // tm,),
        in_specs=[pl.BlockSpec((tm, QK_W), row),
                  pl.BlockSpec((tm, FOUR_W), row),
                  pl.BlockSpec((tm, D_MODEL), row),
                  pl.BlockSpec((tm, D_MODEL), row),
                  pl.BlockSpec((tm, D_MODEL), row),
                  pl.BlockSpec((1, 1, N_MOD * D_MODEL), lambda i: (mod_row(i), 0, 0)),
                  pl.BlockSpec((1, D_MODEL), const),
                  pl.BlockSpec((1, D_MODEL), const),
                  pl.BlockSpec((QK_W, D_MODEL), const),
                  pl.BlockSpec((FOUR_W, D_MODEL), const),
                  pl.BlockSpec((D_MODEL, D_MODEL), const),
                  pl.BlockSpec((N_EXPERTS, D_MODEL), const),
                  pl.BlockSpec((D_MODEL, LANES), const)],
        out_specs=[pl.BlockSpec((tm, D_MODEL), row),
                   pl.BlockSpec((tm, EXT_W), row),
                   pl.BlockSpec((N_EXPERTS, tm), lambda i: (0, i))],
        out_shape=[jax.ShapeDtypeStruct((t, D_MODEL), F32),
                   jax.ShapeDtypeStruct((t, EXT_W), BF16),
                   jax.ShapeDtypeStruct((N_EXPERTS, t), F32)],
        compiler_params=_cparams(("arbitrary",)),
        name="post_mixer",
    )(o, fm, ga, gf, x2d, mod3, g_post, g_ffn, wpa, wpf, wout, wrt, wrx)


def _route_kernel(aff_ref, tri_ref, posm_ref, offs_ref, *, cap, n_tok):
    aff = aff_ref[...]
    capf = float(cap)

    def count_ge(v):
        return jnp.sum(jnp.where(aff >= v, 1.0, 0.0), axis=1, keepdims=True)

    def search(i, thr):
        cand = thr | jnp.left_shift(jnp.int32(1), 30 - i)
        return jnp.where(count_ge(pltpu.bitcast(cand, F32)) >= capf, cand, thr)

    thr = lax.fori_loop(0, 31, search, jnp.zeros((N_EXPERTS, 1), I32))
    lo = pltpu.bitcast(thr, F32)
    hi = pltpu.bitcast(thr + 1, F32)

    def refine(i, c):
        lo, hi = c
        mid = lo + (hi - lo) * 0.5
        ok = count_ge(mid) >= capf
        return jnp.where(ok, mid, lo), jnp.where(ok, hi, mid)

    lo, hi = lax.fori_loop(0, 24, refine, (lo, hi))
    gt = aff >= hi
    eq = (aff >= lo) & (aff < hi)
    n_gt = jnp.sum(jnp.where(gt, 1.0, 0.0), axis=1, keepdims=True)
    n_tie = capf - n_gt

    tri = tri_ref[...]
    lane = lax.broadcasted_iota(I32, (1, LANES), 1)
    n_tiles = n_tok // LANES
    carry_eq = jnp.zeros((N_EXPERTS, 1), F32)
    carry_sel = jnp.zeros((N_EXPERTS, 1), F32)
    offs = jnp.zeros((N_EXPERTS, LANES), F32)
    for c in range(n_tiles):
        sl = slice(c * LANES, (c + 1) * LANES)
        eq_f = jnp.where(eq[:, sl], 1.0, 0.0)
        eq_incl = _dot(eq_f.astype(BF16), tri) + carry_eq
        sel_f = jnp.where(gt[:, sl], 1.0, jnp.where(eq_incl <= n_tie, eq_f, 0.0))
        sel_incl = _dot(sel_f.astype(BF16), tri) + carry_sel
        pos = sel_incl - sel_f
        posm_ref[:, sl] = jnp.where(sel_f > 0.5, pos, -1.0).astype(I32)
        offs = jnp.where(lane == c, carry_sel, offs)
        carry_eq = carry_eq + jnp.sum(eq_f, axis=1, keepdims=True)
        carry_sel = carry_sel + jnp.sum(sel_f, axis=1, keepdims=True)
    offs = jnp.where(lane >= n_tiles, carry_sel, offs)
    offs_ref[...] = offs.astype(I32)


def _route(aff_t, cap):
    n_tok = aff_t.shape[1]
    assert n_tok // LANES < LANES
    tri = jnp.asarray(np.triu(np.ones((LANES, LANES))), dtype=BF16)
    return pl.pallas_call(
        functools.partial(_route_kernel, cap=cap, n_tok=n_tok),
        out_shape=[jax.ShapeDtypeStruct((N_EXPERTS, n_tok), I32),
                   jax.ShapeDtypeStruct((N_EXPERTS, LANES), I32)],
        compiler_params=pltpu.CompilerParams(vmem_limit_bytes=VMEM_LIMIT),
        name="route",
    )(aff_t, tri)


def _dispatch_kernel(blo_ref, bhi_ref, posm_ref, h2e_ref, xe_ref, gs_ref, acc_ref):
    e = pl.program_id(0)
    s = pl.program_id(1)
    slot = lax.broadcasted_iota(I32, (SLOT_TILE, TOK_BLOCK), 0) + s * SLOT_TILE
    acc_ref[...] = jnp.zeros_like(acc_ref)

    def body(b, carry):
        pm = posm_ref[e, b]
        onehot = jnp.where(pm == slot, 1.0, 0.0).astype(BF16)
        start = pl.multiple_of(b * TOK_BLOCK, TOK_BLOCK)
        acc_ref[...] += _dot(onehot, h2e_ref[pl.ds(start, TOK_BLOCK), :])
        return carry

    lax.fori_loop(blo_ref[e, s], bhi_ref[e, s], body, 0)
    xe_ref[0] = acc_ref[:, 0:D_MODEL].astype(BF16)
    gs_ref[0] = acc_ref[:, D_MODEL:EXT_W]


def _dispatch(blo, bhi, posm4, h2e, cap):
    ns = cap // SLOT_TILE
    grid_spec = pltpu.PrefetchScalarGridSpec(
        num_scalar_prefetch=2,
        grid=(N_EXPERTS, ns),
        in_specs=[_whole_vmem(), _whole_vmem()],
        out_specs=[pl.BlockSpec((1, SLOT_TILE, D_MODEL), lambda e, s, *_: (e, s, 0)),
                   pl.BlockSpec((1, SLOT_TILE, LANES), lambda e, s, *_: (e, s, 0))],
        scratch_shapes=[pltpu.VMEM((SLOT_TILE, EXT_W), F32)])
    return pl.pallas_call(
        _dispatch_kernel,
        grid_spec=grid_spec,
        out_shape=[jax.ShapeDtypeStruct((N_EXPERTS, cap, D_MODEL), BF16),
                   jax.ShapeDtypeStruct((N_EXPERTS, cap, LANES), F32)],
        compiler_params=_cparams(("arbitrary", "arbitrary")),
        name="dispatch",
    )(blo, bhi, posm4, h2e)


def _moe_kernel(xc_ref, xs_ref, gc_ref, gs_ref, wg_ref, wu_ref, wd_ref, yc_ref, ys_ref, accc_ref, accs_ref):
    e = pl.program_id(0)
    j = pl.program_id(1)

    @pl.when(j == 0)
    def _():
        accc_ref[...] = jnp.zeros_like(accc_ref)
        accs_ref[...] = jnp.zeros_like(accs_ref)

    wg = wg_ref[0].astype(BF16)
    wu = wu_ref[0].astype(BF16)
    wd = wd_ref[0].astype(BF16)
    for x_ref, acc_ref in ((xc_ref, accc_ref), (xs_ref, accs_ref)):
        x = x_ref[0]
        g = _dot(x, wg)
        u = _dot(x, wu)
        hid = (g * jax.nn.sigmoid(g) * u).astype(BF16)
        acc_ref[...] += _dot(hid, wd)

    @pl.when(j == pl.num_programs(1) - 1)
    def _():
        lane = lax.broadcasted_iota(I32, (1, LANES), 1)
        mine = (lane == e) | (lane == e + N_EXPERTS) | (lane == e + 2 * N_EXPERTS)
        for g_ref, acc_ref, y_ref in ((gc_ref, accc_ref, yc_ref), (gs_ref, accs_ref, ys_ref)):
            gate = jnp.sum(jnp.where(mine, g_ref[0], 0.0), axis=-1, keepdims=True)
            y_ref[0] = (acc_ref[...] * gate).astype(BF16)


def _moe(xc, xs, gc, gs, w_gate, w_up, w_down):
    capc, caps = xc.shape[1], xs.shape[1]
    tf = FF_TILE
    per_e = lambda e, j: (e, 0, 0)
    return pl.pallas_call(
        _moe_kernel,
        grid=(N_EXPERTS, D_FF // tf),
        in_specs=[pl.BlockSpec((1, capc, D_MODEL), per_e),
                  pl.BlockSpec((1, caps, D_MODEL), per_e),
                  pl.BlockSpec((1, capc, LANES), per_e),
                  pl.BlockSpec((1, caps, LANES), per_e),
                  pl.BlockSpec((1, D_MODEL, tf), lambda e, j: (e, 0, j)),
                  pl.BlockSpec((1, D_MODEL, tf), lambda e, j: (e, 0, j)),
                  pl.BlockSpec((1, tf, D_MODEL), lambda e, j: (e, j, 0))],
        out_specs=[pl.BlockSpec((1, capc, D_MODEL), per_e),
                   pl.BlockSpec((1, caps, D_MODEL), per_e)],
        out_shape=[jax.ShapeDtypeStruct((N_EXPERTS, capc, D_MODEL), BF16),
                   jax.ShapeDtypeStruct((N_EXPERTS, caps, D_MODEL), BF16)],
        scratch_shapes=[pltpu.VMEM((capc, D_MODEL), F32), pltpu.VMEM((caps, D_MODEL), F32)],
        compiler_params=_cparams(("arbitrary", "arbitrary")),
        name="expert_ffn",
    )(xc, xs, gc, gs, w_gate, w_up, w_down)


def _combine_kernel(slo_ref, shi_ref, post_ref, y_ref, x1_ref, mod_ref, g_ref, o_ref, acc_ref):
    b = pl.program_id(0)
    lane_slot = lax.broadcasted_iota(I32, (TOK_BLOCK, SLOT_TILE), 1)
    acc_ref[...] = jnp.zeros_like(acc_ref)
    pos_all = post_ref[...]
    for e in range(N_EXPERTS):
        pos = pos_all[:, e:e + 1]

        def body(s, carry, e=e, pos=pos):
            onehot = jnp.where(pos - s * SLOT_TILE == lane_slot, 1.0, 0.0).astype(BF16)
            start = pl.multiple_of(s * SLOT_TILE, SLOT_TILE)
            acc_ref[...] += _dot(onehot, y_ref[e, pl.ds(start, SLOT_TILE), :])
            return carry

        lax.fori_loop(slo_ref[e, b], shi_ref[e, b], body, 0)
    gt2 = mod_ref[0][:, 5 * D_MODEL:6 * D_MODEL]
    o_ref[...] = x1_ref[...] + gt2 * _rms(acc_ref[...], g_ref[...], EPS)


def _combine(slo, shi, pos_t, y, x1, mod3, mod_row, g_post_ffn):
    t = x1.shape[0]
    tb = TOK_BLOCK
    grid_spec = pltpu.PrefetchScalarGridSpec(
        num_scalar_prefetch=2,
        grid=(t // tb,),
        in_specs=[pl.BlockSpec((tb, N_EXPERTS), lambda b, *_: (b, 0)),
                  _whole_vmem(),
                  pl.BlockSpec((tb, D_MODEL), lambda b, *_: (b, 0)),
                  pl.BlockSpec((1, 1, N_MOD * D_MODEL), lambda b, *_: (mod_row(b), 0, 0)),
                  pl.BlockSpec((1, D_MODEL), lambda b, *_: (0, 0))],
        out_specs=pl.BlockSpec((tb, D_MODEL), lambda b, *_: (b, 0)),
        scratch_shapes=[pltpu.VMEM((tb, D_MODEL), F32)])
    return pl.pallas_call(
        _combine_kernel,
        grid_spec=grid_spec,
        out_shape=jax.ShapeDtypeStruct((t, D_MODEL), F32),
        compiler_params=_cparams(("arbitrary",)),
        name="combine",
    )(slo, shi, pos_t, y, x1, mod3, g_post_ffn)


def _rope_tables(seq):
    half = HEAD_DIM // 4
    freqs = ROPE_THETA ** (-np.arange(half, dtype=np.float64) / half)
    s = np.arange(seq)
    row = (s // GRID_W)[:, None] * freqs[None, :]
    col = (s % GRID_W)[:, None] * freqs[None, :]
    ang = np.concatenate([row, row, col, col], axis=1)
    ang = np.tile(ang, (1, QK_W // HEAD_DIM))
    lane = np.arange(QK_W)
    sign = np.where((lane % 32) < 16, -1.0, 1.0)[None, :]
    return (jnp.asarray(np.cos(ang), dtype=F32), jnp.asarray(np.sin(ang) * sign, dtype=F32))


def _tile_ranges(offs, cap, n_tok):
    step = TOK_BLOCK // LANES
    nb = n_tok // TOK_BLOCK
    starts = offs[:, 0:nb * step + 1:step]
    lo, hi = starts[:, :-1], starts[:, 1:]
    base = jnp.arange(cap // SLOT_TILE, dtype=I32) * SLOT_TILE
    blo = jnp.sum(hi[:, None, :] <= base[None, :, None], axis=-1).astype(I32)
    bhi = jnp.sum(lo[:, None, :] < (base + SLOT_TILE)[None, :, None], axis=-1).astype(I32)
    slo = (lo // SLOT_TILE).astype(I32)
    shi = ((hi + SLOT_TILE - 1) // SLOT_TILE).astype(I32)
    return blo, bhi, slo, shi


def kernel(x_prompt, x_sample, c, cache_k, cache_v, c_ctx, w_mod, b_mod, g_pre_mix, g_post_mix, g_pre_ffn, g_post_ffn, w_in, lam_q1, lam_k1, lam_q2, lam_k2, g_subln, w_proj_attn, w_proj_fourier, w_out, w_router, w_gate, w_up, w_down):
    assert w_mod.shape[0] == 1
    lam_init = 0.8 - 0.6 * math.exp(-0.3 * 0)
    bp, sp, _ = x_prompt.shape
    bs, ss, _ = x_sample.shape

    cond8 = jnp.concatenate([c_ctx[None, :], c, jnp.zeros((8 - 1 - bs, D_MODEL), F32)], axis=0)
    mod3 = _modulation(cond8, w_mod[0], b_mod).reshape(8, 1, N_MOD * D_MODEL)

    w_in_b = w_in[0].astype(BF16)
    wpa = w_proj_attn[0].astype(BF16)
    wpf = w_proj_fourier[0].astype(BF16)
    wout = w_out[0].astype(BF16)
    wr = w_router[0].astype(BF16)
    wrt = wr.T
    wrx = jnp.concatenate([wr, wr, wr, jnp.zeros((D_MODEL, LANES - 3 * N_EXPERTS), BF16)], axis=1)
    lam_p = jnp.concatenate([lam_q1, lam_k1, lam_q2, lam_k2], axis=0)

    groups = []
    for x, seq, positional, ctx in ((x_prompt, sp, False, None),
                                    (x_sample, ss, True, (cache_k, cache_v))):
        nb = x.shape[0]
        t = nb * seq
        x2d = x.reshape(t, D_MODEL)
        blocks_per_seq = seq // ROW_BLOCK
        if positional:
            mod_row = lambda i, n=blocks_per_seq: 1 + i // n
        else:
            mod_row = lambda i: 0
        pre = _pre_mixer(x2d, mod3, mod_row, g_pre_mix, w_in_b,
                         _rope_tables(seq) if positional else None, seq, write_cache=not positional)
        q, k, v, f, ga, gf = pre[:6]
        o = _attention(lam_p, g_subln, q, k, v, ctx, seq, lam_init)
        fm = _fourier(f, seq)
        x1, h2e, aff_t = _post_mixer(o, fm, ga, gf, x2d, mod3, mod_row, g_post_mix, g_pre_ffn,
                                     wpa, wpf, wout, wrt, wrx)
        cap = 2 * t // N_EXPERTS
        posm, offs = _route(aff_t, cap)
        blo, bhi, slo, shi = _tile_ranges(offs, cap, t)
        posm4 = posm.reshape(N_EXPERTS, t // TOK_BLOCK, 1, TOK_BLOCK)
        xe, gsl = _dispatch(blo, bhi, posm4, h2e, cap)
        groups.append(dict(x1=x1, xe=xe, gsl=gsl, pos_t=posm.T, slo=slo, shi=shi, mod_row=mod_row,
                           cache=pre[6:], shape=x.shape))

    gc, gs_ = groups
    yc, ys = _moe(gc["xe"], gs_["xe"], gc["gsl"], gs_["gsl"], w_gate[0], w_up[0], w_down[0])
    outs = []
    for g, y in ((gc, yc), (gs_, ys)):
        out = _combine(g["slo"], g["shi"], g["pos_t"], y, g["x1"], mod3, g["mod_row"], g_post_ffn)
        outs.append(out.reshape(g["shape"]))
    new_k, new_v = gc["cache"]
    return (outs[0], outs[1], new_k, new_v)
```

```python
import functools
import math

import numpy as np
import jax
import jax.numpy as jnp
from jax import lax
from jax.experimental import pallas as pl
from jax.experimental.pallas import tpu as pltpu

F32 = jnp.float32
BF16 = jnp.bfloat16
I32 = jnp.int32

D_MODEL = 1024
N_HEADS = 6
HEAD_DIM = 64
V_DIM = 128
QK_W = 768
FOUR_W = 256
FOUR_G = 64
IN_W = 4608
N_EXPERTS = 16
D_FF = 2816
N_MOD = 6
GRID_W = 64
ROPE_THETA = 10000.0
EPS = 1e-6
SUBLN_EPS = 1e-5

LANES = 128
ROW_BLOCK = 256
TOK_BLOCK = 256
ROW_CHUNK = 256
FF_TILE = 256
EXT_W = D_MODEL + LANES
TOK_LANE = N_EXPERTS
VMEM_LIMIT = 56 * 1024 * 1024


def _cparams(sem):
    return pltpu.CompilerParams(dimension_semantics=sem, vmem_limit_bytes=VMEM_LIMIT)


def _dot(a, b):
    return jnp.dot(a, b, preferred_element_type=F32)


def _rms(x, g, eps):
    return x * lax.rsqrt(jnp.mean(x * x, axis=-1, keepdims=True) + eps) * g


def _whole_vmem():
    return pl.BlockSpec(memory_space=pltpu.MemorySpace.VMEM)


def _mod_kernel(c_ref, w_ref, b_ref, o_ref):
    c = c_ref[...]
    s = c * jax.nn.sigmoid(c)
    o_ref[...] = _dot(s.astype(BF16), w_ref[...].astype(BF16)) + b_ref[...]


def _modulation(cond8, w_mod, b_mod):
    tn = 512
    n = N_MOD * D_MODEL
    return pl.pallas_call(
        _mod_kernel,
        grid=(n // tn,),
        in_specs=[pl.BlockSpec((8, D_MODEL), lambda j: (0, 0)),
                  pl.BlockSpec((D_MODEL, tn), lambda j: (0, j)),
                  pl.BlockSpec((1, tn), lambda j: (0, j))],
        out_specs=pl.BlockSpec((8, tn), lambda j: (0, j)),
        out_shape=jax.ShapeDtypeStruct((8, n), F32),
        compiler_params=_cparams(("arbitrary",)),
        name="modulation",
    )(cond8, w_mod, b_mod)


def _rope(z, cos, sin_signed, first_half):
    fwd = pltpu.roll(z, QK_W - 16, axis=1)
    bwd = pltpu.roll(z, 16, axis=1)
    return z * cos + jnp.where(first_half, fwd, bwd) * sin_signed


def _pre_kernel(*refs, positional, write_cache):
    it = iter(refs)
    x_ref, mod_ref, g_ref, w_ref = next(it), next(it), next(it), next(it)
    if positional:
        cos_ref, sin_ref = next(it), next(it)
    q_ref, k_ref, v_ref, f_ref, ga_ref, gf_ref = (next(it) for _ in range(6))
    if write_cache:
        kc_ref, vc_ref = next(it), next(it)

    m = mod_ref[0]
    sh1 = m[:, 0:D_MODEL]
    sc1 = m[:, D_MODEL:2 * D_MODEL]
    h = _rms(x_ref[...], g_ref[...], EPS) * (1.0 + sc1) + sh1
    hb = h.astype(BF16)

    def proj(lo, hi):
        return _dot(hb, w_ref[:, lo:hi])

    zq = proj(0, QK_W)
    zk = proj(QK_W, 2 * QK_W)
    zv = proj(2 * QK_W, 3 * QK_W)
    if positional:
        lane = lax.broadcasted_iota(I32, (1, QK_W), 1)
        first_half = (lane % 32) < 16
        cos = cos_ref[...]
        sin_signed = sin_ref[...]
        zq = _rope(zq, cos, sin_signed, first_half)
        zk = _rope(zk, cos, sin_signed, first_half)
    q_ref[...] = zq.astype(BF16)
    k_ref[...] = zk.astype(BF16)
    v_ref[...] = zv.astype(BF16)
    if write_cache:
        for hd in range(N_HEADS):
            kc_ref[0, 0, hd] = zk[:, hd * V_DIM:(hd + 1) * V_DIM]
            vc_ref[0, 0, hd] = zv[:, hd * V_DIM:(hd + 1) * V_DIM]
    f0 = 3 * QK_W
    f_ref[...] = proj(f0, f0 + FOUR_W)
    ga_ref[...] = jax.nn.sigmoid(proj(f0 + FOUR_W, f0 + FOUR_W + D_MODEL))
    gf_ref[...] = jax.nn.sigmoid(proj(f0 + FOUR_W + D_MODEL, IN_W))


def _pre_mixer(x2d, mod3, mod_row, g_pre, w_in_b, rope_tabs, seq, write_cache):
    t = x2d.shape[0]
    tm = ROW_BLOCK
    positional = rope_tabs is not None
    blocks_per_seq = seq // tm
    row = lambda i: (i, 0)
    in_specs = [pl.BlockSpec((tm, D_MODEL), row),
                pl.BlockSpec((1, 1, N_MOD * D_MODEL), lambda i: (mod_row(i), 0, 0)),
                pl.BlockSpec((1, D_MODEL), lambda i: (0, 0)),
                _whole_vmem()]
    args = [x2d, mod3, g_pre, w_in_b]
    if positional:
        in_specs += [pl.BlockSpec((tm, QK_W), lambda i: (i % blocks_per_seq, 0))] * 2
        args += list(rope_tabs)
    out_shape = [jax.ShapeDtypeStruct((t, QK_W), BF16)] * 3 + [
        jax.ShapeDtypeStruct((t, FOUR_W), F32),
        jax.ShapeDtypeStruct((t, D_MODEL), F32),
        jax.ShapeDtypeStruct((t, D_MODEL), F32)]
    out_specs = [pl.BlockSpec((tm, QK_W), row)] * 3 + [
        pl.BlockSpec((tm, FOUR_W), row),
        pl.BlockSpec((tm, D_MODEL), row),
        pl.BlockSpec((tm, D_MODEL), row)]
    if write_cache:
        assert seq == tm
        nb = t // seq
        cshape = jax.ShapeDtypeStruct((nb, 1, N_HEADS, seq, V_DIM), F32)
        cspec = pl.BlockSpec((1, 1, N_HEADS, seq, V_DIM), lambda i: (i, 0, 0, 0, 0))
        out_shape += [cshape, cshape]
        out_specs += [cspec, cspec]
    return pl.pallas_call(
        functools.partial(_pre_kernel, positional=positional, write_cache=write_cache),
        grid=(t // tm,),
        in_specs=in_specs,
        out_specs=out_specs,
        out_shape=out_shape,
        compiler_params=_cparams(("arbitrary",)),
        name="pre_mixer",
    )(*args)


def _attn_kernel(*refs, lam_init, has_ctx):
    it = iter(refs)
    lam_ref, gs_ref, q_ref, k_ref, v_ref = (next(it) for _ in range(5))
    if has_ctx:
        ck_ref, cv_ref = next(it), next(it)
    o_ref = next(it)

    lp = lam_ref[...]
    s1 = jnp.sum(lp[0:1] * lp[1:2], axis=-1, keepdims=True)
    s2 = jnp.sum(lp[2:3] * lp[3:4], axis=-1, keepdims=True)
    lam = jnp.exp(s1) - jnp.exp(s2) + lam_init
    lane = lax.broadcasted_iota(I32, (1, V_DIM), 1)
    comp1 = lane < HEAD_DIM
    scale = HEAD_DIM ** -0.5
    nt = (((1,), (1,)), ((), ()))
    for hd in range(N_HEADS):
        sl = slice(hd * V_DIM, (hd + 1) * V_DIM)
        q = q_ref[:, sl]
        k = k_ref[:, sl]
        v = v_ref[:, sl]
        if has_ctx:
            k = jnp.concatenate([ck_ref[0, 0, hd].astype(BF16), k], axis=0)
            v = jnp.concatenate([cv_ref[0, 0, hd].astype(BF16), v], axis=0)
        zero = jnp.zeros_like(q)
        q1 = jnp.where(comp1, q, zero)
        q2 = jnp.where(comp1, zero, q)
        sa = lax.dot_general(q1, k, nt, preferred_element_type=F32) * scale
        sb = lax.dot_general(q2, k, nt, preferred_element_type=F32) * scale
        ea = jnp.exp(sa - jnp.max(sa, axis=-1, keepdims=True))
        eb = jnp.exp(sb - jnp.max(sb, axis=-1, keepdims=True))
        pa = ea / jnp.sum(ea, axis=-1, keepdims=True)
        pb = eb / jnp.sum(eb, axis=-1, keepdims=True)
        a = (pa - lam * pb).astype(BF16)
        o = _dot(a, v)
        o = _rms(o, gs_ref[...], SUBLN_EPS) * (1.0 - lam_init)
        o_ref[:, sl] = o.astype(BF16)


def _attention(lam_p, g_subln, q, k, v, ctx, seq, lam_init):
    t = q.shape[0]
    tq = ROW_BLOCK
    qb = seq // tq
    has_ctx = ctx is not None
    in_specs = [pl.BlockSpec((4, HEAD_DIM), lambda b, i: (0, 0)),
                pl.BlockSpec((1, V_DIM), lambda b, i: (0, 0)),
                pl.BlockSpec((tq, QK_W), lambda b, i: (b * qb + i, 0)),
                pl.BlockSpec((seq, QK_W), lambda b, i: (b, 0)),
                pl.BlockSpec((seq, QK_W), lambda b, i: (b, 0))]
    args = [lam_p, g_subln, q, k, v]
    if has_ctx:
        past = ctx[0].shape[3]
        cspec = pl.BlockSpec((1, 1, N_HEADS, past, V_DIM), lambda b, i: (b, 0, 0, 0, 0))
        in_specs += [cspec, cspec]
        args += list(ctx)
    return pl.pallas_call(
        functools.partial(_attn_kernel, lam_init=lam_init, has_ctx=has_ctx),
        grid=(t // seq, qb),
        in_specs=in_specs,
        out_specs=pl.BlockSpec((tq, QK_W), lambda b, i: (b * qb + i, 0)),
        out_shape=jax.ShapeDtypeStruct((t, QK_W), BF16),
        compiler_params=_cparams(("arbitrary", "arbitrary")),
        name="diff_attention",
    )(*args)


def _fourier_kernel(f_ref, bc_ref, bs_ref, cs_ref, ss_ref, o_ref):
    fb = f_ref[...].astype(BF16)
    u = _dot(fb, bc_ref[...].astype(BF16)).astype(BF16)
    w = _dot(fb, bs_ref[...].astype(BF16)).astype(BF16)
    o_ref[...] = (_dot(cs_ref[...].astype(BF16), u) - _dot(ss_ref[...].astype(BF16), w)).astype(BF16)


def _dft_consts(seq):
    c = np.arange(FOUR_G)
    ang_c = 2.0 * np.pi * ((c[:, None] * c[None, :]) % FOUR_G) / FOUR_G
    eye = np.eye(FOUR_W // FOUR_G)
    bc = np.kron(eye, np.cos(ang_c)) / math.sqrt(FOUR_G)
    bs = np.kron(eye, np.sin(ang_c)) / math.sqrt(FOUR_G)
    s = np.arange(seq)
    ang_s = 2.0 * np.pi * ((s[:, None] * s[None, :]) % seq) / seq
    cs = np.cos(ang_s) / math.sqrt(seq)
    ss = np.sin(ang_s) / math.sqrt(seq)
    return tuple(jnp.asarray(a, dtype=F32) for a in (bc, bs, cs, ss))


def _fourier(f, seq):
    t = f.shape[0]
    bc, bs, cs, ss = _dft_consts(seq)
    const = lambda b: (0, 0)
    return pl.pallas_call(
        _fourier_kernel,
        grid=(t // seq,),
        in_specs=[pl.BlockSpec((seq, FOUR_W), lambda b: (b, 0)),
                  pl.BlockSpec((FOUR_W, FOUR_W), const),
                  pl.BlockSpec((FOUR_W, FOUR_W), const),
                  pl.BlockSpec((seq, seq), const),
                  pl.BlockSpec((seq, seq), const)],
        out_specs=pl.BlockSpec((seq, FOUR_W), lambda b: (b, 0)),
        out_shape=jax.ShapeDtypeStruct((t, FOUR_W), BF16),
        compiler_params=_cparams(("arbitrary",)),
        name="fourier_mix",
    )(f, bc, bs, cs, ss)


def _post_kernel(o_ref, fm_ref, ga_ref, gf_ref, x_ref, mod_ref, gpost_ref, gffn_ref,
                 wpa_ref, wpf_ref, wout_ref, wrt_ref, wrx_ref,
                 x1_ref, h2e_ref, afft_ref):
    a = _dot(o_ref[...], wpa_ref[...])
    b = _dot(fm_ref[...], wpf_ref[...])
    merged = ga_ref[...] * a + gf_ref[...] * b
    y = _dot(merged.astype(BF16), wout_ref[...])
    m = mod_ref[0]
    gt1 = m[:, 2 * D_MODEL:3 * D_MODEL]
    sh2 = m[:, 3 * D_MODEL:4 * D_MODEL]
    sc2 = m[:, 4 * D_MODEL:5 * D_MODEL]
    x1 = x_ref[...] + gt1 * _rms(y, gpost_ref[...], EPS)
    x1_ref[...] = x1
    h2 = _rms(x1, gffn_ref[...], EPS) * (1.0 + sc2) + sh2
    h2b = h2.astype(BF16)

    lt = lax.dot_general(wrt_ref[...], h2b, (((1,), (1,)), ((), ())), preferred_element_type=F32)
    et = jnp.exp(lt - jnp.max(lt, axis=0, keepdims=True))
    afft_ref[...] = et / jnp.sum(et, axis=0, keepdims=True)

    le = _dot(h2b, wrx_ref[...])
    tm = le.shape[0]
    lane = lax.broadcasted_iota(I32, (1, LANES), 1)
    is_aff = lane < N_EXPERTS
    mx = jnp.max(jnp.where(is_aff, le, -jnp.inf), axis=-1, keepdims=True)
    ee = jnp.exp(le - mx)
    aff = ee / jnp.sum(jnp.where(is_aff, ee, 0.0), axis=-1, keepdims=True)
    tok = (pl.program_id(0) * tm + lax.broadcasted_iota(I32, (tm, 1), 0)).astype(F32)
    h2e_ref[:, 0:D_MODEL] = h2
    h2e_ref[:, D_MODEL:EXT_W] = jnp.where(is_aff, aff, jnp.where(lane == TOK_LANE, tok, 0.0))


def _post_mixer(o, fm, ga, gf, x2d, mod3, mod_row, g_post, g_ffn, wpa, wpf, wout, wrt, wrx):
    t = x2d.shape[0]
    tm = ROW_BLOCK
    row = lambda i: (i, 0)
    const = lambda i: (0, 0)
    return pl.pallas_call(
        _post_kernel,
        grid=(t // tm,),
        in_specs=[pl.BlockSpec((tm, QK_W), row),
                  pl.BlockSpec((tm, FOUR_W), row),
                  pl.BlockSpec((tm, D_MODEL), row),
                  pl.BlockSpec((tm, D_MODEL), row),
                  pl.BlockSpec((tm, D_MODEL), row),
                  pl.BlockSpec((1, 1, N_MOD * D_MODEL), lambda i: (mod_row(i), 0, 0)),
                  pl.BlockSpec((1, D_MODEL), const),
                  pl.BlockSpec((1, D_MODEL), const),
                  pl.BlockSpec((QK_W, D_MODEL), const),
                  pl.BlockSpec((FOUR_W, D_MODEL), const),
                  pl.BlockSpec((D_MODEL, D_MODEL), const),
                  pl.BlockSpec((N_EXPERTS, D_MODEL), const),
                  pl.BlockSpec((D_MODEL, LANES), const)],
        out_specs=[pl.BlockSpec((tm, D_MODEL), row),
                   pl.BlockSpec((tm, EXT_W), row),
                   pl.BlockSpec((N_EXPERTS, tm), lambda i: (0, i))],
        out_shape=[jax.ShapeDtypeStruct((t, D_MODEL), F32),
                   jax.ShapeDtypeStruct((t, EXT_W), F32),
                   jax.ShapeDtypeStruct((N_EXPERTS, t), F32)],
        compiler_params=_cparams(("arbitrary",)),
        name="post_mixer",
    )(o, fm, ga, gf, x2d, mod3, g_post, g_ffn, wpa, wpf, wout, wrt, wrx)


def _route_kernel(aff_ref, tri_ref, posm_ref, qdst_ref, offs_ref, rows_ref, *, cap, n_tok):
    aff = aff_ref[...]
    capf = float(cap)

    def count_ge(v):
        return jnp.sum(jnp.where(aff >= v, 1.0, 0.0), axis=1, keepdims=True)

    def search(i, thr):
        cand = thr | jnp.left_shift(jnp.int32(1), 30 - i)
        return jnp.where(count_ge(pltpu.bitcast(cand, F32)) >= capf, cand, thr)

    thr = lax.fori_loop(0, 31, search, jnp.zeros((N_EXPERTS, 1), I32))
    lo = pltpu.bitcast(thr, F32)
    hi = pltpu.bitcast(thr + 1, F32)

    def refine(i, c):
        lo, hi = c
        mid = lo + (hi - lo) * 0.5
        ok = count_ge(mid) >= capf
        return jnp.where(ok, mid, lo), jnp.where(ok, hi, mid)

    lo, hi = lax.fori_loop(0, 24, refine, (lo, hi))
    gt = aff >= hi
    eq = (aff >= lo) & (aff < hi)
    n_gt = jnp.sum(jnp.where(gt, 1.0, 0.0), axis=1, keepdims=True)
    n_tie = capf - n_gt

    tri = tri_ref[...]
    lane = lax.broadcasted_iota(I32, (1, LANES), 1)
    ei = lax.broadcasted_iota(I32, (N_EXPERTS, N_EXPERTS), 0)
    ej = lax.broadcasted_iota(I32, (N_EXPERTS, N_EXPERTS), 1)
    below = jnp.where(ej < ei, 1.0, 0.0).astype(BF16)
    n_tiles = n_tok // LANES
    carry_eq = jnp.zeros((N_EXPERTS, 1), F32)
    carry_sel = jnp.zeros((N_EXPERTS, 1), F32)
    carry_row = jnp.zeros((N_EXPERTS, 1), F32)
    offs = jnp.zeros((N_EXPERTS, LANES), F32)
    rows = jnp.zeros((N_EXPERTS, LANES), F32)
    for c in range(n_tiles):
        sl = slice(c * LANES, (c + 1) * LANES)
        eq_f = jnp.where(eq[:, sl], 1.0, 0.0)
        eq_incl = _dot(eq_f.astype(BF16), tri) + carry_eq
        sel_f = jnp.where(gt[:, sl], 1.0, jnp.where(eq_incl <= n_tie, eq_f, 0.0))
        sel_b = sel_f.astype(BF16)
        sel_incl = _dot(sel_b, tri) + carry_sel
        posm_ref[:, sl] = jnp.where(sel_f > 0.5, sel_incl - sel_f, -1.0).astype(I32)
        cnt = jnp.broadcast_to(jnp.sum(sel_f, axis=0, keepdims=True), (N_EXPERTS, LANES))
        tok_start = _dot(cnt.astype(BF16), tri) - cnt + carry_row
        qdst_ref[:, sl] = (tok_start + _dot(below, sel_b)).astype(I32)
        offs = jnp.where(lane == c, carry_sel, offs)
        rows = jnp.where(lane == c, carry_row, rows)
        carry_eq = carry_eq + jnp.sum(eq_f, axis=1, keepdims=True)
        carry_sel = carry_sel + jnp.sum(sel_f, axis=1, keepdims=True)
        carry_row = carry_row + jnp.sum(cnt, axis=1, keepdims=True)
    offs_ref[...] = jnp.where(lane >= n_tiles, carry_sel, offs).astype(I32)
    rows_ref[...] = jnp.where(lane >= n_tiles, carry_row, rows).astype(I32)


def _route(aff_t, cap):
    n_tok = aff_t.shape[1]
    assert n_tok // LANES < LANES
    tri = jnp.asarray(np.triu(np.ones((LANES, LANES))), dtype=BF16)
    return pl.pallas_call(
        functools.partial(_route_kernel, cap=cap, n_tok=n_tok),
        out_shape=[jax.ShapeDtypeStruct((N_EXPERTS, n_tok), I32),
                   jax.ShapeDtypeStruct((N_EXPERTS, n_tok), I32),
                   jax.ShapeDtypeStruct((N_EXPERTS, LANES), I32),
                   jax.ShapeDtypeStruct((N_EXPERTS, LANES), I32)],
        compiler_params=pltpu.CompilerParams(vmem_limit_bytes=VMEM_LIMIT),
        name="route",
    )(aff_t, tri)


def _slots_kernel(clo_ref, chi_ref, posm_ref, qdst_ref, idx_ref, qslot_ref):
    e = pl.program_id(0)
    s = pl.program_id(1)
    slot = lax.broadcasted_iota(I32, (LANES, LANES), 0) + s * LANES
    lane = lax.broadcasted_iota(I32, (1, LANES), 1)

    def body(c, carry):
        tok_acc, row_acc = carry
        hit = posm_ref[e, c] == slot
        tok = (c * LANES + lane).astype(F32)
        row = qdst_ref[e, c].astype(F32)
        return tok_acc + jnp.where(hit, tok, 0.0), row_acc + jnp.where(hit, row, 0.0)

    zero = jnp.zeros((LANES, LANES), F32)
    tok_acc, row_acc = lax.fori_loop(clo_ref[e, s], chi_ref[e, s], body, (zero, zero))
    eye = lax.broadcasted_iota(I32, (LANES, LANES), 0) == lax.broadcasted_iota(I32, (LANES, LANES), 1)

    def as_row(acc):
        col = jnp.sum(acc, axis=1, keepdims=True)
        return jnp.sum(jnp.where(eye, col, 0.0), axis=0, keepdims=True).astype(I32)

    idx_ref[0, pl.ds(s, 1), :] = as_row(tok_acc)
    qslot_ref[0, pl.ds(s, 1), :] = as_row(row_acc)


def _slot_lists(clo, chi, posm4, qdst4, cap):
    ns = cap // LANES
    grid_spec = pltpu.PrefetchScalarGridSpec(
        num_scalar_prefetch=2,
        grid=(N_EXPERTS, ns),
        in_specs=[_whole_vmem(), _whole_vmem()],
        out_specs=[pl.BlockSpec((1, ns, LANES), lambda e, s, *_: (e, 0, 0)),
                   pl.BlockSpec((1, ns, LANES), lambda e, s, *_: (e, 0, 0))])
    idx, qslot = pl.pallas_call(
        _slots_kernel,
        grid_spec=grid_spec,
        out_shape=[jax.ShapeDtypeStruct((N_EXPERTS, ns, LANES), I32),
                   jax.ShapeDtypeStruct((N_EXPERTS, ns, LANES), I32)],
        compiler_params=_cparams(("arbitrary", "arbitrary")),
        name="slot_lists",
    )(clo, chi, posm4, qdst4)
    return idx.reshape(N_EXPERTS, cap), qslot.reshape(N_EXPERTS, cap)


def _moe_kernel(idxc_ref, idxs_ref, qc_ref, qs_ref,
                hc_ref, hs_ref, wg_ref, wu_ref, wd_ref, zc_ref, zs_ref,
                xbuf, ybuf, xb_ref, acc_ref, gsem, ssem, *, capc, caps):
    e = pl.program_id(0)
    j = pl.program_id(1)
    last_j = pl.num_programs(1) - 1
    slot = e % 2
    groups = ((hc_ref, idxc_ref, zc_ref, qc_ref, 0, capc), (hs_ref, idxs_ref, zs_ref, qs_ref, capc, caps))

    def row_copies(ex, sl, start):
        for h_ref, idx_ref, z_ref, q_ref, base, cap in groups:
            def body(p, carry, h_ref=h_ref, idx_ref=idx_ref, z_ref=z_ref, q_ref=q_ref, base=base):
                if start:
                    src = h_ref.at[pl.ds(idx_ref[ex, p], 1), :]
                    pltpu.make_async_copy(src, xbuf.at[sl, pl.ds(base + p, 1), :], gsem.at[sl]).start()
                else:
                    dst = z_ref.at[pl.ds(q_ref[ex, p], 1), :]
                    pltpu.make_async_copy(ybuf.at[sl, pl.ds(base + p, 1), :], dst, ssem.at[sl]).start()
                return carry
            lax.fori_loop(0, cap, body, 0, unroll=8)

    def wait_all(buf, sem, sl):
        pltpu.make_async_copy(buf.at[sl], buf.at[sl], sem.at[sl]).wait()

    @pl.when((e == 0) & (j == 0))
    def _():
        row_copies(0, 0, True)

    @pl.when(j == 0)
    def _():
        wait_all(xbuf, gsem, slot)
        xb_ref[...] = xbuf[slot, :, 0:D_MODEL].astype(BF16)
        acc_ref[...] = jnp.zeros_like(acc_ref)

        @pl.when(e + 1 < pl.num_programs(0))
        def _():
            row_copies(e + 1, 1 - slot, True)

    x = xb_ref[...]
    g = _dot(x, wg_ref[0].astype(BF16))
    u = _dot(x, wu_ref[0].astype(BF16))
    hid = (g * jax.nn.sigmoid(g) * u).astype(BF16)
    acc_ref[...] += _dot(hid, wd_ref[0].astype(BF16))

    @pl.when(j == last_j)
    def _():
        @pl.when(e >= 2)
        def _():
            wait_all(ybuf, ssem, slot)

        ext = xbuf[slot, :, D_MODEL:EXT_W]
        lane = lax.broadcasted_iota(I32, (1, LANES), 1)
        gate = jnp.sum(jnp.where(lane == e, ext, 0.0), axis=-1, keepdims=True)
        ybuf[slot, :, 0:D_MODEL] = acc_ref[...] * gate
        ybuf[slot, :, D_MODEL:EXT_W] = ext
        row_copies(e, slot, False)

        @pl.when(e == pl.num_programs(0) - 1)
        def _():
            wait_all(ybuf, ssem, 1 - slot)
            wait_all(ybuf, ssem, slot)


def _moe(idxc, idxs, qc, qs, hc, hs, w_gate, w_up, w_down):
    capc, caps = idxc.shape[1], idxs.shape[1]
    rows = capc + caps
    tf = FF_TILE
    any_spec = pl.BlockSpec(memory_space=pl.ANY)
    grid_spec = pltpu.PrefetchScalarGridSpec(
        num_scalar_prefetch=4,
        grid=(N_EXPERTS, D_FF // tf),
        in_specs=[any_spec, any_spec,
                  pl.BlockSpec((1, D_MODEL, tf), lambda e, j, *_: (e, 0, j)),
                  pl.BlockSpec((1, D_MODEL, tf), lambda e, j, *_: (e, 0, j)),
                  pl.BlockSpec((1, tf, D_MODEL), lambda e, j, *_: (e, j, 0))],
        out_specs=[any_spec, any_spec],
        scratch_shapes=[pltpu.VMEM((2, rows, EXT_W), F32),
                        pltpu.VMEM((2, rows, EXT_W), F32),
                        pltpu.VMEM((rows, D_MODEL), BF16),
                        pltpu.VMEM((rows, D_MODEL), F32),
                        pltpu.SemaphoreType.DMA((2,)),
                        pltpu.SemaphoreType.DMA((2,))])
    return pl.pallas_call(
        functools.partial(_moe_kernel, capc=capc, caps=caps),
        grid_spec=grid_spec,
        out_shape=[jax.ShapeDtypeStruct((N_EXPERTS * capc, EXT_W), F32),
                   jax.ShapeDtypeStruct((N_EXPERTS * caps, EXT_W), F32)],
        compiler_params=_cparams(("arbitrary", "arbitrary")),
        name="expert_ffn",
    )(idxc, idxs, qc, qs, hc, hs, w_gate, w_up, w_down)


PAIR_FIRST, PAIR_LAST, PAIR_VALID = 1, 2, 4


def _combine_kernel(chunk_ref, blk_ref, flag_ref, z_ref, x1_ref, mod_ref, g_ref, o_ref, acc_ref):
    i = pl.program_id(0)
    flag = flag_ref[i]

    @pl.when((flag & PAIR_FIRST) != 0)
    def _():
        acc_ref[...] = jnp.zeros_like(acc_ref)

    @pl.when((flag & PAIR_VALID) != 0)
    def _():
        z = z_ref[...]
        sub = lax.broadcasted_iota(I32, (TOK_BLOCK, ROW_CHUNK), 0)
        eye = lax.broadcasted_iota(I32, (ROW_CHUNK, ROW_CHUNK), 0) == lax.broadcasted_iota(
            I32, (ROW_CHUNK, ROW_CHUNK), 1)
        tok_col = z[:, D_MODEL + TOK_LANE:D_MODEL + TOK_LANE + 1]
        tok_row = jnp.sum(jnp.where(eye, tok_col, 0.0), axis=0, keepdims=True)
        want = (blk_ref[i] * TOK_BLOCK + sub).astype(F32)
        onehot = jnp.where(tok_row == want, 1.0, 0.0).astype(BF16)
        y = z[:, 0:D_MODEL]
        y_hi = y.astype(BF16)
        y_lo = (y - y_hi.astype(F32)).astype(BF16)
        acc_ref[...] += _dot(onehot, y_hi) + _dot(onehot, y_lo)

    @pl.when((flag & PAIR_LAST) != 0)
    def _():
        gt2 = mod_ref[0][:, 5 * D_MODEL:6 * D_MODEL]
        o_ref[...] = x1_ref[...] + gt2 * _rms(acc_ref[...], g_ref[...], EPS)


def _combine(pairs, z, x1, mod3, mod_row, g_post_ffn):
    chunk, blk, flag = pairs
    t = x1.shape[0]
    tb = TOK_BLOCK
    grid_spec = pltpu.PrefetchScalarGridSpec(
        num_scalar_prefetch=3,
        grid=(chunk.shape[0],),
        in_specs=[pl.BlockSpec((ROW_CHUNK, EXT_W), lambda i, c, b, f: (c[i], 0)),
                  pl.BlockSpec((tb, D_MODEL), lambda i, c, b, f: (b[i], 0)),
                  pl.BlockSpec((1, 1, N_MOD * D_MODEL), lambda i, c, b, f: (mod_row(b[i]), 0, 0)),
                  pl.BlockSpec((1, D_MODEL), lambda i, c, b, f: (0, 0))],
        out_specs=pl.BlockSpec((tb, D_MODEL), lambda i, c, b, f: (b[i], 0)),
        scratch_shapes=[pltpu.VMEM((tb, D_MODEL), F32)])
    return pl.pallas_call(
        _combine_kernel,
        grid_spec=grid_spec,
        out_shape=jax.ShapeDtypeStruct((t, D_MODEL), F32),
        compiler_params=_cparams(("arbitrary",)),
        name="combine",
    )(chunk, blk, flag, z, x1, mod3, g_post_ffn)


def _rope_tables(seq):
    half = HEAD_DIM // 4
    freqs = ROPE_THETA ** (-np.arange(half, dtype=np.float64) / half)
    s = np.arange(seq)
    row = (s // GRID_W)[:, None] * freqs[None, :]
    col = (s % GRID_W)[:, None] * freqs[None, :]
    ang = np.concatenate([row, row, col, col], axis=1)
    ang = np.tile(ang, (1, QK_W // HEAD_DIM))
    lane = np.arange(QK_W)
    sign = np.where((lane % 32) < 16, -1.0, 1.0)[None, :]
    return (jnp.asarray(np.cos(ang), dtype=F32), jnp.asarray(np.sin(ang) * sign, dtype=F32))


def _slot_tile_ranges(offs, cap, n_tok):
    nt = n_tok // LANES
    lo, hi = offs[:, :nt], offs[:, 1:nt + 1]
    base = jnp.arange(cap // LANES, dtype=I32) * LANES
    clo = jnp.sum(hi[:, None, :] <= base[None, :, None], axis=-1).astype(I32)
    chi = jnp.sum(lo[:, None, :] < (base + LANES)[None, :, None], axis=-1).astype(I32)
    return clo, chi


def _combine_pairs(rows, n_tok):
    step = TOK_BLOCK // LANES
    nb = n_tok // TOK_BLOCK
    nc = 2 * n_tok // ROW_CHUNK
    starts = rows[0, 0:nb * step + 1:step]
    lo, hi = starts[:-1], starts[1:]
    c_lo = jnp.minimum(lo // ROW_CHUNK, nc - 1)
    c_hi = jnp.maximum((hi + ROW_CHUNK - 1) // ROW_CHUNK, c_lo + 1)
    n = c_hi - c_lo
    ends = jnp.cumsum(n)
    begins = ends - n
    i = jnp.arange(nb + nc, dtype=I32)
    valid = i < ends[-1]
    blk = jnp.minimum(jnp.searchsorted(ends, i, side="right"), nb - 1).astype(I32)
    off = jnp.minimum(i - begins[blk], n[blk] - 1)
    chunk = (c_lo[blk] + off).astype(I32)
    first = valid & (i == begins[blk])
    last = valid & (i == ends[blk] - 1)
    flag = (first * PAIR_FIRST + last * PAIR_LAST + valid * PAIR_VALID).astype(I32)
    return chunk, blk, flag


def kernel(x_prompt, x_sample, c, cache_k, cache_v, c_ctx, w_mod, b_mod, g_pre_mix, g_post_mix, g_pre_ffn, g_post_ffn, w_in, lam_q1, lam_k1, lam_q2, lam_k2, g_subln, w_proj_attn, w_proj_fourier, w_out, w_router, w_gate, w_up, w_down):
    assert w_mod.shape[0] == 1
    lam_init = 0.8 - 0.6 * math.exp(-0.3 * 0)
    bp, sp, _ = x_prompt.shape
    bs, ss, _ = x_sample.shape

    cond8 = jnp.concatenate([c_ctx[None, :], c, jnp.zeros((8 - 1 - bs, D_MODEL), F32)], axis=0)
    mod3 = _modulation(cond8, w_mod[0], b_mod).reshape(8, 1, N_MOD * D_MODEL)

    w_in_b = w_in[0].astype(BF16)
    wpa = w_proj_attn[0].astype(BF16)
    wpf = w_proj_fourier[0].astype(BF16)
    wout = w_out[0].astype(BF16)
    wr = w_router[0].astype(BF16)
    wrt = wr.T
    wrx = jnp.concatenate([wr, jnp.zeros((D_MODEL, LANES - N_EXPERTS), BF16)], axis=1)
    lam_p = jnp.concatenate([lam_q1, lam_k1, lam_q2, lam_k2], axis=0)

    groups = []
    for x, seq, positional, ctx in ((x_prompt, sp, False, None),
                                    (x_sample, ss, True, (cache_k, cache_v))):
        nb = x.shape[0]
        t = nb * seq
        x2d = x.reshape(t, D_MODEL)
        blocks_per_seq = seq // ROW_BLOCK
        if positional:
            mod_row = lambda i, n=blocks_per_seq: 1 + i // n
        else:
            mod_row = lambda i: 0
        pre = _pre_mixer(x2d, mod3, mod_row, g_pre_mix, w_in_b,
                         _rope_tables(seq) if positional else None, seq, write_cache=not positional)
        q, k, v, f, ga, gf = pre[:6]
        o = _attention(lam_p, g_subln, q, k, v, ctx, seq, lam_init)
        fm = _fourier(f, seq)
        x1, h2e, aff_t = _post_mixer(o, fm, ga, gf, x2d, mod3, mod_row, g_post_mix, g_pre_ffn,
                                     wpa, wpf, wout, wrt, wrx)
        cap = 2 * t // N_EXPERTS
        posm, qdst, offs, rows = _route(aff_t, cap)
        clo, chi = _slot_tile_ranges(offs, cap, t)
        tiled = (N_EXPERTS, t // LANES, 1, LANES)
        idx, qslot = _slot_lists(clo, chi, posm.reshape(tiled), qdst.reshape(tiled), cap)
        groups.append(dict(x1=x1, h2e=h2e, idx=idx, qslot=qslot, pairs=_combine_pairs(rows, t),
                           mod_row=mod_row, cache=pre[6:], shape=x.shape))

    gc, gs_ = groups
    zc, zs = _moe(gc["idx"], gs_["idx"], gc["qslot"], gs_["qslot"], gc["h2e"], gs_["h2e"],
                  w_gate[0], w_up[0], w_down[0])
    outs = []
    for g, z in ((gc, zc), (gs_, zs)):
        out = _combine(g["pairs"], z, g["x1"], mod3, g["mod_row"], g_post_ffn)
        outs.append(out.reshape(g["shape"]))
    new_k, new_v = gc["cache"]
    return (outs[0], outs[1], new_k, new_v)
```

```python
import functools
import math

import numpy as np
import jax
import jax.numpy as jnp
from jax import lax
from jax.experimental import pallas as pl
from jax.experimental.pallas import tpu as pltpu

F32 = jnp.float32
BF16 = jnp.bfloat16
I32 = jnp.int32

D_MODEL = 1024
N_HEADS = 6
HEAD_DIM = 64
V_DIM = 128
QK_W = 768
FOUR_W = 256
FOUR_G = 64
IN_W = 4608
N_EXPERTS = 16
D_FF = 2816
N_MOD = 6
GRID_W = 64
ROPE_THETA = 10000.0
EPS = 1e-6
SUBLN_EPS = 1e-5

LANES = 128
ROW_BLOCK = 256
TOK_BLOCK = 256
ROW_CHUNK = 256
FF_TILE = 256
FF_STEPS = D_FF // FF_TILE
ROW_TILE = D_MODEL // LANES
VMEM_LIMIT = 56 * 1024 * 1024


def _cparams(sem):
    return pltpu.CompilerParams(dimension_semantics=sem, vmem_limit_bytes=VMEM_LIMIT)


def _dot(a, b):
    return jnp.dot(a, b, preferred_element_type=F32)


def _rms(x, g, eps):
    return x * lax.rsqrt(jnp.mean(x * x, axis=-1, keepdims=True) + eps) * g


def _whole_vmem():
    return pl.BlockSpec(memory_space=pltpu.MemorySpace.VMEM)


def _mod_kernel(c_ref, w_ref, b_ref, o_ref):
    c = c_ref[...]
    s = c * jax.nn.sigmoid(c)
    o_ref[...] = _dot(s.astype(BF16), w_ref[...].astype(BF16)) + b_ref[...]


def _modulation(cond8, w_mod, b_mod):
    tn = 512
    n = N_MOD * D_MODEL
    return pl.pallas_call(
        _mod_kernel,
        grid=(n // tn,),
        in_specs=[pl.BlockSpec((8, D_MODEL), lambda j: (0, 0)),
                  pl.BlockSpec((D_MODEL, tn), lambda j: (0, j)),
                  pl.BlockSpec((1, tn), lambda j: (0, j))],
        out_specs=pl.BlockSpec((8, tn), lambda j: (0, j)),
        out_shape=jax.ShapeDtypeStruct((8, n), F32),
        compiler_params=_cparams(("arbitrary",)),
        name="modulation",
    )(cond8, w_mod, b_mod)


def _rope(z, cos, sin_signed, first_half):
    fwd = pltpu.roll(z, QK_W - 16, axis=1)
    bwd = pltpu.roll(z, 16, axis=1)
    return z * cos + jnp.where(first_half, fwd, bwd) * sin_signed


def _pre_kernel(*refs, positional, write_cache):
    it = iter(refs)
    x_ref, mod_ref, g_ref, w_ref = next(it), next(it), next(it), next(it)
    if positional:
        cos_ref, sin_ref = next(it), next(it)
    q_ref, k_ref, v_ref, f_ref, ga_ref, gf_ref = (next(it) for _ in range(6))
    if write_cache:
        kc_ref, vc_ref = next(it), next(it)

    m = mod_ref[0]
    sh1 = m[:, 0:D_MODEL]
    sc1 = m[:, D_MODEL:2 * D_MODEL]
    h = _rms(x_ref[...], g_ref[...], EPS) * (1.0 + sc1) + sh1
    hb = h.astype(BF16)

    def proj(lo, hi):
        return _dot(hb, w_ref[:, lo:hi])

    zq = proj(0, QK_W)
    zk = proj(QK_W, 2 * QK_W)
    zv = proj(2 * QK_W, 3 * QK_W)
    if positional:
        lane = lax.broadcasted_iota(I32, (1, QK_W), 1)
        first_half = (lane % 32) < 16
        cos = cos_ref[...]
        sin_signed = sin_ref[...]
        zq = _rope(zq, cos, sin_signed, first_half)
        zk = _rope(zk, cos, sin_signed, first_half)
    q_ref[...] = zq.astype(BF16)
    k_ref[...] = zk.astype(BF16)
    v_ref[...] = zv.astype(BF16)
    if write_cache:
        for hd in range(N_HEADS):
            kc_ref[0, 0, hd] = zk[:, hd * V_DIM:(hd + 1) * V_DIM]
            vc_ref[0, 0, hd] = zv[:, hd * V_DIM:(hd + 1) * V_DIM]
    f0 = 3 * QK_W
    f_ref[...] = proj(f0, f0 + FOUR_W)
    ga_ref[...] = jax.nn.sigmoid(proj(f0 + FOUR_W, f0 + FOUR_W + D_MODEL))
    gf_ref[...] = jax.nn.sigmoid(proj(f0 + FOUR_W + D_MODEL, IN_W))


def _pre_mixer(x2d, mod3, mod_row, g_pre, w_in_b, rope_tabs, seq, write_cache):
    t = x2d.shape[0]
    tm = ROW_BLOCK
    positional = rope_tabs is not None
    blocks_per_seq = seq // tm
    row = lambda i: (i, 0)
    in_specs = [pl.BlockSpec((tm, D_MODEL), row),
                pl.BlockSpec((1, 1, N_MOD * D_MODEL), lambda i: (mod_row(i), 0, 0)),
                pl.BlockSpec((1, D_MODEL), lambda i: (0, 0)),
                _whole_vmem()]
    args = [x2d, mod3, g_pre, w_in_b]
    if positional:
        in_specs += [pl.BlockSpec((tm, QK_W), lambda i: (i % blocks_per_seq, 0))] * 2
        args += list(rope_tabs)
    out_shape = [jax.ShapeDtypeStruct((t, QK_W), BF16)] * 3 + [
        jax.ShapeDtypeStruct((t, FOUR_W), F32),
        jax.ShapeDtypeStruct((t, D_MODEL), F32),
        jax.ShapeDtypeStruct((t, D_MODEL), F32)]
    out_specs = [pl.BlockSpec((tm, QK_W), row)] * 3 + [
        pl.BlockSpec((tm, FOUR_W), row),
        pl.BlockSpec((tm, D_MODEL), row),
        pl.BlockSpec((tm, D_MODEL), row)]
    if write_cache:
        assert seq == tm
        nb = t // seq
        cshape = jax.ShapeDtypeStruct((nb, 1, N_HEADS, seq, V_DIM), F32)
        cspec = pl.BlockSpec((1, 1, N_HEADS, seq, V_DIM), lambda i: (i, 0, 0, 0, 0))
        out_shape += [cshape, cshape]
        out_specs += [cspec, cspec]
    return pl.pallas_call(
        functools.partial(_pre_kernel, positional=positional, write_cache=write_cache),
        grid=(t // tm,),
        in_specs=in_specs,
        out_specs=out_specs,
        out_shape=out_shape,
        compiler_params=_cparams(("arbitrary",)),
        name="pre_mixer",
    )(*args)


def _attn_kernel(*refs, lam_init, has_ctx):
    it = iter(refs)
    lam_ref, gs_ref, q_ref, k_ref, v_ref = (next(it) for _ in range(5))
    if has_ctx:
        ck_ref, cv_ref = next(it), next(it)
    o_ref = next(it)

    lp = lam_ref[...]
    s1 = jnp.sum(lp[0:1] * lp[1:2], axis=-1, keepdims=True)
    s2 = jnp.sum(lp[2:3] * lp[3:4], axis=-1, keepdims=True)
    lam = jnp.exp(s1) - jnp.exp(s2) + lam_init
    lane = lax.broadcasted_iota(I32, (1, V_DIM), 1)
    comp1 = lane < HEAD_DIM
    scale = HEAD_DIM ** -0.5
    nt = (((1,), (1,)), ((), ()))
    for hd in range(N_HEADS):
        sl = slice(hd * V_DIM, (hd + 1) * V_DIM)
        q = q_ref[:, sl]
        k = k_ref[:, sl]
        v = v_ref[:, sl]
        if has_ctx:
            k = jnp.concatenate([ck_ref[0, 0, hd].astype(BF16), k], axis=0)
            v = jnp.concatenate([cv_ref[0, 0, hd].astype(BF16), v], axis=0)
        zero = jnp.zeros_like(q)
        q1 = jnp.where(comp1, q, zero)
        q2 = jnp.where(comp1, zero, q)
        sa = lax.dot_general(q1, k, nt, preferred_element_type=F32) * scale
        sb = lax.dot_general(q2, k, nt, preferred_element_type=F32) * scale
        ea = jnp.exp(sa - jnp.max(sa, axis=-1, keepdims=True))
        eb = jnp.exp(sb - jnp.max(sb, axis=-1, keepdims=True))
        pa = ea / jnp.sum(ea, axis=-1, keepdims=True)
        pb = eb / jnp.sum(eb, axis=-1, keepdims=True)
        a = (pa - lam * pb).astype(BF16)
        o = _dot(a, v)
        o = _rms(o, gs_ref[...], SUBLN_EPS) * (1.0 - lam_init)
        o_ref[:, sl] = o.astype(BF16)


def _attention(lam_p, g_subln, q, k, v, ctx, seq, lam_init):
    t = q.shape[0]
    tq = ROW_BLOCK
    qb = seq // tq
    has_ctx = ctx is not None
    in_specs = [pl.BlockSpec((4, HEAD_DIM), lambda b, i: (0, 0)),
                pl.BlockSpec((1, V_DIM), lambda b, i: (0, 0)),
                pl.BlockSpec((tq, QK_W), lambda b, i: (b * qb + i, 0)),
                pl.BlockSpec((seq, QK_W), lambda b, i: (b, 0)),
                pl.BlockSpec((seq, QK_W), lambda b, i: (b, 0))]
    args = [lam_p, g_subln, q, k, v]
    if has_ctx:
        past = ctx[0].shape[3]
        cspec = pl.BlockSpec((1, 1, N_HEADS, past, V_DIM), lambda b, i: (b, 0, 0, 0, 0))
        in_specs += [cspec, cspec]
        args += list(ctx)
    return pl.pallas_call(
        functools.partial(_attn_kernel, lam_init=lam_init, has_ctx=has_ctx),
        grid=(t // seq, qb),
        in_specs=in_specs,
        out_specs=pl.BlockSpec((tq, QK_W), lambda b, i: (b * qb + i, 0)),
        out_shape=jax.ShapeDtypeStruct((t, QK_W), BF16),
        compiler_params=_cparams(("arbitrary", "arbitrary")),
        name="diff_attention",
    )(*args)


def _fourier_kernel(f_ref, bc_ref, bs_ref, cs_ref, ss_ref, o_ref):
    fb = f_ref[...].astype(BF16)
    u = _dot(fb, bc_ref[...].astype(BF16)).astype(BF16)
    w = _dot(fb, bs_ref[...].astype(BF16)).astype(BF16)
    o_ref[...] = (_dot(cs_ref[...].astype(BF16), u) - _dot(ss_ref[...].astype(BF16), w)).astype(BF16)


def _dft_consts(seq):
    c = np.arange(FOUR_G)
    ang_c = 2.0 * np.pi * ((c[:, None] * c[None, :]) % FOUR_G) / FOUR_G
    eye = np.eye(FOUR_W // FOUR_G)
    bc = np.kron(eye, np.cos(ang_c)) / math.sqrt(FOUR_G)
    bs = np.kron(eye, np.sin(ang_c)) / math.sqrt(FOUR_G)
    s = np.arange(seq)
    ang_s = 2.0 * np.pi * ((s[:, None] * s[None, :]) % seq) / seq
    cs = np.cos(ang_s) / math.sqrt(seq)
    ss = np.sin(ang_s) / math.sqrt(seq)
    return tuple(jnp.asarray(a, dtype=F32) for a in (bc, bs, cs, ss))


def _fourier(f, seq):
    t = f.shape[0]
    bc, bs, cs, ss = _dft_consts(seq)
    const = lambda b: (0, 0)
    return pl.pallas_call(
        _fourier_kernel,
        grid=(t // seq,),
        in_specs=[pl.BlockSpec((seq, FOUR_W), lambda b: (b, 0)),
                  pl.BlockSpec((FOUR_W, FOUR_W), const),
                  pl.BlockSpec((FOUR_W, FOUR_W), const),
                  pl.BlockSpec((seq, seq), const),
                  pl.BlockSpec((seq, seq), const)],
        out_specs=pl.BlockSpec((seq, FOUR_W), lambda b: (b, 0)),
        out_shape=jax.ShapeDtypeStruct((t, FOUR_W), BF16),
        compiler_params=_cparams(("arbitrary",)),
        name="fourier_mix",
    )(f, bc, bs, cs, ss)


def _post_kernel(o_ref, fm_ref, ga_ref, gf_ref, x_ref, mod_ref, gpost_ref, gffn_ref,
                 wpa_ref, wpf_ref, wout_ref, wrt_ref,
                 x1_ref, h2t_ref, afft_ref):
    a = _dot(o_ref[...], wpa_ref[...])
    b = _dot(fm_ref[...], wpf_ref[...])
    merged = ga_ref[...] * a + gf_ref[...] * b
    y = _dot(merged.astype(BF16), wout_ref[...])
    m = mod_ref[0]
    gt1 = m[:, 2 * D_MODEL:3 * D_MODEL]
    sh2 = m[:, 3 * D_MODEL:4 * D_MODEL]
    sc2 = m[:, 4 * D_MODEL:5 * D_MODEL]
    x1 = x_ref[...] + gt1 * _rms(y, gpost_ref[...], EPS)
    x1_ref[...] = x1
    h2 = _rms(x1, gffn_ref[...], EPS) * (1.0 + sc2) + sh2
    h2b = h2.astype(BF16)

    lt = lax.dot_general(wrt_ref[...], h2b, (((1,), (1,)), ((), ())), preferred_element_type=F32)
    et = jnp.exp(lt - jnp.max(lt, axis=0, keepdims=True))
    afft_ref[...] = et / jnp.sum(et, axis=0, keepdims=True)

    tm = h2.shape[0]
    for kc in range(ROW_TILE):
        h2t_ref[pl.ds(kc, tm, stride=ROW_TILE), :] = h2[:, kc * LANES:(kc + 1) * LANES]


def _post_mixer(o, fm, ga, gf, x2d, mod3, mod_row, g_post, g_ffn, wpa, wpf, wout, wrt):
    t = x2d.shape[0]
    tm = ROW_BLOCK
    row = lambda i: (i, 0)
    const = lambda i: (0, 0)
    return pl.pallas_call(
        _post_kernel,
        grid=(t // tm,),
        in_specs=[pl.BlockSpec((tm, QK_W), row),
                  pl.BlockSpec((tm, FOUR_W), row),
                  pl.BlockSpec((tm, D_MODEL), row),
                  pl.BlockSpec((tm, D_MODEL), row),
                  pl.BlockSpec((tm, D_MODEL), row),
                  pl.BlockSpec((1, 1, N_MOD * D_MODEL), lambda i: (mod_row(i), 0, 0)),
                  pl.BlockSpec((1, D_MODEL), const),
                  pl.BlockSpec((1, D_MODEL), const),
                  pl.BlockSpec((QK_W, D_MODEL), const),
                  pl.BlockSpec((FOUR_W, D_MODEL), const),
                  pl.BlockSpec((D_MODEL, D_MODEL), const),
                  pl.BlockSpec((N_EXPERTS, D_MODEL), const)],
        out_specs=[pl.BlockSpec((tm, D_MODEL), row),
                   pl.BlockSpec((tm * ROW_TILE, LANES), row),
                   pl.BlockSpec((N_EXPERTS, tm), lambda i: (0, i))],
        out_shape=[jax.ShapeDtypeStruct((t, D_MODEL), F32),
                   jax.ShapeDtypeStruct((t * ROW_TILE, LANES), F32),
                   jax.ShapeDtypeStruct((N_EXPERTS, t), F32)],
        compiler_params=_cparams(("arbitrary",)),
        name="post_mixer",
    )(o, fm, ga, gf, x2d, mod3, g_post, g_ffn, wpa, wpf, wout, wrt)


def _route_kernel(aff_ref, tri_ref, posm_ref, qdst_ref, span_ref, offs_ref, rows_ref, *, cap, n_tok):
    aff = aff_ref[...]
    capf = float(cap)

    def count_ge(v):
        return jnp.sum(jnp.where(aff >= v, 1.0, 0.0), axis=1, keepdims=True)

    def search(i, thr):
        cand = thr | jnp.left_shift(jnp.int32(1), 30 - i)
        return jnp.where(count_ge(pltpu.bitcast(cand, F32)) >= capf, cand, thr)

    thr = lax.fori_loop(0, 31, search, jnp.zeros((N_EXPERTS, 1), I32))
    lo = pltpu.bitcast(thr, F32)
    hi = pltpu.bitcast(thr + 1, F32)

    def refine(i, c):
        lo, hi = c
        mid = lo + (hi - lo) * 0.5
        ok = count_ge(mid) >= capf
        return jnp.where(ok, mid, lo), jnp.where(ok, hi, mid)

    lo, hi = lax.fori_loop(0, 24, refine, (lo, hi))
    gt = aff >= hi
    eq = (aff >= lo) & (aff < hi)
    n_gt = jnp.sum(jnp.where(gt, 1.0, 0.0), axis=1, keepdims=True)
    n_tie = capf - n_gt

    tri = tri_ref[...]
    lane = lax.broadcasted_iota(I32, (1, LANES), 1)
    ei = lax.broadcasted_iota(I32, (N_EXPERTS, N_EXPERTS), 0)
    ej = lax.broadcasted_iota(I32, (N_EXPERTS, N_EXPERTS), 1)
    below = jnp.where(ej < ei, 1.0, 0.0).astype(BF16)
    sub_sq = lax.broadcasted_iota(I32, (LANES, LANES), 0)
    n_tiles = n_tok // LANES
    carry_eq = jnp.zeros((N_EXPERTS, 1), F32)
    carry_sel = jnp.zeros((N_EXPERTS, 1), F32)
    carry_row = jnp.zeros((N_EXPERTS, 1), F32)
    offs = jnp.zeros((N_EXPERTS, LANES), F32)
    rows = jnp.zeros((N_EXPERTS, LANES), F32)
    for c in range(n_tiles):
        sl = slice(c * LANES, (c + 1) * LANES)
        eq_f = jnp.where(eq[:, sl], 1.0, 0.0)
        eq_incl = _dot(eq_f.astype(BF16), tri) + carry_eq
        sel_f = jnp.where(gt[:, sl], 1.0, jnp.where(eq_incl <= n_tie, eq_f, 0.0))
        sel_b = sel_f.astype(BF16)
        sel_incl = _dot(sel_b, tri) + carry_sel
        posm_ref[:, sl] = jnp.where(sel_f > 0.5, sel_incl - sel_f, -1.0).astype(I32)
        cnt = jnp.broadcast_to(jnp.sum(sel_f, axis=0, keepdims=True), (N_EXPERTS, LANES))
        tok_start = _dot(cnt.astype(BF16), tri) - cnt + carry_row
        qdst_ref[:, sl] = (tok_start + _dot(below, sel_b)).astype(I32)
        first_row = jnp.broadcast_to(tok_start[0:1, :], (LANES, LANES))
        end_row = jnp.broadcast_to(tok_start[0:1, :] + cnt[0:1, :], (LANES, LANES))
        span_ref[sl, :] = jnp.where(sub_sq == 0, first_row, jnp.where(sub_sq == 1, end_row, 0.0)).T
        offs = jnp.where(lane == c, carry_sel, offs)
        rows = jnp.where(lane == c, carry_row, rows)
        carry_eq = carry_eq + jnp.sum(eq_f, axis=1, keepdims=True)
        carry_sel = carry_sel + jnp.sum(sel_f, axis=1, keepdims=True)
        carry_row = carry_row + jnp.sum(cnt, axis=1, keepdims=True)
    offs_ref[...] = jnp.where(lane >= n_tiles, carry_sel, offs).astype(I32)
    rows_ref[...] = jnp.where(lane >= n_tiles, carry_row, rows).astype(I32)


def _route(aff_t, cap):
    n_tok = aff_t.shape[1]
    assert n_tok // LANES < LANES
    tri = jnp.asarray(np.triu(np.ones((LANES, LANES))), dtype=BF16)
    return pl.pallas_call(
        functools.partial(_route_kernel, cap=cap, n_tok=n_tok),
        out_shape=[jax.ShapeDtypeStruct((N_EXPERTS, n_tok), I32),
                   jax.ShapeDtypeStruct((N_EXPERTS, n_tok), I32),
                   jax.ShapeDtypeStruct((n_tok, LANES), F32),
                   jax.ShapeDtypeStruct((N_EXPERTS, LANES), I32),
                   jax.ShapeDtypeStruct((N_EXPERTS, LANES), I32)],
        compiler_params=pltpu.CompilerParams(vmem_limit_bytes=VMEM_LIMIT),
        name="route",
    )(aff_t, tri)


def _slots_kernel(clo_ref, chi_ref, posm_ref, qdst_ref, idx_ref, qslot_ref):
    e = pl.program_id(0)
    s = pl.program_id(1)
    slot = lax.broadcasted_iota(I32, (LANES, LANES), 0) + s * LANES
    lane = lax.broadcasted_iota(I32, (1, LANES), 1)

    def body(c, carry):
        tok_acc, row_acc = carry
        hit = posm_ref[e, c] == slot
        tok = (c * LANES + lane).astype(F32)
        row = qdst_ref[e, c].astype(F32)
        return tok_acc + jnp.where(hit, tok, 0.0), row_acc + jnp.where(hit, row, 0.0)

    zero = jnp.zeros((LANES, LANES), F32)
    tok_acc, row_acc = lax.fori_loop(clo_ref[e, s], chi_ref[e, s], body, (zero, zero))
    eye = lax.broadcasted_iota(I32, (LANES, LANES), 0) == lax.broadcasted_iota(I32, (LANES, LANES), 1)

    def as_row(acc):
        col = jnp.sum(acc, axis=1, keepdims=True)
        return jnp.sum(jnp.where(eye, col, 0.0), axis=0, keepdims=True).astype(I32)

    idx_ref[0, pl.ds(s, 1), :] = as_row(tok_acc) * ROW_TILE
    qslot_ref[0, pl.ds(s, 1), :] = as_row(row_acc) * ROW_TILE


def _slot_lists(clo, chi, posm4, qdst4, cap):
    ns = cap // LANES
    grid_spec = pltpu.PrefetchScalarGridSpec(
        num_scalar_prefetch=2,
        grid=(N_EXPERTS, ns),
        in_specs=[_whole_vmem(), _whole_vmem()],
        out_specs=[pl.BlockSpec((1, ns, LANES), lambda e, s, *_: (e, 0, 0)),
                   pl.BlockSpec((1, ns, LANES), lambda e, s, *_: (e, 0, 0))])
    idx, qslot = pl.pallas_call(
        _slots_kernel,
        grid_spec=grid_spec,
        out_shape=[jax.ShapeDtypeStruct((N_EXPERTS, ns, LANES), I32),
                   jax.ShapeDtypeStruct((N_EXPERTS, ns, LANES), I32)],
        compiler_params=_cparams(("arbitrary", "arbitrary")),
        name="slot_lists",
    )(clo, chi, posm4, qdst4)
    return idx.reshape(N_EXPERTS, cap), qslot.reshape(N_EXPERTS, cap)


def _moe_kernel(idxc_ref, idxs_ref, qc_ref, qs_ref,
                hc_ref, hs_ref, wr_ref, wg_ref, wu_ref, wd_ref, zc_ref, zs_ref,
                xbuf, ybuf, xb_ref, gate_ref, acc_ref, gsem, ssem, *, capc, caps):
    e = pl.program_id(0)
    j = pl.program_id(1)
    n_e = pl.num_programs(0)
    n_j = FF_STEPS
    slot = e % 2
    other = 1 - slot
    rows = capc + caps
    gc, gs = _per_step(capc), _per_step(caps)
    groups = ((hc_ref, idxc_ref, zc_ref, qc_ref, gc, 0), (hs_ref, idxs_ref, zs_ref, qs_ref, gs, gc * n_j))

    def tile(ref, first_sublane):
        return ref.at[pl.ds(pl.multiple_of(first_sublane, ROW_TILE), ROW_TILE), :]

    def gather(ex, sl, step, i, group):
        h_ref, idx_ref, _, _, per_step, base = group
        p = step * per_step + i
        src = tile(h_ref, idx_ref[ex * (per_step * n_j) + p])
        pltpu.make_async_copy(src, tile(xbuf.at[sl], (base + p) * ROW_TILE), gsem.at[sl]).start()

    def scatter(table_row, sl, step, i, group):
        _, _, z_ref, q_ref, per_step, base = group
        p = step * per_step + i
        dst = tile(z_ref, q_ref[table_row * (per_step * n_j) + p])
        pltpu.make_async_copy(tile(ybuf.at[sl], (base + p) * ROW_TILE), dst, ssem.at[sl]).start()

    def all_steps(fn):
        for group in groups:
            def body(p, carry, group=group):
                fn(p, group)
                return carry
            lax.fori_loop(0, group[4] * n_j, body, 0, unroll=8)

    def wait_all(buf, sem, sl):
        pltpu.make_async_copy(buf.at[sl], buf.at[sl], sem.at[sl]).wait()

    @pl.when((e == 0) & (j == 0))
    def _():
        ybuf[...] = jnp.zeros_like(ybuf)
        all_steps(lambda p, group: gather(0, 0, 0, p, group))

    @pl.when(j == 0)
    def _():
        wait_all(xbuf, gsem, slot)
        for base, n, dst in ((0, capc, 0), (gc * n_j, caps, capc)):
            for kc in range(ROW_TILE):
                tiles = xbuf[slot, pl.ds(base * ROW_TILE + kc, n, stride=ROW_TILE), :]
                xb_ref[dst:dst + n, kc * LANES:(kc + 1) * LANES] = tiles.astype(BF16)
        acc_ref[...] = jnp.zeros_like(acc_ref)
        logits = _dot(xb_ref[...], wr_ref[...])
        lane = lax.broadcasted_iota(I32, (1, LANES), 1)
        is_expert = lane < N_EXPERTS
        ex = jnp.exp(logits - jnp.max(jnp.where(is_expert, logits, -jnp.inf), axis=-1, keepdims=True))
        mine = jnp.sum(jnp.where(lane == e, ex, 0.0), axis=-1, keepdims=True)
        gate = mine / jnp.sum(jnp.where(is_expert, ex, 0.0), axis=-1, keepdims=True)
        gate_ref[...] = jnp.broadcast_to(gate, gate_ref.shape)

    nxt = jnp.minimum(e + 1, n_e - 1)
    for group in groups:
        for i in range(group[4]):
            gather(nxt, other, j, i, group)
            scatter(e, other, j, i, group)

    x = xb_ref[...]
    g = _dot(x, wg_ref[0].astype(BF16))
    u = _dot(x, wu_ref[0].astype(BF16))
    hid = (g * jax.nn.sigmoid(g) * u).astype(BF16)
    acc_ref[...] += _dot(hid, wd_ref[0].astype(BF16))

    @pl.when(j == n_j - 1)
    def _():
        @pl.when(e >= 1)
        def _():
            wait_all(ybuf, ssem, slot)

        gate = gate_ref[...]
        for base, n, src in ((0, capc, 0), (gc * n_j, caps, capc)):
            for kc in range(ROW_TILE):
                y = acc_ref[src:src + n, kc * LANES:(kc + 1) * LANES] * gate[src:src + n]
                ybuf[slot, pl.ds(base * ROW_TILE + kc, n, stride=ROW_TILE), :] = y

        @pl.when(e == n_e - 1)
        def _():
            all_steps(lambda p, group: scatter(e + 1, slot, 0, p, group))
            wait_all(ybuf, ssem, other)
            wait_all(ybuf, ssem, slot)
            wait_all(xbuf, gsem, other)


def _per_step(cap):
    return -(-cap // FF_STEPS)


def _copy_tables(idx, qslot, n_rows):
    n_e, cap = idx.shape
    padded = _per_step(cap) * FF_STEPS
    n_pad = padded - cap
    idx_p = jnp.concatenate([idx, jnp.zeros((n_e, n_pad), I32)], axis=1)
    spare = n_rows + jnp.arange(padded + n_e * n_pad, dtype=I32)
    lead = spare[:padded][None, :]
    pad_rows = spare[padded:].reshape(n_e, n_pad)
    q_p = jnp.concatenate([lead, jnp.concatenate([qslot // ROW_TILE, pad_rows], axis=1)], axis=0) * ROW_TILE
    return idx_p.reshape(-1), q_p.reshape(-1), n_rows + padded + n_e * n_pad


def _moe(idxc, idxs, qc, qs, hc, hs, wrx, w_gate, w_up, w_down):
    capc, caps = idxc.shape[1], idxs.shape[1]
    rows = capc + caps
    tf = FF_TILE
    idxc, qc, zc_rows = _copy_tables(idxc, qc, N_EXPERTS * capc)
    idxs, qs, zs_rows = _copy_tables(idxs, qs, N_EXPERTS * caps)
    buf_rows = (_per_step(capc) + _per_step(caps)) * FF_STEPS
    any_spec = pl.BlockSpec(memory_space=pl.ANY)
    grid_spec = pltpu.PrefetchScalarGridSpec(
        num_scalar_prefetch=4,
        grid=(N_EXPERTS, FF_STEPS),
        in_specs=[any_spec, any_spec,
                  pl.BlockSpec((D_MODEL, LANES), lambda e, j, *_: (0, 0)),
                  pl.BlockSpec((1, D_MODEL, tf), lambda e, j, *_: (e, 0, j)),
                  pl.BlockSpec((1, D_MODEL, tf), lambda e, j, *_: (e, 0, j)),
                  pl.BlockSpec((1, tf, D_MODEL), lambda e, j, *_: (e, j, 0))],
        out_specs=[any_spec, any_spec],
        scratch_shapes=[pltpu.VMEM((2, buf_rows * ROW_TILE, LANES), F32),
                        pltpu.VMEM((2, buf_rows * ROW_TILE, LANES), F32),
                        pltpu.VMEM((rows, D_MODEL), BF16),
                        pltpu.VMEM((rows, LANES), F32),
                        pltpu.VMEM((rows, D_MODEL), F32),
                        pltpu.SemaphoreType.DMA((2,)),
                        pltpu.SemaphoreType.DMA((2,))])
    return pl.pallas_call(
        functools.partial(_moe_kernel, capc=capc, caps=caps),
        grid_spec=grid_spec,
        out_shape=[jax.ShapeDtypeStruct((zc_rows * ROW_TILE, LANES), F32),
                   jax.ShapeDtypeStruct((zs_rows * ROW_TILE, LANES), F32)],
        compiler_params=_cparams(("arbitrary", "arbitrary")),
        name="expert_ffn",
    )(idxc, idxs, qc, qs, hc, hs, wrx, w_gate, w_up, w_down)


PAIR_FIRST, PAIR_LAST, PAIR_VALID = 1, 2, 4


def _combine_kernel(chunk_ref, blk_ref, flag_ref, z_ref, span_ref, x1_ref, mod_ref, g_ref, o_ref, acc_ref):
    i = pl.program_id(0)
    flag = flag_ref[i]

    @pl.when((flag & PAIR_FIRST) != 0)
    def _():
        acc_ref[...] = jnp.zeros_like(acc_ref)

    @pl.when((flag & PAIR_VALID) != 0)
    def _():
        row = (chunk_ref[i] * ROW_CHUNK + lax.broadcasted_iota(I32, (TOK_BLOCK, ROW_CHUNK), 1)).astype(F32)
        first_row = span_ref[:, 0:1]
        end_row = span_ref[:, 1:2]
        onehot = jnp.where((row >= first_row) & (row < end_row), 1.0, 0.0).astype(BF16)
        y = jnp.concatenate([z_ref[pl.ds(kc, ROW_CHUNK, stride=ROW_TILE), :] for kc in range(ROW_TILE)], axis=1)
        y_hi = y.astype(BF16)
        y_lo = (y - y_hi.astype(F32)).astype(BF16)
        acc_ref[...] += _dot(onehot, y_hi) + _dot(onehot, y_lo)

    @pl.when((flag & PAIR_LAST) != 0)
    def _():
        gt2 = mod_ref[0][:, 5 * D_MODEL:6 * D_MODEL]
        o_ref[...] = x1_ref[...] + gt2 * _rms(acc_ref[...], g_ref[...], EPS)


def _combine(pairs, z, span, x1, mod3, mod_row, g_post_ffn):
    chunk, blk, flag = pairs
    t = x1.shape[0]
    tb = TOK_BLOCK
    grid_spec = pltpu.PrefetchScalarGridSpec(
        num_scalar_prefetch=3,
        grid=(chunk.shape[0],),
        in_specs=[pl.BlockSpec((ROW_CHUNK * ROW_TILE, LANES), lambda i, c, b, f: (c[i], 0)),
                  pl.BlockSpec((tb, LANES), lambda i, c, b, f: (b[i], 0)),
                  pl.BlockSpec((tb, D_MODEL), lambda i, c, b, f: (b[i], 0)),
                  pl.BlockSpec((1, 1, N_MOD * D_MODEL), lambda i, c, b, f: (mod_row(b[i]), 0, 0)),
                  pl.BlockSpec((1, D_MODEL), lambda i, c, b, f: (0, 0))],
        out_specs=pl.BlockSpec((tb, D_MODEL), lambda i, c, b, f: (b[i], 0)),
        scratch_shapes=[pltpu.VMEM((tb, D_MODEL), F32)])
    return pl.pallas_call(
        _combine_kernel,
        grid_spec=grid_spec,
        out_shape=jax.ShapeDtypeStruct((t, D_MODEL), F32),
        compiler_params=_cparams(("arbitrary",)),
        name="combine",
    )(chunk, blk, flag, z, span, x1, mod3, g_post_ffn)


def _rope_tables(seq):
    half = HEAD_DIM // 4
    freqs = ROPE_THETA ** (-np.arange(half, dtype=np.float64) / half)
    s = np.arange(seq)
    row = (s // GRID_W)[:, None] * freqs[None, :]
    col = (s % GRID_W)[:, None] * freqs[None, :]
    ang = np.concatenate([row, row, col, col], axis=1)
    ang = np.tile(ang, (1, QK_W // HEAD_DIM))
    lane = np.arange(QK_W)
    sign = np.where((lane % 32) < 16, -1.0, 1.0)[None, :]
    return (jnp.asarray(np.cos(ang), dtype=F32), jnp.asarray(np.sin(ang) * sign, dtype=F32))


def _slot_tile_ranges(offs, cap, n_tok):
    nt = n_tok // LANES
    lo, hi = offs[:, :nt], offs[:, 1:nt + 1]
    base = jnp.arange(cap // LANES, dtype=I32) * LANES
    clo = jnp.sum(hi[:, None, :] <= base[None, :, None], axis=-1).astype(I32)
    chi = jnp.sum(lo[:, None, :] < (base + LANES)[None, :, None], axis=-1).astype(I32)
    return clo, chi


def _combine_pairs(rows, n_tok):
    step = TOK_BLOCK // LANES
    nb = n_tok // TOK_BLOCK
    nc = 2 * n_tok // ROW_CHUNK
    starts = rows[0, 0:nb * step + 1:step]
    lo, hi = starts[:-1], starts[1:]
    c_lo = jnp.minimum(lo // ROW_CHUNK, nc - 1)
    c_hi = jnp.maximum((hi + ROW_CHUNK - 1) // ROW_CHUNK, c_lo + 1)
    n = c_hi - c_lo
    ends = jnp.cumsum(n)
    begins = ends - n
    i = jnp.arange(nb + nc, dtype=I32)
    valid = i < ends[-1]
    blk = jnp.minimum(jnp.sum(ends[None, :] <= i[:, None], axis=1), nb - 1).astype(I32)
    off = jnp.minimum(i - begins[blk], n[blk] - 1)
    chunk = (c_lo[blk] + off).astype(I32)
    first = valid & (i == begins[blk])
    last = valid & (i == ends[blk] - 1)
    flag = (first * PAIR_FIRST + last * PAIR_LAST + valid * PAIR_VALID).astype(I32)
    return chunk, blk, flag


def kernel(x_prompt, x_sample, c, cache_k, cache_v, c_ctx, w_mod, b_mod, g_pre_mix, g_post_mix, g_pre_ffn, g_post_ffn, w_in, lam_q1, lam_k1, lam_q2, lam_k2, g_subln, w_proj_attn, w_proj_fourier, w_out, w_router, w_gate, w_up, w_down):
    assert w_mod.shape[0] == 1
    lam_init = 0.8 - 0.6 * math.exp(-0.3 * 0)
    bp, sp, _ = x_prompt.shape
    bs, ss, _ = x_sample.shape

    cond8 = jnp.concatenate([c_ctx[None, :], c, jnp.zeros((8 - 1 - bs, D_MODEL), F32)], axis=0)
    mod3 = _modulation(cond8, w_mod[0], b_mod).reshape(8, 1, N_MOD * D_MODEL)

    w_in_b = w_in[0].astype(BF16)
    wpa = w_proj_attn[0].astype(BF16)
    wpf = w_proj_fourier[0].astype(BF16)
    wout = w_out[0].astype(BF16)
    wr = w_router[0].astype(BF16)
    wrt = wr.T
    wrx = jnp.concatenate([wr, jnp.zeros((D_MODEL, LANES - N_EXPERTS), BF16)], axis=1)
    lam_p = jnp.concatenate([lam_q1, lam_k1, lam_q2, lam_k2], axis=0)

    groups = []
    for x, seq, positional, ctx in ((x_prompt, sp, False, None),
                                    (x_sample, ss, True, (cache_k, cache_v))):
        nb = x.shape[0]
        t = nb * seq
        x2d = x.reshape(t, D_MODEL)
        blocks_per_seq = seq // ROW_BLOCK
        if positional:
            mod_row = lambda i, n=blocks_per_seq: 1 + i // n
        else:
            mod_row = lambda i: 0
        pre = _pre_mixer(x2d, mod3, mod_row, g_pre_mix, w_in_b,
                         _rope_tables(seq) if positional else None, seq, write_cache=not positional)
        q, k, v, f, ga, gf = pre[:6]
        o = _attention(lam_p, g_subln, q, k, v, ctx, seq, lam_init)
        fm = _fourier(f, seq)
        x1, h2t, aff_t = _post_mixer(o, fm, ga, gf, x2d, mod3, mod_row, g_post_mix, g_pre_ffn,
                                     wpa, wpf, wout, wrt)
        cap = 2 * t // N_EXPERTS
        posm, qdst, span, offs, rows = _route(aff_t, cap)
        clo, chi = _slot_tile_ranges(offs, cap, t)
        tiled = (N_EXPERTS, t // LANES, 1, LANES)
        idx, qslot = _slot_lists(clo, chi, posm.reshape(tiled), qdst.reshape(tiled), cap)
        groups.append(dict(x1=x1, h2t=h2t, idx=idx, qslot=qslot, span=span, pairs=_combine_pairs(rows, t),
                           mod_row=mod_row, cache=pre[6:], shape=x.shape))

    gc, gs_ = groups
    zc, zs = _moe(gc["idx"], gs_["idx"], gc["qslot"], gs_["qslot"], gc["h2t"], gs_["h2t"], wrx,
                  w_gate[0], w_up[0], w_down[0])
    outs = []
    for g, z in ((gc, zc), (gs_, zs)):
        out = _combine(g["pairs"], z, g["span"], g["x1"], mod3, g["mod_row"], g_post_ffn)
        outs.append(out.reshape(g["shape"]))
    new_k, new_v = gc["cache"]
    return (outs[0], outs[1], new_k, new_v)
```

```python
import functools
import math

import numpy as np
import jax
import jax.numpy as jnp
from jax import lax
from jax.experimental import pallas as pl
from jax.experimental.pallas import tpu as pltpu

F32 = jnp.float32
BF16 = jnp.bfloat16
I32 = jnp.int32

D_MODEL = 1024
N_HEADS = 6
HEAD_DIM = 64
V_DIM = 128
QK_W = 768
FOUR_W = 256
FOUR_G = 64
IN_W = 4608
N_EXPERTS = 16
D_FF = 2816
N_MOD = 6
GRID_W = 64
ROPE_THETA = 10000.0
EPS = 1e-6
SUBLN_EPS = 1e-5

LANES = 128
ROW_BLOCK = 256
TOK_BLOCK = 256
ROW_CHUNK = 256
FF_TILE = 256
FF_STEPS = D_FF // FF_TILE
ROW_TILE = D_MODEL // LANES
VMEM_LIMIT = 56 * 1024 * 1024


def _cparams(sem):
    return pltpu.CompilerParams(dimension_semantics=sem, vmem_limit_bytes=VMEM_LIMIT)


def _dot(a, b):
    return jnp.dot(a, b, preferred_element_type=F32)


def _rms(x, g, eps):
    return x * lax.rsqrt(jnp.mean(x * x, axis=-1, keepdims=True) + eps) * g


def _whole_vmem():
    return pl.BlockSpec(memory_space=pltpu.MemorySpace.VMEM)


def _mod_kernel(c_ref, w_ref, b_ref, o_ref):
    c = c_ref[...]
    s = c * jax.nn.sigmoid(c)
    o_ref[...] = _dot(s.astype(BF16), w_ref[...].astype(BF16)) + b_ref[...]


def _modulation(cond8, w_mod, b_mod):
    tn = 512
    n = N_MOD * D_MODEL
    return pl.pallas_call(
        _mod_kernel,
        grid=(n // tn,),
        in_specs=[pl.BlockSpec((8, D_MODEL), lambda j: (0, 0)),
                  pl.BlockSpec((D_MODEL, tn), lambda j: (0, j)),
                  pl.BlockSpec((1, tn), lambda j: (0, j))],
        out_specs=pl.BlockSpec((8, tn), lambda j: (0, j)),
        out_shape=jax.ShapeDtypeStruct((8, n), F32),
        compiler_params=_cparams(("arbitrary",)),
        name="modulation",
    )(cond8, w_mod, b_mod)


def _rope(z, cos, sin_signed, first_half):
    fwd = pltpu.roll(z, QK_W - 16, axis=1)
    bwd = pltpu.roll(z, 16, axis=1)
    return z * cos + jnp.where(first_half, fwd, bwd) * sin_signed


def _pre_kernel(*refs, positional, write_cache):
    it = iter(refs)
    x_ref, mod_ref, g_ref, w_ref = next(it), next(it), next(it), next(it)
    if positional:
        cos_ref, sin_ref = next(it), next(it)
    q_ref, k_ref, v_ref, f_ref, ga_ref, gf_ref = (next(it) for _ in range(6))
    if write_cache:
        kc_ref, vc_ref = next(it), next(it)

    m = mod_ref[0]
    sh1 = m[:, 0:D_MODEL]
    sc1 = m[:, D_MODEL:2 * D_MODEL]
    h = _rms(x_ref[...], g_ref[...], EPS) * (1.0 + sc1) + sh1
    hb = h.astype(BF16)

    def proj(lo, hi):
        return _dot(hb, w_ref[:, lo:hi])

    zq = proj(0, QK_W)
    zk = proj(QK_W, 2 * QK_W)
    zv = proj(2 * QK_W, 3 * QK_W)
    if positional:
        lane = lax.broadcasted_iota(I32, (1, QK_W), 1)
        first_half = (lane % 32) < 16
        cos = cos_ref[...]
        sin_signed = sin_ref[...]
        zq = _rope(zq, cos, sin_signed, first_half)
        zk = _rope(zk, cos, sin_signed, first_half)
    q_ref[...] = zq.astype(BF16)
    k_ref[...] = zk.astype(BF16)
    v_ref[...] = zv.astype(BF16)
    if write_cache:
        for hd in range(N_HEADS):
            kc_ref[0, 0, hd] = zk[:, hd * V_DIM:(hd + 1) * V_DIM]
            vc_ref[0, 0, hd] = zv[:, hd * V_DIM:(hd + 1) * V_DIM]
    f0 = 3 * QK_W
    f_ref[...] = proj(f0, f0 + FOUR_W)
    ga_ref[...] = jax.nn.sigmoid(proj(f0 + FOUR_W, f0 + FOUR_W + D_MODEL))
    gf_ref[...] = jax.nn.sigmoid(proj(f0 + FOUR_W + D_MODEL, IN_W))


def _pre_mixer(x2d, mod3, mod_row, g_pre, w_in_b, rope_tabs, seq, write_cache):
    t = x2d.shape[0]
    tm = ROW_BLOCK
    positional = rope_tabs is not None
    blocks_per_seq = seq // tm
    row = lambda i: (i, 0)
    in_specs = [pl.BlockSpec((tm, D_MODEL), row),
                pl.BlockSpec((1, 1, N_MOD * D_MODEL), lambda i: (mod_row(i), 0, 0)),
                pl.BlockSpec((1, D_MODEL), lambda i: (0, 0)),
                _whole_vmem()]
    args = [x2d, mod3, g_pre, w_in_b]
    if positional:
        in_specs += [pl.BlockSpec((tm, QK_W), lambda i: (i % blocks_per_seq, 0))] * 2
        args += list(rope_tabs)
    out_shape = [jax.ShapeDtypeStruct((t, QK_W), BF16)] * 3 + [
        jax.ShapeDtypeStruct((t, FOUR_W), F32),
        jax.ShapeDtypeStruct((t, D_MODEL), F32),
        jax.ShapeDtypeStruct((t, D_MODEL), F32)]
    out_specs = [pl.BlockSpec((tm, QK_W), row)] * 3 + [
        pl.BlockSpec((tm, FOUR_W), row),
        pl.BlockSpec((tm, D_MODEL), row),
        pl.BlockSpec((tm, D_MODEL), row)]
    if write_cache:
        assert seq == tm
        nb = t // seq
        cshape = jax.ShapeDtypeStruct((nb, 1, N_HEADS, seq, V_DIM), F32)
        cspec = pl.BlockSpec((1, 1, N_HEADS, seq, V_DIM), lambda i: (i, 0, 0, 0, 0))
        out_shape += [cshape, cshape]
        out_specs += [cspec, cspec]
    return pl.pallas_call(
        functools.partial(_pre_kernel, positional=positional, write_cache=write_cache),
        grid=(t // tm,),
        in_specs=in_specs,
        out_specs=out_specs,
        out_shape=out_shape,
        compiler_params=_cparams(("arbitrary",)),
        name="pre_mixer",
    )(*args)


def _attn_kernel(*refs, lam_init, has_ctx):
    it = iter(refs)
    lam_ref, gs_ref, q_ref, k_ref, v_ref = (next(it) for _ in range(5))
    if has_ctx:
        ck_ref, cv_ref = next(it), next(it)
    o_ref = next(it)

    lp = lam_ref[...]
    s1 = jnp.sum(lp[0:1] * lp[1:2], axis=-1, keepdims=True)
    s2 = jnp.sum(lp[2:3] * lp[3:4], axis=-1, keepdims=True)
    lam = jnp.exp(s1) - jnp.exp(s2) + lam_init
    lane = lax.broadcasted_iota(I32, (1, V_DIM), 1)
    comp1 = lane < HEAD_DIM
    scale = jnp.asarray(HEAD_DIM ** -0.5, BF16)
    nt = (((1,), (1,)), ((), ()))

    def attend(qc, k, v_ones):
        s = lax.dot_general(qc, k, nt, preferred_element_type=F32)
        ex = jnp.exp(s - jnp.max(s, axis=-1, keepdims=True)).astype(BF16)
        ov = _dot(ex, v_ones)
        return ov[:, 0:V_DIM] / ov[:, V_DIM:2 * V_DIM]

    for hd in range(N_HEADS):
        sl = slice(hd * V_DIM, (hd + 1) * V_DIM)
        q = q_ref[:, sl] * scale
        k = k_ref[:, sl]
        v = v_ref[:, sl]
        if has_ctx:
            k = jnp.concatenate([ck_ref[0, 0, hd].astype(BF16), k], axis=0)
            v = jnp.concatenate([cv_ref[0, 0, hd].astype(BF16), v], axis=0)
        v_ones = jnp.concatenate([v, jnp.ones_like(v)], axis=1)
        zero = jnp.zeros_like(q)
        o = attend(jnp.where(comp1, q, zero), k, v_ones) - lam * attend(jnp.where(comp1, zero, q), k, v_ones)
        o = _rms(o, gs_ref[...], SUBLN_EPS) * (1.0 - lam_init)
        o_ref[:, sl] = o.astype(BF16)


def _attention(lam_p, g_subln, q, k, v, ctx, seq, lam_init):
    t = q.shape[0]
    tq = ROW_BLOCK
    qb = seq // tq
    has_ctx = ctx is not None
    in_specs = [pl.BlockSpec((4, HEAD_DIM), lambda b, i: (0, 0)),
                pl.BlockSpec((1, V_DIM), lambda b, i: (0, 0)),
                pl.BlockSpec((tq, QK_W), lambda b, i: (b * qb + i, 0)),
                pl.BlockSpec((seq, QK_W), lambda b, i: (b, 0)),
                pl.BlockSpec((seq, QK_W), lambda b, i: (b, 0))]
    args = [lam_p, g_subln, q, k, v]
    if has_ctx:
        past = ctx[0].shape[3]
        cspec = pl.BlockSpec((1, 1, N_HEADS, past, V_DIM), lambda b, i: (b, 0, 0, 0, 0))
        in_specs += [cspec, cspec]
        args += list(ctx)
    return pl.pallas_call(
        functools.partial(_attn_kernel, lam_init=lam_init, has_ctx=has_ctx),
        grid=(t // seq, qb),
        in_specs=in_specs,
        out_specs=pl.BlockSpec((tq, QK_W), lambda b, i: (b * qb + i, 0)),
        out_shape=jax.ShapeDtypeStruct((t, QK_W), BF16),
        compiler_params=_cparams(("arbitrary", "arbitrary")),
        name="diff_attention",
    )(*args)


def _fourier_kernel(f_ref, bc_ref, bs_ref, cs_ref, ss_ref, o_ref):
    fb = f_ref[...].astype(BF16)
    u = _dot(fb, bc_ref[...].astype(BF16)).astype(BF16)
    w = _dot(fb, bs_ref[...].astype(BF16)).astype(BF16)
    o_ref[...] = (_dot(cs_ref[...].astype(BF16), u) - _dot(ss_ref[...].astype(BF16), w)).astype(BF16)


def _dft_consts(seq):
    c = np.arange(FOUR_G)
    ang_c = 2.0 * np.pi * ((c[:, None] * c[None, :]) % FOUR_G) / FOUR_G
    eye = np.eye(FOUR_W // FOUR_G)
    bc = np.kron(eye, np.cos(ang_c)) / math.sqrt(FOUR_G)
    bs = np.kron(eye, np.sin(ang_c)) / math.sqrt(FOUR_G)
    s = np.arange(seq)
    ang_s = 2.0 * np.pi * ((s[:, None] * s[None, :]) % seq) / seq
    cs = np.cos(ang_s) / math.sqrt(seq)
    ss = np.sin(ang_s) / math.sqrt(seq)
    return tuple(jnp.asarray(a, dtype=F32) for a in (bc, bs, cs, ss))


def _fourier(f, seq):
    t = f.shape[0]
    bc, bs, cs, ss = _dft_consts(seq)
    const = lambda b: (0, 0)
    return pl.pallas_call(
        _fourier_kernel,
        grid=(t // seq,),
        in_specs=[pl.BlockSpec((seq, FOUR_W), lambda b: (b, 0)),
                  pl.BlockSpec((FOUR_W, FOUR_W), const),
                  pl.BlockSpec((FOUR_W, FOUR_W), const),
                  pl.BlockSpec((seq, seq), const),
                  pl.BlockSpec((seq, seq), const)],
        out_specs=pl.BlockSpec((seq, FOUR_W), lambda b: (b, 0)),
        out_shape=jax.ShapeDtypeStruct((t, FOUR_W), BF16),
        compiler_params=_cparams(("arbitrary",)),
        name="fourier_mix",
    )(f, bc, bs, cs, ss)


def _post_kernel(o_ref, fm_ref, ga_ref, gf_ref, x_ref, mod_ref, gpost_ref, gffn_ref,
                 wpa_ref, wpf_ref, wout_ref, wrt_ref,
                 x1_ref, h2t_ref, afft_ref):
    a = _dot(o_ref[...], wpa_ref[...])
    b = _dot(fm_ref[...], wpf_ref[...])
    merged = ga_ref[...] * a + gf_ref[...] * b
    y = _dot(merged.astype(BF16), wout_ref[...])
    m = mod_ref[0]
    gt1 = m[:, 2 * D_MODEL:3 * D_MODEL]
    sh2 = m[:, 3 * D_MODEL:4 * D_MODEL]
    sc2 = m[:, 4 * D_MODEL:5 * D_MODEL]
    x1 = x_ref[...] + gt1 * _rms(y, gpost_ref[...], EPS)
    x1_ref[...] = x1
    h2 = _rms(x1, gffn_ref[...], EPS) * (1.0 + sc2) + sh2
    h2b = h2.astype(BF16)

    lt = lax.dot_general(wrt_ref[...], h2b, (((1,), (1,)), ((), ())), preferred_element_type=F32)
    et = jnp.exp(lt - jnp.max(lt, axis=0, keepdims=True))
    afft_ref[...] = et / jnp.sum(et, axis=0, keepdims=True)

    tm = h2.shape[0]
    for kc in range(ROW_TILE):
        h2t_ref[pl.ds(kc, tm, stride=ROW_TILE), :] = h2[:, kc * LANES:(kc + 1) * LANES]


def _post_mixer(o, fm, ga, gf, x2d, mod3, mod_row, g_post, g_ffn, wpa, wpf, wout, wrt):
    t = x2d.shape[0]
    tm = ROW_BLOCK
    row = lambda i: (i, 0)
    const = lambda i: (0, 0)
    return pl.pallas_call(
        _post_kernel,
        grid=(t // tm,),
        in_specs=[pl.BlockSpec((tm, QK_W), row),
                  pl.BlockSpec((tm, FOUR_W), row),
                  pl.BlockSpec((tm, D_MODEL), row),
                  pl.BlockSpec((tm, D_MODEL), row),
                  pl.BlockSpec((tm, D_MODEL), row),
                  pl.BlockSpec((1, 1, N_MOD * D_MODEL), lambda i: (mod_row(i), 0, 0)),
                  pl.BlockSpec((1, D_MODEL), const),
                  pl.BlockSpec((1, D_MODEL), const),
                  pl.BlockSpec((QK_W, D_MODEL), const),
                  pl.BlockSpec((FOUR_W, D_MODEL), const),
                  pl.BlockSpec((D_MODEL, D_MODEL), const),
                  pl.BlockSpec((N_EXPERTS, D_MODEL), const)],
        out_specs=[pl.BlockSpec((tm, D_MODEL), row),
                   pl.BlockSpec((tm * ROW_TILE, LANES), row),
                   pl.BlockSpec((N_EXPERTS, tm), lambda i: (0, i))],
        out_shape=[jax.ShapeDtypeStruct((t, D_MODEL), F32),
                   jax.ShapeDtypeStruct((t * ROW_TILE, LANES), F32),
                   jax.ShapeDtypeStruct((N_EXPERTS, t), F32)],
        compiler_params=_cparams(("arbitrary",)),
        name="post_mixer",
    )(o, fm, ga, gf, x2d, mod3, g_post, g_ffn, wpa, wpf, wout, wrt)


def _route_kernel(aff_ref, tri_ref, posm_ref, qdst_ref, span_ref, offs_ref, rows_ref, *, cap, n_tok):
    aff = aff_ref[...]
    capf = float(cap)

    def count_ge(v):
        return jnp.sum(jnp.where(aff >= v, 1.0, 0.0), axis=1, keepdims=True)

    def search(i, thr):
        cand = thr | jnp.left_shift(jnp.int32(1), 30 - i)
        return jnp.where(count_ge(pltpu.bitcast(cand, F32)) >= capf, cand, thr)

    thr = lax.fori_loop(0, 31, search, jnp.zeros((N_EXPERTS, 1), I32))
    lo = pltpu.bitcast(thr, F32)
    hi = pltpu.bitcast(thr + 1, F32)

    def refine(i, c):
        lo, hi = c
        mid = lo + (hi - lo) * 0.5
        ok = count_ge(mid) >= capf
        return jnp.where(ok, mid, lo), jnp.where(ok, hi, mid)

    lo, hi = lax.fori_loop(0, 24, refine, (lo, hi))
    gt = aff >= hi
    eq = (aff >= lo) & (aff < hi)
    n_gt = jnp.sum(jnp.where(gt, 1.0, 0.0), axis=1, keepdims=True)
    n_tie = capf - n_gt

    tri = tri_ref[...]
    lane = lax.broadcasted_iota(I32, (1, LANES), 1)
    ei = lax.broadcasted_iota(I32, (N_EXPERTS, N_EXPERTS), 0)
    ej = lax.broadcasted_iota(I32, (N_EXPERTS, N_EXPERTS), 1)
    below = jnp.where(ej < ei, 1.0, 0.0).astype(BF16)
    sub_sq = lax.broadcasted_iota(I32, (LANES, LANES), 0)
    n_tiles = n_tok // LANES
    carry_eq = jnp.zeros((N_EXPERTS, 1), F32)
    carry_sel = jnp.zeros((N_EXPERTS, 1), F32)
    carry_row = jnp.zeros((N_EXPERTS, 1), F32)
    offs = jnp.zeros((N_EXPERTS, LANES), F32)
    rows = jnp.zeros((N_EXPERTS, LANES), F32)
    for c in range(n_tiles):
        sl = slice(c * LANES, (c + 1) * LANES)
        eq_f = jnp.where(eq[:, sl], 1.0, 0.0)
        eq_incl = _dot(eq_f.astype(BF16), tri) + carry_eq
        sel_f = jnp.where(gt[:, sl], 1.0, jnp.where(eq_incl <= n_tie, eq_f, 0.0))
        sel_b = sel_f.astype(BF16)
        sel_incl = _dot(sel_b, tri) + carry_sel
        posm_ref[:, sl] = jnp.where(sel_f > 0.5, sel_incl - sel_f, -1.0).astype(I32)
        cnt = jnp.broadcast_to(jnp.sum(sel_f, axis=0, keepdims=True), (N_EXPERTS, LANES))
        tok_start = _dot(cnt.astype(BF16), tri) - cnt + carry_row
        qdst_ref[:, sl] = (tok_start + _dot(below, sel_b)).astype(I32)
        first_row = jnp.broadcast_to(tok_start[0:1, :], (LANES, LANES))
        end_row = jnp.broadcast_to(tok_start[0:1, :] + cnt[0:1, :], (LANES, LANES))
        span_ref[sl, :] = jnp.where(sub_sq == 0, first_row, jnp.where(sub_sq == 1, end_row, 0.0)).T
        offs = jnp.where(lane == c, carry_sel, offs)
        rows = jnp.where(lane == c, carry_row, rows)
        carry_eq = carry_eq + jnp.sum(eq_f, axis=1, keepdims=True)
        carry_sel = carry_sel + jnp.sum(sel_f, axis=1, keepdims=True)
        carry_row = carry_row + jnp.sum(cnt, axis=1, keepdims=True)
    offs_ref[...] = jnp.where(lane >= n_tiles, carry_sel, offs).astype(I32)
    rows_ref[...] = jnp.where(lane >= n_tiles, carry_row, rows).astype(I32)


def _route(aff_t, cap):
    n_tok = aff_t.shape[1]
    assert n_tok // LANES < LANES
    tri = jnp.asarray(np.triu(np.ones((LANES, LANES))), dtype=BF16)
    return pl.pallas_call(
        functools.partial(_route_kernel, cap=cap, n_tok=n_tok),
        out_shape=[jax.ShapeDtypeStruct((N_EXPERTS, n_tok), I32),
                   jax.ShapeDtypeStruct((N_EXPERTS, n_tok), I32),
                   jax.ShapeDtypeStruct((n_tok, LANES), F32),
                   jax.ShapeDtypeStruct((N_EXPERTS, LANES), I32),
                   jax.ShapeDtypeStruct((N_EXPERTS, LANES), I32)],
        compiler_params=pltpu.CompilerParams(vmem_limit_bytes=VMEM_LIMIT),
        name="route",
    )(aff_t, tri)


def _slots_kernel(clo_ref, chi_ref, posm_ref, qdst_ref, idx_ref, qslot_ref):
    e = pl.program_id(0)
    s = pl.program_id(1)
    slot = lax.broadcasted_iota(I32, (LANES, LANES), 0) + s * LANES
    lane = lax.broadcasted_iota(I32, (1, LANES), 1)

    def body(c, carry):
        tok_acc, row_acc = carry
        hit = posm_ref[e, c] == slot
        tok = (c * LANES + lane).astype(F32)
        row = qdst_ref[e, c].astype(F32)
        return tok_acc + jnp.where(hit, tok, 0.0), row_acc + jnp.where(hit, row, 0.0)

    zero = jnp.zeros((LANES, LANES), F32)
    tok_acc, row_acc = lax.fori_loop(clo_ref[e, s], chi_ref[e, s], body, (zero, zero))
    eye = lax.broadcasted_iota(I32, (LANES, LANES), 0) == lax.broadcasted_iota(I32, (LANES, LANES), 1)

    def as_row(acc):
        col = jnp.sum(acc, axis=1, keepdims=True)
        return jnp.sum(jnp.where(eye, col, 0.0), axis=0, keepdims=True).astype(I32)

    idx_ref[0, pl.ds(s, 1), :] = as_row(tok_acc) * ROW_TILE
    qslot_ref[0, pl.ds(s, 1), :] = as_row(row_acc) * ROW_TILE


def _slot_lists(clo, chi, posm4, qdst4, cap):
    ns = cap // LANES
    grid_spec = pltpu.PrefetchScalarGridSpec(
        num_scalar_prefetch=2,
        grid=(N_EXPERTS, ns),
        in_specs=[_whole_vmem(), _whole_vmem()],
        out_specs=[pl.BlockSpec((1, ns, LANES), lambda e, s, *_: (e, 0, 0)),
                   pl.BlockSpec((1, ns, LANES), lambda e, s, *_: (e, 0, 0))])
    idx, qslot = pl.pallas_call(
        _slots_kernel,
        grid_spec=grid_spec,
        out_shape=[jax.ShapeDtypeStruct((N_EXPERTS, ns, LANES), I32),
                   jax.ShapeDtypeStruct((N_EXPERTS, ns, LANES), I32)],
        compiler_params=_cparams(("arbitrary", "arbitrary")),
        name="slot_lists",
    )(clo, chi, posm4, qdst4)
    return idx.reshape(N_EXPERTS, cap), qslot.reshape(N_EXPERTS, cap)


def _moe_kernel(idxc_ref, idxs_ref, qc_ref, qs_ref,
                hc_ref, hs_ref, wr_ref, wg_ref, wu_ref, wd_ref, zc_ref, zs_ref,
                xbuf, ybuf, xb_ref, gate_ref, acc_ref, gsem, ssem, *, capc, caps):
    e = pl.program_id(0)
    j = pl.program_id(1)
    n_e = pl.num_programs(0)
    n_j = FF_STEPS
    slot = e % 2
    other = 1 - slot
    rows = capc + caps
    gc, gs = _per_step(capc), _per_step(caps)
    groups = ((hc_ref, idxc_ref, zc_ref, qc_ref, gc, 0), (hs_ref, idxs_ref, zs_ref, qs_ref, gs, gc * n_j))

    def tile(ref, first_sublane):
        return ref.at[pl.ds(pl.multiple_of(first_sublane, ROW_TILE), ROW_TILE), :]

    def gather(ex, sl, step, i, group):
        h_ref, idx_ref, _, _, per_step, base = group
        p = step * per_step + i
        src = tile(h_ref, idx_ref[ex * (per_step * n_j) + p])
        pltpu.make_async_copy(src, tile(xbuf.at[sl], (base + p) * ROW_TILE), gsem.at[sl]).start()

    def scatter(table_row, sl, step, i, group):
        _, _, z_ref, q_ref, per_step, base = group
        p = step * per_step + i
        dst = tile(z_ref, q_ref[table_row * (per_step * n_j) + p])
        pltpu.make_async_copy(tile(ybuf.at[sl], (base + p) * ROW_TILE), dst, ssem.at[sl]).start()

    def all_steps(fn):
        for group in groups:
            def body(p, carry, group=group):
                fn(p, group)
                return carry
            lax.fori_loop(0, group[4] * n_j, body, 0, unroll=8)

    def wait_all(buf, sem, sl):
        pltpu.make_async_copy(buf.at[sl], buf.at[sl], sem.at[sl]).wait()

    @pl.when((e == 0) & (j == 0))
    def _():
        ybuf[...] = jnp.zeros_like(ybuf)
        all_steps(lambda p, group: gather(0, 0, 0, p, group))

    @pl.when(j == 0)
    def _():
        wait_all(xbuf, gsem, slot)
        for base, n, dst in ((0, capc, 0), (gc * n_j, caps, capc)):
            for kc in range(ROW_TILE):
                tiles = xbuf[slot, pl.ds(base * ROW_TILE + kc, n, stride=ROW_TILE), :]
                xb_ref[dst:dst + n, kc * LANES:(kc + 1) * LANES] = tiles.astype(BF16)
        acc_ref[...] = jnp.zeros_like(acc_ref)
        logits = _dot(xb_ref[...], wr_ref[...])
        lane = lax.broadcasted_iota(I32, (1, LANES), 1)
        is_expert = lane < N_EXPERTS
        ex = jnp.exp(logits - jnp.max(jnp.where(is_expert, logits, -jnp.inf), axis=-1, keepdims=True))
        mine = jnp.sum(jnp.where(lane == e, ex, 0.0), axis=-1, keepdims=True)
        gate = mine / jnp.sum(jnp.where(is_expert, ex, 0.0), axis=-1, keepdims=True)
        gate_ref[...] = jnp.broadcast_to(gate, gate_ref.shape)

    nxt = jnp.minimum(e + 1, n_e - 1)
    for group in groups:
        for i in range(group[4]):
            gather(nxt, other, j, i, group)
            scatter(e, other, j, i, group)

    x = xb_ref[...]
    g = _dot(x, wg_ref[0].astype(BF16))
    u = _dot(x, wu_ref[0].astype(BF16))
    hid = (g * jax.nn.sigmoid(g) * u).astype(BF16)
    acc_ref[...] += _dot(hid, wd_ref[0].astype(BF16))

    @pl.when(j == n_j - 1)
    def _():
        @pl.when(e >= 1)
        def _():
            wait_all(ybuf, ssem, slot)

        gate = gate_ref[...]
        for base, n, src in ((0, capc, 0), (gc * n_j, caps, capc)):
            for kc in range(ROW_TILE):
                y = acc_ref[src:src + n, kc * LANES:(kc + 1) * LANES] * gate[src:src + n]
                ybuf[slot, pl.ds(base * ROW_TILE + kc, n, stride=ROW_TILE), :] = y

        @pl.when(e == n_e - 1)
        def _():
            all_steps(lambda p, group: scatter(e + 1, slot, 0, p, group))
            wait_all(ybuf, ssem, other)
            wait_all(ybuf, ssem, slot)
            wait_all(xbuf, gsem, other)


def _per_step(cap):
    return -(-cap // FF_STEPS)


def _copy_tables(idx, qslot, n_rows):
    n_e, cap = idx.shape
    padded = _per_step(cap) * FF_STEPS
    n_pad = padded - cap
    idx_p = jnp.concatenate([idx, jnp.zeros((n_e, n_pad), I32)], axis=1)
    spare = n_rows + jnp.arange(padded + n_e * n_pad, dtype=I32)
    lead = spare[:padded][None, :]
    pad_rows = spare[padded:].reshape(n_e, n_pad)
    q_p = jnp.concatenate([lead, jnp.concatenate([qslot // ROW_TILE, pad_rows], axis=1)], axis=0) * ROW_TILE
    return idx_p.reshape(-1), q_p.reshape(-1), n_rows + padded + n_e * n_pad


def _moe(idxc, idxs, qc, qs, hc, hs, wrx, w_gate, w_up, w_down):
    capc, caps = idxc.shape[1], idxs.shape[1]
    rows = capc + caps
    tf = FF_TILE
    idxc, qc, zc_rows = _copy_tables(idxc, qc, N_EXPERTS * capc)
    idxs, qs, zs_rows = _copy_tables(idxs, qs, N_EXPERTS * caps)
    buf_rows = (_per_step(capc) + _per_step(caps)) * FF_STEPS
    any_spec = pl.BlockSpec(memory_space=pl.ANY)
    grid_spec = pltpu.PrefetchScalarGridSpec(
        num_scalar_prefetch=4,
        grid=(N_EXPERTS, FF_STEPS),
        in_specs=[any_spec, any_spec,
                  pl.BlockSpec((D_MODEL, LANES), lambda e, j, *_: (0, 0)),
                  pl.BlockSpec((1, D_MODEL, tf), lambda e, j, *_: (e, 0, j)),
                  pl.BlockSpec((1, D_MODEL, tf), lambda e, j, *_: (e, 0, j)),
                  pl.BlockSpec((1, tf, D_MODEL), lambda e, j, *_: (e, j, 0))],
        out_specs=[any_spec, any_spec],
        scratch_shapes=[pltpu.VMEM((2, buf_rows * ROW_TILE, LANES), F32),
                        pltpu.VMEM((2, buf_rows * ROW_TILE, LANES), F32),
                        pltpu.VMEM((rows, D_MODEL), BF16),
                        pltpu.VMEM((rows, LANES), F32),
                        pltpu.VMEM((rows, D_MODEL), F32),
                        pltpu.SemaphoreType.DMA((2,)),
                        pltpu.SemaphoreType.DMA((2,))])
    return pl.pallas_call(
        functools.partial(_moe_kernel, capc=capc, caps=caps),
        grid_spec=grid_spec,
        out_shape=[jax.ShapeDtypeStruct((zc_rows * ROW_TILE, LANES), F32),
                   jax.ShapeDtypeStruct((zs_rows * ROW_TILE, LANES), F32)],
        compiler_params=_cparams(("arbitrary", "arbitrary")),
        name="expert_ffn",
    )(idxc, idxs, qc, qs, hc, hs, wrx, w_gate, w_up, w_down)


PAIR_FIRST, PAIR_LAST, PAIR_VALID = 1, 2, 4


def _combine_kernel(chunk_ref, blk_ref, flag_ref, z_ref, span_ref, x1_ref, mod_ref, g_ref, o_ref, acc_ref):
    i = pl.program_id(0)
    flag = flag_ref[i]

    @pl.when((flag & PAIR_FIRST) != 0)
    def _():
        acc_ref[...] = jnp.zeros_like(acc_ref)

    @pl.when((flag & PAIR_VALID) != 0)
    def _():
        row = (chunk_ref[i] * ROW_CHUNK + lax.broadcasted_iota(I32, (TOK_BLOCK, ROW_CHUNK), 1)).astype(F32)
        first_row = span_ref[:, 0:1]
        end_row = span_ref[:, 1:2]
        onehot = jnp.where((row >= first_row) & (row < end_row), 1.0, 0.0).astype(BF16)
        y = jnp.concatenate([z_ref[pl.ds(kc, ROW_CHUNK, stride=ROW_TILE), :] for kc in range(ROW_TILE)], axis=1)
        y_hi = y.astype(BF16)
        y_lo = (y - y_hi.astype(F32)).astype(BF16)
        acc_ref[...] += _dot(onehot, y_hi) + _dot(onehot, y_lo)

    @pl.when((flag & PAIR_LAST) != 0)
    def _():
        gt2 = mod_ref[0][:, 5 * D_MODEL:6 * D_MODEL]
        o_ref[...] = x1_ref[...] + gt2 * _rms(acc_ref[...], g_ref[...], EPS)


def _combine(pairs, z, span, x1, mod3, mod_row, g_post_ffn):
    chunk, blk, flag = pairs
    t = x1.shape[0]
    tb = TOK_BLOCK
    grid_spec = pltpu.PrefetchScalarGridSpec(
        num_scalar_prefetch=3,
        grid=(chunk.shape[0],),
        in_specs=[pl.BlockSpec((ROW_CHUNK * ROW_TILE, LANES), lambda i, c, b, f: (c[i], 0)),
                  pl.BlockSpec((tb, LANES), lambda i, c, b, f: (b[i], 0)),
                  pl.BlockSpec((tb, D_MODEL), lambda i, c, b, f: (b[i], 0)),
                  pl.BlockSpec((1, 1, N_MOD * D_MODEL), lambda i, c, b, f: (mod_row(b[i]), 0, 0)),
                  pl.BlockSpec((1, D_MODEL), lambda i, c, b, f: (0, 0))],
        out_specs=pl.BlockSpec((tb, D_MODEL), lambda i, c, b, f: (b[i], 0)),
        scratch_shapes=[pltpu.VMEM((tb, D_MODEL), F32)])
    return pl.pallas_call(
        _combine_kernel,
        grid_spec=grid_spec,
        out_shape=jax.ShapeDtypeStruct((t, D_MODEL), F32),
        compiler_params=_cparams(("arbitrary",)),
        name="combine",
    )(chunk, blk, flag, z, span, x1, mod3, g_post_ffn)


def _rope_tables(seq):
    half = HEAD_DIM // 4
    freqs = ROPE_THETA ** (-np.arange(half, dtype=np.float64) / half)
    s = np.arange(seq)
    row = (s // GRID_W)[:, None] * freqs[None, :]
    col = (s % GRID_W)[:, None] * freqs[None, :]
    ang = np.concatenate([row, row, col, col], axis=1)
    ang = np.tile(ang, (1, QK_W // HEAD_DIM))
    lane = np.arange(QK_W)
    sign = np.where((lane % 32) < 16, -1.0, 1.0)[None, :]
    return (jnp.asarray(np.cos(ang), dtype=F32), jnp.asarray(np.sin(ang) * sign, dtype=F32))


def _slot_tile_ranges(offs, cap, n_tok):
    nt = n_tok // LANES
    lo, hi = offs[:, :nt], offs[:, 1:nt + 1]
    base = jnp.arange(cap // LANES, dtype=I32) * LANES
    clo = jnp.sum(hi[:, None, :] <= base[None, :, None], axis=-1).astype(I32)
    chi = jnp.sum(lo[:, None, :] < (base + LANES)[None, :, None], axis=-1).astype(I32)
    return clo, chi


def _combine_pairs(rows, n_tok):
    step = TOK_BLOCK // LANES
    nb = n_tok // TOK_BLOCK
    nc = 2 * n_tok // ROW_CHUNK
    starts = rows[0, 0:nb * step + 1:step]
    lo, hi = starts[:-1], starts[1:]
    c_lo = jnp.minimum(lo // ROW_CHUNK, nc - 1)
    c_hi = jnp.maximum((hi + ROW_CHUNK - 1) // ROW_CHUNK, c_lo + 1)
    n = c_hi - c_lo
    ends = jnp.cumsum(n)
    begins = ends - n
    i = jnp.arange(nb + nc, dtype=I32)
    valid = i < ends[-1]
    blk = jnp.minimum(jnp.sum(ends[None, :] <= i[:, None], axis=1), nb - 1).astype(I32)
    off = jnp.minimum(i - begins[blk], n[blk] - 1)
    chunk = (c_lo[blk] + off).astype(I32)
    first = valid & (i == begins[blk])
    last = valid & (i == ends[blk] - 1)
    flag = (first * PAIR_FIRST + last * PAIR_LAST + valid * PAIR_VALID).astype(I32)
    return chunk, blk, flag


def kernel(x_prompt, x_sample, c, cache_k, cache_v, c_ctx, w_mod, b_mod, g_pre_mix, g_post_mix, g_pre_ffn, g_post_ffn, w_in, lam_q1, lam_k1, lam_q2, lam_k2, g_subln, w_proj_attn, w_proj_fourier, w_out, w_router, w_gate, w_up, w_down):
    assert w_mod.shape[0] == 1
    lam_init = 0.8 - 0.6 * math.exp(-0.3 * 0)
    bp, sp, _ = x_prompt.shape
    bs, ss, _ = x_sample.shape

    cond8 = jnp.concatenate([c_ctx[None, :], c, jnp.zeros((8 - 1 - bs, D_MODEL), F32)], axis=0)
    mod3 = _modulation(cond8, w_mod[0], b_mod).reshape(8, 1, N_MOD * D_MODEL)

    w_in_b = w_in[0].astype(BF16)
    wpa = w_proj_attn[0].astype(BF16)
    wpf = w_proj_fourier[0].astype(BF16)
    wout = w_out[0].astype(BF16)
    wr = w_router[0].astype(BF16)
    wrt = wr.T
    wrx = jnp.concatenate([wr, jnp.zeros((D_MODEL, LANES - N_EXPERTS), BF16)], axis=1)
    lam_p = jnp.concatenate([lam_q1, lam_k1, lam_q2, lam_k2], axis=0)

    groups = []
    for x, seq, positional, ctx in ((x_prompt, sp, False, None),
                                    (x_sample, ss, True, (cache_k, cache_v))):
        nb = x.shape[0]
        t = nb * seq
        x2d = x.reshape(t, D_MODEL)
        blocks_per_seq = seq // ROW_BLOCK
        if positional:
            mod_row = lambda i, n=blocks_per_seq: 1 + i // n
        else:
            mod_row = lambda i: 0
        pre = _pre_mixer(x2d, mod3, mod_row, g_pre_mix, w_in_b,
                         _rope_tables(seq) if positional else None, seq, write_cache=not positional)
        q, k, v, f, ga, gf = pre[:6]
        o = _attention(lam_p, g_subln, q, k, v, ctx, seq, lam_init)
        fm = _fourier(f, seq)
        x1, h2t, aff_t = _post_mixer(o, fm, ga, gf, x2d, mod3, mod_row, g_post_mix, g_pre_ffn,
                                     wpa, wpf, wout, wrt)
        cap = 2 * t // N_EXPERTS
        posm, qdst, span, offs, rows = _route(aff_t, cap)
        clo, chi = _slot_tile_ranges(offs, cap, t)
        tiled = (N_EXPERTS, t // LANES, 1, LANES)
        idx, qslot = _slot_lists(clo, chi, posm.reshape(tiled), qdst.reshape(tiled), cap)
        groups.append(dict(x1=x1, h2t=h2t, idx=idx, qslot=qslot, span=span, pairs=_combine_pairs(rows, t),
                           mod_row=mod_row, cache=pre[6:], shape=x.shape))

    gc, gs_ = groups
    zc, zs = _moe(gc["idx"], gs_["idx"], gc["qslot"], gs_["qslot"], gc["h2t"], gs_["h2t"], wrx,
                  w_gate[0], w_up[0], w_down[0])
    outs = []
    for g, z in ((gc, zc), (gs_, zs)):
        out = _combine(g["pairs"], z, g["span"], g["x1"], mod3, g["mod_row"], g_post_ffn)
        outs.append(out.reshape(g["shape"]))
    new_k, new_v = gc["cache"]
    return (outs[0], outs[1], new_k, new_v)
```

```python
import functools
import math

import numpy as np
import jax
import jax.numpy as jnp
from jax import lax
from jax.experimental import pallas as pl
from jax.experimental.pallas import tpu as pltpu

F32 = jnp.float32
BF16 = jnp.bfloat16
I32 = jnp.int32

D_MODEL = 1024
N_HEADS = 6
HEAD_DIM = 64
V_DIM = 128
QK_W = 768
FOUR_W = 256
FOUR_G = 64
IN_W = 4608
N_EXPERTS = 16
D_FF = 2816
N_MOD = 6
GRID_W = 64
ROPE_THETA = 10000.0
EPS = 1e-6
SUBLN_EPS = 1e-5

LANES = 128
ROW_BLOCK = 256
POST_BLOCK = 1024
POST_SUB = 256
TOK_BLOCK = 256
ROW_CHUNK = 256
FF_TILE = 256
FF_STEPS = D_FF // FF_TILE
ROW_TILE = D_MODEL // LANES
TOKEN_BITS = 13
VMEM_LIMIT = 56 * 1024 * 1024


def _cparams(sem):
    return pltpu.CompilerParams(dimension_semantics=sem, vmem_limit_bytes=VMEM_LIMIT)


def _dot(a, b):
    return jnp.dot(a, b, preferred_element_type=F32)


def _rms(x, g, eps):
    return x * lax.rsqrt(jnp.mean(x * x, axis=-1, keepdims=True) + eps) * g


def _whole_vmem():
    return pl.BlockSpec(memory_space=pltpu.MemorySpace.VMEM)


def _mod_kernel(c_ref, w_ref, b_ref, o_ref):
    c = c_ref[...]
    s = c * jax.nn.sigmoid(c)
    o_ref[...] = _dot(s.astype(BF16), w_ref[...].astype(BF16)) + b_ref[...]


def _modulation(cond8, w_mod, b_mod):
    tn = 512
    n = N_MOD * D_MODEL
    return pl.pallas_call(
        _mod_kernel,
        grid=(n // tn,),
        in_specs=[pl.BlockSpec((8, D_MODEL), lambda j: (0, 0)),
                  pl.BlockSpec((D_MODEL, tn), lambda j: (0, j)),
                  pl.BlockSpec((1, tn), lambda j: (0, j))],
        out_specs=pl.BlockSpec((8, tn), lambda j: (0, j)),
        out_shape=jax.ShapeDtypeStruct((8, n), F32),
        compiler_params=_cparams(("arbitrary",)),
        name="modulation",
    )(cond8, w_mod, b_mod)


def _rope(z, cos, sin_signed, first_half):
    fwd = pltpu.roll(z, QK_W - 16, axis=1)
    bwd = pltpu.roll(z, 16, axis=1)
    return z * cos + jnp.where(first_half, fwd, bwd) * sin_signed


def _pre_kernel(*refs, positional, write_cache, fuse_fourier):
    it = iter(refs)
    x_ref, mod_ref, g_ref, w_ref = next(it), next(it), next(it), next(it)
    if positional:
        cos_ref, sin_ref = next(it), next(it)
    if fuse_fourier:
        dft_refs = [next(it) for _ in range(4)]
    q_ref, k_ref, v_ref, f_ref, ga_ref, gf_ref = (next(it) for _ in range(6))
    if write_cache:
        kc_ref, vc_ref = next(it), next(it)

    m = mod_ref[0]
    sh1 = m[:, 0:D_MODEL]
    sc1 = m[:, D_MODEL:2 * D_MODEL]
    h = _rms(x_ref[...], g_ref[...], EPS) * (1.0 + sc1) + sh1
    hb = h.astype(BF16)

    def proj(lo, hi):
        return _dot(hb, w_ref[:, lo:hi])

    zq = proj(0, QK_W)
    zk = proj(QK_W, 2 * QK_W)
    zv = proj(2 * QK_W, 3 * QK_W)
    if positional:
        lane = lax.broadcasted_iota(I32, (1, QK_W), 1)
        first_half = (lane % 32) < 16
        cos = cos_ref[...]
        sin_signed = sin_ref[...]
        zq = _rope(zq, cos, sin_signed, first_half)
        zk = _rope(zk, cos, sin_signed, first_half)
    q_ref[...] = zq.astype(BF16)
    k_ref[...] = zk.astype(BF16)
    v_ref[...] = zv.astype(BF16)
    if write_cache:
        for hd in range(N_HEADS):
            kc_ref[0, 0, hd] = zk[:, hd * V_DIM:(hd + 1) * V_DIM]
            vc_ref[0, 0, hd] = zv[:, hd * V_DIM:(hd + 1) * V_DIM]
    f0 = 3 * QK_W
    f = proj(f0, f0 + FOUR_W)
    f_ref[...] = _dft_real(f, *dft_refs) if fuse_fourier else f
    ga_ref[...] = jax.nn.sigmoid(proj(f0 + FOUR_W, f0 + FOUR_W + D_MODEL)).astype(BF16)
    gf_ref[...] = jax.nn.sigmoid(proj(f0 + FOUR_W + D_MODEL, IN_W)).astype(BF16)


def _pre_mixer(x2d, mod3, mod_row, g_pre, w_in_b, rope_tabs, seq, write_cache):
    t = x2d.shape[0]
    tm = ROW_BLOCK
    positional = rope_tabs is not None
    fuse_fourier = seq == tm
    blocks_per_seq = seq // tm
    row = lambda i: (i, 0)
    in_specs = [pl.BlockSpec((tm, D_MODEL), row),
                pl.BlockSpec((1, 1, N_MOD * D_MODEL), lambda i: (mod_row(i, tm), 0, 0)),
                pl.BlockSpec((1, D_MODEL), lambda i: (0, 0)),
                _whole_vmem()]
    args = [x2d, mod3, g_pre, w_in_b]
    if positional:
        in_specs += [pl.BlockSpec((tm, QK_W), lambda i: (i % blocks_per_seq, 0))] * 2
        args += list(rope_tabs)
    if fuse_fourier:
        consts = _dft_consts(seq)
        in_specs += [pl.BlockSpec(c.shape, lambda i: (0, 0)) for c in consts]
        args += list(consts)
    out_shape = [jax.ShapeDtypeStruct((t, QK_W), BF16)] * 3 + [
        jax.ShapeDtypeStruct((t, FOUR_W), BF16 if fuse_fourier else F32),
        jax.ShapeDtypeStruct((t, D_MODEL), BF16),
        jax.ShapeDtypeStruct((t, D_MODEL), BF16)]
    out_specs = [pl.BlockSpec((tm, QK_W), row)] * 3 + [
        pl.BlockSpec((tm, FOUR_W), row),
        pl.BlockSpec((tm, D_MODEL), row),
        pl.BlockSpec((tm, D_MODEL), row)]
    if write_cache:
        assert seq == tm
        nb = t // seq
        cshape = jax.ShapeDtypeStruct((nb, 1, N_HEADS, seq, V_DIM), F32)
        cspec = pl.BlockSpec((1, 1, N_HEADS, seq, V_DIM), lambda i: (i, 0, 0, 0, 0))
        out_shape += [cshape, cshape]
        out_specs += [cspec, cspec]
    return pl.pallas_call(
        functools.partial(_pre_kernel, positional=positional, write_cache=write_cache,
                          fuse_fourier=fuse_fourier),
        grid=(t // tm,),
        in_specs=in_specs,
        out_specs=out_specs,
        out_shape=out_shape,
        compiler_params=_cparams(("arbitrary",)),
        name="pre_mixer",
    )(*args)


def _attn_kernel(*refs, lam_init, has_ctx):
    it = iter(refs)
    lam_ref, gs_ref, q_ref, k_ref, v_ref = (next(it) for _ in range(5))
    if has_ctx:
        ck_ref, cv_ref = next(it), next(it)
    o_ref = next(it)

    lp = lam_ref[...]
    s1 = jnp.sum(lp[0:1] * lp[1:2], axis=-1, keepdims=True)
    s2 = jnp.sum(lp[2:3] * lp[3:4], axis=-1, keepdims=True)
    lam = jnp.exp(s1) - jnp.exp(s2) + lam_init
    lane = lax.broadcasted_iota(I32, (1, V_DIM), 1)
    comp1 = lane < HEAD_DIM
    scale = jnp.asarray(HEAD_DIM ** -0.5, BF16)
    nt = (((1,), (1,)), ((), ()))

    def attend(qc, k, v_ones):
        s = lax.dot_general(qc, k, nt, preferred_element_type=F32)
        ex = jnp.exp(s - jnp.max(s, axis=-1, keepdims=True)).astype(BF16)
        ov = _dot(ex, v_ones)
        return ov[:, 0:V_DIM] / ov[:, V_DIM:2 * V_DIM]

    for hd in range(N_HEADS):
        sl = slice(hd * V_DIM, (hd + 1) * V_DIM)
        q = q_ref[:, sl] * scale
        k = k_ref[:, sl]
        v = v_ref[:, sl]
        if has_ctx:
            k = jnp.concatenate([ck_ref[0, 0, hd].astype(BF16), k], axis=0)
            v = jnp.concatenate([cv_ref[0, 0, hd].astype(BF16), v], axis=0)
        v_ones = jnp.concatenate([v, jnp.ones_like(v)], axis=1)
        zero = jnp.zeros_like(q)
        o = attend(jnp.where(comp1, q, zero), k, v_ones) - lam * attend(jnp.where(comp1, zero, q), k, v_ones)
        o = _rms(o, gs_ref[...], SUBLN_EPS) * (1.0 - lam_init)
        o_ref[:, sl] = o.astype(BF16)


def _attention(lam_p, g_subln, q, k, v, ctx, seq, lam_init):
    t = q.shape[0]
    tq = ROW_BLOCK
    qb = seq // tq
    has_ctx = ctx is not None
    in_specs = [pl.BlockSpec((4, HEAD_DIM), lambda b, i: (0, 0)),
                pl.BlockSpec((1, V_DIM), lambda b, i: (0, 0)),
                pl.BlockSpec((tq, QK_W), lambda b, i: (b * qb + i, 0)),
                pl.BlockSpec((seq, QK_W), lambda b, i: (b, 0)),
                pl.BlockSpec((seq, QK_W), lambda b, i: (b, 0))]
    args = [lam_p, g_subln, q, k, v]
    if has_ctx:
        past = ctx[0].shape[3]
        cspec = pl.BlockSpec((1, 1, N_HEADS, past, V_DIM), lambda b, i: (b, 0, 0, 0, 0))
        in_specs += [cspec, cspec]
        args += list(ctx)
    return pl.pallas_call(
        functools.partial(_attn_kernel, lam_init=lam_init, has_ctx=has_ctx),
        grid=(t // seq, qb),
        in_specs=in_specs,
        out_specs=pl.BlockSpec((tq, QK_W), lambda b, i: (b * qb + i, 0)),
        out_shape=jax.ShapeDtypeStruct((t, QK_W), BF16),
        compiler_params=_cparams(("arbitrary", "arbitrary")),
        name="diff_attention",
    )(*args)


def _dft_real(f, bc_ref, bs_ref, cs_ref, ss_ref):
    fb = f.astype(BF16)
    u = _dot(fb, bc_ref[...].astype(BF16)).astype(BF16)
    w = _dot(fb, bs_ref[...].astype(BF16)).astype(BF16)
    return (_dot(cs_ref[...].astype(BF16), u) - _dot(ss_ref[...].astype(BF16), w)).astype(BF16)


def _fourier_kernel(f_ref, bc_ref, bs_ref, cs_ref, ss_ref, o_ref):
    o_ref[...] = _dft_real(f_ref[...], bc_ref, bs_ref, cs_ref, ss_ref)


def _dft_consts(seq):
    c = np.arange(FOUR_G)
    ang_c = 2.0 * np.pi * ((c[:, None] * c[None, :]) % FOUR_G) / FOUR_G
    eye = np.eye(FOUR_W // FOUR_G)
    bc = np.kron(eye, np.cos(ang_c)) / math.sqrt(FOUR_G)
    bs = np.kron(eye, np.sin(ang_c)) / math.sqrt(FOUR_G)
    s = np.arange(seq)
    ang_s = 2.0 * np.pi * ((s[:, None] * s[None, :]) % seq) / seq
    cs = np.cos(ang_s) / math.sqrt(seq)
    ss = np.sin(ang_s) / math.sqrt(seq)
    return tuple(jnp.asarray(a, dtype=F32) for a in (bc, bs, cs, ss))


def _fourier(f, seq):
    t = f.shape[0]
    bc, bs, cs, ss = _dft_consts(seq)
    const = lambda b: (0, 0)
    return pl.pallas_call(
        _fourier_kernel,
        grid=(t // seq,),
        in_specs=[pl.BlockSpec((seq, FOUR_W), lambda b: (b, 0)),
                  pl.BlockSpec((FOUR_W, FOUR_W), const),
                  pl.BlockSpec((FOUR_W, FOUR_W), const),
                  pl.BlockSpec((seq, seq), const),
                  pl.BlockSpec((seq, seq), const)],
        out_specs=pl.BlockSpec((seq, FOUR_W), lambda b: (b, 0)),
        out_shape=jax.ShapeDtypeStruct((t, FOUR_W), BF16),
        compiler_params=_cparams(("arbitrary",)),
        name="fourier_mix",
    )(f, bc, bs, cs, ss)


def _post_kernel(o_ref, fm_ref, ga_ref, gf_ref, x_ref, mod_ref, gpost_ref, gffn_ref,
                 wpa_ref, wpf_ref, wout_ref, wrx_ref,
                 x1_ref, h2t_ref, afft_ref):
    m = mod_ref[0]
    gt1 = m[:, 2 * D_MODEL:3 * D_MODEL]
    sh2 = m[:, 3 * D_MODEL:4 * D_MODEL]
    sc2 = m[:, 4 * D_MODEL:5 * D_MODEL]
    subs = [slice(r0, r0 + POST_SUB) for r0 in range(0, o_ref.shape[0], POST_SUB)]
    ab = [(_dot(o_ref[rs, :], wpa_ref[...]), _dot(fm_ref[rs, :], wpf_ref[...])) for rs in subs]
    merged = [(ga_ref[rs, :] * a + gf_ref[rs, :] * b).astype(BF16) for rs, (a, b) in zip(subs, ab)]
    ys = [_dot(mg, wout_ref[...]) for mg in merged]
    h2s = []
    for rs, y in zip(subs, ys):
        x1 = x_ref[rs, :] + gt1 * _rms(y, gpost_ref[...], EPS)
        x1_ref[rs, :] = x1
        h2s.append(_rms(x1, gffn_ref[...], EPS) * (1.0 + sc2) + sh2)
    logits = [_dot(h2.astype(BF16), wrx_ref[...]) for h2 in h2s]
    for rs, h2, lg in zip(subs, h2s, logits):
        lt = lg.T[0:N_EXPERTS]
        et = jnp.exp(lt - jnp.max(lt, axis=0, keepdims=True))
        afft_ref[:, rs] = et / jnp.sum(et, axis=0, keepdims=True)
        for kc in range(ROW_TILE):
            h2t_ref[pl.ds(rs.start * ROW_TILE + kc, POST_SUB, stride=ROW_TILE), :] = h2[:, kc * LANES:(kc + 1) * LANES]


def _post_mixer(o, fm, ga, gf, x2d, mod3, mod_row, g_post, g_ffn, wpa, wpf, wout, wrx):
    t = x2d.shape[0]
    tm = POST_BLOCK
    row = lambda i: (i, 0)
    const = lambda i: (0, 0)
    return pl.pallas_call(
        _post_kernel,
        grid=(t // tm,),
        in_specs=[pl.BlockSpec((tm, QK_W), row),
                  pl.BlockSpec((tm, FOUR_W), row),
                  pl.BlockSpec((tm, D_MODEL), row),
                  pl.BlockSpec((tm, D_MODEL), row),
                  pl.BlockSpec((tm, D_MODEL), row),
                  pl.BlockSpec((1, 1, N_MOD * D_MODEL), lambda i: (mod_row(i, tm), 0, 0)),
                  pl.BlockSpec((1, D_MODEL), const),
                  pl.BlockSpec((1, D_MODEL), const),
                  pl.BlockSpec((QK_W, D_MODEL), const),
                  pl.BlockSpec((FOUR_W, D_MODEL), const),
                  pl.BlockSpec((D_MODEL, D_MODEL), const),
                  pl.BlockSpec((D_MODEL, LANES), const)],
        out_specs=[pl.BlockSpec((tm, D_MODEL), row),
                   pl.BlockSpec((tm * ROW_TILE, LANES), row),
                   pl.BlockSpec((N_EXPERTS, tm), lambda i: (0, i))],
        out_shape=[jax.ShapeDtypeStruct((t, D_MODEL), F32),
                   jax.ShapeDtypeStruct((t * ROW_TILE, LANES), F32),
                   jax.ShapeDtypeStruct((N_EXPERTS, t), F32)],
        compiler_params=_cparams(("arbitrary",)),
        name="post_mixer",
    )(o, fm, ga, gf, x2d, mod3, g_post, g_ffn, wpa, wpf, wout, wrx)


def _route_kernel(aff_ref, tri_ref, posm_ref, pack_ref, span_ref, offs_ref, rows_ref, *, cap, n_tok):
    aff = aff_ref[...]
    capf = float(cap)

    def count_ge(v):
        return jnp.sum(jnp.where(aff >= v, 1.0, 0.0), axis=1, keepdims=True)

    def search(i, thr):
        cand = thr | jnp.left_shift(jnp.int32(1), 30 - i)
        return jnp.where(count_ge(pltpu.bitcast(cand, F32)) >= capf, cand, thr)

    thr = lax.fori_loop(0, 31, search, jnp.zeros((N_EXPERTS, 1), I32))
    lo = pltpu.bitcast(thr, F32)
    hi = pltpu.bitcast(thr + 1, F32)

    def refine(i, c):
        lo, hi = c
        mid = lo + (hi - lo) * 0.5
        ok = count_ge(mid) >= capf
        return jnp.where(ok, mid, lo), jnp.where(ok, hi, mid)

    lo, hi = lax.fori_loop(0, 24, refine, (lo, hi))
    gt = aff >= hi
    eq = (aff >= lo) & (aff < hi)
    n_gt = jnp.sum(jnp.where(gt, 1.0, 0.0), axis=1, keepdims=True)
    n_tie = capf - n_gt

    tri = tri_ref[...]
    lane = lax.broadcasted_iota(I32, (1, LANES), 1)
    ei = lax.broadcasted_iota(I32, (N_EXPERTS, N_EXPERTS), 0)
    ej = lax.broadcasted_iota(I32, (N_EXPERTS, N_EXPERTS), 1)
    below = jnp.where(ej < ei, 1.0, 0.0).astype(BF16)
    sub_sq = lax.broadcasted_iota(I32, (LANES, LANES), 0)
    n_tiles = n_tok // LANES
    carry_eq = jnp.zeros((N_EXPERTS, 1), F32)
    carry_sel = jnp.zeros((N_EXPERTS, 1), F32)
    carry_row = jnp.zeros((N_EXPERTS, 1), F32)
    offs = jnp.zeros((N_EXPERTS, LANES), F32)
    rows = jnp.zeros((N_EXPERTS, LANES), F32)
    for c in range(n_tiles):
        sl = slice(c * LANES, (c + 1) * LANES)
        eq_f = jnp.where(eq[:, sl], 1.0, 0.0)
        eq_incl = _dot(eq_f.astype(BF16), tri) + carry_eq
        sel_f = jnp.where(gt[:, sl], 1.0, jnp.where(eq_incl <= n_tie, eq_f, 0.0))
        sel_b = sel_f.astype(BF16)
        sel_incl = _dot(sel_b, tri) + carry_sel
        posm_ref[c] = jnp.where(sel_f > 0.5, sel_incl - sel_f, -1.0).astype(I32)
        cnt = jnp.broadcast_to(jnp.sum(sel_f, axis=0, keepdims=True), (N_EXPERTS, LANES))
        tok_start = _dot(cnt.astype(BF16), tri) - cnt + carry_row
        row = (tok_start + _dot(below, sel_b)).astype(I32)
        pack_ref[c] = row * (1 << TOKEN_BITS) + (c * LANES + lane)
        first_row = jnp.broadcast_to(tok_start[0:1, :], (LANES, LANES))
        end_row = jnp.broadcast_to(tok_start[0:1, :] + cnt[0:1, :], (LANES, LANES))
        span_ref[sl, :] = jnp.where(sub_sq == 0, first_row, jnp.where(sub_sq == 1, end_row, 0.0)).T
        offs = jnp.where(lane == c, carry_sel, offs)
        rows = jnp.where(lane == c, carry_row, rows)
        carry_eq = carry_eq + jnp.sum(eq_f, axis=1, keepdims=True)
        carry_sel = carry_sel + jnp.sum(sel_f, axis=1, keepdims=True)
        carry_row = carry_row + jnp.sum(cnt, axis=1, keepdims=True)
    offs_ref[...] = jnp.where(lane >= n_tiles, carry_sel, offs).astype(I32)
    rows_ref[...] = jnp.where(lane >= n_tiles, carry_row, rows).astype(I32)


def _route(aff_t, cap):
    n_tok = aff_t.shape[1]
    assert n_tok // LANES < LANES
    tri = jnp.asarray(np.triu(np.ones((LANES, LANES))), dtype=BF16)
    return pl.pallas_call(
        functools.partial(_route_kernel, cap=cap, n_tok=n_tok),
        out_shape=[jax.ShapeDtypeStruct((n_tok // LANES, N_EXPERTS, LANES), I32),
                   jax.ShapeDtypeStruct((n_tok // LANES, N_EXPERTS, LANES), I32),
                   jax.ShapeDtypeStruct((n_tok, LANES), F32),
                   jax.ShapeDtypeStruct((N_EXPERTS, LANES), I32),
                   jax.ShapeDtypeStruct((N_EXPERTS, LANES), I32)],
        compiler_params=pltpu.CompilerParams(vmem_limit_bytes=VMEM_LIMIT),
        name="route",
    )(aff_t, tri)


def _slots_kernel(clo_ref, chi_ref, posm_ref, pack_ref, idx_ref, qslot_ref):
    e = pl.program_id(0)
    s = pl.program_id(1)
    slot = lax.broadcasted_iota(I32, (LANES, LANES), 0) + s * LANES

    def body(c, acc):
        hit = posm_ref[c, pl.ds(e, 1), :] == slot
        return acc + jnp.where(hit, pack_ref[c, pl.ds(e, 1), :], 0)

    acc = lax.fori_loop(clo_ref[e, s], chi_ref[e, s], body, jnp.zeros((LANES, LANES), I32))
    eye = lax.broadcasted_iota(I32, (LANES, LANES), 0) == lax.broadcasted_iota(I32, (LANES, LANES), 1)

    def as_row(part):
        col = jnp.sum(part.astype(F32), axis=1, keepdims=True)
        return jnp.sum(jnp.where(eye, col, 0.0), axis=0, keepdims=True).astype(I32)

    idx_ref[0, pl.ds(s, 1), :] = as_row(acc & ((1 << TOKEN_BITS) - 1)) * ROW_TILE
    qslot_ref[0, pl.ds(s, 1), :] = as_row(lax.shift_right_logical(acc, TOKEN_BITS)) * ROW_TILE


def _slot_lists(clo, chi, posm4, qdst4, cap):
    ns = cap // LANES
    grid_spec = pltpu.PrefetchScalarGridSpec(
        num_scalar_prefetch=2,
        grid=(N_EXPERTS, ns),
        in_specs=[_whole_vmem(), _whole_vmem()],
        out_specs=[pl.BlockSpec((1, ns, LANES), lambda e, s, *_: (e, 0, 0)),
                   pl.BlockSpec((1, ns, LANES), lambda e, s, *_: (e, 0, 0))])
    idx, qslot = pl.pallas_call(
        _slots_kernel,
        grid_spec=grid_spec,
        out_shape=[jax.ShapeDtypeStruct((N_EXPERTS, ns, LANES), I32),
                   jax.ShapeDtypeStruct((N_EXPERTS, ns, LANES), I32)],
        compiler_params=_cparams(("arbitrary", "arbitrary")),
        name="slot_lists",
    )(clo, chi, posm4, qdst4)
    return idx.reshape(N_EXPERTS, cap), qslot.reshape(N_EXPERTS, cap)


def _moe_kernel(idxc_ref, idxs_ref, qc_ref, qs_ref,
                hc_ref, hs_ref, wr_ref, wg_ref, wu_ref, wd_ref, zc_ref, zs_ref,
                xbuf, ybuf, xb_ref, gate_ref, acc_ref, gsem, ssem, *, capc, caps):
    e = pl.program_id(0)
    j = pl.program_id(1)
    n_e = pl.num_programs(0)
    n_j = FF_STEPS
    slot = e % 2
    other = 1 - slot
    rows = capc + caps
    gc, gs = _per_step(capc), _per_step(caps)
    groups = ((hc_ref, idxc_ref, zc_ref, qc_ref, gc, 0), (hs_ref, idxs_ref, zs_ref, qs_ref, gs, gc * n_j))

    def tile(ref, first_sublane):
        return ref.at[pl.ds(pl.multiple_of(first_sublane, ROW_TILE), ROW_TILE), :]

    def gather(ex, sl, step, i, group):
        h_ref, idx_ref, _, _, per_step, base = group
        p = step * per_step + i
        src = tile(h_ref, idx_ref[ex * (per_step * n_j) + p])
        pltpu.make_async_copy(src, tile(xbuf.at[sl], (base + p) * ROW_TILE), gsem.at[sl]).start()

    def scatter(table_row, sl, step, i, group):
        _, _, z_ref, q_ref, per_step, base = group
        p = step * per_step + i
        dst = tile(z_ref, q_ref[table_row * (per_step * n_j) + p])
        pltpu.make_async_copy(tile(ybuf.at[sl], (base + p) * ROW_TILE), dst, ssem.at[sl]).start()

    def all_steps(fn):
        for group in groups:
            def body(p, carry, group=group):
                fn(p, group)
                return carry
            lax.fori_loop(0, group[4] * n_j, body, 0, unroll=8)

    def wait_all(buf, sem, sl):
        pltpu.make_async_copy(buf.at[sl], buf.at[sl], sem.at[sl]).wait()

    @pl.when((e == 0) & (j == 0))
    def _():
        ybuf[...] = jnp.zeros_like(ybuf)
        all_steps(lambda p, group: gather(0, 0, 0, p, group))

    @pl.when(j == 0)
    def _():
        wait_all(xbuf, gsem, slot)
        for base, n, dst in ((0, capc, 0), (gc * n_j, caps, capc)):
            for kc in range(ROW_TILE):
                tiles = xbuf[slot, pl.ds(base * ROW_TILE + kc, n, stride=ROW_TILE), :]
                xb_ref[dst:dst + n, kc * LANES:(kc + 1) * LANES] = tiles.astype(BF16)
        acc_ref[...] = jnp.zeros_like(acc_ref)
        logits = _dot(xb_ref[...], wr_ref[...])
        lane = lax.broadcasted_iota(I32, (1, LANES), 1)
        is_expert = lane < N_EXPERTS
        ex = jnp.exp(logits - jnp.max(jnp.where(is_expert, logits, -jnp.inf), axis=-1, keepdims=True))
        mine = jnp.sum(jnp.where(lane == e, ex, 0.0), axis=-1, keepdims=True)
        gate = mine / jnp.sum(jnp.where(is_expert, ex, 0.0), axis=-1, keepdims=True)
        gate_ref[...] = jnp.broadcast_to(gate, gate_ref.shape)

    nxt = jnp.minimum(e + 1, n_e - 1)
    for group in groups:
        for i in range(group[4]):
            gather(nxt, other, j, i, group)
            scatter(e, other, j, i, group)

    x = xb_ref[...]
    g = _dot(x, wg_ref[0].astype(BF16))
    u = _dot(x, wu_ref[0].astype(BF16))
    hid = (g * jax.nn.sigmoid(g) * u).astype(BF16)
    acc_ref[...] += _dot(hid, wd_ref[0].astype(BF16))

    @pl.when(j == n_j - 1)
    def _():
        @pl.when(e >= 1)
        def _():
            wait_all(ybuf, ssem, slot)

        gate = gate_ref[...]
        for base, n, src in ((0, capc, 0), (gc * n_j, caps, capc)):
            for kc in range(ROW_TILE):
                y = acc_ref[src:src + n, kc * LANES:(kc + 1) * LANES] * gate[src:src + n]
                ybuf[slot, pl.ds(base * ROW_TILE + kc, n, stride=ROW_TILE), :] = y

        @pl.when(e == n_e - 1)
        def _():
            all_steps(lambda p, group: scatter(e + 1, slot, 0, p, group))
            wait_all(ybuf, ssem, other)
            wait_all(ybuf, ssem, slot)
            wait_all(xbuf, gsem, other)


def _per_step(cap):
    return -(-cap // FF_STEPS)


def _copy_tables(idx, qslot, n_rows):
    n_e, cap = idx.shape
    padded = _per_step(cap) * FF_STEPS
    n_pad = padded - cap
    idx_p = jnp.concatenate([idx, jnp.zeros((n_e, n_pad), I32)], axis=1)
    spare = n_rows + jnp.arange(padded + n_e * n_pad, dtype=I32)
    lead = spare[:padded][None, :]
    pad_rows = spare[padded:].reshape(n_e, n_pad)
    q_p = jnp.concatenate([lead, jnp.concatenate([qslot // ROW_TILE, pad_rows], axis=1)], axis=0) * ROW_TILE
    return idx_p.reshape(-1), q_p.reshape(-1), n_rows + padded + n_e * n_pad


def _moe(idxc, idxs, qc, qs, hc, hs, wrx, w_gate, w_up, w_down):
    capc, caps = idxc.shape[1], idxs.shape[1]
    rows = capc + caps
    tf = FF_TILE
    idxc, qc, zc_rows = _copy_tables(idxc, qc, N_EXPERTS * capc)
    idxs, qs, zs_rows = _copy_tables(idxs, qs, N_EXPERTS * caps)
    buf_rows = (_per_step(capc) + _per_step(caps)) * FF_STEPS
    any_spec = pl.BlockSpec(memory_space=pl.ANY)
    grid_spec = pltpu.PrefetchScalarGridSpec(
        num_scalar_prefetch=4,
        grid=(N_EXPERTS, FF_STEPS),
        in_specs=[any_spec, any_spec,
                  pl.BlockSpec((D_MODEL, LANES), lambda e, j, *_: (0, 0)),
                  pl.BlockSpec((1, D_MODEL, tf), lambda e, j, *_: (e, 0, j)),
                  pl.BlockSpec((1, D_MODEL, tf), lambda e, j, *_: (e, 0, j)),
                  pl.BlockSpec((1, tf, D_MODEL), lambda e, j, *_: (e, j, 0))],
        out_specs=[any_spec, any_spec],
        scratch_shapes=[pltpu.VMEM((2, buf_rows * ROW_TILE, LANES), F32),
                        pltpu.VMEM((2, buf_rows * ROW_TILE, LANES), F32),
                        pltpu.VMEM((rows, D_MODEL), BF16),
                        pltpu.VMEM((rows, LANES), F32),
                        pltpu.VMEM((rows, D_MODEL), F32),
                        pltpu.SemaphoreType.DMA((2,)),
                        pltpu.SemaphoreType.DMA((2,))])
    return pl.pallas_call(
        functools.partial(_moe_kernel, capc=capc, caps=caps),
        grid_spec=grid_spec,
        out_shape=[jax.ShapeDtypeStruct((zc_rows * ROW_TILE, LANES), F32),
                   jax.ShapeDtypeStruct((zs_rows * ROW_TILE, LANES), F32)],
        compiler_params=_cparams(("arbitrary", "arbitrary")),
        name="expert_ffn",
    )(idxc, idxs, qc, qs, hc, hs, wrx, w_gate, w_up, w_down)


PAIR_FIRST, PAIR_LAST, PAIR_VALID = 1, 2, 4


def _combine_kernel(chunk_ref, blk_ref, flag_ref, z_ref, span_ref, x1_ref, mod_ref, g_ref, o_ref, acc_ref):
    i = pl.program_id(0)
    flag = flag_ref[i]

    @pl.when((flag & PAIR_FIRST) != 0)
    def _():
        acc_ref[...] = jnp.zeros_like(acc_ref)

    @pl.when((flag & PAIR_VALID) != 0)
    def _():
        row = (chunk_ref[i] * ROW_CHUNK + lax.broadcasted_iota(I32, (TOK_BLOCK, ROW_CHUNK), 1)).astype(F32)
        first_row = span_ref[:, 0:1]
        end_row = span_ref[:, 1:2]
        onehot = jnp.where((row >= first_row) & (row < end_row), 1.0, 0.0).astype(BF16)
        y = jnp.concatenate([z_ref[pl.ds(kc, ROW_CHUNK, stride=ROW_TILE), :] for kc in range(ROW_TILE)], axis=1)
        y_hi = y.astype(BF16)
        y_lo = (y - y_hi.astype(F32)).astype(BF16)
        acc_ref[...] += _dot(onehot, y_hi) + _dot(onehot, y_lo)

    @pl.when((flag & PAIR_LAST) != 0)
    def _():
        gt2 = mod_ref[0][:, 5 * D_MODEL:6 * D_MODEL]
        o_ref[...] = x1_ref[...] + gt2 * _rms(acc_ref[...], g_ref[...], EPS)


def _combine(pairs, z, span, x1, mod3, mod_row, g_post_ffn):
    chunk, blk, flag = pairs
    t = x1.shape[0]
    tb = TOK_BLOCK
    grid_spec = pltpu.PrefetchScalarGridSpec(
        num_scalar_prefetch=3,
        grid=(chunk.shape[0],),
        in_specs=[pl.BlockSpec((ROW_CHUNK * ROW_TILE, LANES), lambda i, c, b, f: (c[i], 0)),
                  pl.BlockSpec((tb, LANES), lambda i, c, b, f: (b[i], 0)),
                  pl.BlockSpec((tb, D_MODEL), lambda i, c, b, f: (b[i], 0)),
                  pl.BlockSpec((1, 1, N_MOD * D_MODEL), lambda i, c, b, f: (mod_row(b[i], tb), 0, 0)),
                  pl.BlockSpec((1, D_MODEL), lambda i, c, b, f: (0, 0))],
        out_specs=pl.BlockSpec((tb, D_MODEL), lambda i, c, b, f: (b[i], 0)),
        scratch_shapes=[pltpu.VMEM((tb, D_MODEL), F32)])
    return pl.pallas_call(
        _combine_kernel,
        grid_spec=grid_spec,
        out_shape=jax.ShapeDtypeStruct((t, D_MODEL), F32),
        compiler_params=_cparams(("arbitrary",)),
        name="combine",
    )(chunk, blk, flag, z, span, x1, mod3, g_post_ffn)


def _rope_tables(seq):
    half = HEAD_DIM // 4
    freqs = ROPE_THETA ** (-np.arange(half, dtype=np.float64) / half)
    s = np.arange(seq)
    row = (s // GRID_W)[:, None] * freqs[None, :]
    col = (s % GRID_W)[:, None] * freqs[None, :]
    ang = np.concatenate([row, row, col, col], axis=1)
    ang = np.tile(ang, (1, QK_W // HEAD_DIM))
    lane = np.arange(QK_W)
    sign = np.where((lane % 32) < 16, -1.0, 1.0)[None, :]
    return (jnp.asarray(np.cos(ang), dtype=F32), jnp.asarray(np.sin(ang) * sign, dtype=F32))


def _slot_tile_ranges(offs, cap, n_tok):
    nt = n_tok // LANES
    lo, hi = offs[:, :nt], offs[:, 1:nt + 1]
    base = jnp.arange(cap // LANES, dtype=I32) * LANES
    clo = jnp.sum(hi[:, None, :] <= base[None, :, None], axis=-1).astype(I32)
    chi = jnp.sum(lo[:, None, :] < (base + LANES)[None, :, None], axis=-1).astype(I32)
    return clo, chi


def _combine_pairs(rows, n_tok):
    step = TOK_BLOCK // LANES
    nb = n_tok // TOK_BLOCK
    nc = 2 * n_tok // ROW_CHUNK
    starts = rows[0, 0:nb * step + 1:step]
    lo, hi = starts[:-1], starts[1:]
    c_lo = jnp.minimum(lo // ROW_CHUNK, nc - 1)
    c_hi = jnp.maximum((hi + ROW_CHUNK - 1) // ROW_CHUNK, c_lo + 1)
    n = c_hi - c_lo
    ends = jnp.cumsum(n)
    begins = ends - n
    i = jnp.arange(nb + nc, dtype=I32)
    valid = i < ends[-1]
    blk = jnp.minimum(jnp.sum(ends[None, :] <= i[:, None], axis=1), nb - 1).astype(I32)
    off = jnp.minimum(i - begins[blk], n[blk] - 1)
    chunk = (c_lo[blk] + off).astype(I32)
    first = valid & (i == begins[blk])
    last = valid & (i == ends[blk] - 1)
    flag = (first * PAIR_FIRST + last * PAIR_LAST + valid * PAIR_VALID).astype(I32)
    return chunk, blk, flag


def kernel(x_prompt, x_sample, c, cache_k, cache_v, c_ctx, w_mod, b_mod, g_pre_mix, g_post_mix, g_pre_ffn, g_post_ffn, w_in, lam_q1, lam_k1, lam_q2, lam_k2, g_subln, w_proj_attn, w_proj_fourier, w_out, w_router, w_gate, w_up, w_down):
    assert w_mod.shape[0] == 1
    lam_init = 0.8 - 0.6 * math.exp(-0.3 * 0)
    bp, sp, _ = x_prompt.shape
    bs, ss, _ = x_sample.shape

    cond8 = jnp.concatenate([c_ctx[None, :], c, jnp.zeros((8 - 1 - bs, D_MODEL), F32)], axis=0)
    mod3 = _modulation(cond8, w_mod[0], b_mod).reshape(8, 1, N_MOD * D_MODEL)

    w_in_b = w_in[0].astype(BF16)
    wpa = w_proj_attn[0].astype(BF16)
    wpf = w_proj_fourier[0].astype(BF16)
    wout = w_out[0].astype(BF16)
    wr = w_router[0].astype(BF16)
    wrx = jnp.concatenate([wr, jnp.zeros((D_MODEL, LANES - N_EXPERTS), BF16)], axis=1)
    lam_p = jnp.concatenate([lam_q1, lam_k1, lam_q2, lam_k2], axis=0)

    groups = []
    for x, seq, positional, ctx in ((x_prompt, sp, False, None),
                                    (x_sample, ss, True, (cache_k, cache_v))):
        nb = x.shape[0]
        t = nb * seq
        x2d = x.reshape(t, D_MODEL)
        if positional:
            mod_row = lambda i, tm, seq=seq: 1 + (i * tm) // seq
        else:
            mod_row = lambda i, tm: 0
        pre = _pre_mixer(x2d, mod3, mod_row, g_pre_mix, w_in_b,
                         _rope_tables(seq) if positional else None, seq, write_cache=not positional)
        q, k, v, f, ga, gf = pre[:6]
        o = _attention(lam_p, g_subln, q, k, v, ctx, seq, lam_init)
        fm = f if seq == ROW_BLOCK else _fourier(f, seq)
        x1, h2t, aff_t = _post_mixer(o, fm, ga, gf, x2d, mod3, mod_row, g_post_mix, g_pre_ffn,
                                     wpa, wpf, wout, wrx)
        cap = 2 * t // N_EXPERTS
        assert t <= 1 << TOKEN_BITS
        posm, pack, span, offs, rows = _route(aff_t, cap)
        clo, chi = _slot_tile_ranges(offs, cap, t)
        idx, qslot = _slot_lists(clo, chi, posm, pack, cap)
        groups.append(dict(x1=x1, h2t=h2t, idx=idx, qslot=qslot, span=span, pairs=_combine_pairs(rows, t),
                           mod_row=mod_row, cache=pre[6:], shape=x.shape))

    gc, gs_ = groups
    zc, zs = _moe(gc["idx"], gs_["idx"], gc["qslot"], gs_["qslot"], gc["h2t"], gs_["h2t"], wrx,
                  w_gate[0], w_up[0], w_down[0])
    outs = []
    for g, z in ((gc, zc), (gs_, zs)):
        out = _combine(g["pairs"], z, g["span"], g["x1"], mod3, g["mod_row"], g_post_ffn)
        outs.append(out.reshape(g["shape"]))
    new_k, new_v = gc["cache"]
    return (outs[0], outs[1], new_k, new_v)
```

```python
import functools
import math

import numpy as np
import jax
import jax.numpy as jnp
from jax import lax
from jax.experimental import pallas as pl
from jax.experimental.pallas import tpu as pltpu

F32 = jnp.float32
BF16 = jnp.bfloat16
I32 = jnp.int32

D_MODEL = 1024
N_HEADS = 6
HEAD_DIM = 64
V_DIM = 128
QK_W = 768
FOUR_W = 256
FOUR_G = 64
IN_W = 4608
N_EXPERTS = 16
D_FF = 2816
N_MOD = 6
GRID_W = 64
ROPE_THETA = 10000.0
EPS = 1e-6
SUBLN_EPS = 1e-5

LANES = 128
ROW_BLOCK = 256
POST_BLOCK = 1024
POST_SUB = 256
TOK_BLOCK = 256
ROW_CHUNK = 256
FF_TILE = 256
FF_STEPS = D_FF // FF_TILE
ROW_TILE = D_MODEL // LANES
TOKEN_BITS = 13
VMEM_LIMIT = 56 * 1024 * 1024


def _cparams(sem):
    return pltpu.CompilerParams(dimension_semantics=sem, vmem_limit_bytes=VMEM_LIMIT)


def _dot(a, b):
    return jnp.dot(a, b, preferred_element_type=F32)


def _rms(x, g, eps):
    return x * lax.rsqrt(jnp.mean(x * x, axis=-1, keepdims=True) + eps) * g


def _whole_vmem():
    return pl.BlockSpec(memory_space=pltpu.MemorySpace.VMEM)


def _mod_kernel(c_ref, w_ref, b_ref, o_ref):
    c = c_ref[...]
    s = c * jax.nn.sigmoid(c)
    o_ref[...] = _dot(s.astype(BF16), w_ref[...].astype(BF16)) + b_ref[...]


def _modulation(cond8, w_mod, b_mod):
    tn = 512
    n = N_MOD * D_MODEL
    return pl.pallas_call(
        _mod_kernel,
        grid=(n // tn,),
        in_specs=[pl.BlockSpec((8, D_MODEL), lambda j: (0, 0)),
                  pl.BlockSpec((D_MODEL, tn), lambda j: (0, j)),
                  pl.BlockSpec((1, tn), lambda j: (0, j))],
        out_specs=pl.BlockSpec((8, tn), lambda j: (0, j)),
        out_shape=jax.ShapeDtypeStruct((8, n), F32),
        compiler_params=_cparams(("arbitrary",)),
        name="modulation",
    )(cond8, w_mod, b_mod)


def _rope(z, cos, sin_signed, first_half):
    fwd = pltpu.roll(z, QK_W - 16, axis=1)
    bwd = pltpu.roll(z, 16, axis=1)
    return z * cos + jnp.where(first_half, fwd, bwd) * sin_signed


def _pre_kernel(*refs, positional, write_cache, fuse_fourier):
    it = iter(refs)
    x_ref, mod_ref, g_ref, w_ref = next(it), next(it), next(it), next(it)
    if positional:
        cos_ref, sin_ref = next(it), next(it)
    if fuse_fourier:
        dft_refs = [next(it) for _ in range(4)]
    q_ref, k_ref, v_ref, f_ref, ga_ref, gf_ref = (next(it) for _ in range(6))
    if write_cache:
        kc_ref, vc_ref = next(it), next(it)

    m = mod_ref[0]
    sh1 = m[:, 0:D_MODEL]
    sc1 = m[:, D_MODEL:2 * D_MODEL]
    h = _rms(x_ref[...], g_ref[...], EPS) * (1.0 + sc1) + sh1
    hb = h.astype(BF16)

    def proj(lo, hi):
        return _dot(hb, w_ref[:, lo:hi])

    zq = proj(0, QK_W)
    zk = proj(QK_W, 2 * QK_W)
    zv = proj(2 * QK_W, 3 * QK_W)
    if positional:
        lane = lax.broadcasted_iota(I32, (1, QK_W), 1)
        first_half = (lane % 32) < 16
        cos = cos_ref[...]
        sin_signed = sin_ref[...]
        zq = _rope(zq, cos, sin_signed, first_half)
        zk = _rope(zk, cos, sin_signed, first_half)
    q_ref[...] = zq.astype(BF16)
    k_ref[...] = zk.astype(BF16)
    v_ref[...] = zv.astype(BF16)
    if write_cache:
        for hd in range(N_HEADS):
            kc_ref[0, 0, hd] = zk[:, hd * V_DIM:(hd + 1) * V_DIM]
            vc_ref[0, 0, hd] = zv[:, hd * V_DIM:(hd + 1) * V_DIM]
    f0 = 3 * QK_W
    f = proj(f0, f0 + FOUR_W)
    f_ref[...] = _dft_real(f, *dft_refs) if fuse_fourier else f
    ga_ref[...] = jax.nn.sigmoid(proj(f0 + FOUR_W, f0 + FOUR_W + D_MODEL)).astype(BF16)
    gf_ref[...] = jax.nn.sigmoid(proj(f0 + FOUR_W + D_MODEL, IN_W)).astype(BF16)


def _pre_mixer(x2d, mod3, mod_row, g_pre, w_in_b, rope_tabs, seq, write_cache):
    t = x2d.shape[0]
    tm = ROW_BLOCK
    positional = rope_tabs is not None
    fuse_fourier = seq == tm
    blocks_per_seq = seq // tm
    row = lambda i: (i, 0)
    in_specs = [pl.BlockSpec((tm, D_MODEL), row),
                pl.BlockSpec((1, 1, N_MOD * D_MODEL), lambda i: (mod_row(i, tm), 0, 0)),
                pl.BlockSpec((1, D_MODEL), lambda i: (0, 0)),
                _whole_vmem()]
    args = [x2d, mod3, g_pre, w_in_b]
    if positional:
        in_specs += [pl.BlockSpec((tm, QK_W), lambda i: (i % blocks_per_seq, 0))] * 2
        args += list(rope_tabs)
    if fuse_fourier:
        consts = _dft_consts(seq)
        in_specs += [pl.BlockSpec(c.shape, lambda i: (0, 0)) for c in consts]
        args += list(consts)
    out_shape = [jax.ShapeDtypeStruct((t, QK_W), BF16)] * 3 + [
        jax.ShapeDtypeStruct((t, FOUR_W), BF16 if fuse_fourier else F32),
        jax.ShapeDtypeStruct((t, D_MODEL), BF16),
        jax.ShapeDtypeStruct((t, D_MODEL), BF16)]
    out_specs = [pl.BlockSpec((tm, QK_W), row)] * 3 + [
        pl.BlockSpec((tm, FOUR_W), row),
        pl.BlockSpec((tm, D_MODEL), row),
        pl.BlockSpec((tm, D_MODEL), row)]
    if write_cache:
        assert seq == tm
        nb = t // seq
        cshape = jax.ShapeDtypeStruct((nb, 1, N_HEADS, seq, V_DIM), F32)
        cspec = pl.BlockSpec((1, 1, N_HEADS, seq, V_DIM), lambda i: (i, 0, 0, 0, 0))
        out_shape += [cshape, cshape]
        out_specs += [cspec, cspec]
    return pl.pallas_call(
        functools.partial(_pre_kernel, positional=positional, write_cache=write_cache,
                          fuse_fourier=fuse_fourier),
        grid=(t // tm,),
        in_specs=in_specs,
        out_specs=out_specs,
        out_shape=out_shape,
        compiler_params=_cparams(("arbitrary",)),
        name="pre_mixer",
    )(*args)


def _attn_kernel(*refs, lam_init, has_ctx):
    it = iter(refs)
    lam_ref, gs_ref, q_ref, k_ref, v_ref = (next(it) for _ in range(5))
    if has_ctx:
        ck_ref, cv_ref = next(it), next(it)
    o_ref = next(it)

    lp = lam_ref[...]
    s1 = jnp.sum(lp[0:1] * lp[1:2], axis=-1, keepdims=True)
    s2 = jnp.sum(lp[2:3] * lp[3:4], axis=-1, keepdims=True)
    lam = jnp.exp(s1) - jnp.exp(s2) + lam_init
    lane = lax.broadcasted_iota(I32, (1, V_DIM), 1)
    comp1 = lane < HEAD_DIM
    scale = jnp.asarray(HEAD_DIM ** -0.5, BF16)
    nt = (((1,), (1,)), ((), ()))

    def attend(qc, k, v_ones):
        s = lax.dot_general(qc, k, nt, preferred_element_type=F32)
        ex = jnp.exp(s - jnp.max(s, axis=-1, keepdims=True)).astype(BF16)
        ov = _dot(ex, v_ones)
        return ov[:, 0:V_DIM] / ov[:, V_DIM:2 * V_DIM]

    for hd in range(N_HEADS):
        sl = slice(hd * V_DIM, (hd + 1) * V_DIM)
        q = q_ref[:, sl] * scale
        k = k_ref[:, sl]
        v = v_ref[:, sl]
        if has_ctx:
            k = jnp.concatenate([ck_ref[0, 0, hd].astype(BF16), k], axis=0)
            v = jnp.concatenate([cv_ref[0, 0, hd].astype(BF16), v], axis=0)
        v_ones = jnp.concatenate([v, jnp.ones_like(v)], axis=1)
        zero = jnp.zeros_like(q)
        o = attend(jnp.where(comp1, q, zero), k, v_ones) - lam * attend(jnp.where(comp1, zero, q), k, v_ones)
        o = _rms(o, gs_ref[...], SUBLN_EPS) * (1.0 - lam_init)
        o_ref[:, sl] = o.astype(BF16)


def _attention(lam_p, g_subln, q, k, v, ctx, seq, lam_init):
    t = q.shape[0]
    tq = ROW_BLOCK
    qb = seq // tq
    has_ctx = ctx is not None
    in_specs = [pl.BlockSpec((4, HEAD_DIM), lambda b, i: (0, 0)),
                pl.BlockSpec((1, V_DIM), lambda b, i: (0, 0)),
                pl.BlockSpec((tq, QK_W), lambda b, i: (b * qb + i, 0)),
                pl.BlockSpec((seq, QK_W), lambda b, i: (b, 0)),
                pl.BlockSpec((seq, QK_W), lambda b, i: (b, 0))]
    args = [lam_p, g_subln, q, k, v]
    if has_ctx:
        past = ctx[0].shape[3]
        cspec = pl.BlockSpec((1, 1, N_HEADS, past, V_DIM), lambda b, i: (b, 0, 0, 0, 0))
        in_specs += [cspec, cspec]
        args += list(ctx)
    return pl.pallas_call(
        functools.partial(_attn_kernel, lam_init=lam_init, has_ctx=has_ctx),
        grid=(t // seq, qb),
        in_specs=in_specs,
        out_specs=pl.BlockSpec((tq, QK_W), lambda b, i: (b * qb + i, 0)),
        out_shape=jax.ShapeDtypeStruct((t, QK_W), BF16),
        compiler_params=_cparams(("arbitrary", "arbitrary")),
        name="diff_attention",
    )(*args)


def _dft_real(f, bc_ref, bs_ref, cs_ref, ss_ref):
    fb = f.astype(BF16)
    u = _dot(fb, bc_ref[...].astype(BF16)).astype(BF16)
    w = _dot(fb, bs_ref[...].astype(BF16)).astype(BF16)
    return (_dot(cs_ref[...].astype(BF16), u) - _dot(ss_ref[...].astype(BF16), w)).astype(BF16)


def _fourier_kernel(f_ref, bc_ref, bs_ref, cs_ref, ss_ref, o_ref):
    o_ref[...] = _dft_real(f_ref[...], bc_ref, bs_ref, cs_ref, ss_ref)


def _dft_consts(seq):
    c = np.arange(FOUR_G)
    ang_c = 2.0 * np.pi * ((c[:, None] * c[None, :]) % FOUR_G) / FOUR_G
    eye = np.eye(FOUR_W // FOUR_G)
    bc = np.kron(eye, np.cos(ang_c)) / math.sqrt(FOUR_G)
    bs = np.kron(eye, np.sin(ang_c)) / math.sqrt(FOUR_G)
    s = np.arange(seq)
    ang_s = 2.0 * np.pi * ((s[:, None] * s[None, :]) % seq) / seq
    cs = np.cos(ang_s) / math.sqrt(seq)
    ss = np.sin(ang_s) / math.sqrt(seq)
    return tuple(jnp.asarray(a, dtype=F32) for a in (bc, bs, cs, ss))


def _fourier(f, seq):
    t = f.shape[0]
    bc, bs, cs, ss = _dft_consts(seq)
    const = lambda b: (0, 0)
    return pl.pallas_call(
        _fourier_kernel,
        grid=(t // seq,),
        in_specs=[pl.BlockSpec((seq, FOUR_W), lambda b: (b, 0)),
                  pl.BlockSpec((FOUR_W, FOUR_W), const),
                  pl.BlockSpec((FOUR_W, FOUR_W), const),
                  pl.BlockSpec((seq, seq), const),
                  pl.BlockSpec((seq, seq), const)],
        out_specs=pl.BlockSpec((seq, FOUR_W), lambda b: (b, 0)),
        out_shape=jax.ShapeDtypeStruct((t, FOUR_W), BF16),
        compiler_params=_cparams(("arbitrary",)),
        name="fourier_mix",
    )(f, bc, bs, cs, ss)


def _post_kernel(o_ref, fm_ref, ga_ref, gf_ref, x_ref, mod_ref, gpost_ref, gffn_ref,
                 wpa_ref, wpf_ref, wout_ref, wrx_ref,
                 x1_ref, h2t_ref, afft_ref):
    m = mod_ref[0]
    gt1 = m[:, 2 * D_MODEL:3 * D_MODEL]
    sh2 = m[:, 3 * D_MODEL:4 * D_MODEL]
    sc2 = m[:, 4 * D_MODEL:5 * D_MODEL]
    subs = [slice(r0, r0 + POST_SUB) for r0 in range(0, o_ref.shape[0], POST_SUB)]
    ab = [(_dot(o_ref[rs, :], wpa_ref[...]), _dot(fm_ref[rs, :], wpf_ref[...])) for rs in subs]
    merged = [(ga_ref[rs, :] * a + gf_ref[rs, :] * b).astype(BF16) for rs, (a, b) in zip(subs, ab)]
    ys = [_dot(mg, wout_ref[...]) for mg in merged]
    h2s = []
    for rs, y in zip(subs, ys):
        x1 = x_ref[rs, :] + gt1 * _rms(y, gpost_ref[...], EPS)
        x1_ref[rs, :] = x1
        h2s.append(_rms(x1, gffn_ref[...], EPS) * (1.0 + sc2) + sh2)
    logits = [_dot(h2.astype(BF16), wrx_ref[...]) for h2 in h2s]
    for rs, h2, lg in zip(subs, h2s, logits):
        lt = lg.T[0:N_EXPERTS]
        et = jnp.exp(lt - jnp.max(lt, axis=0, keepdims=True))
        afft_ref[:, rs] = et / jnp.sum(et, axis=0, keepdims=True)
        for kc in range(ROW_TILE):
            h2t_ref[pl.ds(rs.start * ROW_TILE + kc, POST_SUB, stride=ROW_TILE), :] = h2[:, kc * LANES:(kc + 1) * LANES]


def _post_mixer(o, fm, ga, gf, x2d, mod3, mod_row, g_post, g_ffn, wpa, wpf, wout, wrx):
    t = x2d.shape[0]
    tm = POST_BLOCK
    row = lambda i: (i, 0)
    const = lambda i: (0, 0)
    return pl.pallas_call(
        _post_kernel,
        grid=(t // tm,),
        in_specs=[pl.BlockSpec((tm, QK_W), row),
                  pl.BlockSpec((tm, FOUR_W), row),
                  pl.BlockSpec((tm, D_MODEL), row),
                  pl.BlockSpec((tm, D_MODEL), row),
                  pl.BlockSpec((tm, D_MODEL), row),
                  pl.BlockSpec((1, 1, N_MOD * D_MODEL), lambda i: (mod_row(i, tm), 0, 0)),
                  pl.BlockSpec((1, D_MODEL), const),
                  pl.BlockSpec((1, D_MODEL), const),
                  pl.BlockSpec((QK_W, D_MODEL), const),
                  pl.BlockSpec((FOUR_W, D_MODEL), const),
                  pl.BlockSpec((D_MODEL, D_MODEL), const),
                  pl.BlockSpec((D_MODEL, LANES), const)],
        out_specs=[pl.BlockSpec((tm, D_MODEL), row),
                   pl.BlockSpec((tm * ROW_TILE, LANES), row),
                   pl.BlockSpec((N_EXPERTS, tm), lambda i: (0, i))],
        out_shape=[jax.ShapeDtypeStruct((t, D_MODEL), F32),
                   jax.ShapeDtypeStruct((t * ROW_TILE, LANES), F32),
                   jax.ShapeDtypeStruct((N_EXPERTS, t), F32)],
        compiler_params=_cparams(("arbitrary",)),
        name="post_mixer",
    )(o, fm, ga, gf, x2d, mod3, g_post, g_ffn, wpa, wpf, wout, wrx)


def _route_kernel(aff_ref, tri_ref, posm_ref, pack_ref, span_ref, offs_ref, rows_ref, *, cap, n_tok):
    aff = aff_ref[...]
    capf = float(cap)

    def count_ge(v):
        return jnp.sum(jnp.where(aff >= v, 1.0, 0.0), axis=1, keepdims=True)

    def search(i, thr):
        cand = thr | jnp.left_shift(jnp.int32(1), 30 - i)
        return jnp.where(count_ge(pltpu.bitcast(cand, F32)) >= capf, cand, thr)

    thr = lax.fori_loop(0, 31, search, jnp.zeros((N_EXPERTS, 1), I32))
    lo = pltpu.bitcast(thr, F32)
    hi = pltpu.bitcast(thr + 1, F32)

    def refine(i, c):
        lo, hi = c
        mid = lo + (hi - lo) * 0.5
        ok = count_ge(mid) >= capf
        return jnp.where(ok, mid, lo), jnp.where(ok, hi, mid)

    lo, hi = lax.fori_loop(0, 24, refine, (lo, hi))
    gt = aff >= hi
    eq = (aff >= lo) & (aff < hi)
    n_gt = jnp.sum(jnp.where(gt, 1.0, 0.0), axis=1, keepdims=True)
    n_tie = capf - n_gt

    tri = tri_ref[...]
    lane = lax.broadcasted_iota(I32, (1, LANES), 1)
    ei = lax.broadcasted_iota(I32, (N_EXPERTS, N_EXPERTS), 0)
    ej = lax.broadcasted_iota(I32, (N_EXPERTS, N_EXPERTS), 1)
    below = jnp.where(ej < ei, 1.0, 0.0).astype(BF16)
    sub_sq = lax.broadcasted_iota(I32, (LANES, LANES), 0)
    n_tiles = n_tok // LANES
    carry_eq = jnp.zeros((N_EXPERTS, 1), F32)
    carry_sel = jnp.zeros((N_EXPERTS, 1), F32)
    carry_row = jnp.zeros((N_EXPERTS, 1), F32)
    offs = jnp.zeros((N_EXPERTS, LANES), F32)
    rows = jnp.zeros((N_EXPERTS, LANES), F32)
    for c in range(n_tiles):
        sl = slice(c * LANES, (c + 1) * LANES)
        eq_f = jnp.where(eq[:, sl], 1.0, 0.0)
        eq_incl = _dot(eq_f.astype(BF16), tri) + carry_eq
        sel_f = jnp.where(gt[:, sl], 1.0, jnp.where(eq_incl <= n_tie, eq_f, 0.0))
        sel_b = sel_f.astype(BF16)
        sel_incl = _dot(sel_b, tri) + carry_sel
        posm_ref[c] = jnp.where(sel_f > 0.5, sel_incl - sel_f, -1.0).astype(I32)
        cnt = jnp.broadcast_to(jnp.sum(sel_f, axis=0, keepdims=True), (N_EXPERTS, LANES))
        tok_start = _dot(cnt.astype(BF16), tri) - cnt + carry_row
        row = (tok_start + _dot(below, sel_b)).astype(I32)
        pack_ref[c] = row * (1 << TOKEN_BITS) + (c * LANES + lane)
        first_row = jnp.broadcast_to(tok_start[0:1, :], (LANES, LANES))
        end_row = jnp.broadcast_to(tok_start[0:1, :] + cnt[0:1, :], (LANES, LANES))
        span_ref[sl, :] = jnp.where(sub_sq == 0, first_row, jnp.where(sub_sq == 1, end_row, 0.0)).T
        offs = jnp.where(lane == c, carry_sel, offs)
        rows = jnp.where(lane == c, carry_row, rows)
        carry_eq = carry_eq + jnp.sum(eq_f, axis=1, keepdims=True)
        carry_sel = carry_sel + jnp.sum(sel_f, axis=1, keepdims=True)
        carry_row = carry_row + jnp.sum(cnt, axis=1, keepdims=True)
    offs_ref[...] = jnp.where(lane >= n_tiles, carry_sel, offs).astype(I32)
    rows_ref[...] = jnp.where(lane >= n_tiles, carry_row, rows).astype(I32)


def _route(aff_t, cap):
    n_tok = aff_t.shape[1]
    assert n_tok // LANES < LANES
    tri = jnp.asarray(np.triu(np.ones((LANES, LANES))), dtype=BF16)
    return pl.pallas_call(
        functools.partial(_route_kernel, cap=cap, n_tok=n_tok),
        out_shape=[jax.ShapeDtypeStruct((n_tok // LANES, N_EXPERTS, LANES), I32),
                   jax.ShapeDtypeStruct((n_tok // LANES, N_EXPERTS, LANES), I32),
                   jax.ShapeDtypeStruct((n_tok, LANES), F32),
                   jax.ShapeDtypeStruct((N_EXPERTS, LANES), I32),
                   jax.ShapeDtypeStruct((N_EXPERTS, LANES), I32)],
        compiler_params=pltpu.CompilerParams(vmem_limit_bytes=VMEM_LIMIT),
        name="route",
    )(aff_t, tri)


def _slots_kernel(clo_ref, chi_ref, posm_ref, pack_ref, idx_ref, qslot_ref):
    e = pl.program_id(0)
    sub = lax.broadcasted_iota(I32, (LANES, LANES), 0)
    eye = sub == lax.broadcasted_iota(I32, (LANES, LANES), 1)

    def as_row(part):
        col = jnp.sum(part.astype(F32), axis=1, keepdims=True)
        return jnp.sum(jnp.where(eye, col, 0.0), axis=0, keepdims=True).astype(I32)

    for s in range(idx_ref.shape[1]):
        slot = sub + s * LANES

        def body(c, acc, slot=slot):
            hit = posm_ref[c, pl.ds(e, 1), :] == slot
            return acc + jnp.where(hit, pack_ref[c, pl.ds(e, 1), :], 0)

        acc = lax.fori_loop(clo_ref[e, s], chi_ref[e, s], body, jnp.zeros((LANES, LANES), I32))
        idx_ref[0, s:s + 1, :] = as_row(acc & ((1 << TOKEN_BITS) - 1)) * ROW_TILE
        qslot_ref[0, s:s + 1, :] = as_row(lax.shift_right_logical(acc, TOKEN_BITS)) * ROW_TILE


def _slot_lists(clo, chi, posm4, qdst4, cap):
    ns = cap // LANES
    grid_spec = pltpu.PrefetchScalarGridSpec(
        num_scalar_prefetch=2,
        grid=(N_EXPERTS,),
        in_specs=[_whole_vmem(), _whole_vmem()],
        out_specs=[pl.BlockSpec((1, ns, LANES), lambda e, *_: (e, 0, 0)),
                   pl.BlockSpec((1, ns, LANES), lambda e, *_: (e, 0, 0))])
    idx, qslot = pl.pallas_call(
        _slots_kernel,
        grid_spec=grid_spec,
        out_shape=[jax.ShapeDtypeStruct((N_EXPERTS, ns, LANES), I32),
                   jax.ShapeDtypeStruct((N_EXPERTS, ns, LANES), I32)],
        compiler_params=_cparams(("arbitrary",)),
        name="slot_lists",
    )(clo, chi, posm4, qdst4)
    return idx.reshape(N_EXPERTS, cap), qslot.reshape(N_EXPERTS, cap)


def _moe_kernel(idxc_ref, idxs_ref, qc_ref, qs_ref,
                hc_ref, hs_ref, wr_ref, wg_ref, wu_ref, wd_ref, zc_ref, zs_ref,
                xbuf, ybuf, xb_ref, gate_ref, acc_ref, gsem, ssem, *, capc, caps):
    e = pl.program_id(0)
    j = pl.program_id(1)
    n_e = pl.num_programs(0)
    n_j = FF_STEPS
    slot = e % 2
    other = 1 - slot
    rows = capc + caps
    gc, gs = _per_step(capc), _per_step(caps)
    groups = ((hc_ref, idxc_ref, zc_ref, qc_ref, gc, 0), (hs_ref, idxs_ref, zs_ref, qs_ref, gs, gc * n_j))

    def tile(ref, first_sublane):
        return ref.at[pl.ds(pl.multiple_of(first_sublane, ROW_TILE), ROW_TILE), :]

    def gather(ex, sl, step, i, group):
        h_ref, idx_ref, _, _, per_step, base = group
        p = step * per_step + i
        src = tile(h_ref, idx_ref[ex * (per_step * n_j) + p])
        pltpu.make_async_copy(src, tile(xbuf.at[sl], (base + p) * ROW_TILE), gsem.at[sl]).start()

    def scatter(table_row, sl, step, i, group):
        _, _, z_ref, q_ref, per_step, base = group
        p = step * per_step + i
        dst = tile(z_ref, q_ref[table_row * (per_step * n_j) + p])
        pltpu.make_async_copy(tile(ybuf.at[sl], (base + p) * ROW_TILE), dst, ssem.at[sl]).start()

    def all_steps(fn):
        for group in groups:
            def body(p, carry, group=group):
                fn(p, group)
                return carry
            lax.fori_loop(0, group[4] * n_j, body, 0, unroll=8)

    def wait_all(buf, sem, sl):
        pltpu.make_async_copy(buf.at[sl], buf.at[sl], sem.at[sl]).wait()

    @pl.when((e == 0) & (j == 0))
    def _():
        ybuf[...] = jnp.zeros_like(ybuf)
        all_steps(lambda p, group: gather(0, 0, 0, p, group))

    @pl.when(j == 0)
    def _():
        wait_all(xbuf, gsem, slot)
        for base, n, dst in ((0, capc, 0), (gc * n_j, caps, capc)):
            for kc in range(ROW_TILE):
                tiles = xbuf[slot, pl.ds(base * ROW_TILE + kc, n, stride=ROW_TILE), :]
                xb_ref[dst:dst + n, kc * LANES:(kc + 1) * LANES] = tiles.astype(BF16)
        acc_ref[...] = jnp.zeros_like(acc_ref)
        logits = _dot(xb_ref[...], wr_ref[...])
        lane = lax.broadcasted_iota(I32, (1, LANES), 1)
        is_expert = lane < N_EXPERTS
        ex = jnp.exp(logits - jnp.max(jnp.where(is_expert, logits, -jnp.inf), axis=-1, keepdims=True))
        mine = jnp.sum(jnp.where(lane == e, ex, 0.0), axis=-1, keepdims=True)
        gate = mine / jnp.sum(jnp.where(is_expert, ex, 0.0), axis=-1, keepdims=True)
        gate_ref[...] = jnp.broadcast_to(gate, gate_ref.shape)

    nxt = jnp.minimum(e + 1, n_e - 1)
    for group in groups:
        for i in range(group[4]):
            gather(nxt, other, j, i, group)
            scatter(e, other, j, i, group)

    x = xb_ref[...]
    g = _dot(x, wg_ref[0].astype(BF16))
    u = _dot(x, wu_ref[0].astype(BF16))
    hid = (g * jax.nn.sigmoid(g) * u).astype(BF16)
    acc_ref[...] += _dot(hid, wd_ref[0].astype(BF16))

    @pl.when(j == n_j - 1)
    def _():
        @pl.when(e >= 1)
        def _():
            wait_all(ybuf, ssem, slot)

        gate = gate_ref[...]
        for base, n, src in ((0, capc, 0), (gc * n_j, caps, capc)):
            for kc in range(ROW_TILE):
                y = acc_ref[src:src + n, kc * LANES:(kc + 1) * LANES] * gate[src:src + n]
                ybuf[slot, pl.ds(base * ROW_TILE + kc, n, stride=ROW_TILE), :] = y

        @pl.when(e == n_e - 1)
        def _():
            all_steps(lambda p, group: scatter(e + 1, slot, 0, p, group))
            wait_all(ybuf, ssem, other)
            wait_all(ybuf, ssem, slot)
            wait_all(xbuf, gsem, other)


def _per_step(cap):
    return -(-cap // FF_STEPS)


def _copy_tables(idx, qslot, n_rows):
    n_e, cap = idx.shape
    padded = _per_step(cap) * FF_STEPS
    n_pad = padded - cap
    idx_p = jnp.concatenate([idx, jnp.zeros((n_e, n_pad), I32)], axis=1)
    spare = n_rows + jnp.arange(padded + n_e * n_pad, dtype=I32)
    lead = spare[:padded][None, :]
    pad_rows = spare[padded:].reshape(n_e, n_pad)
    q_p = jnp.concatenate([lead, jnp.concatenate([qslot // ROW_TILE, pad_rows], axis=1)], axis=0) * ROW_TILE
    return idx_p.reshape(-1), q_p.reshape(-1), n_rows + padded + n_e * n_pad


def _moe(idxc, idxs, qc, qs, hc, hs, wrx, w_gate, w_up, w_down):
    capc, caps = idxc.shape[1], idxs.shape[1]
    rows = capc + caps
    tf = FF_TILE
    idxc, qc, zc_rows = _copy_tables(idxc, qc, N_EXPERTS * capc)
    idxs, qs, zs_rows = _copy_tables(idxs, qs, N_EXPERTS * caps)
    buf_rows = (_per_step(capc) + _per_step(caps)) * FF_STEPS
    any_spec = pl.BlockSpec(memory_space=pl.ANY)
    grid_spec = pltpu.PrefetchScalarGridSpec(
        num_scalar_prefetch=4,
        grid=(N_EXPERTS, FF_STEPS),
        in_specs=[any_spec, any_spec,
                  pl.BlockSpec((D_MODEL, LANES), lambda e, j, *_: (0, 0)),
                  pl.BlockSpec((1, D_MODEL, tf), lambda e, j, *_: (e, 0, j)),
                  pl.BlockSpec((1, D_MODEL, tf), lambda e, j, *_: (e, 0, j)),
                  pl.BlockSpec((1, tf, D_MODEL), lambda e, j, *_: (e, j, 0))],
        out_specs=[any_spec, any_spec],
        scratch_shapes=[pltpu.VMEM((2, buf_rows * ROW_TILE, LANES), F32),
                        pltpu.VMEM((2, buf_rows * ROW_TILE, LANES), F32),
                        pltpu.VMEM((rows, D_MODEL), BF16),
                        pltpu.VMEM((rows, LANES), F32),
                        pltpu.VMEM((rows, D_MODEL), F32),
                        pltpu.SemaphoreType.DMA((2,)),
                        pltpu.SemaphoreType.DMA((2,))])
    return pl.pallas_call(
        functools.partial(_moe_kernel, capc=capc, caps=caps),
        grid_spec=grid_spec,
        out_shape=[jax.ShapeDtypeStruct((zc_rows * ROW_TILE, LANES), F32),
                   jax.ShapeDtypeStruct((zs_rows * ROW_TILE, LANES), F32)],
        compiler_params=_cparams(("arbitrary", "arbitrary")),
        name="expert_ffn",
    )(idxc, idxs, qc, qs, hc, hs, wrx, w_gate, w_up, w_down)


PAIR_FIRST, PAIR_LAST, PAIR_VALID = 1, 2, 4


def _combine_kernel(chunk_ref, blk_ref, flag_ref, z_ref, span_ref, x1_ref, mod_ref, g_ref, o_ref, acc_ref):
    i = pl.program_id(0)
    flag = flag_ref[i]

    @pl.when((flag & PAIR_FIRST) != 0)
    def _():
        acc_ref[...] = jnp.zeros_like(acc_ref)

    @pl.when((flag & PAIR_VALID) != 0)
    def _():
        row = (chunk_ref[i] * ROW_CHUNK + lax.broadcasted_iota(I32, (TOK_BLOCK, ROW_CHUNK), 1)).astype(F32)
        first_row = span_ref[:, 0:1]
        end_row = span_ref[:, 1:2]
        onehot = jnp.where((row >= first_row) & (row < end_row), 1.0, 0.0).astype(BF16)
        y = jnp.concatenate([z_ref[pl.ds(kc, ROW_CHUNK, stride=ROW_TILE), :] for kc in range(ROW_TILE)], axis=1)
        acc_ref[...] += _dot(onehot, y.astype(BF16))

    @pl.when((flag & PAIR_LAST) != 0)
    def _():
        gt2 = mod_ref[0][:, 5 * D_MODEL:6 * D_MODEL]
        o_ref[...] = x1_ref[...] + gt2 * _rms(acc_ref[...], g_ref[...], EPS)


def _combine(pairs, z, span, x1, mod3, mod_row, g_post_ffn):
    chunk, blk, flag = pairs
    t = x1.shape[0]
    tb = TOK_BLOCK
    grid_spec = pltpu.PrefetchScalarGridSpec(
        num_scalar_prefetch=3,
        grid=(chunk.shape[0],),
        in_specs=[pl.BlockSpec((ROW_CHUNK * ROW_TILE, LANES), lambda i, c, b, f: (c[i], 0)),
                  pl.BlockSpec((tb, LANES), lambda i, c, b, f: (b[i], 0)),
                  pl.BlockSpec((tb, D_MODEL), lambda i, c, b, f: (b[i], 0)),
                  pl.BlockSpec((1, 1, N_MOD * D_MODEL), lambda i, c, b, f: (mod_row(b[i], tb), 0, 0)),
                  pl.BlockSpec((1, D_MODEL), lambda i, c, b, f: (0, 0))],
        out_specs=pl.BlockSpec((tb, D_MODEL), lambda i, c, b, f: (b[i], 0)),
        scratch_shapes=[pltpu.VMEM((tb, D_MODEL), F32)])
    return pl.pallas_call(
        _combine_kernel,
        grid_spec=grid_spec,
        out_shape=jax.ShapeDtypeStruct((t, D_MODEL), F32),
        compiler_params=_cparams(("arbitrary",)),
        name="combine",
    )(chunk, blk, flag, z, span, x1, mod3, g_post_ffn)


def _rope_tables(seq):
    half = HEAD_DIM // 4
    freqs = ROPE_THETA ** (-np.arange(half, dtype=np.float64) / half)
    s = np.arange(seq)
    row = (s // GRID_W)[:, None] * freqs[None, :]
    col = (s % GRID_W)[:, None] * freqs[None, :]
    ang = np.concatenate([row, row, col, col], axis=1)
    ang = np.tile(ang, (1, QK_W // HEAD_DIM))
    lane = np.arange(QK_W)
    sign = np.where((lane % 32) < 16, -1.0, 1.0)[None, :]
    return (jnp.asarray(np.cos(ang), dtype=F32), jnp.asarray(np.sin(ang) * sign, dtype=F32))


def _slot_tile_ranges(offs, cap, n_tok):
    nt = n_tok // LANES
    lo, hi = offs[:, :nt], offs[:, 1:nt + 1]
    base = jnp.arange(cap // LANES, dtype=I32) * LANES
    clo = jnp.sum(hi[:, None, :] <= base[None, :, None], axis=-1).astype(I32)
    chi = jnp.sum(lo[:, None, :] < (base + LANES)[None, :, None], axis=-1).astype(I32)
    return clo, chi


def _combine_pairs(rows, n_tok):
    step = TOK_BLOCK // LANES
    nb = n_tok // TOK_BLOCK
    nc = 2 * n_tok // ROW_CHUNK
    starts = rows[0, 0:nb * step + 1:step]
    lo, hi = starts[:-1], starts[1:]
    c_lo = jnp.minimum(lo // ROW_CHUNK, nc - 1)
    c_hi = jnp.maximum((hi + ROW_CHUNK - 1) // ROW_CHUNK, c_lo + 1)
    n = c_hi - c_lo
    ends = jnp.cumsum(n)
    begins = ends - n
    i = jnp.arange(nb + nc, dtype=I32)
    valid = i < ends[-1]
    blk = jnp.minimum(jnp.sum(ends[None, :] <= i[:, None], axis=1), nb - 1).astype(I32)
    onehot = blk[:, None] == jnp.arange(nb, dtype=I32)[None, :]
    table = jnp.stack([begins, n, c_lo], axis=1)
    picked = jnp.sum(jnp.where(onehot[:, :, None], table[None, :, :], 0), axis=1)
    begin_i, n_i, c_lo_i = picked[:, 0], picked[:, 1], picked[:, 2]
    off = jnp.minimum(i - begin_i, n_i - 1)
    chunk = (c_lo_i + off).astype(I32)
    first = valid & (i == begin_i)
    last = valid & (i == begin_i + n_i - 1)
    flag = (first * PAIR_FIRST + last * PAIR_LAST + valid * PAIR_VALID).astype(I32)
    return chunk, blk, flag


def kernel(x_prompt, x_sample, c, cache_k, cache_v, c_ctx, w_mod, b_mod, g_pre_mix, g_post_mix, g_pre_ffn, g_post_ffn, w_in, lam_q1, lam_k1, lam_q2, lam_k2, g_subln, w_proj_attn, w_proj_fourier, w_out, w_router, w_gate, w_up, w_down):
    assert w_mod.shape[0] == 1
    lam_init = 0.8 - 0.6 * math.exp(-0.3 * 0)
    bp, sp, _ = x_prompt.shape
    bs, ss, _ = x_sample.shape

    cond8 = jnp.concatenate([c_ctx[None, :], c, jnp.zeros((8 - 1 - bs, D_MODEL), F32)], axis=0)
    mod3 = _modulation(cond8, w_mod[0], b_mod).reshape(8, 1, N_MOD * D_MODEL)

    w_in_b = w_in[0].astype(BF16)
    wpa = w_proj_attn[0].astype(BF16)
    wpf = w_proj_fourier[0].astype(BF16)
    wout = w_out[0].astype(BF16)
    wr = w_router[0].astype(BF16)
    wrx = jnp.concatenate([wr, jnp.zeros((D_MODEL, LANES - N_EXPERTS), BF16)], axis=1)
    lam_p = jnp.concatenate([lam_q1, lam_k1, lam_q2, lam_k2], axis=0)

    groups = []
    for x, seq, positional, ctx in ((x_prompt, sp, False, None),
                                    (x_sample, ss, True, (cache_k, cache_v))):
        nb = x.shape[0]
        t = nb * seq
        x2d = x.reshape(t, D_MODEL)
        if positional:
            mod_row = lambda i, tm, seq=seq: 1 + (i * tm) // seq
        else:
            mod_row = lambda i, tm: 0
        pre = _pre_mixer(x2d, mod3, mod_row, g_pre_mix, w_in_b,
                         _rope_tables(seq) if positional else None, seq, write_cache=not positional)
        q, k, v, f, ga, gf = pre[:6]
        o = _attention(lam_p, g_subln, q, k, v, ctx, seq, lam_init)
        fm = f if seq == ROW_BLOCK else _fourier(f, seq)
        x1, h2t, aff_t = _post_mixer(o, fm, ga, gf, x2d, mod3, mod_row, g_post_mix, g_pre_ffn,
                                     wpa, wpf, wout, wrx)
        cap = 2 * t // N_EXPERTS
        assert t <= 1 << TOKEN_BITS
        posm, pack, span, offs, rows = _route(aff_t, cap)
        clo, chi = _slot_tile_ranges(offs, cap, t)
        idx, qslot = _slot_lists(clo, chi, posm, pack, cap)
        groups.append(dict(x1=x1, h2t=h2t, idx=idx, qslot=qslot, span=span, pairs=_combine_pairs(rows, t),
                           mod_row=mod_row, cache=pre[6:], shape=x.shape))

    gc, gs_ = groups
    zc, zs = _moe(gc["idx"], gs_["idx"], gc["qslot"], gs_["qslot"], gc["h2t"], gs_["h2t"], wrx,
                  w_gate[0], w_up[0], w_down[0])
    outs = []
    for g, z in ((gc, zc), (gs_, zs)):
        out = _combine(g["pairs"], z, g["span"], g["x1"], mod3, g["mod_row"], g_post_ffn)
        outs.append(out.reshape(g["shape"]))
    new_k, new_v = gc["cache"]
    return (outs[0], outs[1], new_k, new_v)
```

```python
import functools
import math

import numpy as np
import jax
import jax.numpy as jnp
from jax import lax
from jax.experimental import pallas as pl
from jax.experimental.pallas import tpu as pltpu

F32 = jnp.float32
BF16 = jnp.bfloat16
I32 = jnp.int32

D_MODEL = 1024
N_HEADS = 6
HEAD_DIM = 64
V_DIM = 128
QK_W = 768
FOUR_W = 256
FOUR_G = 64
IN_W = 4608
N_EXPERTS = 16
D_FF = 2816
N_MOD = 6
GRID_W = 64
ROPE_THETA = 10000.0
EPS = 1e-6
SUBLN_EPS = 1e-5

LANES = 128
ROW_BLOCK = 256
POST_BLOCK = 1024
POST_SUB = 256
TOK_BLOCK = 512
ROW_CHUNK = 256
FF_TILE = 256
FF_STEPS = D_FF // FF_TILE
ROW_TILE = D_MODEL // LANES
TOKEN_BITS = 13
VMEM_LIMIT = 56 * 1024 * 1024


def _cparams(sem):
    return pltpu.CompilerParams(dimension_semantics=sem, vmem_limit_bytes=VMEM_LIMIT)


def _dot(a, b):
    return jnp.dot(a, b, preferred_element_type=F32)


def _rms(x, g, eps):
    return x * lax.rsqrt(jnp.mean(x * x, axis=-1, keepdims=True) + eps) * g


def _whole_vmem():
    return pl.BlockSpec(memory_space=pltpu.MemorySpace.VMEM)


def _mod_kernel(c_ref, w_ref, b_ref, o_ref):
    c = c_ref[...]
    s = c * jax.nn.sigmoid(c)
    o_ref[...] = _dot(s.astype(BF16), w_ref[...].astype(BF16)) + b_ref[...]


def _modulation(cond8, w_mod, b_mod):
    tn = 1024
    n = N_MOD * D_MODEL
    return pl.pallas_call(
        _mod_kernel,
        grid=(n // tn,),
        in_specs=[pl.BlockSpec((8, D_MODEL), lambda j: (0, 0)),
                  pl.BlockSpec((D_MODEL, tn), lambda j: (0, j)),
                  pl.BlockSpec((1, tn), lambda j: (0, j))],
        out_specs=pl.BlockSpec((8, tn), lambda j: (0, j)),
        out_shape=jax.ShapeDtypeStruct((8, n), F32),
        compiler_params=_cparams(("arbitrary",)),
        name="modulation",
    )(cond8, w_mod, b_mod)


def _rope(z, cos, sin_signed, first_half):
    fwd = pltpu.roll(z, QK_W - 16, axis=1)
    bwd = pltpu.roll(z, 16, axis=1)
    return z * cos + jnp.where(first_half, fwd, bwd) * sin_signed


def _pre_kernel(*refs, positional, write_cache, fuse_fourier):
    it = iter(refs)
    x_ref, mod_ref, g_ref, w_ref = next(it), next(it), next(it), next(it)
    if positional:
        cos_ref, sin_ref = next(it), next(it)
    if fuse_fourier:
        dft_refs = [next(it) for _ in range(4)]
    q_ref, k_ref, v_ref, f_ref, ga_ref, gf_ref = (next(it) for _ in range(6))
    if write_cache:
        kc_ref, vc_ref = next(it), next(it)

    m = mod_ref[0]
    sh1 = m[:, 0:D_MODEL]
    sc1 = m[:, D_MODEL:2 * D_MODEL]
    h = _rms(x_ref[...], g_ref[...], EPS) * (1.0 + sc1) + sh1
    hb = h.astype(BF16)

    def proj(lo, hi):
        return _dot(hb, w_ref[:, lo:hi])

    zq = proj(0, QK_W)
    zk = proj(QK_W, 2 * QK_W)
    zv = proj(2 * QK_W, 3 * QK_W)
    if positional:
        lane = lax.broadcasted_iota(I32, (1, QK_W), 1)
        first_half = (lane % 32) < 16
        cos = cos_ref[...]
        sin_signed = sin_ref[...]
        zq = _rope(zq, cos, sin_signed, first_half)
        zk = _rope(zk, cos, sin_signed, first_half)
    q_ref[...] = zq.astype(BF16)
    k_ref[...] = zk.astype(BF16)
    v_ref[...] = zv.astype(BF16)
    if write_cache:
        for hd in range(N_HEADS):
            kc_ref[0, 0, hd] = zk[:, hd * V_DIM:(hd + 1) * V_DIM]
            vc_ref[0, 0, hd] = zv[:, hd * V_DIM:(hd + 1) * V_DIM]
    f0 = 3 * QK_W
    f = proj(f0, f0 + FOUR_W)
    f_ref[...] = _dft_real(f, *dft_refs) if fuse_fourier else f
    ga_ref[...] = jax.nn.sigmoid(proj(f0 + FOUR_W, f0 + FOUR_W + D_MODEL)).astype(BF16)
    gf_ref[...] = jax.nn.sigmoid(proj(f0 + FOUR_W + D_MODEL, IN_W)).astype(BF16)


def _pre_mixer(x2d, mod3, mod_row, g_pre, w_in_b, rope_tabs, seq, write_cache):
    t = x2d.shape[0]
    tm = ROW_BLOCK
    positional = rope_tabs is not None
    fuse_fourier = seq == tm
    blocks_per_seq = seq // tm
    row = lambda i: (i, 0)
    in_specs = [pl.BlockSpec((tm, D_MODEL), row),
                pl.BlockSpec((1, 1, N_MOD * D_MODEL), lambda i: (mod_row(i, tm), 0, 0)),
                pl.BlockSpec((1, D_MODEL), lambda i: (0, 0)),
                _whole_vmem()]
    args = [x2d, mod3, g_pre, w_in_b]
    if positional:
        in_specs += [pl.BlockSpec((tm, QK_W), lambda i: (i % blocks_per_seq, 0))] * 2
        args += list(rope_tabs)
    if fuse_fourier:
        consts = _dft_consts(seq)
        in_specs += [pl.BlockSpec(c.shape, lambda i: (0, 0)) for c in consts]
        args += list(consts)
    out_shape = [jax.ShapeDtypeStruct((t, QK_W), BF16)] * 3 + [
        jax.ShapeDtypeStruct((t, FOUR_W), BF16 if fuse_fourier else F32),
        jax.ShapeDtypeStruct((t, D_MODEL), BF16),
        jax.ShapeDtypeStruct((t, D_MODEL), BF16)]
    out_specs = [pl.BlockSpec((tm, QK_W), row)] * 3 + [
        pl.BlockSpec((tm, FOUR_W), row),
        pl.BlockSpec((tm, D_MODEL), row),
        pl.BlockSpec((tm, D_MODEL), row)]
    if write_cache:
        assert seq == tm
        nb = t // seq
        cshape = jax.ShapeDtypeStruct((nb, 1, N_HEADS, seq, V_DIM), F32)
        cspec = pl.BlockSpec((1, 1, N_HEADS, seq, V_DIM), lambda i: (i, 0, 0, 0, 0))
        out_shape += [cshape, cshape]
        out_specs += [cspec, cspec]
    return pl.pallas_call(
        functools.partial(_pre_kernel, positional=positional, write_cache=write_cache,
                          fuse_fourier=fuse_fourier),
        grid=(t // tm,),
        in_specs=in_specs,
        out_specs=out_specs,
        out_shape=out_shape,
        compiler_params=_cparams(("arbitrary",)),
        name="pre_mixer",
    )(*args)


def _attn_kernel(*refs, lam_init, has_ctx):
    it = iter(refs)
    lam_ref, gs_ref, q_ref, k_ref, v_ref = (next(it) for _ in range(5))
    if has_ctx:
        ck_ref, cv_ref = next(it), next(it)
    o_ref = next(it)

    lp = lam_ref[...]
    s1 = jnp.sum(lp[0:1] * lp[1:2], axis=-1, keepdims=True)
    s2 = jnp.sum(lp[2:3] * lp[3:4], axis=-1, keepdims=True)
    lam = jnp.exp(s1) - jnp.exp(s2) + lam_init
    lane = lax.broadcasted_iota(I32, (1, V_DIM), 1)
    comp1 = lane < HEAD_DIM
    scale = jnp.asarray(HEAD_DIM ** -0.5, BF16)
    nt = (((1,), (1,)), ((), ()))

    def attend(qc, k, v_ones):
        s = lax.dot_general(qc, k, nt, preferred_element_type=F32)
        ex = jnp.exp(s - jnp.max(s, axis=-1, keepdims=True)).astype(BF16)
        ov = _dot(ex, v_ones)
        return ov[:, 0:V_DIM] / ov[:, V_DIM:2 * V_DIM]

    for hd in range(N_HEADS):
        sl = slice(hd * V_DIM, (hd + 1) * V_DIM)
        q = q_ref[:, sl] * scale
        k = k_ref[:, sl]
        v = v_ref[:, sl]
        if has_ctx:
            k = jnp.concatenate([ck_ref[0, 0, hd].astype(BF16), k], axis=0)
            v = jnp.concatenate([cv_ref[0, 0, hd].astype(BF16), v], axis=0)
        v_ones = jnp.concatenate([v, jnp.ones_like(v)], axis=1)
        zero = jnp.zeros_like(q)
        o = attend(jnp.where(comp1, q, zero), k, v_ones) - lam * attend(jnp.where(comp1, zero, q), k, v_ones)
        o = _rms(o, gs_ref[...], SUBLN_EPS) * (1.0 - lam_init)
        o_ref[:, sl] = o.astype(BF16)


def _attention(lam_p, g_subln, q, k, v, ctx, seq, lam_init):
    t = q.shape[0]
    tq = ROW_BLOCK
    qb = seq // tq
    has_ctx = ctx is not None
    in_specs = [pl.BlockSpec((4, HEAD_DIM), lambda b, i: (0, 0)),
                pl.BlockSpec((1, V_DIM), lambda b, i: (0, 0)),
                pl.BlockSpec((tq, QK_W), lambda b, i: (b * qb + i, 0)),
                pl.BlockSpec((seq, QK_W), lambda b, i: (b, 0)),
                pl.BlockSpec((seq, QK_W), lambda b, i: (b, 0))]
    args = [lam_p, g_subln, q, k, v]
    if has_ctx:
        past = ctx[0].shape[3]
        cspec = pl.BlockSpec((1, 1, N_HEADS, past, V_DIM), lambda b, i: (b, 0, 0, 0, 0))
        in_specs += [cspec, cspec]
        args += list(ctx)
    return pl.pallas_call(
        functools.partial(_attn_kernel, lam_init=lam_init, has_ctx=has_ctx),
        grid=(t // seq, qb),
        in_specs=in_specs,
        out_specs=pl.BlockSpec((tq, QK_W), lambda b, i: (b * qb + i, 0)),
        out_shape=jax.ShapeDtypeStruct((t, QK_W), BF16),
        compiler_params=_cparams(("arbitrary", "arbitrary")),
        name="diff_attention",
    )(*args)


def _dft_real(f, bc_ref, bs_ref, cs_ref, ss_ref):
    fb = f.astype(BF16)
    u = _dot(fb, bc_ref[...].astype(BF16)).astype(BF16)
    w = _dot(fb, bs_ref[...].astype(BF16)).astype(BF16)
    return (_dot(cs_ref[...].astype(BF16), u) - _dot(ss_ref[...].astype(BF16), w)).astype(BF16)


def _fourier_kernel(f_ref, bc_ref, bs_ref, cs_ref, ss_ref, o_ref):
    o_ref[...] = _dft_real(f_ref[...], bc_ref, bs_ref, cs_ref, ss_ref)


def _dft_consts(seq):
    c = np.arange(FOUR_G)
    ang_c = 2.0 * np.pi * ((c[:, None] * c[None, :]) % FOUR_G) / FOUR_G
    eye = np.eye(FOUR_W // FOUR_G)
    bc = np.kron(eye, np.cos(ang_c)) / math.sqrt(FOUR_G)
    bs = np.kron(eye, np.sin(ang_c)) / math.sqrt(FOUR_G)
    s = np.arange(seq)
    ang_s = 2.0 * np.pi * ((s[:, None] * s[None, :]) % seq) / seq
    cs = np.cos(ang_s) / math.sqrt(seq)
    ss = np.sin(ang_s) / math.sqrt(seq)
    return tuple(jnp.asarray(a, dtype=F32) for a in (bc, bs, cs, ss))


def _fourier(f, seq):
    t = f.shape[0]
    bc, bs, cs, ss = _dft_consts(seq)
    const = lambda b: (0, 0)
    return pl.pallas_call(
        _fourier_kernel,
        grid=(t // seq,),
        in_specs=[pl.BlockSpec((seq, FOUR_W), lambda b: (b, 0)),
                  pl.BlockSpec((FOUR_W, FOUR_W), const),
                  pl.BlockSpec((FOUR_W, FOUR_W), const),
                  pl.BlockSpec((seq, seq), const),
                  pl.BlockSpec((seq, seq), const)],
        out_specs=pl.BlockSpec((seq, FOUR_W), lambda b: (b, 0)),
        out_shape=jax.ShapeDtypeStruct((t, FOUR_W), BF16),
        compiler_params=_cparams(("arbitrary",)),
        name="fourier_mix",
    )(f, bc, bs, cs, ss)


def _post_kernel(o_ref, fm_ref, ga_ref, gf_ref, x_ref, mod_ref, gpost_ref, gffn_ref,
                 wpa_ref, wpf_ref, wout_ref, wrx_ref,
                 x1_ref, h2t_ref, afft_ref):
    m = mod_ref[0]
    gt1 = m[:, 2 * D_MODEL:3 * D_MODEL]
    sh2 = m[:, 3 * D_MODEL:4 * D_MODEL]
    sc2 = m[:, 4 * D_MODEL:5 * D_MODEL]
    subs = [slice(r0, r0 + POST_SUB) for r0 in range(0, o_ref.shape[0], POST_SUB)]
    ab = [(_dot(o_ref[rs, :], wpa_ref[...]), _dot(fm_ref[rs, :], wpf_ref[...])) for rs in subs]
    merged = [(ga_ref[rs, :] * a + gf_ref[rs, :] * b).astype(BF16) for rs, (a, b) in zip(subs, ab)]
    ys = [_dot(mg, wout_ref[...]) for mg in merged]
    h2s = []
    for rs, y in zip(subs, ys):
        x1 = x_ref[rs, :] + gt1 * _rms(y, gpost_ref[...], EPS)
        x1_ref[rs, :] = x1
        h2s.append(_rms(x1, gffn_ref[...], EPS) * (1.0 + sc2) + sh2)
    logits = [_dot(h2.astype(BF16), wrx_ref[...]) for h2 in h2s]
    for rs, h2, lg in zip(subs, h2s, logits):
        lt = lg.T[0:N_EXPERTS]
        et = jnp.exp(lt - jnp.max(lt, axis=0, keepdims=True))
        aff = et / jnp.sum(et, axis=0, keepdims=True)
        for u in range(POST_SUB // LANES):
            afft_ref[:, rs.start // LANES + u, :] = aff[:, u * LANES:(u + 1) * LANES]
        for kc in range(ROW_TILE):
            h2t_ref[pl.ds(rs.start * ROW_TILE + kc, POST_SUB, stride=ROW_TILE), :] = h2[:, kc * LANES:(kc + 1) * LANES]


def _post_mixer(o, fm, ga, gf, x2d, mod3, mod_row, g_post, g_ffn, wpa, wpf, wout, wrx):
    t = x2d.shape[0]
    tm = POST_BLOCK
    row = lambda i: (i, 0)
    const = lambda i: (0, 0)
    return pl.pallas_call(
        _post_kernel,
        grid=(t // tm,),
        in_specs=[pl.BlockSpec((tm, QK_W), row),
                  pl.BlockSpec((tm, FOUR_W), row),
                  pl.BlockSpec((tm, D_MODEL), row),
                  pl.BlockSpec((tm, D_MODEL), row),
                  pl.BlockSpec((tm, D_MODEL), row),
                  pl.BlockSpec((1, 1, N_MOD * D_MODEL), lambda i: (mod_row(i, tm), 0, 0)),
                  pl.BlockSpec((1, D_MODEL), const),
                  pl.BlockSpec((1, D_MODEL), const),
                  pl.BlockSpec((QK_W, D_MODEL), const),
                  pl.BlockSpec((FOUR_W, D_MODEL), const),
                  pl.BlockSpec((D_MODEL, D_MODEL), const),
                  pl.BlockSpec((D_MODEL, LANES), const)],
        out_specs=[pl.BlockSpec((tm, D_MODEL), row),
                   pl.BlockSpec((tm * ROW_TILE, LANES), row),
                   pl.BlockSpec((N_EXPERTS, tm // LANES, LANES), lambda i: (0, i, 0))],
        out_shape=[jax.ShapeDtypeStruct((t, D_MODEL), F32),
                   jax.ShapeDtypeStruct((t * ROW_TILE, LANES), F32),
                   jax.ShapeDtypeStruct((N_EXPERTS, t // LANES, LANES), F32)],
        compiler_params=_cparams(("arbitrary",)),
        name="post_mixer",
    )(o, fm, ga, gf, x2d, mod3, g_post, g_ffn, wpa, wpf, wout, wrx)


RANGE_ROWS = 8


def _route_kernel(aff_ref, posm_ref, pack_ref, span_ref, ranges_ref, rows_ref, *, cap, n_tok):
    aff = aff_ref[...]
    nt = n_tok // LANES
    capf = float(cap)

    def count_ge(v):
        return jnp.sum(jnp.where(aff >= v, 1.0, 0.0), axis=(1, 2), keepdims=True)

    def search(i, thr):
        cand = thr | jnp.left_shift(jnp.int32(1), 30 - i)
        return jnp.where(count_ge(pltpu.bitcast(cand, F32)) >= capf, cand, thr)

    thr = lax.fori_loop(0, 31, search, jnp.zeros((N_EXPERTS, 1, 1), I32))
    lo = pltpu.bitcast(thr, F32)
    hi = pltpu.bitcast(thr + 1, F32)

    def refine(i, c):
        lo, hi = c
        mid = lo + (hi - lo) * 0.5
        ok = count_ge(mid) >= capf
        return jnp.where(ok, mid, lo), jnp.where(ok, hi, mid)

    lo, hi = lax.fori_loop(0, 12, refine, (lo, hi))
    gt = aff >= hi
    eq = (aff >= lo) & (aff < hi)
    n_tie = capf - jnp.sum(jnp.where(gt, 1.0, 0.0), axis=(1, 2), keepdims=True)

    sq0 = lax.broadcasted_iota(I32, (LANES, LANES), 0)
    sq1 = lax.broadcasted_iota(I32, (LANES, LANES), 1)
    along_total = jnp.concatenate([jnp.where(sq0 <= sq1, 1.0, 0.0), jnp.ones((LANES, LANES), F32)],
                                  axis=1).astype(BF16)
    m = N_EXPERTS * nt
    r0 = lax.broadcasted_iota(I32, (m, m), 0)
    r1 = lax.broadcasted_iota(I32, (m, m), 1)
    earlier = jnp.where((r0 // nt == r1 // nt) & (r1 < r0), 1.0, 0.0).astype(BF16)
    lane = lax.broadcasted_iota(I32, (1, LANES), 1)
    token = lax.broadcasted_iota(I32, (nt, LANES), 0) * LANES + lane

    def tile_counts(x):
        both = _dot(x.reshape(m, LANES).astype(BF16), along_total)
        total = both[:, LANES:]
        before = _dot(earlier, total.astype(BF16))
        shape = (N_EXPERTS, nt, LANES)
        return both[:, :LANES].reshape(shape), total.reshape(shape), before.reshape(shape)

    eq_f = jnp.where(eq, 1.0, 0.0)
    eq_along, _, eq_before = tile_counts(eq_f)
    sel = jnp.where(gt, 1.0, jnp.where(eq_along + eq_before <= n_tie, eq_f, 0.0))
    sel_along, sel_total, sel_before = tile_counts(sel)
    posm_ref[...] = jnp.where(sel > 0.5, sel_along + sel_before - sel, -1.0).astype(I32)

    cnt = jnp.sum(sel, axis=0)
    rows_before = jnp.sum(sel_before, axis=0)
    tok_start = _dot(cnt.astype(BF16), along_total[:, :LANES]) - cnt + rows_before
    k = jnp.zeros((nt, LANES), F32)
    slots_before = jnp.zeros((nt, LANES), F32)
    slots_here = jnp.zeros((nt, LANES), F32)
    for e in range(N_EXPERTS):
        pack_ref[e] = (tok_start + k).astype(I32) * (1 << TOKEN_BITS) + token
        k = k + sel[e]
        slots_before = jnp.where(lane == e, sel_before[e], slots_before)
        slots_here = jnp.where(lane == e, sel_total[e], slots_here)

    eye = jnp.where(sq0 == sq1, 1.0, 0.0).astype(BF16)
    tok_end = tok_start + cnt
    parts = []
    for v in (tok_start, tok_end):
        high = jnp.floor(v * (1.0 / LANES))
        parts += [high, v - high * LANES]
    for c in range(nt):
        rows4 = jnp.zeros((LANES, LANES), F32)
        for r, part in enumerate(parts):
            rows4 = jnp.where(sq0 == r, jnp.broadcast_to(part[c:c + 1, :], (LANES, LANES)), rows4)
        span_ref[c * LANES:(c + 1) * LANES, :] = lax.dot_general(
            eye, rows4.astype(BF16), (((1,), (1,)), ((), ())), preferred_element_type=F32)

    ranges_ref[...] = jnp.zeros_like(ranges_ref)
    for s in range(cap // LANES):
        done = jnp.where(slots_before + slots_here <= float(s * LANES), 1.0, 0.0)
        begun = jnp.where(slots_before < float((s + 1) * LANES), 1.0, 0.0)
        ranges_ref[s:s + 1, :] = jnp.sum(done, axis=0, keepdims=True).astype(I32)
        ranges_ref[RANGE_ROWS + s:RANGE_ROWS + s + 1, :] = jnp.sum(begun, axis=0, keepdims=True).astype(I32)
    rows_ref[...] = rows_before.astype(I32)


def _route(aff, cap):
    n_e, nt, _ = aff.shape
    n_tok = nt * LANES
    assert cap // LANES <= RANGE_ROWS
    return pl.pallas_call(
        functools.partial(_route_kernel, cap=cap, n_tok=n_tok),
        out_shape=[jax.ShapeDtypeStruct((n_e, nt, LANES), I32),
                   jax.ShapeDtypeStruct((n_e, nt, LANES), I32),
                   jax.ShapeDtypeStruct((n_tok, LANES), F32),
                   jax.ShapeDtypeStruct((2 * RANGE_ROWS, LANES), I32),
                   jax.ShapeDtypeStruct((nt, LANES), I32)],
        compiler_params=pltpu.CompilerParams(vmem_limit_bytes=VMEM_LIMIT),
        name="route",
    )(aff)


def _slots_kernel(ranges_ref, posm_ref, pack_ref, idx_ref, qslot_ref):
    e = pl.program_id(0)
    sub = lax.broadcasted_iota(I32, (LANES, LANES), 0)
    eye = sub == lax.broadcasted_iota(I32, (LANES, LANES), 1)

    def as_row(part):
        col = jnp.sum(part.astype(F32), axis=1, keepdims=True)
        return jnp.sum(jnp.where(eye, col, 0.0), axis=0, keepdims=True).astype(I32)

    for s in range(idx_ref.shape[1]):
        slot = sub + s * LANES

        def body(c, acc, slot=slot):
            hit = posm_ref[e, pl.ds(c, 1), :] == slot
            return acc + jnp.where(hit, pack_ref[e, pl.ds(c, 1), :], 0)

        acc = lax.fori_loop(ranges_ref[s, e], ranges_ref[RANGE_ROWS + s, e], body,
                            jnp.zeros((LANES, LANES), I32))
        idx_ref[0, s:s + 1, :] = as_row(acc & ((1 << TOKEN_BITS) - 1)) * ROW_TILE
        qslot_ref[0, s:s + 1, :] = as_row(lax.shift_right_logical(acc, TOKEN_BITS)) * ROW_TILE


def _slot_lists(ranges, posm4, qdst4, cap):
    ns = cap // LANES
    grid_spec = pltpu.PrefetchScalarGridSpec(
        num_scalar_prefetch=1,
        grid=(N_EXPERTS,),
        in_specs=[_whole_vmem(), _whole_vmem()],
        out_specs=[pl.BlockSpec((1, ns, LANES), lambda e, *_: (e, 0, 0)),
                   pl.BlockSpec((1, ns, LANES), lambda e, *_: (e, 0, 0))])
    idx, qslot = pl.pallas_call(
        _slots_kernel,
        grid_spec=grid_spec,
        out_shape=[jax.ShapeDtypeStruct((N_EXPERTS, ns, LANES), I32),
                   jax.ShapeDtypeStruct((N_EXPERTS, ns, LANES), I32)],
        compiler_params=_cparams(("arbitrary",)),
        name="slot_lists",
    )(ranges, posm4, qdst4)
    return idx.reshape(N_EXPERTS, cap), qslot.reshape(N_EXPERTS, cap)


def _moe_kernel(idxc_ref, idxs_ref, qc_ref, qs_ref,
                hc_ref, hs_ref, wr_ref, wg_ref, wu_ref, wd_ref, zc_ref, zs_ref,
                xbuf, ybuf, xb_ref, gate_ref, acc_ref, gsem, ssem, *, capc, caps):
    e = pl.program_id(0)
    j = pl.program_id(1)
    n_e = pl.num_programs(0)
    n_j = FF_STEPS
    slot = e % 2
    other = 1 - slot
    rows = capc + caps
    gc, gs = _per_step(capc), _per_step(caps)
    groups = ((hc_ref, idxc_ref, zc_ref, qc_ref, gc, 0), (hs_ref, idxs_ref, zs_ref, qs_ref, gs, gc * n_j))

    def tile(ref, first_sublane):
        return ref.at[pl.ds(pl.multiple_of(first_sublane, ROW_TILE), ROW_TILE), :]

    def gather(ex, sl, step, i, group):
        h_ref, idx_ref, _, _, per_step, base = group
        p = step * per_step + i
        src = tile(h_ref, idx_ref[ex * (per_step * n_j) + p])
        pltpu.make_async_copy(src, tile(xbuf.at[sl], (base + p) * ROW_TILE), gsem.at[sl]).start()

    def scatter(table_row, sl, step, i, group):
        _, _, z_ref, q_ref, per_step, base = group
        p = step * per_step + i
        dst = tile(z_ref, q_ref[table_row * (per_step * n_j) + p])
        pltpu.make_async_copy(tile(ybuf.at[sl], (base + p) * ROW_TILE), dst, ssem.at[sl]).start()

    def all_steps(fn):
        for group in groups:
            def body(p, carry, group=group):
                fn(p, group)
                return carry
            lax.fori_loop(0, group[4] * n_j, body, 0, unroll=8)

    def wait_all(buf, sem, sl):
        pltpu.make_async_copy(buf.at[sl], buf.at[sl], sem.at[sl]).wait()

    @pl.when((e == 0) & (j == 0))
    def _():
        ybuf[...] = jnp.zeros_like(ybuf)
        all_steps(lambda p, group: gather(0, 0, 0, p, group))

    @pl.when(j == 0)
    def _():
        wait_all(xbuf, gsem, slot)
        for base, n, dst in ((0, capc, 0), (gc * n_j, caps, capc)):
            for kc in range(ROW_TILE):
                tiles = xbuf[slot, pl.ds(base * ROW_TILE + kc, n, stride=ROW_TILE), :]
                xb_ref[dst:dst + n, kc * LANES:(kc + 1) * LANES] = tiles.astype(BF16)
        acc_ref[...] = jnp.zeros_like(acc_ref)
        logits = _dot(xb_ref[...], wr_ref[...])
        lane = lax.broadcasted_iota(I32, (1, LANES), 1)
        is_expert = lane < N_EXPERTS
        ex = jnp.exp(logits - jnp.max(jnp.where(is_expert, logits, -jnp.inf), axis=-1, keepdims=True))
        mine = jnp.sum(jnp.where(lane == e, ex, 0.0), axis=-1, keepdims=True)
        gate = mine / jnp.sum(jnp.where(is_expert, ex, 0.0), axis=-1, keepdims=True)
        gate_ref[...] = jnp.broadcast_to(gate, gate_ref.shape)

    nxt = jnp.minimum(e + 1, n_e - 1)
    for group in groups:
        for i in range(group[4]):
            gather(nxt, other, j, i, group)
            scatter(e, other, j, i, group)

    x = xb_ref[...]
    g = _dot(x, wg_ref[0].astype(BF16))
    u = _dot(x, wu_ref[0].astype(BF16))
    hid = (g * jax.nn.sigmoid(g) * u).astype(BF16)
    acc_ref[...] += _dot(hid, wd_ref[0].astype(BF16))

    @pl.when(j == n_j - 1)
    def _():
        @pl.when(e >= 1)
        def _():
            wait_all(ybuf, ssem, slot)

        gate = gate_ref[...]
        for base, n, src in ((0, capc, 0), (gc * n_j, caps, capc)):
            for kc in range(ROW_TILE):
                y = acc_ref[src:src + n, kc * LANES:(kc + 1) * LANES] * gate[src:src + n]
                ybuf[slot, pl.ds(base * ROW_TILE + kc, n, stride=ROW_TILE), :] = y

        @pl.when(e == n_e - 1)
        def _():
            all_steps(lambda p, group: scatter(e + 1, slot, 0, p, group))
            wait_all(ybuf, ssem, other)
            wait_all(ybuf, ssem, slot)
            wait_all(xbuf, gsem, other)


def _per_step(cap):
    return -(-cap // FF_STEPS)


def _copy_tables(idx, qslot, n_rows):
    n_e, cap = idx.shape
    padded = _per_step(cap) * FF_STEPS
    n_pad = padded - cap
    idx_p = jnp.concatenate([idx, jnp.zeros((n_e, n_pad), I32)], axis=1)
    spare = n_rows + jnp.arange(padded + n_e * n_pad, dtype=I32)
    lead = spare[:padded][None, :]
    pad_rows = spare[padded:].reshape(n_e, n_pad)
    q_p = jnp.concatenate([lead, jnp.concatenate([qslot // ROW_TILE, pad_rows], axis=1)], axis=0) * ROW_TILE
    return idx_p.reshape(-1), q_p.reshape(-1), n_rows + padded + n_e * n_pad


def _moe(idxc, idxs, qc, qs, hc, hs, wrx, w_gate, w_up, w_down):
    capc, caps = idxc.shape[1], idxs.shape[1]
    rows = capc + caps
    tf = FF_TILE
    idxc, qc, zc_rows = _copy_tables(idxc, qc, N_EXPERTS * capc)
    idxs, qs, zs_rows = _copy_tables(idxs, qs, N_EXPERTS * caps)
    buf_rows = (_per_step(capc) + _per_step(caps)) * FF_STEPS
    any_spec = pl.BlockSpec(memory_space=pl.ANY)
    grid_spec = pltpu.PrefetchScalarGridSpec(
        num_scalar_prefetch=4,
        grid=(N_EXPERTS, FF_STEPS),
        in_specs=[any_spec, any_spec,
                  pl.BlockSpec((D_MODEL, LANES), lambda e, j, *_: (0, 0)),
                  pl.BlockSpec((1, D_MODEL, tf), lambda e, j, *_: (e, 0, j)),
                  pl.BlockSpec((1, D_MODEL, tf), lambda e, j, *_: (e, 0, j)),
                  pl.BlockSpec((1, tf, D_MODEL), lambda e, j, *_: (e, j, 0))],
        out_specs=[any_spec, any_spec],
        scratch_shapes=[pltpu.VMEM((2, buf_rows * ROW_TILE, LANES), F32),
                        pltpu.VMEM((2, buf_rows * ROW_TILE, LANES), F32),
                        pltpu.VMEM((rows, D_MODEL), BF16),
                        pltpu.VMEM((rows, LANES), F32),
                        pltpu.VMEM((rows, D_MODEL), F32),
                        pltpu.SemaphoreType.DMA((2,)),
                        pltpu.SemaphoreType.DMA((2,))])
    return pl.pallas_call(
        functools.partial(_moe_kernel, capc=capc, caps=caps),
        grid_spec=grid_spec,
        out_shape=[jax.ShapeDtypeStruct((zc_rows * ROW_TILE, LANES), F32),
                   jax.ShapeDtypeStruct((zs_rows * ROW_TILE, LANES), F32)],
        compiler_params=_cparams(("arbitrary", "arbitrary")),
        name="expert_ffn",
    )(idxc, idxs, qc, qs, hc, hs, wrx, w_gate, w_up, w_down)


PAIR_FIRST, PAIR_LAST, PAIR_VALID = 1, 2, 4


def _combine_kernel(chunk_ref, blk_ref, flag_ref, z_ref, span_ref, x1_ref, mod_ref, g_ref, o_ref, acc_ref):
    i = pl.program_id(0)
    flag = flag_ref[i]

    @pl.when((flag & PAIR_FIRST) != 0)
    def _():
        acc_ref[...] = jnp.zeros_like(acc_ref)

    @pl.when((flag & PAIR_VALID) != 0)
    def _():
        row = (chunk_ref[i] * ROW_CHUNK + lax.broadcasted_iota(I32, (TOK_BLOCK, ROW_CHUNK), 1)).astype(F32)
        first_row = span_ref[:, 0:1] * LANES + span_ref[:, 1:2]
        end_row = span_ref[:, 2:3] * LANES + span_ref[:, 3:4]
        onehot = jnp.where((row >= first_row) & (row < end_row), 1.0, 0.0).astype(BF16)
        y = jnp.concatenate([z_ref[pl.ds(kc, ROW_CHUNK, stride=ROW_TILE), :] for kc in range(ROW_TILE)], axis=1)
        acc_ref[...] += _dot(onehot, y.astype(BF16))

    @pl.when((flag & PAIR_LAST) != 0)
    def _():
        gt2 = mod_ref[0][:, 5 * D_MODEL:6 * D_MODEL]
        o_ref[...] = x1_ref[...] + gt2 * _rms(acc_ref[...], g_ref[...], EPS)


def _combine(pairs, z, span, x1, mod3, mod_row, g_post_ffn):
    chunk, blk, flag = pairs
    t = x1.shape[0]
    tb = TOK_BLOCK
    grid_spec = pltpu.PrefetchScalarGridSpec(
        num_scalar_prefetch=3,
        grid=(chunk.shape[0],),
        in_specs=[pl.BlockSpec((ROW_CHUNK * ROW_TILE, LANES), lambda i, c, b, f: (c[i], 0)),
                  pl.BlockSpec((tb, LANES), lambda i, c, b, f: (b[i], 0)),
                  pl.BlockSpec((tb, D_MODEL), lambda i, c, b, f: (b[i], 0)),
                  pl.BlockSpec((1, 1, N_MOD * D_MODEL), lambda i, c, b, f: (mod_row(b[i], tb), 0, 0)),
                  pl.BlockSpec((1, D_MODEL), lambda i, c, b, f: (0, 0))],
        out_specs=pl.BlockSpec((tb, D_MODEL), lambda i, c, b, f: (b[i], 0)),
        scratch_shapes=[pltpu.VMEM((tb, D_MODEL), F32)])
    return pl.pallas_call(
        _combine_kernel,
        grid_spec=grid_spec,
        out_shape=jax.ShapeDtypeStruct((t, D_MODEL), F32),
        compiler_params=_cparams(("arbitrary",)),
        name="combine",
    )(chunk, blk, flag, z, span, x1, mod3, g_post_ffn)


def _rope_tables(seq):
    half = HEAD_DIM // 4
    freqs = ROPE_THETA ** (-np.arange(half, dtype=np.float64) / half)
    s = np.arange(seq)
    row = (s // GRID_W)[:, None] * freqs[None, :]
    col = (s % GRID_W)[:, None] * freqs[None, :]
    ang = np.concatenate([row, row, col, col], axis=1)
    ang = np.tile(ang, (1, QK_W // HEAD_DIM))
    lane = np.arange(QK_W)
    sign = np.where((lane % 32) < 16, -1.0, 1.0)[None, :]
    return (jnp.asarray(np.cos(ang), dtype=F32), jnp.asarray(np.sin(ang) * sign, dtype=F32))


def _combine_pairs(rows, n_tok):
    step = TOK_BLOCK // LANES
    nb = n_tok // TOK_BLOCK
    nc = 2 * n_tok // ROW_CHUNK
    lo = rows[0:nb * step:step, 0]
    hi = jnp.concatenate([lo[1:], jnp.full((1,), 2 * n_tok, I32)])
    c_lo = jnp.minimum(lo // ROW_CHUNK, nc - 1)
    c_hi = jnp.maximum((hi + ROW_CHUNK - 1) // ROW_CHUNK, c_lo + 1)
    n = c_hi - c_lo
    ends = jnp.cumsum(n)
    begins = ends - n
    i = jnp.arange(nb + nc, dtype=I32)
    valid = i < ends[-1]
    blk = jnp.minimum(jnp.sum(ends[None, :] <= i[:, None], axis=1), nb - 1).astype(I32)
    onehot = blk[:, None] == jnp.arange(nb, dtype=I32)[None, :]
    table = jnp.stack([begins, n, c_lo], axis=1)
    picked = jnp.sum(jnp.where(onehot[:, :, None], table[None, :, :], 0), axis=1)
    begin_i, n_i, c_lo_i = picked[:, 0], picked[:, 1], picked[:, 2]
    off = jnp.minimum(i - begin_i, n_i - 1)
    chunk = (c_lo_i + off).astype(I32)
    first = valid & (i == begin_i)
    last = valid & (i == begin_i + n_i - 1)
    flag = (first * PAIR_FIRST + last * PAIR_LAST + valid * PAIR_VALID).astype(I32)
    return chunk, blk, flag


def kernel(x_prompt, x_sample, c, cache_k, cache_v, c_ctx, w_mod, b_mod, g_pre_mix, g_post_mix, g_pre_ffn, g_post_ffn, w_in, lam_q1, lam_k1, lam_q2, lam_k2, g_subln, w_proj_attn, w_proj_fourier, w_out, w_router, w_gate, w_up, w_down):
    assert w_mod.shape[0] == 1
    lam_init = 0.8 - 0.6 * math.exp(-0.3 * 0)
    bp, sp, _ = x_prompt.shape
    bs, ss, _ = x_sample.shape

    cond8 = jnp.concatenate([c_ctx[None, :], c, jnp.zeros((8 - 1 - bs, D_MODEL), F32)], axis=0)
    mod3 = _modulation(cond8, w_mod[0], b_mod).reshape(8, 1, N_MOD * D_MODEL)

    w_in_b = w_in[0].astype(BF16)
    wpa = w_proj_attn[0].astype(BF16)
    wpf = w_proj_fourier[0].astype(BF16)
    wout = w_out[0].astype(BF16)
    wr = w_router[0].astype(BF16)
    wrx = jnp.concatenate([wr, jnp.zeros((D_MODEL, LANES - N_EXPERTS), BF16)], axis=1)
    lam_p = jnp.concatenate([lam_q1, lam_k1, lam_q2, lam_k2], axis=0)

    groups = []
    for x, seq, positional, ctx in ((x_prompt, sp, False, None),
                                    (x_sample, ss, True, (cache_k, cache_v))):
        nb = x.shape[0]
        t = nb * seq
        x2d = x.reshape(t, D_MODEL)
        if positional:
            mod_row = lambda i, tm, seq=seq: 1 + (i * tm) // seq
        else:
            mod_row = lambda i, tm: 0
        pre = _pre_mixer(x2d, mod3, mod_row, g_pre_mix, w_in_b,
                         _rope_tables(seq) if positional else None, seq, write_cache=not positional)
        q, k, v, f, ga, gf = pre[:6]
        o = _attention(lam_p, g_subln, q, k, v, ctx, seq, lam_init)
        fm = f if seq == ROW_BLOCK else _fourier(f, seq)
        x1, h2t, aff_t = _post_mixer(o, fm, ga, gf, x2d, mod3, mod_row, g_post_mix, g_pre_ffn,
                                     wpa, wpf, wout, wrx)
        cap = 2 * t // N_EXPERTS
        assert t <= 1 << TOKEN_BITS
        posm, pack, span, ranges, rows = _route(aff_t, cap)
        idx, qslot = _slot_lists(ranges, posm, pack, cap)
        groups.append(dict(x1=x1, h2t=h2t, idx=idx, qslot=qslot, span=span, pairs=_combine_pairs(rows, t),
                           mod_row=mod_row, cache=pre[6:], shape=x.shape))

    gc, gs_ = groups
    zc, zs = _moe(gc["idx"], gs_["idx"], gc["qslot"], gs_["qslot"], gc["h2t"], gs_["h2t"], wrx,
                  w_gate[0], w_up[0], w_down[0])
    outs = []
    for g, z in ((gc, zc), (gs_, zs)):
        out = _combine(g["pairs"], z, g["span"], g["x1"], mod3, g["mod_row"], g_post_ffn)
        outs.append(out.reshape(g["shape"]))
    new_k, new_v = gc["cache"]
    return (outs[0], outs[1], new_k, new_v)
```

```python
import functools
import math

import numpy as np
import jax
import jax.numpy as jnp
from jax import lax
from jax.experimental import pallas as pl
from jax.experimental.pallas import tpu as pltpu

F32 = jnp.float32
BF16 = jnp.bfloat16
I32 = jnp.int32

D_MODEL = 1024
N_HEADS = 6
HEAD_DIM = 64
V_DIM = 128
QK_W = 768
FOUR_W = 256
FOUR_G = 64
IN_W = 4608
N_EXPERTS = 16
D_FF = 2816
N_MOD = 6
GRID_W = 64
ROPE_THETA = 10000.0
EPS = 1e-6
SUBLN_EPS = 1e-5

LANES = 128
ROW_BLOCK = 256
PRE_BLOCK = 512
GATE_CHUNK = 256
POST_BLOCK = 1024
POST_SUB = 256
TOK_BLOCK = 512
ROW_CHUNK = 256
FF_TILE = 256
FF_STEPS = D_FF // FF_TILE
ROW_TILE = D_MODEL // LANES
TOKEN_BITS = 13
VMEM_LIMIT = 56 * 1024 * 1024


def _cparams(sem):
    return pltpu.CompilerParams(dimension_semantics=sem, vmem_limit_bytes=VMEM_LIMIT)


def _dot(a, b):
    return jnp.dot(a, b, preferred_element_type=F32)


def _rms(x, g, eps):
    return x * lax.rsqrt(jnp.mean(x * x, axis=-1, keepdims=True) + eps) * g


def _whole_vmem():
    return pl.BlockSpec(memory_space=pltpu.MemorySpace.VMEM)


def _mod_kernel(c_ref, w_ref, b_ref, o_ref):
    c = c_ref[...]
    s = c * jax.nn.sigmoid(c)
    o_ref[...] = _dot(s.astype(BF16), w_ref[...].astype(BF16)) + b_ref[...]


def _modulation(cond8, w_mod, b_mod):
    tn = 1024
    n = N_MOD * D_MODEL
    return pl.pallas_call(
        _mod_kernel,
        grid=(n // tn,),
        in_specs=[pl.BlockSpec((8, D_MODEL), lambda j: (0, 0)),
                  pl.BlockSpec((D_MODEL, tn), lambda j: (0, j)),
                  pl.BlockSpec((1, tn), lambda j: (0, j))],
        out_specs=pl.BlockSpec((8, tn), lambda j: (0, j)),
        out_shape=jax.ShapeDtypeStruct((8, n), F32),
        compiler_params=_cparams(("arbitrary",)),
        name="modulation",
    )(cond8, w_mod, b_mod)


def _diff_lambda(lp, lam_init):
    s1 = jnp.sum(lp[0:1] * lp[1:2], axis=-1, keepdims=True)
    s2 = jnp.sum(lp[2:3] * lp[3:4], axis=-1, keepdims=True)
    return jnp.exp(s1) - jnp.exp(s2) + lam_init


def _diff_attention_head(q, k, v, lam, g_subln, lam_init):
    comp1 = lax.broadcasted_iota(I32, (1, V_DIM), 1) < HEAD_DIM
    scale = jnp.asarray(HEAD_DIM ** -0.5, BF16)
    v_ones = jnp.concatenate([v, jnp.ones_like(v)], axis=1)

    def attend(qc):
        s = lax.dot_general(qc, k, (((1,), (1,)), ((), ())), preferred_element_type=F32)
        ex = jnp.exp(s - jnp.max(s, axis=-1, keepdims=True)).astype(BF16)
        ov = _dot(ex, v_ones)
        return ov[:, 0:V_DIM] / ov[:, V_DIM:2 * V_DIM]

    qs = q * scale
    zero = jnp.zeros_like(qs)
    o = attend(jnp.where(comp1, qs, zero)) - lam * attend(jnp.where(comp1, zero, qs))
    return (_rms(o, g_subln, SUBLN_EPS) * (1.0 - lam_init)).astype(BF16)


def _rope(z, cos, sin_signed, first_half):
    fwd = pltpu.roll(z, QK_W - 16, axis=1)
    bwd = pltpu.roll(z, 16, axis=1)
    return z * cos + jnp.where(first_half, fwd, bwd) * sin_signed


def _pre_kernel(*refs, positional, fuse_seq, lam_init):
    it = iter(refs)
    x_ref, mod_ref, g_ref, w_ref = next(it), next(it), next(it), next(it)
    if positional:
        cos_ref, sin_ref = next(it), next(it)
    if fuse_seq:
        dft_refs = [next(it) for _ in range(4)]
        lam_ref, gs_ref = next(it), next(it)
        o_ref, f_ref, ga_ref, gf_ref, kc_ref, vc_ref = (next(it) for _ in range(6))
    else:
        q_ref, k_ref, v_ref, f_ref, ga_ref, gf_ref = (next(it) for _ in range(6))

    m = mod_ref[0]
    sh1 = m[:, 0:D_MODEL]
    sc1 = m[:, D_MODEL:2 * D_MODEL]
    h = _rms(x_ref[...], g_ref[...], EPS) * (1.0 + sc1) + sh1
    hb = h.astype(BF16)

    def proj(lo, hi):
        return _dot(hb, w_ref[:, lo:hi])

    zq = proj(0, QK_W)
    zk = proj(QK_W, 2 * QK_W)
    zv = proj(2 * QK_W, 3 * QK_W)
    if positional:
        lane = lax.broadcasted_iota(I32, (1, QK_W), 1)
        first_half = (lane % 32) < 16
        cos = cos_ref[...]
        sin_signed = sin_ref[...]
        zq = _rope(zq, cos, sin_signed, first_half)
        zk = _rope(zk, cos, sin_signed, first_half)
    qb, kb, vb = zq.astype(BF16), zk.astype(BF16), zv.astype(BF16)
    f0 = 3 * QK_W
    g0 = f0 + FOUR_W

    def gate_chunk(ref, lo, c):
        cols = slice(c * GATE_CHUNK, (c + 1) * GATE_CHUNK)
        ref[:, cols] = jax.nn.sigmoid(proj(lo + cols.start, lo + cols.stop)).astype(BF16)

    gate_work = [functools.partial(gate_chunk, ref, lo, c)
                 for ref, lo in ((ga_ref, g0), (gf_ref, g0 + D_MODEL)) for c in range(D_MODEL // GATE_CHUNK)]
    if not fuse_seq:
        q_ref[...] = qb
        k_ref[...] = kb
        v_ref[...] = vb
        f_ref[...] = proj(f0, g0)
        for work in gate_work:
            work()
        return

    seqs = [slice(b * fuse_seq, (b + 1) * fuse_seq) for b in range(x_ref.shape[0] // fuse_seq)]
    f = proj(f0, g0)
    lam = _diff_lambda(lam_ref[...], lam_init)
    for b, rs in enumerate(seqs):
        f_ref[rs, :] = _dft_real(f[rs, :], *dft_refs)
        for hd in range(N_HEADS):
            sl = slice(hd * V_DIM, (hd + 1) * V_DIM)
            kc_ref[b, 0, hd] = zk[rs, sl]
            vc_ref[b, 0, hd] = zv[rs, sl]
            o_ref[rs, sl] = _diff_attention_head(qb[rs, sl], kb[rs, sl], vb[rs, sl], lam, gs_ref[...], lam_init)
            if gate_work:
                gate_work.pop(0)()
    for work in gate_work:
        work()


def _pre_mixer(x2d, mod3, mod_row, g_pre, w_in_b, rope_tabs, seq, self_contained, attn_params, tm):
    t = x2d.shape[0]
    positional = rope_tabs is not None
    lam_p, g_subln, lam_init = attn_params
    if self_contained:
        assert tm % seq == 0
    blocks_per_seq = max(seq // tm, 1)
    row = lambda i: (i, 0)
    in_specs = [pl.BlockSpec((tm, D_MODEL), row),
                pl.BlockSpec((1, 1, N_MOD * D_MODEL), lambda i: (mod_row(i, tm), 0, 0)),
                pl.BlockSpec((1, D_MODEL), lambda i: (0, 0)),
                _whole_vmem()]
    args = [x2d, mod3, g_pre, w_in_b]
    if positional:
        in_specs += [pl.BlockSpec((tm, QK_W), lambda i: (i % blocks_per_seq, 0))] * 2
        args += list(rope_tabs)
    if self_contained:
        consts = _dft_consts(seq) + (lam_p, g_subln)
        in_specs += [pl.BlockSpec(c.shape, lambda i: (0, 0)) for c in consts]
        args += list(consts)
    n_wide = 1 if self_contained else 3
    out_shape = [jax.ShapeDtypeStruct((t, QK_W), BF16)] * n_wide + [
        jax.ShapeDtypeStruct((t, FOUR_W), BF16 if self_contained else F32),
        jax.ShapeDtypeStruct((t, D_MODEL), BF16),
        jax.ShapeDtypeStruct((t, D_MODEL), BF16)]
    out_specs = [pl.BlockSpec((tm, QK_W), row)] * n_wide + [
        pl.BlockSpec((tm, FOUR_W), row),
        pl.BlockSpec((tm, D_MODEL), row),
        pl.BlockSpec((tm, D_MODEL), row)]
    if self_contained:
        nb = t // seq
        cshape = jax.ShapeDtypeStruct((nb, 1, N_HEADS, seq, V_DIM), F32)
        cspec = pl.BlockSpec((tm // seq, 1, N_HEADS, seq, V_DIM), lambda i: (i, 0, 0, 0, 0))
        out_shape += [cshape, cshape]
        out_specs += [cspec, cspec]
    return pl.pallas_call(
        functools.partial(_pre_kernel, positional=positional, fuse_seq=seq if self_contained else 0,
                          lam_init=lam_init),
        grid=(t // tm,),
        in_specs=in_specs,
        out_specs=out_specs,
        out_shape=out_shape,
        compiler_params=_cparams(("arbitrary",)),
        name="pre_mixer",
    )(*args)


def _attn_kernel(*refs, lam_init, has_ctx):
    it = iter(refs)
    lam_ref, gs_ref, q_ref, k_ref, v_ref = (next(it) for _ in range(5))
    if has_ctx:
        ck_ref, cv_ref = next(it), next(it)
    o_ref = next(it)

    lam = _diff_lambda(lam_ref[...], lam_init)
    for hd in range(N_HEADS):
        sl = slice(hd * V_DIM, (hd + 1) * V_DIM)
        k = k_ref[:, sl]
        v = v_ref[:, sl]
        if has_ctx:
            k = jnp.concatenate([ck_ref[0, 0, hd].astype(BF16), k], axis=0)
            v = jnp.concatenate([cv_ref[0, 0, hd].astype(BF16), v], axis=0)
        o_ref[:, sl] = _diff_attention_head(q_ref[:, sl], k, v, lam, gs_ref[...], lam_init)


def _attention(lam_p, g_subln, q, k, v, ctx, seq, lam_init):
    t = q.shape[0]
    tq = ROW_BLOCK
    qb = seq // tq
    has_ctx = ctx is not None
    in_specs = [pl.BlockSpec((4, HEAD_DIM), lambda b, i: (0, 0)),
                pl.BlockSpec((1, V_DIM), lambda b, i: (0, 0)),
                pl.BlockSpec((tq, QK_W), lambda b, i: (b * qb + i, 0)),
                pl.BlockSpec((seq, QK_W), lambda b, i: (b, 0)),
                pl.BlockSpec((seq, QK_W), lambda b, i: (b, 0))]
    args = [lam_p, g_subln, q, k, v]
    if has_ctx:
        past = ctx[0].shape[3]
        cspec = pl.BlockSpec((1, 1, N_HEADS, past, V_DIM), lambda b, i: (b, 0, 0, 0, 0))
        in_specs += [cspec, cspec]
        args += list(ctx)
    return pl.pallas_call(
        functools.partial(_attn_kernel, lam_init=lam_init, has_ctx=has_ctx),
        grid=(t // seq, qb),
        in_specs=in_specs,
        out_specs=pl.BlockSpec((tq, QK_W), lambda b, i: (b * qb + i, 0)),
        out_shape=jax.ShapeDtypeStruct((t, QK_W), BF16),
        compiler_params=_cparams(("arbitrary", "arbitrary")),
        name="diff_attention",
    )(*args)


def _dft_real(f, bc_ref, bs_ref, cs_ref, ss_ref):
    fb = f.astype(BF16)
    u = _dot(fb, bc_ref[...].astype(BF16)).astype(BF16)
    w = _dot(fb, bs_ref[...].astype(BF16)).astype(BF16)
    return (_dot(cs_ref[...].astype(BF16), u) - _dot(ss_ref[...].astype(BF16), w)).astype(BF16)


def _fourier_kernel(f_ref, bc_ref, bs_ref, cs_ref, ss_ref, o_ref):
    o_ref[...] = _dft_real(f_ref[...], bc_ref, bs_ref, cs_ref, ss_ref)


def _dft_consts(seq):
    c = np.arange(FOUR_G)
    ang_c = 2.0 * np.pi * ((c[:, None] * c[None, :]) % FOUR_G) / FOUR_G
    eye = np.eye(FOUR_W // FOUR_G)
    bc = np.kron(eye, np.cos(ang_c)) / math.sqrt(FOUR_G)
    bs = np.kron(eye, np.sin(ang_c)) / math.sqrt(FOUR_G)
    s = np.arange(seq)
    ang_s = 2.0 * np.pi * ((s[:, None] * s[None, :]) % seq) / seq
    cs = np.cos(ang_s) / math.sqrt(seq)
    ss = np.sin(ang_s) / math.sqrt(seq)
    return tuple(jnp.asarray(a, dtype=F32) for a in (bc, bs, cs, ss))


def _fourier(f, seq):
    t = f.shape[0]
    bc, bs, cs, ss = _dft_consts(seq)
    const = lambda b: (0, 0)
    return pl.pallas_call(
        _fourier_kernel,
        grid=(t // seq,),
        in_specs=[pl.BlockSpec((seq, FOUR_W), lambda b: (b, 0)),
                  pl.BlockSpec((FOUR_W, FOUR_W), const),
                  pl.BlockSpec((FOUR_W, FOUR_W), const),
                  pl.BlockSpec((seq, seq), const),
                  pl.BlockSpec((seq, seq), const)],
        out_specs=pl.BlockSpec((seq, FOUR_W), lambda b: (b, 0)),
        out_shape=jax.ShapeDtypeStruct((t, FOUR_W), BF16),
        compiler_params=_cparams(("arbitrary",)),
        name="fourier_mix",
    )(f, bc, bs, cs, ss)


def _post_kernel(o_ref, fm_ref, ga_ref, gf_ref, x_ref, mod_ref, gpost_ref, gffn_ref,
                 wpa_ref, wpf_ref, wout_ref, wrx_ref,
                 x1_ref, h2t_ref, afft_ref):
    m = mod_ref[0]
    gt1 = m[:, 2 * D_MODEL:3 * D_MODEL]
    sh2 = m[:, 3 * D_MODEL:4 * D_MODEL]
    sc2 = m[:, 4 * D_MODEL:5 * D_MODEL]
    subs = [slice(r0, r0 + POST_SUB) for r0 in range(0, o_ref.shape[0], POST_SUB)]
    ab = [(_dot(o_ref[rs, :], wpa_ref[...]), _dot(fm_ref[rs, :], wpf_ref[...])) for rs in subs]
    merged = [(ga_ref[rs, :] * a + gf_ref[rs, :] * b).astype(BF16) for rs, (a, b) in zip(subs, ab)]
    ys = [_dot(mg, wout_ref[...]) for mg in merged]
    h2s = []
    for rs, y in zip(subs, ys):
        x1 = x_ref[rs, :] + gt1 * _rms(y, gpost_ref[...], EPS)
        x1_ref[rs, :] = x1
        h2s.append(_rms(x1, gffn_ref[...], EPS) * (1.0 + sc2) + sh2)
    logits = [_dot(h2.astype(BF16), wrx_ref[...]) for h2 in h2s]
    for rs, h2, lg in zip(subs, h2s, logits):
        lt = lg.T[0:N_EXPERTS]
        et = jnp.exp(lt - jnp.max(lt, axis=0, keepdims=True))
        aff = et / jnp.sum(et, axis=0, keepdims=True)
        for u in range(POST_SUB // LANES):
            afft_ref[:, rs.start // LANES + u, :] = aff[:, u * LANES:(u + 1) * LANES]
        for kc in range(ROW_TILE):
            h2t_ref[pl.ds(rs.start * ROW_TILE + kc, POST_SUB, stride=ROW_TILE), :] = h2[:, kc * LANES:(kc + 1) * LANES]


def _post_mixer(o, fm, ga, gf, x2d, mod3, mod_row, g_post, g_ffn, wpa, wpf, wout, wrx):
    t = x2d.shape[0]
    tm = POST_BLOCK
    row = lambda i: (i, 0)
    const = lambda i: (0, 0)
    return pl.pallas_call(
        _post_kernel,
        grid=(t // tm,),
        in_specs=[pl.BlockSpec((tm, QK_W), row),
                  pl.BlockSpec((tm, FOUR_W), row),
                  pl.BlockSpec((tm, D_MODEL), row),
                  pl.BlockSpec((tm, D_MODEL), row),
                  pl.BlockSpec((tm, D_MODEL), row),
                  pl.BlockSpec((1, 1, N_MOD * D_MODEL), lambda i: (mod_row(i, tm), 0, 0)),
                  pl.BlockSpec((1, D_MODEL), const),
                  pl.BlockSpec((1, D_MODEL), const),
                  pl.BlockSpec((QK_W, D_MODEL), const),
                  pl.BlockSpec((FOUR_W, D_MODEL), const),
                  pl.BlockSpec((D_MODEL, D_MODEL), const),
                  pl.BlockSpec((D_MODEL, LANES), const)],
        out_specs=[pl.BlockSpec((tm, D_MODEL), row),
                   pl.BlockSpec((tm * ROW_TILE, LANES), row),
                   pl.BlockSpec((N_EXPERTS, tm // LANES, LANES), lambda i: (0, i, 0))],
        out_shape=[jax.ShapeDtypeStruct((t, D_MODEL), F32),
                   jax.ShapeDtypeStruct((t * ROW_TILE, LANES), F32),
                   jax.ShapeDtypeStruct((N_EXPERTS, t // LANES, LANES), F32)],
        compiler_params=_cparams(("arbitrary",)),
        name="post_mixer",
    )(o, fm, ga, gf, x2d, mod3, g_post, g_ffn, wpa, wpf, wout, wrx)


RANGE_ROWS = 8


def _route_kernel(aff_ref, posm_ref, pack_ref, span_ref, ranges_ref, rows_ref, *, cap, n_tok):
    aff = aff_ref[...]
    nt = n_tok // LANES
    capf = float(cap)

    def count_ge(v):
        return jnp.sum(jnp.where(aff >= v, 1.0, 0.0), axis=(1, 2), keepdims=True)

    def search(i, thr):
        cand = thr | jnp.left_shift(jnp.int32(1), 30 - i)
        return jnp.where(count_ge(pltpu.bitcast(cand, F32)) >= capf, cand, thr)

    thr = lax.fori_loop(0, 31, search, jnp.zeros((N_EXPERTS, 1, 1), I32))
    lo = pltpu.bitcast(thr, F32)
    hi = pltpu.bitcast(thr + 1, F32)

    def refine(i, c):
        lo, hi = c
        mid = lo + (hi - lo) * 0.5
        ok = count_ge(mid) >= capf
        return jnp.where(ok, mid, lo), jnp.where(ok, hi, mid)

    lo, hi = lax.fori_loop(0, 12, refine, (lo, hi))
    gt = aff >= hi
    eq = (aff >= lo) & (aff < hi)
    n_tie = capf - jnp.sum(jnp.where(gt, 1.0, 0.0), axis=(1, 2), keepdims=True)

    sq0 = lax.broadcasted_iota(I32, (LANES, LANES), 0)
    sq1 = lax.broadcasted_iota(I32, (LANES, LANES), 1)
    along_total = jnp.concatenate([jnp.where(sq0 <= sq1, 1.0, 0.0), jnp.ones((LANES, LANES), F32)],
                                  axis=1).astype(BF16)
    m = N_EXPERTS * nt
    r0 = lax.broadcasted_iota(I32, (m, m), 0)
    r1 = lax.broadcasted_iota(I32, (m, m), 1)
    earlier = jnp.where((r0 // nt == r1 // nt) & (r1 < r0), 1.0, 0.0).astype(BF16)
    lane = lax.broadcasted_iota(I32, (1, LANES), 1)
    token = lax.broadcasted_iota(I32, (nt, LANES), 0) * LANES + lane

    def tile_counts(x):
        both = _dot(x.reshape(m, LANES).astype(BF16), along_total)
        total = both[:, LANES:]
        before = _dot(earlier, total.astype(BF16))
        shape = (N_EXPERTS, nt, LANES)
        return both[:, :LANES].reshape(shape), total.reshape(shape), before.reshape(shape)

    eq_f = jnp.where(eq, 1.0, 0.0)
    eq_along, _, eq_before = tile_counts(eq_f)
    sel = jnp.where(gt, 1.0, jnp.where(eq_along + eq_before <= n_tie, eq_f, 0.0))
    sel_along, sel_total, sel_before = tile_counts(sel)
    posm_ref[...] = jnp.where(sel > 0.5, sel_along + sel_before - sel, -1.0).astype(I32)

    cnt = jnp.sum(sel, axis=0)
    rows_before = jnp.sum(sel_before, axis=0)
    tok_start = _dot(cnt.astype(BF16), along_total[:, :LANES]) - cnt + rows_before
    k = jnp.zeros((nt, LANES), F32)
    slots_before = jnp.zeros((nt, LANES), F32)
    slots_here = jnp.zeros((nt, LANES), F32)
    for e in range(N_EXPERTS):
        pack_ref[e] = (tok_start + k).astype(I32) * (1 << TOKEN_BITS) + token
        k = k + sel[e]
        slots_before = jnp.where(lane == e, sel_before[e], slots_before)
        slots_here = jnp.where(lane == e, sel_total[e], slots_here)

    eye = jnp.where(sq0 == sq1, 1.0, 0.0).astype(BF16)
    tok_end = tok_start + cnt
    parts = []
    for v in (tok_start, tok_end):
        high = jnp.floor(v * (1.0 / LANES))
        parts += [high, v - high * LANES]
    for c in range(nt):
        rows4 = jnp.zeros((LANES, LANES), F32)
        for r, part in enumerate(parts):
            rows4 = jnp.where(sq0 == r, jnp.broadcast_to(part[c:c + 1, :], (LANES, LANES)), rows4)
        span_ref[c * LANES:(c + 1) * LANES, :] = lax.dot_general(
            eye, rows4.astype(BF16), (((1,), (1,)), ((), ())), preferred_element_type=F32)

    ranges_ref[...] = jnp.zeros_like(ranges_ref)
    for s in range(cap // LANES):
        done = jnp.where(slots_before + slots_here <= float(s * LANES), 1.0, 0.0)
        begun = jnp.where(slots_before < float((s + 1) * LANES), 1.0, 0.0)
        ranges_ref[s:s + 1, :] = jnp.sum(done, axis=0, keepdims=True).astype(I32)
        ranges_ref[RANGE_ROWS + s:RANGE_ROWS + s + 1, :] = jnp.sum(begun, axis=0, keepdims=True).astype(I32)
    rows_ref[...] = rows_before.astype(I32)


def _route(aff, cap):
    n_e, nt, _ = aff.shape
    n_tok = nt * LANES
    assert cap // LANES <= RANGE_ROWS
    return pl.pallas_call(
        functools.partial(_route_kernel, cap=cap, n_tok=n_tok),
        out_shape=[jax.ShapeDtypeStruct((n_e, nt, LANES), I32),
                   jax.ShapeDtypeStruct((n_e, nt, LANES), I32),
                   jax.ShapeDtypeStruct((n_tok, LANES), F32),
                   jax.ShapeDtypeStruct((2 * RANGE_ROWS, LANES), I32),
                   jax.ShapeDtypeStruct((nt, LANES), I32)],
        compiler_params=pltpu.CompilerParams(vmem_limit_bytes=VMEM_LIMIT),
        name="route",
    )(aff)


def _slots_kernel(ranges_ref, posm_ref, pack_ref, idx_ref, qslot_ref):
    e = pl.program_id(0)
    sub = lax.broadcasted_iota(I32, (LANES, LANES), 0)
    eye = sub == lax.broadcasted_iota(I32, (LANES, LANES), 1)

    def as_row(part):
        col = jnp.sum(part.astype(F32), axis=1, keepdims=True)
        return jnp.sum(jnp.where(eye, col, 0.0), axis=0, keepdims=True).astype(I32)

    for s in range(idx_ref.shape[1]):
        slot = sub + s * LANES

        def body(c, acc, slot=slot):
            hit = posm_ref[e, pl.ds(c, 1), :] == slot
            return acc + jnp.where(hit, pack_ref[e, pl.ds(c, 1), :], 0)

        acc = lax.fori_loop(ranges_ref[s, e], ranges_ref[RANGE_ROWS + s, e], body,
                            jnp.zeros((LANES, LANES), I32))
        idx_ref[0, s:s + 1, :] = as_row(acc & ((1 << TOKEN_BITS) - 1)) * ROW_TILE
        qslot_ref[0, s:s + 1, :] = as_row(lax.shift_right_logical(acc, TOKEN_BITS)) * ROW_TILE


def _slot_lists(ranges, posm4, qdst4, cap):
    ns = cap // LANES
    grid_spec = pltpu.PrefetchScalarGridSpec(
        num_scalar_prefetch=1,
        grid=(N_EXPERTS,),
        in_specs=[_whole_vmem(), _whole_vmem()],
        out_specs=[pl.BlockSpec((1, ns, LANES), lambda e, *_: (e, 0, 0)),
                   pl.BlockSpec((1, ns, LANES), lambda e, *_: (e, 0, 0))])
    idx, qslot = pl.pallas_call(
        _slots_kernel,
        grid_spec=grid_spec,
        out_shape=[jax.ShapeDtypeStruct((N_EXPERTS, ns, LANES), I32),
                   jax.ShapeDtypeStruct((N_EXPERTS, ns, LANES), I32)],
        compiler_params=_cparams(("arbitrary",)),
        name="slot_lists",
    )(ranges, posm4, qdst4)
    return idx.reshape(N_EXPERTS, cap), qslot.reshape(N_EXPERTS, cap)


def _moe_kernel(idxc_ref, idxs_ref, qc_ref, qs_ref,
                hc_ref, hs_ref, wr_ref, wg_ref, wu_ref, wd_ref, zc_ref, zs_ref,
                xbuf, ybuf, xb_ref, gate_ref, acc_ref, gsem, ssem, *, capc, caps):
    e = pl.program_id(0)
    j = pl.program_id(1)
    n_e = pl.num_programs(0)
    n_j = FF_STEPS
    slot = e % 2
    other = 1 - slot
    rows = capc + caps
    gc, gs = _per_step(capc), _per_step(caps)
    groups = ((hc_ref, idxc_ref, zc_ref, qc_ref, gc, 0), (hs_ref, idxs_ref, zs_ref, qs_ref, gs, gc * n_j))

    def tile(ref, first_sublane):
        return ref.at[pl.ds(pl.multiple_of(first_sublane, ROW_TILE), ROW_TILE), :]

    def gather(ex, sl, step, i, group):
        h_ref, idx_ref, _, _, per_step, base = group
        p = step * per_step + i
        src = tile(h_ref, idx_ref[ex * (per_step * n_j) + p])
        pltpu.make_async_copy(src, tile(xbuf.at[sl], (base + p) * ROW_TILE), gsem.at[sl]).start()

    def scatter(table_row, sl, step, i, group):
        _, _, z_ref, q_ref, per_step, base = group
        p = step * per_step + i
        dst = tile(z_ref, q_ref[table_row * (per_step * n_j) + p])
        pltpu.make_async_copy(tile(ybuf.at[sl], (base + p) * ROW_TILE), dst, ssem.at[sl]).start()

    def all_steps(fn):
        for group in groups:
            def body(p, carry, group=group):
                fn(p, group)
                return carry
            lax.fori_loop(0, group[4] * n_j, body, 0, unroll=8)

    def wait_all(buf, sem, sl):
        pltpu.make_async_copy(buf.at[sl], buf.at[sl], sem.at[sl]).wait()

    @pl.when((e == 0) & (j == 0))
    def _():
        ybuf[...] = jnp.zeros_like(ybuf)
        all_steps(lambda p, group: gather(0, 0, 0, p, group))

    @pl.when(j == 0)
    def _():
        wait_all(xbuf, gsem, slot)
        for base, n, dst in ((0, capc, 0), (gc * n_j, caps, capc)):
            for kc in range(ROW_TILE):
                tiles = xbuf[slot, pl.ds(base * ROW_TILE + kc, n, stride=ROW_TILE), :]
                xb_ref[dst:dst + n, kc * LANES:(kc + 1) * LANES] = tiles.astype(BF16)
        acc_ref[...] = jnp.zeros_like(acc_ref)
        logits = _dot(xb_ref[...], wr_ref[...])
        lane = lax.broadcasted_iota(I32, (1, LANES), 1)
        is_expert = lane < N_EXPERTS
        ex = jnp.exp(logits - jnp.max(jnp.where(is_expert, logits, -jnp.inf), axis=-1, keepdims=True))
        mine = jnp.sum(jnp.where(lane == e, ex, 0.0), axis=-1, keepdims=True)
        gate = mine / jnp.sum(jnp.where(is_expert, ex, 0.0), axis=-1, keepdims=True)
        gate_ref[...] = jnp.broadcast_to(gate, gate_ref.shape)

    nxt = jnp.minimum(e + 1, n_e - 1)
    for group in groups:
        for i in range(group[4]):
            gather(nxt, other, j, i, group)
            scatter(e, other, j, i, group)

    x = xb_ref[...]
    g = _dot(x, wg_ref[0].astype(BF16))
    u = _dot(x, wu_ref[0].astype(BF16))
    hid = (g * jax.nn.sigmoid(g) * u).astype(BF16)
    acc_ref[...] += _dot(hid, wd_ref[0].astype(BF16))

    @pl.when(j == n_j - 1)
    def _():
        @pl.when(e >= 1)
        def _():
            wait_all(ybuf, ssem, slot)

        gate = gate_ref[...]
        for base, n, src in ((0, capc, 0), (gc * n_j, caps, capc)):
            for kc in range(ROW_TILE):
                y = acc_ref[src:src + n, kc * LANES:(kc + 1) * LANES] * gate[src:src + n]
                ybuf[slot, pl.ds(base * ROW_TILE + kc, n, stride=ROW_TILE), :] = y

        @pl.when(e == n_e - 1)
        def _():
            all_steps(lambda p, group: scatter(e + 1, slot, 0, p, group))
            wait_all(ybuf, ssem, other)
            wait_all(ybuf, ssem, slot)
            wait_all(xbuf, gsem, other)


def _per_step(cap):
    return -(-cap // FF_STEPS)


def _copy_tables(idx, qslot, n_rows):
    n_e, cap = idx.shape
    padded = _per_step(cap) * FF_STEPS
    n_pad = padded - cap
    idx_p = jnp.concatenate([idx, jnp.zeros((n_e, n_pad), I32)], axis=1)
    spare = n_rows + jnp.arange(padded + n_e * n_pad, dtype=I32)
    lead = spare[:padded][None, :]
    pad_rows = spare[padded:].reshape(n_e, n_pad)
    q_p = jnp.concatenate([lead, jnp.concatenate([qslot // ROW_TILE, pad_rows], axis=1)], axis=0) * ROW_TILE
    return idx_p.reshape(-1), q_p.reshape(-1), n_rows + padded + n_e * n_pad


def _moe(idxc, idxs, qc, qs, hc, hs, wrx, w_gate, w_up, w_down):
    capc, caps = idxc.shape[1], idxs.shape[1]
    rows = capc + caps
    tf = FF_TILE
    idxc, qc, zc_rows = _copy_tables(idxc, qc, N_EXPERTS * capc)
    idxs, qs, zs_rows = _copy_tables(idxs, qs, N_EXPERTS * caps)
    buf_rows = (_per_step(capc) + _per_step(caps)) * FF_STEPS
    any_spec = pl.BlockSpec(memory_space=pl.ANY)
    grid_spec = pltpu.PrefetchScalarGridSpec(
        num_scalar_prefetch=4,
        grid=(N_EXPERTS, FF_STEPS),
        in_specs=[any_spec, any_spec,
                  pl.BlockSpec((D_MODEL, LANES), lambda e, j, *_: (0, 0)),
                  pl.BlockSpec((1, D_MODEL, tf), lambda e, j, *_: (e, 0, j)),
                  pl.BlockSpec((1, D_MODEL, tf), lambda e, j, *_: (e, 0, j)),
                  pl.BlockSpec((1, tf, D_MODEL), lambda e, j, *_: (e, j, 0))],
        out_specs=[any_spec, any_spec],
        scratch_shapes=[pltpu.VMEM((2, buf_rows * ROW_TILE, LANES), F32),
                        pltpu.VMEM((2, buf_rows * ROW_TILE, LANES), F32),
                        pltpu.VMEM((rows, D_MODEL), BF16),
                        pltpu.VMEM((rows, LANES), F32),
                        pltpu.VMEM((rows, D_MODEL), F32),
                        pltpu.SemaphoreType.DMA((2,)),
                        pltpu.SemaphoreType.DMA((2,))])
    return pl.pallas_call(
        functools.partial(_moe_kernel, capc=capc, caps=caps),
        grid_spec=grid_spec,
        out_shape=[jax.ShapeDtypeStruct((zc_rows * ROW_TILE, LANES), F32),
                   jax.ShapeDtypeStruct((zs_rows * ROW_TILE, LANES), F32)],
        compiler_params=_cparams(("arbitrary", "arbitrary")),
        name="expert_ffn",
    )(idxc, idxs, qc, qs, hc, hs, wrx, w_gate, w_up, w_down)


PAIR_FIRST, PAIR_LAST, PAIR_VALID = 1, 2, 4


def _combine_kernel(chunk_ref, blk_ref, flag_ref, z_ref, span_ref, x1_ref, mod_ref, g_ref, o_ref, acc_ref):
    i = pl.program_id(0)
    flag = flag_ref[i]

    @pl.when((flag & PAIR_FIRST) != 0)
    def _():
        acc_ref[...] = jnp.zeros_like(acc_ref)

    @pl.when((flag & PAIR_VALID) != 0)
    def _():
        row = (chunk_ref[i] * ROW_CHUNK + lax.broadcasted_iota(I32, (TOK_BLOCK, ROW_CHUNK), 1)).astype(F32)
        first_row = span_ref[:, 0:1] * LANES + span_ref[:, 1:2]
        end_row = span_ref[:, 2:3] * LANES + span_ref[:, 3:4]
        onehot = jnp.where((row >= first_row) & (row < end_row), 1.0, 0.0).astype(BF16)
        y = jnp.concatenate([z_ref[pl.ds(kc, ROW_CHUNK, stride=ROW_TILE), :] for kc in range(ROW_TILE)], axis=1)
        acc_ref[...] += _dot(onehot, y.astype(BF16))

    @pl.when((flag & PAIR_LAST) != 0)
    def _():
        gt2 = mod_ref[0][:, 5 * D_MODEL:6 * D_MODEL]
        o_ref[...] = x1_ref[...] + gt2 * _rms(acc_ref[...], g_ref[...], EPS)


def _combine(pairs, z, span, x1, mod3, mod_row, g_post_ffn):
    chunk, blk, flag = pairs
    t = x1.shape[0]
    tb = TOK_BLOCK
    grid_spec = pltpu.PrefetchScalarGridSpec(
        num_scalar_prefetch=3,
        grid=(chunk.shape[0],),
        in_specs=[pl.BlockSpec((ROW_CHUNK * ROW_TILE, LANES), lambda i, c, b, f: (c[i], 0)),
                  pl.BlockSpec((tb, LANES), lambda i, c, b, f: (b[i], 0)),
                  pl.BlockSpec((tb, D_MODEL), lambda i, c, b, f: (b[i], 0)),
                  pl.BlockSpec((1, 1, N_MOD * D_MODEL), lambda i, c, b, f: (mod_row(b[i], tb), 0, 0)),
                  pl.BlockSpec((1, D_MODEL), lambda i, c, b, f: (0, 0))],
        out_specs=pl.BlockSpec((tb, D_MODEL), lambda i, c, b, f: (b[i], 0)),
        scratch_shapes=[pltpu.VMEM((tb, D_MODEL), F32)])
    return pl.pallas_call(
        _combine_kernel,
        grid_spec=grid_spec,
        out_shape=jax.ShapeDtypeStruct((t, D_MODEL), F32),
        compiler_params=_cparams(("arbitrary",)),
        name="combine",
    )(chunk, blk, flag, z, span, x1, mod3, g_post_ffn)


def _rope_tables(seq):
    half = HEAD_DIM // 4
    freqs = ROPE_THETA ** (-np.arange(half, dtype=np.float64) / half)
    s = np.arange(seq)
    row = (s // GRID_W)[:, None] * freqs[None, :]
    col = (s % GRID_W)[:, None] * freqs[None, :]
    ang = np.concatenate([row, row, col, col], axis=1)
    ang = np.tile(ang, (1, QK_W // HEAD_DIM))
    lane = np.arange(QK_W)
    sign = np.where((lane % 32) < 16, -1.0, 1.0)[None, :]
    return (jnp.asarray(np.cos(ang), dtype=F32), jnp.asarray(np.sin(ang) * sign, dtype=F32))


def _combine_pairs(rows, n_tok):
    step = TOK_BLOCK // LANES
    nb = n_tok // TOK_BLOCK
    nc = 2 * n_tok // ROW_CHUNK
    lo = rows[0:nb * step:step, 0]
    hi = jnp.concatenate([lo[1:], jnp.full((1,), 2 * n_tok, I32)])
    c_lo = jnp.minimum(lo // ROW_CHUNK, nc - 1)
    c_hi = jnp.maximum((hi + ROW_CHUNK - 1) // ROW_CHUNK, c_lo + 1)
    n = c_hi - c_lo
    ends = jnp.cumsum(n)
    begins = ends - n
    i = jnp.arange(nb + nc, dtype=I32)
    valid = i < ends[-1]
    blk = jnp.minimum(jnp.sum(ends[None, :] <= i[:, None], axis=1), nb - 1).astype(I32)
    onehot = blk[:, None] == jnp.arange(nb, dtype=I32)[None, :]
    table = jnp.stack([begins, n, c_lo], axis=1)
    picked = jnp.sum(jnp.where(onehot[:, :, None], table[None, :, :], 0), axis=1)
    begin_i, n_i, c_lo_i = picked[:, 0], picked[:, 1], picked[:, 2]
    off = jnp.minimum(i - begin_i, n_i - 1)
    chunk = (c_lo_i + off).astype(I32)
    first = valid & (i == begin_i)
    last = valid & (i == begin_i + n_i - 1)
    flag = (first * PAIR_FIRST + last * PAIR_LAST + valid * PAIR_VALID).astype(I32)
    return chunk, blk, flag


def kernel(x_prompt, x_sample, c, cache_k, cache_v, c_ctx, w_mod, b_mod, g_pre_mix, g_post_mix, g_pre_ffn, g_post_ffn, w_in, lam_q1, lam_k1, lam_q2, lam_k2, g_subln, w_proj_attn, w_proj_fourier, w_out, w_router, w_gate, w_up, w_down):
    assert w_mod.shape[0] == 1
    lam_init = 0.8 - 0.6 * math.exp(-0.3 * 0)
    bp, sp, _ = x_prompt.shape
    bs, ss, _ = x_sample.shape

    cond8 = jnp.concatenate([c_ctx[None, :], c, jnp.zeros((8 - 1 - bs, D_MODEL), F32)], axis=0)
    mod3 = _modulation(cond8, w_mod[0], b_mod).reshape(8, 1, N_MOD * D_MODEL)

    w_in_b = w_in[0].astype(BF16)
    wpa = w_proj_attn[0].astype(BF16)
    wpf = w_proj_fourier[0].astype(BF16)
    wout = w_out[0].astype(BF16)
    wr = w_router[0].astype(BF16)
    wrx = jnp.concatenate([wr, jnp.zeros((D_MODEL, LANES - N_EXPERTS), BF16)], axis=1)
    lam_p = jnp.concatenate([lam_q1, lam_k1, lam_q2, lam_k2], axis=0)

    groups = []
    for x, seq, positional, ctx in ((x_prompt, sp, False, None),
                                    (x_sample, ss, True, (cache_k, cache_v))):
        nb = x.shape[0]
        t = nb * seq
        x2d = x.reshape(t, D_MODEL)
        if positional:
            mod_row = lambda i, tm, seq=seq: 1 + (i * tm) // seq
        else:
            mod_row = lambda i, tm: 0
        self_contained = ctx is None and PRE_BLOCK % seq == 0
        pre = _pre_mixer(x2d, mod3, mod_row, g_pre_mix, w_in_b, _rope_tables(seq) if positional else None, seq,
                         self_contained, (lam_p, g_subln, lam_init), tm=PRE_BLOCK if self_contained else ROW_BLOCK)
        if self_contained:
            o, fm, ga, gf = pre[:4]
        else:
            q, k, v, f, ga, gf = pre
            o = _attention(lam_p, g_subln, q, k, v, ctx, seq, lam_init)
            fm = _fourier(f, seq)
        x1, h2t, aff_t = _post_mixer(o, fm, ga, gf, x2d, mod3, mod_row, g_post_mix, g_pre_ffn,
                                     wpa, wpf, wout, wrx)
        cap = 2 * t // N_EXPERTS
        assert t <= 1 << TOKEN_BITS
        posm, pack, span, ranges, rows = _route(aff_t, cap)
        idx, qslot = _slot_lists(ranges, posm, pack, cap)
        groups.append(dict(x1=x1, h2t=h2t, idx=idx, qslot=qslot, span=span, pairs=_combine_pairs(rows, t),
                           mod_row=mod_row, cache=pre[4:] if self_contained else None, shape=x.shape))

    gc, gs_ = groups
    zc, zs = _moe(gc["idx"], gs_["idx"], gc["qslot"], gs_["qslot"], gc["h2t"], gs_["h2t"], wrx,
                  w_gate[0], w_up[0], w_down[0])
    outs = []
    for g, z in ((gc, zc), (gs_, zs)):
        out = _combine(g["pairs"], z, g["span"], g["x1"], mod3, g["mod_row"], g_post_ffn)
        outs.append(out.reshape(g["shape"]))
    new_k, new_v = gc["cache"]
    return (outs[0], outs[1], new_k, new_v)
```

```python
import functools
import math

import numpy as np
import jax
import jax.numpy as jnp
from jax import lax
from jax.experimental import pallas as pl
from jax.experimental.pallas import tpu as pltpu

F32 = jnp.float32
BF16 = jnp.bfloat16
I32 = jnp.int32

D_MODEL = 1024
N_HEADS = 6
HEAD_DIM = 64
V_DIM = 128
QK_W = 768
FOUR_W = 256
FOUR_G = 64
IN_W = 4608
N_EXPERTS = 16
D_FF = 2816
N_MOD = 6
GRID_W = 64
ROPE_THETA = 10000.0
EPS = 1e-6
SUBLN_EPS = 1e-5

LANES = 128
ROW_BLOCK = 256
PRE_BLOCK = 512
GATE_CHUNK = 256
POST_BLOCK = 1024
POST_SUB = 256
TOK_BLOCK = 256
ROW_CHUNK = 256
FF_TILE = 256
FF_STEPS = D_FF // FF_TILE
ROW_TILE = D_MODEL // LANES
TOKEN_BITS = 13
VMEM_LIMIT = 56 * 1024 * 1024


def _cparams(sem):
    return pltpu.CompilerParams(dimension_semantics=sem, vmem_limit_bytes=VMEM_LIMIT)


def _dot(a, b):
    return jnp.dot(a, b, preferred_element_type=F32)


def _rms(x, g, eps):
    return x * lax.rsqrt(jnp.mean(x * x, axis=-1, keepdims=True) + eps) * g


def _whole_vmem():
    return pl.BlockSpec(memory_space=pltpu.MemorySpace.VMEM)


def _mod_kernel(c_ref, w_ref, b_ref, o_ref):
    c = c_ref[...]
    s = c * jax.nn.sigmoid(c)
    o_ref[...] = _dot(s.astype(BF16), w_ref[...].astype(BF16)) + b_ref[...]


def _modulation(cond8, w_mod, b_mod):
    tn = 1024
    n = N_MOD * D_MODEL
    return pl.pallas_call(
        _mod_kernel,
        grid=(n // tn,),
        in_specs=[pl.BlockSpec((8, D_MODEL), lambda j: (0, 0)),
                  pl.BlockSpec((D_MODEL, tn), lambda j: (0, j)),
                  pl.BlockSpec((1, tn), lambda j: (0, j))],
        out_specs=pl.BlockSpec((8, tn), lambda j: (0, j)),
        out_shape=jax.ShapeDtypeStruct((8, n), F32),
        compiler_params=_cparams(("arbitrary",)),
        name="modulation",
    )(cond8, w_mod, b_mod)


def _diff_lambda(lp, lam_init):
    s1 = jnp.sum(lp[0:1] * lp[1:2], axis=-1, keepdims=True)
    s2 = jnp.sum(lp[2:3] * lp[3:4], axis=-1, keepdims=True)
    return jnp.exp(s1) - jnp.exp(s2) + lam_init


def _diff_attention_head(q, k, v, lam, g_subln, lam_init):
    comp1 = lax.broadcasted_iota(I32, (1, V_DIM), 1) < HEAD_DIM
    scale = jnp.asarray(HEAD_DIM ** -0.5, BF16)
    v_ones = jnp.concatenate([v, jnp.ones_like(v)], axis=1)

    def attend(qc):
        s = lax.dot_general(qc, k, (((1,), (1,)), ((), ())), preferred_element_type=F32)
        ex = jnp.exp(s - jnp.max(s, axis=-1, keepdims=True)).astype(BF16)
        ov = _dot(ex, v_ones)
        return ov[:, 0:V_DIM] / ov[:, V_DIM:2 * V_DIM]

    qs = q * scale
    zero = jnp.zeros_like(qs)
    o = attend(jnp.where(comp1, qs, zero)) - lam * attend(jnp.where(comp1, zero, qs))
    return (_rms(o, g_subln, SUBLN_EPS) * (1.0 - lam_init)).astype(BF16)


def _rope(z, cos, sin_signed, first_half):
    fwd = pltpu.roll(z, QK_W - 16, axis=1)
    bwd = pltpu.roll(z, 16, axis=1)
    return z * cos + jnp.where(first_half, fwd, bwd) * sin_signed


def _pre_kernel(*refs, positional, fuse_seq, lam_init):
    it = iter(refs)
    x_ref, mod_ref, g_ref, w_ref = next(it), next(it), next(it), next(it)
    if positional:
        cos_ref, sin_ref = next(it), next(it)
    if fuse_seq:
        dft_refs = [next(it) for _ in range(4)]
        lam_ref, gs_ref = next(it), next(it)
        o_ref, f_ref, ga_ref, gf_ref, kc_ref, vc_ref = (next(it) for _ in range(6))
    else:
        q_ref, k_ref, v_ref, f_ref, ga_ref, gf_ref = (next(it) for _ in range(6))

    m = mod_ref[0]
    sh1 = m[:, 0:D_MODEL]
    sc1 = m[:, D_MODEL:2 * D_MODEL]
    h = _rms(x_ref[...], g_ref[...], EPS) * (1.0 + sc1) + sh1
    hb = h.astype(BF16)

    def proj(lo, hi):
        return _dot(hb, w_ref[:, lo:hi])

    zq = proj(0, QK_W)
    zk = proj(QK_W, 2 * QK_W)
    zv = proj(2 * QK_W, 3 * QK_W)
    if positional:
        lane = lax.broadcasted_iota(I32, (1, QK_W), 1)
        first_half = (lane % 32) < 16
        cos = cos_ref[...]
        sin_signed = sin_ref[...]
        zq = _rope(zq, cos, sin_signed, first_half)
        zk = _rope(zk, cos, sin_signed, first_half)
    qb, kb, vb = zq.astype(BF16), zk.astype(BF16), zv.astype(BF16)
    f0 = 3 * QK_W
    g0 = f0 + FOUR_W

    def gate_chunk(ref, lo, c):
        cols = slice(c * GATE_CHUNK, (c + 1) * GATE_CHUNK)
        ref[:, cols] = jax.nn.sigmoid(proj(lo + cols.start, lo + cols.stop)).astype(BF16)

    gate_work = [functools.partial(gate_chunk, ref, lo, c)
                 for ref, lo in ((ga_ref, g0), (gf_ref, g0 + D_MODEL)) for c in range(D_MODEL // GATE_CHUNK)]
    if not fuse_seq:
        q_ref[...] = qb
        k_ref[...] = kb
        v_ref[...] = vb
        f_ref[...] = proj(f0, g0)
        for work in gate_work:
            work()
        return

    seqs = [slice(b * fuse_seq, (b + 1) * fuse_seq) for b in range(x_ref.shape[0] // fuse_seq)]
    f = proj(f0, g0)
    lam = _diff_lambda(lam_ref[...], lam_init)
    for b, rs in enumerate(seqs):
        f_ref[rs, :] = _dft_real(f[rs, :], *dft_refs)
        for hd in range(N_HEADS):
            sl = slice(hd * V_DIM, (hd + 1) * V_DIM)
            kc_ref[b, 0, hd] = zk[rs, sl]
            vc_ref[b, 0, hd] = zv[rs, sl]
            o_ref[rs, sl] = _diff_attention_head(qb[rs, sl], kb[rs, sl], vb[rs, sl], lam, gs_ref[...], lam_init)
            if gate_work:
                gate_work.pop(0)()
    for work in gate_work:
        work()


def _pre_mixer(x2d, mod3, mod_row, g_pre, w_in_b, rope_tabs, seq, self_contained, attn_params, tm):
    t = x2d.shape[0]
    positional = rope_tabs is not None
    lam_p, g_subln, lam_init = attn_params
    if self_contained:
        assert tm % seq == 0
    blocks_per_seq = max(seq // tm, 1)
    row = lambda i: (i, 0)
    in_specs = [pl.BlockSpec((tm, D_MODEL), row),
                pl.BlockSpec((1, 1, N_MOD * D_MODEL), lambda i: (mod_row(i, tm), 0, 0)),
                pl.BlockSpec((1, D_MODEL), lambda i: (0, 0)),
                _whole_vmem()]
    args = [x2d, mod3, g_pre, w_in_b]
    if positional:
        in_specs += [pl.BlockSpec((tm, QK_W), lambda i: (i % blocks_per_seq, 0))] * 2
        args += list(rope_tabs)
    if self_contained:
        consts = _dft_consts(seq) + (lam_p, g_subln)
        in_specs += [pl.BlockSpec(c.shape, lambda i: (0, 0)) for c in consts]
        args += list(consts)
    n_wide = 1 if self_contained else 3
    out_shape = [jax.ShapeDtypeStruct((t, QK_W), BF16)] * n_wide + [
        jax.ShapeDtypeStruct((t, FOUR_W), BF16 if self_contained else F32),
        jax.ShapeDtypeStruct((t, D_MODEL), BF16),
        jax.ShapeDtypeStruct((t, D_MODEL), BF16)]
    out_specs = [pl.BlockSpec((tm, QK_W), row)] * n_wide + [
        pl.BlockSpec((tm, FOUR_W), row),
        pl.BlockSpec((tm, D_MODEL), row),
        pl.BlockSpec((tm, D_MODEL), row)]
    if self_contained:
        nb = t // seq
        cshape = jax.ShapeDtypeStruct((nb, 1, N_HEADS, seq, V_DIM), F32)
        cspec = pl.BlockSpec((tm // seq, 1, N_HEADS, seq, V_DIM), lambda i: (i, 0, 0, 0, 0))
        out_shape += [cshape, cshape]
        out_specs += [cspec, cspec]
    return pl.pallas_call(
        functools.partial(_pre_kernel, positional=positional, fuse_seq=seq if self_contained else 0,
                          lam_init=lam_init),
        grid=(t // tm,),
        in_specs=in_specs,
        out_specs=out_specs,
        out_shape=out_shape,
        compiler_params=_cparams(("arbitrary",)),
        name="pre_mixer",
    )(*args)


def _attn_kernel(*refs, lam_init, has_ctx):
    it = iter(refs)
    lam_ref, gs_ref, q_ref, k_ref, v_ref = (next(it) for _ in range(5))
    if has_ctx:
        ck_ref, cv_ref = next(it), next(it)
    o_ref = next(it)

    lam = _diff_lambda(lam_ref[...], lam_init)
    for hd in range(N_HEADS):
        sl = slice(hd * V_DIM, (hd + 1) * V_DIM)
        k = k_ref[:, sl]
        v = v_ref[:, sl]
        if has_ctx:
            k = jnp.concatenate([ck_ref[0, 0, hd].astype(BF16), k], axis=0)
            v = jnp.concatenate([cv_ref[0, 0, hd].astype(BF16), v], axis=0)
        o_ref[:, sl] = _diff_attention_head(q_ref[:, sl], k, v, lam, gs_ref[...], lam_init)


def _attention(lam_p, g_subln, q, k, v, ctx, seq, lam_init):
    t = q.shape[0]
    tq = ROW_BLOCK
    qb = seq // tq
    has_ctx = ctx is not None
    in_specs = [pl.BlockSpec((4, HEAD_DIM), lambda b, i: (0, 0)),
                pl.BlockSpec((1, V_DIM), lambda b, i: (0, 0)),
                pl.BlockSpec((tq, QK_W), lambda b, i: (b * qb + i, 0)),
                pl.BlockSpec((seq, QK_W), lambda b, i: (b, 0)),
                pl.BlockSpec((seq, QK_W), lambda b, i: (b, 0))]
    args = [lam_p, g_subln, q, k, v]
    if has_ctx:
        past = ctx[0].shape[3]
        cspec = pl.BlockSpec((1, 1, N_HEADS, past, V_DIM), lambda b, i: (b, 0, 0, 0, 0))
        in_specs += [cspec, cspec]
        args += list(ctx)
    return pl.pallas_call(
        functools.partial(_attn_kernel, lam_init=lam_init, has_ctx=has_ctx),
        grid=(t // seq, qb),
        in_specs=in_specs,
        out_specs=pl.BlockSpec((tq, QK_W), lambda b, i: (b * qb + i, 0)),
        out_shape=jax.ShapeDtypeStruct((t, QK_W), BF16),
        compiler_params=_cparams(("arbitrary", "arbitrary")),
        name="diff_attention",
    )(*args)


def _dft_real(f, bc_ref, bs_ref, cs_ref, ss_ref):
    fb = f.astype(BF16)
    u = _dot(fb, bc_ref[...].astype(BF16)).astype(BF16)
    w = _dot(fb, bs_ref[...].astype(BF16)).astype(BF16)
    return (_dot(cs_ref[...].astype(BF16), u) - _dot(ss_ref[...].astype(BF16), w)).astype(BF16)


def _fourier_kernel(f_ref, bc_ref, bs_ref, cs_ref, ss_ref, o_ref):
    o_ref[...] = _dft_real(f_ref[...], bc_ref, bs_ref, cs_ref, ss_ref)


def _dft_consts(seq):
    c = np.arange(FOUR_G)
    ang_c = 2.0 * np.pi * ((c[:, None] * c[None, :]) % FOUR_G) / FOUR_G
    eye = np.eye(FOUR_W // FOUR_G)
    bc = np.kron(eye, np.cos(ang_c)) / math.sqrt(FOUR_G)
    bs = np.kron(eye, np.sin(ang_c)) / math.sqrt(FOUR_G)
    s = np.arange(seq)
    ang_s = 2.0 * np.pi * ((s[:, None] * s[None, :]) % seq) / seq
    cs = np.cos(ang_s) / math.sqrt(seq)
    ss = np.sin(ang_s) / math.sqrt(seq)
    return tuple(jnp.asarray(a, dtype=F32) for a in (bc, bs, cs, ss))


def _fourier(f, seq):
    t = f.shape[0]
    bc, bs, cs, ss = _dft_consts(seq)
    const = lambda b: (0, 0)
    return pl.pallas_call(
        _fourier_kernel,
        grid=(t // seq,),
        in_specs=[pl.BlockSpec((seq, FOUR_W), lambda b: (b, 0)),
                  pl.BlockSpec((FOUR_W, FOUR_W), const),
                  pl.BlockSpec((FOUR_W, FOUR_W), const),
                  pl.BlockSpec((seq, seq), const),
                  pl.BlockSpec((seq, seq), const)],
        out_specs=pl.BlockSpec((seq, FOUR_W), lambda b: (b, 0)),
        out_shape=jax.ShapeDtypeStruct((t, FOUR_W), BF16),
        compiler_params=_cparams(("arbitrary",)),
        name="fourier_mix",
    )(f, bc, bs, cs, ss)


def _post_kernel(o_ref, fm_ref, ga_ref, gf_ref, x_ref, mod_ref, gpost_ref, gffn_ref,
                 wpa_ref, wpf_ref, wout_ref, wrx_ref,
                 x1_ref, h2t_ref, afft_ref):
    m = mod_ref[0]
    gt1 = m[:, 2 * D_MODEL:3 * D_MODEL]
    sh2 = m[:, 3 * D_MODEL:4 * D_MODEL]
    sc2 = m[:, 4 * D_MODEL:5 * D_MODEL]
    subs = [slice(r0, r0 + POST_SUB) for r0 in range(0, o_ref.shape[0], POST_SUB)]
    ab = [(_dot(o_ref[rs, :], wpa_ref[...]), _dot(fm_ref[rs, :], wpf_ref[...])) for rs in subs]
    merged = [(ga_ref[rs, :] * a + gf_ref[rs, :] * b).astype(BF16) for rs, (a, b) in zip(subs, ab)]
    ys = [_dot(mg, wout_ref[...]) for mg in merged]
    h2s = []
    for rs, y in zip(subs, ys):
        x1 = x_ref[rs, :] + gt1 * _rms(y, gpost_ref[...], EPS)
        x1_ref[rs, :] = x1
        h2s.append(_rms(x1, gffn_ref[...], EPS) * (1.0 + sc2) + sh2)
    logits = [_dot(h2.astype(BF16), wrx_ref[...]) for h2 in h2s]
    for rs, h2, lg in zip(subs, h2s, logits):
        lt = lg.T[0:N_EXPERTS]
        et = jnp.exp(lt - jnp.max(lt, axis=0, keepdims=True))
        aff = et / jnp.sum(et, axis=0, keepdims=True)
        for u in range(POST_SUB // LANES):
            afft_ref[:, rs.start // LANES + u, :] = aff[:, u * LANES:(u + 1) * LANES]
        for kc in range(ROW_TILE):
            h2t_ref[pl.ds(rs.start * ROW_TILE + kc, POST_SUB, stride=ROW_TILE), :] = h2[:, kc * LANES:(kc + 1) * LANES]


def _post_mixer(o, fm, ga, gf, x2d, mod3, mod_row, g_post, g_ffn, wpa, wpf, wout, wrx):
    t = x2d.shape[0]
    tm = POST_BLOCK
    row = lambda i: (i, 0)
    const = lambda i: (0, 0)
    return pl.pallas_call(
        _post_kernel,
        grid=(t // tm,),
        in_specs=[pl.BlockSpec((tm, QK_W), row),
                  pl.BlockSpec((tm, FOUR_W), row),
                  pl.BlockSpec((tm, D_MODEL), row),
                  pl.BlockSpec((tm, D_MODEL), row),
                  pl.BlockSpec((tm, D_MODEL), row),
                  pl.BlockSpec((1, 1, N_MOD * D_MODEL), lambda i: (mod_row(i, tm), 0, 0)),
                  pl.BlockSpec((1, D_MODEL), const),
                  pl.BlockSpec((1, D_MODEL), const),
                  pl.BlockSpec((QK_W, D_MODEL), const),
                  pl.BlockSpec((FOUR_W, D_MODEL), const),
                  pl.BlockSpec((D_MODEL, D_MODEL), const),
                  pl.BlockSpec((D_MODEL, LANES), const)],
        out_specs=[pl.BlockSpec((tm, D_MODEL), row),
                   pl.BlockSpec((tm * ROW_TILE, LANES), row),
                   pl.BlockSpec((N_EXPERTS, tm // LANES, LANES), lambda i: (0, i, 0))],
        out_shape=[jax.ShapeDtypeStruct((t, D_MODEL), F32),
                   jax.ShapeDtypeStruct((t * ROW_TILE, LANES), F32),
                   jax.ShapeDtypeStruct((N_EXPERTS, t // LANES, LANES), F32)],
        compiler_params=_cparams(("arbitrary",)),
        name="post_mixer",
    )(o, fm, ga, gf, x2d, mod3, g_post, g_ffn, wpa, wpf, wout, wrx)


RANGE_ROWS = 8


def _route_kernel(aff_ref, posm_ref, pack_ref, span_ref, ranges_ref, rows_ref, *, cap, n_tok):
    aff = aff_ref[...]
    nt = n_tok // LANES
    capf = float(cap)

    def count_ge(v):
        return jnp.sum(jnp.where(aff >= v, 1.0, 0.0), axis=(1, 2), keepdims=True)

    def search(i, thr):
        cand = thr | jnp.left_shift(jnp.int32(1), 30 - i)
        return jnp.where(count_ge(pltpu.bitcast(cand, F32)) >= capf, cand, thr)

    thr = lax.fori_loop(0, 31, search, jnp.zeros((N_EXPERTS, 1, 1), I32))
    lo = pltpu.bitcast(thr, F32)
    hi = pltpu.bitcast(thr + 1, F32)

    def refine(i, c):
        lo, hi = c
        mid = lo + (hi - lo) * 0.5
        ok = count_ge(mid) >= capf
        return jnp.where(ok, mid, lo), jnp.where(ok, hi, mid)

    lo, hi = lax.fori_loop(0, 12, refine, (lo, hi))
    gt = aff >= hi
    eq = (aff >= lo) & (aff < hi)
    n_tie = capf - jnp.sum(jnp.where(gt, 1.0, 0.0), axis=(1, 2), keepdims=True)

    sq0 = lax.broadcasted_iota(I32, (LANES, LANES), 0)
    sq1 = lax.broadcasted_iota(I32, (LANES, LANES), 1)
    along_total = jnp.concatenate([jnp.where(sq0 <= sq1, 1.0, 0.0), jnp.ones((LANES, LANES), F32)],
                                  axis=1).astype(BF16)
    m = N_EXPERTS * nt
    r0 = lax.broadcasted_iota(I32, (m, m), 0)
    r1 = lax.broadcasted_iota(I32, (m, m), 1)
    earlier = jnp.where((r0 // nt == r1 // nt) & (r1 < r0), 1.0, 0.0).astype(BF16)
    lane = lax.broadcasted_iota(I32, (1, LANES), 1)
    token = lax.broadcasted_iota(I32, (nt, LANES), 0) * LANES + lane

    def tile_counts(x):
        both = _dot(x.reshape(m, LANES).astype(BF16), along_total)
        total = both[:, LANES:]
        before = _dot(earlier, total.astype(BF16))
        shape = (N_EXPERTS, nt, LANES)
        return both[:, :LANES].reshape(shape), total.reshape(shape), before.reshape(shape)

    eq_f = jnp.where(eq, 1.0, 0.0)
    eq_along, _, eq_before = tile_counts(eq_f)
    sel = jnp.where(gt, 1.0, jnp.where(eq_along + eq_before <= n_tie, eq_f, 0.0))
    sel_along, sel_total, sel_before = tile_counts(sel)
    posm_ref[...] = jnp.where(sel > 0.5, sel_along + sel_before - sel, -1.0).astype(I32)

    cnt = jnp.sum(sel, axis=0)
    rows_before = jnp.sum(sel_before, axis=0)
    tok_start = _dot(cnt.astype(BF16), along_total[:, :LANES]) - cnt + rows_before
    k = jnp.zeros((nt, LANES), F32)
    slots_before = jnp.zeros((nt, LANES), F32)
    slots_here = jnp.zeros((nt, LANES), F32)
    for e in range(N_EXPERTS):
        pack_ref[e] = (tok_start + k).astype(I32) * (1 << TOKEN_BITS) + token
        k = k + sel[e]
        slots_before = jnp.where(lane == e, sel_before[e], slots_before)
        slots_here = jnp.where(lane == e, sel_total[e], slots_here)

    eye = jnp.where(sq0 == sq1, 1.0, 0.0).astype(BF16)
    tok_end = tok_start + cnt
    parts = []
    for v in (tok_start, tok_end):
        high = jnp.floor(v * (1.0 / LANES))
        parts += [high, v - high * LANES]
    for c in range(nt):
        rows4 = jnp.zeros((LANES, LANES), F32)
        for r, part in enumerate(parts):
            rows4 = jnp.where(sq0 == r, jnp.broadcast_to(part[c:c + 1, :], (LANES, LANES)), rows4)
        span_ref[c * LANES:(c + 1) * LANES, :] = lax.dot_general(
            eye, rows4.astype(BF16), (((1,), (1,)), ((), ())), preferred_element_type=F32)

    ranges_ref[...] = jnp.zeros_like(ranges_ref)
    for s in range(cap // LANES):
        done = jnp.where(slots_before + slots_here <= float(s * LANES), 1.0, 0.0)
        begun = jnp.where(slots_before < float((s + 1) * LANES), 1.0, 0.0)
        ranges_ref[s:s + 1, :] = jnp.sum(done, axis=0, keepdims=True).astype(I32)
        ranges_ref[RANGE_ROWS + s:RANGE_ROWS + s + 1, :] = jnp.sum(begun, axis=0, keepdims=True).astype(I32)
    rows_ref[...] = rows_before.astype(I32)


def _route(aff, cap):
    n_e, nt, _ = aff.shape
    n_tok = nt * LANES
    assert cap // LANES <= RANGE_ROWS
    return pl.pallas_call(
        functools.partial(_route_kernel, cap=cap, n_tok=n_tok),
        out_shape=[jax.ShapeDtypeStruct((n_e, nt, LANES), I32),
                   jax.ShapeDtypeStruct((n_e, nt, LANES), I32),
                   jax.ShapeDtypeStruct((n_tok, LANES), F32),
                   jax.ShapeDtypeStruct((2 * RANGE_ROWS, LANES), I32),
                   jax.ShapeDtypeStruct((nt, LANES), I32)],
        compiler_params=pltpu.CompilerParams(vmem_limit_bytes=VMEM_LIMIT),
        name="route",
    )(aff)


def _slots_kernel(ranges_ref, posm_ref, pack_ref, idx_ref, qslot_ref, acc_ref):
    e = pl.program_id(0)
    sub = lax.broadcasted_iota(I32, (LANES, LANES), 0)
    eye = sub == lax.broadcasted_iota(I32, (LANES, LANES), 1)
    n_tiles = idx_ref.shape[1]

    for s in range(n_tiles):
        slot = sub + s * LANES

        def body(c, acc, slot=slot):
            hit = posm_ref[e, pl.ds(c, 1), :] == slot
            return acc + jnp.where(hit, pack_ref[e, pl.ds(c, 1), :], 0)

        acc_ref[s] = lax.fori_loop(ranges_ref[s, e], ranges_ref[RANGE_ROWS + s, e], body,
                                   jnp.zeros((LANES, LANES), I32))

    def as_row(part):
        col = jnp.sum(part.astype(F32), axis=1, keepdims=True)
        return jnp.sum(jnp.where(eye, col, 0.0), axis=0, keepdims=True).astype(I32)

    for s in range(n_tiles):
        acc = acc_ref[s]
        idx_ref[0, s:s + 1, :] = as_row(acc & ((1 << TOKEN_BITS) - 1)) * ROW_TILE
        qslot_ref[0, s:s + 1, :] = as_row(lax.shift_right_logical(acc, TOKEN_BITS)) * ROW_TILE


def _slot_lists(ranges, posm4, qdst4, cap):
    ns = cap // LANES
    grid_spec = pltpu.PrefetchScalarGridSpec(
        num_scalar_prefetch=1,
        grid=(N_EXPERTS,),
        in_specs=[_whole_vmem(), _whole_vmem()],
        out_specs=[pl.BlockSpec((1, ns, LANES), lambda e, *_: (e, 0, 0)),
                   pl.BlockSpec((1, ns, LANES), lambda e, *_: (e, 0, 0))],
        scratch_shapes=[pltpu.VMEM((ns, LANES, LANES), I32)])
    idx, qslot = pl.pallas_call(
        _slots_kernel,
        grid_spec=grid_spec,
        out_shape=[jax.ShapeDtypeStruct((N_EXPERTS, ns, LANES), I32),
                   jax.ShapeDtypeStruct((N_EXPERTS, ns, LANES), I32)],
        compiler_params=_cparams(("arbitrary",)),
        name="slot_lists",
    )(ranges, posm4, qdst4)
    return idx.reshape(N_EXPERTS, cap), qslot.reshape(N_EXPERTS, cap)


def _moe_kernel(idxc_ref, idxs_ref, qc_ref, qs_ref,
                hc_ref, hs_ref, wr_ref, wg_ref, wu_ref, wd_ref, zc_ref, zs_ref,
                xbuf, ybuf, xb_ref, gate_ref, acc_ref, gsem, ssem, *, capc, caps):
    e = pl.program_id(0)
    j = pl.program_id(1)
    n_e = pl.num_programs(0)
    n_j = FF_STEPS
    slot = e % 2
    other = 1 - slot
    rows = capc + caps
    gc, gs = _per_step(capc), _per_step(caps)
    groups = ((hc_ref, idxc_ref, zc_ref, qc_ref, gc, 0), (hs_ref, idxs_ref, zs_ref, qs_ref, gs, gc * n_j))

    def tile(ref, first_sublane):
        return ref.at[pl.ds(pl.multiple_of(first_sublane, ROW_TILE), ROW_TILE), :]

    def gather(ex, sl, step, i, group):
        h_ref, idx_ref, _, _, per_step, base = group
        p = step * per_step + i
        src = tile(h_ref, idx_ref[ex * (per_step * n_j) + p])
        pltpu.make_async_copy(src, tile(xbuf.at[sl], (base + p) * ROW_TILE), gsem.at[sl]).start()

    def scatter(table_row, sl, step, i, group):
        _, _, z_ref, q_ref, per_step, base = group
        p = step * per_step + i
        dst = tile(z_ref, q_ref[table_row * (per_step * n_j) + p])
        pltpu.make_async_copy(tile(ybuf.at[sl], (base + p) * ROW_TILE), dst, ssem.at[sl]).start()

    def all_steps(fn):
        for group in groups:
            def body(p, carry, group=group):
                fn(p, group)
                return carry
            lax.fori_loop(0, group[4] * n_j, body, 0, unroll=8)

    def wait_all(buf, sem, sl):
        pltpu.make_async_copy(buf.at[sl], buf.at[sl], sem.at[sl]).wait()

    @pl.when((e == 0) & (j == 0))
    def _():
        ybuf[...] = jnp.zeros_like(ybuf)
        all_steps(lambda p, group: gather(0, 0, 0, p, group))

    @pl.when(j == 0)
    def _():
        wait_all(xbuf, gsem, slot)
        for base, n, dst in ((0, capc, 0), (gc * n_j, caps, capc)):
            for kc in range(ROW_TILE):
                tiles = xbuf[slot, pl.ds(base * ROW_TILE + kc, n, stride=ROW_TILE), :]
                xb_ref[dst:dst + n, kc * LANES:(kc + 1) * LANES] = tiles.astype(BF16)
        acc_ref[...] = jnp.zeros_like(acc_ref)
        logits = _dot(xb_ref[...], wr_ref[...])
        lane = lax.broadcasted_iota(I32, (1, LANES), 1)
        is_expert = lane < N_EXPERTS
        ex = jnp.exp(logits - jnp.max(jnp.where(is_expert, logits, -jnp.inf), axis=-1, keepdims=True))
        mine = jnp.sum(jnp.where(lane == e, ex, 0.0), axis=-1, keepdims=True)
        gate = mine / jnp.sum(jnp.where(is_expert, ex, 0.0), axis=-1, keepdims=True)
        gate_ref[...] = jnp.broadcast_to(gate, gate_ref.shape)

    nxt = jnp.minimum(e + 1, n_e - 1)
    for group in groups:
        for i in range(group[4]):
            gather(nxt, other, j, i, group)
            scatter(e, other, j, i, group)

    x = xb_ref[...]
    g = _dot(x, wg_ref[0].astype(BF16))
    u = _dot(x, wu_ref[0].astype(BF16))
    hid = (g * jax.nn.sigmoid(g) * u).astype(BF16)
    acc_ref[...] += _dot(hid, wd_ref[0].astype(BF16))

    @pl.when(j == n_j - 1)
    def _():
        @pl.when(e >= 1)
        def _():
            wait_all(ybuf, ssem, slot)

        gate = gate_ref[...]
        for base, n, src in ((0, capc, 0), (gc * n_j, caps, capc)):
            for kc in range(ROW_TILE):
                y = acc_ref[src:src + n, kc * LANES:(kc + 1) * LANES] * gate[src:src + n]
                ybuf[slot, pl.ds(base * ROW_TILE + kc, n, stride=ROW_TILE), :] = y

        @pl.when(e == n_e - 1)
        def _():
            all_steps(lambda p, group: scatter(e + 1, slot, 0, p, group))
            wait_all(ybuf, ssem, other)
            wait_all(ybuf, ssem, slot)
            wait_all(xbuf, gsem, other)


def _per_step(cap):
    return -(-cap // FF_STEPS)


def _copy_tables(idx, qslot, n_rows):
    n_e, cap = idx.shape
    padded = _per_step(cap) * FF_STEPS
    n_pad = padded - cap
    idx_p = jnp.concatenate([idx, jnp.zeros((n_e, n_pad), I32)], axis=1)
    spare = n_rows + jnp.arange(padded + n_e * n_pad, dtype=I32)
    lead = spare[:padded][None, :]
    pad_rows = spare[padded:].reshape(n_e, n_pad)
    q_p = jnp.concatenate([lead, jnp.concatenate([qslot // ROW_TILE, pad_rows], axis=1)], axis=0) * ROW_TILE
    return idx_p.reshape(-1), q_p.reshape(-1), n_rows + padded + n_e * n_pad


def _moe(idxc, idxs, qc, qs, hc, hs, wrx, w_gate, w_up, w_down):
    capc, caps = idxc.shape[1], idxs.shape[1]
    rows = capc + caps
    tf = FF_TILE
    idxc, qc, zc_rows = _copy_tables(idxc, qc, N_EXPERTS * capc)
    idxs, qs, zs_rows = _copy_tables(idxs, qs, N_EXPERTS * caps)
    buf_rows = (_per_step(capc) + _per_step(caps)) * FF_STEPS
    any_spec = pl.BlockSpec(memory_space=pl.ANY)
    grid_spec = pltpu.PrefetchScalarGridSpec(
        num_scalar_prefetch=4,
        grid=(N_EXPERTS, FF_STEPS),
        in_specs=[any_spec, any_spec,
                  pl.BlockSpec((D_MODEL, LANES), lambda e, j, *_: (0, 0)),
                  pl.BlockSpec((1, D_MODEL, tf), lambda e, j, *_: (e, 0, j)),
                  pl.BlockSpec((1, D_MODEL, tf), lambda e, j, *_: (e, 0, j)),
                  pl.BlockSpec((1, tf, D_MODEL), lambda e, j, *_: (e, j, 0))],
        out_specs=[any_spec, any_spec],
        scratch_shapes=[pltpu.VMEM((2, buf_rows * ROW_TILE, LANES), F32),
                        pltpu.VMEM((2, buf_rows * ROW_TILE, LANES), F32),
                        pltpu.VMEM((rows, D_MODEL), BF16),
                        pltpu.VMEM((rows, LANES), F32),
                        pltpu.VMEM((rows, D_MODEL), F32),
                        pltpu.SemaphoreType.DMA((2,)),
                        pltpu.SemaphoreType.DMA((2,))])
    return pl.pallas_call(
        functools.partial(_moe_kernel, capc=capc, caps=caps),
        grid_spec=grid_spec,
        out_shape=[jax.ShapeDtypeStruct((zc_rows * ROW_TILE, LANES), F32),
                   jax.ShapeDtypeStruct((zs_rows * ROW_TILE, LANES), F32)],
        compiler_params=_cparams(("arbitrary", "arbitrary")),
        name="expert_ffn",
    )(idxc, idxs, qc, qs, hc, hs, wrx, w_gate, w_up, w_down)


PAIR_FIRST, PAIR_LAST, PAIR_VALID = 1, 2, 4


def _combine_kernel(chunk_ref, blk_ref, flag_ref, z_ref, span_ref, x1_ref, mod_ref, g_ref, o_ref, acc_ref):
    i = pl.program_id(0)
    flag = flag_ref[i]

    @pl.when((flag & PAIR_FIRST) != 0)
    def _():
        acc_ref[...] = jnp.zeros_like(acc_ref)

    @pl.when((flag & PAIR_VALID) != 0)
    def _():
        row = (chunk_ref[i] * ROW_CHUNK + lax.broadcasted_iota(I32, (TOK_BLOCK, ROW_CHUNK), 1)).astype(F32)
        first_row = span_ref[:, 0:1] * LANES + span_ref[:, 1:2]
        end_row = span_ref[:, 2:3] * LANES + span_ref[:, 3:4]
        onehot = jnp.where((row >= first_row) & (row < end_row), 1.0, 0.0).astype(BF16)
        y = jnp.concatenate([z_ref[pl.ds(kc, ROW_CHUNK, stride=ROW_TILE), :] for kc in range(ROW_TILE)], axis=1)
        acc_ref[...] += _dot(onehot, y.astype(BF16))

    @pl.when((flag & PAIR_LAST) != 0)
    def _():
        gt2 = mod_ref[0][:, 5 * D_MODEL:6 * D_MODEL]
        o_ref[...] = x1_ref[...] + gt2 * _rms(acc_ref[...], g_ref[...], EPS)


def _combine(pairs, z, span, x1, mod3, mod_row, g_post_ffn):
    chunk, blk, flag = pairs
    t = x1.shape[0]
    tb = TOK_BLOCK
    grid_spec = pltpu.PrefetchScalarGridSpec(
        num_scalar_prefetch=3,
        grid=(chunk.shape[0],),
        in_specs=[pl.BlockSpec((ROW_CHUNK * ROW_TILE, LANES), lambda i, c, b, f: (c[i], 0)),
                  pl.BlockSpec((tb, LANES), lambda i, c, b, f: (b[i], 0)),
                  pl.BlockSpec((tb, D_MODEL), lambda i, c, b, f: (b[i], 0)),
                  pl.BlockSpec((1, 1, N_MOD * D_MODEL), lambda i, c, b, f: (mod_row(b[i], tb), 0, 0)),
                  pl.BlockSpec((1, D_MODEL), lambda i, c, b, f: (0, 0))],
        out_specs=pl.BlockSpec((tb, D_MODEL), lambda i, c, b, f: (b[i], 0)),
        scratch_shapes=[pltpu.VMEM((tb, D_MODEL), F32)])
    return pl.pallas_call(
        _combine_kernel,
        grid_spec=grid_spec,
        out_shape=jax.ShapeDtypeStruct((t, D_MODEL), F32),
        compiler_params=_cparams(("arbitrary",)),
        name="combine",
    )(chunk, blk, flag, z, span, x1, mod3, g_post_ffn)


def _rope_tables(seq):
    half = HEAD_DIM // 4
    freqs = ROPE_THETA ** (-np.arange(half, dtype=np.float64) / half)
    s = np.arange(seq)
    row = (s // GRID_W)[:, None] * freqs[None, :]
    col = (s % GRID_W)[:, None] * freqs[None, :]
    ang = np.concatenate([row, row, col, col], axis=1)
    ang = np.tile(ang, (1, QK_W // HEAD_DIM))
    lane = np.arange(QK_W)
    sign = np.where((lane % 32) < 16, -1.0, 1.0)[None, :]
    return (jnp.asarray(np.cos(ang), dtype=F32), jnp.asarray(np.sin(ang) * sign, dtype=F32))


def _combine_pairs(rows, n_tok):
    step = TOK_BLOCK // LANES
    nb = n_tok // TOK_BLOCK
    nc = 2 * n_tok // ROW_CHUNK
    lo = rows[0:nb * step:step, 0]
    hi = jnp.concatenate([lo[1:], jnp.full((1,), 2 * n_tok, I32)])
    c_lo = jnp.minimum(lo // ROW_CHUNK, nc - 1)
    c_hi = jnp.maximum((hi + ROW_CHUNK - 1) // ROW_CHUNK, c_lo + 1)
    n = c_hi - c_lo
    ends = jnp.cumsum(n)
    begins = ends - n
    i = jnp.arange(nb + nc, dtype=I32)
    valid = i < ends[-1]
    blk = jnp.minimum(jnp.sum(ends[None, :] <= i[:, None], axis=1), nb - 1).astype(I32)
    onehot = blk[:, None] == jnp.arange(nb, dtype=I32)[None, :]
    table = jnp.stack([begins, n, c_lo], axis=1)
    picked = jnp.sum(jnp.where(onehot[:, :, None], table[None, :, :], 0), axis=1)
    begin_i, n_i, c_lo_i = picked[:, 0], picked[:, 1], picked[:, 2]
    off = jnp.minimum(i - begin_i, n_i - 1)
    chunk = (c_lo_i + off).astype(I32)
    first = valid & (i == begin_i)
    last = valid & (i == begin_i + n_i - 1)
    flag = (first * PAIR_FIRST + last * PAIR_LAST + valid * PAIR_VALID).astype(I32)
    return chunk, blk, flag


def kernel(x_prompt, x_sample, c, cache_k, cache_v, c_ctx, w_mod, b_mod, g_pre_mix, g_post_mix, g_pre_ffn, g_post_ffn, w_in, lam_q1, lam_k1, lam_q2, lam_k2, g_subln, w_proj_attn, w_proj_fourier, w_out, w_router, w_gate, w_up, w_down):
    assert w_mod.shape[0] == 1
    lam_init = 0.8 - 0.6 * math.exp(-0.3 * 0)
    bp, sp, _ = x_prompt.shape
    bs, ss, _ = x_sample.shape

    cond8 = jnp.concatenate([c_ctx[None, :], c, jnp.zeros((8 - 1 - bs, D_MODEL), F32)], axis=0)
    mod3 = _modulation(cond8, w_mod[0], b_mod).reshape(8, 1, N_MOD * D_MODEL)

    w_in_b = w_in[0].astype(BF16)
    wpa = w_proj_attn[0].astype(BF16)
    wpf = w_proj_fourier[0].astype(BF16)
    wout = w_out[0].astype(BF16)
    wr = w_router[0].astype(BF16)
    wrx = jnp.concatenate([wr, jnp.zeros((D_MODEL, LANES - N_EXPERTS), BF16)], axis=1)
    lam_p = jnp.concatenate([lam_q1, lam_k1, lam_q2, lam_k2], axis=0)

    groups = []
    for x, seq, positional, ctx in ((x_prompt, sp, False, None),
                                    (x_sample, ss, True, (cache_k, cache_v))):
        nb = x.shape[0]
        t = nb * seq
        x2d = x.reshape(t, D_MODEL)
        if positional:
            mod_row = lambda i, tm, seq=seq: 1 + (i * tm) // seq
        else:
            mod_row = lambda i, tm: 0
        self_contained = ctx is None and PRE_BLOCK % seq == 0
        pre = _pre_mixer(x2d, mod3, mod_row, g_pre_mix, w_in_b, _rope_tables(seq) if positional else None, seq,
                         self_contained, (lam_p, g_subln, lam_init), tm=PRE_BLOCK if self_contained else ROW_BLOCK)
        if self_contained:
            o, fm, ga, gf = pre[:4]
        else:
            q, k, v, f, ga, gf = pre
            o = _attention(lam_p, g_subln, q, k, v, ctx, seq, lam_init)
            fm = _fourier(f, seq)
        x1, h2t, aff_t = _post_mixer(o, fm, ga, gf, x2d, mod3, mod_row, g_post_mix, g_pre_ffn,
                                     wpa, wpf, wout, wrx)
        cap = 2 * t // N_EXPERTS
        assert t <= 1 << TOKEN_BITS
        posm, pack, span, ranges, rows = _route(aff_t, cap)
        idx, qslot = _slot_lists(ranges, posm, pack, cap)
        groups.append(dict(x1=x1, h2t=h2t, idx=idx, qslot=qslot, span=span, pairs=_combine_pairs(rows, t),
                           mod_row=mod_row, cache=pre[4:] if self_contained else None, shape=x.shape))

    gc, gs_ = groups
    zc, zs = _moe(gc["idx"], gs_["idx"], gc["qslot"], gs_["qslot"], gc["h2t"], gs_["h2t"], wrx,
                  w_gate[0], w_up[0], w_down[0])
    outs = []
    for g, z in ((gc, zc), (gs_, zs)):
        out = _combine(g["pairs"], z, g["span"], g["x1"], mod3, g["mod_row"], g_post_ffn)
        outs.append(out.reshape(g["shape"]))
    new_k, new_v = gc["cache"]
    return (outs[0], outs[1], new_k, new_v)
```

```python
import functools
import math

import numpy as np
import jax
import jax.numpy as jnp
from jax import lax
from jax.experimental import pallas as pl
from jax.experimental.pallas import tpu as pltpu

F32 = jnp.float32
BF16 = jnp.bfloat16
I32 = jnp.int32

D_MODEL = 1024
N_HEADS = 6
HEAD_DIM = 64
V_DIM = 128
QK_W = 768
FOUR_W = 256
FOUR_G = 64
IN_W = 4608
N_EXPERTS = 16
D_FF = 2816
N_MOD = 6
GRID_W = 64
ROPE_THETA = 10000.0
EPS = 1e-6
SUBLN_EPS = 1e-5

LANES = 128
ROW_BLOCK = 256
PRE_BLOCK = 512
GATE_CHUNK = 256
POST_BLOCK = 1024
POST_SUB = 256
TOK_BLOCK = 256
ROW_CHUNK = 256
FF_TILE = 256
FF_STEPS = D_FF // FF_TILE
ROW_TILE = D_MODEL // LANES
TOKEN_BITS = 13
VMEM_LIMIT = 56 * 1024 * 1024


def _cparams(sem):
    return pltpu.CompilerParams(dimension_semantics=sem, vmem_limit_bytes=VMEM_LIMIT)


def _dot(a, b):
    return jnp.dot(a, b, preferred_element_type=F32)


def _rms(x, g, eps):
    return x * lax.rsqrt(jnp.mean(x * x, axis=-1, keepdims=True) + eps) * g


def _whole_vmem():
    return pl.BlockSpec(memory_space=pltpu.MemorySpace.VMEM)


def _mod_kernel(c_ref, w_ref, b_ref, o_ref):
    c = c_ref[...]
    s = c * jax.nn.sigmoid(c)
    o_ref[...] = _dot(s.astype(BF16), w_ref[...].astype(BF16)) + b_ref[...]


def _modulation(cond8, w_mod, b_mod):
    tn = 1024
    n = N_MOD * D_MODEL
    return pl.pallas_call(
        _mod_kernel,
        grid=(n // tn,),
        in_specs=[pl.BlockSpec((8, D_MODEL), lambda j: (0, 0)),
                  pl.BlockSpec((D_MODEL, tn), lambda j: (0, j)),
                  pl.BlockSpec((1, tn), lambda j: (0, j))],
        out_specs=pl.BlockSpec((8, tn), lambda j: (0, j)),
        out_shape=jax.ShapeDtypeStruct((8, n), F32),
        compiler_params=_cparams(("arbitrary",)),
        name="modulation",
    )(cond8, w_mod, b_mod)


def _diff_lambda(lp, lam_init):
    s1 = jnp.sum(lp[0:1] * lp[1:2], axis=-1, keepdims=True)
    s2 = jnp.sum(lp[2:3] * lp[3:4], axis=-1, keepdims=True)
    return jnp.exp(s1) - jnp.exp(s2) + lam_init


def _diff_attention_head(q, k, v, lam, g_subln, lam_init):
    comp1 = lax.broadcasted_iota(I32, (1, V_DIM), 1) < HEAD_DIM
    scale = jnp.asarray(HEAD_DIM ** -0.5, BF16)
    v_ones = jnp.concatenate([v, jnp.ones_like(v)], axis=1)

    def attend(qc):
        s = lax.dot_general(qc, k, (((1,), (1,)), ((), ())), preferred_element_type=F32)
        ex = jnp.exp(s - jnp.max(s, axis=-1, keepdims=True)).astype(BF16)
        ov = _dot(ex, v_ones)
        return ov[:, 0:V_DIM] / ov[:, V_DIM:2 * V_DIM]

    qs = q * scale
    zero = jnp.zeros_like(qs)
    o = attend(jnp.where(comp1, qs, zero)) - lam * attend(jnp.where(comp1, zero, qs))
    return (_rms(o, g_subln, SUBLN_EPS) * (1.0 - lam_init)).astype(BF16)


def _rope(z, cos, sin_signed, first_half):
    fwd = pltpu.roll(z, QK_W - 16, axis=1)
    bwd = pltpu.roll(z, 16, axis=1)
    return z * cos + jnp.where(first_half, fwd, bwd) * sin_signed


def _pre_kernel(*refs, positional, fuse_seq, lam_init):
    it = iter(refs)
    x_ref, mod_ref, g_ref, w_ref = next(it), next(it), next(it), next(it)
    if positional:
        cos_ref, sin_ref = next(it), next(it)
    if fuse_seq:
        dft_refs = [next(it) for _ in range(4)]
        lam_ref, gs_ref = next(it), next(it)
        o_ref, f_ref, ga_ref, gf_ref, kc_ref, vc_ref = (next(it) for _ in range(6))
    else:
        q_ref, k_ref, v_ref, f_ref, ga_ref, gf_ref = (next(it) for _ in range(6))

    m = mod_ref[0]
    sh1 = m[:, 0:D_MODEL]
    sc1 = m[:, D_MODEL:2 * D_MODEL]
    h = _rms(x_ref[...], g_ref[...], EPS) * (1.0 + sc1) + sh1
    hb = h.astype(BF16)

    def proj(lo, hi):
        return _dot(hb, w_ref[:, lo:hi])

    zq = proj(0, QK_W)
    zk = proj(QK_W, 2 * QK_W)
    zv = proj(2 * QK_W, 3 * QK_W)
    if positional:
        lane = lax.broadcasted_iota(I32, (1, QK_W), 1)
        first_half = (lane % 32) < 16
        cos = cos_ref[...]
        sin_signed = sin_ref[...]
        zq = _rope(zq, cos, sin_signed, first_half)
        zk = _rope(zk, cos, sin_signed, first_half)
    qb, kb, vb = zq.astype(BF16), zk.astype(BF16), zv.astype(BF16)
    f0 = 3 * QK_W
    g0 = f0 + FOUR_W

    def gate_chunk(ref, lo, c):
        cols = slice(c * GATE_CHUNK, (c + 1) * GATE_CHUNK)
        ref[:, cols] = jax.nn.sigmoid(proj(lo + cols.start, lo + cols.stop)).astype(BF16)

    gate_work = [functools.partial(gate_chunk, ref, lo, c)
                 for ref, lo in ((ga_ref, g0), (gf_ref, g0 + D_MODEL)) for c in range(D_MODEL // GATE_CHUNK)]
    if not fuse_seq:
        q_ref[...] = qb
        k_ref[...] = kb
        v_ref[...] = vb
        f_ref[...] = proj(f0, g0)
        for work in gate_work:
            work()
        return

    seqs = [slice(b * fuse_seq, (b + 1) * fuse_seq) for b in range(x_ref.shape[0] // fuse_seq)]
    f = proj(f0, g0)
    lam = _diff_lambda(lam_ref[...], lam_init)
    for b, rs in enumerate(seqs):
        f_ref[rs, :] = _dft_real(f[rs, :], *dft_refs)
        for hd in range(N_HEADS):
            sl = slice(hd * V_DIM, (hd + 1) * V_DIM)
            kc_ref[b, 0, hd] = zk[rs, sl]
            vc_ref[b, 0, hd] = zv[rs, sl]
            o_ref[rs, sl] = _diff_attention_head(qb[rs, sl], kb[rs, sl], vb[rs, sl], lam, gs_ref[...], lam_init)
            if gate_work:
                gate_work.pop(0)()
    for work in gate_work:
        work()


def _pre_mixer(x2d, mod3, mod_row, g_pre, w_in_b, rope_tabs, seq, self_contained, attn_params, tm):
    t = x2d.shape[0]
    positional = rope_tabs is not None
    lam_p, g_subln, lam_init = attn_params
    if self_contained:
        assert tm % seq == 0
    blocks_per_seq = max(seq // tm, 1)
    row = lambda i: (i, 0)
    in_specs = [pl.BlockSpec((tm, D_MODEL), row),
                pl.BlockSpec((1, 1, N_MOD * D_MODEL), lambda i: (mod_row(i, tm), 0, 0)),
                pl.BlockSpec((1, D_MODEL), lambda i: (0, 0)),
                _whole_vmem()]
    args = [x2d, mod3, g_pre, w_in_b]
    if positional:
        in_specs += [pl.BlockSpec((tm, QK_W), lambda i: (i % blocks_per_seq, 0))] * 2
        args += list(rope_tabs)
    if self_contained:
        consts = _dft_consts(seq) + (lam_p, g_subln)
        in_specs += [pl.BlockSpec(c.shape, lambda i: (0, 0)) for c in consts]
        args += list(consts)
    n_wide = 1 if self_contained else 3
    out_shape = [jax.ShapeDtypeStruct((t, QK_W), BF16)] * n_wide + [
        jax.ShapeDtypeStruct((t, FOUR_W), BF16 if self_contained else F32),
        jax.ShapeDtypeStruct((t, D_MODEL), BF16),
        jax.ShapeDtypeStruct((t, D_MODEL), BF16)]
    out_specs = [pl.BlockSpec((tm, QK_W), row)] * n_wide + [
        pl.BlockSpec((tm, FOUR_W), row),
        pl.BlockSpec((tm, D_MODEL), row),
        pl.BlockSpec((tm, D_MODEL), row)]
    if self_contained:
        nb = t // seq
        cshape = jax.ShapeDtypeStruct((nb, 1, N_HEADS, seq, V_DIM), F32)
        cspec = pl.BlockSpec((tm // seq, 1, N_HEADS, seq, V_DIM), lambda i: (i, 0, 0, 0, 0))
        out_shape += [cshape, cshape]
        out_specs += [cspec, cspec]
    return pl.pallas_call(
        functools.partial(_pre_kernel, positional=positional, fuse_seq=seq if self_contained else 0,
                          lam_init=lam_init),
        grid=(t // tm,),
        in_specs=in_specs,
        out_specs=out_specs,
        out_shape=out_shape,
        compiler_params=_cparams(("arbitrary",)),
        name="pre_mixer",
    )(*args)


def _attn_kernel(*refs, lam_init, has_ctx):
    it = iter(refs)
    lam_ref, gs_ref, q_ref, k_ref, v_ref = (next(it) for _ in range(5))
    if has_ctx:
        ck_ref, cv_ref = next(it), next(it)
    o_ref = next(it)

    lam = _diff_lambda(lam_ref[...], lam_init)
    for hd in range(N_HEADS):
        sl = slice(hd * V_DIM, (hd + 1) * V_DIM)
        k = k_ref[:, sl]
        v = v_ref[:, sl]
        if has_ctx:
            k = jnp.concatenate([ck_ref[0, 0, hd].astype(BF16), k], axis=0)
            v = jnp.concatenate([cv_ref[0, 0, hd].astype(BF16), v], axis=0)
        o_ref[:, sl] = _diff_attention_head(q_ref[:, sl], k, v, lam, gs_ref[...], lam_init)


def _attention(lam_p, g_subln, q, k, v, ctx, seq, lam_init):
    t = q.shape[0]
    tq = ROW_BLOCK
    qb = seq // tq
    has_ctx = ctx is not None
    in_specs = [pl.BlockSpec((4, HEAD_DIM), lambda b, i: (0, 0)),
                pl.BlockSpec((1, V_DIM), lambda b, i: (0, 0)),
                pl.BlockSpec((tq, QK_W), lambda b, i: (b * qb + i, 0)),
                pl.BlockSpec((seq, QK_W), lambda b, i: (b, 0)),
                pl.BlockSpec((seq, QK_W), lambda b, i: (b, 0))]
    args = [lam_p, g_subln, q, k, v]
    if has_ctx:
        past = ctx[0].shape[3]
        cspec = pl.BlockSpec((1, 1, N_HEADS, past, V_DIM), lambda b, i: (b, 0, 0, 0, 0))
        in_specs += [cspec, cspec]
        args += list(ctx)
    return pl.pallas_call(
        functools.partial(_attn_kernel, lam_init=lam_init, has_ctx=has_ctx),
        grid=(t // seq, qb),
        in_specs=in_specs,
        out_specs=pl.BlockSpec((tq, QK_W), lambda b, i: (b * qb + i, 0)),
        out_shape=jax.ShapeDtypeStruct((t, QK_W), BF16),
        compiler_params=_cparams(("arbitrary", "arbitrary")),
        name="diff_attention",
    )(*args)


def _dft_real(f, bc_ref, bs_ref, cs_ref, ss_ref):
    fb = f.astype(BF16)
    u = _dot(fb, bc_ref[...].astype(BF16)).astype(BF16)
    w = _dot(fb, bs_ref[...].astype(BF16)).astype(BF16)
    return (_dot(cs_ref[...].astype(BF16), u) - _dot(ss_ref[...].astype(BF16), w)).astype(BF16)


def _fourier_kernel(f_ref, bc_ref, bs_ref, cs_ref, ss_ref, o_ref):
    o_ref[...] = _dft_real(f_ref[...], bc_ref, bs_ref, cs_ref, ss_ref)


def _dft_consts(seq):
    c = np.arange(FOUR_G)
    ang_c = 2.0 * np.pi * ((c[:, None] * c[None, :]) % FOUR_G) / FOUR_G
    eye = np.eye(FOUR_W // FOUR_G)
    bc = np.kron(eye, np.cos(ang_c)) / math.sqrt(FOUR_G)
    bs = np.kron(eye, np.sin(ang_c)) / math.sqrt(FOUR_G)
    s = np.arange(seq)
    ang_s = 2.0 * np.pi * ((s[:, None] * s[None, :]) % seq) / seq
    cs = np.cos(ang_s) / math.sqrt(seq)
    ss = np.sin(ang_s) / math.sqrt(seq)
    return tuple(jnp.asarray(a, dtype=F32) for a in (bc, bs, cs, ss))


def _fourier(f, seq):
    t = f.shape[0]
    bc, bs, cs, ss = _dft_consts(seq)
    const = lambda b: (0, 0)
    return pl.pallas_call(
        _fourier_kernel,
        grid=(t // seq,),
        in_specs=[pl.BlockSpec((seq, FOUR_W), lambda b: (b, 0)),
                  pl.BlockSpec((FOUR_W, FOUR_W), const),
                  pl.BlockSpec((FOUR_W, FOUR_W), const),
                  pl.BlockSpec((seq, seq), const),
                  pl.BlockSpec((seq, seq), const)],
        out_specs=pl.BlockSpec((seq, FOUR_W), lambda b: (b, 0)),
        out_shape=jax.ShapeDtypeStruct((t, FOUR_W), BF16),
        compiler_params=_cparams(("arbitrary",)),
        name="fourier_mix",
    )(f, bc, bs, cs, ss)


def _post_kernel(o_ref, fm_ref, ga_ref, gf_ref, x_ref, mod_ref, gpost_ref, gffn_ref,
                 wpa_ref, wpf_ref, wout_ref, wrx_ref,
                 x1_ref, h2t_ref, afft_ref):
    m = mod_ref[0]
    gt1 = m[:, 2 * D_MODEL:3 * D_MODEL]
    sh2 = m[:, 3 * D_MODEL:4 * D_MODEL]
    sc2 = m[:, 4 * D_MODEL:5 * D_MODEL]
    subs = [slice(r0, r0 + POST_SUB) for r0 in range(0, o_ref.shape[0], POST_SUB)]
    ab = [(_dot(o_ref[rs, :], wpa_ref[...]), _dot(fm_ref[rs, :], wpf_ref[...])) for rs in subs]
    merged = [(ga_ref[rs, :] * a + gf_ref[rs, :] * b).astype(BF16) for rs, (a, b) in zip(subs, ab)]
    ys = [_dot(mg, wout_ref[...]) for mg in merged]
    h2s = []
    for rs, y in zip(subs, ys):
        x1 = x_ref[rs, :] + gt1 * _rms(y, gpost_ref[...], EPS)
        x1_ref[rs, :] = x1
        h2s.append(_rms(x1, gffn_ref[...], EPS) * (1.0 + sc2) + sh2)
    logits = [_dot(h2.astype(BF16), wrx_ref[...]) for h2 in h2s]
    for rs, h2, lg in zip(subs, h2s, logits):
        lt = lg.T[0:N_EXPERTS]
        et = jnp.exp(lt - jnp.max(lt, axis=0, keepdims=True))
        aff = et / jnp.sum(et, axis=0, keepdims=True)
        for u in range(POST_SUB // LANES):
            afft_ref[:, rs.start // LANES + u, :] = aff[:, u * LANES:(u + 1) * LANES]
        for kc in range(ROW_TILE):
            h2t_ref[pl.ds(rs.start * ROW_TILE + kc, POST_SUB, stride=ROW_TILE), :] = h2[:, kc * LANES:(kc + 1) * LANES]


def _post_mixer(o, fm, ga, gf, x2d, mod3, mod_row, g_post, g_ffn, wpa, wpf, wout, wrx):
    t = x2d.shape[0]
    tm = POST_BLOCK
    row = lambda i: (i, 0)
    const = lambda i: (0, 0)
    return pl.pallas_call(
        _post_kernel,
        grid=(t // tm,),
        in_specs=[pl.BlockSpec((tm, QK_W), row),
                  pl.BlockSpec((tm, FOUR_W), row),
                  pl.BlockSpec((tm, D_MODEL), row),
                  pl.BlockSpec((tm, D_MODEL), row),
                  pl.BlockSpec((tm, D_MODEL), row),
                  pl.BlockSpec((1, 1, N_MOD * D_MODEL), lambda i: (mod_row(i, tm), 0, 0)),
                  pl.BlockSpec((1, D_MODEL), const),
                  pl.BlockSpec((1, D_MODEL), const),
                  pl.BlockSpec((QK_W, D_MODEL), const),
                  pl.BlockSpec((FOUR_W, D_MODEL), const),
                  pl.BlockSpec((D_MODEL, D_MODEL), const),
                  pl.BlockSpec((D_MODEL, LANES), const)],
        out_specs=[pl.BlockSpec((tm, D_MODEL), row),
                   pl.BlockSpec((tm * ROW_TILE, LANES), row),
                   pl.BlockSpec((N_EXPERTS, tm // LANES, LANES), lambda i: (0, i, 0))],
        out_shape=[jax.ShapeDtypeStruct((t, D_MODEL), F32),
                   jax.ShapeDtypeStruct((t * ROW_TILE, LANES), F32),
                   jax.ShapeDtypeStruct((N_EXPERTS, t // LANES, LANES), F32)],
        compiler_params=_cparams(("arbitrary",)),
        name="post_mixer",
    )(o, fm, ga, gf, x2d, mod3, g_post, g_ffn, wpa, wpf, wout, wrx)


RANGE_ROWS = 8


def _route_kernel(aff_ref, posm_ref, pack_ref, span_ref, ranges_ref, rows_ref, *, cap, n_tok):
    aff = aff_ref[...]
    nt = n_tok // LANES
    capf = float(cap)

    def count_ge(v):
        return jnp.sum(jnp.where(aff >= v, 1.0, 0.0), axis=(1, 2), keepdims=True)

    def search(i, thr):
        cand = thr | jnp.left_shift(jnp.int32(1), 30 - i)
        return jnp.where(count_ge(pltpu.bitcast(cand, F32)) >= capf, cand, thr)

    thr = lax.fori_loop(0, 31, search, jnp.zeros((N_EXPERTS, 1, 1), I32))
    lo = pltpu.bitcast(thr, F32)
    hi = pltpu.bitcast(thr + 1, F32)

    def refine(i, c):
        lo, hi = c
        mid = lo + (hi - lo) * 0.5
        ok = count_ge(mid) >= capf
        return jnp.where(ok, mid, lo), jnp.where(ok, hi, mid)

    lo, hi = lax.fori_loop(0, 12, refine, (lo, hi))
    gt = aff >= hi
    eq = (aff >= lo) & (aff < hi)
    n_tie = capf - jnp.sum(jnp.where(gt, 1.0, 0.0), axis=(1, 2), keepdims=True)

    sq0 = lax.broadcasted_iota(I32, (LANES, LANES), 0)
    sq1 = lax.broadcasted_iota(I32, (LANES, LANES), 1)
    along_total = jnp.concatenate([jnp.where(sq0 <= sq1, 1.0, 0.0), jnp.ones((LANES, LANES), F32)],
                                  axis=1).astype(BF16)
    m = N_EXPERTS * nt
    r0 = lax.broadcasted_iota(I32, (m, m), 0)
    r1 = lax.broadcasted_iota(I32, (m, m), 1)
    earlier = jnp.where((r0 // nt == r1 // nt) & (r1 < r0), 1.0, 0.0).astype(BF16)
    lane = lax.broadcasted_iota(I32, (1, LANES), 1)
    token = lax.broadcasted_iota(I32, (nt, LANES), 0) * LANES + lane

    def tile_counts(x):
        both = _dot(x.reshape(m, LANES).astype(BF16), along_total)
        total = both[:, LANES:]
        before = _dot(earlier, total.astype(BF16))
        shape = (N_EXPERTS, nt, LANES)
        return both[:, :LANES].reshape(shape), total.reshape(shape), before.reshape(shape)

    eq_f = jnp.where(eq, 1.0, 0.0)
    eq_along, _, eq_before = tile_counts(eq_f)
    sel = jnp.where(gt, 1.0, jnp.where(eq_along + eq_before <= n_tie, eq_f, 0.0))
    sel_along, sel_total, sel_before = tile_counts(sel)
    posm_ref[...] = jnp.where(sel > 0.5, sel_along + sel_before - sel, -1.0).astype(I32)

    cnt = jnp.sum(sel, axis=0)
    rows_before = jnp.sum(sel_before, axis=0)
    tok_start = _dot(cnt.astype(BF16), along_total[:, :LANES]) - cnt + rows_before
    k = jnp.zeros((nt, LANES), F32)
    slots_before = jnp.zeros((nt, LANES), F32)
    slots_here = jnp.zeros((nt, LANES), F32)
    for e in range(N_EXPERTS):
        pack_ref[e] = (tok_start + k).astype(I32) * (1 << TOKEN_BITS) + token
        k = k + sel[e]
        slots_before = jnp.where(lane == e, sel_before[e], slots_before)
        slots_here = jnp.where(lane == e, sel_total[e], slots_here)

    eye = jnp.where(sq0 == sq1, 1.0, 0.0).astype(BF16)
    tok_end = tok_start + cnt
    parts = []
    for v in (tok_start, tok_end):
        high = jnp.floor(v * (1.0 / LANES))
        parts += [high, v - high * LANES]
    for c in range(nt):
        rows4 = jnp.zeros((LANES, LANES), F32)
        for r, part in enumerate(parts):
            rows4 = jnp.where(sq0 == r, jnp.broadcast_to(part[c:c + 1, :], (LANES, LANES)), rows4)
        span_ref[c * LANES:(c + 1) * LANES, :] = lax.dot_general(
            eye, rows4.astype(BF16), (((1,), (1,)), ((), ())), preferred_element_type=F32)

    ranges_ref[...] = jnp.zeros_like(ranges_ref)
    for s in range(cap // LANES):
        done = jnp.where(slots_before + slots_here <= float(s * LANES), 1.0, 0.0)
        begun = jnp.where(slots_before < float((s + 1) * LANES), 1.0, 0.0)
        ranges_ref[s:s + 1, :] = jnp.sum(done, axis=0, keepdims=True).astype(I32)
        ranges_ref[RANGE_ROWS + s:RANGE_ROWS + s + 1, :] = jnp.sum(begun, axis=0, keepdims=True).astype(I32)
    rows_ref[...] = rows_before.astype(I32)


def _route(aff, cap):
    n_e, nt, _ = aff.shape
    n_tok = nt * LANES
    assert cap // LANES <= RANGE_ROWS
    return pl.pallas_call(
        functools.partial(_route_kernel, cap=cap, n_tok=n_tok),
        out_shape=[jax.ShapeDtypeStruct((n_e, nt, LANES), I32),
                   jax.ShapeDtypeStruct((n_e, nt, LANES), I32),
                   jax.ShapeDtypeStruct((n_tok, LANES), F32),
                   jax.ShapeDtypeStruct((2 * RANGE_ROWS, LANES), I32),
                   jax.ShapeDtypeStruct((nt, LANES), I32)],
        compiler_params=pltpu.CompilerParams(vmem_limit_bytes=VMEM_LIMIT),
        name="route",
    )(aff)


def _slots_kernel(ranges_ref, posm_ref, pack_ref, idx_ref, qslot_ref, acc_ref):
    e = pl.program_id(0)
    sub = lax.broadcasted_iota(I32, (LANES, LANES), 0)
    eye = sub == lax.broadcasted_iota(I32, (LANES, LANES), 1)
    n_tiles = idx_ref.shape[1]

    for s in range(n_tiles):
        slot = sub + s * LANES

        def body(c, acc, slot=slot):
            hit = posm_ref[e, pl.ds(c, 1), :] == slot
            return acc + jnp.where(hit, pack_ref[e, pl.ds(c, 1), :], 0)

        acc_ref[s] = lax.fori_loop(ranges_ref[s, e], ranges_ref[RANGE_ROWS + s, e], body,
                                   jnp.zeros((LANES, LANES), I32))

    def as_row(part):
        col = jnp.sum(part.astype(F32), axis=1, keepdims=True)
        return jnp.sum(jnp.where(eye, col, 0.0), axis=0, keepdims=True).astype(I32)

    for s in range(n_tiles):
        acc = acc_ref[s]
        idx_ref[0, s:s + 1, :] = as_row(acc & ((1 << TOKEN_BITS) - 1)) * ROW_TILE
        qslot_ref[0, s:s + 1, :] = as_row(lax.shift_right_logical(acc, TOKEN_BITS)) * ROW_TILE


def _slot_lists(ranges, posm4, qdst4, cap):
    ns = cap // LANES
    grid_spec = pltpu.PrefetchScalarGridSpec(
        num_scalar_prefetch=1,
        grid=(N_EXPERTS,),
        in_specs=[_whole_vmem(), _whole_vmem()],
        out_specs=[pl.BlockSpec((1, ns, LANES), lambda e, *_: (e, 0, 0)),
                   pl.BlockSpec((1, ns, LANES), lambda e, *_: (e, 0, 0))],
        scratch_shapes=[pltpu.VMEM((ns, LANES, LANES), I32)])
    idx, qslot = pl.pallas_call(
        _slots_kernel,
        grid_spec=grid_spec,
        out_shape=[jax.ShapeDtypeStruct((N_EXPERTS, ns, LANES), I32),
                   jax.ShapeDtypeStruct((N_EXPERTS, ns, LANES), I32)],
        compiler_params=_cparams(("arbitrary",)),
        name="slot_lists",
    )(ranges, posm4, qdst4)
    return idx.reshape(N_EXPERTS, cap), qslot.reshape(N_EXPERTS, cap)


def _moe_kernel(idxc_ref, idxs_ref, qc_ref, qs_ref,
                hc_ref, hs_ref, wr_ref, wg_ref, wu_ref, wd_ref, zc_ref, zs_ref,
                xbuf, ybuf, xb_ref, gate_ref, acc_ref, gsem, ssem, *, capc, caps):
    e = pl.program_id(0)
    j = pl.program_id(1)
    n_e = pl.num_programs(0)
    n_j = FF_STEPS
    slot = e % 2
    other = 1 - slot
    rows = capc + caps
    gc, gs = _per_step(capc), _per_step(caps)
    groups = ((hc_ref, idxc_ref, zc_ref, qc_ref, gc, 0), (hs_ref, idxs_ref, zs_ref, qs_ref, gs, gc * n_j))

    def tile(ref, first_sublane):
        return ref.at[pl.ds(pl.multiple_of(first_sublane, ROW_TILE), ROW_TILE), :]

    def gather(ex, sl, step, i, group):
        h_ref, idx_ref, _, _, per_step, base = group
        p = step * per_step + i
        src = tile(h_ref, idx_ref[ex * (per_step * n_j) + p])
        pltpu.make_async_copy(src, tile(xbuf.at[sl], (base + p) * ROW_TILE), gsem.at[sl]).start()

    def scatter(table_row, sl, step, i, group):
        _, _, z_ref, q_ref, per_step, base = group
        p = step * per_step + i
        dst = tile(z_ref, q_ref[table_row * (per_step * n_j) + p])
        pltpu.make_async_copy(tile(ybuf.at[sl], (base + p) * ROW_TILE), dst, ssem.at[sl]).start()

    def all_steps(fn):
        for group in groups:
            def body(p, carry, group=group):
                fn(p, group)
                return carry
            lax.fori_loop(0, group[4] * n_j, body, 0, unroll=8)

    def wait_all(buf, sem, sl):
        pltpu.make_async_copy(buf.at[sl], buf.at[sl], sem.at[sl]).wait()

    @pl.when((e == 0) & (j == 0))
    def _():
        ybuf[...] = jnp.zeros_like(ybuf)
        all_steps(lambda p, group: gather(0, 0, 0, p, group))

    @pl.when(j == 0)
    def _():
        wait_all(xbuf, gsem, slot)
        for base, n, dst in ((0, capc, 0), (gc * n_j, caps, capc)):
            for kc in range(ROW_TILE):
                tiles = xbuf[slot, pl.ds(base * ROW_TILE + kc, n, stride=ROW_TILE), :]
                xb_ref[dst:dst + n, kc * LANES:(kc + 1) * LANES] = tiles.astype(BF16)
        acc_ref[...] = jnp.zeros_like(acc_ref)
        logits = _dot(xb_ref[...], wr_ref[...])
        lane = lax.broadcasted_iota(I32, (1, LANES), 1)
        is_expert = lane < N_EXPERTS
        ex = jnp.exp(logits - jnp.max(jnp.where(is_expert, logits, -jnp.inf), axis=-1, keepdims=True))
        mine = jnp.sum(jnp.where(lane == e, ex, 0.0), axis=-1, keepdims=True)
        gate = mine / jnp.sum(jnp.where(is_expert, ex, 0.0), axis=-1, keepdims=True)
        gate_ref[...] = jnp.broadcast_to(gate, gate_ref.shape)

    nxt = jnp.minimum(e + 1, n_e - 1)
    for group in groups:
        for i in range(group[4]):
            gather(nxt, other, j, i, group)
            scatter(e, other, j, i, group)

    x = xb_ref[...]
    g = _dot(x, wg_ref[0].astype(BF16))
    u = _dot(x, wu_ref[0].astype(BF16))
    hid = (g * jax.nn.sigmoid(g) * u).astype(BF16)
    acc_ref[...] += _dot(hid, wd_ref[0].astype(BF16))

    @pl.when(j == n_j - 1)
    def _():
        @pl.when(e >= 1)
        def _():
            wait_all(ybuf, ssem, slot)

        gate = gate_ref[...]
        for base, n, src in ((0, capc, 0), (gc * n_j, caps, capc)):
            for kc in range(ROW_TILE):
                y = acc_ref[src:src + n, kc * LANES:(kc + 1) * LANES] * gate[src:src + n]
                ybuf[slot, pl.ds(base * ROW_TILE + kc, n, stride=ROW_TILE), :] = y

        @pl.when(e == n_e - 1)
        def _():
            all_steps(lambda p, group: scatter(e + 1, slot, 0, p, group))
            wait_all(ybuf, ssem, other)
            wait_all(ybuf, ssem, slot)
            wait_all(xbuf, gsem, other)


def _per_step(cap):
    return -(-cap // FF_STEPS)


def _copy_tables(idx, qslot, n_rows):
    n_e, cap = idx.shape
    padded = _per_step(cap) * FF_STEPS
    n_pad = padded - cap
    idx_p = jnp.concatenate([idx, jnp.zeros((n_e, n_pad), I32)], axis=1)
    spare = n_rows + jnp.arange(padded + n_e * n_pad, dtype=I32)
    lead = spare[:padded][None, :]
    pad_rows = spare[padded:].reshape(n_e, n_pad)
    q_p = jnp.concatenate([lead, jnp.concatenate([qslot // ROW_TILE, pad_rows], axis=1)], axis=0) * ROW_TILE
    return idx_p.reshape(-1), q_p.reshape(-1), n_rows + padded + n_e * n_pad


def _moe(idxc, idxs, qc, qs, hc, hs, wrx, w_gate, w_up, w_down):
    capc, caps = idxc.shape[1], idxs.shape[1]
    rows = capc + caps
    tf = FF_TILE
    idxc, qc, zc_rows = _copy_tables(idxc, qc, N_EXPERTS * capc)
    idxs, qs, zs_rows = _copy_tables(idxs, qs, N_EXPERTS * caps)
    buf_rows = (_per_step(capc) + _per_step(caps)) * FF_STEPS
    any_spec = pl.BlockSpec(memory_space=pl.ANY)
    grid_spec = pltpu.PrefetchScalarGridSpec(
        num_scalar_prefetch=4,
        grid=(N_EXPERTS, FF_STEPS),
        in_specs=[any_spec, any_spec,
                  pl.BlockSpec((D_MODEL, LANES), lambda e, j, *_: (0, 0)),
                  pl.BlockSpec((1, D_MODEL, tf), lambda e, j, *_: (e, 0, j)),
                  pl.BlockSpec((1, D_MODEL, tf), lambda e, j, *_: (e, 0, j)),
                  pl.BlockSpec((1, tf, D_MODEL), lambda e, j, *_: (e, j, 0))],
        out_specs=[any_spec, any_spec],
        scratch_shapes=[pltpu.VMEM((2, buf_rows * ROW_TILE, LANES), F32),
                        pltpu.VMEM((2, buf_rows * ROW_TILE, LANES), F32),
                        pltpu.VMEM((rows, D_MODEL), BF16),
                        pltpu.VMEM((rows, LANES), F32),
                        pltpu.VMEM((rows, D_MODEL), F32),
                        pltpu.SemaphoreType.DMA((2,)),
                        pltpu.SemaphoreType.DMA((2,))])
    return pl.pallas_call(
        functools.partial(_moe_kernel, capc=capc, caps=caps),
        grid_spec=grid_spec,
        out_shape=[jax.ShapeDtypeStruct((zc_rows * ROW_TILE, LANES), F32),
                   jax.ShapeDtypeStruct((zs_rows * ROW_TILE, LANES), F32)],
        compiler_params=_cparams(("arbitrary", "arbitrary")),
        name="expert_ffn",
    )(idxc, idxs, qc, qs, hc, hs, wrx, w_gate, w_up, w_down)


Z_BUFFERS = 3


def _combine_kernel(clo_ref, chi_ref, z_ref, span_ref, x1_ref, mod_ref, g_ref, o_ref,
                    zbuf, acc_ref, sems, state, *, n_chunks):
    b = pl.program_id(0)
    chunk_rows = ROW_CHUNK * ROW_TILE

    @pl.when(b == 0)
    def _():
        state[0] = 0
        state[1] = 0

    def chunk_copy(c):
        src = z_ref.at[pl.ds(pl.multiple_of(c * chunk_rows, chunk_rows), chunk_rows), :]
        return pltpu.make_async_copy(src, zbuf.at[c % Z_BUFFERS], sems.at[c % Z_BUFFERS])

    acc_ref[...] = jnp.zeros_like(acc_ref)
    first_row = span_ref[:, 0:1] * LANES + span_ref[:, 1:2]
    end_row = span_ref[:, 2:3] * LANES + span_ref[:, 3:4]
    lane_row = lax.broadcasted_iota(I32, (TOK_BLOCK, ROW_CHUNK), 1)

    def body(c, carry):
        for _ in range(Z_BUFFERS):
            nxt = state[0]

            @pl.when(nxt <= jnp.minimum(c + Z_BUFFERS - 1, n_chunks - 1))
            def _():
                chunk_copy(nxt).start()
                state[0] = nxt + 1

        @pl.when(state[1] <= c)
        def _():
            chunk_copy(c).wait()
            state[1] = c + 1

        slot = c % Z_BUFFERS
        row = (c * ROW_CHUNK + lane_row).astype(F32)
        onehot = jnp.where((row >= first_row) & (row < end_row), 1.0, 0.0).astype(BF16)
        y = jnp.concatenate([zbuf[slot, pl.ds(kc, ROW_CHUNK, stride=ROW_TILE), :] for kc in range(ROW_TILE)],
                            axis=1)
        acc_ref[...] += _dot(onehot, y.astype(BF16))
        return carry

    lax.fori_loop(clo_ref[b], chi_ref[b], body, 0)
    gt2 = mod_ref[0][:, 5 * D_MODEL:6 * D_MODEL]
    o_ref[...] = x1_ref[...] + gt2 * _rms(acc_ref[...], g_ref[...], EPS)


def _combine(ranges, z, span, x1, mod3, mod_row, g_post_ffn):
    clo, chi = ranges
    t = x1.shape[0]
    tb = TOK_BLOCK
    grid_spec = pltpu.PrefetchScalarGridSpec(
        num_scalar_prefetch=2,
        grid=(t // tb,),
        in_specs=[pl.BlockSpec(memory_space=pl.ANY),
                  pl.BlockSpec((tb, LANES), lambda b, *_: (b, 0)),
                  pl.BlockSpec((tb, D_MODEL), lambda b, *_: (b, 0)),
                  pl.BlockSpec((1, 1, N_MOD * D_MODEL), lambda b, *_: (mod_row(b, tb), 0, 0)),
                  pl.BlockSpec((1, D_MODEL), lambda b, *_: (0, 0))],
        out_specs=pl.BlockSpec((tb, D_MODEL), lambda b, *_: (b, 0)),
        scratch_shapes=[pltpu.VMEM((Z_BUFFERS, ROW_CHUNK * ROW_TILE, LANES), F32),
                        pltpu.VMEM((tb, D_MODEL), F32),
                        pltpu.SemaphoreType.DMA((Z_BUFFERS,)),
                        pltpu.SMEM((2,), I32)])
    return pl.pallas_call(
        functools.partial(_combine_kernel, n_chunks=2 * t // ROW_CHUNK),
        grid_spec=grid_spec,
        out_shape=jax.ShapeDtypeStruct((t, D_MODEL), F32),
        compiler_params=_cparams(("arbitrary",)),
        name="combine",
    )(clo, chi, z, span, x1, mod3, g_post_ffn)


def _rope_tables(seq):
    half = HEAD_DIM // 4
    freqs = ROPE_THETA ** (-np.arange(half, dtype=np.float64) / half)
    s = np.arange(seq)
    row = (s // GRID_W)[:, None] * freqs[None, :]
    col = (s % GRID_W)[:, None] * freqs[None, :]
    ang = np.concatenate([row, row, col, col], axis=1)
    ang = np.tile(ang, (1, QK_W // HEAD_DIM))
    lane = np.arange(QK_W)
    sign = np.where((lane % 32) < 16, -1.0, 1.0)[None, :]
    return (jnp.asarray(np.cos(ang), dtype=F32), jnp.asarray(np.sin(ang) * sign, dtype=F32))


def _combine_ranges(rows, n_tok):
    step = TOK_BLOCK // LANES
    nb = n_tok // TOK_BLOCK
    lo = rows[0:nb * step:step, 0]
    hi = jnp.concatenate([lo[1:], jnp.full((1,), 2 * n_tok, I32)])
    return (lo // ROW_CHUNK).astype(I32), ((hi + ROW_CHUNK - 1) // ROW_CHUNK).astype(I32)


def kernel(x_prompt, x_sample, c, cache_k, cache_v, c_ctx, w_mod, b_mod, g_pre_mix, g_post_mix, g_pre_ffn, g_post_ffn, w_in, lam_q1, lam_k1, lam_q2, lam_k2, g_subln, w_proj_attn, w_proj_fourier, w_out, w_router, w_gate, w_up, w_down):
    assert w_mod.shape[0] == 1
    lam_init = 0.8 - 0.6 * math.exp(-0.3 * 0)
    bp, sp, _ = x_prompt.shape
    bs, ss, _ = x_sample.shape

    cond8 = jnp.concatenate([c_ctx[None, :], c, jnp.zeros((8 - 1 - bs, D_MODEL), F32)], axis=0)
    mod3 = _modulation(cond8, w_mod[0], b_mod).reshape(8, 1, N_MOD * D_MODEL)

    w_in_b = w_in[0].astype(BF16)
    wpa = w_proj_attn[0].astype(BF16)
    wpf = w_proj_fourier[0].astype(BF16)
    wout = w_out[0].astype(BF16)
    wr = w_router[0].astype(BF16)
    wrx = jnp.concatenate([wr, jnp.zeros((D_MODEL, LANES - N_EXPERTS), BF16)], axis=1)
    lam_p = jnp.concatenate([lam_q1, lam_k1, lam_q2, lam_k2], axis=0)

    groups = []
    for x, seq, positional, ctx in ((x_prompt, sp, False, None),
                                    (x_sample, ss, True, (cache_k, cache_v))):
        nb = x.shape[0]
        t = nb * seq
        x2d = x.reshape(t, D_MODEL)
        if positional:
            mod_row = lambda i, tm, seq=seq: 1 + (i * tm) // seq
        else:
            mod_row = lambda i, tm: 0
        self_contained = ctx is None and PRE_BLOCK % seq == 0
        pre = _pre_mixer(x2d, mod3, mod_row, g_pre_mix, w_in_b, _rope_tables(seq) if positional else None, seq,
                         self_contained, (lam_p, g_subln, lam_init), tm=PRE_BLOCK if self_contained else ROW_BLOCK)
        if self_contained:
            o, fm, ga, gf = pre[:4]
        else:
            q, k, v, f, ga, gf = pre
            o = _attention(lam_p, g_subln, q, k, v, ctx, seq, lam_init)
            fm = _fourier(f, seq)
        x1, h2t, aff_t = _post_mixer(o, fm, ga, gf, x2d, mod3, mod_row, g_post_mix, g_pre_ffn,
                                     wpa, wpf, wout, wrx)
        cap = 2 * t // N_EXPERTS
        assert t <= 1 << TOKEN_BITS
        posm, pack, span, ranges, rows = _route(aff_t, cap)
        idx, qslot = _slot_lists(ranges, posm, pack, cap)
        groups.append(dict(x1=x1, h2t=h2t, idx=idx, qslot=qslot, span=span, ranges=_combine_ranges(rows, t),
                           mod_row=mod_row, cache=pre[4:] if self_contained else None, shape=x.shape))

    gc, gs_ = groups
    zc, zs = _moe(gc["idx"], gs_["idx"], gc["qslot"], gs_["qslot"], gc["h2t"], gs_["h2t"], wrx,
                  w_gate[0], w_up[0], w_down[0])
    outs = []
    for g, z in ((gc, zc), (gs_, zs)):
        out = _combine(g["ranges"], z, g["span"], g["x1"], mod3, g["mod_row"], g_post_ffn)
        outs.append(out.reshape(g["shape"]))
    new_k, new_v = gc["cache"]
    return (outs[0], outs[1], new_k, new_v)
```

```python
import functools
import math

import numpy as np
import jax
import jax.numpy as jnp
from jax import lax
from jax.experimental import pallas as pl
from jax.experimental.pallas import tpu as pltpu

F32 = jnp.float32
BF16 = jnp.bfloat16
I32 = jnp.int32

D_MODEL = 1024
N_HEADS = 6
HEAD_DIM = 64
V_DIM = 128
QK_W = 768
FOUR_W = 256
FOUR_G = 64
IN_W = 4608
N_EXPERTS = 16
D_FF = 2816
N_MOD = 6
GRID_W = 64
ROPE_THETA = 10000.0
EPS = 1e-6
SUBLN_EPS = 1e-5

LANES = 128
ROW_BLOCK = 256
PRE_BLOCK = 512
GATE_CHUNK = 256
POST_BLOCK = 1024
POST_SUB = 256
TOK_BLOCK = 256
ROW_CHUNK = 256
FF_TILE = 256
FF_STEPS = D_FF // FF_TILE
ROW_TILE = D_MODEL // LANES
TOKEN_BITS = 13
VMEM_LIMIT = 56 * 1024 * 1024


def _cparams(sem):
    return pltpu.CompilerParams(dimension_semantics=sem, vmem_limit_bytes=VMEM_LIMIT)


def _dot(a, b):
    return jnp.dot(a, b, preferred_element_type=F32)


def _rms(x, g, eps):
    return x * lax.rsqrt(jnp.mean(x * x, axis=-1, keepdims=True) + eps) * g


def _whole_vmem():
    return pl.BlockSpec(memory_space=pltpu.MemorySpace.VMEM)


def _mod_kernel(c_ref, w_ref, b_ref, o_ref):
    c = c_ref[...]
    s = c * jax.nn.sigmoid(c)
    o_ref[...] = _dot(s.astype(BF16), w_ref[...].astype(BF16)) + b_ref[...]


def _modulation(cond8, w_mod, b_mod):
    tn = 1024
    n = N_MOD * D_MODEL
    return pl.pallas_call(
        _mod_kernel,
        grid=(n // tn,),
        in_specs=[pl.BlockSpec((8, D_MODEL), lambda j: (0, 0)),
                  pl.BlockSpec((D_MODEL, tn), lambda j: (0, j)),
                  pl.BlockSpec((1, tn), lambda j: (0, j))],
        out_specs=pl.BlockSpec((8, tn), lambda j: (0, j)),
        out_shape=jax.ShapeDtypeStruct((8, n), F32),
        compiler_params=_cparams(("arbitrary",)),
        name="modulation",
    )(cond8, w_mod, b_mod)


def _diff_lambda(lp, lam_init):
    s1 = jnp.sum(lp[0:1] * lp[1:2], axis=-1, keepdims=True)
    s2 = jnp.sum(lp[2:3] * lp[3:4], axis=-1, keepdims=True)
    return jnp.exp(s1) - jnp.exp(s2) + lam_init


def _diff_attention_head(q, k, v, lam, g_subln, lam_init):
    comp1 = lax.broadcasted_iota(I32, (1, V_DIM), 1) < HEAD_DIM
    scale = jnp.asarray(HEAD_DIM ** -0.5, BF16)
    v_ones = jnp.concatenate([v, jnp.ones_like(v)], axis=1)

    def attend(qc):
        s = lax.dot_general(qc, k, (((1,), (1,)), ((), ())), preferred_element_type=F32)
        ex = jnp.exp(s - jnp.max(s, axis=-1, keepdims=True)).astype(BF16)
        ov = _dot(ex, v_ones)
        return ov[:, 0:V_DIM] / ov[:, V_DIM:2 * V_DIM]

    qs = q * scale
    zero = jnp.zeros_like(qs)
    o = attend(jnp.where(comp1, qs, zero)) - lam * attend(jnp.where(comp1, zero, qs))
    return (_rms(o, g_subln, SUBLN_EPS) * (1.0 - lam_init)).astype(BF16)


def _rope(z, cos, sin_signed, first_half):
    fwd = pltpu.roll(z, QK_W - 16, axis=1)
    bwd = pltpu.roll(z, 16, axis=1)
    return z * cos + jnp.where(first_half, fwd, bwd) * sin_signed


def _pre_kernel(*refs, positional, fuse_seq, lam_init):
    it = iter(refs)
    x_ref, mod_ref, g_ref, w_ref = next(it), next(it), next(it), next(it)
    if positional:
        cos_ref, sin_ref = next(it), next(it)
    if fuse_seq:
        dft_refs = [next(it) for _ in range(4)]
        lam_ref, gs_ref = next(it), next(it)
        o_ref, f_ref, ga_ref, gf_ref, kc_ref, vc_ref = (next(it) for _ in range(6))
    else:
        q_ref, k_ref, v_ref, f_ref, ga_ref, gf_ref = (next(it) for _ in range(6))

    m = mod_ref[0]
    sh1 = m[:, 0:D_MODEL]
    sc1 = m[:, D_MODEL:2 * D_MODEL]
    h = _rms(x_ref[...], g_ref[...], EPS) * (1.0 + sc1) + sh1
    hb = h.astype(BF16)

    def proj(lo, hi):
        return _dot(hb, w_ref[:, lo:hi])

    zq = proj(0, QK_W)
    zk = proj(QK_W, 2 * QK_W)
    zv = proj(2 * QK_W, 3 * QK_W)
    if positional:
        lane = lax.broadcasted_iota(I32, (1, QK_W), 1)
        first_half = (lane % 32) < 16
        cos = cos_ref[...]
        sin_signed = sin_ref[...]
        zq = _rope(zq, cos, sin_signed, first_half)
        zk = _rope(zk, cos, sin_signed, first_half)
    qb, kb, vb = zq.astype(BF16), zk.astype(BF16), zv.astype(BF16)
    f0 = 3 * QK_W
    g0 = f0 + FOUR_W

    def gate_chunk(ref, lo, c):
        cols = slice(c * GATE_CHUNK, (c + 1) * GATE_CHUNK)
        ref[:, cols] = jax.nn.sigmoid(proj(lo + cols.start, lo + cols.stop)).astype(BF16)

    gate_work = [functools.partial(gate_chunk, ref, lo, c)
                 for ref, lo in ((ga_ref, g0), (gf_ref, g0 + D_MODEL)) for c in range(D_MODEL // GATE_CHUNK)]
    if not fuse_seq:
        q_ref[...] = qb
        k_ref[...] = kb
        v_ref[...] = vb
        f_ref[...] = proj(f0, g0)
        for work in gate_work:
            work()
        return

    seqs = [slice(b * fuse_seq, (b + 1) * fuse_seq) for b in range(x_ref.shape[0] // fuse_seq)]
    f = proj(f0, g0)
    lam = _diff_lambda(lam_ref[...], lam_init)
    for b, rs in enumerate(seqs):
        f_ref[rs, :] = _dft_real(f[rs, :], *dft_refs)
        for hd in range(N_HEADS):
            sl = slice(hd * V_DIM, (hd + 1) * V_DIM)
            kc_ref[b, 0, hd] = zk[rs, sl]
            vc_ref[b, 0, hd] = zv[rs, sl]
            o_ref[rs, sl] = _diff_attention_head(qb[rs, sl], kb[rs, sl], vb[rs, sl], lam, gs_ref[...], lam_init)
            if gate_work:
                gate_work.pop(0)()
    for work in gate_work:
        work()


def _pre_mixer(x2d, mod3, mod_row, g_pre, w_in_b, rope_tabs, seq, self_contained, attn_params, tm):
    t = x2d.shape[0]
    positional = rope_tabs is not None
    lam_p, g_subln, lam_init = attn_params
    if self_contained:
        assert tm % seq == 0
    blocks_per_seq = max(seq // tm, 1)
    row = lambda i: (i, 0)
    in_specs = [pl.BlockSpec((tm, D_MODEL), row),
                pl.BlockSpec((1, 1, N_MOD * D_MODEL), lambda i: (mod_row(i, tm), 0, 0)),
                pl.BlockSpec((1, D_MODEL), lambda i: (0, 0)),
                _whole_vmem()]
    args = [x2d, mod3, g_pre, w_in_b]
    if positional:
        in_specs += [pl.BlockSpec((tm, QK_W), lambda i: (i % blocks_per_seq, 0))] * 2
        args += list(rope_tabs)
    if self_contained:
        consts = _dft_consts(seq) + (lam_p, g_subln)
        in_specs += [pl.BlockSpec(c.shape, lambda i: (0, 0)) for c in consts]
        args += list(consts)
    n_wide = 1 if self_contained else 3
    out_shape = [jax.ShapeDtypeStruct((t, QK_W), BF16)] * n_wide + [
        jax.ShapeDtypeStruct((t, FOUR_W), BF16 if self_contained else F32),
        jax.ShapeDtypeStruct((t, D_MODEL), BF16),
        jax.ShapeDtypeStruct((t, D_MODEL), BF16)]
    out_specs = [pl.BlockSpec((tm, QK_W), row)] * n_wide + [
        pl.BlockSpec((tm, FOUR_W), row),
        pl.BlockSpec((tm, D_MODEL), row),
        pl.BlockSpec((tm, D_MODEL), row)]
    if self_contained:
        nb = t // seq
        cshape = jax.ShapeDtypeStruct((nb, 1, N_HEADS, seq, V_DIM), F32)
        cspec = pl.BlockSpec((tm // seq, 1, N_HEADS, seq, V_DIM), lambda i: (i, 0, 0, 0, 0))
        out_shape += [cshape, cshape]
        out_specs += [cspec, cspec]
    return pl.pallas_call(
        functools.partial(_pre_kernel, positional=positional, fuse_seq=seq if self_contained else 0,
                          lam_init=lam_init),
        grid=(t // tm,),
        in_specs=in_specs,
        out_specs=out_specs,
        out_shape=out_shape,
        compiler_params=_cparams(("arbitrary",)),
        name="pre_mixer",
    )(*args)


def _attn_kernel(*refs, lam_init, has_ctx):
    it = iter(refs)
    lam_ref, gs_ref, q_ref, k_ref, v_ref = (next(it) for _ in range(5))
    if has_ctx:
        ck_ref, cv_ref = next(it), next(it)
    o_ref = next(it)

    lam = _diff_lambda(lam_ref[...], lam_init)
    for hd in range(N_HEADS):
        sl = slice(hd * V_DIM, (hd + 1) * V_DIM)
        k = k_ref[:, sl]
        v = v_ref[:, sl]
        if has_ctx:
            k = jnp.concatenate([ck_ref[0, 0, hd].astype(BF16), k], axis=0)
            v = jnp.concatenate([cv_ref[0, 0, hd].astype(BF16), v], axis=0)
        o_ref[:, sl] = _diff_attention_head(q_ref[:, sl], k, v, lam, gs_ref[...], lam_init)


def _attention(lam_p, g_subln, q, k, v, ctx, seq, lam_init):
    t = q.shape[0]
    tq = ROW_BLOCK
    qb = seq // tq
    has_ctx = ctx is not None
    in_specs = [pl.BlockSpec((4, HEAD_DIM), lambda b, i: (0, 0)),
                pl.BlockSpec((1, V_DIM), lambda b, i: (0, 0)),
                pl.BlockSpec((tq, QK_W), lambda b, i: (b * qb + i, 0)),
                pl.BlockSpec((seq, QK_W), lambda b, i: (b, 0)),
                pl.BlockSpec((seq, QK_W), lambda b, i: (b, 0))]
    args = [lam_p, g_subln, q, k, v]
    if has_ctx:
        past = ctx[0].shape[3]
        cspec = pl.BlockSpec((1, 1, N_HEADS, past, V_DIM), lambda b, i: (b, 0, 0, 0, 0))
        in_specs += [cspec, cspec]
        args += list(ctx)
    return pl.pallas_call(
        functools.partial(_attn_kernel, lam_init=lam_init, has_ctx=has_ctx),
        grid=(t // seq, qb),
        in_specs=in_specs,
        out_specs=pl.BlockSpec((tq, QK_W), lambda b, i: (b * qb + i, 0)),
        out_shape=jax.ShapeDtypeStruct((t, QK_W), BF16),
        compiler_params=_cparams(("arbitrary", "arbitrary")),
        name="diff_attention",
    )(*args)


def _dft_real(f, bc_ref, bs_ref, cs_ref, ss_ref):
    fb = f.astype(BF16)
    u = _dot(fb, bc_ref[...].astype(BF16)).astype(BF16)
    w = _dot(fb, bs_ref[...].astype(BF16)).astype(BF16)
    return (_dot(cs_ref[...].astype(BF16), u) - _dot(ss_ref[...].astype(BF16), w)).astype(BF16)


def _fourier_kernel(f_ref, bc_ref, bs_ref, cs_ref, ss_ref, o_ref):
    o_ref[...] = _dft_real(f_ref[...], bc_ref, bs_ref, cs_ref, ss_ref)


def _dft_consts(seq):
    c = np.arange(FOUR_G)
    ang_c = 2.0 * np.pi * ((c[:, None] * c[None, :]) % FOUR_G) / FOUR_G
    eye = np.eye(FOUR_W // FOUR_G)
    bc = np.kron(eye, np.cos(ang_c)) / math.sqrt(FOUR_G)
    bs = np.kron(eye, np.sin(ang_c)) / math.sqrt(FOUR_G)
    s = np.arange(seq)
    ang_s = 2.0 * np.pi * ((s[:, None] * s[None, :]) % seq) / seq
    cs = np.cos(ang_s) / math.sqrt(seq)
    ss = np.sin(ang_s) / math.sqrt(seq)
    return tuple(jnp.asarray(a, dtype=F32) for a in (bc, bs, cs, ss))


def _fourier(f, seq):
    t = f.shape[0]
    bc, bs, cs, ss = _dft_consts(seq)
    const = lambda b: (0, 0)
    return pl.pallas_call(
        _fourier_kernel,
        grid=(t // seq,),
        in_specs=[pl.BlockSpec((seq, FOUR_W), lambda b: (b, 0)),
                  pl.BlockSpec((FOUR_W, FOUR_W), const),
                  pl.BlockSpec((FOUR_W, FOUR_W), const),
                  pl.BlockSpec((seq, seq), const),
                  pl.BlockSpec((seq, seq), const)],
        out_specs=pl.BlockSpec((seq, FOUR_W), lambda b: (b, 0)),
        out_shape=jax.ShapeDtypeStruct((t, FOUR_W), BF16),
        compiler_params=_cparams(("arbitrary",)),
        name="fourier_mix",
    )(f, bc, bs, cs, ss)


def _post_kernel(o_ref, fm_ref, ga_ref, gf_ref, x_ref, mod_ref, gpost_ref, gffn_ref,
                 wpa_ref, wpf_ref, wout_ref, wrx_ref,
                 x1_ref, h2t_ref, afft_ref):
    m = mod_ref[0]
    gt1 = m[:, 2 * D_MODEL:3 * D_MODEL]
    sh2 = m[:, 3 * D_MODEL:4 * D_MODEL]
    sc2 = m[:, 4 * D_MODEL:5 * D_MODEL]
    subs = [slice(r0, r0 + POST_SUB) for r0 in range(0, o_ref.shape[0], POST_SUB)]
    ab = [(_dot(o_ref[rs, :], wpa_ref[...]), _dot(fm_ref[rs, :], wpf_ref[...])) for rs in subs]
    merged = [(ga_ref[rs, :] * a + gf_ref[rs, :] * b).astype(BF16) for rs, (a, b) in zip(subs, ab)]
    ys = [_dot(mg, wout_ref[...]) for mg in merged]
    h2s = []
    for rs, y in zip(subs, ys):
        x1 = x_ref[rs, :] + gt1 * _rms(y, gpost_ref[...], EPS)
        x1_ref[rs, :] = x1
        h2s.append(_rms(x1, gffn_ref[...], EPS) * (1.0 + sc2) + sh2)
    logits = [_dot(h2.astype(BF16), wrx_ref[...]) for h2 in h2s]
    for rs, h2, lg in zip(subs, h2s, logits):
        lt = lg.T[0:N_EXPERTS]
        et = jnp.exp(lt - jnp.max(lt, axis=0, keepdims=True))
        aff = et / jnp.sum(et, axis=0, keepdims=True)
        for u in range(POST_SUB // LANES):
            afft_ref[:, rs.start // LANES + u, :] = aff[:, u * LANES:(u + 1) * LANES]
        for kc in range(ROW_TILE):
            h2t_ref[pl.ds(rs.start * ROW_TILE + kc, POST_SUB, stride=ROW_TILE), :] = h2[:, kc * LANES:(kc + 1) * LANES]


def _post_mixer(o, fm, ga, gf, x2d, mod3, mod_row, g_post, g_ffn, wpa, wpf, wout, wrx):
    t = x2d.shape[0]
    tm = POST_BLOCK
    row = lambda i: (i, 0)
    const = lambda i: (0, 0)
    return pl.pallas_call(
        _post_kernel,
        grid=(t // tm,),
        in_specs=[pl.BlockSpec((tm, QK_W), row),
                  pl.BlockSpec((tm, FOUR_W), row),
                  pl.BlockSpec((tm, D_MODEL), row),
                  pl.BlockSpec((tm, D_MODEL), row),
                  pl.BlockSpec((tm, D_MODEL), row),
                  pl.BlockSpec((1, 1, N_MOD * D_MODEL), lambda i: (mod_row(i, tm), 0, 0)),
                  pl.BlockSpec((1, D_MODEL), const),
                  pl.BlockSpec((1, D_MODEL), const),
                  pl.BlockSpec((QK_W, D_MODEL), const),
                  pl.BlockSpec((FOUR_W, D_MODEL), const),
                  pl.BlockSpec((D_MODEL, D_MODEL), const),
                  pl.BlockSpec((D_MODEL, LANES), const)],
        out_specs=[pl.BlockSpec((tm, D_MODEL), row),
                   pl.BlockSpec((tm * ROW_TILE, LANES), row),
                   pl.BlockSpec((N_EXPERTS, tm // LANES, LANES), lambda i: (0, i, 0))],
        out_shape=[jax.ShapeDtypeStruct((t, D_MODEL), F32),
                   jax.ShapeDtypeStruct((t * ROW_TILE, LANES), F32),
                   jax.ShapeDtypeStruct((N_EXPERTS, t // LANES, LANES), F32)],
        compiler_params=_cparams(("arbitrary",)),
        name="post_mixer",
    )(o, fm, ga, gf, x2d, mod3, g_post, g_ffn, wpa, wpf, wout, wrx)


RANGE_ROWS = 8


def _route_kernel(aff_ref, posm_ref, pack_ref, span_ref, ranges_ref, rows_ref, *, cap, n_tok):
    aff = aff_ref[...]
    nt = n_tok // LANES
    capf = float(cap)

    def count_ge(v):
        return jnp.sum(jnp.where(aff >= v, 1.0, 0.0), axis=(1, 2), keepdims=True)

    def search(i, thr):
        cand = thr | jnp.left_shift(jnp.int32(1), 30 - i)
        return jnp.where(count_ge(pltpu.bitcast(cand, F32)) >= capf, cand, thr)

    thr = lax.fori_loop(0, 31, search, jnp.zeros((N_EXPERTS, 1, 1), I32))
    lo = pltpu.bitcast(thr, F32)
    hi = pltpu.bitcast(thr + 1, F32)

    def refine(i, c):
        lo, hi = c
        mid = lo + (hi - lo) * 0.5
        ok = count_ge(mid) >= capf
        return jnp.where(ok, mid, lo), jnp.where(ok, hi, mid)

    lo, hi = lax.fori_loop(0, 12, refine, (lo, hi))
    gt = aff >= hi
    eq = (aff >= lo) & (aff < hi)
    n_tie = capf - jnp.sum(jnp.where(gt, 1.0, 0.0), axis=(1, 2), keepdims=True)

    sq0 = lax.broadcasted_iota(I32, (LANES, LANES), 0)
    sq1 = lax.broadcasted_iota(I32, (LANES, LANES), 1)
    along_total = jnp.concatenate([jnp.where(sq0 <= sq1, 1.0, 0.0), jnp.ones((LANES, LANES), F32)],
                                  axis=1).astype(BF16)
    m = N_EXPERTS * nt
    r0 = lax.broadcasted_iota(I32, (m, m), 0)
    r1 = lax.broadcasted_iota(I32, (m, m), 1)
    earlier = jnp.where((r0 // nt == r1 // nt) & (r1 < r0), 1.0, 0.0).astype(BF16)
    lane = lax.broadcasted_iota(I32, (1, LANES), 1)
    token = lax.broadcasted_iota(I32, (nt, LANES), 0) * LANES + lane

    def tile_counts(x):
        both = _dot(x.reshape(m, LANES).astype(BF16), along_total)
        total = both[:, LANES:]
        before = _dot(earlier, total.astype(BF16))
        shape = (N_EXPERTS, nt, LANES)
        return both[:, :LANES].reshape(shape), total.reshape(shape), before.reshape(shape)

    eq_f = jnp.where(eq, 1.0, 0.0)
    eq_along, _, eq_before = tile_counts(eq_f)
    sel = jnp.where(gt, 1.0, jnp.where(eq_along + eq_before <= n_tie, eq_f, 0.0))
    sel_along, sel_total, sel_before = tile_counts(sel)
    posm_ref[...] = jnp.where(sel > 0.5, sel_along + sel_before - sel, -1.0).astype(I32)

    cnt = jnp.sum(sel, axis=0)
    rows_before = jnp.sum(sel_before, axis=0)
    tok_start = _dot(cnt.astype(BF16), along_total[:, :LANES]) - cnt + rows_before
    k = jnp.zeros((nt, LANES), F32)
    slots_before = jnp.zeros((nt, LANES), F32)
    slots_here = jnp.zeros((nt, LANES), F32)
    for e in range(N_EXPERTS):
        pack_ref[e] = (tok_start + k).astype(I32) * (1 << TOKEN_BITS) + token
        k = k + sel[e]
        slots_before = jnp.where(lane == e, sel_before[e], slots_before)
        slots_here = jnp.where(lane == e, sel_total[e], slots_here)

    eye = jnp.where(sq0 == sq1, 1.0, 0.0).astype(BF16)
    tok_end = tok_start + cnt
    parts = []
    for v in (tok_start, tok_end):
        high = jnp.floor(v * (1.0 / LANES))
        parts += [high, v - high * LANES]
    for c in range(nt):
        rows4 = jnp.zeros((LANES, LANES), F32)
        for r, part in enumerate(parts):
            rows4 = jnp.where(sq0 == r, jnp.broadcast_to(part[c:c + 1, :], (LANES, LANES)), rows4)
        span_ref[c * LANES:(c + 1) * LANES, :] = lax.dot_general(
            eye, rows4.astype(BF16), (((1,), (1,)), ((), ())), preferred_element_type=F32)

    ranges_ref[...] = jnp.zeros_like(ranges_ref)
    for s in range(cap // LANES):
        done = jnp.where(slots_before + slots_here <= float(s * LANES), 1.0, 0.0)
        begun = jnp.where(slots_before < float((s + 1) * LANES), 1.0, 0.0)
        ranges_ref[s:s + 1, :] = jnp.sum(done, axis=0, keepdims=True).astype(I32)
        ranges_ref[RANGE_ROWS + s:RANGE_ROWS + s + 1, :] = jnp.sum(begun, axis=0, keepdims=True).astype(I32)
    rows_ref[...] = rows_before.astype(I32)


def _route(aff, cap):
    n_e, nt, _ = aff.shape
    n_tok = nt * LANES
    assert cap // LANES <= RANGE_ROWS
    return pl.pallas_call(
        functools.partial(_route_kernel, cap=cap, n_tok=n_tok),
        out_shape=[jax.ShapeDtypeStruct((n_e, nt, LANES), I32),
                   jax.ShapeDtypeStruct((n_e, nt, LANES), I32),
                   jax.ShapeDtypeStruct((n_tok, LANES), F32),
                   jax.ShapeDtypeStruct((2 * RANGE_ROWS, LANES), I32),
                   jax.ShapeDtypeStruct((nt, LANES), I32)],
        compiler_params=pltpu.CompilerParams(vmem_limit_bytes=VMEM_LIMIT),
        name="route",
    )(aff)


def _slots_kernel(ranges_ref, posm_ref, pack_ref, idx_ref, qslot_ref, acc_ref):
    e = pl.program_id(0)
    sub = lax.broadcasted_iota(I32, (LANES, LANES), 0)
    eye = sub == lax.broadcasted_iota(I32, (LANES, LANES), 1)
    n_tiles = idx_ref.shape[1]

    for s in range(n_tiles):
        slot = sub + s * LANES

        def body(c, acc, slot=slot):
            hit = posm_ref[e, pl.ds(c, 1), :] == slot
            return acc + jnp.where(hit, pack_ref[e, pl.ds(c, 1), :], 0)

        acc_ref[s] = lax.fori_loop(ranges_ref[s, e], ranges_ref[RANGE_ROWS + s, e], body,
                                   jnp.zeros((LANES, LANES), I32))

    def as_row(part):
        col = jnp.sum(part.astype(F32), axis=1, keepdims=True)
        return jnp.sum(jnp.where(eye, col, 0.0), axis=0, keepdims=True).astype(I32)

    for s in range(n_tiles):
        acc = acc_ref[s]
        idx_ref[0, s:s + 1, :] = as_row(acc & ((1 << TOKEN_BITS) - 1)) * ROW_TILE
        qslot_ref[0, s:s + 1, :] = as_row(lax.shift_right_logical(acc, TOKEN_BITS)) * ROW_TILE


def _slot_lists(ranges, posm4, qdst4, cap):
    ns = cap // LANES
    grid_spec = pltpu.PrefetchScalarGridSpec(
        num_scalar_prefetch=1,
        grid=(N_EXPERTS,),
        in_specs=[_whole_vmem(), _whole_vmem()],
        out_specs=[pl.BlockSpec((1, ns, LANES), lambda e, *_: (e, 0, 0)),
                   pl.BlockSpec((1, ns, LANES), lambda e, *_: (e, 0, 0))],
        scratch_shapes=[pltpu.VMEM((ns, LANES, LANES), I32)])
    idx, qslot = pl.pallas_call(
        _slots_kernel,
        grid_spec=grid_spec,
        out_shape=[jax.ShapeDtypeStruct((N_EXPERTS, ns, LANES), I32),
                   jax.ShapeDtypeStruct((N_EXPERTS, ns, LANES), I32)],
        compiler_params=_cparams(("arbitrary",)),
        name="slot_lists",
    )(ranges, posm4, qdst4)
    return idx.reshape(N_EXPERTS, cap), qslot.reshape(N_EXPERTS, cap)


def _moe_kernel(idxc_ref, idxs_ref, qc_ref, qs_ref,
                hc_ref, hs_ref, wr_ref, wg_ref, wu_ref, wd_ref, zc_ref, zs_ref,
                xbuf, ybuf, xb_ref, gate_ref, acc_ref, gsem, ssem, *, capc, caps):
    e = pl.program_id(0)
    j = pl.program_id(1)
    n_e = pl.num_programs(0)
    n_j = FF_STEPS
    slot = e % 2
    other = 1 - slot
    rows = capc + caps
    gc, gs = _per_step(capc), _per_step(caps)
    groups = ((hc_ref, idxc_ref, zc_ref, qc_ref, gc, 0), (hs_ref, idxs_ref, zs_ref, qs_ref, gs, gc * n_j))

    def tile(ref, first_sublane):
        return ref.at[pl.ds(pl.multiple_of(first_sublane, ROW_TILE), ROW_TILE), :]

    def gather(ex, sl, step, i, group):
        h_ref, idx_ref, _, _, per_step, base = group
        p = step * per_step + i
        src = tile(h_ref, idx_ref[ex * (per_step * n_j) + p])
        pltpu.make_async_copy(src, tile(xbuf.at[sl], (base + p) * ROW_TILE), gsem.at[sl]).start()

    def scatter(table_row, sl, step, i, group):
        _, _, z_ref, q_ref, per_step, base = group
        p = step * per_step + i
        dst = tile(z_ref, q_ref[table_row * (per_step * n_j) + p])
        pltpu.make_async_copy(tile(ybuf.at[sl], (base + p) * ROW_TILE), dst, ssem.at[sl]).start()

    def all_steps(fn):
        for group in groups:
            def body(p, carry, group=group):
                fn(p, group)
                return carry
            lax.fori_loop(0, group[4] * n_j, body, 0, unroll=8)

    def wait_all(buf, sem, sl):
        pltpu.make_async_copy(buf.at[sl], buf.at[sl], sem.at[sl]).wait()

    @pl.when((e == 0) & (j == 0))
    def _():
        ybuf[...] = jnp.zeros_like(ybuf)
        all_steps(lambda p, group: gather(0, 0, 0, p, group))

    @pl.when(j == 0)
    def _():
        wait_all(xbuf, gsem, slot)
        for base, n, dst in ((0, capc, 0), (gc * n_j, caps, capc)):
            for kc in range(ROW_TILE):
                tiles = xbuf[slot, pl.ds(base * ROW_TILE + kc, n, stride=ROW_TILE), :]
                xb_ref[dst:dst + n, kc * LANES:(kc + 1) * LANES] = tiles.astype(BF16)
        acc_ref[...] = jnp.zeros_like(acc_ref)
        logits = _dot(xb_ref[...], wr_ref[...])
        lane = lax.broadcasted_iota(I32, (1, LANES), 1)
        is_expert = lane < N_EXPERTS
        ex = jnp.exp(logits - jnp.max(jnp.where(is_expert, logits, -jnp.inf), axis=-1, keepdims=True))
        mine = jnp.sum(jnp.where(lane == e, ex, 0.0), axis=-1, keepdims=True)
        gate = mine / jnp.sum(jnp.where(is_expert, ex, 0.0), axis=-1, keepdims=True)
        gate_ref[...] = jnp.broadcast_to(gate, gate_ref.shape)

    nxt = jnp.minimum(e + 1, n_e - 1)
    for group in groups:
        for i in range(group[4]):
            gather(nxt, other, j, i, group)
            scatter(e, other, j, i, group)

    x = xb_ref[...]
    g = _dot(x, wg_ref[0].astype(BF16))
    u = _dot(x, wu_ref[0].astype(BF16))
    hid = (g * jax.nn.sigmoid(g) * u).astype(BF16)
    acc_ref[...] += _dot(hid, wd_ref[0].astype(BF16))

    @pl.when(j == n_j - 1)
    def _():
        @pl.when(e >= 1)
        def _():
            wait_all(ybuf, ssem, slot)

        gate = gate_ref[...]
        for base, n, src in ((0, capc, 0), (gc * n_j, caps, capc)):
            for kc in range(ROW_TILE):
                y = acc_ref[src:src + n, kc * LANES:(kc + 1) * LANES] * gate[src:src + n]
                ybuf[slot, pl.ds(base * ROW_TILE + kc, n, stride=ROW_TILE), :] = y

        @pl.when(e == n_e - 1)
        def _():
            all_steps(lambda p, group: scatter(e + 1, slot, 0, p, group))
            wait_all(ybuf, ssem, other)
            wait_all(ybuf, ssem, slot)
            wait_all(xbuf, gsem, other)


def _per_step(cap):
    return -(-cap // FF_STEPS)


def _copy_tables(idx, qslot, n_rows):
    n_e, cap = idx.shape
    padded = _per_step(cap) * FF_STEPS
    n_pad = padded - cap
    idx_p = jnp.concatenate([idx, jnp.zeros((n_e, n_pad), I32)], axis=1)
    spare = n_rows + jnp.arange(padded + n_e * n_pad, dtype=I32)
    lead = spare[:padded][None, :]
    pad_rows = spare[padded:].reshape(n_e, n_pad)
    q_p = jnp.concatenate([lead, jnp.concatenate([qslot // ROW_TILE, pad_rows], axis=1)], axis=0) * ROW_TILE
    return idx_p.reshape(-1), q_p.reshape(-1), n_rows + padded + n_e * n_pad


def _moe(idxc, idxs, qc, qs, hc, hs, wrx, w_gate, w_up, w_down):
    capc, caps = idxc.shape[1], idxs.shape[1]
    rows = capc + caps
    tf = FF_TILE
    idxc, qc, zc_rows = _copy_tables(idxc, qc, N_EXPERTS * capc)
    idxs, qs, zs_rows = _copy_tables(idxs, qs, N_EXPERTS * caps)
    buf_rows = (_per_step(capc) + _per_step(caps)) * FF_STEPS
    any_spec = pl.BlockSpec(memory_space=pl.ANY)
    grid_spec = pltpu.PrefetchScalarGridSpec(
        num_scalar_prefetch=4,
        grid=(N_EXPERTS, FF_STEPS),
        in_specs=[any_spec, any_spec,
                  pl.BlockSpec((D_MODEL, LANES), lambda e, j, *_: (0, 0)),
                  pl.BlockSpec((1, D_MODEL, tf), lambda e, j, *_: (e, 0, j)),
                  pl.BlockSpec((1, D_MODEL, tf), lambda e, j, *_: (e, 0, j)),
                  pl.BlockSpec((1, tf, D_MODEL), lambda e, j, *_: (e, j, 0))],
        out_specs=[any_spec, any_spec],
        scratch_shapes=[pltpu.VMEM((2, buf_rows * ROW_TILE, LANES), F32),
                        pltpu.VMEM((2, buf_rows * ROW_TILE, LANES), F32),
                        pltpu.VMEM((rows, D_MODEL), BF16),
                        pltpu.VMEM((rows, LANES), F32),
                        pltpu.VMEM((rows, D_MODEL), F32),
                        pltpu.SemaphoreType.DMA((2,)),
                        pltpu.SemaphoreType.DMA((2,))])
    return pl.pallas_call(
        functools.partial(_moe_kernel, capc=capc, caps=caps),
        grid_spec=grid_spec,
        out_shape=[jax.ShapeDtypeStruct((zc_rows * ROW_TILE, LANES), F32),
                   jax.ShapeDtypeStruct((zs_rows * ROW_TILE, LANES), F32)],
        compiler_params=_cparams(("arbitrary", "arbitrary")),
        name="expert_ffn",
    )(idxc, idxs, qc, qs, hc, hs, wrx, w_gate, w_up, w_down)


Z_BUFFERS = 3


def _combine_kernel(clo_ref, chi_ref, z_ref, span_ref, x1_ref, mod_ref, g_ref, o_ref,
                    zbuf, acc_ref, sems, state, *, n_chunks):
    b = pl.program_id(0)

    @pl.when(b == 0)
    def _():
        state[0] = 0
        state[1] = 0

    def chunk_copies(c):
        first = pl.multiple_of(c * ROW_CHUNK, ROW_CHUNK)
        return [pltpu.make_async_copy(z_ref.at[pl.ds(first, ROW_CHUNK), kc, :], zbuf.at[c % Z_BUFFERS, kc],
                                      sems.at[c % Z_BUFFERS]) for kc in range(ROW_TILE)]

    acc_ref[...] = jnp.zeros_like(acc_ref)
    first_row = span_ref[:, 0:1] * LANES + span_ref[:, 1:2]
    end_row = span_ref[:, 2:3] * LANES + span_ref[:, 3:4]
    lane_row = lax.broadcasted_iota(I32, (TOK_BLOCK, ROW_CHUNK), 1)

    def body(c, carry):
        for _ in range(Z_BUFFERS):
            nxt = state[0]

            @pl.when(nxt <= jnp.minimum(c + Z_BUFFERS - 1, n_chunks - 1))
            def _():
                for copy in chunk_copies(nxt):
                    copy.start()
                state[0] = nxt + 1

        @pl.when(state[1] <= c)
        def _():
            for copy in chunk_copies(c):
                copy.wait()
            state[1] = c + 1

        slot = c % Z_BUFFERS
        row = (c * ROW_CHUNK + lane_row).astype(F32)
        onehot = jnp.where((row >= first_row) & (row < end_row), 1.0, 0.0).astype(BF16)
        y = jnp.concatenate([zbuf[slot, kc] for kc in range(ROW_TILE)], axis=1)
        acc_ref[...] += _dot(onehot, y.astype(BF16))
        return carry

    lax.fori_loop(clo_ref[b], chi_ref[b], body, 0)
    gt2 = mod_ref[0][:, 5 * D_MODEL:6 * D_MODEL]
    o_ref[...] = x1_ref[...] + gt2 * _rms(acc_ref[...], g_ref[...], EPS)


def _combine(ranges, z, span, x1, mod3, mod_row, g_post_ffn):
    clo, chi = ranges
    t = x1.shape[0]
    tb = TOK_BLOCK
    grid_spec = pltpu.PrefetchScalarGridSpec(
        num_scalar_prefetch=2,
        grid=(t // tb,),
        in_specs=[pl.BlockSpec(memory_space=pl.ANY),
                  pl.BlockSpec((tb, LANES), lambda b, *_: (b, 0)),
                  pl.BlockSpec((tb, D_MODEL), lambda b, *_: (b, 0)),
                  pl.BlockSpec((1, 1, N_MOD * D_MODEL), lambda b, *_: (mod_row(b, tb), 0, 0)),
                  pl.BlockSpec((1, D_MODEL), lambda b, *_: (0, 0))],
        out_specs=pl.BlockSpec((tb, D_MODEL), lambda b, *_: (b, 0)),
        scratch_shapes=[pltpu.VMEM((Z_BUFFERS, ROW_TILE, ROW_CHUNK, LANES), F32),
                        pltpu.VMEM((tb, D_MODEL), F32),
                        pltpu.SemaphoreType.DMA((Z_BUFFERS,)),
                        pltpu.SMEM((2,), I32)])
    return pl.pallas_call(
        functools.partial(_combine_kernel, n_chunks=2 * t // ROW_CHUNK),
        grid_spec=grid_spec,
        out_shape=jax.ShapeDtypeStruct((t, D_MODEL), F32),
        compiler_params=_cparams(("arbitrary",)),
        name="combine",
    )(clo, chi, z.reshape(-1, ROW_TILE, LANES), span, x1, mod3, g_post_ffn)


def _rope_tables(seq):
    half = HEAD_DIM // 4
    freqs = ROPE_THETA ** (-np.arange(half, dtype=np.float64) / half)
    s = np.arange(seq)
    row = (s // GRID_W)[:, None] * freqs[None, :]
    col = (s % GRID_W)[:, None] * freqs[None, :]
    ang = np.concatenate([row, row, col, col], axis=1)
    ang = np.tile(ang, (1, QK_W // HEAD_DIM))
    lane = np.arange(QK_W)
    sign = np.where((lane % 32) < 16, -1.0, 1.0)[None, :]
    return (jnp.asarray(np.cos(ang), dtype=F32), jnp.asarray(np.sin(ang) * sign, dtype=F32))


def _combine_ranges(rows, n_tok):
    step = TOK_BLOCK // LANES
    nb = n_tok // TOK_BLOCK
    lo = rows[0:nb * step:step, 0]
    hi = jnp.concatenate([lo[1:], jnp.full((1,), 2 * n_tok, I32)])
    return (lo // ROW_CHUNK).astype(I32), ((hi + ROW_CHUNK - 1) // ROW_CHUNK).astype(I32)


def kernel(x_prompt, x_sample, c, cache_k, cache_v, c_ctx, w_mod, b_mod, g_pre_mix, g_post_mix, g_pre_ffn, g_post_ffn, w_in, lam_q1, lam_k1, lam_q2, lam_k2, g_subln, w_proj_attn, w_proj_fourier, w_out, w_router, w_gate, w_up, w_down):
    assert w_mod.shape[0] == 1
    lam_init = 0.8 - 0.6 * math.exp(-0.3 * 0)
    bp, sp, _ = x_prompt.shape
    bs, ss, _ = x_sample.shape

    cond8 = jnp.concatenate([c_ctx[None, :], c, jnp.zeros((8 - 1 - bs, D_MODEL), F32)], axis=0)
    mod3 = _modulation(cond8, w_mod[0], b_mod).reshape(8, 1, N_MOD * D_MODEL)

    w_in_b = w_in[0].astype(BF16)
    wpa = w_proj_attn[0].astype(BF16)
    wpf = w_proj_fourier[0].astype(BF16)
    wout = w_out[0].astype(BF16)
    wr = w_router[0].astype(BF16)
    wrx = jnp.concatenate([wr, jnp.zeros((D_MODEL, LANES - N_EXPERTS), BF16)], axis=1)
    lam_p = jnp.concatenate([lam_q1, lam_k1, lam_q2, lam_k2], axis=0)

    groups = []
    for x, seq, positional, ctx in ((x_prompt, sp, False, None),
                                    (x_sample, ss, True, (cache_k, cache_v))):
        nb = x.shape[0]
        t = nb * seq
        x2d = x.reshape(t, D_MODEL)
        if positional:
            mod_row = lambda i, tm, seq=seq: 1 + (i * tm) // seq
        else:
            mod_row = lambda i, tm: 0
        self_contained = ctx is None and PRE_BLOCK % seq == 0
        pre = _pre_mixer(x2d, mod3, mod_row, g_pre_mix, w_in_b, _rope_tables(seq) if positional else None, seq,
                         self_contained, (lam_p, g_subln, lam_init), tm=PRE_BLOCK if self_contained else ROW_BLOCK)
        if self_contained:
            o, fm, ga, gf = pre[:4]
        else:
            q, k, v, f, ga, gf = pre
            o = _attention(lam_p, g_subln, q, k, v, ctx, seq, lam_init)
            fm = _fourier(f, seq)
        x1, h2t, aff_t = _post_mixer(o, fm, ga, gf, x2d, mod3, mod_row, g_post_mix, g_pre_ffn,
                                     wpa, wpf, wout, wrx)
        cap = 2 * t // N_EXPERTS
        assert t <= 1 << TOKEN_BITS
        posm, pack, span, ranges, rows = _route(aff_t, cap)
        idx, qslot = _slot_lists(ranges, posm, pack, cap)
        groups.append(dict(x1=x1, h2t=h2t, idx=idx, qslot=qslot, span=span, ranges=_combine_ranges(rows, t),
                           mod_row=mod_row, cache=pre[4:] if self_contained else None, shape=x.shape))

    gc, gs_ = groups
    zc, zs = _moe(gc["idx"], gs_["idx"], gc["qslot"], gs_["qslot"], gc["h2t"], gs_["h2t"], wrx,
                  w_gate[0], w_up[0], w_down[0])
    outs = []
    for g, z in ((gc, zc), (gs_, zs)):
        out = _combine(g["ranges"], z, g["span"], g["x1"], mod3, g["mod_row"], g_post_ffn)
        outs.append(out.reshape(g["shape"]))
    new_k, new_v = gc["cache"]
    return (outs[0], outs[1], new_k, new_v)
```

```python
import functools
import math

import numpy as np
import jax
import jax.numpy as jnp
from jax import lax
from jax.experimental import pallas as pl
from jax.experimental.pallas import tpu as pltpu

F32 = jnp.float32
BF16 = jnp.bfloat16
I32 = jnp.int32

D_MODEL = 1024
N_HEADS = 6
HEAD_DIM = 64
V_DIM = 128
QK_W = 768
FOUR_W = 256
FOUR_G = 64
IN_W = 4608
N_EXPERTS = 16
D_FF = 2816
N_MOD = 6
GRID_W = 64
ROPE_THETA = 10000.0
EPS = 1e-6
SUBLN_EPS = 1e-5

LANES = 128
ROW_BLOCK = 256
PRE_BLOCK = 512
GATE_CHUNK = 256
POST_BLOCK = 1024
POST_SUB = 256
TOK_BLOCK = 256
ROW_CHUNK = 256
FF_TILE = 256
FF_STEPS = D_FF // FF_TILE
ROW_TILE = D_MODEL // LANES
TOKEN_BITS = 13
VMEM_LIMIT = 56 * 1024 * 1024


def _cparams(sem):
    return pltpu.CompilerParams(dimension_semantics=sem, vmem_limit_bytes=VMEM_LIMIT)


def _dot(a, b):
    return jnp.dot(a, b, preferred_element_type=F32)


def _rms(x, g, eps):
    return x * lax.rsqrt(jnp.mean(x * x, axis=-1, keepdims=True) + eps) * g


def _whole_vmem():
    return pl.BlockSpec(memory_space=pltpu.MemorySpace.VMEM)


def _mod_kernel(c_ref, w_ref, b_ref, o_ref):
    c = c_ref[...]
    s = c * jax.nn.sigmoid(c)
    o_ref[...] = _dot(s.astype(BF16), w_ref[...].astype(BF16)) + b_ref[...]


def _modulation(cond8, w_mod, b_mod):
    tn = 1024
    n = N_MOD * D_MODEL
    return pl.pallas_call(
        _mod_kernel,
        grid=(n // tn,),
        in_specs=[pl.BlockSpec((8, D_MODEL), lambda j: (0, 0)),
                  pl.BlockSpec((D_MODEL, tn), lambda j: (0, j)),
                  pl.BlockSpec((1, tn), lambda j: (0, j))],
        out_specs=pl.BlockSpec((8, tn), lambda j: (0, j)),
        out_shape=jax.ShapeDtypeStruct((8, n), F32),
        compiler_params=_cparams(("arbitrary",)),
        name="modulation",
    )(cond8, w_mod, b_mod)


def _diff_lambda(lp, lam_init):
    s1 = jnp.sum(lp[0:1] * lp[1:2], axis=-1, keepdims=True)
    s2 = jnp.sum(lp[2:3] * lp[3:4], axis=-1, keepdims=True)
    return jnp.exp(s1) - jnp.exp(s2) + lam_init


def _diff_attention_head(q, k, v, lam, g_subln, lam_init):
    comp1 = lax.broadcasted_iota(I32, (1, V_DIM), 1) < HEAD_DIM
    scale = jnp.asarray(HEAD_DIM ** -0.5, BF16)
    v_ones = jnp.concatenate([v, jnp.ones_like(v)], axis=1)

    def attend(qc):
        s = lax.dot_general(qc, k, (((1,), (1,)), ((), ())), preferred_element_type=F32)
        ex = jnp.exp(s - jnp.max(s, axis=-1, keepdims=True)).astype(BF16)
        ov = _dot(ex, v_ones)
        return ov[:, 0:V_DIM] / ov[:, V_DIM:2 * V_DIM]

    qs = q * scale
    zero = jnp.zeros_like(qs)
    o = attend(jnp.where(comp1, qs, zero)) - lam * attend(jnp.where(comp1, zero, qs))
    return (_rms(o, g_subln, SUBLN_EPS) * (1.0 - lam_init)).astype(BF16)


def _rope(z, cos, sin_signed, first_half):
    fwd = pltpu.roll(z, QK_W - 16, axis=1)
    bwd = pltpu.roll(z, 16, axis=1)
    return z * cos + jnp.where(first_half, fwd, bwd) * sin_signed


def _pre_kernel(*refs, positional, fuse_seq, lam_init):
    it = iter(refs)
    x_ref, mod_ref, g_ref, w_ref = next(it), next(it), next(it), next(it)
    if positional:
        cos_ref, sin_ref = next(it), next(it)
    if fuse_seq:
        dft_refs = [next(it) for _ in range(4)]
        lam_ref, gs_ref = next(it), next(it)
        o_ref, f_ref, ga_ref, gf_ref, kc_ref, vc_ref = (next(it) for _ in range(6))
    else:
        q_ref, k_ref, v_ref, f_ref, ga_ref, gf_ref = (next(it) for _ in range(6))

    m = mod_ref[0]
    sh1 = m[:, 0:D_MODEL]
    sc1 = m[:, D_MODEL:2 * D_MODEL]
    h = _rms(x_ref[...], g_ref[...], EPS) * (1.0 + sc1) + sh1
    hb = h.astype(BF16)

    def proj(lo, hi):
        return _dot(hb, w_ref[:, lo:hi])

    zq = proj(0, QK_W)
    zk = proj(QK_W, 2 * QK_W)
    zv = proj(2 * QK_W, 3 * QK_W)
    if positional:
        lane = lax.broadcasted_iota(I32, (1, QK_W), 1)
        first_half = (lane % 32) < 16
        cos = cos_ref[...]
        sin_signed = sin_ref[...]
        zq = _rope(zq, cos, sin_signed, first_half)
        zk = _rope(zk, cos, sin_signed, first_half)
    qb, kb, vb = zq.astype(BF16), zk.astype(BF16), zv.astype(BF16)
    f0 = 3 * QK_W
    g0 = f0 + FOUR_W

    def gate_chunk(ref, lo, c):
        cols = slice(c * GATE_CHUNK, (c + 1) * GATE_CHUNK)
        ref[:, cols] = jax.nn.sigmoid(proj(lo + cols.start, lo + cols.stop)).astype(BF16)

    gate_work = [functools.partial(gate_chunk, ref, lo, c)
                 for ref, lo in ((ga_ref, g0), (gf_ref, g0 + D_MODEL)) for c in range(D_MODEL // GATE_CHUNK)]
    if not fuse_seq:
        q_ref[...] = qb
        k_ref[...] = kb
        v_ref[...] = vb
        f_ref[...] = proj(f0, g0)
        for work in gate_work:
            work()
        return

    seqs = [slice(b * fuse_seq, (b + 1) * fuse_seq) for b in range(x_ref.shape[0] // fuse_seq)]
    f = proj(f0, g0)
    lam = _diff_lambda(lam_ref[...], lam_init)
    for b, rs in enumerate(seqs):
        f_ref[rs, :] = _dft_real(f[rs, :], *dft_refs)
        for hd in range(N_HEADS):
            sl = slice(hd * V_DIM, (hd + 1) * V_DIM)
            kc_ref[b, 0, hd] = zk[rs, sl]
            vc_ref[b, 0, hd] = zv[rs, sl]
            o_ref[rs, sl] = _diff_attention_head(qb[rs, sl], kb[rs, sl], vb[rs, sl], lam, gs_ref[...], lam_init)
            if gate_work:
                gate_work.pop(0)()
    for work in gate_work:
        work()


def _pre_mixer(x2d, mod3, mod_row, g_pre, w_in_b, rope_tabs, seq, self_contained, attn_params, tm):
    t = x2d.shape[0]
    positional = rope_tabs is not None
    lam_p, g_subln, lam_init = attn_params
    if self_contained:
        assert tm % seq == 0
    blocks_per_seq = max(seq // tm, 1)
    row = lambda i: (i, 0)
    in_specs = [pl.BlockSpec((tm, D_MODEL), row),
                pl.BlockSpec((1, 1, N_MOD * D_MODEL), lambda i: (mod_row(i, tm), 0, 0)),
                pl.BlockSpec((1, D_MODEL), lambda i: (0, 0)),
                _whole_vmem()]
    args = [x2d, mod3, g_pre, w_in_b]
    if positional:
        in_specs += [pl.BlockSpec((tm, QK_W), lambda i: (i % blocks_per_seq, 0))] * 2
        args += list(rope_tabs)
    if self_contained:
        consts = _dft_consts(seq) + (lam_p, g_subln)
        in_specs += [pl.BlockSpec(c.shape, lambda i: (0, 0)) for c in consts]
        args += list(consts)
    n_wide = 1 if self_contained else 3
    out_shape = [jax.ShapeDtypeStruct((t, QK_W), BF16)] * n_wide + [
        jax.ShapeDtypeStruct((t, FOUR_W), BF16 if self_contained else F32),
        jax.ShapeDtypeStruct((t, D_MODEL), BF16),
        jax.ShapeDtypeStruct((t, D_MODEL), BF16)]
    out_specs = [pl.BlockSpec((tm, QK_W), row)] * n_wide + [
        pl.BlockSpec((tm, FOUR_W), row),
        pl.BlockSpec((tm, D_MODEL), row),
        pl.BlockSpec((tm, D_MODEL), row)]
    if self_contained:
        nb = t // seq
        cshape = jax.ShapeDtypeStruct((nb, 1, N_HEADS, seq, V_DIM), F32)
        cspec = pl.BlockSpec((tm // seq, 1, N_HEADS, seq, V_DIM), lambda i: (i, 0, 0, 0, 0))
        out_shape += [cshape, cshape]
        out_specs += [cspec, cspec]
    return pl.pallas_call(
        functools.partial(_pre_kernel, positional=positional, fuse_seq=seq if self_contained else 0,
                          lam_init=lam_init),
        grid=(t // tm,),
        in_specs=in_specs,
        out_specs=out_specs,
        out_shape=out_shape,
        compiler_params=_cparams(("arbitrary",)),
        name="pre_mixer",
    )(*args)


def _attn_kernel(*refs, lam_init, has_ctx):
    it = iter(refs)
    lam_ref, gs_ref, q_ref, k_ref, v_ref = (next(it) for _ in range(5))
    if has_ctx:
        ck_ref, cv_ref = next(it), next(it)
    o_ref = next(it)

    lam = _diff_lambda(lam_ref[...], lam_init)
    for hd in range(N_HEADS):
        sl = slice(hd * V_DIM, (hd + 1) * V_DIM)
        k = k_ref[:, sl]
        v = v_ref[:, sl]
        if has_ctx:
            k = jnp.concatenate([ck_ref[0, 0, hd].astype(BF16), k], axis=0)
            v = jnp.concatenate([cv_ref[0, 0, hd].astype(BF16), v], axis=0)
        o_ref[:, sl] = _diff_attention_head(q_ref[:, sl], k, v, lam, gs_ref[...], lam_init)


def _attention(lam_p, g_subln, q, k, v, ctx, seq, lam_init):
    t = q.shape[0]
    tq = ROW_BLOCK
    qb = seq // tq
    has_ctx = ctx is not None
    in_specs = [pl.BlockSpec((4, HEAD_DIM), lambda b, i: (0, 0)),
                pl.BlockSpec((1, V_DIM), lambda b, i: (0, 0)),
                pl.BlockSpec((tq, QK_W), lambda b, i: (b * qb + i, 0)),
                pl.BlockSpec((seq, QK_W), lambda b, i: (b, 0)),
                pl.BlockSpec((seq, QK_W), lambda b, i: (b, 0))]
    args = [lam_p, g_subln, q, k, v]
    if has_ctx:
        past = ctx[0].shape[3]
        cspec = pl.BlockSpec((1, 1, N_HEADS, past, V_DIM), lambda b, i: (b, 0, 0, 0, 0))
        in_specs += [cspec, cspec]
        args += list(ctx)
    return pl.pallas_call(
        functools.partial(_attn_kernel, lam_init=lam_init, has_ctx=has_ctx),
        grid=(t // seq, qb),
        in_specs=in_specs,
        out_specs=pl.BlockSpec((tq, QK_W), lambda b, i: (b * qb + i, 0)),
        out_shape=jax.ShapeDtypeStruct((t, QK_W), BF16),
        compiler_params=_cparams(("arbitrary", "arbitrary")),
        name="diff_attention",
    )(*args)


def _dft_real(f, bc_ref, bs_ref, cs_ref, ss_ref):
    fb = f.astype(BF16)
    u = _dot(fb, bc_ref[...].astype(BF16)).astype(BF16)
    w = _dot(fb, bs_ref[...].astype(BF16)).astype(BF16)
    return (_dot(cs_ref[...].astype(BF16), u) - _dot(ss_ref[...].astype(BF16), w)).astype(BF16)


def _fourier_kernel(f_ref, bc_ref, bs_ref, cs_ref, ss_ref, o_ref):
    o_ref[...] = _dft_real(f_ref[...], bc_ref, bs_ref, cs_ref, ss_ref)


def _dft_consts(seq):
    c = np.arange(FOUR_G)
    ang_c = 2.0 * np.pi * ((c[:, None] * c[None, :]) % FOUR_G) / FOUR_G
    eye = np.eye(FOUR_W // FOUR_G)
    bc = np.kron(eye, np.cos(ang_c)) / math.sqrt(FOUR_G)
    bs = np.kron(eye, np.sin(ang_c)) / math.sqrt(FOUR_G)
    s = np.arange(seq)
    ang_s = 2.0 * np.pi * ((s[:, None] * s[None, :]) % seq) / seq
    cs = np.cos(ang_s) / math.sqrt(seq)
    ss = np.sin(ang_s) / math.sqrt(seq)
    return tuple(jnp.asarray(a, dtype=F32) for a in (bc, bs, cs, ss))


def _fourier(f, seq):
    t = f.shape[0]
    bc, bs, cs, ss = _dft_consts(seq)
    const = lambda b: (0, 0)
    return pl.pallas_call(
        _fourier_kernel,
        grid=(t // seq,),
        in_specs=[pl.BlockSpec((seq, FOUR_W), lambda b: (b, 0)),
                  pl.BlockSpec((FOUR_W, FOUR_W), const),
                  pl.BlockSpec((FOUR_W, FOUR_W), const),
                  pl.BlockSpec((seq, seq), const),
                  pl.BlockSpec((seq, seq), const)],
        out_specs=pl.BlockSpec((seq, FOUR_W), lambda b: (b, 0)),
        out_shape=jax.ShapeDtypeStruct((t, FOUR_W), BF16),
        compiler_params=_cparams(("arbitrary",)),
        name="fourier_mix",
    )(f, bc, bs, cs, ss)


def _post_kernel(o_ref, fm_ref, ga_ref, gf_ref, x_ref, mod_ref, gpost_ref, gffn_ref,
                 wpa_ref, wpf_ref, wout_ref, wrx_ref,
                 x1_ref, h2t_ref, afft_ref):
    m = mod_ref[0]
    gt1 = m[:, 2 * D_MODEL:3 * D_MODEL]
    sh2 = m[:, 3 * D_MODEL:4 * D_MODEL]
    sc2 = m[:, 4 * D_MODEL:5 * D_MODEL]
    subs = [slice(r0, r0 + POST_SUB) for r0 in range(0, o_ref.shape[0], POST_SUB)]
    ab = [(_dot(o_ref[rs, :], wpa_ref[...]), _dot(fm_ref[rs, :], wpf_ref[...])) for rs in subs]
    merged = [(ga_ref[rs, :] * a + gf_ref[rs, :] * b).astype(BF16) for rs, (a, b) in zip(subs, ab)]
    ys = [_dot(mg, wout_ref[...]) for mg in merged]
    h2s = []
    for rs, y in zip(subs, ys):
        x1 = x_ref[rs, :] + gt1 * _rms(y, gpost_ref[...], EPS)
        x1_ref[rs, :] = x1
        h2s.append(_rms(x1, gffn_ref[...], EPS) * (1.0 + sc2) + sh2)
    logits = [_dot(h2.astype(BF16), wrx_ref[...]) for h2 in h2s]
    for rs, h2, lg in zip(subs, h2s, logits):
        lt = lg.T[0:N_EXPERTS]
        et = jnp.exp(lt - jnp.max(lt, axis=0, keepdims=True))
        aff = et / jnp.sum(et, axis=0, keepdims=True)
        for u in range(POST_SUB // LANES):
            afft_ref[:, rs.start // LANES + u, :] = aff[:, u * LANES:(u + 1) * LANES]
        for kc in range(ROW_TILE):
            h2t_ref[pl.ds(rs.start * ROW_TILE + kc, POST_SUB, stride=ROW_TILE), :] = h2[:, kc * LANES:(kc + 1) * LANES]


def _post_mixer(o, fm, ga, gf, x2d, mod3, mod_row, g_post, g_ffn, wpa, wpf, wout, wrx):
    t = x2d.shape[0]
    tm = POST_BLOCK
    row = lambda i: (i, 0)
    const = lambda i: (0, 0)
    return pl.pallas_call(
        _post_kernel,
        grid=(t // tm,),
        in_specs=[pl.BlockSpec((tm, QK_W), row),
                  pl.BlockSpec((tm, FOUR_W), row),
                  pl.BlockSpec((tm, D_MODEL), row),
                  pl.BlockSpec((tm, D_MODEL), row),
                  pl.BlockSpec((tm, D_MODEL), row),
                  pl.BlockSpec((1, 1, N_MOD * D_MODEL), lambda i: (mod_row(i, tm), 0, 0)),
                  pl.BlockSpec((1, D_MODEL), const),
                  pl.BlockSpec((1, D_MODEL), const),
                  pl.BlockSpec((QK_W, D_MODEL), const),
                  pl.BlockSpec((FOUR_W, D_MODEL), const),
                  pl.BlockSpec((D_MODEL, D_MODEL), const),
                  pl.BlockSpec((D_MODEL, LANES), const)],
        out_specs=[pl.BlockSpec((tm, D_MODEL), row),
                   pl.BlockSpec((tm * ROW_TILE, LANES), row),
                   pl.BlockSpec((N_EXPERTS, tm // LANES, LANES), lambda i: (0, i, 0))],
        out_shape=[jax.ShapeDtypeStruct((t, D_MODEL), F32),
                   jax.ShapeDtypeStruct((t * ROW_TILE, LANES), F32),
                   jax.ShapeDtypeStruct((N_EXPERTS, t // LANES, LANES), F32)],
        compiler_params=_cparams(("arbitrary",)),
        name="post_mixer",
    )(o, fm, ga, gf, x2d, mod3, g_post, g_ffn, wpa, wpf, wout, wrx)


RANGE_ROWS = 8


def _route_kernel(aff_ref, posm_ref, pack_ref, span_ref, ranges_ref, rows_ref, *, cap, n_tok):
    aff = aff_ref[...]
    nt = n_tok // LANES
    capf = float(cap)

    def count_ge(v):
        return jnp.sum(jnp.where(aff >= v, 1.0, 0.0), axis=(1, 2), keepdims=True)

    def search(i, thr):
        cand = thr | jnp.left_shift(jnp.int32(1), 30 - i)
        return jnp.where(count_ge(pltpu.bitcast(cand, F32)) >= capf, cand, thr)

    thr = lax.fori_loop(0, 31, search, jnp.zeros((N_EXPERTS, 1, 1), I32))
    lo = pltpu.bitcast(thr, F32)
    hi = pltpu.bitcast(thr + 1, F32)

    def refine(i, c):
        lo, hi = c
        mid = lo + (hi - lo) * 0.5
        ok = count_ge(mid) >= capf
        return jnp.where(ok, mid, lo), jnp.where(ok, hi, mid)

    lo, hi = lax.fori_loop(0, 12, refine, (lo, hi))
    gt = aff >= hi
    eq = (aff >= lo) & (aff < hi)
    n_tie = capf - jnp.sum(jnp.where(gt, 1.0, 0.0), axis=(1, 2), keepdims=True)

    sq0 = lax.broadcasted_iota(I32, (LANES, LANES), 0)
    sq1 = lax.broadcasted_iota(I32, (LANES, LANES), 1)
    along_total = jnp.concatenate([jnp.where(sq0 <= sq1, 1.0, 0.0), jnp.ones((LANES, LANES), F32)],
                                  axis=1).astype(BF16)
    m = N_EXPERTS * nt
    r0 = lax.broadcasted_iota(I32, (m, m), 0)
    r1 = lax.broadcasted_iota(I32, (m, m), 1)
    earlier = jnp.where((r0 // nt == r1 // nt) & (r1 < r0), 1.0, 0.0).astype(BF16)
    lane = lax.broadcasted_iota(I32, (1, LANES), 1)
    token = lax.broadcasted_iota(I32, (nt, LANES), 0) * LANES + lane

    def tile_counts(x):
        both = _dot(x.reshape(m, LANES).astype(BF16), along_total)
        total = both[:, LANES:]
        before = _dot(earlier, total.astype(BF16))
        shape = (N_EXPERTS, nt, LANES)
        return both[:, :LANES].reshape(shape), total.reshape(shape), before.reshape(shape)

    eq_f = jnp.where(eq, 1.0, 0.0)
    eq_along, _, eq_before = tile_counts(eq_f)
    sel = jnp.where(gt, 1.0, jnp.where(eq_along + eq_before <= n_tie, eq_f, 0.0))
    sel_along, sel_total, sel_before = tile_counts(sel)
    posm_ref[...] = jnp.where(sel > 0.5, sel_along + sel_before - sel, -1.0).astype(I32)

    cnt = jnp.sum(sel, axis=0)
    rows_before = jnp.sum(sel_before, axis=0)
    tok_start = _dot(cnt.astype(BF16), along_total[:, :LANES]) - cnt + rows_before
    k = jnp.zeros((nt, LANES), F32)
    slots_before = jnp.zeros((nt, LANES), F32)
    slots_here = jnp.zeros((nt, LANES), F32)
    for e in range(N_EXPERTS):
        pack_ref[e] = (tok_start + k).astype(I32) * (1 << TOKEN_BITS) + token
        k = k + sel[e]
        slots_before = jnp.where(lane == e, sel_before[e], slots_before)
        slots_here = jnp.where(lane == e, sel_total[e], slots_here)

    eye = jnp.where(sq0 == sq1, 1.0, 0.0).astype(BF16)
    tok_end = tok_start + cnt
    parts = []
    for v in (tok_start, tok_end):
        high = jnp.floor(v * (1.0 / LANES))
        parts += [high, v - high * LANES]
    for c in range(nt):
        rows4 = jnp.zeros((LANES, LANES), F32)
        for r, part in enumerate(parts):
            rows4 = jnp.where(sq0 == r, jnp.broadcast_to(part[c:c + 1, :], (LANES, LANES)), rows4)
        span_ref[c * LANES:(c + 1) * LANES, :] = lax.dot_general(
            eye, rows4.astype(BF16), (((1,), (1,)), ((), ())), preferred_element_type=F32)

    ranges_ref[...] = jnp.zeros_like(ranges_ref)
    for s in range(cap // LANES):
        done = jnp.where(slots_before + slots_here <= float(s * LANES), 1.0, 0.0)
        begun = jnp.where(slots_before < float((s + 1) * LANES), 1.0, 0.0)
        ranges_ref[s:s + 1, :] = jnp.sum(done, axis=0, keepdims=True).astype(I32)
        ranges_ref[RANGE_ROWS + s:RANGE_ROWS + s + 1, :] = jnp.sum(begun, axis=0, keepdims=True).astype(I32)
    rows_ref[...] = rows_before.astype(I32)


def _route(aff, cap):
    n_e, nt, _ = aff.shape
    n_tok = nt * LANES
    assert cap // LANES <= RANGE_ROWS
    return pl.pallas_call(
        functools.partial(_route_kernel, cap=cap, n_tok=n_tok),
        out_shape=[jax.ShapeDtypeStruct((n_e, nt, LANES), I32),
                   jax.ShapeDtypeStruct((n_e, nt, LANES), I32),
                   jax.ShapeDtypeStruct((n_tok, LANES), F32),
                   jax.ShapeDtypeStruct((2 * RANGE_ROWS, LANES), I32),
                   jax.ShapeDtypeStruct((nt, LANES), I32)],
        compiler_params=pltpu.CompilerParams(vmem_limit_bytes=VMEM_LIMIT),
        name="route",
    )(aff)


def _slots_kernel(ranges_ref, posm_ref, pack_ref, idx_ref, qslot_ref, acc_ref):
    e = pl.program_id(0)
    sub = lax.broadcasted_iota(I32, (LANES, LANES), 0)
    eye = sub == lax.broadcasted_iota(I32, (LANES, LANES), 1)
    n_tiles = idx_ref.shape[1]

    for s in range(n_tiles):
        slot = sub + s * LANES

        def body(c, acc, slot=slot):
            hit = posm_ref[e, pl.ds(c, 1), :] == slot
            return acc + jnp.where(hit, pack_ref[e, pl.ds(c, 1), :], 0)

        acc_ref[s] = lax.fori_loop(ranges_ref[s, e], ranges_ref[RANGE_ROWS + s, e], body,
                                   jnp.zeros((LANES, LANES), I32))

    def as_row(part):
        col = jnp.sum(part.astype(F32), axis=1, keepdims=True)
        return jnp.sum(jnp.where(eye, col, 0.0), axis=0, keepdims=True).astype(I32)

    for s in range(n_tiles):
        acc = acc_ref[s]
        idx_ref[0, s:s + 1, :] = as_row(acc & ((1 << TOKEN_BITS) - 1)) * ROW_TILE
        qslot_ref[0, s:s + 1, :] = as_row(lax.shift_right_logical(acc, TOKEN_BITS)) * ROW_TILE


def _slot_lists(ranges, posm4, qdst4, cap):
    ns = cap // LANES
    grid_spec = pltpu.PrefetchScalarGridSpec(
        num_scalar_prefetch=1,
        grid=(N_EXPERTS,),
        in_specs=[_whole_vmem(), _whole_vmem()],
        out_specs=[pl.BlockSpec((1, ns, LANES), lambda e, *_: (e, 0, 0)),
                   pl.BlockSpec((1, ns, LANES), lambda e, *_: (e, 0, 0))],
        scratch_shapes=[pltpu.VMEM((ns, LANES, LANES), I32)])
    idx, qslot = pl.pallas_call(
        _slots_kernel,
        grid_spec=grid_spec,
        out_shape=[jax.ShapeDtypeStruct((N_EXPERTS, ns, LANES), I32),
                   jax.ShapeDtypeStruct((N_EXPERTS, ns, LANES), I32)],
        compiler_params=_cparams(("arbitrary",)),
        name="slot_lists",
    )(ranges, posm4, qdst4)
    return idx.reshape(N_EXPERTS, cap), qslot.reshape(N_EXPERTS, cap)


def _moe_kernel(idxc_ref, idxs_ref, qc_ref, qs_ref,
                hc_ref, hs_ref, wr_ref, wg_ref, wu_ref, wd_ref, zc_ref, zs_ref,
                xbuf, ybuf, xb_ref, gate_ref, acc_ref, gsem, ssem, *, capc, caps):
    e = pl.program_id(0)
    j = pl.program_id(1)
    n_e = pl.num_programs(0)
    n_j = FF_STEPS
    slot = e % 2
    other = 1 - slot
    rows = capc + caps
    gc, gs = _per_step(capc), _per_step(caps)
    groups = ((hc_ref, idxc_ref, zc_ref, qc_ref, gc, 0), (hs_ref, idxs_ref, zs_ref, qs_ref, gs, gc * n_j))

    def tile(ref, first_sublane):
        return ref.at[pl.ds(pl.multiple_of(first_sublane, ROW_TILE), ROW_TILE), :]

    def gather(ex, sl, step, i, group):
        h_ref, idx_ref, _, _, per_step, base = group
        p = step * per_step + i
        src = tile(h_ref, idx_ref[ex * (per_step * n_j) + p])
        pltpu.make_async_copy(src, xbuf.at[sl, :, base + p, :], gsem.at[sl]).start()

    def scatter(table_row, sl, step, i, group):
        _, _, z_ref, q_ref, per_step, base = group
        p = step * per_step + i
        dst = tile(z_ref, q_ref[table_row * (per_step * n_j) + p])
        pltpu.make_async_copy(ybuf.at[sl, :, base + p, :], dst, ssem.at[sl]).start()

    def all_steps(fn):
        for group in groups:
            def body(p, carry, group=group):
                fn(p, group)
                return carry
            lax.fori_loop(0, group[4] * n_j, body, 0, unroll=8)

    def wait_all(buf, sem, sl):
        pltpu.make_async_copy(buf.at[sl], buf.at[sl], sem.at[sl]).wait()

    @pl.when((e == 0) & (j == 0))
    def _():
        ybuf[...] = jnp.zeros_like(ybuf)
        all_steps(lambda p, group: gather(0, 0, 0, p, group))

    @pl.when(j == 0)
    def _():
        wait_all(xbuf, gsem, slot)
        for base, n, dst in ((0, capc, 0), (gc * n_j, caps, capc)):
            for kc in range(ROW_TILE):
                xb_ref[dst:dst + n, kc * LANES:(kc + 1) * LANES] = xbuf[slot, kc, base:base + n, :].astype(BF16)
        acc_ref[...] = jnp.zeros_like(acc_ref)
        logits = _dot(xb_ref[...], wr_ref[...])
        lane = lax.broadcasted_iota(I32, (1, LANES), 1)
        is_expert = lane < N_EXPERTS
        ex = jnp.exp(logits - jnp.max(jnp.where(is_expert, logits, -jnp.inf), axis=-1, keepdims=True))
        mine = jnp.sum(jnp.where(lane == e, ex, 0.0), axis=-1, keepdims=True)
        gate = mine / jnp.sum(jnp.where(is_expert, ex, 0.0), axis=-1, keepdims=True)
        gate_ref[...] = jnp.broadcast_to(gate, gate_ref.shape)

    nxt = jnp.minimum(e + 1, n_e - 1)
    for group in groups:
        for i in range(group[4]):
            gather(nxt, other, j, i, group)
            scatter(e, other, j, i, group)

    x = xb_ref[...]
    g = _dot(x, wg_ref[0].astype(BF16))
    u = _dot(x, wu_ref[0].astype(BF16))
    hid = (g * jax.nn.sigmoid(g) * u).astype(BF16)
    acc_ref[...] += _dot(hid, wd_ref[0].astype(BF16))

    @pl.when(j == n_j - 1)
    def _():
        @pl.when(e >= 1)
        def _():
            wait_all(ybuf, ssem, slot)

        gate = gate_ref[...]
        for base, n, src in ((0, capc, 0), (gc * n_j, caps, capc)):
            for kc in range(ROW_TILE):
                y = acc_ref[src:src + n, kc * LANES:(kc + 1) * LANES] * gate[src:src + n]
                ybuf[slot, kc, base:base + n, :] = y

        @pl.when(e == n_e - 1)
        def _():
            all_steps(lambda p, group: scatter(e + 1, slot, 0, p, group))
            wait_all(ybuf, ssem, other)
            wait_all(ybuf, ssem, slot)
            wait_all(xbuf, gsem, other)


def _per_step(cap):
    return -(-cap // FF_STEPS)


def _copy_tables(idx, qslot, n_rows):
    n_e, cap = idx.shape
    padded = _per_step(cap) * FF_STEPS
    n_pad = padded - cap
    idx_p = jnp.concatenate([idx, jnp.zeros((n_e, n_pad), I32)], axis=1)
    spare = n_rows + jnp.arange(padded + n_e * n_pad, dtype=I32)
    lead = spare[:padded][None, :]
    pad_rows = spare[padded:].reshape(n_e, n_pad)
    q_p = jnp.concatenate([lead, jnp.concatenate([qslot // ROW_TILE, pad_rows], axis=1)], axis=0) * ROW_TILE
    return idx_p.reshape(-1), q_p.reshape(-1), n_rows + padded + n_e * n_pad


def _moe(idxc, idxs, qc, qs, hc, hs, wrx, w_gate, w_up, w_down):
    capc, caps = idxc.shape[1], idxs.shape[1]
    rows = capc + caps
    tf = FF_TILE
    idxc, qc, zc_rows = _copy_tables(idxc, qc, N_EXPERTS * capc)
    idxs, qs, zs_rows = _copy_tables(idxs, qs, N_EXPERTS * caps)
    buf_rows = (_per_step(capc) + _per_step(caps)) * FF_STEPS
    any_spec = pl.BlockSpec(memory_space=pl.ANY)
    grid_spec = pltpu.PrefetchScalarGridSpec(
        num_scalar_prefetch=4,
        grid=(N_EXPERTS, FF_STEPS),
        in_specs=[any_spec, any_spec,
                  pl.BlockSpec((D_MODEL, LANES), lambda e, j, *_: (0, 0)),
                  pl.BlockSpec((1, D_MODEL, tf), lambda e, j, *_: (e, 0, j)),
                  pl.BlockSpec((1, D_MODEL, tf), lambda e, j, *_: (e, 0, j)),
                  pl.BlockSpec((1, tf, D_MODEL), lambda e, j, *_: (e, j, 0))],
        out_specs=[any_spec, any_spec],
        scratch_shapes=[pltpu.VMEM((2, ROW_TILE, buf_rows, LANES), F32),
                        pltpu.VMEM((2, ROW_TILE, buf_rows, LANES), F32),
                        pltpu.VMEM((rows, D_MODEL), BF16),
                        pltpu.VMEM((rows, LANES), F32),
                        pltpu.VMEM((rows, D_MODEL), F32),
                        pltpu.SemaphoreType.DMA((2,)),
                        pltpu.SemaphoreType.DMA((2,))])
    return pl.pallas_call(
        functools.partial(_moe_kernel, capc=capc, caps=caps),
        grid_spec=grid_spec,
        out_shape=[jax.ShapeDtypeStruct((zc_rows * ROW_TILE, LANES), F32),
                   jax.ShapeDtypeStruct((zs_rows * ROW_TILE, LANES), F32)],
        compiler_params=_cparams(("arbitrary", "arbitrary")),
        name="expert_ffn",
    )(idxc, idxs, qc, qs, hc, hs, wrx, w_gate, w_up, w_down)


Z_BUFFERS = 3


def _combine_kernel(clo_ref, chi_ref, z_ref, span_ref, x1_ref, mod_ref, g_ref, o_ref,
                    zbuf, acc_ref, sems, state, *, n_chunks):
    b = pl.program_id(0)
    chunk_rows = ROW_CHUNK * ROW_TILE

    @pl.when(b == 0)
    def _():
        state[0] = 0
        state[1] = 0

    def chunk_copy(c):
        src = z_ref.at[pl.ds(pl.multiple_of(c * chunk_rows, chunk_rows), chunk_rows), :]
        return pltpu.make_async_copy(src, zbuf.at[c % Z_BUFFERS], sems.at[c % Z_BUFFERS])

    acc_ref[...] = jnp.zeros_like(acc_ref)
    first_row = span_ref[:, 0:1] * LANES + span_ref[:, 1:2]
    end_row = span_ref[:, 2:3] * LANES + span_ref[:, 3:4]
    lane_row = lax.broadcasted_iota(I32, (TOK_BLOCK, ROW_CHUNK), 1)

    def body(c, carry):
        for _ in range(Z_BUFFERS):
            nxt = state[0]

            @pl.when(nxt <= jnp.minimum(c + Z_BUFFERS - 1, n_chunks - 1))
            def _():
                chunk_copy(nxt).start()
                state[0] = nxt + 1

        @pl.when(state[1] <= c)
        def _():
            chunk_copy(c).wait()
            state[1] = c + 1

        slot = c % Z_BUFFERS
        row = (c * ROW_CHUNK + lane_row).astype(F32)
        onehot = jnp.where((row >= first_row) & (row < end_row), 1.0, 0.0).astype(BF16)
        y = jnp.concatenate([zbuf[slot, pl.ds(kc, ROW_CHUNK, stride=ROW_TILE), :] for kc in range(ROW_TILE)],
                            axis=1)
        acc_ref[...] += _dot(onehot, y.astype(BF16))
        return carry

    lax.fori_loop(clo_ref[b], chi_ref[b], body, 0)
    gt2 = mod_ref[0][:, 5 * D_MODEL:6 * D_MODEL]
    o_ref[...] = x1_ref[...] + gt2 * _rms(acc_ref[...], g_ref[...], EPS)


def _combine(ranges, z, span, x1, mod3, mod_row, g_post_ffn):
    clo, chi = ranges
    t = x1.shape[0]
    tb = TOK_BLOCK
    grid_spec = pltpu.PrefetchScalarGridSpec(
        num_scalar_prefetch=2,
        grid=(t // tb,),
        in_specs=[pl.BlockSpec(memory_space=pl.ANY),
                  pl.BlockSpec((tb, LANES), lambda b, *_: (b, 0)),
                  pl.BlockSpec((tb, D_MODEL), lambda b, *_: (b, 0)),
                  pl.BlockSpec((1, 1, N_MOD * D_MODEL), lambda b, *_: (mod_row(b, tb), 0, 0)),
                  pl.BlockSpec((1, D_MODEL), lambda b, *_: (0, 0))],
        out_specs=pl.BlockSpec((tb, D_MODEL), lambda b, *_: (b, 0)),
        scratch_shapes=[pltpu.VMEM((Z_BUFFERS, ROW_CHUNK * ROW_TILE, LANES), F32),
                        pltpu.VMEM((tb, D_MODEL), F32),
                        pltpu.SemaphoreType.DMA((Z_BUFFERS,)),
                        pltpu.SMEM((2,), I32)])
    return pl.pallas_call(
        functools.partial(_combine_kernel, n_chunks=2 * t // ROW_CHUNK),
        grid_spec=grid_spec,
        out_shape=jax.ShapeDtypeStruct((t, D_MODEL), F32),
        compiler_params=_cparams(("arbitrary",)),
        name="combine",
    )(clo, chi, z, span, x1, mod3, g_post_ffn)


def _rope_tables(seq):
    half = HEAD_DIM // 4
    freqs = ROPE_THETA ** (-np.arange(half, dtype=np.float64) / half)
    s = np.arange(seq)
    row = (s // GRID_W)[:, None] * freqs[None, :]
    col = (s % GRID_W)[:, None] * freqs[None, :]
    ang = np.concatenate([row, row, col, col], axis=1)
    ang = np.tile(ang, (1, QK_W // HEAD_DIM))
    lane = np.arange(QK_W)
    sign = np.where((lane % 32) < 16, -1.0, 1.0)[None, :]
    return (jnp.asarray(np.cos(ang), dtype=F32), jnp.asarray(np.sin(ang) * sign, dtype=F32))


def _combine_ranges(rows, n_tok):
    step = TOK_BLOCK // LANES
    nb = n_tok // TOK_BLOCK
    lo = rows[0:nb * step:step, 0]
    hi = jnp.concatenate([lo[1:], jnp.full((1,), 2 * n_tok, I32)])
    return (lo // ROW_CHUNK).astype(I32), ((hi + ROW_CHUNK - 1) // ROW_CHUNK).astype(I32)


def kernel(x_prompt, x_sample, c, cache_k, cache_v, c_ctx, w_mod, b_mod, g_pre_mix, g_post_mix, g_pre_ffn, g_post_ffn, w_in, lam_q1, lam_k1, lam_q2, lam_k2, g_subln, w_proj_attn, w_proj_fourier, w_out, w_router, w_gate, w_up, w_down):
    assert w_mod.shape[0] == 1
    lam_init = 0.8 - 0.6 * math.exp(-0.3 * 0)
    bp, sp, _ = x_prompt.shape
    bs, ss, _ = x_sample.shape

    cond8 = jnp.concatenate([c_ctx[None, :], c, jnp.zeros((8 - 1 - bs, D_MODEL), F32)], axis=0)
    mod3 = _modulation(cond8, w_mod[0], b_mod).reshape(8, 1, N_MOD * D_MODEL)

    w_in_b = w_in[0].astype(BF16)
    wpa = w_proj_attn[0].astype(BF16)
    wpf = w_proj_fourier[0].astype(BF16)
    wout = w_out[0].astype(BF16)
    wr = w_router[0].astype(BF16)
    wrx = jnp.concatenate([wr, jnp.zeros((D_MODEL, LANES - N_EXPERTS), BF16)], axis=1)
    lam_p = jnp.concatenate([lam_q1, lam_k1, lam_q2, lam_k2], axis=0)

    groups = []
    for x, seq, positional, ctx in ((x_prompt, sp, False, None),
                                    (x_sample, ss, True, (cache_k, cache_v))):
        nb = x.shape[0]
        t = nb * seq
        x2d = x.reshape(t, D_MODEL)
        if positional:
            mod_row = lambda i, tm, seq=seq: 1 + (i * tm) // seq
        else:
            mod_row = lambda i, tm: 0
        self_contained = ctx is None and PRE_BLOCK % seq == 0
        pre = _pre_mixer(x2d, mod3, mod_row, g_pre_mix, w_in_b, _rope_tables(seq) if positional else None, seq,
                         self_contained, (lam_p, g_subln, lam_init), tm=PRE_BLOCK if self_contained else ROW_BLOCK)
        if self_contained:
            o, fm, ga, gf = pre[:4]
        else:
            q, k, v, f, ga, gf = pre
            o = _attention(lam_p, g_subln, q, k, v, ctx, seq, lam_init)
            fm = _fourier(f, seq)
        x1, h2t, aff_t = _post_mixer(o, fm, ga, gf, x2d, mod3, mod_row, g_post_mix, g_pre_ffn,
                                     wpa, wpf, wout, wrx)
        cap = 2 * t // N_EXPERTS
        assert t <= 1 << TOKEN_BITS
        posm, pack, span, ranges, rows = _route(aff_t, cap)
        idx, qslot = _slot_lists(ranges, posm, pack, cap)
        groups.append(dict(x1=x1, h2t=h2t, idx=idx, qslot=qslot, span=span, ranges=_combine_ranges(rows, t),
                           mod_row=mod_row, cache=pre[4:] if self_contained else None, shape=x.shape))

    gc, gs_ = groups
    zc, zs = _moe(gc["idx"], gs_["idx"], gc["qslot"], gs_["qslot"], gc["h2t"], gs_["h2t"], wrx,
                  w_gate[0], w_up[0], w_down[0])
    outs = []
    for g, z in ((gc, zc), (gs_, zs)):
        out = _combine(g["ranges"], z, g["span"], g["x1"], mod3, g["mod_row"], g_post_ffn)
        outs.append(out.reshape(g["shape"]))
    new_k, new_v = gc["cache"]
    return (outs[0], outs[1], new_k, new_v)
```

```python
import functools
import math

import numpy as np
import jax
import jax.numpy as jnp
from jax import lax
from jax.experimental import pallas as pl
from jax.experimental.pallas import tpu as pltpu

F32 = jnp.float32
BF16 = jnp.bfloat16
I32 = jnp.int32

D_MODEL = 1024
N_HEADS = 6
HEAD_DIM = 64
V_DIM = 128
QK_W = 768
FOUR_W = 256
FOUR_G = 64
IN_W = 4608
N_EXPERTS = 16
D_FF = 2816
N_MOD = 6
GRID_W = 64
ROPE_THETA = 10000.0
EPS = 1e-6
SUBLN_EPS = 1e-5

LANES = 128
ROW_BLOCK = 256
PRE_BLOCK = 512
GATE_CHUNK = 256
POST_BLOCK = 1024
POST_SUB = 256
TOK_BLOCK = 256
ROW_CHUNK = 256
FF_TILE = 256
FF_STEPS = D_FF // FF_TILE
ROW_TILE = D_MODEL // LANES
TOKEN_BITS = 13
VMEM_LIMIT = 56 * 1024 * 1024


def _cparams(sem):
    return pltpu.CompilerParams(dimension_semantics=sem, vmem_limit_bytes=VMEM_LIMIT)


def _dot(a, b):
    return jnp.dot(a, b, preferred_element_type=F32)


def _rms(x, g, eps):
    return x * lax.rsqrt(jnp.mean(x * x, axis=-1, keepdims=True) + eps) * g


def _whole_vmem():
    return pl.BlockSpec(memory_space=pltpu.MemorySpace.VMEM)


def _mod_kernel(c_ref, w_ref, b_ref, o_ref):
    c = c_ref[...]
    s = c * jax.nn.sigmoid(c)
    o_ref[...] = _dot(s.astype(BF16), w_ref[...].astype(BF16)) + b_ref[...]


def _modulation(cond8, w_mod, b_mod):
    tn = 1024
    n = N_MOD * D_MODEL
    return pl.pallas_call(
        _mod_kernel,
        grid=(n // tn,),
        in_specs=[pl.BlockSpec((8, D_MODEL), lambda j: (0, 0)),
                  pl.BlockSpec((D_MODEL, tn), lambda j: (0, j)),
                  pl.BlockSpec((1, tn), lambda j: (0, j))],
        out_specs=pl.BlockSpec((8, tn), lambda j: (0, j)),
        out_shape=jax.ShapeDtypeStruct((8, n), F32),
        compiler_params=_cparams(("arbitrary",)),
        name="modulation",
    )(cond8, w_mod, b_mod)


def _diff_lambda(lp, lam_init):
    s1 = jnp.sum(lp[0:1] * lp[1:2], axis=-1, keepdims=True)
    s2 = jnp.sum(lp[2:3] * lp[3:4], axis=-1, keepdims=True)
    return jnp.exp(s1) - jnp.exp(s2) + lam_init


def _attention_weights(q, k):
    comp1 = lax.broadcasted_iota(I32, (1, V_DIM), 1) < HEAD_DIM
    qs = q * jnp.asarray(HEAD_DIM ** -0.5, BF16)
    zero = jnp.zeros_like(qs)

    def weights(qc):
        s = lax.dot_general(qc, k, (((1,), (1,)), ((), ())), preferred_element_type=F32)
        return jnp.exp(s - jnp.max(s, axis=-1, keepdims=True)).astype(BF16)

    return weights(jnp.where(comp1, qs, zero)), weights(jnp.where(comp1, zero, qs))


def _attention_output(weights, v, lam, g_subln, lam_init):
    v_ones = jnp.concatenate([v, jnp.ones_like(v)], axis=1)

    def attend(ex):
        ov = _dot(ex, v_ones)
        return ov[:, 0:V_DIM] / ov[:, V_DIM:2 * V_DIM]

    o = attend(weights[0]) - lam * attend(weights[1])
    return (_rms(o, g_subln, SUBLN_EPS) * (1.0 - lam_init)).astype(BF16)


def _rope(z, cos, sin_signed, first_half):
    fwd = pltpu.roll(z, QK_W - 16, axis=1)
    bwd = pltpu.roll(z, 16, axis=1)
    return z * cos + jnp.where(first_half, fwd, bwd) * sin_signed


def _pre_kernel(*refs, positional):
    it = iter(refs)
    x_ref, mod_ref, g_ref, w_ref = next(it), next(it), next(it), next(it)
    if positional:
        cos_ref, sin_ref = next(it), next(it)
    q_ref, k_ref, v_ref, f_ref, ga_ref, gf_ref = (next(it) for _ in range(6))

    m = mod_ref[0]
    sh1 = m[:, 0:D_MODEL]
    sc1 = m[:, D_MODEL:2 * D_MODEL]
    h = _rms(x_ref[...], g_ref[...], EPS) * (1.0 + sc1) + sh1
    hb = h.astype(BF16)

    def proj(lo, hi):
        return _dot(hb, w_ref[:, lo:hi])

    zq = proj(0, QK_W)
    zk = proj(QK_W, 2 * QK_W)
    zv = proj(2 * QK_W, 3 * QK_W)
    if positional:
        lane = lax.broadcasted_iota(I32, (1, QK_W), 1)
        first_half = (lane % 32) < 16
        cos = cos_ref[...]
        sin_signed = sin_ref[...]
        zq = _rope(zq, cos, sin_signed, first_half)
        zk = _rope(zk, cos, sin_signed, first_half)
    q_ref[...] = zq.astype(BF16)
    k_ref[...] = zk.astype(BF16)
    v_ref[...] = zv.astype(BF16)
    f0 = 3 * QK_W
    g0 = f0 + FOUR_W
    f_ref[...] = proj(f0, g0)
    ga_ref[...] = jax.nn.sigmoid(proj(g0, g0 + D_MODEL)).astype(BF16)
    gf_ref[...] = jax.nn.sigmoid(proj(g0 + D_MODEL, IN_W)).astype(BF16)


def _ctx_kernel(x_ref, mod_ref, g_ref, w_ref, bc_ref, bs_ref, cs_ref, ss_ref, lam_ref, gs_ref,
                o_ref, f_ref, ga_ref, gf_ref, kc_ref, vc_ref, *, seq, lam_init):
    m = mod_ref[0]
    sh1 = m[:, 0:D_MODEL]
    sc1 = m[:, D_MODEL:2 * D_MODEL]
    hb = (_rms(x_ref[...], g_ref[...], EPS) * (1.0 + sc1) + sh1).astype(BF16)
    seqs = [slice(b * seq, (b + 1) * seq) for b in range(x_ref.shape[0] // seq)]
    lam = _diff_lambda(lam_ref[...], lam_init)
    f0 = 3 * QK_W
    g0 = f0 + FOUR_W

    def proj(lo, hi):
        return _dot(hb, w_ref[:, lo:hi])

    def qkv(cache_ref, lo):
        heads = []
        for c in range(QK_W // GATE_CHUNK):
            cols = slice(c * GATE_CHUNK, (c + 1) * GATE_CHUNK)
            z = proj(lo + cols.start, lo + cols.stop)
            for hd in range(cols.start // V_DIM, cols.stop // V_DIM):
                z_hd = z[:, hd * V_DIM - cols.start:(hd + 1) * V_DIM - cols.start]
                if cache_ref is not None:
                    for b, rs in enumerate(seqs):
                        cache_ref[b, 0, hd] = z_hd[rs, :]
                heads.append(z_hd.astype(BF16))
        return heads

    def fourier_chunk():
        f = proj(f0, g0)
        for rs in seqs:
            f_ref[rs, :] = _dft_real(f[rs, :], bc_ref, bs_ref, cs_ref, ss_ref)

    def gate_chunk(ref, lo, c):
        cols = slice(c * GATE_CHUNK, (c + 1) * GATE_CHUNK)
        ref[:, cols] = jax.nn.sigmoid(proj(lo + cols.start, lo + cols.stop)).astype(BF16)


    q, k, v = qkv(None, 0), qkv(kc_ref, QK_W), qkv(vc_ref, 2 * QK_W)
    matmul_work = [fourier_chunk] + [functools.partial(gate_chunk, ref, lo, c)
                                     for ref, lo in ((ga_ref, g0), (gf_ref, g0 + D_MODEL))
                                     for c in range(D_MODEL // GATE_CHUNK)]
    for rs in seqs:
        for hd in range(N_HEADS):
            weights = _attention_weights(q[hd][rs, :], k[hd][rs, :])
            if matmul_work:
                matmul_work.pop(0)()
            o_ref[rs, hd * V_DIM:(hd + 1) * V_DIM] = _attention_output(weights, v[hd][rs, :], lam, gs_ref[...],
                                                                      lam_init)
    for work in matmul_work:
        work()


def _ctx_mixer(x2d, mod3, mod_row, g_pre, w_in_b, seq, attn_params, tm):
    t = x2d.shape[0]
    lam_p, g_subln, lam_init = attn_params
    assert tm % seq == 0 and t % tm == 0
    row = lambda i: (i, 0)
    const = lambda i: (0, 0)
    consts = _dft_consts(seq) + (lam_p, g_subln)
    nb = t // seq
    cshape = jax.ShapeDtypeStruct((nb, 1, N_HEADS, seq, V_DIM), F32)
    cspec = pl.BlockSpec((tm // seq, 1, N_HEADS, seq, V_DIM), lambda i: (i, 0, 0, 0, 0))
    return pl.pallas_call(
        functools.partial(_ctx_kernel, seq=seq, lam_init=lam_init),
        grid=(t // tm,),
        in_specs=[pl.BlockSpec((tm, D_MODEL), row),
                  pl.BlockSpec((1, 1, N_MOD * D_MODEL), lambda i: (mod_row(i, tm), 0, 0)),
                  pl.BlockSpec((1, D_MODEL), const),
                  _whole_vmem()] + [pl.BlockSpec(c.shape, const) for c in consts],
        out_specs=[pl.BlockSpec((tm, QK_W), row),
                   pl.BlockSpec((tm, FOUR_W), row),
                   pl.BlockSpec((tm, D_MODEL), row),
                   pl.BlockSpec((tm, D_MODEL), row),
                   cspec, cspec],
        out_shape=[jax.ShapeDtypeStruct((t, QK_W), BF16),
                   jax.ShapeDtypeStruct((t, FOUR_W), BF16),
                   jax.ShapeDtypeStruct((t, D_MODEL), BF16),
                   jax.ShapeDtypeStruct((t, D_MODEL), BF16),
                   cshape, cshape],
        compiler_params=_cparams(("arbitrary",)),
        name="ctx_mixer",
    )(x2d, mod3, g_pre, w_in_b, *consts)


def _pre_mixer(x2d, mod3, mod_row, g_pre, w_in_b, rope_tabs, seq, tm):
    t = x2d.shape[0]
    positional = rope_tabs is not None
    assert seq % tm == 0
    blocks_per_seq = seq // tm
    row = lambda i: (i, 0)
    in_specs = [pl.BlockSpec((tm, D_MODEL), row),
                pl.BlockSpec((1, 1, N_MOD * D_MODEL), lambda i: (mod_row(i, tm), 0, 0)),
                pl.BlockSpec((1, D_MODEL), lambda i: (0, 0)),
                _whole_vmem()]
    args = [x2d, mod3, g_pre, w_in_b]
    if positional:
        in_specs += [pl.BlockSpec((tm, QK_W), lambda i: (i % blocks_per_seq, 0))] * 2
        args += list(rope_tabs)
    out_shape = [jax.ShapeDtypeStruct((t, QK_W), BF16)] * 3 + [
        jax.ShapeDtypeStruct((t, FOUR_W), F32),
        jax.ShapeDtypeStruct((t, D_MODEL), BF16),
        jax.ShapeDtypeStruct((t, D_MODEL), BF16)]
    out_specs = [pl.BlockSpec((tm, QK_W), row)] * 3 + [
        pl.BlockSpec((tm, FOUR_W), row),
        pl.BlockSpec((tm, D_MODEL), row),
        pl.BlockSpec((tm, D_MODEL), row)]
    return pl.pallas_call(
        functools.partial(_pre_kernel, positional=positional),
        grid=(t // tm,),
        in_specs=in_specs,
        out_specs=out_specs,
        out_shape=out_shape,
        compiler_params=_cparams(("arbitrary",)),
        name="pre_mixer",
    )(*args)


def _attn_kernel(*refs, lam_init, has_ctx):
    it = iter(refs)
    lam_ref, gs_ref, q_ref, k_ref, v_ref = (next(it) for _ in range(5))
    if has_ctx:
        ck_ref, cv_ref = next(it), next(it)
    o_ref = next(it)

    lam = _diff_lambda(lam_ref[...], lam_init)

    def operand(ref, cache_ref, hd):
        x = ref[:, hd * V_DIM:(hd + 1) * V_DIM]
        return jnp.concatenate([cache_ref[0, 0, hd].astype(BF16), x], axis=0) if has_ctx else x

    def head_weights(hd):
        return _attention_weights(q_ref[:, hd * V_DIM:(hd + 1) * V_DIM], operand(k_ref, ck_ref if has_ctx else None, hd))

    weights = head_weights(0)
    for hd in range(N_HEADS):
        nxt = head_weights(hd + 1) if hd + 1 < N_HEADS else None
        v = operand(v_ref, cv_ref if has_ctx else None, hd)
        o_ref[:, hd * V_DIM:(hd + 1) * V_DIM] = _attention_output(weights, v, lam, gs_ref[...], lam_init)
        weights = nxt


def _attention(lam_p, g_subln, q, k, v, ctx, seq, lam_init):
    t = q.shape[0]
    tq = ROW_BLOCK
    qb = seq // tq
    has_ctx = ctx is not None
    in_specs = [pl.BlockSpec((4, HEAD_DIM), lambda b, i: (0, 0)),
                pl.BlockSpec((1, V_DIM), lambda b, i: (0, 0)),
                pl.BlockSpec((tq, QK_W), lambda b, i: (b * qb + i, 0)),
                pl.BlockSpec((seq, QK_W), lambda b, i: (b, 0)),
                pl.BlockSpec((seq, QK_W), lambda b, i: (b, 0))]
    args = [lam_p, g_subln, q, k, v]
    if has_ctx:
        past = ctx[0].shape[3]
        cspec = pl.BlockSpec((1, 1, N_HEADS, past, V_DIM), lambda b, i: (b, 0, 0, 0, 0))
        in_specs += [cspec, cspec]
        args += list(ctx)
    return pl.pallas_call(
        functools.partial(_attn_kernel, lam_init=lam_init, has_ctx=has_ctx),
        grid=(t // seq, qb),
        in_specs=in_specs,
        out_specs=pl.BlockSpec((tq, QK_W), lambda b, i: (b * qb + i, 0)),
        out_shape=jax.ShapeDtypeStruct((t, QK_W), BF16),
        compiler_params=_cparams(("arbitrary", "arbitrary")),
        name="diff_attention",
    )(*args)


def _dft_real(f, bc_ref, bs_ref, cs_ref, ss_ref):
    fb = f.astype(BF16)
    u = _dot(fb, bc_ref[...].astype(BF16)).astype(BF16)
    w = _dot(fb, bs_ref[...].astype(BF16)).astype(BF16)
    return (_dot(cs_ref[...].astype(BF16), u) - _dot(ss_ref[...].astype(BF16), w)).astype(BF16)


def _fourier_kernel(f_ref, bc_ref, bs_ref, cs_ref, ss_ref, o_ref):
    o_ref[...] = _dft_real(f_ref[...], bc_ref, bs_ref, cs_ref, ss_ref)


def _dft_consts(seq):
    c = np.arange(FOUR_G)
    ang_c = 2.0 * np.pi * ((c[:, None] * c[None, :]) % FOUR_G) / FOUR_G
    eye = np.eye(FOUR_W // FOUR_G)
    bc = np.kron(eye, np.cos(ang_c)) / math.sqrt(FOUR_G)
    bs = np.kron(eye, np.sin(ang_c)) / math.sqrt(FOUR_G)
    s = np.arange(seq)
    ang_s = 2.0 * np.pi * ((s[:, None] * s[None, :]) % seq) / seq
    cs = np.cos(ang_s) / math.sqrt(seq)
    ss = np.sin(ang_s) / math.sqrt(seq)
    return tuple(jnp.asarray(a, dtype=F32) for a in (bc, bs, cs, ss))


def _fourier(f, seq):
    t = f.shape[0]
    bc, bs, cs, ss = _dft_consts(seq)
    const = lambda b: (0, 0)
    return pl.pallas_call(
        _fourier_kernel,
        grid=(t // seq,),
        in_specs=[pl.BlockSpec((seq, FOUR_W), lambda b: (b, 0)),
                  pl.BlockSpec((FOUR_W, FOUR_W), const),
                  pl.BlockSpec((FOUR_W, FOUR_W), const),
                  pl.BlockSpec((seq, seq), const),
                  pl.BlockSpec((seq, seq), const)],
        out_specs=pl.BlockSpec((seq, FOUR_W), lambda b: (b, 0)),
        out_shape=jax.ShapeDtypeStruct((t, FOUR_W), BF16),
        compiler_params=_cparams(("arbitrary",)),
        name="fourier_mix",
    )(f, bc, bs, cs, ss)


def _post_kernel(o_ref, fm_ref, ga_ref, gf_ref, x_ref, mod_ref, gpost_ref, gffn_ref,
                 wpa_ref, wpf_ref, wout_ref, wrx_ref,
                 x1_ref, h2t_ref, afft_ref):
    m = mod_ref[0]
    gt1 = m[:, 2 * D_MODEL:3 * D_MODEL]
    sh2 = m[:, 3 * D_MODEL:4 * D_MODEL]
    sc2 = m[:, 4 * D_MODEL:5 * D_MODEL]
    subs = [slice(r0, r0 + POST_SUB) for r0 in range(0, o_ref.shape[0], POST_SUB)]
    ab = [(_dot(o_ref[rs, :], wpa_ref[...]), _dot(fm_ref[rs, :], wpf_ref[...])) for rs in subs]
    merged = [(ga_ref[rs, :] * a + gf_ref[rs, :] * b).astype(BF16) for rs, (a, b) in zip(subs, ab)]
    ys = [_dot(mg, wout_ref[...]) for mg in merged]
    h2s = []
    for rs, y in zip(subs, ys):
        x1 = x_ref[rs, :] + gt1 * _rms(y, gpost_ref[...], EPS)
        x1_ref[rs, :] = x1
        h2s.append(_rms(x1, gffn_ref[...], EPS) * (1.0 + sc2) + sh2)
    logits = [_dot(h2.astype(BF16), wrx_ref[...]) for h2 in h2s]
    for rs, h2, lg in zip(subs, h2s, logits):
        lt = lg.T[0:N_EXPERTS]
        et = jnp.exp(lt - jnp.max(lt, axis=0, keepdims=True))
        aff = et / jnp.sum(et, axis=0, keepdims=True)
        for u in range(POST_SUB // LANES):
            afft_ref[:, rs.start // LANES + u, :] = aff[:, u * LANES:(u + 1) * LANES]
        for kc in range(ROW_TILE):
            h2t_ref[pl.ds(rs.start * ROW_TILE + kc, POST_SUB, stride=ROW_TILE), :] = h2[:, kc * LANES:(kc + 1) * LANES]


def _post_mixer(o, fm, ga, gf, x2d, mod3, mod_row, g_post, g_ffn, wpa, wpf, wout, wrx):
    t = x2d.shape[0]
    tm = POST_BLOCK
    row = lambda i: (i, 0)
    const = lambda i: (0, 0)
    return pl.pallas_call(
        _post_kernel,
        grid=(t // tm,),
        in_specs=[pl.BlockSpec((tm, QK_W), row),
                  pl.BlockSpec((tm, FOUR_W), row),
                  pl.BlockSpec((tm, D_MODEL), row),
                  pl.BlockSpec((tm, D_MODEL), row),
                  pl.BlockSpec((tm, D_MODEL), row),
                  pl.BlockSpec((1, 1, N_MOD * D_MODEL), lambda i: (mod_row(i, tm), 0, 0)),
                  pl.BlockSpec((1, D_MODEL), const),
                  pl.BlockSpec((1, D_MODEL), const),
                  pl.BlockSpec((QK_W, D_MODEL), const),
                  pl.BlockSpec((FOUR_W, D_MODEL), const),
                  pl.BlockSpec((D_MODEL, D_MODEL), const),
                  pl.BlockSpec((D_MODEL, LANES), const)],
        out_specs=[pl.BlockSpec((tm, D_MODEL), row),
                   pl.BlockSpec((tm * ROW_TILE, LANES), row),
                   pl.BlockSpec((N_EXPERTS, tm // LANES, LANES), lambda i: (0, i, 0))],
        out_shape=[jax.ShapeDtypeStruct((t, D_MODEL), F32),
                   jax.ShapeDtypeStruct((t * ROW_TILE, LANES), F32),
                   jax.ShapeDtypeStruct((N_EXPERTS, t // LANES, LANES), F32)],
        compiler_params=_cparams(("arbitrary",)),
        name="post_mixer",
    )(o, fm, ga, gf, x2d, mod3, g_post, g_ffn, wpa, wpf, wout, wrx)


RANGE_ROWS = 8


def _route_kernel(aff_ref, posm_ref, pack_ref, span_ref, ranges_ref, rows_ref, *, cap, n_tok):
    aff = aff_ref[...]
    nt = n_tok // LANES
    capf = float(cap)

    def count_ge(v):
        return jnp.sum(jnp.where(aff >= v, 1.0, 0.0), axis=(1, 2), keepdims=True)

    def search(i, thr):
        cand = thr | jnp.left_shift(jnp.int32(1), 30 - i)
        return jnp.where(count_ge(pltpu.bitcast(cand, F32)) >= capf, cand, thr)

    thr = lax.fori_loop(0, 31, search, jnp.zeros((N_EXPERTS, 1, 1), I32))
    lo = pltpu.bitcast(thr, F32)
    hi = pltpu.bitcast(thr + 1, F32)

    def refine(i, c):
        lo, hi = c
        mid = lo + (hi - lo) * 0.5
        ok = count_ge(mid) >= capf
        return jnp.where(ok, mid, lo), jnp.where(ok, hi, mid)

    lo, hi = lax.fori_loop(0, 12, refine, (lo, hi))
    gt = aff >= hi
    eq = (aff >= lo) & (aff < hi)
    n_tie = capf - jnp.sum(jnp.where(gt, 1.0, 0.0), axis=(1, 2), keepdims=True)

    sq0 = lax.broadcasted_iota(I32, (LANES, LANES), 0)
    sq1 = lax.broadcasted_iota(I32, (LANES, LANES), 1)
    along_total = jnp.concatenate([jnp.where(sq0 <= sq1, 1.0, 0.0), jnp.ones((LANES, LANES), F32)],
                                  axis=1).astype(BF16)
    m = N_EXPERTS * nt
    r0 = lax.broadcasted_iota(I32, (m, m), 0)
    r1 = lax.broadcasted_iota(I32, (m, m), 1)
    earlier = jnp.where((r0 // nt == r1 // nt) & (r1 < r0), 1.0, 0.0).astype(BF16)
    lane = lax.broadcasted_iota(I32, (1, LANES), 1)
    token = lax.broadcasted_iota(I32, (nt, LANES), 0) * LANES + lane

    def tile_counts(x):
        both = _dot(x.reshape(m, LANES).astype(BF16), along_total)
        total = both[:, LANES:]
        before = _dot(earlier, total.astype(BF16))
        shape = (N_EXPERTS, nt, LANES)
        return both[:, :LANES].reshape(shape), total.reshape(shape), before.reshape(shape)

    eq_f = jnp.where(eq, 1.0, 0.0)
    eq_along, _, eq_before = tile_counts(eq_f)
    sel = jnp.where(gt, 1.0, jnp.where(eq_along + eq_before <= n_tie, eq_f, 0.0))
    sel_along, sel_total, sel_before = tile_counts(sel)
    posm_ref[...] = jnp.where(sel > 0.5, sel_along + sel_before - sel, -1.0).astype(I32)

    cnt = jnp.sum(sel, axis=0)
    rows_before = jnp.sum(sel_before, axis=0)
    tok_start = _dot(cnt.astype(BF16), along_total[:, :LANES]) - cnt + rows_before
    k = jnp.zeros((nt, LANES), F32)
    slots_before = jnp.zeros((nt, LANES), F32)
    slots_here = jnp.zeros((nt, LANES), F32)
    for e in range(N_EXPERTS):
        pack_ref[e] = (tok_start + k).astype(I32) * (1 << TOKEN_BITS) + token
        k = k + sel[e]
        slots_before = jnp.where(lane == e, sel_before[e], slots_before)
        slots_here = jnp.where(lane == e, sel_total[e], slots_here)

    eye = jnp.where(sq0 == sq1, 1.0, 0.0).astype(BF16)
    tok_end = tok_start + cnt
    parts = []
    for v in (tok_start, tok_end):
        high = jnp.floor(v * (1.0 / LANES))
        parts += [high, v - high * LANES]
    for c in range(nt):
        rows4 = jnp.zeros((LANES, LANES), F32)
        for r, part in enumerate(parts):
            rows4 = jnp.where(sq0 == r, jnp.broadcast_to(part[c:c + 1, :], (LANES, LANES)), rows4)
        span_ref[c * LANES:(c + 1) * LANES, :] = lax.dot_general(
            eye, rows4.astype(BF16), (((1,), (1,)), ((), ())), preferred_element_type=F32)

    ranges_ref[...] = jnp.zeros_like(ranges_ref)
    for s in range(cap // LANES):
        done = jnp.where(slots_before + slots_here <= float(s * LANES), 1.0, 0.0)
        begun = jnp.where(slots_before < float((s + 1) * LANES), 1.0, 0.0)
        ranges_ref[s:s + 1, :] = jnp.sum(done, axis=0, keepdims=True).astype(I32)
        ranges_ref[RANGE_ROWS + s:RANGE_ROWS + s + 1, :] = jnp.sum(begun, axis=0, keepdims=True).astype(I32)
    rows_ref[...] = rows_before.astype(I32)


def _route(aff, cap):
    n_e, nt, _ = aff.shape
    n_tok = nt * LANES
    assert cap // LANES <= RANGE_ROWS
    return pl.pallas_call(
        functools.partial(_route_kernel, cap=cap, n_tok=n_tok),
        out_shape=[jax.ShapeDtypeStruct((n_e, nt, LANES), I32),
                   jax.ShapeDtypeStruct((n_e, nt, LANES), I32),
                   jax.ShapeDtypeStruct((n_tok, LANES), F32),
                   jax.ShapeDtypeStruct((2 * RANGE_ROWS, LANES), I32),
                   jax.ShapeDtypeStruct((nt, LANES), I32)],
        compiler_params=pltpu.CompilerParams(vmem_limit_bytes=VMEM_LIMIT),
        name="route",
    )(aff)


def _slots_kernel(ranges_ref, posm_ref, pack_ref, idx_ref, qslot_ref, acc_ref):
    e = pl.program_id(0)
    sub = lax.broadcasted_iota(I32, (LANES, LANES), 0)
    eye = sub == lax.broadcasted_iota(I32, (LANES, LANES), 1)
    n_tiles = idx_ref.shape[1]

    for s in range(n_tiles):
        slot = sub + s * LANES

        def body(c, acc, slot=slot):
            hit = posm_ref[e, pl.ds(c, 1), :] == slot
            return acc + jnp.where(hit, pack_ref[e, pl.ds(c, 1), :], 0)

        acc_ref[s] = lax.fori_loop(ranges_ref[s, e], ranges_ref[RANGE_ROWS + s, e], body,
                                   jnp.zeros((LANES, LANES), I32))

    def as_row(part):
        col = jnp.sum(part.astype(F32), axis=1, keepdims=True)
        return jnp.sum(jnp.where(eye, col, 0.0), axis=0, keepdims=True).astype(I32)

    for s in range(n_tiles):
        acc = acc_ref[s]
        idx_ref[0, s:s + 1, :] = as_row(acc & ((1 << TOKEN_BITS) - 1)) * ROW_TILE
        qslot_ref[0, s:s + 1, :] = as_row(lax.shift_right_logical(acc, TOKEN_BITS)) * ROW_TILE


def _slot_lists(ranges, posm4, qdst4, cap):
    ns = cap // LANES
    grid_spec = pltpu.PrefetchScalarGridSpec(
        num_scalar_prefetch=1,
        grid=(N_EXPERTS,),
        in_specs=[_whole_vmem(), _whole_vmem()],
        out_specs=[pl.BlockSpec((1, ns, LANES), lambda e, *_: (e, 0, 0)),
                   pl.BlockSpec((1, ns, LANES), lambda e, *_: (e, 0, 0))],
        scratch_shapes=[pltpu.VMEM((ns, LANES, LANES), I32)])
    idx, qslot = pl.pallas_call(
        _slots_kernel,
        grid_spec=grid_spec,
        out_shape=[jax.ShapeDtypeStruct((N_EXPERTS, ns, LANES), I32),
                   jax.ShapeDtypeStruct((N_EXPERTS, ns, LANES), I32)],
        compiler_params=_cparams(("arbitrary",)),
        name="slot_lists",
    )(ranges, posm4, qdst4)
    return idx.reshape(N_EXPERTS, cap), qslot.reshape(N_EXPERTS, cap)


def _moe_kernel(idxc_ref, idxs_ref, qc_ref, qs_ref,
                hc_ref, hs_ref, wr_ref, wg_ref, wu_ref, wd_ref, zc_ref, zs_ref,
                xbuf, ybuf, xb_ref, gate_ref, acc_ref, gsem, ssem, *, capc, caps):
    e = pl.program_id(0)
    j = pl.program_id(1)
    n_e = pl.num_programs(0)
    n_j = FF_STEPS
    slot = e % 2
    other = 1 - slot
    rows = capc + caps
    gc, gs = _per_step(capc), _per_step(caps)
    groups = ((hc_ref, idxc_ref, zc_ref, qc_ref, gc, 0), (hs_ref, idxs_ref, zs_ref, qs_ref, gs, gc * n_j))

    def tile(ref, first_sublane):
        return ref.at[pl.ds(pl.multiple_of(first_sublane, ROW_TILE), ROW_TILE), :]

    def gather(ex, sl, step, i, group):
        h_ref, idx_ref, _, _, per_step, base = group
        p = step * per_step + i
        src = tile(h_ref, idx_ref[ex * (per_step * n_j) + p])
        pltpu.make_async_copy(src, xbuf.at[sl, :, base + p, :], gsem.at[sl]).start()

    def scatter(table_row, sl, step, i, group):
        _, _, z_ref, q_ref, per_step, base = group
        p = step * per_step + i
        dst = tile(z_ref, q_ref[table_row * (per_step * n_j) + p])
        pltpu.make_async_copy(ybuf.at[sl, :, base + p, :], dst, ssem.at[sl]).start()

    def all_steps(fn):
        for group in groups:
            def body(p, carry, group=group):
                fn(p, group)
                return carry
            lax.fori_loop(0, group[4] * n_j, body, 0, unroll=8)

    def wait_all(buf, sem, sl):
        pltpu.make_async_copy(buf.at[sl], buf.at[sl], sem.at[sl]).wait()

    @pl.when((e == 0) & (j == 0))
    def _():
        ybuf[...] = jnp.zeros_like(ybuf)
        all_steps(lambda p, group: gather(0, 0, 0, p, group))

    @pl.when(j == 0)
    def _():
        wait_all(xbuf, gsem, slot)
        for base, n, dst in ((0, capc, 0), (gc * n_j, caps, capc)):
            for kc in range(ROW_TILE):
                xb_ref[dst:dst + n, kc * LANES:(kc + 1) * LANES] = xbuf[slot, kc, base:base + n, :].astype(BF16)
        acc_ref[...] = jnp.zeros_like(acc_ref)
        logits = _dot(xb_ref[...], wr_ref[...])
        lane = lax.broadcasted_iota(I32, (1, LANES), 1)
        is_expert = lane < N_EXPERTS
        ex = jnp.exp(logits - jnp.max(jnp.where(is_expert, logits, -jnp.inf), axis=-1, keepdims=True))
        mine = jnp.sum(jnp.where(lane == e, ex, 0.0), axis=-1, keepdims=True)
        gate = mine / jnp.sum(jnp.where(is_expert, ex, 0.0), axis=-1, keepdims=True)
        gate_ref[...] = jnp.broadcast_to(gate, gate_ref.shape)

    nxt = jnp.minimum(e + 1, n_e - 1)
    for group in groups:
        for i in range(group[4]):
            gather(nxt, other, j, i, group)
            scatter(e, other, j, i, group)

    x = xb_ref[...]
    g = _dot(x, wg_ref[0].astype(BF16))
    u = _dot(x, wu_ref[0].astype(BF16))
    hid = (g * jax.nn.sigmoid(g) * u).astype(BF16)
    acc_ref[...] += _dot(hid, wd_ref[0].astype(BF16))

    @pl.when(j == n_j - 1)
    def _():
        @pl.when(e >= 1)
        def _():
            wait_all(ybuf, ssem, slot)

        gate = gate_ref[...]
        for base, n, src in ((0, capc, 0), (gc * n_j, caps, capc)):
            for kc in range(ROW_TILE):
                y = acc_ref[src:src + n, kc * LANES:(kc + 1) * LANES] * gate[src:src + n]
                ybuf[slot, kc, base:base + n, :] = y

        @pl.when(e == n_e - 1)
        def _():
            all_steps(lambda p, group: scatter(e + 1, slot, 0, p, group))
            wait_all(ybuf, ssem, other)
            wait_all(ybuf, ssem, slot)
            wait_all(xbuf, gsem, other)


def _per_step(cap):
    return -(-cap // FF_STEPS)


def _copy_tables(idx, qslot, n_rows):
    n_e, cap = idx.shape
    padded = _per_step(cap) * FF_STEPS
    n_pad = padded - cap
    idx_p = jnp.concatenate([idx, jnp.zeros((n_e, n_pad), I32)], axis=1)
    spare = n_rows + jnp.arange(padded + n_e * n_pad, dtype=I32)
    lead = spare[:padded][None, :]
    pad_rows = spare[padded:].reshape(n_e, n_pad)
    q_p = jnp.concatenate([lead, jnp.concatenate([qslot // ROW_TILE, pad_rows], axis=1)], axis=0) * ROW_TILE
    return idx_p.reshape(-1), q_p.reshape(-1), n_rows + padded + n_e * n_pad


def _moe(idxc, idxs, qc, qs, hc, hs, wrx, w_gate, w_up, w_down):
    capc, caps = idxc.shape[1], idxs.shape[1]
    rows = capc + caps
    tf = FF_TILE
    idxc, qc, zc_rows = _copy_tables(idxc, qc, N_EXPERTS * capc)
    idxs, qs, zs_rows = _copy_tables(idxs, qs, N_EXPERTS * caps)
    buf_rows = (_per_step(capc) + _per_step(caps)) * FF_STEPS
    any_spec = pl.BlockSpec(memory_space=pl.ANY)
    grid_spec = pltpu.PrefetchScalarGridSpec(
        num_scalar_prefetch=4,
        grid=(N_EXPERTS, FF_STEPS),
        in_specs=[any_spec, any_spec,
                  pl.BlockSpec((D_MODEL, LANES), lambda e, j, *_: (0, 0)),
                  pl.BlockSpec((1, D_MODEL, tf), lambda e, j, *_: (e, 0, j)),
                  pl.BlockSpec((1, D_MODEL, tf), lambda e, j, *_: (e, 0, j)),
                  pl.BlockSpec((1, tf, D_MODEL), lambda e, j, *_: (e, j, 0))],
        out_specs=[any_spec, any_spec],
        scratch_shapes=[pltpu.VMEM((2, ROW_TILE, buf_rows, LANES), F32),
                        pltpu.VMEM((2, ROW_TILE, buf_rows, LANES), F32),
                        pltpu.VMEM((rows, D_MODEL), BF16),
                        pltpu.VMEM((rows, LANES), F32),
                        pltpu.VMEM((rows, D_MODEL), F32),
                        pltpu.SemaphoreType.DMA((2,)),
                        pltpu.SemaphoreType.DMA((2,))])
    return pl.pallas_call(
        functools.partial(_moe_kernel, capc=capc, caps=caps),
        grid_spec=grid_spec,
        out_shape=[jax.ShapeDtypeStruct((zc_rows * ROW_TILE, LANES), F32),
                   jax.ShapeDtypeStruct((zs_rows * ROW_TILE, LANES), F32)],
        compiler_params=_cparams(("arbitrary", "arbitrary")),
        name="expert_ffn",
    )(idxc, idxs, qc, qs, hc, hs, wrx, w_gate, w_up, w_down)


Z_BUFFERS = 3


def _combine_kernel(clo_ref, chi_ref, z_ref, span_ref, x1_ref, mod_ref, g_ref, o_ref,
                    zbuf, acc_ref, sems, state, *, n_chunks):
    b = pl.program_id(0)
    chunk_rows = ROW_CHUNK * ROW_TILE

    @pl.when(b == 0)
    def _():
        state[0] = 0
        state[1] = 0

    def chunk_copy(c):
        src = z_ref.at[pl.ds(pl.multiple_of(c * chunk_rows, chunk_rows), chunk_rows), :]
        return pltpu.make_async_copy(src, zbuf.at[c % Z_BUFFERS], sems.at[c % Z_BUFFERS])

    acc_ref[...] = jnp.zeros_like(acc_ref)
    first_row = span_ref[:, 0:1] * LANES + span_ref[:, 1:2]
    end_row = span_ref[:, 2:3] * LANES + span_ref[:, 3:4]
    lane_row = lax.broadcasted_iota(I32, (TOK_BLOCK, ROW_CHUNK), 1)

    def body(c, carry):
        for _ in range(Z_BUFFERS):
            nxt = state[0]

            @pl.when(nxt <= jnp.minimum(c + Z_BUFFERS - 1, n_chunks - 1))
            def _():
                chunk_copy(nxt).start()
                state[0] = nxt + 1

        @pl.when(state[1] <= c)
        def _():
            chunk_copy(c).wait()
            state[1] = c + 1

        slot = c % Z_BUFFERS
        row = (c * ROW_CHUNK + lane_row).astype(F32)
        onehot = jnp.where((row >= first_row) & (row < end_row), 1.0, 0.0).astype(BF16)
        y = jnp.concatenate([zbuf[slot, pl.ds(kc, ROW_CHUNK, stride=ROW_TILE), :] for kc in range(ROW_TILE)],
                            axis=1)
        acc_ref[...] += _dot(onehot, y.astype(BF16))
        return carry

    lax.fori_loop(clo_ref[b], chi_ref[b], body, 0)
    gt2 = mod_ref[0][:, 5 * D_MODEL:6 * D_MODEL]
    o_ref[...] = x1_ref[...] + gt2 * _rms(acc_ref[...], g_ref[...], EPS)


def _combine(ranges, z, span, x1, mod3, mod_row, g_post_ffn):
    clo, chi = ranges
    t = x1.shape[0]
    tb = TOK_BLOCK
    grid_spec = pltpu.PrefetchScalarGridSpec(
        num_scalar_prefetch=2,
        grid=(t // tb,),
        in_specs=[pl.BlockSpec(memory_space=pl.ANY),
                  pl.BlockSpec((tb, LANES), lambda b, *_: (b, 0)),
                  pl.BlockSpec((tb, D_MODEL), lambda b, *_: (b, 0)),
                  pl.BlockSpec((1, 1, N_MOD * D_MODEL), lambda b, *_: (mod_row(b, tb), 0, 0)),
                  pl.BlockSpec((1, D_MODEL), lambda b, *_: (0, 0))],
        out_specs=pl.BlockSpec((tb, D_MODEL), lambda b, *_: (b, 0)),
        scratch_shapes=[pltpu.VMEM((Z_BUFFERS, ROW_CHUNK * ROW_TILE, LANES), F32),
                        pltpu.VMEM((tb, D_MODEL), F32),
                        pltpu.SemaphoreType.DMA((Z_BUFFERS,)),
                        pltpu.SMEM((2,), I32)])
    return pl.pallas_call(
        functools.partial(_combine_kernel, n_chunks=2 * t // ROW_CHUNK),
        grid_spec=grid_spec,
        out_shape=jax.ShapeDtypeStruct((t, D_MODEL), F32),
        compiler_params=_cparams(("arbitrary",)),
        name="combine",
    )(clo, chi, z, span, x1, mod3, g_post_ffn)


def _rope_tables(seq):
    half = HEAD_DIM // 4
    freqs = ROPE_THETA ** (-np.arange(half, dtype=np.float64) / half)
    s = np.arange(seq)
    row = (s // GRID_W)[:, None] * freqs[None, :]
    col = (s % GRID_W)[:, None] * freqs[None, :]
    ang = np.concatenate([row, row, col, col], axis=1)
    ang = np.tile(ang, (1, QK_W // HEAD_DIM))
    lane = np.arange(QK_W)
    sign = np.where((lane % 32) < 16, -1.0, 1.0)[None, :]
    return (jnp.asarray(np.cos(ang), dtype=F32), jnp.asarray(np.sin(ang) * sign, dtype=F32))


def _combine_ranges(rows, n_tok):
    step = TOK_BLOCK // LANES
    nb = n_tok // TOK_BLOCK
    lo = rows[0:nb * step:step, 0]
    hi = jnp.concatenate([lo[1:], jnp.full((1,), 2 * n_tok, I32)])
    return (lo // ROW_CHUNK).astype(I32), ((hi + ROW_CHUNK - 1) // ROW_CHUNK).astype(I32)


def kernel(x_prompt, x_sample, c, cache_k, cache_v, c_ctx, w_mod, b_mod, g_pre_mix, g_post_mix, g_pre_ffn, g_post_ffn, w_in, lam_q1, lam_k1, lam_q2, lam_k2, g_subln, w_proj_attn, w_proj_fourier, w_out, w_router, w_gate, w_up, w_down):
    assert w_mod.shape[0] == 1
    lam_init = 0.8 - 0.6 * math.exp(-0.3 * 0)
    bp, sp, _ = x_prompt.shape
    bs, ss, _ = x_sample.shape

    cond8 = jnp.concatenate([c_ctx[None, :], c, jnp.zeros((8 - 1 - bs, D_MODEL), F32)], axis=0)
    mod3 = _modulation(cond8, w_mod[0], b_mod).reshape(8, 1, N_MOD * D_MODEL)

    w_in_b = w_in[0].astype(BF16)
    wpa = w_proj_attn[0].astype(BF16)
    wpf = w_proj_fourier[0].astype(BF16)
    wout = w_out[0].astype(BF16)
    wr = w_router[0].astype(BF16)
    wrx = jnp.concatenate([wr, jnp.zeros((D_MODEL, LANES - N_EXPERTS), BF16)], axis=1)
    lam_p = jnp.concatenate([lam_q1, lam_k1, lam_q2, lam_k2], axis=0)

    groups = []
    for x, seq, positional, ctx in ((x_prompt, sp, False, None),
                                    (x_sample, ss, True, (cache_k, cache_v))):
        nb = x.shape[0]
        t = nb * seq
        x2d = x.reshape(t, D_MODEL)
        if positional:
            mod_row = lambda i, tm, seq=seq: 1 + (i * tm) // seq
        else:
            mod_row = lambda i, tm: 0
        self_contained = ctx is None and not positional
        if self_contained:
            assert PRE_BLOCK % seq == 0
            pre = _ctx_mixer(x2d, mod3, mod_row, g_pre_mix, w_in_b, seq, (lam_p, g_subln, lam_init), tm=PRE_BLOCK)
            o, fm, ga, gf = pre[:4]
        else:
            q, k, v, f, ga, gf = _pre_mixer(x2d, mod3, mod_row, g_pre_mix, w_in_b,
                                            _rope_tables(seq) if positional else None, seq, tm=ROW_BLOCK)
            o = _attention(lam_p, g_subln, q, k, v, ctx, seq, lam_init)
            fm = _fourier(f, seq)
        x1, h2t, aff_t = _post_mixer(o, fm, ga, gf, x2d, mod3, mod_row, g_post_mix, g_pre_ffn,
                                     wpa, wpf, wout, wrx)
        cap = 2 * t // N_EXPERTS
        assert t <= 1 << TOKEN_BITS
        posm, pack, span, ranges, rows = _route(aff_t, cap)
        idx, qslot = _slot_lists(ranges, posm, pack, cap)
        groups.append(dict(x1=x1, h2t=h2t, idx=idx, qslot=qslot, span=span, ranges=_combine_ranges(rows, t),
                           mod_row=mod_row, cache=pre[4:] if self_contained else None, shape=x.shape))

    gc, gs_ = groups
    zc, zs = _moe(gc["idx"], gs_["idx"], gc["qslot"], gs_["qslot"], gc["h2t"], gs_["h2t"], wrx,
                  w_gate[0], w_up[0], w_down[0])
    outs = []
    for g, z in ((gc, zc), (gs_, zs)):
        out = _combine(g["ranges"], z, g["span"], g["x1"], mod3, g["mod_row"], g_post_ffn)
        outs.append(out.reshape(g["shape"]))
    new_k, new_v = gc["cache"]
    return (outs[0], outs[1], new_k, new_v)
```

```python
import functools
import math

import numpy as np
import jax
import jax.numpy as jnp
from jax import lax
from jax.experimental import pallas as pl
from jax.experimental.pallas import tpu as pltpu

F32 = jnp.float32
BF16 = jnp.bfloat16
I32 = jnp.int32

D_MODEL = 1024
N_HEADS = 6
HEAD_DIM = 64
V_DIM = 128
QK_W = 768
FOUR_W = 256
FOUR_G = 64
IN_W = 4608
N_EXPERTS = 16
D_FF = 2816
N_MOD = 6
GRID_W = 64
ROPE_THETA = 10000.0
EPS = 1e-6
SUBLN_EPS = 1e-5

LANES = 128
ROW_BLOCK = 256
PRE_BLOCK = 512
GATE_CHUNK = 256
POST_BLOCK = 1024
POST_SUB = 256
WEIGHT_ROWS = 128
TOK_BLOCK = 256
ROW_CHUNK = 256
FF_TILE = 256
FF_STEPS = D_FF // FF_TILE
ROW_TILE = D_MODEL // LANES
TOKEN_BITS = 13
VMEM_LIMIT = 56 * 1024 * 1024


def _cparams(sem):
    return pltpu.CompilerParams(dimension_semantics=sem, vmem_limit_bytes=VMEM_LIMIT)


def _dot(a, b):
    return jnp.dot(a, b, preferred_element_type=F32)


def _rms(x, g, eps):
    return x * lax.rsqrt(jnp.mean(x * x, axis=-1, keepdims=True) + eps) * g


def _whole_vmem():
    return pl.BlockSpec(memory_space=pltpu.MemorySpace.VMEM)


def _cast_weight(w_hbm, wb_ref, stage_ref, sem):
    rows = stage_ref.shape[1]
    chunks = [pltpu.make_async_copy(w_hbm.at[pl.ds(r0, rows), :], stage_ref.at[n % 2], sem.at[n % 2])
              for n, r0 in enumerate(range(0, w_hbm.shape[0], rows))]
    chunks[0].start()
    for n, chunk in enumerate(chunks):
        if n + 1 < len(chunks):
            chunks[n + 1].start()
        chunk.wait()
        wb_ref[n * rows:(n + 1) * rows, :] = stage_ref[n % 2].astype(BF16)


def _mod_kernel(c_ref, w_ref, b_ref, o_ref):
    c = c_ref[...]
    s = c * jax.nn.sigmoid(c)
    o_ref[...] = _dot(s.astype(BF16), w_ref[...].astype(BF16)) + b_ref[...]


def _modulation(cond8, w_mod, b_mod):
    tn = 1024
    n = N_MOD * D_MODEL
    return pl.pallas_call(
        _mod_kernel,
        grid=(n // tn,),
        in_specs=[pl.BlockSpec((8, D_MODEL), lambda j: (0, 0)),
                  pl.BlockSpec((D_MODEL, tn), lambda j: (0, j)),
                  pl.BlockSpec((1, tn), lambda j: (0, j))],
        out_specs=pl.BlockSpec((8, tn), lambda j: (0, j)),
        out_shape=jax.ShapeDtypeStruct((8, n), F32),
        compiler_params=_cparams(("arbitrary",)),
        name="modulation",
    )(cond8, w_mod, b_mod)


def _diff_lambda(lp, lam_init):
    s1 = jnp.sum(lp[0:1] * lp[1:2], axis=-1, keepdims=True)
    s2 = jnp.sum(lp[2:3] * lp[3:4], axis=-1, keepdims=True)
    return jnp.exp(s1) - jnp.exp(s2) + lam_init


def _attention_weights(q, k):
    comp1 = lax.broadcasted_iota(I32, (1, V_DIM), 1) < HEAD_DIM
    qs = q * jnp.asarray(HEAD_DIM ** -0.5, BF16)
    zero = jnp.zeros_like(qs)

    def weights(qc):
        s = lax.dot_general(qc, k, (((1,), (1,)), ((), ())), preferred_element_type=F32)
        return jnp.exp(s - jnp.max(s, axis=-1, keepdims=True)).astype(BF16)

    return weights(jnp.where(comp1, qs, zero)), weights(jnp.where(comp1, zero, qs))


def _attention_output(weights, v, lam, g_subln, lam_init):
    v_ones = jnp.concatenate([v, jnp.ones_like(v)], axis=1)

    def attend(ex):
        ov = _dot(ex, v_ones)
        return ov[:, 0:V_DIM] / ov[:, V_DIM:2 * V_DIM]

    o = attend(weights[0]) - lam * attend(weights[1])
    return (_rms(o, g_subln, SUBLN_EPS) * (1.0 - lam_init)).astype(BF16)


def _rope(z, cos, sin_signed, first_half):
    fwd = pltpu.roll(z, QK_W - 16, axis=1)
    bwd = pltpu.roll(z, 16, axis=1)
    return z * cos + jnp.where(first_half, fwd, bwd) * sin_signed


def _pre_kernel(*refs, positional):
    it = iter(refs)
    x_ref, mod_ref, g_ref, w_ref = next(it), next(it), next(it), next(it)
    if positional:
        cos_ref, sin_ref = next(it), next(it)
    q_ref, k_ref, v_ref, f_ref, ga_ref, gf_ref = (next(it) for _ in range(6))

    m = mod_ref[0]
    sh1 = m[:, 0:D_MODEL]
    sc1 = m[:, D_MODEL:2 * D_MODEL]
    h = _rms(x_ref[...], g_ref[...], EPS) * (1.0 + sc1) + sh1
    hb = h.astype(BF16)

    def proj(lo, hi):
        return _dot(hb, w_ref[:, lo:hi])

    zq = proj(0, QK_W)
    zk = proj(QK_W, 2 * QK_W)
    zv = proj(2 * QK_W, 3 * QK_W)
    if positional:
        lane = lax.broadcasted_iota(I32, (1, QK_W), 1)
        first_half = (lane % 32) < 16
        cos = cos_ref[...]
        sin_signed = sin_ref[...]
        zq = _rope(zq, cos, sin_signed, first_half)
        zk = _rope(zk, cos, sin_signed, first_half)
    q_ref[...] = zq.astype(BF16)
    k_ref[...] = zk.astype(BF16)
    v_ref[...] = zv.astype(BF16)
    f0 = 3 * QK_W
    g0 = f0 + FOUR_W
    f_ref[...] = proj(f0, g0)
    ga_ref[...] = jax.nn.sigmoid(proj(g0, g0 + D_MODEL)).astype(BF16)
    gf_ref[...] = jax.nn.sigmoid(proj(g0 + D_MODEL, IN_W)).astype(BF16)


def _ctx_kernel(x_ref, mod_ref, g_ref, w_hbm, bc_ref, bs_ref, cs_ref, ss_ref, lam_ref, gs_ref,
                o_ref, f_ref, ga_ref, gf_ref, kc_ref, vc_ref, wb_hbm,
                w_ref, stage_ref, sem, *, seq, lam_init):
    weight_out = pltpu.make_async_copy(w_ref, wb_hbm, sem.at[2])

    @pl.when(pl.program_id(0) == 0)
    def _():
        _cast_weight(w_hbm, w_ref, stage_ref, sem)
        weight_out.start()

    @pl.when(pl.program_id(0) == pl.num_programs(0) - 1)
    def _():
        weight_out.wait()

    m = mod_ref[0]
    sh1 = m[:, 0:D_MODEL]
    sc1 = m[:, D_MODEL:2 * D_MODEL]
    hb = (_rms(x_ref[...], g_ref[...], EPS) * (1.0 + sc1) + sh1).astype(BF16)
    seqs = [slice(b * seq, (b + 1) * seq) for b in range(x_ref.shape[0] // seq)]
    lam = _diff_lambda(lam_ref[...], lam_init)
    f0 = 3 * QK_W
    g0 = f0 + FOUR_W

    def proj(lo, hi):
        return _dot(hb, w_ref[:, lo:hi])

    q, k, v = {}, {}, {}

    def qkv_chunk(store, cache_ref, lo, c):
        cols = slice(c * GATE_CHUNK, (c + 1) * GATE_CHUNK)
        z = proj(lo + cols.start, lo + cols.stop)
        for hd in range(cols.start // V_DIM, cols.stop // V_DIM):
            z_hd = z[:, hd * V_DIM - cols.start:(hd + 1) * V_DIM - cols.start]
            if cache_ref is not None:
                for b, rs in enumerate(seqs):
                    cache_ref[b, 0, hd] = z_hd[rs, :]
            store[hd] = z_hd.astype(BF16)

    def fourier_chunk():
        f = proj(f0, g0)
        for rs in seqs:
            f_ref[rs, :] = _dft_real(f[rs, :], bc_ref, bs_ref, cs_ref, ss_ref)

    def gate_chunk(ref, lo, c):
        cols = slice(c * GATE_CHUNK, (c + 1) * GATE_CHUNK)
        ref[:, cols] = jax.nn.sigmoid(proj(lo + cols.start, lo + cols.stop)).astype(BF16)

    matmul_work = [functools.partial(qkv_chunk, store, cache, lo, c)
                   for c in range(QK_W // GATE_CHUNK)
                   for store, cache, lo in ((q, None, 0), (k, kc_ref, QK_W), (v, vc_ref, 2 * QK_W))]
    matmul_work.append(fourier_chunk)
    matmul_work += [functools.partial(gate_chunk, ref, lo, c)
                    for ref, lo in ((ga_ref, g0), (gf_ref, g0 + D_MODEL)) for c in range(D_MODEL // GATE_CHUNK)]
    heads_left = len(seqs) * N_HEADS
    for rs in seqs:
        for hd in range(N_HEADS):
            while hd not in v:
                matmul_work.pop(0)()
            weights = _attention_weights(q[hd][rs, :], k[hd][rs, :])
            if matmul_work:
                matmul_work.pop(0)()
            o_ref[rs, hd * V_DIM:(hd + 1) * V_DIM] = _attention_output(weights, v[hd][rs, :], lam, gs_ref[...],
                                                                      lam_init)
            heads_left -= 1
            if len(matmul_work) > heads_left:
                matmul_work.pop(0)()
    for work in matmul_work:
        work()


def _ctx_mixer(x2d, mod3, mod_row, g_pre, w_in, seq, attn_params, tm):
    t = x2d.shape[0]
    any_spec = pl.BlockSpec(memory_space=pl.ANY)
    lam_p, g_subln, lam_init = attn_params
    assert tm % seq == 0 and t % tm == 0
    row = lambda i: (i, 0)
    const = lambda i: (0, 0)
    consts = _dft_consts(seq) + (lam_p, g_subln)
    nb = t // seq
    cshape = jax.ShapeDtypeStruct((nb, 1, N_HEADS, seq, V_DIM), F32)
    cspec = pl.BlockSpec((tm // seq, 1, N_HEADS, seq, V_DIM), lambda i: (i, 0, 0, 0, 0))
    return pl.pallas_call(
        functools.partial(_ctx_kernel, seq=seq, lam_init=lam_init),
        grid=(t // tm,),
        in_specs=[pl.BlockSpec((tm, D_MODEL), row),
                  pl.BlockSpec((1, 1, N_MOD * D_MODEL), lambda i: (mod_row(i, tm), 0, 0)),
                  pl.BlockSpec((1, D_MODEL), const),
                  any_spec] + [pl.BlockSpec(c.shape, const) for c in consts],
        out_specs=[pl.BlockSpec((tm, QK_W), row),
                   pl.BlockSpec((tm, FOUR_W), row),
                   pl.BlockSpec((tm, D_MODEL), row),
                   pl.BlockSpec((tm, D_MODEL), row),
                   cspec, cspec, any_spec],
        out_shape=[jax.ShapeDtypeStruct((t, QK_W), BF16),
                   jax.ShapeDtypeStruct((t, FOUR_W), BF16),
                   jax.ShapeDtypeStruct((t, D_MODEL), BF16),
                   jax.ShapeDtypeStruct((t, D_MODEL), BF16),
                   cshape, cshape,
                   jax.ShapeDtypeStruct(w_in.shape, BF16)],
        scratch_shapes=[pltpu.VMEM(w_in.shape, BF16),
                        pltpu.VMEM((2, WEIGHT_ROWS, w_in.shape[1]), F32),
                        pltpu.SemaphoreType.DMA((3,))],
        compiler_params=_cparams(("arbitrary",)),
        name="ctx_mixer",
    )(x2d, mod3, g_pre, w_in, *consts)


def _pre_mixer(x2d, mod3, mod_row, g_pre, w_in_b, rope_tabs, seq, tm):
    t = x2d.shape[0]
    positional = rope_tabs is not None
    assert seq % tm == 0
    blocks_per_seq = seq // tm
    row = lambda i: (i, 0)
    in_specs = [pl.BlockSpec((tm, D_MODEL), row),
                pl.BlockSpec((1, 1, N_MOD * D_MODEL), lambda i: (mod_row(i, tm), 0, 0)),
                pl.BlockSpec((1, D_MODEL), lambda i: (0, 0)),
                _whole_vmem()]
    args = [x2d, mod3, g_pre, w_in_b]
    if positional:
        in_specs += [pl.BlockSpec((tm, QK_W), lambda i: (i % blocks_per_seq, 0))] * 2
        args += list(rope_tabs)
    out_shape = [jax.ShapeDtypeStruct((t, QK_W), BF16)] * 3 + [
        jax.ShapeDtypeStruct((t, FOUR_W), F32),
        jax.ShapeDtypeStruct((t, D_MODEL), BF16),
        jax.ShapeDtypeStruct((t, D_MODEL), BF16)]
    out_specs = [pl.BlockSpec((tm, QK_W), row)] * 3 + [
        pl.BlockSpec((tm, FOUR_W), row),
        pl.BlockSpec((tm, D_MODEL), row),
        pl.BlockSpec((tm, D_MODEL), row)]
    return pl.pallas_call(
        functools.partial(_pre_kernel, positional=positional),
        grid=(t // tm,),
        in_specs=in_specs,
        out_specs=out_specs,
        out_shape=out_shape,
        compiler_params=_cparams(("arbitrary",)),
        name="pre_mixer",
    )(*args)


def _attn_kernel(*refs, lam_init, has_ctx):
    it = iter(refs)
    lam_ref, gs_ref, q_ref, k_ref, v_ref = (next(it) for _ in range(5))
    if has_ctx:
        ck_ref, cv_ref = next(it), next(it)
    o_ref = next(it)

    lam = _diff_lambda(lam_ref[...], lam_init)

    def operand(ref, cache_ref, hd):
        x = ref[:, hd * V_DIM:(hd + 1) * V_DIM]
        return jnp.concatenate([cache_ref[0, 0, hd].astype(BF16), x], axis=0) if has_ctx else x

    def head_weights(hd):
        return _attention_weights(q_ref[:, hd * V_DIM:(hd + 1) * V_DIM], operand(k_ref, ck_ref if has_ctx else None, hd))

    weights = head_weights(0)
    for hd in range(N_HEADS):
        nxt = head_weights(hd + 1) if hd + 1 < N_HEADS else None
        v = operand(v_ref, cv_ref if has_ctx else None, hd)
        o_ref[:, hd * V_DIM:(hd + 1) * V_DIM] = _attention_output(weights, v, lam, gs_ref[...], lam_init)
        weights = nxt


def _attention(lam_p, g_subln, q, k, v, ctx, seq, lam_init):
    t = q.shape[0]
    tq = ROW_BLOCK
    qb = seq // tq
    has_ctx = ctx is not None
    in_specs = [pl.BlockSpec((4, HEAD_DIM), lambda b, i: (0, 0)),
                pl.BlockSpec((1, V_DIM), lambda b, i: (0, 0)),
                pl.BlockSpec((tq, QK_W), lambda b, i: (b * qb + i, 0)),
                pl.BlockSpec((seq, QK_W), lambda b, i: (b, 0)),
                pl.BlockSpec((seq, QK_W), lambda b, i: (b, 0))]
    args = [lam_p, g_subln, q, k, v]
    if has_ctx:
        past = ctx[0].shape[3]
        cspec = pl.BlockSpec((1, 1, N_HEADS, past, V_DIM), lambda b, i: (b, 0, 0, 0, 0))
        in_specs += [cspec, cspec]
        args += list(ctx)
    return pl.pallas_call(
        functools.partial(_attn_kernel, lam_init=lam_init, has_ctx=has_ctx),
        grid=(t // seq, qb),
        in_specs=in_specs,
        out_specs=pl.BlockSpec((tq, QK_W), lambda b, i: (b * qb + i, 0)),
        out_shape=jax.ShapeDtypeStruct((t, QK_W), BF16),
        compiler_params=_cparams(("arbitrary", "arbitrary")),
        name="diff_attention",
    )(*args)


def _dft_real(f, bc_ref, bs_ref, cs_ref, ss_ref):
    fb = f.astype(BF16)
    u = _dot(fb, bc_ref[...].astype(BF16)).astype(BF16)
    w = _dot(fb, bs_ref[...].astype(BF16)).astype(BF16)
    return (_dot(cs_ref[...].astype(BF16), u) - _dot(ss_ref[...].astype(BF16), w)).astype(BF16)


def _fourier_kernel(f_ref, bc_ref, bs_ref, cs_ref, ss_ref, o_ref):
    o_ref[...] = _dft_real(f_ref[...], bc_ref, bs_ref, cs_ref, ss_ref)


def _dft_consts(seq):
    c = np.arange(FOUR_G)
    ang_c = 2.0 * np.pi * ((c[:, None] * c[None, :]) % FOUR_G) / FOUR_G
    eye = np.eye(FOUR_W // FOUR_G)
    bc = np.kron(eye, np.cos(ang_c)) / math.sqrt(FOUR_G)
    bs = np.kron(eye, np.sin(ang_c)) / math.sqrt(FOUR_G)
    s = np.arange(seq)
    ang_s = 2.0 * np.pi * ((s[:, None] * s[None, :]) % seq) / seq
    cs = np.cos(ang_s) / math.sqrt(seq)
    ss = np.sin(ang_s) / math.sqrt(seq)
    return tuple(jnp.asarray(a, dtype=F32) for a in (bc, bs, cs, ss))


def _fourier(f, seq):
    t = f.shape[0]
    bc, bs, cs, ss = _dft_consts(seq)
    const = lambda b: (0, 0)
    return pl.pallas_call(
        _fourier_kernel,
        grid=(t // seq,),
        in_specs=[pl.BlockSpec((seq, FOUR_W), lambda b: (b, 0)),
                  pl.BlockSpec((FOUR_W, FOUR_W), const),
                  pl.BlockSpec((FOUR_W, FOUR_W), const),
                  pl.BlockSpec((seq, seq), const),
                  pl.BlockSpec((seq, seq), const)],
        out_specs=pl.BlockSpec((seq, FOUR_W), lambda b: (b, 0)),
        out_shape=jax.ShapeDtypeStruct((t, FOUR_W), BF16),
        compiler_params=_cparams(("arbitrary",)),
        name="fourier_mix",
    )(f, bc, bs, cs, ss)


def _post_kernel(o_ref, fm_ref, ga_ref, gf_ref, x_ref, mod_ref, gpost_ref, gffn_ref,
                 wpa_ref, wpf_ref, wout_ref, wrx_ref,
                 x1_ref, h2t_ref, afft_ref):
    m = mod_ref[0]
    gt1 = m[:, 2 * D_MODEL:3 * D_MODEL]
    sh2 = m[:, 3 * D_MODEL:4 * D_MODEL]
    sc2 = m[:, 4 * D_MODEL:5 * D_MODEL]
    subs = [slice(r0, r0 + POST_SUB) for r0 in range(0, o_ref.shape[0], POST_SUB)]
    ab = [(_dot(o_ref[rs, :], wpa_ref[...]), _dot(fm_ref[rs, :], wpf_ref[...])) for rs in subs]
    merged = [(ga_ref[rs, :] * a + gf_ref[rs, :] * b).astype(BF16) for rs, (a, b) in zip(subs, ab)]
    ys = [_dot(mg, wout_ref[...]) for mg in merged]
    h2s = []
    for rs, y in zip(subs, ys):
        x1 = x_ref[rs, :] + gt1 * _rms(y, gpost_ref[...], EPS)
        x1_ref[rs, :] = x1
        h2s.append(_rms(x1, gffn_ref[...], EPS) * (1.0 + sc2) + sh2)
    logits = [_dot(h2.astype(BF16), wrx_ref[...]) for h2 in h2s]
    for rs, h2, lg in zip(subs, h2s, logits):
        lt = lg.T[0:N_EXPERTS]
        et = jnp.exp(lt - jnp.max(lt, axis=0, keepdims=True))
        aff = et / jnp.sum(et, axis=0, keepdims=True)
        for u in range(POST_SUB // LANES):
            afft_ref[:, rs.start // LANES + u, :] = aff[:, u * LANES:(u + 1) * LANES]
        for kc in range(ROW_TILE):
            h2t_ref[pl.ds(rs.start * ROW_TILE + kc, POST_SUB, stride=ROW_TILE), :] = h2[:, kc * LANES:(kc + 1) * LANES]


def _post_mixer(o, fm, ga, gf, x2d, mod3, mod_row, g_post, g_ffn, wpa, wpf, wout, wrx):
    t = x2d.shape[0]
    tm = POST_BLOCK
    row = lambda i: (i, 0)
    const = lambda i: (0, 0)
    return pl.pallas_call(
        _post_kernel,
        grid=(t // tm,),
        in_specs=[pl.BlockSpec((tm, QK_W), row),
                  pl.BlockSpec((tm, FOUR_W), row),
                  pl.BlockSpec((tm, D_MODEL), row),
                  pl.BlockSpec((tm, D_MODEL), row),
                  pl.BlockSpec((tm, D_MODEL), row),
                  pl.BlockSpec((1, 1, N_MOD * D_MODEL), lambda i: (mod_row(i, tm), 0, 0)),
                  pl.BlockSpec((1, D_MODEL), const),
                  pl.BlockSpec((1, D_MODEL), const),
                  pl.BlockSpec((QK_W, D_MODEL), const),
                  pl.BlockSpec((FOUR_W, D_MODEL), const),
                  pl.BlockSpec((D_MODEL, D_MODEL), const),
                  pl.BlockSpec((D_MODEL, LANES), const)],
        out_specs=[pl.BlockSpec((tm, D_MODEL), row),
                   pl.BlockSpec((tm * ROW_TILE, LANES), row),
                   pl.BlockSpec((N_EXPERTS, tm // LANES, LANES), lambda i: (0, i, 0))],
        out_shape=[jax.ShapeDtypeStruct((t, D_MODEL), F32),
                   jax.ShapeDtypeStruct((t * ROW_TILE, LANES), F32),
                   jax.ShapeDtypeStruct((N_EXPERTS, t // LANES, LANES), F32)],
        compiler_params=_cparams(("arbitrary",)),
        name="post_mixer",
    )(o, fm, ga, gf, x2d, mod3, g_post, g_ffn, wpa, wpf, wout, wrx)


RANGE_ROWS = 8


def _route_kernel(aff_ref, posm_ref, pack_ref, span_ref, ranges_ref, rows_ref, *, cap, n_tok):
    aff = aff_ref[...]
    nt = n_tok // LANES
    capf = float(cap)

    def count_ge(v):
        return jnp.sum(jnp.where(aff >= v, 1.0, 0.0), axis=(1, 2), keepdims=True)

    def search(i, thr):
        cand = thr | jnp.left_shift(jnp.int32(1), 30 - i)
        return jnp.where(count_ge(pltpu.bitcast(cand, F32)) >= capf, cand, thr)

    thr = lax.fori_loop(0, 31, search, jnp.zeros((N_EXPERTS, 1, 1), I32))
    lo = pltpu.bitcast(thr, F32)
    hi = pltpu.bitcast(thr + 1, F32)

    def refine(i, c):
        lo, hi = c
        mid = lo + (hi - lo) * 0.5
        ok = count_ge(mid) >= capf
        return jnp.where(ok, mid, lo), jnp.where(ok, hi, mid)

    lo, hi = lax.fori_loop(0, 12, refine, (lo, hi))
    gt = aff >= hi
    eq = (aff >= lo) & (aff < hi)
    n_tie = capf - jnp.sum(jnp.where(gt, 1.0, 0.0), axis=(1, 2), keepdims=True)

    sq0 = lax.broadcasted_iota(I32, (LANES, LANES), 0)
    sq1 = lax.broadcasted_iota(I32, (LANES, LANES), 1)
    along_total = jnp.concatenate([jnp.where(sq0 <= sq1, 1.0, 0.0), jnp.ones((LANES, LANES), F32)],
                                  axis=1).astype(BF16)
    m = N_EXPERTS * nt
    r0 = lax.broadcasted_iota(I32, (m, m), 0)
    r1 = lax.broadcasted_iota(I32, (m, m), 1)
    earlier = jnp.where((r0 // nt == r1 // nt) & (r1 < r0), 1.0, 0.0).astype(BF16)
    lane = lax.broadcasted_iota(I32, (1, LANES), 1)
    token = lax.broadcasted_iota(I32, (nt, LANES), 0) * LANES + lane

    def tile_counts(x):
        both = _dot(x.reshape(m, LANES).astype(BF16), along_total)
        total = both[:, LANES:]
        before = _dot(earlier, total.astype(BF16))
        shape = (N_EXPERTS, nt, LANES)
        return both[:, :LANES].reshape(shape), total.reshape(shape), before.reshape(shape)

    eq_f = jnp.where(eq, 1.0, 0.0)
    eq_along, _, eq_before = tile_counts(eq_f)
    sel = jnp.where(gt, 1.0, jnp.where(eq_along + eq_before <= n_tie, eq_f, 0.0))
    sel_along, sel_total, sel_before = tile_counts(sel)
    posm_ref[...] = jnp.where(sel > 0.5, sel_along + sel_before - sel, -1.0).astype(I32)

    cnt = jnp.sum(sel, axis=0)
    rows_before = jnp.sum(sel_before, axis=0)
    tok_start = _dot(cnt.astype(BF16), along_total[:, :LANES]) - cnt + rows_before
    k = jnp.zeros((nt, LANES), F32)
    slots_before = jnp.zeros((nt, LANES), F32)
    slots_here = jnp.zeros((nt, LANES), F32)
    for e in range(N_EXPERTS):
        pack_ref[e] = (tok_start + k).astype(I32) * (1 << TOKEN_BITS) + token
        k = k + sel[e]
        slots_before = jnp.where(lane == e, sel_before[e], slots_before)
        slots_here = jnp.where(lane == e, sel_total[e], slots_here)

    eye = jnp.where(sq0 == sq1, 1.0, 0.0).astype(BF16)
    tok_end = tok_start + cnt
    parts = []
    for v in (tok_start, tok_end):
        high = jnp.floor(v * (1.0 / LANES))
        parts += [high, v - high * LANES]
    for c in range(nt):
        rows4 = jnp.zeros((LANES, LANES), F32)
        for r, part in enumerate(parts):
            rows4 = jnp.where(sq0 == r, jnp.broadcast_to(part[c:c + 1, :], (LANES, LANES)), rows4)
        span_ref[c * LANES:(c + 1) * LANES, :] = lax.dot_general(
            eye, rows4.astype(BF16), (((1,), (1,)), ((), ())), preferred_element_type=F32)

    ranges_ref[...] = jnp.zeros_like(ranges_ref)
    for s in range(cap // LANES):
        done = jnp.where(slots_before + slots_here <= float(s * LANES), 1.0, 0.0)
        begun = jnp.where(slots_before < float((s + 1) * LANES), 1.0, 0.0)
        ranges_ref[s:s + 1, :] = jnp.sum(done, axis=0, keepdims=True).astype(I32)
        ranges_ref[RANGE_ROWS + s:RANGE_ROWS + s + 1, :] = jnp.sum(begun, axis=0, keepdims=True).astype(I32)
    rows_ref[...] = rows_before.astype(I32)


def _route(aff, cap):
    n_e, nt, _ = aff.shape
    n_tok = nt * LANES
    assert cap // LANES <= RANGE_ROWS
    return pl.pallas_call(
        functools.partial(_route_kernel, cap=cap, n_tok=n_tok),
        out_shape=[jax.ShapeDtypeStruct((n_e, nt, LANES), I32),
                   jax.ShapeDtypeStruct((n_e, nt, LANES), I32),
                   jax.ShapeDtypeStruct((n_tok, LANES), F32),
                   jax.ShapeDtypeStruct((2 * RANGE_ROWS, LANES), I32),
                   jax.ShapeDtypeStruct((nt, LANES), I32)],
        compiler_params=pltpu.CompilerParams(vmem_limit_bytes=VMEM_LIMIT),
        name="route",
    )(aff)


def _slots_kernel(ranges_ref, posm_ref, pack_ref, idx_ref, qslot_ref, acc_ref):
    e = pl.program_id(0)
    sub = lax.broadcasted_iota(I32, (LANES, LANES), 0)
    eye = sub == lax.broadcasted_iota(I32, (LANES, LANES), 1)
    n_tiles = idx_ref.shape[1]

    for s in range(n_tiles):
        slot = sub + s * LANES

        def body(c, acc, slot=slot):
            hit = posm_ref[e, pl.ds(c, 1), :] == slot
            return acc + jnp.where(hit, pack_ref[e, pl.ds(c, 1), :], 0)

        acc_ref[s] = lax.fori_loop(ranges_ref[s, e], ranges_ref[RANGE_ROWS + s, e], body,
                                   jnp.zeros((LANES, LANES), I32))

    def as_row(part):
        col = jnp.sum(part.astype(F32), axis=1, keepdims=True)
        return jnp.sum(jnp.where(eye, col, 0.0), axis=0, keepdims=True).astype(I32)

    for s in range(n_tiles):
        acc = acc_ref[s]
        idx_ref[0, s:s + 1, :] = as_row(acc & ((1 << TOKEN_BITS) - 1)) * ROW_TILE
        qslot_ref[0, s:s + 1, :] = as_row(lax.shift_right_logical(acc, TOKEN_BITS)) * ROW_TILE


def _slot_lists(ranges, posm4, qdst4, cap):
    ns = cap // LANES
    grid_spec = pltpu.PrefetchScalarGridSpec(
        num_scalar_prefetch=1,
        grid=(N_EXPERTS,),
        in_specs=[_whole_vmem(), _whole_vmem()],
        out_specs=[pl.BlockSpec((1, ns, LANES), lambda e, *_: (e, 0, 0)),
                   pl.BlockSpec((1, ns, LANES), lambda e, *_: (e, 0, 0))],
        scratch_shapes=[pltpu.VMEM((ns, LANES, LANES), I32)])
    idx, qslot = pl.pallas_call(
        _slots_kernel,
        grid_spec=grid_spec,
        out_shape=[jax.ShapeDtypeStruct((N_EXPERTS, ns, LANES), I32),
                   jax.ShapeDtypeStruct((N_EXPERTS, ns, LANES), I32)],
        compiler_params=_cparams(("arbitrary",)),
        name="slot_lists",
    )(ranges, posm4, qdst4)
    return idx.reshape(N_EXPERTS, cap), qslot.reshape(N_EXPERTS, cap)


def _moe_kernel(idxc_ref, idxs_ref, qc_ref, qs_ref,
                hc_ref, hs_ref, wr_ref, wg_ref, wu_ref, wd_ref, zc_ref, zs_ref,
                xbuf, ybuf, xb_ref, gate_ref, acc_ref, gsem, ssem, *, capc, caps):
    e = pl.program_id(0)
    j = pl.program_id(1)
    n_e = pl.num_programs(0)
    n_j = FF_STEPS
    slot = e % 2
    other = 1 - slot
    rows = capc + caps
    gc, gs = _per_step(capc), _per_step(caps)
    groups = ((hc_ref, idxc_ref, zc_ref, qc_ref, gc, 0), (hs_ref, idxs_ref, zs_ref, qs_ref, gs, gc * n_j))

    def tile(ref, first_sublane):
        return ref.at[pl.ds(pl.multiple_of(first_sublane, ROW_TILE), ROW_TILE), :]

    def gather(ex, sl, step, i, group):
        h_ref, idx_ref, _, _, per_step, base = group
        p = step * per_step + i
        src = tile(h_ref, idx_ref[ex * (per_step * n_j) + p])
        pltpu.make_async_copy(src, xbuf.at[sl, :, base + p, :], gsem.at[sl]).start()

    def scatter(table_row, sl, step, i, group):
        _, _, z_ref, q_ref, per_step, base = group
        p = step * per_step + i
        dst = tile(z_ref, q_ref[table_row * (per_step * n_j) + p])
        pltpu.make_async_copy(ybuf.at[sl, :, base + p, :], dst, ssem.at[sl]).start()

    def all_steps(fn):
        for group in groups:
            def body(p, carry, group=group):
                fn(p, group)
                return carry
            lax.fori_loop(0, group[4] * n_j, body, 0, unroll=8)

    def wait_all(buf, sem, sl):
        pltpu.make_async_copy(buf.at[sl], buf.at[sl], sem.at[sl]).wait()

    @pl.when((e == 0) & (j == 0))
    def _():
        ybuf[...] = jnp.zeros_like(ybuf)
        all_steps(lambda p, group: gather(0, 0, 0, p, group))

    @pl.when(j == 0)
    def _():
        wait_all(xbuf, gsem, slot)
        for base, n, dst in ((0, capc, 0), (gc * n_j, caps, capc)):
            for kc in range(ROW_TILE):
                xb_ref[dst:dst + n, kc * LANES:(kc + 1) * LANES] = xbuf[slot, kc, base:base + n, :].astype(BF16)
        acc_ref[...] = jnp.zeros_like(acc_ref)
        logits = _dot(xb_ref[...], wr_ref[...])
        lane = lax.broadcasted_iota(I32, (1, LANES), 1)
        is_expert = lane < N_EXPERTS
        ex = jnp.exp(logits - jnp.max(jnp.where(is_expert, logits, -jnp.inf), axis=-1, keepdims=True))
        mine = jnp.sum(jnp.where(lane == e, ex, 0.0), axis=-1, keepdims=True)
        gate = mine / jnp.sum(jnp.where(is_expert, ex, 0.0), axis=-1, keepdims=True)
        gate_ref[...] = jnp.broadcast_to(gate, gate_ref.shape)

    nxt = jnp.minimum(e + 1, n_e - 1)
    for group in groups:
        for i in range(group[4]):
            gather(nxt, other, j, i, group)
            scatter(e, other, j, i, group)

    x = xb_ref[...]
    g = _dot(x, wg_ref[0].astype(BF16))
    u = _dot(x, wu_ref[0].astype(BF16))
    hid = (g * jax.nn.sigmoid(g) * u).astype(BF16)
    acc_ref[...] += _dot(hid, wd_ref[0].astype(BF16))

    @pl.when(j == n_j - 1)
    def _():
        @pl.when(e >= 1)
        def _():
            wait_all(ybuf, ssem, slot)

        gate = gate_ref[...]
        for base, n, src in ((0, capc, 0), (gc * n_j, caps, capc)):
            for kc in range(ROW_TILE):
                y = acc_ref[src:src + n, kc * LANES:(kc + 1) * LANES] * gate[src:src + n]
                ybuf[slot, kc, base:base + n, :] = y

        @pl.when(e == n_e - 1)
        def _():
            all_steps(lambda p, group: scatter(e + 1, slot, 0, p, group))
            wait_all(ybuf, ssem, other)
            wait_all(ybuf, ssem, slot)
            wait_all(xbuf, gsem, other)


def _per_step(cap):
    return -(-cap // FF_STEPS)


def _copy_tables(idx, qslot, n_rows):
    n_e, cap = idx.shape
    padded = _per_step(cap) * FF_STEPS
    n_pad = padded - cap
    idx_p = jnp.concatenate([idx, jnp.zeros((n_e, n_pad), I32)], axis=1)
    spare = n_rows + jnp.arange(padded + n_e * n_pad, dtype=I32)
    lead = spare[:padded][None, :]
    pad_rows = spare[padded:].reshape(n_e, n_pad)
    q_p = jnp.concatenate([lead, jnp.concatenate([qslot // ROW_TILE, pad_rows], axis=1)], axis=0) * ROW_TILE
    return idx_p.reshape(-1), q_p.reshape(-1), n_rows + padded + n_e * n_pad


def _moe(idxc, idxs, qc, qs, hc, hs, wrx, w_gate, w_up, w_down):
    capc, caps = idxc.shape[1], idxs.shape[1]
    rows = capc + caps
    tf = FF_TILE
    idxc, qc, zc_rows = _copy_tables(idxc, qc, N_EXPERTS * capc)
    idxs, qs, zs_rows = _copy_tables(idxs, qs, N_EXPERTS * caps)
    buf_rows = (_per_step(capc) + _per_step(caps)) * FF_STEPS
    any_spec = pl.BlockSpec(memory_space=pl.ANY)
    grid_spec = pltpu.PrefetchScalarGridSpec(
        num_scalar_prefetch=4,
        grid=(N_EXPERTS, FF_STEPS),
        in_specs=[any_spec, any_spec,
                  pl.BlockSpec((D_MODEL, LANES), lambda e, j, *_: (0, 0)),
                  pl.BlockSpec((1, D_MODEL, tf), lambda e, j, *_: (e, 0, j)),
                  pl.BlockSpec((1, D_MODEL, tf), lambda e, j, *_: (e, 0, j)),
                  pl.BlockSpec((1, tf, D_MODEL), lambda e, j, *_: (e, j, 0))],
        out_specs=[any_spec, any_spec],
        scratch_shapes=[pltpu.VMEM((2, ROW_TILE, buf_rows, LANES), F32),
                        pltpu.VMEM((2, ROW_TILE, buf_rows, LANES), F32),
                        pltpu.VMEM((rows, D_MODEL), BF16),
                        pltpu.VMEM((rows, LANES), F32),
                        pltpu.VMEM((rows, D_MODEL), F32),
                        pltpu.SemaphoreType.DMA((2,)),
                        pltpu.SemaphoreType.DMA((2,))])
    return pl.pallas_call(
        functools.partial(_moe_kernel, capc=capc, caps=caps),
        grid_spec=grid_spec,
        out_shape=[jax.ShapeDtypeStruct((zc_rows * ROW_TILE, LANES), F32),
                   jax.ShapeDtypeStruct((zs_rows * ROW_TILE, LANES), F32)],
        compiler_params=_cparams(("arbitrary", "arbitrary")),
        name="expert_ffn",
    )(idxc, idxs, qc, qs, hc, hs, wrx, w_gate, w_up, w_down)


Z_BUFFERS = 3


def _combine_kernel(clo_ref, chi_ref, z_ref, span_ref, x1_ref, mod_ref, g_ref, o_ref,
                    zbuf, acc_ref, sems, state, *, n_chunks):
    b = pl.program_id(0)
    chunk_rows = ROW_CHUNK * ROW_TILE

    @pl.when(b == 0)
    def _():
        state[0] = 0
        state[1] = 0

    def chunk_copy(c):
        src = z_ref.at[pl.ds(pl.multiple_of(c * chunk_rows, chunk_rows), chunk_rows), :]
        return pltpu.make_async_copy(src, zbuf.at[c % Z_BUFFERS], sems.at[c % Z_BUFFERS])

    acc_ref[...] = jnp.zeros_like(acc_ref)
    first_row = span_ref[:, 0:1] * LANES + span_ref[:, 1:2]
    end_row = span_ref[:, 2:3] * LANES + span_ref[:, 3:4]
    lane_row = lax.broadcasted_iota(I32, (TOK_BLOCK, ROW_CHUNK), 1)

    def body(c, carry):
        for _ in range(Z_BUFFERS):
            nxt = state[0]

            @pl.when(nxt <= jnp.minimum(c + Z_BUFFERS - 1, n_chunks - 1))
            def _():
                chunk_copy(nxt).start()
                state[0] = nxt + 1

        @pl.when(state[1] <= c)
        def _():
            chunk_copy(c).wait()
            state[1] = c + 1

        slot = c % Z_BUFFERS
        row = (c * ROW_CHUNK + lane_row).astype(F32)
        onehot = jnp.where((row >= first_row) & (row < end_row), 1.0, 0.0).astype(BF16)
        y = jnp.concatenate([zbuf[slot, pl.ds(kc, ROW_CHUNK, stride=ROW_TILE), :] for kc in range(ROW_TILE)],
                            axis=1)
        acc_ref[...] += _dot(onehot, y.astype(BF16))
        return carry

    lax.fori_loop(clo_ref[b], chi_ref[b], body, 0)
    gt2 = mod_ref[0][:, 5 * D_MODEL:6 * D_MODEL]
    o_ref[...] = x1_ref[...] + gt2 * _rms(acc_ref[...], g_ref[...], EPS)


def _combine(ranges, z, span, x1, mod3, mod_row, g_post_ffn):
    clo, chi = ranges
    t = x1.shape[0]
    tb = TOK_BLOCK
    grid_spec = pltpu.PrefetchScalarGridSpec(
        num_scalar_prefetch=2,
        grid=(t // tb,),
        in_specs=[pl.BlockSpec(memory_space=pl.ANY),
                  pl.BlockSpec((tb, LANES), lambda b, *_: (b, 0)),
                  pl.BlockSpec((tb, D_MODEL), lambda b, *_: (b, 0)),
                  pl.BlockSpec((1, 1, N_MOD * D_MODEL), lambda b, *_: (mod_row(b, tb), 0, 0)),
                  pl.BlockSpec((1, D_MODEL), lambda b, *_: (0, 0))],
        out_specs=pl.BlockSpec((tb, D_MODEL), lambda b, *_: (b, 0)),
        scratch_shapes=[pltpu.VMEM((Z_BUFFERS, ROW_CHUNK * ROW_TILE, LANES), F32),
                        pltpu.VMEM((tb, D_MODEL), F32),
                        pltpu.SemaphoreType.DMA((Z_BUFFERS,)),
                        pltpu.SMEM((2,), I32)])
    return pl.pallas_call(
        functools.partial(_combine_kernel, n_chunks=2 * t // ROW_CHUNK),
        grid_spec=grid_spec,
        out_shape=jax.ShapeDtypeStruct((t, D_MODEL), F32),
        compiler_params=_cparams(("arbitrary",)),
        name="combine",
    )(clo, chi, z, span, x1, mod3, g_post_ffn)


def _rope_tables(seq):
    half = HEAD_DIM // 4
    freqs = ROPE_THETA ** (-np.arange(half, dtype=np.float64) / half)
    s = np.arange(seq)
    row = (s // GRID_W)[:, None] * freqs[None, :]
    col = (s % GRID_W)[:, None] * freqs[None, :]
    ang = np.concatenate([row, row, col, col], axis=1)
    ang = np.tile(ang, (1, QK_W // HEAD_DIM))
    lane = np.arange(QK_W)
    sign = np.where((lane % 32) < 16, -1.0, 1.0)[None, :]
    return (jnp.asarray(np.cos(ang), dtype=F32), jnp.asarray(np.sin(ang) * sign, dtype=F32))


def _combine_ranges(rows, n_tok):
    step = TOK_BLOCK // LANES
    nb = n_tok // TOK_BLOCK
    lo = rows[0:nb * step:step, 0]
    hi = jnp.concatenate([lo[1:], jnp.full((1,), 2 * n_tok, I32)])
    return (lo // ROW_CHUNK).astype(I32), ((hi + ROW_CHUNK - 1) // ROW_CHUNK).astype(I32)


def kernel(x_prompt, x_sample, c, cache_k, cache_v, c_ctx, w_mod, b_mod, g_pre_mix, g_post_mix, g_pre_ffn, g_post_ffn, w_in, lam_q1, lam_k1, lam_q2, lam_k2, g_subln, w_proj_attn, w_proj_fourier, w_out, w_router, w_gate, w_up, w_down):
    assert w_mod.shape[0] == 1
    lam_init = 0.8 - 0.6 * math.exp(-0.3 * 0)
    bp, sp, _ = x_prompt.shape
    bs, ss, _ = x_sample.shape

    cond8 = jnp.concatenate([c_ctx[None, :], c, jnp.zeros((8 - 1 - bs, D_MODEL), F32)], axis=0)
    mod3 = _modulation(cond8, w_mod[0], b_mod).reshape(8, 1, N_MOD * D_MODEL)

    w_in_b = None
    wpa =w_proj_attn[0].astype(BF16)
    wpf = w_proj_fourier[0].astype(BF16)
    wout = w_out[0].astype(BF16)
    wr = w_router[0].astype(BF16)
    wrx = jnp.concatenate([wr, jnp.zeros((D_MODEL, LANES - N_EXPERTS), BF16)], axis=1)
    lam_p = jnp.concatenate([lam_q1, lam_k1, lam_q2, lam_k2], axis=0)

    groups = []
    for x, seq, positional, ctx in ((x_prompt, sp, False, None),
                                    (x_sample, ss, True, (cache_k, cache_v))):
        nb = x.shape[0]
        t = nb * seq
        x2d = x.reshape(t, D_MODEL)
        if positional:
            mod_row = lambda i, tm, seq=seq: 1 + (i * tm) // seq
        else:
            mod_row = lambda i, tm: 0
        self_contained = ctx is None and not positional
        if self_contained:
            assert PRE_BLOCK % seq == 0
            pre = _ctx_mixer(x2d, mod3, mod_row, g_pre_mix, w_in[0], seq, (lam_p, g_subln, lam_init), tm=PRE_BLOCK)
            o, fm, ga, gf = pre[:4]
            w_in_b = pre[6]
        else:
            q, k, v, f, ga, gf = _pre_mixer(x2d, mod3, mod_row, g_pre_mix,
                                            w_in[0].astype(BF16) if w_in_b is None else w_in_b,
                                            _rope_tables(seq) if positional else None, seq, tm=ROW_BLOCK)
            o = _attention(lam_p, g_subln, q, k, v, ctx, seq, lam_init)
            fm = _fourier(f, seq)
        x1, h2t, aff_t = _post_mixer(o, fm, ga, gf, x2d, mod3, mod_row, g_post_mix, g_pre_ffn,
                                     wpa, wpf, wout, wrx)
        cap = 2 * t // N_EXPERTS
        assert t <= 1 << TOKEN_BITS
        posm, pack, span, ranges, rows = _route(aff_t, cap)
        idx, qslot = _slot_lists(ranges, posm, pack, cap)
        groups.append(dict(x1=x1, h2t=h2t, idx=idx, qslot=qslot, span=span, ranges=_combine_ranges(rows, t),
                           mod_row=mod_row, cache=pre[4:6] if self_contained else None, shape=x.shape))

    gc, gs_ = groups
    zc, zs = _moe(gc["idx"], gs_["idx"], gc["qslot"], gs_["qslot"], gc["h2t"], gs_["h2t"], wrx,
                  w_gate[0], w_up[0], w_down[0])
    outs = []
    for g, z in ((gc, zc), (gs_, zs)):
        out = _combine(g["ranges"], z, g["span"], g["x1"], mod3, g["mod_row"], g_post_ffn)
        outs.append(out.reshape(g["shape"]))
    new_k, new_v = gc["cache"]
    return (outs[0], outs[1], new_k, new_v)
```

```python
import functools
import math

import numpy as np
import jax
import jax.numpy as jnp
from jax import lax
from jax.experimental import pallas as pl
from jax.experimental.pallas import tpu as pltpu

F32 = jnp.float32
BF16 = jnp.bfloat16
I32 = jnp.int32

D_MODEL = 1024
N_HEADS = 6
HEAD_DIM = 64
V_DIM = 128
QK_W = 768
FOUR_W = 256
FOUR_G = 64
IN_W = 4608
N_EXPERTS = 16
D_FF = 2816
N_MOD = 6
GRID_W = 64
ROPE_THETA = 10000.0
EPS = 1e-6
SUBLN_EPS = 1e-5

LANES = 128
ROW_BLOCK = 256
PRE_BLOCK = 512
GATE_CHUNK = 256
POST_BLOCK = 1024
POST_SUB = 256
WEIGHT_ROWS = 128
TOK_BLOCK = 256
ROW_CHUNK = 256
FF_TILE = 256
FF_STEPS = D_FF // FF_TILE
ROW_TILE = D_MODEL // LANES
TOKEN_BITS = 13
VMEM_LIMIT = 56 * 1024 * 1024


def _cparams(sem):
    return pltpu.CompilerParams(dimension_semantics=sem, vmem_limit_bytes=VMEM_LIMIT)


def _dot(a, b):
    return jnp.dot(a, b, preferred_element_type=F32)


def _rms(x, g, eps):
    return x * lax.rsqrt(jnp.mean(x * x, axis=-1, keepdims=True) + eps) * g


def _whole_vmem():
    return pl.BlockSpec(memory_space=pltpu.MemorySpace.VMEM)


def _cast_weight(w_hbm, wb_ref, stage_ref, sem):
    rows = stage_ref.shape[1]
    chunks = [pltpu.make_async_copy(w_hbm.at[pl.ds(r0, rows), :], stage_ref.at[n % 2], sem.at[n % 2])
              for n, r0 in enumerate(range(0, w_hbm.shape[0], rows))]
    chunks[0].start()
    for n, chunk in enumerate(chunks):
        if n + 1 < len(chunks):
            chunks[n + 1].start()
        chunk.wait()
        wb_ref[n * rows:(n + 1) * rows, :] = stage_ref[n % 2].astype(BF16)


def _mod_kernel(c_ref, w_ref, b_ref, o_ref):
    c = c_ref[...]
    s = c * jax.nn.sigmoid(c)
    o_ref[...] = _dot(s.astype(BF16), w_ref[...].astype(BF16)) + b_ref[...]


def _modulation(cond8, w_mod, b_mod):
    tn = 1024
    n = N_MOD * D_MODEL
    return pl.pallas_call(
        _mod_kernel,
        grid=(n // tn,),
        in_specs=[pl.BlockSpec((8, D_MODEL), lambda j: (0, 0)),
                  pl.BlockSpec((D_MODEL, tn), lambda j: (0, j)),
                  pl.BlockSpec((1, tn), lambda j: (0, j))],
        out_specs=pl.BlockSpec((8, tn), lambda j: (0, j)),
        out_shape=jax.ShapeDtypeStruct((8, n), F32),
        compiler_params=_cparams(("arbitrary",)),
        name="modulation",
    )(cond8, w_mod, b_mod)


def _diff_lambda(lp, lam_init):
    s1 = jnp.sum(lp[0:1] * lp[1:2], axis=-1, keepdims=True)
    s2 = jnp.sum(lp[2:3] * lp[3:4], axis=-1, keepdims=True)
    return jnp.exp(s1) - jnp.exp(s2) + lam_init


def _attention_weights(q, k):
    comp1 = lax.broadcasted_iota(I32, (1, V_DIM), 1) < HEAD_DIM
    qs = q * jnp.asarray(HEAD_DIM ** -0.5, BF16)
    zero = jnp.zeros_like(qs)

    def weights(qc):
        s = lax.dot_general(qc, k, (((1,), (1,)), ((), ())), preferred_element_type=F32)
        return jnp.exp(s - jnp.max(s, axis=-1, keepdims=True)).astype(BF16)

    return weights(jnp.where(comp1, qs, zero)), weights(jnp.where(comp1, zero, qs))


def _attention_output(weights, v, lam, g_subln, lam_init):
    v_ones = jnp.concatenate([v, jnp.ones_like(v)], axis=1)

    def attend(ex):
        ov = _dot(ex, v_ones)
        return ov[:, 0:V_DIM] / ov[:, V_DIM:2 * V_DIM]

    o = attend(weights[0]) - lam * attend(weights[1])
    return (_rms(o, g_subln, SUBLN_EPS) * (1.0 - lam_init)).astype(BF16)


def _rope(z, cos, sin_signed, first_half):
    fwd = pltpu.roll(z, QK_W - 16, axis=1)
    bwd = pltpu.roll(z, 16, axis=1)
    return z * cos + jnp.where(first_half, fwd, bwd) * sin_signed


def _pre_kernel(*refs, positional):
    it = iter(refs)
    x_ref, mod_ref, g_ref, w_hbm = next(it), next(it), next(it), next(it)
    if positional:
        cos_ref, sin_ref = next(it), next(it)
    q_ref, k_ref, v_ref, f_ref, ga_ref, gf_ref = (next(it) for _ in range(6))
    wb_hbm, w_ref, stage_ref, sem = next(it), next(it), next(it), next(it)
    weight_out = pltpu.make_async_copy(w_ref, wb_hbm, sem.at[2])

    @pl.when(pl.program_id(0) == 0)
    def _():
        _cast_weight(w_hbm, w_ref, stage_ref, sem)
        weight_out.start()

    @pl.when(pl.program_id(0) == pl.num_programs(0) - 1)
    def _():
        weight_out.wait()

    m = mod_ref[0]
    sh1 = m[:, 0:D_MODEL]
    sc1 = m[:, D_MODEL:2 * D_MODEL]
    h = _rms(x_ref[...], g_ref[...], EPS) * (1.0 + sc1) + sh1
    hb = h.astype(BF16)

    def proj(lo, hi):
        return _dot(hb, w_ref[:, lo:hi])

    zq = proj(0, QK_W)
    zk = proj(QK_W, 2 * QK_W)
    zv = proj(2 * QK_W, 3 * QK_W)
    if positional:
        lane = lax.broadcasted_iota(I32, (1, QK_W), 1)
        first_half = (lane % 32) < 16
        cos = cos_ref[...]
        sin_signed = sin_ref[...]
        zq = _rope(zq, cos, sin_signed, first_half)
        zk = _rope(zk, cos, sin_signed, first_half)
    q_ref[...] = zq.astype(BF16)
    k_ref[...] = zk.astype(BF16)
    v_ref[...] = zv.astype(BF16)
    f0 = 3 * QK_W
    g0 = f0 + FOUR_W
    f_ref[...] = proj(f0, g0)
    ga_ref[...] = jax.nn.sigmoid(proj(g0, g0 + D_MODEL)).astype(BF16)
    gf_ref[...] = jax.nn.sigmoid(proj(g0 + D_MODEL, IN_W)).astype(BF16)


def _ctx_kernel(x_ref, mod_ref, g_ref, w_ref, bc_ref, bs_ref, cs_ref, ss_ref, lam_ref, gs_ref,
                o_ref, f_ref, ga_ref, gf_ref, kc_ref, vc_ref, *, seq, lam_init):
    m = mod_ref[0]
    sh1 = m[:, 0:D_MODEL]
    sc1 = m[:, D_MODEL:2 * D_MODEL]
    hb = (_rms(x_ref[...], g_ref[...], EPS) * (1.0 + sc1) + sh1).astype(BF16)
    seqs = [slice(b * seq, (b + 1) * seq) for b in range(x_ref.shape[0] // seq)]
    lam = _diff_lambda(lam_ref[...], lam_init)
    f0 = 3 * QK_W
    g0 = f0 + FOUR_W

    def proj(lo, hi):
        return _dot(hb, w_ref[:, lo:hi])

    q, k, v = {}, {}, {}

    def qkv_chunk(store, cache_ref, lo, c):
        cols = slice(c * GATE_CHUNK, (c + 1) * GATE_CHUNK)
        z = proj(lo + cols.start, lo + cols.stop)
        for hd in range(cols.start // V_DIM, cols.stop // V_DIM):
            z_hd = z[:, hd * V_DIM - cols.start:(hd + 1) * V_DIM - cols.start]
            if cache_ref is not None:
                for b, rs in enumerate(seqs):
                    cache_ref[b, 0, hd] = z_hd[rs, :]
            store[hd] = z_hd.astype(BF16)

    def fourier_chunk():
        f = proj(f0, g0)
        for rs in seqs:
            f_ref[rs, :] = _dft_real(f[rs, :], bc_ref, bs_ref, cs_ref, ss_ref)

    def gate_chunk(ref, lo, c):
        cols = slice(c * GATE_CHUNK, (c + 1) * GATE_CHUNK)
        ref[:, cols] = jax.nn.sigmoid(proj(lo + cols.start, lo + cols.stop)).astype(BF16)

    matmul_work = [functools.partial(qkv_chunk, store, cache, lo, c)
                   for c in range(QK_W // GATE_CHUNK)
                   for store, cache, lo in ((q, None, 0), (k, kc_ref, QK_W), (v, vc_ref, 2 * QK_W))]
    matmul_work.append(fourier_chunk)
    matmul_work += [functools.partial(gate_chunk, ref, lo, c)
                    for ref, lo in ((ga_ref, g0), (gf_ref, g0 + D_MODEL)) for c in range(D_MODEL // GATE_CHUNK)]
    heads_left = len(seqs) * N_HEADS
    for rs in seqs:
        for hd in range(N_HEADS):
            while hd not in v:
                matmul_work.pop(0)()
            weights = _attention_weights(q[hd][rs, :], k[hd][rs, :])
            if matmul_work:
                matmul_work.pop(0)()
            o_ref[rs, hd * V_DIM:(hd + 1) * V_DIM] = _attention_output(weights, v[hd][rs, :], lam, gs_ref[...],
                                                                      lam_init)
            heads_left -= 1
            if len(matmul_work) > heads_left:
                matmul_work.pop(0)()
    for work in matmul_work:
        work()


def _ctx_mixer(x2d, mod3, mod_row, g_pre, w_in_b, seq, attn_params, tm):
    t = x2d.shape[0]
    lam_p, g_subln, lam_init = attn_params
    assert tm % seq == 0 and t % tm == 0
    row = lambda i: (i, 0)
    const = lambda i: (0, 0)
    consts = _dft_consts(seq) + (lam_p, g_subln)
    nb = t // seq
    cshape = jax.ShapeDtypeStruct((nb, 1, N_HEADS, seq, V_DIM), F32)
    cspec = pl.BlockSpec((tm // seq, 1, N_HEADS, seq, V_DIM), lambda i: (i, 0, 0, 0, 0))
    return pl.pallas_call(
        functools.partial(_ctx_kernel, seq=seq, lam_init=lam_init),
        grid=(t // tm,),
        in_specs=[pl.BlockSpec((tm, D_MODEL), row),
                  pl.BlockSpec((1, 1, N_MOD * D_MODEL), lambda i: (mod_row(i, tm), 0, 0)),
                  pl.BlockSpec((1, D_MODEL), const),
                  _whole_vmem()] + [pl.BlockSpec(c.shape, const) for c in consts],
        out_specs=[pl.BlockSpec((tm, QK_W), row),
                   pl.BlockSpec((tm, FOUR_W), row),
                   pl.BlockSpec((tm, D_MODEL), row),
                   pl.BlockSpec((tm, D_MODEL), row),
                   cspec, cspec],
        out_shape=[jax.ShapeDtypeStruct((t, QK_W), BF16),
                   jax.ShapeDtypeStruct((t, FOUR_W), BF16),
                   jax.ShapeDtypeStruct((t, D_MODEL), BF16),
                   jax.ShapeDtypeStruct((t, D_MODEL), BF16),
                   cshape, cshape],
        compiler_params=_cparams(("arbitrary",)),
        name="ctx_mixer",
    )(x2d, mod3, g_pre, w_in_b, *consts)


def _pre_mixer(x2d, mod3, mod_row, g_pre, w_in, rope_tabs, seq, tm):
    t = x2d.shape[0]
    positional = rope_tabs is not None
    assert seq % tm == 0
    blocks_per_seq = seq // tm
    row = lambda i: (i, 0)
    any_spec = pl.BlockSpec(memory_space=pl.ANY)
    in_specs = [pl.BlockSpec((tm, D_MODEL), row),
                pl.BlockSpec((1, 1, N_MOD * D_MODEL), lambda i: (mod_row(i, tm), 0, 0)),
                pl.BlockSpec((1, D_MODEL), lambda i: (0, 0)),
                any_spec]
    args = [x2d, mod3, g_pre, w_in]
    if positional:
        in_specs += [pl.BlockSpec((tm, QK_W), lambda i: (i % blocks_per_seq, 0))] * 2
        args += list(rope_tabs)
    out_shape = [jax.ShapeDtypeStruct((t, QK_W), BF16)] * 3 + [
        jax.ShapeDtypeStruct((t, FOUR_W), F32),
        jax.ShapeDtypeStruct((t, D_MODEL), BF16),
        jax.ShapeDtypeStruct((t, D_MODEL), BF16),
        jax.ShapeDtypeStruct(w_in.shape, BF16)]
    out_specs = [pl.BlockSpec((tm, QK_W), row)] * 3 + [
        pl.BlockSpec((tm, FOUR_W), row),
        pl.BlockSpec((tm, D_MODEL), row),
        pl.BlockSpec((tm, D_MODEL), row),
        any_spec]
    return pl.pallas_call(
        functools.partial(_pre_kernel, positional=positional),
        grid=(t // tm,),
        in_specs=in_specs,
        out_specs=out_specs,
        out_shape=out_shape,
        scratch_shapes=[pltpu.VMEM(w_in.shape, BF16),
                        pltpu.VMEM((2, WEIGHT_ROWS, w_in.shape[1]), F32),
                        pltpu.SemaphoreType.DMA((3,))],
        compiler_params=_cparams(("arbitrary",)),
        name="pre_mixer",
    )(*args)


def _attn_kernel(*refs, lam_init, has_ctx):
    it = iter(refs)
    lam_ref, gs_ref, q_ref, k_ref, v_ref = (next(it) for _ in range(5))
    if has_ctx:
        ck_ref, cv_ref = next(it), next(it)
    o_ref = next(it)

    lam = _diff_lambda(lam_ref[...], lam_init)

    def operand(ref, cache_ref, hd):
        x = ref[:, hd * V_DIM:(hd + 1) * V_DIM]
        return jnp.concatenate([cache_ref[0, 0, hd].astype(BF16), x], axis=0) if has_ctx else x

    def head_weights(hd):
        return _attention_weights(q_ref[:, hd * V_DIM:(hd + 1) * V_DIM], operand(k_ref, ck_ref if has_ctx else None, hd))

    weights = head_weights(0)
    for hd in range(N_HEADS):
        nxt = head_weights(hd + 1) if hd + 1 < N_HEADS else None
        v = operand(v_ref, cv_ref if has_ctx else None, hd)
        o_ref[:, hd * V_DIM:(hd + 1) * V_DIM] = _attention_output(weights, v, lam, gs_ref[...], lam_init)
        weights = nxt


def _attention(lam_p, g_subln, q, k, v, ctx, seq, lam_init):
    t = q.shape[0]
    tq = ROW_BLOCK
    qb = seq // tq
    has_ctx = ctx is not None
    in_specs = [pl.BlockSpec((4, HEAD_DIM), lambda b, i: (0, 0)),
                pl.BlockSpec((1, V_DIM), lambda b, i: (0, 0)),
                pl.BlockSpec((tq, QK_W), lambda b, i: (b * qb + i, 0)),
                pl.BlockSpec((seq, QK_W), lambda b, i: (b, 0)),
                pl.BlockSpec((seq, QK_W), lambda b, i: (b, 0))]
    args = [lam_p, g_subln, q, k, v]
    if has_ctx:
        past = ctx[0].shape[3]
        cspec = pl.BlockSpec((1, 1, N_HEADS, past, V_DIM), lambda b, i: (b, 0, 0, 0, 0))
        in_specs += [cspec, cspec]
        args += list(ctx)
    return pl.pallas_call(
        functools.partial(_attn_kernel, lam_init=lam_init, has_ctx=has_ctx),
        grid=(t // seq, qb),
        in_specs=in_specs,
        out_specs=pl.BlockSpec((tq, QK_W), lambda b, i: (b * qb + i, 0)),
        out_shape=jax.ShapeDtypeStruct((t, QK_W), BF16),
        compiler_params=_cparams(("arbitrary", "arbitrary")),
        name="diff_attention",
    )(*args)


def _dft_real(f, bc_ref, bs_ref, cs_ref, ss_ref):
    fb = f.astype(BF16)
    u = _dot(fb, bc_ref[...].astype(BF16)).astype(BF16)
    w = _dot(fb, bs_ref[...].astype(BF16)).astype(BF16)
    return (_dot(cs_ref[...].astype(BF16), u) - _dot(ss_ref[...].astype(BF16), w)).astype(BF16)


def _fourier_kernel(f_ref, bc_ref, bs_ref, cs_ref, ss_ref, o_ref):
    o_ref[...] = _dft_real(f_ref[...], bc_ref, bs_ref, cs_ref, ss_ref)


def _dft_consts(seq):
    c = np.arange(FOUR_G)
    ang_c = 2.0 * np.pi * ((c[:, None] * c[None, :]) % FOUR_G) / FOUR_G
    eye = np.eye(FOUR_W // FOUR_G)
    bc = np.kron(eye, np.cos(ang_c)) / math.sqrt(FOUR_G)
    bs = np.kron(eye, np.sin(ang_c)) / math.sqrt(FOUR_G)
    s = np.arange(seq)
    ang_s = 2.0 * np.pi * ((s[:, None] * s[None, :]) % seq) / seq
    cs = np.cos(ang_s) / math.sqrt(seq)
    ss = np.sin(ang_s) / math.sqrt(seq)
    return tuple(jnp.asarray(a, dtype=F32) for a in (bc, bs, cs, ss))


def _fourier(f, seq):
    t = f.shape[0]
    bc, bs, cs, ss = _dft_consts(seq)
    const = lambda b: (0, 0)
    return pl.pallas_call(
        _fourier_kernel,
        grid=(t // seq,),
        in_specs=[pl.BlockSpec((seq, FOUR_W), lambda b: (b, 0)),
                  pl.BlockSpec((FOUR_W, FOUR_W), const),
                  pl.BlockSpec((FOUR_W, FOUR_W), const),
                  pl.BlockSpec((seq, seq), const),
                  pl.BlockSpec((seq, seq), const)],
        out_specs=pl.BlockSpec((seq, FOUR_W), lambda b: (b, 0)),
        out_shape=jax.ShapeDtypeStruct((t, FOUR_W), BF16),
        compiler_params=_cparams(("arbitrary",)),
        name="fourier_mix",
    )(f, bc, bs, cs, ss)


def _post_kernel(o_ref, fm_ref, ga_ref, gf_ref, x_ref, mod_ref, gpost_ref, gffn_ref,
                 wpa_ref, wpf_ref, wout_ref, wrx_ref,
                 x1_ref, h2t_ref, afft_ref):
    m = mod_ref[0]
    gt1 = m[:, 2 * D_MODEL:3 * D_MODEL]
    sh2 = m[:, 3 * D_MODEL:4 * D_MODEL]
    sc2 = m[:, 4 * D_MODEL:5 * D_MODEL]
    subs = [slice(r0, r0 + POST_SUB) for r0 in range(0, o_ref.shape[0], POST_SUB)]
    ab = [(_dot(o_ref[rs, :], wpa_ref[...]), _dot(fm_ref[rs, :], wpf_ref[...])) for rs in subs]
    merged = [(ga_ref[rs, :] * a + gf_ref[rs, :] * b).astype(BF16) for rs, (a, b) in zip(subs, ab)]
    ys = [_dot(mg, wout_ref[...]) for mg in merged]
    h2s = []
    for rs, y in zip(subs, ys):
        x1 = x_ref[rs, :] + gt1 * _rms(y, gpost_ref[...], EPS)
        x1_ref[rs, :] = x1
        h2s.append(_rms(x1, gffn_ref[...], EPS) * (1.0 + sc2) + sh2)
    logits = [_dot(h2.astype(BF16), wrx_ref[...]) for h2 in h2s]
    for rs, h2, lg in zip(subs, h2s, logits):
        lt = lg.T[0:N_EXPERTS]
        et = jnp.exp(lt - jnp.max(lt, axis=0, keepdims=True))
        aff = et / jnp.sum(et, axis=0, keepdims=True)
        for u in range(POST_SUB // LANES):
            afft_ref[:, rs.start // LANES + u, :] = aff[:, u * LANES:(u + 1) * LANES]
        for kc in range(ROW_TILE):
            h2t_ref[pl.ds(rs.start * ROW_TILE + kc, POST_SUB, stride=ROW_TILE), :] = h2[:, kc * LANES:(kc + 1) * LANES]


def _post_mixer(o, fm, ga, gf, x2d, mod3, mod_row, g_post, g_ffn, wpa, wpf, wout, wrx):
    t = x2d.shape[0]
    tm = POST_BLOCK
    row = lambda i: (i, 0)
    const = lambda i: (0, 0)
    return pl.pallas_call(
        _post_kernel,
        grid=(t // tm,),
        in_specs=[pl.BlockSpec((tm, QK_W), row),
                  pl.BlockSpec((tm, FOUR_W), row),
                  pl.BlockSpec((tm, D_MODEL), row),
                  pl.BlockSpec((tm, D_MODEL), row),
                  pl.BlockSpec((tm, D_MODEL), row),
                  pl.BlockSpec((1, 1, N_MOD * D_MODEL), lambda i: (mod_row(i, tm), 0, 0)),
                  pl.BlockSpec((1, D_MODEL), const),
                  pl.BlockSpec((1, D_MODEL), const),
                  pl.BlockSpec((QK_W, D_MODEL), const),
                  pl.BlockSpec((FOUR_W, D_MODEL), const),
                  pl.BlockSpec((D_MODEL, D_MODEL), const),
                  pl.BlockSpec((D_MODEL, LANES), const)],
        out_specs=[pl.BlockSpec((tm, D_MODEL), row),
                   pl.BlockSpec((tm * ROW_TILE, LANES), row),
                   pl.BlockSpec((N_EXPERTS, tm // LANES, LANES), lambda i: (0, i, 0))],
        out_shape=[jax.ShapeDtypeStruct((t, D_MODEL), F32),
                   jax.ShapeDtypeStruct((t * ROW_TILE, LANES), F32),
                   jax.ShapeDtypeStruct((N_EXPERTS, t // LANES, LANES), F32)],
        compiler_params=_cparams(("arbitrary",)),
        name="post_mixer",
    )(o, fm, ga, gf, x2d, mod3, g_post, g_ffn, wpa, wpf, wout, wrx)


RANGE_ROWS = 8


def _route_kernel(aff_ref, posm_ref, pack_ref, span_ref, ranges_ref, rows_ref, *, cap, n_tok):
    aff = aff_ref[...]
    nt = n_tok // LANES
    capf = float(cap)

    def count_ge(v):
        return jnp.sum(jnp.where(aff >= v, 1.0, 0.0), axis=(1, 2), keepdims=True)

    def search(i, thr):
        cand = thr | jnp.left_shift(jnp.int32(1), 30 - i)
        return jnp.where(count_ge(pltpu.bitcast(cand, F32)) >= capf, cand, thr)

    thr = lax.fori_loop(0, 31, search, jnp.zeros((N_EXPERTS, 1, 1), I32))
    lo = pltpu.bitcast(thr, F32)
    hi = pltpu.bitcast(thr + 1, F32)

    def refine(i, c):
        lo, hi = c
        mid = lo + (hi - lo) * 0.5
        ok = count_ge(mid) >= capf
        return jnp.where(ok, mid, lo), jnp.where(ok, hi, mid)

    lo, hi = lax.fori_loop(0, 12, refine, (lo, hi))
    gt = aff >= hi
    eq = (aff >= lo) & (aff < hi)
    n_tie = capf - jnp.sum(jnp.where(gt, 1.0, 0.0), axis=(1, 2), keepdims=True)

    sq0 = lax.broadcasted_iota(I32, (LANES, LANES), 0)
    sq1 = lax.broadcasted_iota(I32, (LANES, LANES), 1)
    along_total = jnp.concatenate([jnp.where(sq0 <= sq1, 1.0, 0.0), jnp.ones((LANES, LANES), F32)],
                                  axis=1).astype(BF16)
    m = N_EXPERTS * nt
    r0 = lax.broadcasted_iota(I32, (m, m), 0)
    r1 = lax.broadcasted_iota(I32, (m, m), 1)
    earlier = jnp.where((r0 // nt == r1 // nt) & (r1 < r0), 1.0, 0.0).astype(BF16)
    lane = lax.broadcasted_iota(I32, (1, LANES), 1)
    token = lax.broadcasted_iota(I32, (nt, LANES), 0) * LANES + lane

    def tile_counts(x):
        both = _dot(x.reshape(m, LANES).astype(BF16), along_total)
        total = both[:, LANES:]
        before = _dot(earlier, total.astype(BF16))
        shape = (N_EXPERTS, nt, LANES)
        return both[:, :LANES].reshape(shape), total.reshape(shape), before.reshape(shape)

    eq_f = jnp.where(eq, 1.0, 0.0)
    eq_along, _, eq_before = tile_counts(eq_f)
    sel = jnp.where(gt, 1.0, jnp.where(eq_along + eq_before <= n_tie, eq_f, 0.0))
    sel_along, sel_total, sel_before = tile_counts(sel)
    posm_ref[...] = jnp.where(sel > 0.5, sel_along + sel_before - sel, -1.0).astype(I32)

    cnt = jnp.sum(sel, axis=0)
    rows_before = jnp.sum(sel_before, axis=0)
    tok_start = _dot(cnt.astype(BF16), along_total[:, :LANES]) - cnt + rows_before
    k = jnp.zeros((nt, LANES), F32)
    slots_before = jnp.zeros((nt, LANES), F32)
    slots_here = jnp.zeros((nt, LANES), F32)
    for e in range(N_EXPERTS):
        pack_ref[e] = (tok_start + k).astype(I32) * (1 << TOKEN_BITS) + token
        k = k + sel[e]
        slots_before = jnp.where(lane == e, sel_before[e], slots_before)
        slots_here = jnp.where(lane == e, sel_total[e], slots_here)

    eye = jnp.where(sq0 == sq1, 1.0, 0.0).astype(BF16)
    tok_end = tok_start + cnt
    parts = []
    for v in (tok_start, tok_end):
        high = jnp.floor(v * (1.0 / LANES))
        parts += [high, v - high * LANES]
    for c in range(nt):
        rows4 = jnp.zeros((LANES, LANES), F32)
        for r, part in enumerate(parts):
            rows4 = jnp.where(sq0 == r, jnp.broadcast_to(part[c:c + 1, :], (LANES, LANES)), rows4)
        span_ref[c * LANES:(c + 1) * LANES, :] = lax.dot_general(
            eye, rows4.astype(BF16), (((1,), (1,)), ((), ())), preferred_element_type=F32)

    ranges_ref[...] = jnp.zeros_like(ranges_ref)
    for s in range(cap // LANES):
        done = jnp.where(slots_before + slots_here <= float(s * LANES), 1.0, 0.0)
        begun = jnp.where(slots_before < float((s + 1) * LANES), 1.0, 0.0)
        ranges_ref[s:s + 1, :] = jnp.sum(done, axis=0, keepdims=True).astype(I32)
        ranges_ref[RANGE_ROWS + s:RANGE_ROWS + s + 1, :] = jnp.sum(begun, axis=0, keepdims=True).astype(I32)
    rows_ref[...] = rows_before.astype(I32)


def _route(aff, cap):
    n_e, nt, _ = aff.shape
    n_tok = nt * LANES
    assert cap // LANES <= RANGE_ROWS
    return pl.pallas_call(
        functools.partial(_route_kernel, cap=cap, n_tok=n_tok),
        out_shape=[jax.ShapeDtypeStruct((n_e, nt, LANES), I32),
                   jax.ShapeDtypeStruct((n_e, nt, LANES), I32),
                   jax.ShapeDtypeStruct((n_tok, LANES), F32),
                   jax.ShapeDtypeStruct((2 * RANGE_ROWS, LANES), I32),
                   jax.ShapeDtypeStruct((nt, LANES), I32)],
        compiler_params=pltpu.CompilerParams(vmem_limit_bytes=VMEM_LIMIT),
        name="route",
    )(aff)


def _slots_kernel(ranges_ref, posm_ref, pack_ref, idx_ref, qslot_ref, acc_ref):
    e = pl.program_id(0)
    sub = lax.broadcasted_iota(I32, (LANES, LANES), 0)
    eye = sub == lax.broadcasted_iota(I32, (LANES, LANES), 1)
    n_tiles = idx_ref.shape[1]

    for s in range(n_tiles):
        slot = sub + s * LANES

        def body(c, acc, slot=slot):
            hit = posm_ref[e, pl.ds(c, 1), :] == slot
            return acc + jnp.where(hit, pack_ref[e, pl.ds(c, 1), :], 0)

        acc_ref[s] = lax.fori_loop(ranges_ref[s, e], ranges_ref[RANGE_ROWS + s, e], body,
                                   jnp.zeros((LANES, LANES), I32))

    def as_row(part):
        col = jnp.sum(part.astype(F32), axis=1, keepdims=True)
        return jnp.sum(jnp.where(eye, col, 0.0), axis=0, keepdims=True).astype(I32)

    for s in range(n_tiles):
        acc = acc_ref[s]
        idx_ref[0, s:s + 1, :] = as_row(acc & ((1 << TOKEN_BITS) - 1)) * ROW_TILE
        qslot_ref[0, s:s + 1, :] = as_row(lax.shift_right_logical(acc, TOKEN_BITS)) * ROW_TILE


def _slot_lists(ranges, posm4, qdst4, cap):
    ns = cap // LANES
    grid_spec = pltpu.PrefetchScalarGridSpec(
        num_scalar_prefetch=1,
        grid=(N_EXPERTS,),
        in_specs=[_whole_vmem(), _whole_vmem()],
        out_specs=[pl.BlockSpec((1, ns, LANES), lambda e, *_: (e, 0, 0)),
                   pl.BlockSpec((1, ns, LANES), lambda e, *_: (e, 0, 0))],
        scratch_shapes=[pltpu.VMEM((ns, LANES, LANES), I32)])
    idx, qslot = pl.pallas_call(
        _slots_kernel,
        grid_spec=grid_spec,
        out_shape=[jax.ShapeDtypeStruct((N_EXPERTS, ns, LANES), I32),
                   jax.ShapeDtypeStruct((N_EXPERTS, ns, LANES), I32)],
        compiler_params=_cparams(("arbitrary",)),
        name="slot_lists",
    )(ranges, posm4, qdst4)
    return idx.reshape(N_EXPERTS, cap), qslot.reshape(N_EXPERTS, cap)


def _moe_kernel(idxc_ref, idxs_ref, qc_ref, qs_ref,
                hc_ref, hs_ref, wr_ref, wg_ref, wu_ref, wd_ref, zc_ref, zs_ref,
                xbuf, ybuf, xb_ref, gate_ref, acc_ref, gsem, ssem, *, capc, caps):
    e = pl.program_id(0)
    j = pl.program_id(1)
    n_e = pl.num_programs(0)
    n_j = FF_STEPS
    slot = e % 2
    other = 1 - slot
    rows = capc + caps
    gc, gs = _per_step(capc), _per_step(caps)
    groups = ((hc_ref, idxc_ref, zc_ref, qc_ref, gc, 0), (hs_ref, idxs_ref, zs_ref, qs_ref, gs, gc * n_j))

    def tile(ref, first_sublane):
        return ref.at[pl.ds(pl.multiple_of(first_sublane, ROW_TILE), ROW_TILE), :]

    def gather(ex, sl, step, i, group):
        h_ref, idx_ref, _, _, per_step, base = group
        p = step * per_step + i
        src = tile(h_ref, idx_ref[ex * (per_step * n_j) + p])
        pltpu.make_async_copy(src, xbuf.at[sl, :, base + p, :], gsem.at[sl]).start()

    def scatter(table_row, sl, step, i, group):
        _, _, z_ref, q_ref, per_step, base = group
        p = step * per_step + i
        dst = tile(z_ref, q_ref[table_row * (per_step * n_j) + p])
        pltpu.make_async_copy(ybuf.at[sl, :, base + p, :], dst, ssem.at[sl]).start()

    def all_steps(fn):
        for group in groups:
            def body(p, carry, group=group):
                fn(p, group)
                return carry
            lax.fori_loop(0, group[4] * n_j, body, 0, unroll=8)

    def wait_all(buf, sem, sl):
        pltpu.make_async_copy(buf.at[sl], buf.at[sl], sem.at[sl]).wait()

    @pl.when((e == 0) & (j == 0))
    def _():
        ybuf[...] = jnp.zeros_like(ybuf)
        all_steps(lambda p, group: gather(0, 0, 0, p, group))

    @pl.when(j == 0)
    def _():
        wait_all(xbuf, gsem, slot)
        for base, n, dst in ((0, capc, 0), (gc * n_j, caps, capc)):
            for kc in range(ROW_TILE):
                xb_ref[dst:dst + n, kc * LANES:(kc + 1) * LANES] = xbuf[slot, kc, base:base + n, :].astype(BF16)
        acc_ref[...] = jnp.zeros_like(acc_ref)
        logits = _dot(xb_ref[...], wr_ref[...])
        lane = lax.broadcasted_iota(I32, (1, LANES), 1)
        is_expert = lane < N_EXPERTS
        ex = jnp.exp(logits - jnp.max(jnp.where(is_expert, logits, -jnp.inf), axis=-1, keepdims=True))
        mine = jnp.sum(jnp.where(lane == e, ex, 0.0), axis=-1, keepdims=True)
        gate = mine / jnp.sum(jnp.where(is_expert, ex, 0.0), axis=-1, keepdims=True)
        gate_ref[...] = jnp.broadcast_to(gate, gate_ref.shape)

    nxt = jnp.minimum(e + 1, n_e - 1)
    for group in groups:
        for i in range(group[4]):
            gather(nxt, other, j, i, group)
            scatter(e, other, j, i, group)

    x = xb_ref[...]
    g = _dot(x, wg_ref[0].astype(BF16))
    u = _dot(x, wu_ref[0].astype(BF16))
    hid = (g * jax.nn.sigmoid(g) * u).astype(BF16)
    acc_ref[...] += _dot(hid, wd_ref[0].astype(BF16))

    @pl.when(j == n_j - 1)
    def _():
        @pl.when(e >= 1)
        def _():
            wait_all(ybuf, ssem, slot)

        gate = gate_ref[...]
        for base, n, src in ((0, capc, 0), (gc * n_j, caps, capc)):
            for kc in range(ROW_TILE):
                y = acc_ref[src:src + n, kc * LANES:(kc + 1) * LANES] * gate[src:src + n]
                ybuf[slot, kc, base:base + n, :] = y

        @pl.when(e == n_e - 1)
        def _():
            all_steps(lambda p, group: scatter(e + 1, slot, 0, p, group))
            wait_all(ybuf, ssem, other)
            wait_all(ybuf, ssem, slot)
            wait_all(xbuf, gsem, other)


def _per_step(cap):
    return -(-cap // FF_STEPS)


def _copy_tables(idx, qslot, n_rows):
    n_e, cap = idx.shape
    padded = _per_step(cap) * FF_STEPS
    n_pad = padded - cap
    idx_p = jnp.concatenate([idx, jnp.zeros((n_e, n_pad), I32)], axis=1)
    spare = n_rows + jnp.arange(padded + n_e * n_pad, dtype=I32)
    lead = spare[:padded][None, :]
    pad_rows = spare[padded:].reshape(n_e, n_pad)
    q_p = jnp.concatenate([lead, jnp.concatenate([qslot // ROW_TILE, pad_rows], axis=1)], axis=0) * ROW_TILE
    return idx_p.reshape(-1), q_p.reshape(-1), n_rows + padded + n_e * n_pad


def _moe(idxc, idxs, qc, qs, hc, hs, wrx, w_gate, w_up, w_down):
    capc, caps = idxc.shape[1], idxs.shape[1]
    rows = capc + caps
    tf = FF_TILE
    idxc, qc, zc_rows = _copy_tables(idxc, qc, N_EXPERTS * capc)
    idxs, qs, zs_rows = _copy_tables(idxs, qs, N_EXPERTS * caps)
    buf_rows = (_per_step(capc) + _per_step(caps)) * FF_STEPS
    any_spec = pl.BlockSpec(memory_space=pl.ANY)
    grid_spec = pltpu.PrefetchScalarGridSpec(
        num_scalar_prefetch=4,
        grid=(N_EXPERTS, FF_STEPS),
        in_specs=[any_spec, any_spec,
                  pl.BlockSpec((D_MODEL, LANES), lambda e, j, *_: (0, 0)),
                  pl.BlockSpec((1, D_MODEL, tf), lambda e, j, *_: (e, 0, j)),
                  pl.BlockSpec((1, D_MODEL, tf), lambda e, j, *_: (e, 0, j)),
                  pl.BlockSpec((1, tf, D_MODEL), lambda e, j, *_: (e, j, 0))],
        out_specs=[any_spec, any_spec],
        scratch_shapes=[pltpu.VMEM((2, ROW_TILE, buf_rows, LANES), F32),
                        pltpu.VMEM((2, ROW_TILE, buf_rows, LANES), F32),
                        pltpu.VMEM((rows, D_MODEL), BF16),
                        pltpu.VMEM((rows, LANES), F32),
                        pltpu.VMEM((rows, D_MODEL), F32),
                        pltpu.SemaphoreType.DMA((2,)),
                        pltpu.SemaphoreType.DMA((2,))])
    return pl.pallas_call(
        functools.partial(_moe_kernel, capc=capc, caps=caps),
        grid_spec=grid_spec,
        out_shape=[jax.ShapeDtypeStruct((zc_rows * ROW_TILE, LANES), F32),
                   jax.ShapeDtypeStruct((zs_rows * ROW_TILE, LANES), F32)],
        compiler_params=_cparams(("arbitrary", "arbitrary")),
        name="expert_ffn",
    )(idxc, idxs, qc, qs, hc, hs, wrx, w_gate, w_up, w_down)


Z_BUFFERS = 3


def _combine_kernel(clo_ref, chi_ref, z_ref, span_ref, x1_ref, mod_ref, g_ref, o_ref,
                    zbuf, acc_ref, sems, state, *, n_chunks):
    b = pl.program_id(0)
    chunk_rows = ROW_CHUNK * ROW_TILE

    @pl.when(b == 0)
    def _():
        state[0] = 0
        state[1] = 0

    def chunk_copy(c):
        src = z_ref.at[pl.ds(pl.multiple_of(c * chunk_rows, chunk_rows), chunk_rows), :]
        return pltpu.make_async_copy(src, zbuf.at[c % Z_BUFFERS], sems.at[c % Z_BUFFERS])

    acc_ref[...] = jnp.zeros_like(acc_ref)
    first_row = span_ref[:, 0:1] * LANES + span_ref[:, 1:2]
    end_row = span_ref[:, 2:3] * LANES + span_ref[:, 3:4]
    lane_row = lax.broadcasted_iota(I32, (TOK_BLOCK, ROW_CHUNK), 1)

    def body(c, carry):
        for _ in range(Z_BUFFERS):
            nxt = state[0]

            @pl.when(nxt <= jnp.minimum(c + Z_BUFFERS - 1, n_chunks - 1))
            def _():
                chunk_copy(nxt).start()
                state[0] = nxt + 1

        @pl.when(state[1] <= c)
        def _():
            chunk_copy(c).wait()
            state[1] = c + 1

        slot = c % Z_BUFFERS
        row = (c * ROW_CHUNK + lane_row).astype(F32)
        onehot = jnp.where((row >= first_row) & (row < end_row), 1.0, 0.0).astype(BF16)
        y = jnp.concatenate([zbuf[slot, pl.ds(kc, ROW_CHUNK, stride=ROW_TILE), :] for kc in range(ROW_TILE)],
                            axis=1)
        acc_ref[...] += _dot(onehot, y.astype(BF16))
        return carry

    lax.fori_loop(clo_ref[b], chi_ref[b], body, 0)
    gt2 = mod_ref[0][:, 5 * D_MODEL:6 * D_MODEL]
    o_ref[...] = x1_ref[...] + gt2 * _rms(acc_ref[...], g_ref[...], EPS)


def _combine(ranges, z, span, x1, mod3, mod_row, g_post_ffn):
    clo, chi = ranges
    t = x1.shape[0]
    tb = TOK_BLOCK
    grid_spec = pltpu.PrefetchScalarGridSpec(
        num_scalar_prefetch=2,
        grid=(t // tb,),
        in_specs=[pl.BlockSpec(memory_space=pl.ANY),
                  pl.BlockSpec((tb, LANES), lambda b, *_: (b, 0)),
                  pl.BlockSpec((tb, D_MODEL), lambda b, *_: (b, 0)),
                  pl.BlockSpec((1, 1, N_MOD * D_MODEL), lambda b, *_: (mod_row(b, tb), 0, 0)),
                  pl.BlockSpec((1, D_MODEL), lambda b, *_: (0, 0))],
        out_specs=pl.BlockSpec((tb, D_MODEL), lambda b, *_: (b, 0)),
        scratch_shapes=[pltpu.VMEM((Z_BUFFERS, ROW_CHUNK * ROW_TILE, LANES), F32),
                        pltpu.VMEM((tb, D_MODEL), F32),
                        pltpu.SemaphoreType.DMA((Z_BUFFERS,)),
                        pltpu.SMEM((2,), I32)])
    return pl.pallas_call(
        functools.partial(_combine_kernel, n_chunks=2 * t // ROW_CHUNK),
        grid_spec=grid_spec,
        out_shape=jax.ShapeDtypeStruct((t, D_MODEL), F32),
        compiler_params=_cparams(("arbitrary",)),
        name="combine",
    )(clo, chi, z, span, x1, mod3, g_post_ffn)


def _rope_tables(seq):
    half = HEAD_DIM // 4
    freqs = ROPE_THETA ** (-np.arange(half, dtype=np.float64) / half)
    s = np.arange(seq)
    row = (s // GRID_W)[:, None] * freqs[None, :]
    col = (s % GRID_W)[:, None] * freqs[None, :]
    ang = np.concatenate([row, row, col, col], axis=1)
    ang = np.tile(ang, (1, QK_W // HEAD_DIM))
    lane = np.arange(QK_W)
    sign = np.where((lane % 32) < 16, -1.0, 1.0)[None, :]
    return (jnp.asarray(np.cos(ang), dtype=F32), jnp.asarray(np.sin(ang) * sign, dtype=F32))


def _combine_ranges(rows, n_tok):
    step = TOK_BLOCK // LANES
    nb = n_tok // TOK_BLOCK
    lo = rows[0:nb * step:step, 0]
    hi = jnp.concatenate([lo[1:], jnp.full((1,), 2 * n_tok, I32)])
    return (lo // ROW_CHUNK).astype(I32), ((hi + ROW_CHUNK - 1) // ROW_CHUNK).astype(I32)


def kernel(x_prompt, x_sample, c, cache_k, cache_v, c_ctx, w_mod, b_mod, g_pre_mix, g_post_mix, g_pre_ffn, g_post_ffn, w_in, lam_q1, lam_k1, lam_q2, lam_k2, g_subln, w_proj_attn, w_proj_fourier, w_out, w_router, w_gate, w_up, w_down):
    assert w_mod.shape[0] == 1
    lam_init = 0.8 - 0.6 * math.exp(-0.3 * 0)
    bp, sp, _ = x_prompt.shape
    bs, ss, _ = x_sample.shape

    cond8 = jnp.concatenate([c_ctx[None, :], c, jnp.zeros((8 - 1 - bs, D_MODEL), F32)], axis=0)
    mod3 = _modulation(cond8, w_mod[0], b_mod).reshape(8, 1, N_MOD * D_MODEL)

    w_in_b = None
    wpa =w_proj_attn[0].astype(BF16)
    wpf = w_proj_fourier[0].astype(BF16)
    wout = w_out[0].astype(BF16)
    wr = w_router[0].astype(BF16)
    wrx = jnp.concatenate([wr, jnp.zeros((D_MODEL, LANES - N_EXPERTS), BF16)], axis=1)
    lam_p = jnp.concatenate([lam_q1, lam_k1, lam_q2, lam_k2], axis=0)

    groups = []
    for x, seq, positional, ctx in ((x_sample, ss, True, (cache_k, cache_v)),
                                    (x_prompt, sp, False, None)):
        nb = x.shape[0]
        t = nb * seq
        x2d = x.reshape(t, D_MODEL)
        if positional:
            mod_row = lambda i, tm, seq=seq: 1 + (i * tm) // seq
        else:
            mod_row = lambda i, tm: 0
        self_contained = ctx is None and not positional
        if self_contained:
            assert PRE_BLOCK % seq == 0
            pre = _ctx_mixer(x2d, mod3, mod_row, g_pre_mix, w_in_b, seq, (lam_p, g_subln, lam_init), tm=PRE_BLOCK)
            o, fm, ga, gf = pre[:4]
        else:
            q, k, v, f, ga, gf, w_in_b = _pre_mixer(x2d, mod3, mod_row, g_pre_mix, w_in[0],
                                                    _rope_tables(seq) if positional else None, seq, tm=ROW_BLOCK)
            o = _attention(lam_p, g_subln, q, k, v, ctx, seq, lam_init)
            fm = _fourier(f, seq)
        x1, h2t, aff_t = _post_mixer(o, fm, ga, gf, x2d, mod3, mod_row, g_post_mix, g_pre_ffn,
                                     wpa, wpf, wout, wrx)
        cap = 2 * t // N_EXPERTS
        assert t <= 1 << TOKEN_BITS
        posm, pack, span, ranges, rows = _route(aff_t, cap)
        idx, qslot = _slot_lists(ranges, posm, pack, cap)
        groups.append(dict(x1=x1, h2t=h2t, idx=idx, qslot=qslot, span=span, ranges=_combine_ranges(rows, t),
                           mod_row=mod_row, cache=pre[4:] if self_contained else None, shape=x.shape))

    gs_, gc = groups
    zc, zs = _moe(gc["idx"], gs_["idx"], gc["qslot"], gs_["qslot"], gc["h2t"], gs_["h2t"], wrx,
                  w_gate[0], w_up[0], w_down[0])
    outs = []
    for g, z in ((gc, zc), (gs_, zs)):
        out = _combine(g["ranges"], z, g["span"], g["x1"], mod3, g["mod_row"], g_post_ffn)
        outs.append(out.reshape(g["shape"]))
    new_k, new_v = gc["cache"]
    return (outs[0], outs[1], new_k, new_v)
```

```python
import functools
import math

import numpy as np
import jax
import jax.numpy as jnp
from jax import lax
from jax.experimental import pallas as pl
from jax.experimental.pallas import tpu as pltpu

F32 = jnp.float32
BF16 = jnp.bfloat16
I32 = jnp.int32

D_MODEL = 1024
N_HEADS = 6
HEAD_DIM = 64
V_DIM = 128
QK_W = 768
FOUR_W = 256
FOUR_G = 64
IN_W = 4608
N_EXPERTS = 16
D_FF = 2816
N_MOD = 6
GRID_W = 64
ROPE_THETA = 10000.0
EPS = 1e-6
SUBLN_EPS = 1e-5

LANES = 128
ROW_BLOCK = 256
PRE_BLOCK = 512
GATE_CHUNK = 256
POST_BLOCK = 1024
POST_SUB = 256
WEIGHT_ROWS = 128
TOK_BLOCK = 256
ROW_CHUNK = 256
FF_TILE = 256
FF_STEPS = D_FF // FF_TILE
ROW_TILE = D_MODEL // LANES
TOKEN_BITS = 13
VMEM_LIMIT = 56 * 1024 * 1024


def _cparams(sem):
    return pltpu.CompilerParams(dimension_semantics=sem, vmem_limit_bytes=VMEM_LIMIT)


def _dot(a, b):
    return jnp.dot(a, b, preferred_element_type=F32)


def _rms(x, g, eps):
    return x * lax.rsqrt(jnp.mean(x * x, axis=-1, keepdims=True) + eps) * g


def _whole_vmem():
    return pl.BlockSpec(memory_space=pltpu.MemorySpace.VMEM)


def _cast_weight(w_hbm, wb_ref, stage_ref, sem):
    rows = stage_ref.shape[1]
    chunks = [pltpu.make_async_copy(w_hbm.at[pl.ds(r0, rows), :], stage_ref.at[n % 2], sem.at[n % 2])
              for n, r0 in enumerate(range(0, w_hbm.shape[0], rows))]
    chunks[0].start()
    for n, chunk in enumerate(chunks):
        if n + 1 < len(chunks):
            chunks[n + 1].start()
        chunk.wait()
        wb_ref[n * rows:(n + 1) * rows, :] = stage_ref[n % 2].astype(BF16)


def _mod_kernel(c_ref, w_ref, b_ref, o_ref):
    c = c_ref[...]
    s = c * jax.nn.sigmoid(c)
    o_ref[...] = _dot(s.astype(BF16), w_ref[...].astype(BF16)) + b_ref[...]


def _modulation(cond8, w_mod, b_mod):
    tn = 1024
    n = N_MOD * D_MODEL
    return pl.pallas_call(
        _mod_kernel,
        grid=(n // tn,),
        in_specs=[pl.BlockSpec((8, D_MODEL), lambda j: (0, 0)),
                  pl.BlockSpec((D_MODEL, tn), lambda j: (0, j)),
                  pl.BlockSpec((1, tn), lambda j: (0, j))],
        out_specs=pl.BlockSpec((8, tn), lambda j: (0, j)),
        out_shape=jax.ShapeDtypeStruct((8, n), F32),
        compiler_params=_cparams(("arbitrary",)),
        name="modulation",
    )(cond8, w_mod, b_mod)


def _diff_lambda(lp, lam_init):
    s1 = jnp.sum(lp[0:1] * lp[1:2], axis=-1, keepdims=True)
    s2 = jnp.sum(lp[2:3] * lp[3:4], axis=-1, keepdims=True)
    return jnp.exp(s1) - jnp.exp(s2) + lam_init


def _attention_weights(q, k):
    comp1 = lax.broadcasted_iota(I32, (1, V_DIM), 1) < HEAD_DIM
    qs = q * jnp.asarray(HEAD_DIM ** -0.5, BF16)
    zero = jnp.zeros_like(qs)

    def weights(qc):
        s = lax.dot_general(qc, k, (((1,), (1,)), ((), ())), preferred_element_type=F32)
        return jnp.exp(s - jnp.max(s, axis=-1, keepdims=True)).astype(BF16)

    return weights(jnp.where(comp1, qs, zero)), weights(jnp.where(comp1, zero, qs))


def _attention_output(weights, v, lam, g_subln, lam_init):
    v_ones = jnp.concatenate([v, jnp.ones_like(v)], axis=1)

    def attend(ex):
        ov = _dot(ex, v_ones)
        return ov[:, 0:V_DIM] / ov[:, V_DIM:2 * V_DIM]

    o = attend(weights[0]) - lam * attend(weights[1])
    return (_rms(o, g_subln, SUBLN_EPS) * (1.0 - lam_init)).astype(BF16)


def _rope(z, cos, sin_signed, first_half):
    fwd = pltpu.roll(z, QK_W - 16, axis=1)
    bwd = pltpu.roll(z, 16, axis=1)
    return z * cos + jnp.where(first_half, fwd, bwd) * sin_signed


def _pre_kernel(*refs, positional):
    it = iter(refs)
    x_ref, mod_ref, g_ref, w_hbm = next(it), next(it), next(it), next(it)
    if positional:
        cos_ref, sin_ref = next(it), next(it)
    q_ref, k_ref, v_ref, f_ref, ga_ref, gf_ref = (next(it) for _ in range(6))
    wb_hbm, w_ref, stage_ref, sem = next(it), next(it), next(it), next(it)
    weight_out = pltpu.make_async_copy(w_ref, wb_hbm, sem.at[2])

    @pl.when(pl.program_id(0) == 0)
    def _():
        _cast_weight(w_hbm, w_ref, stage_ref, sem)
        weight_out.start()

    @pl.when(pl.program_id(0) == pl.num_programs(0) - 1)
    def _():
        weight_out.wait()

    m = mod_ref[0]
    sh1 = m[:, 0:D_MODEL]
    sc1 = m[:, D_MODEL:2 * D_MODEL]
    h = _rms(x_ref[...], g_ref[...], EPS) * (1.0 + sc1) + sh1
    hb = h.astype(BF16)

    def proj(lo, hi):
        return _dot(hb, w_ref[:, lo:hi])

    zq = proj(0, QK_W)
    zk = proj(QK_W, 2 * QK_W)
    zv = proj(2 * QK_W, 3 * QK_W)
    if positional:
        lane = lax.broadcasted_iota(I32, (1, QK_W), 1)
        first_half = (lane % 32) < 16
        cos = cos_ref[...]
        sin_signed = sin_ref[...]
        zq = _rope(zq, cos, sin_signed, first_half)
        zk = _rope(zk, cos, sin_signed, first_half)
    q_ref[...] = zq.astype(BF16)
    k_ref[...] = zk.astype(BF16)
    v_ref[...] = zv.astype(BF16)
    f0 = 3 * QK_W
    g0 = f0 + FOUR_W
    f_ref[...] = proj(f0, g0)
    ga_ref[...] = jax.nn.sigmoid(proj(g0, g0 + D_MODEL)).astype(BF16)
    gf_ref[...] = jax.nn.sigmoid(proj(g0 + D_MODEL, IN_W)).astype(BF16)


def _ctx_kernel(x_ref, mod_ref, g_ref, w_ref, bc_ref, bs_ref, cs_ref, ss_ref, lam_ref, gs_ref,
                o_ref, f_ref, ga_ref, gf_ref, kc_ref, vc_ref, *, seq, lam_init):
    m = mod_ref[0]
    sh1 = m[:, 0:D_MODEL]
    sc1 = m[:, D_MODEL:2 * D_MODEL]
    hb = (_rms(x_ref[...], g_ref[...], EPS) * (1.0 + sc1) + sh1).astype(BF16)
    seqs = [slice(b * seq, (b + 1) * seq) for b in range(x_ref.shape[0] // seq)]
    lam = _diff_lambda(lam_ref[...], lam_init)
    f0 = 3 * QK_W
    g0 = f0 + FOUR_W

    def proj(lo, hi):
        return _dot(hb, w_ref[:, lo:hi])

    q, k, v = {}, {}, {}

    def qkv_chunk(store, cache_ref, lo, c):
        cols = slice(c * GATE_CHUNK, (c + 1) * GATE_CHUNK)
        z = proj(lo + cols.start, lo + cols.stop)
        for hd in range(cols.start // V_DIM, cols.stop // V_DIM):
            z_hd = z[:, hd * V_DIM - cols.start:(hd + 1) * V_DIM - cols.start]
            if cache_ref is not None:
                for b, rs in enumerate(seqs):
                    cache_ref[b, 0, hd] = z_hd[rs, :]
            store[hd] = z_hd.astype(BF16)

    def fourier_chunk():
        f = proj(f0, g0)
        for rs in seqs:
            f_ref[rs, :] = _dft_real(f[rs, :], bc_ref, bs_ref, cs_ref, ss_ref)

    def gate_chunk(ref, lo, c):
        cols = slice(c * GATE_CHUNK, (c + 1) * GATE_CHUNK)
        ref[:, cols] = jax.nn.sigmoid(proj(lo + cols.start, lo + cols.stop)).astype(BF16)

    for store, cache, lo in ((q, None, 0), (k, kc_ref, QK_W), (v, vc_ref, 2 * QK_W)):
        for c in range(QK_W // GATE_CHUNK):
            qkv_chunk(store, cache, lo, c)
    matmul_work = [fourier_chunk] + [functools.partial(gate_chunk, ref, lo, c)
                                     for ref, lo in ((ga_ref, g0), (gf_ref, g0 + D_MODEL))
                                     for c in range(D_MODEL // GATE_CHUNK)]
    for rs in seqs:
        for hd in range(N_HEADS):
            weights = _attention_weights(q[hd][rs, :], k[hd][rs, :])
            if matmul_work:
                matmul_work.pop(0)()
            o_ref[rs, hd * V_DIM:(hd + 1) * V_DIM] = _attention_output(weights, v[hd][rs, :], lam, gs_ref[...],
                                                                      lam_init)
    for work in matmul_work:
        work()


def _ctx_mixer(x2d, mod3, mod_row, g_pre, w_in_b, seq, attn_params, tm):
    t = x2d.shape[0]
    lam_p, g_subln, lam_init = attn_params
    assert tm % seq == 0 and t % tm == 0
    row = lambda i: (i, 0)
    const = lambda i: (0, 0)
    consts = _dft_consts(seq) + (lam_p, g_subln)
    nb = t // seq
    cshape = jax.ShapeDtypeStruct((nb, 1, N_HEADS, seq, V_DIM), F32)
    cspec = pl.BlockSpec((tm // seq, 1, N_HEADS, seq, V_DIM), lambda i: (i, 0, 0, 0, 0))
    return pl.pallas_call(
        functools.partial(_ctx_kernel, seq=seq, lam_init=lam_init),
        grid=(t // tm,),
        in_specs=[pl.BlockSpec((tm, D_MODEL), row),
                  pl.BlockSpec((1, 1, N_MOD * D_MODEL), lambda i: (mod_row(i, tm), 0, 0)),
                  pl.BlockSpec((1, D_MODEL), const),
                  _whole_vmem()] + [pl.BlockSpec(c.shape, const) for c in consts],
        out_specs=[pl.BlockSpec((tm, QK_W), row),
                   pl.BlockSpec((tm, FOUR_W), row),
                   pl.BlockSpec((tm, D_MODEL), row),
                   pl.BlockSpec((tm, D_MODEL), row),
                   cspec, cspec],
        out_shape=[jax.ShapeDtypeStruct((t, QK_W), BF16),
                   jax.ShapeDtypeStruct((t, FOUR_W), BF16),
                   jax.ShapeDtypeStruct((t, D_MODEL), BF16),
                   jax.ShapeDtypeStruct((t, D_MODEL), BF16),
                   cshape, cshape],
        compiler_params=_cparams(("arbitrary",)),
        name="ctx_mixer",
    )(x2d, mod3, g_pre, w_in_b, *consts)


def _pre_mixer(x2d, mod3, mod_row, g_pre, w_in, rope_tabs, seq, tm):
    t = x2d.shape[0]
    positional = rope_tabs is not None
    assert seq % tm == 0
    blocks_per_seq = seq // tm
    row = lambda i: (i, 0)
    any_spec = pl.BlockSpec(memory_space=pl.ANY)
    in_specs = [pl.BlockSpec((tm, D_MODEL), row),
                pl.BlockSpec((1, 1, N_MOD * D_MODEL), lambda i: (mod_row(i, tm), 0, 0)),
                pl.BlockSpec((1, D_MODEL), lambda i: (0, 0)),
                any_spec]
    args = [x2d, mod3, g_pre, w_in]
    if positional:
        in_specs += [pl.BlockSpec((tm, QK_W), lambda i: (i % blocks_per_seq, 0))] * 2
        args += list(rope_tabs)
    out_shape = [jax.ShapeDtypeStruct((t, QK_W), BF16)] * 3 + [
        jax.ShapeDtypeStruct((t, FOUR_W), F32),
        jax.ShapeDtypeStruct((t, D_MODEL), BF16),
        jax.ShapeDtypeStruct((t, D_MODEL), BF16),
        jax.ShapeDtypeStruct(w_in.shape, BF16)]
    out_specs = [pl.BlockSpec((tm, QK_W), row)] * 3 + [
        pl.BlockSpec((tm, FOUR_W), row),
        pl.BlockSpec((tm, D_MODEL), row),
        pl.BlockSpec((tm, D_MODEL), row),
        any_spec]
    return pl.pallas_call(
        functools.partial(_pre_kernel, positional=positional),
        grid=(t // tm,),
        in_specs=in_specs,
        out_specs=out_specs,
        out_shape=out_shape,
        scratch_shapes=[pltpu.VMEM(w_in.shape, BF16),
                        pltpu.VMEM((2, WEIGHT_ROWS, w_in.shape[1]), F32),
                        pltpu.SemaphoreType.DMA((3,))],
        compiler_params=_cparams(("arbitrary",)),
        name="pre_mixer",
    )(*args)


def _attn_kernel(*refs, lam_init, has_ctx):
    it = iter(refs)
    lam_ref, gs_ref, q_ref, k_ref, v_ref = (next(it) for _ in range(5))
    if has_ctx:
        ck_ref, cv_ref = next(it), next(it)
    o_ref = next(it)

    lam = _diff_lambda(lam_ref[...], lam_init)

    def operand(ref, cache_ref, hd):
        x = ref[:, hd * V_DIM:(hd + 1) * V_DIM]
        return jnp.concatenate([cache_ref[0, 0, hd].astype(BF16), x], axis=0) if has_ctx else x

    def head_weights(hd):
        return _attention_weights(q_ref[:, hd * V_DIM:(hd + 1) * V_DIM], operand(k_ref, ck_ref if has_ctx else None, hd))

    weights = head_weights(0)
    for hd in range(N_HEADS):
        nxt = head_weights(hd + 1) if hd + 1 < N_HEADS else None
        v = operand(v_ref, cv_ref if has_ctx else None, hd)
        o_ref[:, hd * V_DIM:(hd + 1) * V_DIM] = _attention_output(weights, v, lam, gs_ref[...], lam_init)
        weights = nxt


def _attention(lam_p, g_subln, q, k, v, ctx, seq, lam_init):
    t = q.shape[0]
    tq = ROW_BLOCK
    qb = seq // tq
    has_ctx = ctx is not None
    in_specs = [pl.BlockSpec((4, HEAD_DIM), lambda b, i: (0, 0)),
                pl.BlockSpec((1, V_DIM), lambda b, i: (0, 0)),
                pl.BlockSpec((tq, QK_W), lambda b, i: (b * qb + i, 0)),
                pl.BlockSpec((seq, QK_W), lambda b, i: (b, 0)),
                pl.BlockSpec((seq, QK_W), lambda b, i: (b, 0))]
    args = [lam_p, g_subln, q, k, v]
    if has_ctx:
        past = ctx[0].shape[3]
        cspec = pl.BlockSpec((1, 1, N_HEADS, past, V_DIM), lambda b, i: (b, 0, 0, 0, 0))
        in_specs += [cspec, cspec]
        args += list(ctx)
    return pl.pallas_call(
        functools.partial(_attn_kernel, lam_init=lam_init, has_ctx=has_ctx),
        grid=(t // seq, qb),
        in_specs=in_specs,
        out_specs=pl.BlockSpec((tq, QK_W), lambda b, i: (b * qb + i, 0)),
        out_shape=jax.ShapeDtypeStruct((t, QK_W), BF16),
        compiler_params=_cparams(("arbitrary", "arbitrary")),
        name="diff_attention",
    )(*args)


def _dft_real(f, bc_ref, bs_ref, cs_ref, ss_ref):
    fb = f.astype(BF16)
    u = _dot(fb, bc_ref[...].astype(BF16)).astype(BF16)
    w = _dot(fb, bs_ref[...].astype(BF16)).astype(BF16)
    return (_dot(cs_ref[...].astype(BF16), u) - _dot(ss_ref[...].astype(BF16), w)).astype(BF16)


def _fourier_kernel(f_ref, bc_ref, bs_ref, cs_ref, ss_ref, o_ref):
    o_ref[...] = _dft_real(f_ref[...], bc_ref, bs_ref, cs_ref, ss_ref)


def _dft_consts(seq):
    c = np.arange(FOUR_G)
    ang_c = 2.0 * np.pi * ((c[:, None] * c[None, :]) % FOUR_G) / FOUR_G
    eye = np.eye(FOUR_W // FOUR_G)
    bc = np.kron(eye, np.cos(ang_c)) / math.sqrt(FOUR_G)
    bs = np.kron(eye, np.sin(ang_c)) / math.sqrt(FOUR_G)
    s = np.arange(seq)
    ang_s = 2.0 * np.pi * ((s[:, None] * s[None, :]) % seq) / seq
    cs = np.cos(ang_s) / math.sqrt(seq)
    ss = np.sin(ang_s) / math.sqrt(seq)
    return tuple(jnp.asarray(a, dtype=F32) for a in (bc, bs, cs, ss))


def _fourier(f, seq):
    t = f.shape[0]
    bc, bs, cs, ss = _dft_consts(seq)
    const = lambda b: (0, 0)
    return pl.pallas_call(
        _fourier_kernel,
        grid=(t // seq,),
        in_specs=[pl.BlockSpec((seq, FOUR_W), lambda b: (b, 0)),
                  pl.BlockSpec((FOUR_W, FOUR_W), const),
                  pl.BlockSpec((FOUR_W, FOUR_W), const),
                  pl.BlockSpec((seq, seq), const),
                  pl.BlockSpec((seq, seq), const)],
        out_specs=pl.BlockSpec((seq, FOUR_W), lambda b: (b, 0)),
        out_shape=jax.ShapeDtypeStruct((t, FOUR_W), BF16),
        compiler_params=_cparams(("arbitrary",)),
        name="fourier_mix",
    )(f, bc, bs, cs, ss)


def _post_kernel(o_ref, fm_ref, ga_ref, gf_ref, x_ref, mod_ref, gpost_ref, gffn_ref,
                 wpa_ref, wpf_ref, wout_ref, wrx_ref,
                 x1_ref, h2t_ref, afft_ref):
    m = mod_ref[0]
    gt1 = m[:, 2 * D_MODEL:3 * D_MODEL]
    sh2 = m[:, 3 * D_MODEL:4 * D_MODEL]
    sc2 = m[:, 4 * D_MODEL:5 * D_MODEL]
    subs = [slice(r0, r0 + POST_SUB) for r0 in range(0, o_ref.shape[0], POST_SUB)]
    ab = [(_dot(o_ref[rs, :], wpa_ref[...]), _dot(fm_ref[rs, :], wpf_ref[...])) for rs in subs]
    merged = [(ga_ref[rs, :] * a + gf_ref[rs, :] * b).astype(BF16) for rs, (a, b) in zip(subs, ab)]
    ys = [_dot(mg, wout_ref[...]) for mg in merged]
    h2s = []
    for rs, y in zip(subs, ys):
        x1 = x_ref[rs, :] + gt1 * _rms(y, gpost_ref[...], EPS)
        x1_ref[rs, :] = x1
        h2s.append(_rms(x1, gffn_ref[...], EPS) * (1.0 + sc2) + sh2)
    logits = [_dot(h2.astype(BF16), wrx_ref[...]) for h2 in h2s]
    for rs, h2, lg in zip(subs, h2s, logits):
        lt = lg.T[0:N_EXPERTS]
        et = jnp.exp(lt - jnp.max(lt, axis=0, keepdims=True))
        aff = et / jnp.sum(et, axis=0, keepdims=True)
        for u in range(POST_SUB // LANES):
            afft_ref[:, rs.start // LANES + u, :] = aff[:, u * LANES:(u + 1) * LANES]
        for kc in range(ROW_TILE):
            h2t_ref[pl.ds(rs.start * ROW_TILE + kc, POST_SUB, stride=ROW_TILE), :] = h2[:, kc * LANES:(kc + 1) * LANES]


def _post_mixer(o, fm, ga, gf, x2d, mod3, mod_row, g_post, g_ffn, wpa, wpf, wout, wrx):
    t = x2d.shape[0]
    tm = POST_BLOCK
    row = lambda i: (i, 0)
    const = lambda i: (0, 0)
    return pl.pallas_call(
        _post_kernel,
        grid=(t // tm,),
        in_specs=[pl.BlockSpec((tm, QK_W), row),
                  pl.BlockSpec((tm, FOUR_W), row),
                  pl.BlockSpec((tm, D_MODEL), row),
                  pl.BlockSpec((tm, D_MODEL), row),
                  pl.BlockSpec((tm, D_MODEL), row),
                  pl.BlockSpec((1, 1, N_MOD * D_MODEL), lambda i: (mod_row(i, tm), 0, 0)),
                  pl.BlockSpec((1, D_MODEL), const),
                  pl.BlockSpec((1, D_MODEL), const),
                  pl.BlockSpec((QK_W, D_MODEL), const),
                  pl.BlockSpec((FOUR_W, D_MODEL), const),
                  pl.BlockSpec((D_MODEL, D_MODEL), const),
                  pl.BlockSpec((D_MODEL, LANES), const)],
        out_specs=[pl.BlockSpec((tm, D_MODEL), row),
                   pl.BlockSpec((tm * ROW_TILE, LANES), row),
                   pl.BlockSpec((N_EXPERTS, tm // LANES, LANES), lambda i: (0, i, 0))],
        out_shape=[jax.ShapeDtypeStruct((t, D_MODEL), F32),
                   jax.ShapeDtypeStruct((t * ROW_TILE, LANES), F32),
                   jax.ShapeDtypeStruct((N_EXPERTS, t // LANES, LANES), F32)],
        compiler_params=_cparams(("arbitrary",)),
        name="post_mixer",
    )(o, fm, ga, gf, x2d, mod3, g_post, g_ffn, wpa, wpf, wout, wrx)


RANGE_ROWS = 8


def _route_kernel(aff_ref, posm_ref, pack_ref, span_ref, ranges_ref, rows_ref, *, cap, n_tok):
    aff = aff_ref[...]
    nt = n_tok // LANES
    capf = float(cap)

    def count_ge(v):
        return jnp.sum(jnp.where(aff >= v, 1.0, 0.0), axis=(1, 2), keepdims=True)

    def search(i, thr):
        cand = thr | jnp.left_shift(jnp.int32(1), 30 - i)
        return jnp.where(count_ge(pltpu.bitcast(cand, F32)) >= capf, cand, thr)

    thr = lax.fori_loop(0, 31, search, jnp.zeros((N_EXPERTS, 1, 1), I32))
    lo = pltpu.bitcast(thr, F32)
    hi = pltpu.bitcast(thr + 1, F32)

    def refine(i, c):
        lo, hi = c
        mid = lo + (hi - lo) * 0.5
        ok = count_ge(mid) >= capf
        return jnp.where(ok, mid, lo), jnp.where(ok, hi, mid)

    lo, hi = lax.fori_loop(0, 12, refine, (lo, hi))
    gt = aff >= hi
    eq = (aff >= lo) & (aff < hi)
    n_tie = capf - jnp.sum(jnp.where(gt, 1.0, 0.0), axis=(1, 2), keepdims=True)

    sq0 = lax.broadcasted_iota(I32, (LANES, LANES), 0)
    sq1 = lax.broadcasted_iota(I32, (LANES, LANES), 1)
    along_total = jnp.concatenate([jnp.where(sq0 <= sq1, 1.0, 0.0), jnp.ones((LANES, LANES), F32)],
                                  axis=1).astype(BF16)
    m = N_EXPERTS * nt
    r0 = lax.broadcasted_iota(I32, (m, m), 0)
    r1 = lax.broadcasted_iota(I32, (m, m), 1)
    earlier = jnp.where((r0 // nt == r1 // nt) & (r1 < r0), 1.0, 0.0).astype(BF16)
    lane = lax.broadcasted_iota(I32, (1, LANES), 1)
    token = lax.broadcasted_iota(I32, (nt, LANES), 0) * LANES + lane

    def tile_counts(x):
        both = _dot(x.reshape(m, LANES).astype(BF16), along_total)
        total = both[:, LANES:]
        before = _dot(earlier, total.astype(BF16))
        shape = (N_EXPERTS, nt, LANES)
        return both[:, :LANES].reshape(shape), total.reshape(shape), before.reshape(shape)

    eq_f = jnp.where(eq, 1.0, 0.0)
    eq_along, _, eq_before = tile_counts(eq_f)
    sel = jnp.where(gt, 1.0, jnp.where(eq_along + eq_before <= n_tie, eq_f, 0.0))
    sel_along, sel_total, sel_before = tile_counts(sel)
    posm_ref[...] = jnp.where(sel > 0.5, sel_along + sel_before - sel, -1.0).astype(I32)

    cnt = jnp.sum(sel, axis=0)
    rows_before = jnp.sum(sel_before, axis=0)
    tok_start = _dot(cnt.astype(BF16), along_total[:, :LANES]) - cnt + rows_before
    k = jnp.zeros((nt, LANES), F32)
    slots_before = jnp.zeros((nt, LANES), F32)
    slots_here = jnp.zeros((nt, LANES), F32)
    for e in range(N_EXPERTS):
        pack_ref[e] = (tok_start + k).astype(I32) * (1 << TOKEN_BITS) + token
        k = k + sel[e]
        slots_before = jnp.where(lane == e, sel_before[e], slots_before)
        slots_here = jnp.where(lane == e, sel_total[e], slots_here)

    eye = jnp.where(sq0 == sq1, 1.0, 0.0).astype(BF16)
    tok_end = tok_start + cnt
    parts = []
    for v in (tok_start, tok_end):
        high = jnp.floor(v * (1.0 / LANES))
        parts += [high, v - high * LANES]
    for c in range(nt):
        rows4 = jnp.zeros((LANES, LANES), F32)
        for r, part in enumerate(parts):
            rows4 = jnp.where(sq0 == r, jnp.broadcast_to(part[c:c + 1, :], (LANES, LANES)), rows4)
        span_ref[c * LANES:(c + 1) * LANES, :] = lax.dot_general(
            eye, rows4.astype(BF16), (((1,), (1,)), ((), ())), preferred_element_type=F32)

    ranges_ref[...] = jnp.zeros_like(ranges_ref)
    for s in range(cap // LANES):
        done = jnp.where(slots_before + slots_here <= float(s * LANES), 1.0, 0.0)
        begun = jnp.where(slots_before < float((s + 1) * LANES), 1.0, 0.0)
        ranges_ref[s:s + 1, :] = jnp.sum(done, axis=0, keepdims=True).astype(I32)
        ranges_ref[RANGE_ROWS + s:RANGE_ROWS + s + 1, :] = jnp.sum(begun, axis=0, keepdims=True).astype(I32)
    rows_ref[...] = rows_before.astype(I32)


def _route(aff, cap):
    n_e, nt, _ = aff.shape
    n_tok = nt * LANES
    assert cap // LANES <= RANGE_ROWS
    return pl.pallas_call(
        functools.partial(_route_kernel, cap=cap, n_tok=n_tok),
        out_shape=[jax.ShapeDtypeStruct((n_e, nt, LANES), I32),
                   jax.ShapeDtypeStruct((n_e, nt, LANES), I32),
                   jax.ShapeDtypeStruct((n_tok, LANES), F32),
                   jax.ShapeDtypeStruct((2 * RANGE_ROWS, LANES), I32),
                   jax.ShapeDtypeStruct((nt, LANES), I32)],
        compiler_params=pltpu.CompilerParams(vmem_limit_bytes=VMEM_LIMIT),
        name="route",
    )(aff)


def _slots_kernel(ranges_ref, posm_ref, pack_ref, idx_ref, qslot_ref, acc_ref):
    e = pl.program_id(0)
    sub = lax.broadcasted_iota(I32, (LANES, LANES), 0)
    eye = sub == lax.broadcasted_iota(I32, (LANES, LANES), 1)
    n_tiles = idx_ref.shape[1]

    for s in range(n_tiles):
        slot = sub + s * LANES

        def body(c, acc, slot=slot):
            hit = posm_ref[e, pl.ds(c, 1), :] == slot
            return acc + jnp.where(hit, pack_ref[e, pl.ds(c, 1), :], 0)

        acc_ref[s] = lax.fori_loop(ranges_ref[s, e], ranges_ref[RANGE_ROWS + s, e], body,
                                   jnp.zeros((LANES, LANES), I32))

    def as_row(part):
        col = jnp.sum(part.astype(F32), axis=1, keepdims=True)
        return jnp.sum(jnp.where(eye, col, 0.0), axis=0, keepdims=True).astype(I32)

    for s in range(n_tiles):
        acc = acc_ref[s]
        idx_ref[0, s:s + 1, :] = as_row(acc & ((1 << TOKEN_BITS) - 1)) * ROW_TILE
        qslot_ref[0, s:s + 1, :] = as_row(lax.shift_right_logical(acc, TOKEN_BITS)) * ROW_TILE


def _slot_lists(ranges, posm4, qdst4, cap):
    ns = cap // LANES
    grid_spec = pltpu.PrefetchScalarGridSpec(
        num_scalar_prefetch=1,
        grid=(N_EXPERTS,),
        in_specs=[_whole_vmem(), _whole_vmem()],
        out_specs=[pl.BlockSpec((1, ns, LANES), lambda e, *_: (e, 0, 0)),
                   pl.BlockSpec((1, ns, LANES), lambda e, *_: (e, 0, 0))],
        scratch_shapes=[pltpu.VMEM((ns, LANES, LANES), I32)])
    idx, qslot = pl.pallas_call(
        _slots_kernel,
        grid_spec=grid_spec,
        out_shape=[jax.ShapeDtypeStruct((N_EXPERTS, ns, LANES), I32),
                   jax.ShapeDtypeStruct((N_EXPERTS, ns, LANES), I32)],
        compiler_params=_cparams(("arbitrary",)),
        name="slot_lists",
    )(ranges, posm4, qdst4)
    return idx.reshape(N_EXPERTS, cap), qslot.reshape(N_EXPERTS, cap)


def _moe_kernel(idxc_ref, idxs_ref, qc_ref, qs_ref,
                hc_ref, hs_ref, wr_ref, wg_ref, wu_ref, wd_ref, zc_ref, zs_ref,
                xbuf, ybuf, xb_ref, gate_ref, acc_ref, gsem, ssem, *, capc, caps):
    e = pl.program_id(0)
    j = pl.program_id(1)
    n_e = pl.num_programs(0)
    n_j = FF_STEPS
    slot = e % 2
    other = 1 - slot
    rows = capc + caps
    gc, gs = _per_step(capc), _per_step(caps)
    groups = ((hc_ref, idxc_ref, zc_ref, qc_ref, gc, 0), (hs_ref, idxs_ref, zs_ref, qs_ref, gs, gc * n_j))

    def tile(ref, first_sublane):
        return ref.at[pl.ds(pl.multiple_of(first_sublane, ROW_TILE), ROW_TILE), :]

    def gather(ex, sl, step, i, group):
        h_ref, idx_ref, _, _, per_step, base = group
        p = step * per_step + i
        src = tile(h_ref, idx_ref[ex * (per_step * n_j) + p])
        pltpu.make_async_copy(src, xbuf.at[sl, :, base + p, :], gsem.at[sl]).start()

    def scatter(table_row, sl, step, i, group):
        _, _, z_ref, q_ref, per_step, base = group
        p = step * per_step + i
        dst = tile(z_ref, q_ref[table_row * (per_step * n_j) + p])
        pltpu.make_async_copy(ybuf.at[sl, :, base + p, :], dst, ssem.at[sl]).start()

    def all_steps(fn):
        for group in groups:
            def body(p, carry, group=group):
                fn(p, group)
                return carry
            lax.fori_loop(0, group[4] * n_j, body, 0, unroll=8)

    def wait_all(buf, sem, sl):
        pltpu.make_async_copy(buf.at[sl], buf.at[sl], sem.at[sl]).wait()

    @pl.when((e == 0) & (j == 0))
    def _():
        ybuf[...] = jnp.zeros_like(ybuf)
        all_steps(lambda p, group: gather(0, 0, 0, p, group))

    @pl.when(j == 0)
    def _():
        wait_all(xbuf, gsem, slot)
        for base, n, dst in ((0, capc, 0), (gc * n_j, caps, capc)):
            for kc in range(ROW_TILE):
                xb_ref[dst:dst + n, kc * LANES:(kc + 1) * LANES] = xbuf[slot, kc, base:base + n, :].astype(BF16)
        acc_ref[...] = jnp.zeros_like(acc_ref)
        logits = _dot(xb_ref[...], wr_ref[...])
        lane = lax.broadcasted_iota(I32, (1, LANES), 1)
        is_expert = lane < N_EXPERTS
        ex = jnp.exp(logits - jnp.max(jnp.where(is_expert, logits, -jnp.inf), axis=-1, keepdims=True))
        mine = jnp.sum(jnp.where(lane == e, ex, 0.0), axis=-1, keepdims=True)
        gate = mine / jnp.sum(jnp.where(is_expert, ex, 0.0), axis=-1, keepdims=True)
        gate_ref[...] = jnp.broadcast_to(gate, gate_ref.shape)

    nxt = jnp.minimum(e + 1, n_e - 1)
    for group in groups:
        for i in range(group[4]):
            gather(nxt, other, j, i, group)
            scatter(e, other, j, i, group)

    x = xb_ref[...]
    g = _dot(x, wg_ref[0].astype(BF16))
    u = _dot(x, wu_ref[0].astype(BF16))
    hid = (g * jax.nn.sigmoid(g) * u).astype(BF16)
    acc_ref[...] += _dot(hid, wd_ref[0].astype(BF16))

    @pl.when(j == n_j - 1)
    def _():
        @pl.when(e >= 1)
        def _():
            wait_all(ybuf, ssem, slot)

        for base, n, src in ((0, capc, 0), (gc * n_j, caps, capc)):
            for r0 in range(0, n, LANES):
                nr = min(LANES, n - r0)
                gate = gate_ref[src + r0:src + r0 + nr, :]
                for kc in range(ROW_TILE):
                    y = acc_ref[src + r0:src + r0 + nr, kc * LANES:(kc + 1) * LANES] * gate
                    ybuf[slot, kc, base + r0:base + r0 + nr, :] = y

        @pl.when(e == n_e - 1)
        def _():
            all_steps(lambda p, group: scatter(e + 1, slot, 0, p, group))
            wait_all(ybuf, ssem, other)
            wait_all(ybuf, ssem, slot)
            wait_all(xbuf, gsem, other)


def _per_step(cap):
    return -(-cap // FF_STEPS)


def _copy_tables(idx, qslot, n_rows):
    n_e, cap = idx.shape
    padded = _per_step(cap) * FF_STEPS
    n_pad = padded - cap
    idx_p = jnp.concatenate([idx, jnp.zeros((n_e, n_pad), I32)], axis=1)
    spare = n_rows + jnp.arange(padded + n_e * n_pad, dtype=I32)
    lead = spare[:padded][None, :]
    pad_rows = spare[padded:].reshape(n_e, n_pad)
    q_p = jnp.concatenate([lead, jnp.concatenate([qslot // ROW_TILE, pad_rows], axis=1)], axis=0) * ROW_TILE
    return idx_p.reshape(-1), q_p.reshape(-1), n_rows + padded + n_e * n_pad


def _moe(idxc, idxs, qc, qs, hc, hs, wrx, w_gate, w_up, w_down):
    capc, caps = idxc.shape[1], idxs.shape[1]
    rows = capc + caps
    tf = FF_TILE
    idxc, qc, zc_rows = _copy_tables(idxc, qc, N_EXPERTS * capc)
    idxs, qs, zs_rows = _copy_tables(idxs, qs, N_EXPERTS * caps)
    buf_rows = (_per_step(capc) + _per_step(caps)) * FF_STEPS
    any_spec = pl.BlockSpec(memory_space=pl.ANY)
    grid_spec = pltpu.PrefetchScalarGridSpec(
        num_scalar_prefetch=4,
        grid=(N_EXPERTS, FF_STEPS),
        in_specs=[any_spec, any_spec,
                  pl.BlockSpec((D_MODEL, LANES), lambda e, j, *_: (0, 0)),
                  pl.BlockSpec((1, D_MODEL, tf), lambda e, j, *_: (e, 0, j)),
                  pl.BlockSpec((1, D_MODEL, tf), lambda e, j, *_: (e, 0, j)),
                  pl.BlockSpec((1, tf, D_MODEL), lambda e, j, *_: (e, j, 0))],
        out_specs=[any_spec, any_spec],
        scratch_shapes=[pltpu.VMEM((2, ROW_TILE, buf_rows, LANES), F32),
                        pltpu.VMEM((2, ROW_TILE, buf_rows, LANES), F32),
                        pltpu.VMEM((rows, D_MODEL), BF16),
                        pltpu.VMEM((rows, LANES), F32),
                        pltpu.VMEM((rows, D_MODEL), F32),
                        pltpu.SemaphoreType.DMA((2,)),
                        pltpu.SemaphoreType.DMA((2,))])
    return pl.pallas_call(
        functools.partial(_moe_kernel, capc=capc, caps=caps),
        grid_spec=grid_spec,
        out_shape=[jax.ShapeDtypeStruct((zc_rows * ROW_TILE, LANES), F32),
                   jax.ShapeDtypeStruct((zs_rows * ROW_TILE, LANES), F32)],
        compiler_params=_cparams(("arbitrary", "arbitrary")),
        name="expert_ffn",
    )(idxc, idxs, qc, qs, hc, hs, wrx, w_gate, w_up, w_down)


Z_BUFFERS = 3


def _combine_kernel(clo_ref, chi_ref, z_ref, span_ref, x1_ref, mod_ref, g_ref, o_ref,
                    zbuf, acc_ref, sems, state, *, n_chunks):
    b = pl.program_id(0)
    chunk_rows = ROW_CHUNK * ROW_TILE

    @pl.when(b == 0)
    def _():
        state[0] = 0
        state[1] = 0

    def chunk_copy(c):
        src = z_ref.at[pl.ds(pl.multiple_of(c * chunk_rows, chunk_rows), chunk_rows), :]
        return pltpu.make_async_copy(src, zbuf.at[c % Z_BUFFERS], sems.at[c % Z_BUFFERS])

    acc_ref[...] = jnp.zeros_like(acc_ref)
    first_row = span_ref[:, 0:1] * LANES + span_ref[:, 1:2]
    end_row = span_ref[:, 2:3] * LANES + span_ref[:, 3:4]
    lane_row = lax.broadcasted_iota(I32, (TOK_BLOCK, ROW_CHUNK), 1)

    def body(c, carry):
        for _ in range(Z_BUFFERS):
            nxt = state[0]

            @pl.when(nxt <= jnp.minimum(c + Z_BUFFERS - 1, n_chunks - 1))
            def _():
                chunk_copy(nxt).start()
                state[0] = nxt + 1

        @pl.when(state[1] <= c)
        def _():
            chunk_copy(c).wait()
            state[1] = c + 1

        slot = c % Z_BUFFERS
        row = (c * ROW_CHUNK + lane_row).astype(F32)
        onehot = jnp.where((row >= first_row) & (row < end_row), 1.0, 0.0).astype(BF16)
        y = jnp.concatenate([zbuf[slot, pl.ds(kc, ROW_CHUNK, stride=ROW_TILE), :] for kc in range(ROW_TILE)],
                            axis=1)
        acc_ref[...] += _dot(onehot, y.astype(BF16))
        return carry

    lax.fori_loop(clo_ref[b], chi_ref[b], body, 0)
    gt2 = mod_ref[0][:, 5 * D_MODEL:6 * D_MODEL]
    o_ref[...] = x1_ref[...] + gt2 * _rms(acc_ref[...], g_ref[...], EPS)


def _combine(ranges, z, span, x1, mod3, mod_row, g_post_ffn):
    clo, chi = ranges
    t = x1.shape[0]
    tb = TOK_BLOCK
    grid_spec = pltpu.PrefetchScalarGridSpec(
        num_scalar_prefetch=2,
        grid=(t // tb,),
        in_specs=[pl.BlockSpec(memory_space=pl.ANY),
                  pl.BlockSpec((tb, LANES), lambda b, *_: (b, 0)),
                  pl.BlockSpec((tb, D_MODEL), lambda b, *_: (b, 0)),
                  pl.BlockSpec((1, 1, N_MOD * D_MODEL), lambda b, *_: (mod_row(b, tb), 0, 0)),
                  pl.BlockSpec((1, D_MODEL), lambda b, *_: (0, 0))],
        out_specs=pl.BlockSpec((tb, D_MODEL), lambda b, *_: (b, 0)),
        scratch_shapes=[pltpu.VMEM((Z_BUFFERS, ROW_CHUNK * ROW_TILE, LANES), F32),
                        pltpu.VMEM((tb, D_MODEL), F32),
                        pltpu.SemaphoreType.DMA((Z_BUFFERS,)),
                        pltpu.SMEM((2,), I32)])
    return pl.pallas_call(
        functools.partial(_combine_kernel, n_chunks=2 * t // ROW_CHUNK),
        grid_spec=grid_spec,
        out_shape=jax.ShapeDtypeStruct((t, D_MODEL), F32),
        compiler_params=_cparams(("arbitrary",)),
        name="combine",
    )(clo, chi, z, span, x1, mod3, g_post_ffn)


def _rope_tables(seq):
    half = HEAD_DIM // 4
    freqs = ROPE_THETA ** (-np.arange(half, dtype=np.float64) / half)
    s = np.arange(seq)
    row = (s // GRID_W)[:, None] * freqs[None, :]
    col = (s % GRID_W)[:, None] * freqs[None, :]
    ang = np.concatenate([row, row, col, col], axis=1)
    ang = np.tile(ang, (1, QK_W // HEAD_DIM))
    lane = np.arange(QK_W)
    sign = np.where((lane % 32) < 16, -1.0, 1.0)[None, :]
    return (jnp.asarray(np.cos(ang), dtype=F32), jnp.asarray(np.sin(ang) * sign, dtype=F32))


def _combine_ranges(rows, n_tok):
    step = TOK_BLOCK // LANES
    nb = n_tok // TOK_BLOCK
    lo = rows[0:nb * step:step, 0]
    hi = jnp.concatenate([lo[1:], jnp.full((1,), 2 * n_tok, I32)])
    return (lo // ROW_CHUNK).astype(I32), ((hi + ROW_CHUNK - 1) // ROW_CHUNK).astype(I32)


def kernel(x_prompt, x_sample, c, cache_k, cache_v, c_ctx, w_mod, b_mod, g_pre_mix, g_post_mix, g_pre_ffn, g_post_ffn, w_in, lam_q1, lam_k1, lam_q2, lam_k2, g_subln, w_proj_attn, w_proj_fourier, w_out, w_router, w_gate, w_up, w_down):
    assert w_mod.shape[0] == 1
    lam_init = 0.8 - 0.6 * math.exp(-0.3 * 0)
    bp, sp, _ = x_prompt.shape
    bs, ss, _ = x_sample.shape

    cond8 = jnp.concatenate([c_ctx[None, :], c, jnp.zeros((8 - 1 - bs, D_MODEL), F32)], axis=0)
    mod3 = _modulation(cond8, w_mod[0], b_mod).reshape(8, 1, N_MOD * D_MODEL)

    w_in_b = None
    wpa =w_proj_attn[0].astype(BF16)
    wpf = w_proj_fourier[0].astype(BF16)
    wout = w_out[0].astype(BF16)
    wr = w_router[0].astype(BF16)
    wrx = jnp.concatenate([wr, jnp.zeros((D_MODEL, LANES - N_EXPERTS), BF16)], axis=1)
    lam_p = jnp.concatenate([lam_q1, lam_k1, lam_q2, lam_k2], axis=0)

    groups = []
    for x, seq, positional, ctx in ((x_sample, ss, True, (cache_k, cache_v)),
                                    (x_prompt, sp, False, None)):
        nb = x.shape[0]
        t = nb * seq
        x2d = x.reshape(t, D_MODEL)
        if positional:
            mod_row = lambda i, tm, seq=seq: 1 + (i * tm) // seq
        else:
            mod_row = lambda i, tm: 0
        self_contained = ctx is None and not positional
        if self_contained:
            assert PRE_BLOCK % seq == 0
            pre = _ctx_mixer(x2d, mod3, mod_row, g_pre_mix, w_in_b, seq, (lam_p, g_subln, lam_init), tm=PRE_BLOCK)
            o, fm, ga, gf = pre[:4]
        else:
            q, k, v, f, ga, gf, w_in_b = _pre_mixer(x2d, mod3, mod_row, g_pre_mix, w_in[0],
                                                    _rope_tables(seq) if positional else None, seq, tm=ROW_BLOCK)
            o = _attention(lam_p, g_subln, q, k, v, ctx, seq, lam_init)
            fm = _fourier(f, seq)
        x1, h2t, aff_t = _post_mixer(o, fm, ga, gf, x2d, mod3, mod_row, g_post_mix, g_pre_ffn,
                                     wpa, wpf, wout, wrx)
        cap = 2 * t // N_EXPERTS
        assert t <= 1 << TOKEN_BITS
        posm, pack, span, ranges, rows = _route(aff_t, cap)
        idx, qslot = _slot_lists(ranges, posm, pack, cap)
        groups.append(dict(x1=x1, h2t=h2t, idx=idx, qslot=qslot, span=span, ranges=_combine_ranges(rows, t),
                           mod_row=mod_row, cache=pre[4:] if self_contained else None, shape=x.shape))

    gs_, gc = groups
    zc, zs = _moe(gc["idx"], gs_["idx"], gc["qslot"], gs_["qslot"], gc["h2t"], gs_["h2t"], wrx,
                  w_gate[0], w_up[0], w_down[0])
    outs = []
    for g, z in ((gc, zc), (gs_, zs)):
        out = _combine(g["ranges"], z, g["span"], g["x1"], mod3, g["mod_row"], g_post_ffn)
        outs.append(out.reshape(g["shape"]))
    new_k, new_v = gc["cache"]
    return (outs[0], outs[1], new_k, new_v)
```

```python
import functools
import math

import numpy as np
import jax
import jax.numpy as jnp
from jax import lax
from jax.experimental import pallas as pl
from jax.experimental.pallas import tpu as pltpu

F32 = jnp.float32
BF16 = jnp.bfloat16
I32 = jnp.int32

D_MODEL = 1024
N_HEADS = 6
HEAD_DIM = 64
V_DIM = 128
QK_W = 768
FOUR_W = 256
FOUR_G = 64
IN_W = 4608
N_EXPERTS = 16
D_FF = 2816
N_MOD = 6
GRID_W = 64
ROPE_THETA = 10000.0
EPS = 1e-6
SUBLN_EPS = 1e-5

LANES = 128
ROW_BLOCK = 256
PRE_BLOCK = 512
GATE_CHUNK = 256
POST_BLOCK = 1024
POST_SUB = 256
WEIGHT_ROWS = 128
TOK_BLOCK = 256
ROW_CHUNK = 256
FF_TILE = 256
FF_STEPS = D_FF // FF_TILE
ROW_TILE = D_MODEL // LANES
TOKEN_BITS = 13
VMEM_LIMIT = 56 * 1024 * 1024


def _cparams(sem):
    return pltpu.CompilerParams(dimension_semantics=sem, vmem_limit_bytes=VMEM_LIMIT)


def _dot(a, b):
    return jnp.dot(a, b, preferred_element_type=F32)


def _rms(x, g, eps):
    return x * lax.rsqrt(jnp.mean(x * x, axis=-1, keepdims=True) + eps) * g


def _whole_vmem():
    return pl.BlockSpec(memory_space=pltpu.MemorySpace.VMEM)


def _cast_weight(w_hbm, wb_ref, stage_ref, sem):
    rows = stage_ref.shape[1]
    chunks = [pltpu.make_async_copy(w_hbm.at[pl.ds(r0, rows), :], stage_ref.at[n % 2], sem.at[n % 2])
              for n, r0 in enumerate(range(0, w_hbm.shape[0], rows))]
    chunks[0].start()
    for n, chunk in enumerate(chunks):
        if n + 1 < len(chunks):
            chunks[n + 1].start()
        chunk.wait()
        wb_ref[n * rows:(n + 1) * rows, :] = stage_ref[n % 2].astype(BF16)


def _mod_kernel(c_ref, w_ref, b_ref, o_ref):
    c = c_ref[...]
    s = c * jax.nn.sigmoid(c)
    o_ref[...] = _dot(s.astype(BF16), w_ref[...].astype(BF16)) + b_ref[...]


def _modulation(cond8, w_mod, b_mod):
    tn = 1024
    n = N_MOD * D_MODEL
    return pl.pallas_call(
        _mod_kernel,
        grid=(n // tn,),
        in_specs=[pl.BlockSpec((8, D_MODEL), lambda j: (0, 0)),
                  pl.BlockSpec((D_MODEL, tn), lambda j: (0, j)),
                  pl.BlockSpec((1, tn), lambda j: (0, j))],
        out_specs=pl.BlockSpec((8, tn), lambda j: (0, j)),
        out_shape=jax.ShapeDtypeStruct((8, n), F32),
        compiler_params=_cparams(("arbitrary",)),
        name="modulation",
    )(cond8, w_mod, b_mod)


def _diff_lambda(lp, lam_init):
    s1 = jnp.sum(lp[0:1] * lp[1:2], axis=-1, keepdims=True)
    s2 = jnp.sum(lp[2:3] * lp[3:4], axis=-1, keepdims=True)
    return jnp.exp(s1) - jnp.exp(s2) + lam_init


def _attention_weights(q, k):
    comp1 = lax.broadcasted_iota(I32, (1, V_DIM), 1) < HEAD_DIM
    qs = q * jnp.asarray(HEAD_DIM ** -0.5, BF16)
    zero = jnp.zeros_like(qs)

    def weights(qc):
        s = lax.dot_general(qc, k, (((1,), (1,)), ((), ())), preferred_element_type=F32)
        return jnp.exp(s - jnp.max(s, axis=-1, keepdims=True)).astype(BF16)

    return weights(jnp.where(comp1, qs, zero)), weights(jnp.where(comp1, zero, qs))


def _attention_output(weights, v, lam, g_subln, lam_init):
    v_ones = jnp.concatenate([v, jnp.ones_like(v)], axis=1)

    def attend(ex):
        ov = _dot(ex, v_ones)
        return ov[:, 0:V_DIM] / ov[:, V_DIM:2 * V_DIM]

    o = attend(weights[0]) - lam * attend(weights[1])
    return (_rms(o, g_subln, SUBLN_EPS) * (1.0 - lam_init)).astype(BF16)


def _rope(z, cos, sin_signed, first_half):
    fwd = pltpu.roll(z, QK_W - 16, axis=1)
    bwd = pltpu.roll(z, 16, axis=1)
    return z * cos + jnp.where(first_half, fwd, bwd) * sin_signed


def _pre_kernel(*refs, positional):
    it = iter(refs)
    x_ref, mod_ref, g_ref, w_hbm = next(it), next(it), next(it), next(it)
    if positional:
        cos_ref, sin_ref = next(it), next(it)
    q_ref, k_ref, v_ref, f_ref, ga_ref, gf_ref = (next(it) for _ in range(6))
    wb_hbm, w_ref, stage_ref, sem = next(it), next(it), next(it), next(it)
    weight_out = pltpu.make_async_copy(w_ref, wb_hbm, sem.at[2])

    @pl.when(pl.program_id(0) == 0)
    def _():
        _cast_weight(w_hbm, w_ref, stage_ref, sem)
        weight_out.start()

    @pl.when(pl.program_id(0) == pl.num_programs(0) - 1)
    def _():
        weight_out.wait()

    m = mod_ref[0]
    sh1 = m[:, 0:D_MODEL]
    sc1 = m[:, D_MODEL:2 * D_MODEL]
    h = _rms(x_ref[...], g_ref[...], EPS) * (1.0 + sc1) + sh1
    hb = h.astype(BF16)

    def proj(lo, hi):
        return _dot(hb, w_ref[:, lo:hi])

    zq = proj(0, QK_W)
    zk = proj(QK_W, 2 * QK_W)
    zv = proj(2 * QK_W, 3 * QK_W)
    if positional:
        lane = lax.broadcasted_iota(I32, (1, QK_W), 1)
        first_half = (lane % 32) < 16
        cos = cos_ref[...]
        sin_signed = sin_ref[...]
        zq = _rope(zq, cos, sin_signed, first_half)
        zk = _rope(zk, cos, sin_signed, first_half)
    q_ref[...] = zq.astype(BF16)
    k_ref[...] = zk.astype(BF16)
    v_ref[...] = zv.astype(BF16)
    f0 = 3 * QK_W
    g0 = f0 + FOUR_W
    f_ref[...] = proj(f0, g0)
    ga_ref[...] = jax.nn.sigmoid(proj(g0, g0 + D_MODEL)).astype(BF16)
    gf_ref[...] = jax.nn.sigmoid(proj(g0 + D_MODEL, IN_W)).astype(BF16)


def _ctx_kernel(x_ref, mod_ref, g_ref, w_ref, bc_ref, bs_ref, cs_ref, ss_ref, lam_ref, gs_ref,
                gpost_ref, gffn_ref, wpa_ref, wpf_ref, wout_ref, wrx_ref,
                kc_ref, vc_ref, x1_ref, h2t_ref, aff_ref,
                o_ref, f_ref, ga_ref, gf_ref, *, seq, lam_init):
    m = mod_ref[0]
    sh1 = m[:, 0:D_MODEL]
    sc1 = m[:, D_MODEL:2 * D_MODEL]
    hb = (_rms(x_ref[...], g_ref[...], EPS) * (1.0 + sc1) + sh1).astype(BF16)
    seqs = [slice(b * seq, (b + 1) * seq) for b in range(x_ref.shape[0] // seq)]
    lam = _diff_lambda(lam_ref[...], lam_init)
    f0 = 3 * QK_W
    g0 = f0 + FOUR_W

    def proj(lo, hi):
        return _dot(hb, w_ref[:, lo:hi])

    q, k, v = {}, {}, {}

    def qkv_chunk(store, cache_ref, lo, c):
        cols = slice(c * GATE_CHUNK, (c + 1) * GATE_CHUNK)
        z = proj(lo + cols.start, lo + cols.stop)
        for hd in range(cols.start // V_DIM, cols.stop // V_DIM):
            z_hd = z[:, hd * V_DIM - cols.start:(hd + 1) * V_DIM - cols.start]
            if cache_ref is not None:
                for b, rs in enumerate(seqs):
                    cache_ref[b, 0, hd] = z_hd[rs, :]
            store[hd] = z_hd.astype(BF16)

    def fourier_chunk():
        f = proj(f0, g0)
        for rs in seqs:
            f_ref[rs, :] = _dft_real(f[rs, :], bc_ref, bs_ref, cs_ref, ss_ref)

    def gate_chunk(ref, lo, c):
        cols = slice(c * GATE_CHUNK, (c + 1) * GATE_CHUNK)
        ref[:, cols] = jax.nn.sigmoid(proj(lo + cols.start, lo + cols.stop)).astype(BF16)

    for store, cache, lo in ((q, None, 0), (k, kc_ref, QK_W), (v, vc_ref, 2 * QK_W)):
        for c in range(QK_W // GATE_CHUNK):
            qkv_chunk(store, cache, lo, c)
    matmul_work = [fourier_chunk] + [functools.partial(gate_chunk, ref, lo, c)
                                     for ref, lo in ((ga_ref, g0), (gf_ref, g0 + D_MODEL))
                                     for c in range(D_MODEL // GATE_CHUNK)]
    for rs in seqs:
        for hd in range(N_HEADS):
            weights = _attention_weights(q[hd][rs, :], k[hd][rs, :])
            if matmul_work:
                matmul_work.pop(0)()
            o_ref[rs, hd * V_DIM:(hd + 1) * V_DIM] = _attention_output(weights, v[hd][rs, :], lam, gs_ref[...],
                                                                      lam_init)
    for work in matmul_work:
        work()

    def store_aff(rs, aff):
        aff_ref[0, :, rs] = aff

    _post_rows(o_ref, f_ref, ga_ref, gf_ref, x_ref, m, gpost_ref, gffn_ref,
               wpa_ref, wpf_ref, wout_ref, wrx_ref, x1_ref, h2t_ref, store_aff)


def _ctx_mixer(x2d, mod3, mod_row, g_pre, w_in_b, seq, attn_params, post_params, tm):
    t = x2d.shape[0]
    lam_p, g_subln, lam_init = attn_params
    g_post, g_ffn, wpa, wpf, wout, wrx = post_params
    assert tm % seq == 0 and t % tm == 0 and tm % POST_SUB == 0
    row = lambda i: (i, 0)
    const = lambda i: (0, 0)
    consts = _dft_consts(seq) + (lam_p, g_subln, g_post, g_ffn)
    nb = t // seq
    cshape = jax.ShapeDtypeStruct((nb, 1, N_HEADS, seq, V_DIM), F32)
    cspec = pl.BlockSpec((tm // seq, 1, N_HEADS, seq, V_DIM), lambda i: (i, 0, 0, 0, 0))
    new_k, new_v, x1, h2t, aff = pl.pallas_call(
        functools.partial(_ctx_kernel, seq=seq, lam_init=lam_init),
        grid=(t // tm,),
        in_specs=[pl.BlockSpec((tm, D_MODEL), row),
                  pl.BlockSpec((1, 1, N_MOD * D_MODEL), lambda i: (mod_row(i, tm), 0, 0)),
                  pl.BlockSpec((1, D_MODEL), const),
                  _whole_vmem()] + [pl.BlockSpec(c.shape, const) for c in consts] + [_whole_vmem()] * 4,
        out_specs=[cspec, cspec,
                   pl.BlockSpec((tm, D_MODEL), row),
                   pl.BlockSpec((tm * ROW_TILE, LANES), row),
                   pl.BlockSpec((1, N_EXPERTS, tm), lambda i: (i, 0, 0))],
        out_shape=[cshape, cshape,
                   jax.ShapeDtypeStruct((t, D_MODEL), F32),
                   jax.ShapeDtypeStruct((t * ROW_TILE, LANES), F32),
                   jax.ShapeDtypeStruct((t // tm, N_EXPERTS, tm), F32)],
        scratch_shapes=[pltpu.VMEM((tm, QK_W), BF16),
                        pltpu.VMEM((tm, FOUR_W), BF16),
                        pltpu.VMEM((tm, D_MODEL), BF16),
                        pltpu.VMEM((tm, D_MODEL), BF16)],
        compiler_params=_cparams(("arbitrary",)),
        name="ctx_mixer",
    )(x2d, mod3, g_pre, w_in_b, *consts, wpa, wpf, wout, wrx)
    aff = aff.transpose(1, 0, 2).reshape(N_EXPERTS, t // LANES, LANES)
    return new_k, new_v, x1, h2t, aff


def _pre_mixer(x2d, mod3, mod_row, g_pre, w_in, rope_tabs, seq, tm):
    t = x2d.shape[0]
    positional = rope_tabs is not None
    assert seq % tm == 0
    blocks_per_seq = seq // tm
    row = lambda i: (i, 0)
    any_spec = pl.BlockSpec(memory_space=pl.ANY)
    in_specs = [pl.BlockSpec((tm, D_MODEL), row),
                pl.BlockSpec((1, 1, N_MOD * D_MODEL), lambda i: (mod_row(i, tm), 0, 0)),
                pl.BlockSpec((1, D_MODEL), lambda i: (0, 0)),
                any_spec]
    args = [x2d, mod3, g_pre, w_in]
    if positional:
        in_specs += [pl.BlockSpec((tm, QK_W), lambda i: (i % blocks_per_seq, 0))] * 2
        args += list(rope_tabs)
    out_shape = [jax.ShapeDtypeStruct((t, QK_W), BF16)] * 3 + [
        jax.ShapeDtypeStruct((t, FOUR_W), F32),
        jax.ShapeDtypeStruct((t, D_MODEL), BF16),
        jax.ShapeDtypeStruct((t, D_MODEL), BF16),
        jax.ShapeDtypeStruct(w_in.shape, BF16)]
    out_specs = [pl.BlockSpec((tm, QK_W), row)] * 3 + [
        pl.BlockSpec((tm, FOUR_W), row),
        pl.BlockSpec((tm, D_MODEL), row),
        pl.BlockSpec((tm, D_MODEL), row),
        any_spec]
    return pl.pallas_call(
        functools.partial(_pre_kernel, positional=positional),
        grid=(t // tm,),
        in_specs=in_specs,
        out_specs=out_specs,
        out_shape=out_shape,
        scratch_shapes=[pltpu.VMEM(w_in.shape, BF16),
                        pltpu.VMEM((2, WEIGHT_ROWS, w_in.shape[1]), F32),
                        pltpu.SemaphoreType.DMA((3,))],
        compiler_params=_cparams(("arbitrary",)),
        name="pre_mixer",
    )(*args)


def _attn_kernel(*refs, lam_init, has_ctx):
    it = iter(refs)
    lam_ref, gs_ref, q_ref, k_ref, v_ref = (next(it) for _ in range(5))
    if has_ctx:
        ck_ref, cv_ref = next(it), next(it)
    o_ref = next(it)

    lam = _diff_lambda(lam_ref[...], lam_init)

    def operand(ref, cache_ref, hd):
        x = ref[:, hd * V_DIM:(hd + 1) * V_DIM]
        return jnp.concatenate([cache_ref[0, 0, hd].astype(BF16), x], axis=0) if has_ctx else x

    def head_weights(hd):
        return _attention_weights(q_ref[:, hd * V_DIM:(hd + 1) * V_DIM], operand(k_ref, ck_ref if has_ctx else None, hd))

    weights = head_weights(0)
    for hd in range(N_HEADS):
        nxt = head_weights(hd + 1) if hd + 1 < N_HEADS else None
        v = operand(v_ref, cv_ref if has_ctx else None, hd)
        o_ref[:, hd * V_DIM:(hd + 1) * V_DIM] = _attention_output(weights, v, lam, gs_ref[...], lam_init)
        weights = nxt


def _attention(lam_p, g_subln, q, k, v, ctx, seq, lam_init):
    t = q.shape[0]
    tq = ROW_BLOCK
    qb = seq // tq
    has_ctx = ctx is not None
    in_specs = [pl.BlockSpec((4, HEAD_DIM), lambda b, i: (0, 0)),
                pl.BlockSpec((1, V_DIM), lambda b, i: (0, 0)),
                pl.BlockSpec((tq, QK_W), lambda b, i: (b * qb + i, 0)),
                pl.BlockSpec((seq, QK_W), lambda b, i: (b, 0)),
                pl.BlockSpec((seq, QK_W), lambda b, i: (b, 0))]
    args = [lam_p, g_subln, q, k, v]
    if has_ctx:
        past = ctx[0].shape[3]
        cspec = pl.BlockSpec((1, 1, N_HEADS, past, V_DIM), lambda b, i: (b, 0, 0, 0, 0))
        in_specs += [cspec, cspec]
        args += list(ctx)
    return pl.pallas_call(
        functools.partial(_attn_kernel, lam_init=lam_init, has_ctx=has_ctx),
        grid=(t // seq, qb),
        in_specs=in_specs,
        out_specs=pl.BlockSpec((tq, QK_W), lambda b, i: (b * qb + i, 0)),
        out_shape=jax.ShapeDtypeStruct((t, QK_W), BF16),
        compiler_params=_cparams(("arbitrary", "arbitrary")),
        name="diff_attention",
    )(*args)


def _dft_real(f, bc_ref, bs_ref, cs_ref, ss_ref):
    fb = f.astype(BF16)
    u = _dot(fb, bc_ref[...].astype(BF16)).astype(BF16)
    w = _dot(fb, bs_ref[...].astype(BF16)).astype(BF16)
    return (_dot(cs_ref[...].astype(BF16), u) - _dot(ss_ref[...].astype(BF16), w)).astype(BF16)


def _fourier_kernel(f_ref, bc_ref, bs_ref, cs_ref, ss_ref, o_ref):
    o_ref[...] = _dft_real(f_ref[...], bc_ref, bs_ref, cs_ref, ss_ref)


def _dft_consts(seq):
    c = np.arange(FOUR_G)
    ang_c = 2.0 * np.pi * ((c[:, None] * c[None, :]) % FOUR_G) / FOUR_G
    eye = np.eye(FOUR_W // FOUR_G)
    bc = np.kron(eye, np.cos(ang_c)) / math.sqrt(FOUR_G)
    bs = np.kron(eye, np.sin(ang_c)) / math.sqrt(FOUR_G)
    s = np.arange(seq)
    ang_s = 2.0 * np.pi * ((s[:, None] * s[None, :]) % seq) / seq
    cs = np.cos(ang_s) / math.sqrt(seq)
    ss = np.sin(ang_s) / math.sqrt(seq)
    return tuple(jnp.asarray(a, dtype=F32) for a in (bc, bs, cs, ss))


def _fourier(f, seq):
    t = f.shape[0]
    bc, bs, cs, ss = _dft_consts(seq)
    const = lambda b: (0, 0)
    return pl.pallas_call(
        _fourier_kernel,
        grid=(t // seq,),
        in_specs=[pl.BlockSpec((seq, FOUR_W), lambda b: (b, 0)),
                  pl.BlockSpec((FOUR_W, FOUR_W), const),
                  pl.BlockSpec((FOUR_W, FOUR_W), const),
                  pl.BlockSpec((seq, seq), const),
                  pl.BlockSpec((seq, seq), const)],
        out_specs=pl.BlockSpec((seq, FOUR_W), lambda b: (b, 0)),
        out_shape=jax.ShapeDtypeStruct((t, FOUR_W), BF16),
        compiler_params=_cparams(("arbitrary",)),
        name="fourier_mix",
    )(f, bc, bs, cs, ss)


def _post_kernel(o_ref, fm_ref, ga_ref, gf_ref, x_ref, mod_ref, gpost_ref, gffn_ref,
                 wpa_ref, wpf_ref, wout_ref, wrx_ref,
                 x1_ref, h2t_ref, afft_ref):
    def store_aff(rs, aff):
        for u in range(POST_SUB // LANES):
            afft_ref[:, rs.start // LANES + u, :] = aff[:, u * LANES:(u + 1) * LANES]

    _post_rows(o_ref, fm_ref, ga_ref, gf_ref, x_ref, mod_ref[0], gpost_ref, gffn_ref,
               wpa_ref, wpf_ref, wout_ref, wrx_ref, x1_ref, h2t_ref, store_aff)


def _post_rows(o_ref, fm_ref, ga_ref, gf_ref, x_ref, m, gpost_ref, gffn_ref,
               wpa_ref, wpf_ref, wout_ref, wrx_ref, x1_ref, h2t_ref, store_aff):
    gt1 = m[:, 2 * D_MODEL:3 * D_MODEL]
    sh2 = m[:, 3 * D_MODEL:4 * D_MODEL]
    sc2 = m[:, 4 * D_MODEL:5 * D_MODEL]
    subs = [slice(r0, r0 + POST_SUB) for r0 in range(0, o_ref.shape[0], POST_SUB)]
    ab = [(_dot(o_ref[rs, :], wpa_ref[...]), _dot(fm_ref[rs, :], wpf_ref[...])) for rs in subs]
    merged = [(ga_ref[rs, :] * a + gf_ref[rs, :] * b).astype(BF16) for rs, (a, b) in zip(subs, ab)]
    ys = [_dot(mg, wout_ref[...]) for mg in merged]
    h2s = []
    for rs, y in zip(subs, ys):
        x1 = x_ref[rs, :] + gt1 * _rms(y, gpost_ref[...], EPS)
        x1_ref[rs, :] = x1
        h2s.append(_rms(x1, gffn_ref[...], EPS) * (1.0 + sc2) + sh2)
    logits = [_dot(h2.astype(BF16), wrx_ref[...]) for h2 in h2s]
    for rs, h2, lg in zip(subs, h2s, logits):
        lt = lg.T[0:N_EXPERTS]
        et = jnp.exp(lt - jnp.max(lt, axis=0, keepdims=True))
        store_aff(rs, et / jnp.sum(et, axis=0, keepdims=True))
        for kc in range(ROW_TILE):
            h2t_ref[pl.ds(rs.start * ROW_TILE + kc, POST_SUB, stride=ROW_TILE), :] = h2[:, kc * LANES:(kc + 1) * LANES]


def _post_mixer(o, fm, ga, gf, x2d, mod3, mod_row, g_post, g_ffn, wpa, wpf, wout, wrx):
    t = x2d.shape[0]
    tm = POST_BLOCK
    row = lambda i: (i, 0)
    const = lambda i: (0, 0)
    return pl.pallas_call(
        _post_kernel,
        grid=(t // tm,),
        in_specs=[pl.BlockSpec((tm, QK_W), row),
                  pl.BlockSpec((tm, FOUR_W), row),
                  pl.BlockSpec((tm, D_MODEL), row),
                  pl.BlockSpec((tm, D_MODEL), row),
                  pl.BlockSpec((tm, D_MODEL), row),
                  pl.BlockSpec((1, 1, N_MOD * D_MODEL), lambda i: (mod_row(i, tm), 0, 0)),
                  pl.BlockSpec((1, D_MODEL), const),
                  pl.BlockSpec((1, D_MODEL), const),
                  pl.BlockSpec((QK_W, D_MODEL), const),
                  pl.BlockSpec((FOUR_W, D_MODEL), const),
                  pl.BlockSpec((D_MODEL, D_MODEL), const),
                  pl.BlockSpec((D_MODEL, LANES), const)],
        out_specs=[pl.BlockSpec((tm, D_MODEL), row),
                   pl.BlockSpec((tm * ROW_TILE, LANES), row),
                   pl.BlockSpec((N_EXPERTS, tm // LANES, LANES), lambda i: (0, i, 0))],
        out_shape=[jax.ShapeDtypeStruct((t, D_MODEL), F32),
                   jax.ShapeDtypeStruct((t * ROW_TILE, LANES), F32),
                   jax.ShapeDtypeStruct((N_EXPERTS, t // LANES, LANES), F32)],
        compiler_params=_cparams(("arbitrary",)),
        name="post_mixer",
    )(o, fm, ga, gf, x2d, mod3, g_post, g_ffn, wpa, wpf, wout, wrx)


RANGE_ROWS = 8


def _route_kernel(aff_ref, posm_ref, pack_ref, span_ref, ranges_ref, rows_ref, *, cap, n_tok):
    aff = aff_ref[...]
    nt = n_tok // LANES
    capf = float(cap)

    def count_ge(v):
        return jnp.sum(jnp.where(aff >= v, 1.0, 0.0), axis=(1, 2), keepdims=True)

    def search(i, thr):
        cand = thr | jnp.left_shift(jnp.int32(1), 30 - i)
        return jnp.where(count_ge(pltpu.bitcast(cand, F32)) >= capf, cand, thr)

    thr = lax.fori_loop(0, 31, search, jnp.zeros((N_EXPERTS, 1, 1), I32))
    lo = pltpu.bitcast(thr, F32)
    hi = pltpu.bitcast(thr + 1, F32)

    def refine(i, c):
        lo, hi = c
        mid = lo + (hi - lo) * 0.5
        ok = count_ge(mid) >= capf
        return jnp.where(ok, mid, lo), jnp.where(ok, hi, mid)

    lo, hi = lax.fori_loop(0, 12, refine, (lo, hi))
    gt = aff >= hi
    eq = (aff >= lo) & (aff < hi)
    n_tie = capf - jnp.sum(jnp.where(gt, 1.0, 0.0), axis=(1, 2), keepdims=True)

    sq0 = lax.broadcasted_iota(I32, (LANES, LANES), 0)
    sq1 = lax.broadcasted_iota(I32, (LANES, LANES), 1)
    along_total = jnp.concatenate([jnp.where(sq0 <= sq1, 1.0, 0.0), jnp.ones((LANES, LANES), F32)],
                                  axis=1).astype(BF16)
    m = N_EXPERTS * nt
    r0 = lax.broadcasted_iota(I32, (m, m), 0)
    r1 = lax.broadcasted_iota(I32, (m, m), 1)
    earlier = jnp.where((r0 // nt == r1 // nt) & (r1 < r0), 1.0, 0.0).astype(BF16)
    lane = lax.broadcasted_iota(I32, (1, LANES), 1)
    token = lax.broadcasted_iota(I32, (nt, LANES), 0) * LANES + lane

    def tile_counts(x):
        both = _dot(x.reshape(m, LANES).astype(BF16), along_total)
        total = both[:, LANES:]
        before = _dot(earlier, total.astype(BF16))
        shape = (N_EXPERTS, nt, LANES)
        return both[:, :LANES].reshape(shape), total.reshape(shape), before.reshape(shape)

    eq_f = jnp.where(eq, 1.0, 0.0)
    eq_along, _, eq_before = tile_counts(eq_f)
    sel = jnp.where(gt, 1.0, jnp.where(eq_along + eq_before <= n_tie, eq_f, 0.0))
    sel_along, sel_total, sel_before = tile_counts(sel)
    posm_ref[...] = jnp.where(sel > 0.5, sel_along + sel_before - sel, -1.0).astype(I32)

    cnt = jnp.sum(sel, axis=0)
    rows_before = jnp.sum(sel_before, axis=0)
    tok_start = _dot(cnt.astype(BF16), along_total[:, :LANES]) - cnt + rows_before
    k = jnp.zeros((nt, LANES), F32)
    slots_before = jnp.zeros((nt, LANES), F32)
    slots_here = jnp.zeros((nt, LANES), F32)
    for e in range(N_EXPERTS):
        pack_ref[e] = (tok_start + k).astype(I32) * (1 << TOKEN_BITS) + token
        k = k + sel[e]
        slots_before = jnp.where(lane == e, sel_before[e], slots_before)
        slots_here = jnp.where(lane == e, sel_total[e], slots_here)

    eye = jnp.where(sq0 == sq1, 1.0, 0.0).astype(BF16)
    tok_end = tok_start + cnt
    parts = []
    for v in (tok_start, tok_end):
        high = jnp.floor(v * (1.0 / LANES))
        parts += [high, v - high * LANES]
    for c in range(nt):
        rows4 = jnp.zeros((LANES, LANES), F32)
        for r, part in enumerate(parts):
            rows4 = jnp.where(sq0 == r, jnp.broadcast_to(part[c:c + 1, :], (LANES, LANES)), rows4)
        span_ref[c * LANES:(c + 1) * LANES, :] = lax.dot_general(
            eye, rows4.astype(BF16), (((1,), (1,)), ((), ())), preferred_element_type=F32)

    ranges_ref[...] = jnp.zeros_like(ranges_ref)
    for s in range(cap // LANES):
        done = jnp.where(slots_before + slots_here <= float(s * LANES), 1.0, 0.0)
        begun = jnp.where(slots_before < float((s + 1) * LANES), 1.0, 0.0)
        ranges_ref[s:s + 1, :] = jnp.sum(done, axis=0, keepdims=True).astype(I32)
        ranges_ref[RANGE_ROWS + s:RANGE_ROWS + s + 1, :] = jnp.sum(begun, axis=0, keepdims=True).astype(I32)
    rows_ref[...] = rows_before.astype(I32)


def _route(aff, cap):
    n_e, nt, _ = aff.shape
    n_tok = nt * LANES
    assert cap // LANES <= RANGE_ROWS
    return pl.pallas_call(
        functools.partial(_route_kernel, cap=cap, n_tok=n_tok),
        out_shape=[jax.ShapeDtypeStruct((n_e, nt, LANES), I32),
                   jax.ShapeDtypeStruct((n_e, nt, LANES), I32),
                   jax.ShapeDtypeStruct((n_tok, LANES), F32),
                   jax.ShapeDtypeStruct((2 * RANGE_ROWS, LANES), I32),
                   jax.ShapeDtypeStruct((nt, LANES), I32)],
        compiler_params=pltpu.CompilerParams(vmem_limit_bytes=VMEM_LIMIT),
        name="route",
    )(aff)


def _slots_kernel(ranges_ref, posm_ref, pack_ref, idx_ref, qslot_ref, acc_ref):
    e = pl.program_id(0)
    sub = lax.broadcasted_iota(I32, (LANES, LANES), 0)
    eye = sub == lax.broadcasted_iota(I32, (LANES, LANES), 1)
    n_tiles = idx_ref.shape[1]

    for s in range(n_tiles):
        slot = sub + s * LANES

        def body(c, acc, slot=slot):
            hit = posm_ref[e, pl.ds(c, 1), :] == slot
            return acc + jnp.where(hit, pack_ref[e, pl.ds(c, 1), :], 0)

        acc_ref[s] = lax.fori_loop(ranges_ref[s, e], ranges_ref[RANGE_ROWS + s, e], body,
                                   jnp.zeros((LANES, LANES), I32))

    def as_row(part):
        col = jnp.sum(part.astype(F32), axis=1, keepdims=True)
        return jnp.sum(jnp.where(eye, col, 0.0), axis=0, keepdims=True).astype(I32)

    for s in range(n_tiles):
        acc = acc_ref[s]
        idx_ref[0, s:s + 1, :] = as_row(acc & ((1 << TOKEN_BITS) - 1)) * ROW_TILE
        qslot_ref[0, s:s + 1, :] = as_row(lax.shift_right_logical(acc, TOKEN_BITS)) * ROW_TILE


def _slot_lists(ranges, posm4, qdst4, cap):
    ns = cap // LANES
    grid_spec = pltpu.PrefetchScalarGridSpec(
        num_scalar_prefetch=1,
        grid=(N_EXPERTS,),
        in_specs=[_whole_vmem(), _whole_vmem()],
        out_specs=[pl.BlockSpec((1, ns, LANES), lambda e, *_: (e, 0, 0)),
                   pl.BlockSpec((1, ns, LANES), lambda e, *_: (e, 0, 0))],
        scratch_shapes=[pltpu.VMEM((ns, LANES, LANES), I32)])
    idx, qslot = pl.pallas_call(
        _slots_kernel,
        grid_spec=grid_spec,
        out_shape=[jax.ShapeDtypeStruct((N_EXPERTS, ns, LANES), I32),
                   jax.ShapeDtypeStruct((N_EXPERTS, ns, LANES), I32)],
        compiler_params=_cparams(("arbitrary",)),
        name="slot_lists",
    )(ranges, posm4, qdst4)
    return idx.reshape(N_EXPERTS, cap), qslot.reshape(N_EXPERTS, cap)


def _moe_kernel(idxc_ref, idxs_ref, qc_ref, qs_ref,
                hc_ref, hs_ref, wr_ref, wg_ref, wu_ref, wd_ref, zc_ref, zs_ref,
                xbuf, ybuf, xb_ref, gate_ref, acc_ref, gsem, ssem, *, capc, caps):
    e = pl.program_id(0)
    j = pl.program_id(1)
    n_e = pl.num_programs(0)
    n_j = FF_STEPS
    slot = e % 2
    other = 1 - slot
    rows = capc + caps
    gc, gs = _per_step(capc), _per_step(caps)
    groups = ((hc_ref, idxc_ref, zc_ref, qc_ref, gc, 0), (hs_ref, idxs_ref, zs_ref, qs_ref, gs, gc * n_j))

    def tile(ref, first_sublane):
        return ref.at[pl.ds(pl.multiple_of(first_sublane, ROW_TILE), ROW_TILE), :]

    def gather(ex, sl, step, i, group):
        h_ref, idx_ref, _, _, per_step, base = group
        p = step * per_step + i
        src = tile(h_ref, idx_ref[ex * (per_step * n_j) + p])
        pltpu.make_async_copy(src, xbuf.at[sl, :, base + p, :], gsem.at[sl]).start()

    def scatter(table_row, sl, step, i, group):
        _, _, z_ref, q_ref, per_step, base = group
        p = step * per_step + i
        dst = tile(z_ref, q_ref[table_row * (per_step * n_j) + p])
        pltpu.make_async_copy(ybuf.at[sl, :, base + p, :], dst, ssem.at[sl]).start()

    def all_steps(fn):
        for group in groups:
            def body(p, carry, group=group):
                fn(p, group)
                return carry
            lax.fori_loop(0, group[4] * n_j, body, 0, unroll=8)

    def wait_all(buf, sem, sl):
        pltpu.make_async_copy(buf.at[sl], buf.at[sl], sem.at[sl]).wait()

    @pl.when((e == 0) & (j == 0))
    def _():
        ybuf[...] = jnp.zeros_like(ybuf)
        all_steps(lambda p, group: gather(0, 0, 0, p, group))

    @pl.when(j == 0)
    def _():
        wait_all(xbuf, gsem, slot)
        for base, n, dst in ((0, capc, 0), (gc * n_j, caps, capc)):
            for kc in range(ROW_TILE):
                xb_ref[dst:dst + n, kc * LANES:(kc + 1) * LANES] = xbuf[slot, kc, base:base + n, :].astype(BF16)
        acc_ref[...] = jnp.zeros_like(acc_ref)
        logits = _dot(xb_ref[...], wr_ref[...])
        lane = lax.broadcasted_iota(I32, (1, LANES), 1)
        is_expert = lane < N_EXPERTS
        ex = jnp.exp(logits - jnp.max(jnp.where(is_expert, logits, -jnp.inf), axis=-1, keepdims=True))
        mine = jnp.sum(jnp.where(lane == e, ex, 0.0), axis=-1, keepdims=True)
        gate = mine / jnp.sum(jnp.where(is_expert, ex, 0.0), axis=-1, keepdims=True)
        gate_ref[...] = jnp.broadcast_to(gate, gate_ref.shape)

    nxt = jnp.minimum(e + 1, n_e - 1)
    for group in groups:
        for i in range(group[4]):
            gather(nxt, other, j, i, group)
            scatter(e, other, j, i, group)

    x = xb_ref[...]
    g = _dot(x, wg_ref[0].astype(BF16))
    u = _dot(x, wu_ref[0].astype(BF16))
    hid = (g * jax.nn.sigmoid(g) * u).astype(BF16)
    acc_ref[...] += _dot(hid, wd_ref[0].astype(BF16))

    @pl.when(j == n_j - 1)
    def _():
        @pl.when(e >= 1)
        def _():
            wait_all(ybuf, ssem, slot)

        for base, n, src in ((0, capc, 0), (gc * n_j, caps, capc)):
            for r0 in range(0, n, LANES):
                nr = min(LANES, n - r0)
                gate = gate_ref[src + r0:src + r0 + nr, :]
                for kc in range(ROW_TILE):
                    y = acc_ref[src + r0:src + r0 + nr, kc * LANES:(kc + 1) * LANES] * gate
                    ybuf[slot, kc, base + r0:base + r0 + nr, :] = y

        @pl.when(e == n_e - 1)
        def _():
            all_steps(lambda p, group: scatter(e + 1, slot, 0, p, group))
            wait_all(ybuf, ssem, other)
            wait_all(ybuf, ssem, slot)
            wait_all(xbuf, gsem, other)


def _per_step(cap):
    return -(-cap // FF_STEPS)


def _copy_tables(idx, qslot, n_rows):
    n_e, cap = idx.shape
    padded = _per_step(cap) * FF_STEPS
    n_pad = padded - cap
    idx_p = jnp.concatenate([idx, jnp.zeros((n_e, n_pad), I32)], axis=1)
    spare = n_rows + jnp.arange(padded + n_e * n_pad, dtype=I32)
    lead = spare[:padded][None, :]
    pad_rows = spare[padded:].reshape(n_e, n_pad)
    q_p = jnp.concatenate([lead, jnp.concatenate([qslot // ROW_TILE, pad_rows], axis=1)], axis=0) * ROW_TILE
    return idx_p.reshape(-1), q_p.reshape(-1), n_rows + padded + n_e * n_pad


def _moe(idxc, idxs, qc, qs, hc, hs, wrx, w_gate, w_up, w_down):
    capc, caps = idxc.shape[1], idxs.shape[1]
    rows = capc + caps
    tf = FF_TILE
    idxc, qc, zc_rows = _copy_tables(idxc, qc, N_EXPERTS * capc)
    idxs, qs, zs_rows = _copy_tables(idxs, qs, N_EXPERTS * caps)
    buf_rows = (_per_step(capc) + _per_step(caps)) * FF_STEPS
    any_spec = pl.BlockSpec(memory_space=pl.ANY)
    grid_spec = pltpu.PrefetchScalarGridSpec(
        num_scalar_prefetch=4,
        grid=(N_EXPERTS, FF_STEPS),
        in_specs=[any_spec, any_spec,
                  pl.BlockSpec((D_MODEL, LANES), lambda e, j, *_: (0, 0)),
                  pl.BlockSpec((1, D_MODEL, tf), lambda e, j, *_: (e, 0, j)),
                  pl.BlockSpec((1, D_MODEL, tf), lambda e, j, *_: (e, 0, j)),
                  pl.BlockSpec((1, tf, D_MODEL), lambda e, j, *_: (e, j, 0))],
        out_specs=[any_spec, any_spec],
        scratch_shapes=[pltpu.VMEM((2, ROW_TILE, buf_rows, LANES), F32),
                        pltpu.VMEM((2, ROW_TILE, buf_rows, LANES), F32),
                        pltpu.VMEM((rows, D_MODEL), BF16),
                        pltpu.VMEM((rows, LANES), F32),
                        pltpu.VMEM((rows, D_MODEL), F32),
                        pltpu.SemaphoreType.DMA((2,)),
                        pltpu.SemaphoreType.DMA((2,))])
    return pl.pallas_call(
        functools.partial(_moe_kernel, capc=capc, caps=caps),
        grid_spec=grid_spec,
        out_shape=[jax.ShapeDtypeStruct((zc_rows * ROW_TILE, LANES), F32),
                   jax.ShapeDtypeStruct((zs_rows * ROW_TILE, LANES), F32)],
        compiler_params=_cparams(("arbitrary", "arbitrary")),
        name="expert_ffn",
    )(idxc, idxs, qc, qs, hc, hs, wrx, w_gate, w_up, w_down)


Z_BUFFERS = 3


def _combine_kernel(clo_ref, chi_ref, z_ref, span_ref, x1_ref, mod_ref, g_ref, o_ref,
                    zbuf, acc_ref, sems, state, *, n_chunks):
    b = pl.program_id(0)
    chunk_rows = ROW_CHUNK * ROW_TILE

    @pl.when(b == 0)
    def _():
        state[0] = 0
        state[1] = 0

    def chunk_copy(c):
        src = z_ref.at[pl.ds(pl.multiple_of(c * chunk_rows, chunk_rows), chunk_rows), :]
        return pltpu.make_async_copy(src, zbuf.at[c % Z_BUFFERS], sems.at[c % Z_BUFFERS])

    acc_ref[...] = jnp.zeros_like(acc_ref)
    first_row = span_ref[:, 0:1] * LANES + span_ref[:, 1:2]
    end_row = span_ref[:, 2:3] * LANES + span_ref[:, 3:4]
    lane_row = lax.broadcasted_iota(I32, (TOK_BLOCK, ROW_CHUNK), 1)

    def body(c, carry):
        for _ in range(Z_BUFFERS):
            nxt = state[0]

            @pl.when(nxt <= jnp.minimum(c + Z_BUFFERS - 1, n_chunks - 1))
            def _():
                chunk_copy(nxt).start()
                state[0] = nxt + 1

        @pl.when(state[1] <= c)
        def _():
            chunk_copy(c).wait()
            state[1] = c + 1

        slot = c % Z_BUFFERS
        row = (c * ROW_CHUNK + lane_row).astype(F32)
        onehot = jnp.where((row >= first_row) & (row < end_row), 1.0, 0.0).astype(BF16)
        y = jnp.concatenate([zbuf[slot, pl.ds(kc, ROW_CHUNK, stride=ROW_TILE), :] for kc in range(ROW_TILE)],
                            axis=1)
        acc_ref[...] += _dot(onehot, y.astype(BF16))
        return carry

    lax.fori_loop(clo_ref[b], chi_ref[b], body, 0)
    gt2 = mod_ref[0][:, 5 * D_MODEL:6 * D_MODEL]
    o_ref[...] = x1_ref[...] + gt2 * _rms(acc_ref[...], g_ref[...], EPS)


def _combine(ranges, z, span, x1, mod3, mod_row, g_post_ffn):
    clo, chi = ranges
    t = x1.shape[0]
    tb = TOK_BLOCK
    grid_spec = pltpu.PrefetchScalarGridSpec(
        num_scalar_prefetch=2,
        grid=(t // tb,),
        in_specs=[pl.BlockSpec(memory_space=pl.ANY),
                  pl.BlockSpec((tb, LANES), lambda b, *_: (b, 0)),
                  pl.BlockSpec((tb, D_MODEL), lambda b, *_: (b, 0)),
                  pl.BlockSpec((1, 1, N_MOD * D_MODEL), lambda b, *_: (mod_row(b, tb), 0, 0)),
                  pl.BlockSpec((1, D_MODEL), lambda b, *_: (0, 0))],
        out_specs=pl.BlockSpec((tb, D_MODEL), lambda b, *_: (b, 0)),
        scratch_shapes=[pltpu.VMEM((Z_BUFFERS, ROW_CHUNK * ROW_TILE, LANES), F32),
                        pltpu.VMEM((tb, D_MODEL), F32),
                        pltpu.SemaphoreType.DMA((Z_BUFFERS,)),
                        pltpu.SMEM((2,), I32)])
    return pl.pallas_call(
        functools.partial(_combine_kernel, n_chunks=2 * t // ROW_CHUNK),
        grid_spec=grid_spec,
        out_shape=jax.ShapeDtypeStruct((t, D_MODEL), F32),
        compiler_params=_cparams(("arbitrary",)),
        name="combine",
    )(clo, chi, z, span, x1, mod3, g_post_ffn)


def _rope_tables(seq):
    half = HEAD_DIM // 4
    freqs = ROPE_THETA ** (-np.arange(half, dtype=np.float64) / half)
    s = np.arange(seq)
    row = (s // GRID_W)[:, None] * freqs[None, :]
    col = (s % GRID_W)[:, None] * freqs[None, :]
    ang = np.concatenate([row, row, col, col], axis=1)
    ang = np.tile(ang, (1, QK_W // HEAD_DIM))
    lane = np.arange(QK_W)
    sign = np.where((lane % 32) < 16, -1.0, 1.0)[None, :]
    return (jnp.asarray(np.cos(ang), dtype=F32), jnp.asarray(np.sin(ang) * sign, dtype=F32))


def _combine_ranges(rows, n_tok):
    step = TOK_BLOCK // LANES
    nb = n_tok // TOK_BLOCK
    lo = rows[0:nb * step:step, 0]
    hi = jnp.concatenate([lo[1:], jnp.full((1,), 2 * n_tok, I32)])
    return (lo // ROW_CHUNK).astype(I32), ((hi + ROW_CHUNK - 1) // ROW_CHUNK).astype(I32)


def kernel(x_prompt, x_sample, c, cache_k, cache_v, c_ctx, w_mod, b_mod, g_pre_mix, g_post_mix, g_pre_ffn, g_post_ffn, w_in, lam_q1, lam_k1, lam_q2, lam_k2, g_subln, w_proj_attn, w_proj_fourier, w_out, w_router, w_gate, w_up, w_down):
    assert w_mod.shape[0] == 1
    lam_init = 0.8 - 0.6 * math.exp(-0.3 * 0)
    bp, sp, _ = x_prompt.shape
    bs, ss, _ = x_sample.shape

    cond8 = jnp.concatenate([c_ctx[None, :], c, jnp.zeros((8 - 1 - bs, D_MODEL), F32)], axis=0)
    mod3 = _modulation(cond8, w_mod[0], b_mod).reshape(8, 1, N_MOD * D_MODEL)

    w_in_b = None
    wpa =w_proj_attn[0].astype(BF16)
    wpf = w_proj_fourier[0].astype(BF16)
    wout = w_out[0].astype(BF16)
    wr = w_router[0].astype(BF16)
    wrx = jnp.concatenate([wr, jnp.zeros((D_MODEL, LANES - N_EXPERTS), BF16)], axis=1)
    lam_p = jnp.concatenate([lam_q1, lam_k1, lam_q2, lam_k2], axis=0)

    groups = []
    for x, seq, positional, ctx in ((x_sample, ss, True, (cache_k, cache_v)),
                                    (x_prompt, sp, False, None)):
        nb = x.shape[0]
        t = nb * seq
        x2d = x.reshape(t, D_MODEL)
        if positional:
            mod_row = lambda i, tm, seq=seq: 1 + (i * tm) // seq
        else:
            mod_row = lambda i, tm: 0
        self_contained = ctx is None and not positional
        if self_contained:
            assert PRE_BLOCK % seq == 0
            new_k, new_v, x1, h2t, aff_t = _ctx_mixer(x2d, mod3, mod_row, g_pre_mix, w_in_b, seq,
                                                      (lam_p, g_subln, lam_init),
                                                      (g_post_mix, g_pre_ffn, wpa, wpf, wout, wrx), tm=PRE_BLOCK)
        else:
            q, k, v, f, ga, gf, w_in_b = _pre_mixer(x2d, mod3, mod_row, g_pre_mix, w_in[0],
                                                    _rope_tables(seq) if positional else None, seq, tm=ROW_BLOCK)
            o = _attention(lam_p, g_subln, q, k, v, ctx, seq, lam_init)
            fm = _fourier(f, seq)
            x1, h2t, aff_t = _post_mixer(o, fm, ga, gf, x2d, mod3, mod_row, g_post_mix, g_pre_ffn,
                                         wpa, wpf, wout, wrx)
        cap = 2 * t // N_EXPERTS
        assert t <= 1 << TOKEN_BITS
        posm, pack, span, ranges, rows = _route(aff_t, cap)
        idx, qslot = _slot_lists(ranges, posm, pack, cap)
        groups.append(dict(x1=x1, h2t=h2t, idx=idx, qslot=qslot, span=span, ranges=_combine_ranges(rows, t),
                           mod_row=mod_row, cache=(new_k, new_v) if self_contained else None, shape=x.shape))

    gs_, gc = groups
    zc, zs = _moe(gc["idx"], gs_["idx"], gc["qslot"], gs_["qslot"], gc["h2t"], gs_["h2t"], wrx,
                  w_gate[0], w_up[0], w_down[0])
    outs = []
    for g, z in ((gc, zc), (gs_, zs)):
        out = _combine(g["ranges"], z, g["span"], g["x1"], mod3, g["mod_row"], g_post_ffn)
        outs.append(out.reshape(g["shape"]))
    new_k, new_v = gc["cache"]
    return (outs[0], outs[1], new_k, new_v)
```

```python
import functools
import math

import numpy as np
import jax
import jax.numpy as jnp
from jax import lax
from jax.experimental import pallas as pl
from jax.experimental.pallas import tpu as pltpu

F32 = jnp.float32
BF16 = jnp.bfloat16
I32 = jnp.int32

D_MODEL = 1024
N_HEADS = 6
HEAD_DIM = 64
V_DIM = 128
QK_W = 768
FOUR_W = 256
FOUR_G = 64
IN_W = 4608
N_EXPERTS = 16
D_FF = 2816
N_MOD = 6
GRID_W = 64
ROPE_THETA = 10000.0
EPS = 1e-6
SUBLN_EPS = 1e-5

LANES = 128
ROW_BLOCK = 256
PRE_BLOCK = 512
GATE_CHUNK = 256
POST_BLOCK = 1024
POST_SUB = 256
WEIGHT_ROWS = 128
TOK_BLOCK = 256
ROW_CHUNK = 256
FF_TILE = 256
FF_STEPS = D_FF // FF_TILE
ROW_TILE = D_MODEL // LANES
TOKEN_BITS = 13
VMEM_LIMIT = 56 * 1024 * 1024


def _cparams(sem):
    return pltpu.CompilerParams(dimension_semantics=sem, vmem_limit_bytes=VMEM_LIMIT)


def _dot(a, b):
    return jnp.dot(a, b, preferred_element_type=F32)


def _rms(x, g, eps):
    return x * lax.rsqrt(jnp.mean(x * x, axis=-1, keepdims=True) + eps) * g


def _whole_vmem():
    return pl.BlockSpec(memory_space=pltpu.MemorySpace.VMEM)


def _cast_weight(w_hbm, wb_ref, stage_ref, sem):
    rows = stage_ref.shape[1]
    chunks = [pltpu.make_async_copy(w_hbm.at[pl.ds(r0, rows), :], stage_ref.at[n % 2], sem.at[n % 2])
              for n, r0 in enumerate(range(0, w_hbm.shape[0], rows))]
    chunks[0].start()
    for n, chunk in enumerate(chunks):
        if n + 1 < len(chunks):
            chunks[n + 1].start()
        chunk.wait()
        wb_ref[n * rows:(n + 1) * rows, :] = stage_ref[n % 2].astype(BF16)


def _mod_kernel(c_ref, w_ref, b_ref, o_ref):
    c = c_ref[...]
    s = c * jax.nn.sigmoid(c)
    o_ref[...] = _dot(s.astype(BF16), w_ref[...].astype(BF16)) + b_ref[...]


def _modulation(cond8, w_mod, b_mod):
    tn = 1024
    n = N_MOD * D_MODEL
    return pl.pallas_call(
        _mod_kernel,
        grid=(n // tn,),
        in_specs=[pl.BlockSpec((8, D_MODEL), lambda j: (0, 0)),
                  pl.BlockSpec((D_MODEL, tn), lambda j: (0, j)),
                  pl.BlockSpec((1, tn), lambda j: (0, j))],
        out_specs=pl.BlockSpec((8, tn), lambda j: (0, j)),
        out_shape=jax.ShapeDtypeStruct((8, n), F32),
        compiler_params=_cparams(("arbitrary",)),
        name="modulation",
    )(cond8, w_mod, b_mod)


def _diff_lambda(lp, lam_init):
    s1 = jnp.sum(lp[0:1] * lp[1:2], axis=-1, keepdims=True)
    s2 = jnp.sum(lp[2:3] * lp[3:4], axis=-1, keepdims=True)
    return jnp.exp(s1) - jnp.exp(s2) + lam_init


def _attention_weights(q, k):
    comp1 = lax.broadcasted_iota(I32, (1, V_DIM), 1) < HEAD_DIM
    qs = q * jnp.asarray(HEAD_DIM ** -0.5, BF16)
    zero = jnp.zeros_like(qs)

    def weights(qc):
        s = lax.dot_general(qc, k, (((1,), (1,)), ((), ())), preferred_element_type=F32)
        return jnp.exp(s - jnp.max(s, axis=-1, keepdims=True)).astype(BF16)

    return weights(jnp.where(comp1, qs, zero)), weights(jnp.where(comp1, zero, qs))


def _attention_output(weights, v, lam, g_subln, lam_init):
    v_ones = jnp.concatenate([v, jnp.ones_like(v)], axis=1)

    def attend(ex):
        ov = _dot(ex, v_ones)
        return ov[:, 0:V_DIM] / ov[:, V_DIM:2 * V_DIM]

    o = attend(weights[0]) - lam * attend(weights[1])
    return (_rms(o, g_subln, SUBLN_EPS) * (1.0 - lam_init)).astype(BF16)


def _rope(z, cos, sin_signed, first_half):
    fwd = pltpu.roll(z, QK_W - 16, axis=1)
    bwd = pltpu.roll(z, 16, axis=1)
    return z * cos + jnp.where(first_half, fwd, bwd) * sin_signed


def _pre_kernel(*refs, positional):
    it = iter(refs)
    x_ref, mod_ref, g_ref, w_hbm = next(it), next(it), next(it), next(it)
    if positional:
        cos_ref, sin_ref = next(it), next(it)
    q_ref, k_ref, v_ref, f_ref, ga_ref, gf_ref = (next(it) for _ in range(6))
    wb_hbm, w_ref, stage_ref, sem = next(it), next(it), next(it), next(it)
    weight_out = pltpu.make_async_copy(w_ref, wb_hbm, sem.at[2])

    @pl.when(pl.program_id(0) == 0)
    def _():
        _cast_weight(w_hbm, w_ref, stage_ref, sem)
        weight_out.start()

    @pl.when(pl.program_id(0) == pl.num_programs(0) - 1)
    def _():
        weight_out.wait()

    m = mod_ref[0]
    sh1 = m[:, 0:D_MODEL]
    sc1 = m[:, D_MODEL:2 * D_MODEL]
    h = _rms(x_ref[...], g_ref[...], EPS) * (1.0 + sc1) + sh1
    hb = h.astype(BF16)

    def proj(lo, hi):
        return _dot(hb, w_ref[:, lo:hi])

    zq = proj(0, QK_W)
    zk = proj(QK_W, 2 * QK_W)
    zv = proj(2 * QK_W, 3 * QK_W)
    if positional:
        lane = lax.broadcasted_iota(I32, (1, QK_W), 1)
        first_half = (lane % 32) < 16
        cos = cos_ref[...]
        sin_signed = sin_ref[...]
        zq = _rope(zq, cos, sin_signed, first_half)
        zk = _rope(zk, cos, sin_signed, first_half)
    q_ref[...] = zq.astype(BF16)
    k_ref[...] = zk.astype(BF16)
    v_ref[...] = zv.astype(BF16)
    f0 = 3 * QK_W
    g0 = f0 + FOUR_W
    f_ref[...] = proj(f0, g0)
    ga_ref[...] = jax.nn.sigmoid(proj(g0, g0 + D_MODEL)).astype(BF16)
    gf_ref[...] = jax.nn.sigmoid(proj(g0 + D_MODEL, IN_W)).astype(BF16)


def _ctx_kernel(x_ref, mod_ref, g_ref, w_ref, bc_ref, bs_ref, cs_ref, ss_ref, lam_ref, gs_ref,
                gpost_ref, gffn_ref, wpa_ref, wpf_ref, wout_ref, wrx_ref,
                kc_ref, vc_ref, x1_ref, h2t_ref, aff_ref,
                o_ref, f_ref, ga_ref, gf_ref, *, seq, lam_init):
    m = mod_ref[0]
    sh1 = m[:, 0:D_MODEL]
    sc1 = m[:, D_MODEL:2 * D_MODEL]
    hb = (_rms(x_ref[...], g_ref[...], EPS) * (1.0 + sc1) + sh1).astype(BF16)
    seqs = [slice(b * seq, (b + 1) * seq) for b in range(x_ref.shape[0] // seq)]
    lam = _diff_lambda(lam_ref[...], lam_init)
    f0 = 3 * QK_W
    g0 = f0 + FOUR_W

    def proj(lo, hi):
        return _dot(hb, w_ref[:, lo:hi])

    q, k, v = {}, {}, {}

    def qkv_chunk(store, cache_ref, lo, c):
        cols = slice(c * GATE_CHUNK, (c + 1) * GATE_CHUNK)
        z = proj(lo + cols.start, lo + cols.stop)
        for hd in range(cols.start // V_DIM, cols.stop // V_DIM):
            z_hd = z[:, hd * V_DIM - cols.start:(hd + 1) * V_DIM - cols.start]
            if cache_ref is not None:
                for b, rs in enumerate(seqs):
                    cache_ref[b, 0, hd] = z_hd[rs, :]
            store[hd] = z_hd.astype(BF16)

    def fourier_chunk():
        f = proj(f0, g0)
        for rs in seqs:
            f_ref[rs, :] = _dft_real(f[rs, :], bc_ref, bs_ref, cs_ref, ss_ref)

    def gate_chunk(ref, lo, c):
        cols = slice(c * GATE_CHUNK, (c + 1) * GATE_CHUNK)
        ref[:, cols] = jax.nn.sigmoid(proj(lo + cols.start, lo + cols.stop)).astype(BF16)

    for store, cache, lo in ((q, None, 0), (k, kc_ref, QK_W), (v, vc_ref, 2 * QK_W)):
        for c in range(QK_W // GATE_CHUNK):
            qkv_chunk(store, cache, lo, c)
    matmul_work = [fourier_chunk] + [functools.partial(gate_chunk, ref, lo, c)
                                     for ref, lo in ((ga_ref, g0), (gf_ref, g0 + D_MODEL))
                                     for c in range(D_MODEL // GATE_CHUNK)]
    def store_aff(rs, aff):
        aff_ref[0, :, rs] = aff

    for rs in seqs:
        for hd in range(N_HEADS):
            weights = _attention_weights(q[hd][rs, :], k[hd][rs, :])
            if matmul_work:
                matmul_work.pop(0)()
            o_ref[rs, hd * V_DIM:(hd + 1) * V_DIM] = _attention_output(weights, v[hd][rs, :], lam, gs_ref[...],
                                                                      lam_init)
        matmul_work += _post_stages(rs, o_ref, f_ref, ga_ref, gf_ref, x_ref, m, gpost_ref, gffn_ref,
                                    wpa_ref, wpf_ref, wout_ref, wrx_ref, x1_ref, h2t_ref, store_aff)
    for work in matmul_work:
        work()


def _ctx_mixer(x2d, mod3, mod_row, g_pre, w_in_b, seq, attn_params, post_params, tm):
    t = x2d.shape[0]
    lam_p, g_subln, lam_init = attn_params
    g_post, g_ffn, wpa, wpf, wout, wrx = post_params
    assert tm % seq == 0 and t % tm == 0 and tm % POST_SUB == 0
    row = lambda i: (i, 0)
    const = lambda i: (0, 0)
    consts = _dft_consts(seq) + (lam_p, g_subln, g_post, g_ffn)
    nb = t // seq
    cshape = jax.ShapeDtypeStruct((nb, 1, N_HEADS, seq, V_DIM), F32)
    cspec = pl.BlockSpec((tm // seq, 1, N_HEADS, seq, V_DIM), lambda i: (i, 0, 0, 0, 0))
    new_k, new_v, x1, h2t, aff = pl.pallas_call(
        functools.partial(_ctx_kernel, seq=seq, lam_init=lam_init),
        grid=(t // tm,),
        in_specs=[pl.BlockSpec((tm, D_MODEL), row),
                  pl.BlockSpec((1, 1, N_MOD * D_MODEL), lambda i: (mod_row(i, tm), 0, 0)),
                  pl.BlockSpec((1, D_MODEL), const),
                  _whole_vmem()] + [pl.BlockSpec(c.shape, const) for c in consts] + [_whole_vmem()] * 4,
        out_specs=[cspec, cspec,
                   pl.BlockSpec((tm, D_MODEL), row),
                   pl.BlockSpec((tm * ROW_TILE, LANES), row),
                   pl.BlockSpec((1, N_EXPERTS, tm), lambda i: (i, 0, 0))],
        out_shape=[cshape, cshape,
                   jax.ShapeDtypeStruct((t, D_MODEL), F32),
                   jax.ShapeDtypeStruct((t * ROW_TILE, LANES), F32),
                   jax.ShapeDtypeStruct((t // tm, N_EXPERTS, tm), F32)],
        scratch_shapes=[pltpu.VMEM((tm, QK_W), BF16),
                        pltpu.VMEM((tm, FOUR_W), BF16),
                        pltpu.VMEM((tm, D_MODEL), BF16),
                        pltpu.VMEM((tm, D_MODEL), BF16)],
        compiler_params=_cparams(("arbitrary",)),
        name="ctx_mixer",
    )(x2d, mod3, g_pre, w_in_b, *consts, wpa, wpf, wout, wrx)
    aff = aff.transpose(1, 0, 2).reshape(N_EXPERTS, t // LANES, LANES)
    return new_k, new_v, x1, h2t, aff


def _pre_mixer(x2d, mod3, mod_row, g_pre, w_in, rope_tabs, seq, tm):
    t = x2d.shape[0]
    positional = rope_tabs is not None
    assert seq % tm == 0
    blocks_per_seq = seq // tm
    row = lambda i: (i, 0)
    any_spec = pl.BlockSpec(memory_space=pl.ANY)
    in_specs = [pl.BlockSpec((tm, D_MODEL), row),
                pl.BlockSpec((1, 1, N_MOD * D_MODEL), lambda i: (mod_row(i, tm), 0, 0)),
                pl.BlockSpec((1, D_MODEL), lambda i: (0, 0)),
                any_spec]
    args = [x2d, mod3, g_pre, w_in]
    if positional:
        in_specs += [pl.BlockSpec((tm, QK_W), lambda i: (i % blocks_per_seq, 0))] * 2
        args += list(rope_tabs)
    out_shape = [jax.ShapeDtypeStruct((t, QK_W), BF16)] * 3 + [
        jax.ShapeDtypeStruct((t, FOUR_W), F32),
        jax.ShapeDtypeStruct((t, D_MODEL), BF16),
        jax.ShapeDtypeStruct((t, D_MODEL), BF16),
        jax.ShapeDtypeStruct(w_in.shape, BF16)]
    out_specs = [pl.BlockSpec((tm, QK_W), row)] * 3 + [
        pl.BlockSpec((tm, FOUR_W), row),
        pl.BlockSpec((tm, D_MODEL), row),
        pl.BlockSpec((tm, D_MODEL), row),
        any_spec]
    return pl.pallas_call(
        functools.partial(_pre_kernel, positional=positional),
        grid=(t // tm,),
        in_specs=in_specs,
        out_specs=out_specs,
        out_shape=out_shape,
        scratch_shapes=[pltpu.VMEM(w_in.shape, BF16),
                        pltpu.VMEM((2, WEIGHT_ROWS, w_in.shape[1]), F32),
                        pltpu.SemaphoreType.DMA((3,))],
        compiler_params=_cparams(("arbitrary",)),
        name="pre_mixer",
    )(*args)


def _attn_kernel(*refs, lam_init, has_ctx):
    it = iter(refs)
    lam_ref, gs_ref, q_ref, k_ref, v_ref = (next(it) for _ in range(5))
    if has_ctx:
        ck_ref, cv_ref = next(it), next(it)
    o_ref = next(it)

    lam = _diff_lambda(lam_ref[...], lam_init)

    def operand(ref, cache_ref, hd):
        x = ref[:, hd * V_DIM:(hd + 1) * V_DIM]
        return jnp.concatenate([cache_ref[0, 0, hd].astype(BF16), x], axis=0) if has_ctx else x

    def head_weights(hd):
        return _attention_weights(q_ref[:, hd * V_DIM:(hd + 1) * V_DIM], operand(k_ref, ck_ref if has_ctx else None, hd))

    weights = head_weights(0)
    for hd in range(N_HEADS):
        nxt = head_weights(hd + 1) if hd + 1 < N_HEADS else None
        v = operand(v_ref, cv_ref if has_ctx else None, hd)
        o_ref[:, hd * V_DIM:(hd + 1) * V_DIM] = _attention_output(weights, v, lam, gs_ref[...], lam_init)
        weights = nxt


def _attention(lam_p, g_subln, q, k, v, ctx, seq, lam_init):
    t = q.shape[0]
    tq = ROW_BLOCK
    qb = seq // tq
    has_ctx = ctx is not None
    in_specs = [pl.BlockSpec((4, HEAD_DIM), lambda b, i: (0, 0)),
                pl.BlockSpec((1, V_DIM), lambda b, i: (0, 0)),
                pl.BlockSpec((tq, QK_W), lambda b, i: (b * qb + i, 0)),
                pl.BlockSpec((seq, QK_W), lambda b, i: (b, 0)),
                pl.BlockSpec((seq, QK_W), lambda b, i: (b, 0))]
    args = [lam_p, g_subln, q, k, v]
    if has_ctx:
        past = ctx[0].shape[3]
        cspec = pl.BlockSpec((1, 1, N_HEADS, past, V_DIM), lambda b, i: (b, 0, 0, 0, 0))
        in_specs += [cspec, cspec]
        args += list(ctx)
    return pl.pallas_call(
        functools.partial(_attn_kernel, lam_init=lam_init, has_ctx=has_ctx),
        grid=(t // seq, qb),
        in_specs=in_specs,
        out_specs=pl.BlockSpec((tq, QK_W), lambda b, i: (b * qb + i, 0)),
        out_shape=jax.ShapeDtypeStruct((t, QK_W), BF16),
        compiler_params=_cparams(("arbitrary", "arbitrary")),
        name="diff_attention",
    )(*args)


def _dft_real(f, bc_ref, bs_ref, cs_ref, ss_ref):
    fb = f.astype(BF16)
    u = _dot(fb, bc_ref[...].astype(BF16)).astype(BF16)
    w = _dot(fb, bs_ref[...].astype(BF16)).astype(BF16)
    return (_dot(cs_ref[...].astype(BF16), u) - _dot(ss_ref[...].astype(BF16), w)).astype(BF16)


def _fourier_kernel(f_ref, bc_ref, bs_ref, cs_ref, ss_ref, o_ref):
    o_ref[...] = _dft_real(f_ref[...], bc_ref, bs_ref, cs_ref, ss_ref)


def _dft_consts(seq):
    c = np.arange(FOUR_G)
    ang_c = 2.0 * np.pi * ((c[:, None] * c[None, :]) % FOUR_G) / FOUR_G
    eye = np.eye(FOUR_W // FOUR_G)
    bc = np.kron(eye, np.cos(ang_c)) / math.sqrt(FOUR_G)
    bs = np.kron(eye, np.sin(ang_c)) / math.sqrt(FOUR_G)
    s = np.arange(seq)
    ang_s = 2.0 * np.pi * ((s[:, None] * s[None, :]) % seq) / seq
    cs = np.cos(ang_s) / math.sqrt(seq)
    ss = np.sin(ang_s) / math.sqrt(seq)
    return tuple(jnp.asarray(a, dtype=F32) for a in (bc, bs, cs, ss))


def _fourier(f, seq):
    t = f.shape[0]
    bc, bs, cs, ss = _dft_consts(seq)
    const = lambda b: (0, 0)
    return pl.pallas_call(
        _fourier_kernel,
        grid=(t // seq,),
        in_specs=[pl.BlockSpec((seq, FOUR_W), lambda b: (b, 0)),
                  pl.BlockSpec((FOUR_W, FOUR_W), const),
                  pl.BlockSpec((FOUR_W, FOUR_W), const),
                  pl.BlockSpec((seq, seq), const),
                  pl.BlockSpec((seq, seq), const)],
        out_specs=pl.BlockSpec((seq, FOUR_W), lambda b: (b, 0)),
        out_shape=jax.ShapeDtypeStruct((t, FOUR_W), BF16),
        compiler_params=_cparams(("arbitrary",)),
        name="fourier_mix",
    )(f, bc, bs, cs, ss)


def _post_kernel(o_ref, fm_ref, ga_ref, gf_ref, x_ref, mod_ref, gpost_ref, gffn_ref,
                 wpa_ref, wpf_ref, wout_ref, wrx_ref,
                 x1_ref, h2t_ref, afft_ref):
    def store_aff(rs, aff):
        for u in range(POST_SUB // LANES):
            afft_ref[:, rs.start // LANES + u, :] = aff[:, u * LANES:(u + 1) * LANES]

    _post_rows(o_ref, fm_ref, ga_ref, gf_ref, x_ref, mod_ref[0], gpost_ref, gffn_ref,
               wpa_ref, wpf_ref, wout_ref, wrx_ref, x1_ref, h2t_ref, store_aff)


def _post_rows(o_ref, fm_ref, ga_ref, gf_ref, x_ref, m, gpost_ref, gffn_ref,
               wpa_ref, wpf_ref, wout_ref, wrx_ref, x1_ref, h2t_ref, store_aff):
    subs = [slice(r0, r0 + POST_SUB) for r0 in range(0, o_ref.shape[0], POST_SUB)]
    stages = [_post_stages(rs, o_ref, fm_ref, ga_ref, gf_ref, x_ref, m, gpost_ref, gffn_ref,
                           wpa_ref, wpf_ref, wout_ref, wrx_ref, x1_ref, h2t_ref, store_aff) for rs in subs]
    for same_stage in zip(*stages):
        for stage in same_stage:
            stage()


def _post_stages(rs, o_ref, fm_ref, ga_ref, gf_ref, x_ref, m, gpost_ref, gffn_ref,
                 wpa_ref, wpf_ref, wout_ref, wrx_ref, x1_ref, h2t_ref, store_aff):
    gt1 = m[:, 2 * D_MODEL:3 * D_MODEL]
    sh2 = m[:, 3 * D_MODEL:4 * D_MODEL]
    sc2 = m[:, 4 * D_MODEL:5 * D_MODEL]
    kept = {}

    def merge():
        a, b = _dot(o_ref[rs, :], wpa_ref[...]), _dot(fm_ref[rs, :], wpf_ref[...])
        kept["merged"] = (ga_ref[rs, :] * a + gf_ref[rs, :] * b).astype(BF16)

    def project():
        y = _dot(kept.pop("merged"), wout_ref[...])
        x1 = x_ref[rs, :] + gt1 * _rms(y, gpost_ref[...], EPS)
        x1_ref[rs, :] = x1
        kept["h2"] = _rms(x1, gffn_ref[...], EPS) * (1.0 + sc2) + sh2

    def route():
        h2 = kept.pop("h2")
        lt = _dot(h2.astype(BF16), wrx_ref[...]).T[0:N_EXPERTS]
        et = jnp.exp(lt - jnp.max(lt, axis=0, keepdims=True))
        store_aff(rs, et / jnp.sum(et, axis=0, keepdims=True))
        n = rs.stop - rs.start
        for kc in range(ROW_TILE):
            h2t_ref[pl.ds(rs.start * ROW_TILE + kc, n, stride=ROW_TILE), :] = h2[:, kc * LANES:(kc + 1) * LANES]

    return [merge, project, route]


def _post_mixer(o, fm, ga, gf, x2d, mod3, mod_row, g_post, g_ffn, wpa, wpf, wout, wrx):
    t = x2d.shape[0]
    tm = POST_BLOCK
    row = lambda i: (i, 0)
    const = lambda i: (0, 0)
    return pl.pallas_call(
        _post_kernel,
        grid=(t // tm,),
        in_specs=[pl.BlockSpec((tm, QK_W), row),
                  pl.BlockSpec((tm, FOUR_W), row),
                  pl.BlockSpec((tm, D_MODEL), row),
                  pl.BlockSpec((tm, D_MODEL), row),
                  pl.BlockSpec((tm, D_MODEL), row),
                  pl.BlockSpec((1, 1, N_MOD * D_MODEL), lambda i: (mod_row(i, tm), 0, 0)),
                  pl.BlockSpec((1, D_MODEL), const),
                  pl.BlockSpec((1, D_MODEL), const),
                  pl.BlockSpec((QK_W, D_MODEL), const),
                  pl.BlockSpec((FOUR_W, D_MODEL), const),
                  pl.BlockSpec((D_MODEL, D_MODEL), const),
                  pl.BlockSpec((D_MODEL, LANES), const)],
        out_specs=[pl.BlockSpec((tm, D_MODEL), row),
                   pl.BlockSpec((tm * ROW_TILE, LANES), row),
                   pl.BlockSpec((N_EXPERTS, tm // LANES, LANES), lambda i: (0, i, 0))],
        out_shape=[jax.ShapeDtypeStruct((t, D_MODEL), F32),
                   jax.ShapeDtypeStruct((t * ROW_TILE, LANES), F32),
                   jax.ShapeDtypeStruct((N_EXPERTS, t // LANES, LANES), F32)],
        compiler_params=_cparams(("arbitrary",)),
        name="post_mixer",
    )(o, fm, ga, gf, x2d, mod3, g_post, g_ffn, wpa, wpf, wout, wrx)


RANGE_ROWS = 8


def _route_kernel(aff_ref, posm_ref, pack_ref, span_ref, ranges_ref, rows_ref, *, cap, n_tok):
    aff = aff_ref[...]
    nt = n_tok // LANES
    capf = float(cap)

    def count_ge(v):
        return jnp.sum(jnp.where(aff >= v, 1.0, 0.0), axis=(1, 2), keepdims=True)

    def search(i, thr):
        cand = thr | jnp.left_shift(jnp.int32(1), 30 - i)
        return jnp.where(count_ge(pltpu.bitcast(cand, F32)) >= capf, cand, thr)

    thr = lax.fori_loop(0, 31, search, jnp.zeros((N_EXPERTS, 1, 1), I32))
    lo = pltpu.bitcast(thr, F32)
    hi = pltpu.bitcast(thr + 1, F32)

    def refine(i, c):
        lo, hi = c
        mid = lo + (hi - lo) * 0.5
        ok = count_ge(mid) >= capf
        return jnp.where(ok, mid, lo), jnp.where(ok, hi, mid)

    lo, hi = lax.fori_loop(0, 12, refine, (lo, hi))
    gt = aff >= hi
    eq = (aff >= lo) & (aff < hi)
    n_tie = capf - jnp.sum(jnp.where(gt, 1.0, 0.0), axis=(1, 2), keepdims=True)

    sq0 = lax.broadcasted_iota(I32, (LANES, LANES), 0)
    sq1 = lax.broadcasted_iota(I32, (LANES, LANES), 1)
    along_total = jnp.concatenate([jnp.where(sq0 <= sq1, 1.0, 0.0), jnp.ones((LANES, LANES), F32)],
                                  axis=1).astype(BF16)
    m = N_EXPERTS * nt
    r0 = lax.broadcasted_iota(I32, (m, m), 0)
    r1 = lax.broadcasted_iota(I32, (m, m), 1)
    earlier = jnp.where((r0 // nt == r1 // nt) & (r1 < r0), 1.0, 0.0).astype(BF16)
    lane = lax.broadcasted_iota(I32, (1, LANES), 1)
    token = lax.broadcasted_iota(I32, (nt, LANES), 0) * LANES + lane

    def tile_counts(x):
        both = _dot(x.reshape(m, LANES).astype(BF16), along_total)
        total = both[:, LANES:]
        before = _dot(earlier, total.astype(BF16))
        shape = (N_EXPERTS, nt, LANES)
        return both[:, :LANES].reshape(shape), total.reshape(shape), before.reshape(shape)

    eq_f = jnp.where(eq, 1.0, 0.0)
    eq_along, _, eq_before = tile_counts(eq_f)
    sel = jnp.where(gt, 1.0, jnp.where(eq_along + eq_before <= n_tie, eq_f, 0.0))
    sel_along, sel_total, sel_before = tile_counts(sel)
    posm_ref[...] = jnp.where(sel > 0.5, sel_along + sel_before - sel, -1.0).astype(I32)

    cnt = jnp.sum(sel, axis=0)
    rows_before = jnp.sum(sel_before, axis=0)
    tok_start = _dot(cnt.astype(BF16), along_total[:, :LANES]) - cnt + rows_before
    k = jnp.zeros((nt, LANES), F32)
    slots_before = jnp.zeros((nt, LANES), F32)
    slots_here = jnp.zeros((nt, LANES), F32)
    for e in range(N_EXPERTS):
        pack_ref[e] = (tok_start + k).astype(I32) * (1 << TOKEN_BITS) + token
        k = k + sel[e]
        slots_before = jnp.where(lane == e, sel_before[e], slots_before)
        slots_here = jnp.where(lane == e, sel_total[e], slots_here)

    eye = jnp.where(sq0 == sq1, 1.0, 0.0).astype(BF16)
    tok_end = tok_start + cnt
    parts = []
    for v in (tok_start, tok_end):
        high = jnp.floor(v * (1.0 / LANES))
        parts += [high, v - high * LANES]
    for c in range(nt):
        rows4 = jnp.zeros((LANES, LANES), F32)
        for r, part in enumerate(parts):
            rows4 = jnp.where(sq0 == r, jnp.broadcast_to(part[c:c + 1, :], (LANES, LANES)), rows4)
        span_ref[c * LANES:(c + 1) * LANES, :] = lax.dot_general(
            eye, rows4.astype(BF16), (((1,), (1,)), ((), ())), preferred_element_type=F32)

    ranges_ref[...] = jnp.zeros_like(ranges_ref)
    for s in range(cap // LANES):
        done = jnp.where(slots_before + slots_here <= float(s * LANES), 1.0, 0.0)
        begun = jnp.where(slots_before < float((s + 1) * LANES), 1.0, 0.0)
        ranges_ref[s:s + 1, :] = jnp.sum(done, axis=0, keepdims=True).astype(I32)
        ranges_ref[RANGE_ROWS + s:RANGE_ROWS + s + 1, :] = jnp.sum(begun, axis=0, keepdims=True).astype(I32)
    rows_ref[...] = rows_before.astype(I32)


def _route(aff, cap):
    n_e, nt, _ = aff.shape
    n_tok = nt * LANES
    assert cap // LANES <= RANGE_ROWS
    return pl.pallas_call(
        functools.partial(_route_kernel, cap=cap, n_tok=n_tok),
        out_shape=[jax.ShapeDtypeStruct((n_e, nt, LANES), I32),
                   jax.ShapeDtypeStruct((n_e, nt, LANES), I32),
                   jax.ShapeDtypeStruct((n_tok, LANES), F32),
                   jax.ShapeDtypeStruct((2 * RANGE_ROWS, LANES), I32),
                   jax.ShapeDtypeStruct((nt, LANES), I32)],
        compiler_params=pltpu.CompilerParams(vmem_limit_bytes=VMEM_LIMIT),
        name="route",
    )(aff)


def _slots_kernel(ranges_ref, posm_ref, pack_ref, idx_ref, qslot_ref, acc_ref):
    e = pl.program_id(0)
    sub = lax.broadcasted_iota(I32, (LANES, LANES), 0)
    eye = sub == lax.broadcasted_iota(I32, (LANES, LANES), 1)
    n_tiles = idx_ref.shape[1]

    for s in range(n_tiles):
        slot = sub + s * LANES

        def body(c, acc, slot=slot):
            hit = posm_ref[e, pl.ds(c, 1), :] == slot
            return acc + jnp.where(hit, pack_ref[e, pl.ds(c, 1), :], 0)

        acc_ref[s] = lax.fori_loop(ranges_ref[s, e], ranges_ref[RANGE_ROWS + s, e], body,
                                   jnp.zeros((LANES, LANES), I32))

    def as_row(part):
        col = jnp.sum(part.astype(F32), axis=1, keepdims=True)
        return jnp.sum(jnp.where(eye, col, 0.0), axis=0, keepdims=True).astype(I32)

    for s in range(n_tiles):
        acc = acc_ref[s]
        idx_ref[0, s:s + 1, :] = as_row(acc & ((1 << TOKEN_BITS) - 1)) * ROW_TILE
        qslot_ref[0, s:s + 1, :] = as_row(lax.shift_right_logical(acc, TOKEN_BITS)) * ROW_TILE


def _slot_lists(ranges, posm4, qdst4, cap):
    ns = cap // LANES
    grid_spec = pltpu.PrefetchScalarGridSpec(
        num_scalar_prefetch=1,
        grid=(N_EXPERTS,),
        in_specs=[_whole_vmem(), _whole_vmem()],
        out_specs=[pl.BlockSpec((1, ns, LANES), lambda e, *_: (e, 0, 0)),
                   pl.BlockSpec((1, ns, LANES), lambda e, *_: (e, 0, 0))],
        scratch_shapes=[pltpu.VMEM((ns, LANES, LANES), I32)])
    idx, qslot = pl.pallas_call(
        _slots_kernel,
        grid_spec=grid_spec,
        out_shape=[jax.ShapeDtypeStruct((N_EXPERTS, ns, LANES), I32),
                   jax.ShapeDtypeStruct((N_EXPERTS, ns, LANES), I32)],
        compiler_params=_cparams(("arbitrary",)),
        name="slot_lists",
    )(ranges, posm4, qdst4)
    return idx.reshape(N_EXPERTS, cap), qslot.reshape(N_EXPERTS, cap)


def _moe_kernel(idxc_ref, idxs_ref, qc_ref, qs_ref,
                hc_ref, hs_ref, wr_ref, wg_a, wg_b, wu_a, wu_b, wd_a, wd_b, zc_ref, zs_ref,
                xbuf, ybuf, xb_ref, gate_ref, acc_ref, gsem, ssem, *, capc, caps):
    e = pl.program_id(0)
    j = pl.program_id(1)
    n_e = pl.num_programs(0)
    n_j = FF_STEPS
    slot = e % 2
    other = 1 - slot
    rows = capc + caps
    gc, gs = _per_step(capc), _per_step(caps)
    groups = ((hc_ref, idxc_ref, zc_ref, qc_ref, gc, 0), (hs_ref, idxs_ref, zs_ref, qs_ref, gs, gc * n_j))

    def tile(ref, first_sublane):
        return ref.at[pl.ds(pl.multiple_of(first_sublane, ROW_TILE), ROW_TILE), :]

    def gather(ex, sl, step, i, group):
        h_ref, idx_ref, _, _, per_step, base = group
        p = step * per_step + i
        src = tile(h_ref, idx_ref[ex * (per_step * n_j) + p])
        pltpu.make_async_copy(src, xbuf.at[sl, :, base + p, :], gsem.at[sl]).start()

    def scatter(table_row, sl, step, i, group):
        _, _, z_ref, q_ref, per_step, base = group
        p = step * per_step + i
        dst = tile(z_ref, q_ref[table_row * (per_step * n_j) + p])
        pltpu.make_async_copy(ybuf.at[sl, :, base + p, :], dst, ssem.at[sl]).start()

    def all_steps(fn):
        for group in groups:
            def body(p, carry, group=group):
                fn(p, group)
                return carry
            lax.fori_loop(0, group[4] * n_j, body, 0, unroll=8)

    def wait_all(buf, sem, sl):
        pltpu.make_async_copy(buf.at[sl], buf.at[sl], sem.at[sl]).wait()

    @pl.when((e == 0) & (j == 0))
    def _():
        ybuf[...] = jnp.zeros_like(ybuf)
        all_steps(lambda p, group: gather(0, 0, 0, p, group))

    @pl.when(j == 0)
    def _():
        wait_all(xbuf, gsem, slot)
        for base, n, dst in ((0, capc, 0), (gc * n_j, caps, capc)):
            for kc in range(ROW_TILE):
                xb_ref[dst:dst + n, kc * LANES:(kc + 1) * LANES] = xbuf[slot, kc, base:base + n, :].astype(BF16)
        acc_ref[...] = jnp.zeros_like(acc_ref)
        logits = _dot(xb_ref[...], wr_ref[...])
        lane = lax.broadcasted_iota(I32, (1, LANES), 1)
        is_expert = lane < N_EXPERTS
        ex = jnp.exp(logits - jnp.max(jnp.where(is_expert, logits, -jnp.inf), axis=-1, keepdims=True))
        mine = jnp.sum(jnp.where(lane == e, ex, 0.0), axis=-1, keepdims=True)
        gate = mine / jnp.sum(jnp.where(is_expert, ex, 0.0), axis=-1, keepdims=True)
        gate_ref[...] = jnp.broadcast_to(gate, gate_ref.shape)

    nxt = jnp.minimum(e + 1, n_e - 1)
    for group in groups:
        for i in range(group[4]):
            gather(nxt, other, j, i, group)
            scatter(e, other, j, i, group)

    half = D_MODEL // 2
    x_a, x_b = xb_ref[:, :half], xb_ref[:, half:]
    g = _dot(x_a, wg_a[0].astype(BF16)) + _dot(x_b, wg_b[0].astype(BF16))
    u = _dot(x_a, wu_a[0].astype(BF16)) + _dot(x_b, wu_b[0].astype(BF16))
    hid = (g * jax.nn.sigmoid(g) * u).astype(BF16)
    acc_ref[:, :half] += _dot(hid, wd_a[0].astype(BF16))
    acc_ref[:, half:] += _dot(hid, wd_b[0].astype(BF16))

    @pl.when(j == n_j - 1)
    def _():
        @pl.when(e >= 1)
        def _():
            wait_all(ybuf, ssem, slot)

        for base, n, src in ((0, capc, 0), (gc * n_j, caps, capc)):
            for r0 in range(0, n, LANES):
                nr = min(LANES, n - r0)
                gate = gate_ref[src + r0:src + r0 + nr, :]
                for kc in range(ROW_TILE):
                    y = acc_ref[src + r0:src + r0 + nr, kc * LANES:(kc + 1) * LANES] * gate
                    ybuf[slot, kc, base + r0:base + r0 + nr, :] = y

        @pl.when(e == n_e - 1)
        def _():
            all_steps(lambda p, group: scatter(e + 1, slot, 0, p, group))
            wait_all(ybuf, ssem, other)
            wait_all(ybuf, ssem, slot)
            wait_all(xbuf, gsem, other)


def _per_step(cap):
    return -(-cap // FF_STEPS)


def _copy_tables(idx, qslot, n_rows):
    n_e, cap = idx.shape
    padded = _per_step(cap) * FF_STEPS
    n_pad = padded - cap
    idx_p = jnp.concatenate([idx, jnp.zeros((n_e, n_pad), I32)], axis=1)
    spare = n_rows + jnp.arange(padded + n_e * n_pad, dtype=I32)
    lead = spare[:padded][None, :]
    pad_rows = spare[padded:].reshape(n_e, n_pad)
    q_p = jnp.concatenate([lead, jnp.concatenate([qslot // ROW_TILE, pad_rows], axis=1)], axis=0) * ROW_TILE
    return idx_p.reshape(-1), q_p.reshape(-1), n_rows + padded + n_e * n_pad


def _moe(idxc, idxs, qc, qs, hc, hs, wrx, w_gate, w_up, w_down):
    capc, caps = idxc.shape[1], idxs.shape[1]
    rows = capc + caps
    tf = FF_TILE
    half = D_MODEL // 2
    idxc, qc, zc_rows = _copy_tables(idxc, qc, N_EXPERTS * capc)
    idxs, qs, zs_rows = _copy_tables(idxs, qs, N_EXPERTS * caps)
    buf_rows = (_per_step(capc) + _per_step(caps)) * FF_STEPS
    any_spec = pl.BlockSpec(memory_space=pl.ANY)
    grid_spec = pltpu.PrefetchScalarGridSpec(
        num_scalar_prefetch=4,
        grid=(N_EXPERTS, FF_STEPS),
        in_specs=[any_spec, any_spec,
                  pl.BlockSpec((D_MODEL, LANES), lambda e, j, *_: (0, 0)),
                  pl.BlockSpec((1, half, tf), lambda e, j, *_: (e, 0, j)),
                  pl.BlockSpec((1, half, tf), lambda e, j, *_: (e, 1, j)),
                  pl.BlockSpec((1, half, tf), lambda e, j, *_: (e, 0, j)),
                  pl.BlockSpec((1, half, tf), lambda e, j, *_: (e, 1, j)),
                  pl.BlockSpec((1, tf, half), lambda e, j, *_: (e, j, 0)),
                  pl.BlockSpec((1, tf, half), lambda e, j, *_: (e, j, 1))],
        out_specs=[any_spec, any_spec],
        scratch_shapes=[pltpu.VMEM((2, ROW_TILE, buf_rows, LANES), F32),
                        pltpu.VMEM((2, ROW_TILE, buf_rows, LANES), F32),
                        pltpu.VMEM((rows, D_MODEL), BF16),
                        pltpu.VMEM((rows, LANES), F32),
                        pltpu.VMEM((rows, D_MODEL), F32),
                        pltpu.SemaphoreType.DMA((2,)),
                        pltpu.SemaphoreType.DMA((2,))])
    return pl.pallas_call(
        functools.partial(_moe_kernel, capc=capc, caps=caps),
        grid_spec=grid_spec,
        out_shape=[jax.ShapeDtypeStruct((zc_rows * ROW_TILE, LANES), F32),
                   jax.ShapeDtypeStruct((zs_rows * ROW_TILE, LANES), F32)],
        compiler_params=_cparams(("arbitrary", "arbitrary")),
        name="expert_ffn",
    )(idxc, idxs, qc, qs, hc, hs, wrx, w_gate, w_gate, w_up, w_up, w_down, w_down)


Z_BUFFERS = 3


def _combine_kernel(clo_ref, chi_ref, z_ref, span_ref, x1_ref, mod_ref, g_ref, o_ref,
                    zbuf, acc_ref, sems, state, *, n_chunks):
    b = pl.program_id(0)
    chunk_rows = ROW_CHUNK * ROW_TILE

    @pl.when(b == 0)
    def _():
        state[0] = 0
        state[1] = 0

    def chunk_copy(c):
        src = z_ref.at[pl.ds(pl.multiple_of(c * chunk_rows, chunk_rows), chunk_rows), :]
        return pltpu.make_async_copy(src, zbuf.at[c % Z_BUFFERS], sems.at[c % Z_BUFFERS])

    acc_ref[...] = jnp.zeros_like(acc_ref)
    first_row = span_ref[:, 0:1] * LANES + span_ref[:, 1:2]
    end_row = span_ref[:, 2:3] * LANES + span_ref[:, 3:4]
    lane_row = lax.broadcasted_iota(I32, (TOK_BLOCK, ROW_CHUNK), 1)

    def body(c, carry):
        for _ in range(Z_BUFFERS):
            nxt = state[0]

            @pl.when(nxt <= jnp.minimum(c + Z_BUFFERS - 1, n_chunks - 1))
            def _():
                chunk_copy(nxt).start()
                state[0] = nxt + 1

        @pl.when(state[1] <= c)
        def _():
            chunk_copy(c).wait()
            state[1] = c + 1

        slot = c % Z_BUFFERS
        row = (c * ROW_CHUNK + lane_row).astype(F32)
        onehot = jnp.where((row >= first_row) & (row < end_row), 1.0, 0.0).astype(BF16)
        y = jnp.concatenate([zbuf[slot, pl.ds(kc, ROW_CHUNK, stride=ROW_TILE), :] for kc in range(ROW_TILE)],
                            axis=1)
        acc_ref[...] += _dot(onehot, y.astype(BF16))
        return carry

    lax.fori_loop(clo_ref[b], chi_ref[b], body, 0)
    gt2 = mod_ref[0][:, 5 * D_MODEL:6 * D_MODEL]
    o_ref[...] = x1_ref[...] + gt2 * _rms(acc_ref[...], g_ref[...], EPS)


def _combine(ranges, z, span, x1, mod3, mod_row, g_post_ffn):
    clo, chi = ranges
    t = x1.shape[0]
    tb = TOK_BLOCK
    grid_spec = pltpu.PrefetchScalarGridSpec(
        num_scalar_prefetch=2,
        grid=(t // tb,),
        in_specs=[pl.BlockSpec(memory_space=pl.ANY),
                  pl.BlockSpec((tb, LANES), lambda b, *_: (b, 0)),
                  pl.BlockSpec((tb, D_MODEL), lambda b, *_: (b, 0)),
                  pl.BlockSpec((1, 1, N_MOD * D_MODEL), lambda b, *_: (mod_row(b, tb), 0, 0)),
                  pl.BlockSpec((1, D_MODEL), lambda b, *_: (0, 0))],
        out_specs=pl.BlockSpec((tb, D_MODEL), lambda b, *_: (b, 0)),
        scratch_shapes=[pltpu.VMEM((Z_BUFFERS, ROW_CHUNK * ROW_TILE, LANES), F32),
                        pltpu.VMEM((tb, D_MODEL), F32),
                        pltpu.SemaphoreType.DMA((Z_BUFFERS,)),
                        pltpu.SMEM((2,), I32)])
    return pl.pallas_call(
        functools.partial(_combine_kernel, n_chunks=2 * t // ROW_CHUNK),
        grid_spec=grid_spec,
        out_shape=jax.ShapeDtypeStruct((t, D_MODEL), F32),
        compiler_params=_cparams(("arbitrary",)),
        name="combine",
    )(clo, chi, z, span, x1, mod3, g_post_ffn)


def _rope_tables(seq):
    half = HEAD_DIM // 4
    freqs = ROPE_THETA ** (-np.arange(half, dtype=np.float64) / half)
    s = np.arange(seq)
    row = (s // GRID_W)[:, None] * freqs[None, :]
    col = (s % GRID_W)[:, None] * freqs[None, :]
    ang = np.concatenate([row, row, col, col], axis=1)
    ang = np.tile(ang, (1, QK_W // HEAD_DIM))
    lane = np.arange(QK_W)
    sign = np.where((lane % 32) < 16, -1.0, 1.0)[None, :]
    return (jnp.asarray(np.cos(ang), dtype=F32), jnp.asarray(np.sin(ang) * sign, dtype=F32))


def _combine_ranges(rows, n_tok):
    step = TOK_BLOCK // LANES
    nb = n_tok // TOK_BLOCK
    lo = rows[0:nb * step:step, 0]
    hi = jnp.concatenate([lo[1:], jnp.full((1,), 2 * n_tok, I32)])
    return (lo // ROW_CHUNK).astype(I32), ((hi + ROW_CHUNK - 1) // ROW_CHUNK).astype(I32)


def kernel(x_prompt, x_sample, c, cache_k, cache_v, c_ctx, w_mod, b_mod, g_pre_mix, g_post_mix, g_pre_ffn, g_post_ffn, w_in, lam_q1, lam_k1, lam_q2, lam_k2, g_subln, w_proj_attn, w_proj_fourier, w_out, w_router, w_gate, w_up, w_down):
    assert w_mod.shape[0] == 1
    lam_init = 0.8 - 0.6 * math.exp(-0.3 * 0)
    bp, sp, _ = x_prompt.shape
    bs, ss, _ = x_sample.shape

    cond8 = jnp.concatenate([c_ctx[None, :], c, jnp.zeros((8 - 1 - bs, D_MODEL), F32)], axis=0)
    mod3 = _modulation(cond8, w_mod[0], b_mod).reshape(8, 1, N_MOD * D_MODEL)

    w_in_b = None
    wpa =w_proj_attn[0].astype(BF16)
    wpf = w_proj_fourier[0].astype(BF16)
    wout = w_out[0].astype(BF16)
    wr = w_router[0].astype(BF16)
    wrx = jnp.concatenate([wr, jnp.zeros((D_MODEL, LANES - N_EXPERTS), BF16)], axis=1)
    lam_p = jnp.concatenate([lam_q1, lam_k1, lam_q2, lam_k2], axis=0)

    groups = []
    for x, seq, positional, ctx in ((x_sample, ss, True, (cache_k, cache_v)),
                                    (x_prompt, sp, False, None)):
        nb = x.shape[0]
        t = nb * seq
        x2d = x.reshape(t, D_MODEL)
        if positional:
            mod_row = lambda i, tm, seq=seq: 1 + (i * tm) // seq
        else:
            mod_row = lambda i, tm: 0
        self_contained = ctx is None and not positional
        if self_contained:
            assert PRE_BLOCK % seq == 0
            new_k, new_v, x1, h2t, aff_t = _ctx_mixer(x2d, mod3, mod_row, g_pre_mix, w_in_b, seq,
                                                      (lam_p, g_subln, lam_init),
                                                      (g_post_mix, g_pre_ffn, wpa, wpf, wout, wrx), tm=PRE_BLOCK)
        else:
            q, k, v, f, ga, gf, w_in_b = _pre_mixer(x2d, mod3, mod_row, g_pre_mix, w_in[0],
                                                    _rope_tables(seq) if positional else None, seq, tm=ROW_BLOCK)
            o = _attention(lam_p, g_subln, q, k, v, ctx, seq, lam_init)
            fm = _fourier(f, seq)
            x1, h2t, aff_t = _post_mixer(o, fm, ga, gf, x2d, mod3, mod_row, g_post_mix, g_pre_ffn,
                                         wpa, wpf, wout, wrx)
        cap = 2 * t // N_EXPERTS
        assert t <= 1 << TOKEN_BITS
        posm, pack, span, ranges, rows = _route(aff_t, cap)
        idx, qslot = _slot_lists(ranges, posm, pack, cap)
        groups.append(dict(x1=x1, h2t=h2t, idx=idx, qslot=qslot, span=span, ranges=_combine_ranges(rows, t),
                           mod_row=mod_row, cache=(new_k, new_v) if self_contained else None, shape=x.shape))

    gs_, gc = groups
    zc, zs = _moe(gc["idx"], gs_["idx"], gc["qslot"], gs_["qslot"], gc["h2t"], gs_["h2t"], wrx,
                  w_gate[0], w_up[0], w_down[0])
    outs = []
    for g, z in ((gc, zc), (gs_, zs)):
        out = _combine(g["ranges"], z, g["span"], g["x1"], mod3, g["mod_row"], g_post_ffn)
        outs.append(out.reshape(g["shape"]))
    new_k, new_v = gc["cache"]
    return (outs[0], outs[1], new_k, new_v)
```

```python
import functools
import math

import numpy as np
import jax
import jax.numpy as jnp
from jax import lax
from jax.experimental import pallas as pl
from jax.experimental.pallas import tpu as pltpu

F32 = jnp.float32
BF16 = jnp.bfloat16
I32 = jnp.int32

D_MODEL = 1024
N_HEADS = 6
HEAD_DIM = 64
V_DIM = 128
QK_W = 768
FOUR_W = 256
FOUR_G = 64
IN_W = 4608
N_EXPERTS = 16
D_FF = 2816
N_MOD = 6
GRID_W = 64
ROPE_THETA = 10000.0
EPS = 1e-6
SUBLN_EPS = 1e-5

LANES = 128
ROW_BLOCK = 256
PRE_BLOCK = 512
GATE_CHUNK = 256
POST_BLOCK = 1024
POST_SUB = 256
WEIGHT_ROWS = 128
TOK_BLOCK = 256
ROW_CHUNK = 256
FF_TILE = 256
FF_STEPS = D_FF // FF_TILE
ROW_TILE = D_MODEL // LANES
TOKEN_BITS = 13
VMEM_LIMIT = 56 * 1024 * 1024


def _cparams(sem):
    return pltpu.CompilerParams(dimension_semantics=sem, vmem_limit_bytes=VMEM_LIMIT)


def _dot(a, b):
    return jnp.dot(a, b, preferred_element_type=F32)


def _rms(x, g, eps):
    return x * lax.rsqrt(jnp.mean(x * x, axis=-1, keepdims=True) + eps) * g


def _whole_vmem():
    return pl.BlockSpec(memory_space=pltpu.MemorySpace.VMEM)


def _cast_weight(w_hbm, wb_ref, stage_ref, sem):
    rows = stage_ref.shape[1]
    chunks = [pltpu.make_async_copy(w_hbm.at[pl.ds(r0, rows), :], stage_ref.at[n % 2], sem.at[n % 2])
              for n, r0 in enumerate(range(0, w_hbm.shape[0], rows))]
    chunks[0].start()
    for n, chunk in enumerate(chunks):
        if n + 1 < len(chunks):
            chunks[n + 1].start()
        chunk.wait()
        wb_ref[n * rows:(n + 1) * rows, :] = stage_ref[n % 2].astype(BF16)


MOD_ROWS = 8


def _mod_kernel(cc_ref, c_ref, w_ref, b_ref, o_ref):
    pad = jnp.zeros((MOD_ROWS - 1 - c_ref.shape[0], D_MODEL), F32)
    c = jnp.concatenate([cc_ref[...], c_ref[...], pad], axis=0)
    s = c * jax.nn.sigmoid(c)
    res = _dot(s.astype(BF16), w_ref[...].astype(BF16)) + b_ref[...]
    for r in range(MOD_ROWS):
        o_ref[r] = res[r:r + 1, :]


def _modulation(c_ctx, c, w_mod, b_mod):
    tn = 1024
    n = N_MOD * D_MODEL
    assert 1 + c.shape[0] <= MOD_ROWS
    return pl.pallas_call(
        _mod_kernel,
        grid=(n // tn,),
        in_specs=[pl.BlockSpec((1, D_MODEL), lambda j: (0, 0)),
                  pl.BlockSpec(c.shape, lambda j: (0, 0)),
                  pl.BlockSpec((D_MODEL, tn), lambda j: (0, j)),
                  pl.BlockSpec((1, tn), lambda j: (0, j))],
        out_specs=pl.BlockSpec((MOD_ROWS, 1, tn), lambda j: (0, 0, j)),
        out_shape=jax.ShapeDtypeStruct((MOD_ROWS, 1, n), F32),
        compiler_params=_cparams(("arbitrary",)),
        name="modulation",
    )(c_ctx[None, :], c, w_mod, b_mod)


def _diff_lambda(lp, lam_init):
    s1 = jnp.sum(lp[0:1] * lp[1:2], axis=-1, keepdims=True)
    s2 = jnp.sum(lp[2:3] * lp[3:4], axis=-1, keepdims=True)
    return jnp.exp(s1) - jnp.exp(s2) + lam_init


def _attention_weights(q, k):
    comp1 = lax.broadcasted_iota(I32, (1, V_DIM), 1) < HEAD_DIM
    qs = q * jnp.asarray(HEAD_DIM ** -0.5, BF16)
    zero = jnp.zeros_like(qs)

    def weights(qc):
        s = lax.dot_general(qc, k, (((1,), (1,)), ((), ())), preferred_element_type=F32)
        return jnp.exp(s - jnp.max(s, axis=-1, keepdims=True)).astype(BF16)

    return weights(jnp.where(comp1, qs, zero)), weights(jnp.where(comp1, zero, qs))


def _attention_output(weights, v, lam, g_subln, lam_init):
    v_ones = jnp.concatenate([v, jnp.ones_like(v)], axis=1)

    def attend(ex):
        ov = _dot(ex, v_ones)
        return ov[:, 0:V_DIM] / ov[:, V_DIM:2 * V_DIM]

    o = attend(weights[0]) - lam * attend(weights[1])
    return (_rms(o, g_subln, SUBLN_EPS) * (1.0 - lam_init)).astype(BF16)


def _rope(z, cos, sin_signed, first_half):
    fwd = pltpu.roll(z, QK_W - 16, axis=1)
    bwd = pltpu.roll(z, 16, axis=1)
    return z * cos + jnp.where(first_half, fwd, bwd) * sin_signed


def _pre_kernel(*refs, positional):
    it = iter(refs)
    x_ref, mod_ref, g_ref, w_hbm = next(it), next(it), next(it), next(it)
    if positional:
        cos_ref, sin_ref = next(it), next(it)
    q_ref, k_ref, v_ref, f_ref, ga_ref, gf_ref = (next(it) for _ in range(6))
    wb_hbm, w_ref, stage_ref, sem = next(it), next(it), next(it), next(it)
    weight_out = pltpu.make_async_copy(w_ref, wb_hbm, sem.at[2])

    @pl.when(pl.program_id(0) == 0)
    def _():
        _cast_weight(w_hbm, w_ref, stage_ref, sem)
        weight_out.start()

    @pl.when(pl.program_id(0) == pl.num_programs(0) - 1)
    def _():
        weight_out.wait()

    m = mod_ref[0]
    sh1 = m[:, 0:D_MODEL]
    sc1 = m[:, D_MODEL:2 * D_MODEL]
    h = _rms(x_ref[...], g_ref[...], EPS) * (1.0 + sc1) + sh1
    hb = h.astype(BF16)

    def proj(lo, hi):
        return _dot(hb, w_ref[:, lo:hi])

    zq = proj(0, QK_W)
    zk = proj(QK_W, 2 * QK_W)
    zv = proj(2 * QK_W, 3 * QK_W)
    if positional:
        lane = lax.broadcasted_iota(I32, (1, QK_W), 1)
        first_half = (lane % 32) < 16
        cos = cos_ref[...]
        sin_signed = sin_ref[...]
        zq = _rope(zq, cos, sin_signed, first_half)
        zk = _rope(zk, cos, sin_signed, first_half)
    q_ref[...] = zq.astype(BF16)
    k_ref[...] = zk.astype(BF16)
    v_ref[...] = zv.astype(BF16)
    f0 = 3 * QK_W
    g0 = f0 + FOUR_W
    f_ref[...] = proj(f0, g0)
    ga_ref[...] = jax.nn.sigmoid(proj(g0, g0 + D_MODEL)).astype(BF16)
    gf_ref[...] = jax.nn.sigmoid(proj(g0 + D_MODEL, IN_W)).astype(BF16)


def _ctx_kernel(x_ref, mod_ref, g_ref, w_ref, bc_ref, bs_ref, cs_ref, ss_ref, lam_ref, gs_ref,
                o_ref, f_ref, ga_ref, gf_ref, kc_ref, vc_ref, *, seq, lam_init):
    m = mod_ref[0]
    sh1 = m[:, 0:D_MODEL]
    sc1 = m[:, D_MODEL:2 * D_MODEL]
    hb = (_rms(x_ref[...], g_ref[...], EPS) * (1.0 + sc1) + sh1).astype(BF16)
    seqs = [slice(b * seq, (b + 1) * seq) for b in range(x_ref.shape[0] // seq)]
    lam = _diff_lambda(lam_ref[...], lam_init)
    f0 = 3 * QK_W
    g0 = f0 + FOUR_W

    def proj(lo, hi):
        return _dot(hb, w_ref[:, lo:hi])

    q, k, v = {}, {}, {}

    def qkv_chunk(store, cache_ref, lo, c):
        cols = slice(c * GATE_CHUNK, (c + 1) * GATE_CHUNK)
        z = proj(lo + cols.start, lo + cols.stop)
        for hd in range(cols.start // V_DIM, cols.stop // V_DIM):
            z_hd = z[:, hd * V_DIM - cols.start:(hd + 1) * V_DIM - cols.start]
            if cache_ref is not None:
                for b, rs in enumerate(seqs):
                    cache_ref[b, 0, hd] = z_hd[rs, :]
            store[hd] = z_hd.astype(BF16)

    def fourier_chunk():
        f = proj(f0, g0)
        for rs in seqs:
            f_ref[rs, :] = _dft_real(f[rs, :], bc_ref, bs_ref, cs_ref, ss_ref)

    def gate_chunk(ref, lo, c):
        cols = slice(c * GATE_CHUNK, (c + 1) * GATE_CHUNK)
        ref[:, cols] = jax.nn.sigmoid(proj(lo + cols.start, lo + cols.stop)).astype(BF16)

    for store, cache, lo in ((q, None, 0), (k, kc_ref, QK_W), (v, vc_ref, 2 * QK_W)):
        for c in range(QK_W // GATE_CHUNK):
            qkv_chunk(store, cache, lo, c)
    matmul_work = [fourier_chunk] + [functools.partial(gate_chunk, ref, lo, c)
                                     for ref, lo in ((ga_ref, g0), (gf_ref, g0 + D_MODEL))
                                     for c in range(D_MODEL // GATE_CHUNK)]
    for rs in seqs:
        for hd in range(N_HEADS):
            weights = _attention_weights(q[hd][rs, :], k[hd][rs, :])
            if matmul_work:
                matmul_work.pop(0)()
            o_ref[rs, hd * V_DIM:(hd + 1) * V_DIM] = _attention_output(weights, v[hd][rs, :], lam, gs_ref[...],
                                                                      lam_init)
    for work in matmul_work:
        work()


def _ctx_mixer(x2d, mod3, mod_row, g_pre, w_in_b, seq, attn_params, tm):
    t = x2d.shape[0]
    lam_p, g_subln, lam_init = attn_params
    assert tm % seq == 0 and t % tm == 0
    row = lambda i: (i, 0)
    const = lambda i: (0, 0)
    consts = _dft_consts(seq) + (lam_p, g_subln)
    nb = t // seq
    cshape = jax.ShapeDtypeStruct((nb, 1, N_HEADS, seq, V_DIM), F32)
    cspec = pl.BlockSpec((tm // seq, 1, N_HEADS, seq, V_DIM), lambda i: (i, 0, 0, 0, 0))
    return pl.pallas_call(
        functools.partial(_ctx_kernel, seq=seq, lam_init=lam_init),
        grid=(t // tm,),
        in_specs=[pl.BlockSpec((tm, D_MODEL), row),
                  pl.BlockSpec((1, 1, N_MOD * D_MODEL), lambda i: (mod_row(i, tm), 0, 0)),
                  pl.BlockSpec((1, D_MODEL), const),
                  _whole_vmem()] + [pl.BlockSpec(c.shape, const) for c in consts],
        out_specs=[pl.BlockSpec((tm, QK_W), row),
                   pl.BlockSpec((tm, FOUR_W), row),
                   pl.BlockSpec((tm, D_MODEL), row),
                   pl.BlockSpec((tm, D_MODEL), row),
                   cspec, cspec],
        out_shape=[jax.ShapeDtypeStruct((t, QK_W), BF16),
                   jax.ShapeDtypeStruct((t, FOUR_W), BF16),
                   jax.ShapeDtypeStruct((t, D_MODEL), BF16),
                   jax.ShapeDtypeStruct((t, D_MODEL), BF16),
                   cshape, cshape],
        compiler_params=_cparams(("arbitrary",)),
        name="ctx_mixer",
    )(x2d, mod3, g_pre, w_in_b, *consts)


def _pre_mixer(x2d, mod3, mod_row, g_pre, w_in, rope_tabs, seq, tm):
    t = x2d.shape[0]
    positional = rope_tabs is not None
    assert seq % tm == 0
    blocks_per_seq = seq // tm
    row = lambda i: (i, 0)
    any_spec = pl.BlockSpec(memory_space=pl.ANY)
    in_specs = [pl.BlockSpec((tm, D_MODEL), row),
                pl.BlockSpec((1, 1, N_MOD * D_MODEL), lambda i: (mod_row(i, tm), 0, 0)),
                pl.BlockSpec((1, D_MODEL), lambda i: (0, 0)),
                any_spec]
    args = [x2d, mod3, g_pre, w_in]
    if positional:
        in_specs += [pl.BlockSpec((tm, QK_W), lambda i: (i % blocks_per_seq, 0))] * 2
        args += list(rope_tabs)
    out_shape = [jax.ShapeDtypeStruct((t, QK_W), BF16)] * 3 + [
        jax.ShapeDtypeStruct((t, FOUR_W), F32),
        jax.ShapeDtypeStruct((t, D_MODEL), BF16),
        jax.ShapeDtypeStruct((t, D_MODEL), BF16),
        jax.ShapeDtypeStruct(w_in.shape, BF16)]
    out_specs = [pl.BlockSpec((tm, QK_W), row)] * 3 + [
        pl.BlockSpec((tm, FOUR_W), row),
        pl.BlockSpec((tm, D_MODEL), row),
        pl.BlockSpec((tm, D_MODEL), row),
        any_spec]
    return pl.pallas_call(
        functools.partial(_pre_kernel, positional=positional),
        grid=(t // tm,),
        in_specs=in_specs,
        out_specs=out_specs,
        out_shape=out_shape,
        scratch_shapes=[pltpu.VMEM(w_in.shape, BF16),
                        pltpu.VMEM((2, WEIGHT_ROWS, w_in.shape[1]), F32),
                        pltpu.SemaphoreType.DMA((3,))],
        compiler_params=_cparams(("arbitrary",)),
        name="pre_mixer",
    )(*args)


def _attn_kernel(*refs, lam_init, has_ctx):
    it = iter(refs)
    lam_ref, gs_ref, q_ref, k_ref, v_ref = (next(it) for _ in range(5))
    if has_ctx:
        ck_ref, cv_ref = next(it), next(it)
    o_ref = next(it)

    lam = _diff_lambda(lam_ref[...], lam_init)

    def operand(ref, cache_ref, hd):
        x = ref[:, hd * V_DIM:(hd + 1) * V_DIM]
        return jnp.concatenate([cache_ref[0, 0, hd].astype(BF16), x], axis=0) if has_ctx else x

    def head_weights(hd):
        return _attention_weights(q_ref[:, hd * V_DIM:(hd + 1) * V_DIM], operand(k_ref, ck_ref if has_ctx else None, hd))

    weights = head_weights(0)
    for hd in range(N_HEADS):
        nxt = head_weights(hd + 1) if hd + 1 < N_HEADS else None
        v = operand(v_ref, cv_ref if has_ctx else None, hd)
        o_ref[:, hd * V_DIM:(hd + 1) * V_DIM] = _attention_output(weights, v, lam, gs_ref[...], lam_init)
        weights = nxt


def _attention(lam_p, g_subln, q, k, v, ctx, seq, lam_init):
    t = q.shape[0]
    tq = ROW_BLOCK
    qb = seq // tq
    has_ctx = ctx is not None
    in_specs = [pl.BlockSpec((4, HEAD_DIM), lambda b, i: (0, 0)),
                pl.BlockSpec((1, V_DIM), lambda b, i: (0, 0)),
                pl.BlockSpec((tq, QK_W), lambda b, i: (b * qb + i, 0)),
                pl.BlockSpec((seq, QK_W), lambda b, i: (b, 0)),
                pl.BlockSpec((seq, QK_W), lambda b, i: (b, 0))]
    args = [lam_p, g_subln, q, k, v]
    if has_ctx:
        past = ctx[0].shape[3]
        cspec = pl.BlockSpec((1, 1, N_HEADS, past, V_DIM), lambda b, i: (b, 0, 0, 0, 0))
        in_specs += [cspec, cspec]
        args += list(ctx)
    return pl.pallas_call(
        functools.partial(_attn_kernel, lam_init=lam_init, has_ctx=has_ctx),
        grid=(t // seq, qb),
        in_specs=in_specs,
        out_specs=pl.BlockSpec((tq, QK_W), lambda b, i: (b * qb + i, 0)),
        out_shape=jax.ShapeDtypeStruct((t, QK_W), BF16),
        compiler_params=_cparams(("arbitrary", "arbitrary")),
        name="diff_attention",
    )(*args)


def _dft_real(f, bc_ref, bs_ref, cs_ref, ss_ref):
    fb = f.astype(BF16)
    u = _dot(fb, bc_ref[...].astype(BF16)).astype(BF16)
    w = _dot(fb, bs_ref[...].astype(BF16)).astype(BF16)
    return (_dot(cs_ref[...].astype(BF16), u) - _dot(ss_ref[...].astype(BF16), w)).astype(BF16)


def _fourier_kernel(f_ref, bc_ref, bs_ref, cs_ref, ss_ref, o_ref):
    o_ref[...] = _dft_real(f_ref[...], bc_ref, bs_ref, cs_ref, ss_ref)


def _dft_consts(seq):
    c = np.arange(FOUR_G)
    ang_c = 2.0 * np.pi * ((c[:, None] * c[None, :]) % FOUR_G) / FOUR_G
    eye = np.eye(FOUR_W // FOUR_G)
    bc = np.kron(eye, np.cos(ang_c)) / math.sqrt(FOUR_G)
    bs = np.kron(eye, np.sin(ang_c)) / math.sqrt(FOUR_G)
    s = np.arange(seq)
    ang_s = 2.0 * np.pi * ((s[:, None] * s[None, :]) % seq) / seq
    cs = np.cos(ang_s) / math.sqrt(seq)
    ss = np.sin(ang_s) / math.sqrt(seq)
    return tuple(jnp.asarray(a, dtype=F32) for a in (bc, bs, cs, ss))


def _fourier(f, seq):
    t = f.shape[0]
    bc, bs, cs, ss = _dft_consts(seq)
    const = lambda b: (0, 0)
    return pl.pallas_call(
        _fourier_kernel,
        grid=(t // seq,),
        in_specs=[pl.BlockSpec((seq, FOUR_W), lambda b: (b, 0)),
                  pl.BlockSpec((FOUR_W, FOUR_W), const),
                  pl.BlockSpec((FOUR_W, FOUR_W), const),
                  pl.BlockSpec((seq, seq), const),
                  pl.BlockSpec((seq, seq), const)],
        out_specs=pl.BlockSpec((seq, FOUR_W), lambda b: (b, 0)),
        out_shape=jax.ShapeDtypeStruct((t, FOUR_W), BF16),
        compiler_params=_cparams(("arbitrary",)),
        name="fourier_mix",
    )(f, bc, bs, cs, ss)


def _post_kernel(o_ref, fm_ref, ga_ref, gf_ref, x_ref, mod_ref, gpost_ref, gffn_ref,
                 wpa_ref, wpf_ref, wout_ref, wrx_ref,
                 x1_ref, h2t_ref, afft_ref):
    m = mod_ref[0]
    gt1 = m[:, 2 * D_MODEL:3 * D_MODEL]
    sh2 = m[:, 3 * D_MODEL:4 * D_MODEL]
    sc2 = m[:, 4 * D_MODEL:5 * D_MODEL]
    subs = [slice(r0, r0 + POST_SUB) for r0 in range(0, o_ref.shape[0], POST_SUB)]
    ab = [(_dot(o_ref[rs, :], wpa_ref[...]), _dot(fm_ref[rs, :], wpf_ref[...])) for rs in subs]
    merged = [(ga_ref[rs, :] * a + gf_ref[rs, :] * b).astype(BF16) for rs, (a, b) in zip(subs, ab)]
    ys = [_dot(mg, wout_ref[...]) for mg in merged]
    h2s = []
    for rs, y in zip(subs, ys):
        x1 = x_ref[rs, :] + gt1 * _rms(y, gpost_ref[...], EPS)
        x1_ref[rs, :] = x1
        h2s.append(_rms(x1, gffn_ref[...], EPS) * (1.0 + sc2) + sh2)
    logits = [_dot(h2.astype(BF16), wrx_ref[...]) for h2 in h2s]
    for rs, h2, lg in zip(subs, h2s, logits):
        lt = lg.T[0:N_EXPERTS]
        et = jnp.exp(lt - jnp.max(lt, axis=0, keepdims=True))
        aff = et / jnp.sum(et, axis=0, keepdims=True)
        for u in range(POST_SUB // LANES):
            afft_ref[:, rs.start // LANES + u, :] = aff[:, u * LANES:(u + 1) * LANES]
        for kc in range(ROW_TILE):
            h2t_ref[pl.ds(rs.start * ROW_TILE + kc, POST_SUB, stride=ROW_TILE), :] = h2[:, kc * LANES:(kc + 1) * LANES]


def _post_mixer(o, fm, ga, gf, x2d, mod3, mod_row, g_post, g_ffn, wpa, wpf, wout, wrx):
    t = x2d.shape[0]
    tm = POST_BLOCK
    row = lambda i: (i, 0)
    const = lambda i: (0, 0)
    return pl.pallas_call(
        _post_kernel,
        grid=(t // tm,),
        in_specs=[pl.BlockSpec((tm, QK_W), row),
                  pl.BlockSpec((tm, FOUR_W), row),
                  pl.BlockSpec((tm, D_MODEL), row),
                  pl.BlockSpec((tm, D_MODEL), row),
                  pl.BlockSpec((tm, D_MODEL), row),
                  pl.BlockSpec((1, 1, N_MOD * D_MODEL), lambda i: (mod_row(i, tm), 0, 0)),
                  pl.BlockSpec((1, D_MODEL), const),
                  pl.BlockSpec((1, D_MODEL), const),
                  pl.BlockSpec((QK_W, D_MODEL), const),
                  pl.BlockSpec((FOUR_W, D_MODEL), const),
                  pl.BlockSpec((D_MODEL, D_MODEL), const),
                  pl.BlockSpec((D_MODEL, LANES), const)],
        out_specs=[pl.BlockSpec((tm, D_MODEL), row),
                   pl.BlockSpec((tm * ROW_TILE, LANES), row),
                   pl.BlockSpec((N_EXPERTS, tm // LANES, LANES), lambda i: (0, i, 0))],
        out_shape=[jax.ShapeDtypeStruct((t, D_MODEL), F32),
                   jax.ShapeDtypeStruct((t * ROW_TILE, LANES), F32),
                   jax.ShapeDtypeStruct((N_EXPERTS, t // LANES, LANES), F32)],
        compiler_params=_cparams(("arbitrary",)),
        name="post_mixer",
    )(o, fm, ga, gf, x2d, mod3, g_post, g_ffn, wpa, wpf, wout, wrx)


RANGE_ROWS = 8


def _route_kernel(aff_ref, posm_ref, pack_ref, span_ref, ranges_ref, rows_ref, *, cap, n_tok):
    aff = aff_ref[...]
    nt = n_tok // LANES
    capf = float(cap)

    def count_ge(v):
        return jnp.sum(jnp.where(aff >= v, 1.0, 0.0), axis=(1, 2), keepdims=True)

    def search(i, thr):
        cand = thr | jnp.left_shift(jnp.int32(1), 30 - i)
        return jnp.where(count_ge(pltpu.bitcast(cand, F32)) >= capf, cand, thr)

    thr = lax.fori_loop(0, 31, search, jnp.zeros((N_EXPERTS, 1, 1), I32))
    lo = pltpu.bitcast(thr, F32)
    hi = pltpu.bitcast(thr + 1, F32)

    def refine(i, c):
        lo, hi = c
        mid = lo + (hi - lo) * 0.5
        ok = count_ge(mid) >= capf
        return jnp.where(ok, mid, lo), jnp.where(ok, hi, mid)

    lo, hi = lax.fori_loop(0, 12, refine, (lo, hi))
    gt = aff >= hi
    eq = (aff >= lo) & (aff < hi)
    n_tie = capf - jnp.sum(jnp.where(gt, 1.0, 0.0), axis=(1, 2), keepdims=True)

    sq0 = lax.broadcasted_iota(I32, (LANES, LANES), 0)
    sq1 = lax.broadcasted_iota(I32, (LANES, LANES), 1)
    along_total = jnp.concatenate([jnp.where(sq0 <= sq1, 1.0, 0.0), jnp.ones((LANES, LANES), F32)],
                                  axis=1).astype(BF16)
    m = N_EXPERTS * nt
    r0 = lax.broadcasted_iota(I32, (m, m), 0)
    r1 = lax.broadcasted_iota(I32, (m, m), 1)
    earlier = jnp.where((r0 // nt == r1 // nt) & (r1 < r0), 1.0, 0.0).astype(BF16)
    lane = lax.broadcasted_iota(I32, (1, LANES), 1)
    token = lax.broadcasted_iota(I32, (nt, LANES), 0) * LANES + lane

    def tile_counts(x):
        both = _dot(x.reshape(m, LANES).astype(BF16), along_total)
        total = both[:, LANES:]
        before = _dot(earlier, total.astype(BF16))
        shape = (N_EXPERTS, nt, LANES)
        return both[:, :LANES].reshape(shape), total.reshape(shape), before.reshape(shape)

    eq_f = jnp.where(eq, 1.0, 0.0)
    eq_along, _, eq_before = tile_counts(eq_f)
    sel = jnp.where(gt, 1.0, jnp.where(eq_along + eq_before <= n_tie, eq_f, 0.0))
    sel_along, sel_total, sel_before = tile_counts(sel)
    posm_ref[...] = jnp.where(sel > 0.5, sel_along + sel_before - sel, -1.0).astype(I32)

    cnt = jnp.sum(sel, axis=0)
    rows_before = jnp.sum(sel_before, axis=0)
    tok_start = _dot(cnt.astype(BF16), along_total[:, :LANES]) - cnt + rows_before
    k = jnp.zeros((nt, LANES), F32)
    slots_before = jnp.zeros((nt, LANES), F32)
    slots_here = jnp.zeros((nt, LANES), F32)
    for e in range(N_EXPERTS):
        pack_ref[e] = (tok_start + k).astype(I32) * (1 << TOKEN_BITS) + token
        k = k + sel[e]
        slots_before = jnp.where(lane == e, sel_before[e], slots_before)
        slots_here = jnp.where(lane == e, sel_total[e], slots_here)

    eye = jnp.where(sq0 == sq1, 1.0, 0.0).astype(BF16)
    tok_end = tok_start + cnt
    parts = []
    for v in (tok_start, tok_end):
        high = jnp.floor(v * (1.0 / LANES))
        parts += [high, v - high * LANES]
    for c in range(nt):
        rows4 = jnp.zeros((LANES, LANES), F32)
        for r, part in enumerate(parts):
            rows4 = jnp.where(sq0 == r, jnp.broadcast_to(part[c:c + 1, :], (LANES, LANES)), rows4)
        span_ref[c * LANES:(c + 1) * LANES, :] = lax.dot_general(
            eye, rows4.astype(BF16), (((1,), (1,)), ((), ())), preferred_element_type=F32)

    ranges_ref[...] = jnp.zeros_like(ranges_ref)
    for s in range(cap // LANES):
        done = jnp.where(slots_before + slots_here <= float(s * LANES), 1.0, 0.0)
        begun = jnp.where(slots_before < float((s + 1) * LANES), 1.0, 0.0)
        ranges_ref[s:s + 1, :] = jnp.sum(done, axis=0, keepdims=True).astype(I32)
        ranges_ref[RANGE_ROWS + s:RANGE_ROWS + s + 1, :] = jnp.sum(begun, axis=0, keepdims=True).astype(I32)
    rows_ref[...] = rows_before.astype(I32)


def _route(aff, cap):
    n_e, nt, _ = aff.shape
    n_tok = nt * LANES
    assert cap // LANES <= RANGE_ROWS
    return pl.pallas_call(
        functools.partial(_route_kernel, cap=cap, n_tok=n_tok),
        out_shape=[jax.ShapeDtypeStruct((n_e, nt, LANES), I32),
                   jax.ShapeDtypeStruct((n_e, nt, LANES), I32),
                   jax.ShapeDtypeStruct((n_tok, LANES), F32),
                   jax.ShapeDtypeStruct((2 * RANGE_ROWS, LANES), I32),
                   jax.ShapeDtypeStruct((nt, LANES), I32)],
        compiler_params=pltpu.CompilerParams(vmem_limit_bytes=VMEM_LIMIT),
        name="route",
    )(aff)


def _slots_kernel(ranges_ref, posm_ref, pack_ref, idx_ref, qslot_ref, acc_ref):
    e = pl.program_id(0)
    sub = lax.broadcasted_iota(I32, (LANES, LANES), 0)
    eye = sub == lax.broadcasted_iota(I32, (LANES, LANES), 1)
    n_tiles = idx_ref.shape[1]

    for s in range(n_tiles):
        slot = sub + s * LANES

        def body(c, acc, slot=slot):
            hit = posm_ref[e, pl.ds(c, 1), :] == slot
            return acc + jnp.where(hit, pack_ref[e, pl.ds(c, 1), :], 0)

        acc_ref[s] = lax.fori_loop(ranges_ref[s, e], ranges_ref[RANGE_ROWS + s, e], body,
                                   jnp.zeros((LANES, LANES), I32))

    def as_row(part):
        col = jnp.sum(part.astype(F32), axis=1, keepdims=True)
        return jnp.sum(jnp.where(eye, col, 0.0), axis=0, keepdims=True).astype(I32)

    for s in range(n_tiles):
        acc = acc_ref[s]
        idx_ref[0, s:s + 1, :] = as_row(acc & ((1 << TOKEN_BITS) - 1)) * ROW_TILE
        qslot_ref[0, s:s + 1, :] = as_row(lax.shift_right_logical(acc, TOKEN_BITS)) * ROW_TILE


def _slot_lists(ranges, posm4, qdst4, cap):
    ns = cap // LANES
    grid_spec = pltpu.PrefetchScalarGridSpec(
        num_scalar_prefetch=1,
        grid=(N_EXPERTS,),
        in_specs=[_whole_vmem(), _whole_vmem()],
        out_specs=[pl.BlockSpec((1, ns, LANES), lambda e, *_: (e, 0, 0)),
                   pl.BlockSpec((1, ns, LANES), lambda e, *_: (e, 0, 0))],
        scratch_shapes=[pltpu.VMEM((ns, LANES, LANES), I32)])
    idx, qslot = pl.pallas_call(
        _slots_kernel,
        grid_spec=grid_spec,
        out_shape=[jax.ShapeDtypeStruct((N_EXPERTS, ns, LANES), I32),
                   jax.ShapeDtypeStruct((N_EXPERTS, ns, LANES), I32)],
        compiler_params=_cparams(("arbitrary",)),
        name="slot_lists",
    )(ranges, posm4, qdst4)
    return idx.reshape(N_EXPERTS, cap), qslot.reshape(N_EXPERTS, cap)


def _moe_kernel(idxc_ref, idxs_ref, qc_ref, qs_ref,
                hc_ref, hs_ref, wr_ref, wg_ref, wu_ref, wd_ref, zc_ref, zs_ref,
                xbuf, ybuf, xb_ref, gate_ref, acc_ref, gsem, ssem, *, capc, caps):
    e = pl.program_id(0)
    j = pl.program_id(1)
    n_e = pl.num_programs(0)
    n_j = FF_STEPS
    slot = e % 2
    other = 1 - slot
    rows = capc + caps
    gc, gs = _per_step(capc), _per_step(caps)
    groups = ((hc_ref, idxc_ref, zc_ref, qc_ref, gc, 0), (hs_ref, idxs_ref, zs_ref, qs_ref, gs, gc * n_j))

    def tile(ref, first_sublane):
        return ref.at[pl.ds(pl.multiple_of(first_sublane, ROW_TILE), ROW_TILE), :]

    def gather(ex, sl, step, i, group):
        h_ref, idx_ref, _, _, per_step, base = group
        p = step * per_step + i
        src = tile(h_ref, idx_ref[ex * (per_step * n_j) + p])
        pltpu.make_async_copy(src, xbuf.at[sl, :, base + p, :], gsem.at[sl]).start()

    def scatter(table_row, sl, step, i, group):
        _, _, z_ref, q_ref, per_step, base = group
        p = step * per_step + i
        dst = tile(z_ref, q_ref[table_row * (per_step * n_j) + p])
        pltpu.make_async_copy(ybuf.at[sl, :, base + p, :], dst, ssem.at[sl]).start()

    def all_steps(fn):
        for group in groups:
            def body(p, carry, group=group):
                fn(p, group)
                return carry
            lax.fori_loop(0, group[4] * n_j, body, 0, unroll=8)

    def wait_all(buf, sem, sl):
        pltpu.make_async_copy(buf.at[sl], buf.at[sl], sem.at[sl]).wait()

    @pl.when((e == 0) & (j == 0))
    def _():
        ybuf[...] = jnp.zeros_like(ybuf)
        all_steps(lambda p, group: gather(0, 0, 0, p, group))

    @pl.when(j == 0)
    def _():
        wait_all(xbuf, gsem, slot)
        for base, n, dst in ((0, capc, 0), (gc * n_j, caps, capc)):
            for kc in range(ROW_TILE):
                xb_ref[dst:dst + n, kc * LANES:(kc + 1) * LANES] = xbuf[slot, kc, base:base + n, :].astype(BF16)
        acc_ref[...] = jnp.zeros_like(acc_ref)
        logits = _dot(xb_ref[...], wr_ref[...])
        lane = lax.broadcasted_iota(I32, (1, LANES), 1)
        is_expert = lane < N_EXPERTS
        ex = jnp.exp(logits - jnp.max(jnp.where(is_expert, logits, -jnp.inf), axis=-1, keepdims=True))
        mine = jnp.sum(jnp.where(lane == e, ex, 0.0), axis=-1, keepdims=True)
        gate = mine / jnp.sum(jnp.where(is_expert, ex, 0.0), axis=-1, keepdims=True)
        gate_ref[...] = jnp.broadcast_to(gate, gate_ref.shape)

    nxt = jnp.minimum(e + 1, n_e - 1)
    for group in groups:
        for i in range(group[4]):
            gather(nxt, other, j, i, group)
            scatter(e, other, j, i, group)

    x = xb_ref[...]
    g = _dot(x, wg_ref[0].astype(BF16))
    u = _dot(x, wu_ref[0].astype(BF16))
    hid = (g * jax.nn.sigmoid(g) * u).astype(BF16)
    acc_ref[...] += _dot(hid, wd_ref[0].astype(BF16))

    @pl.when(j == n_j - 1)
    def _():
        @pl.when(e >= 1)
        def _():
            wait_all(ybuf, ssem, slot)

        for base, n, src in ((0, capc, 0), (gc * n_j, caps, capc)):
            for r0 in range(0, n, LANES):
                nr = min(LANES, n - r0)
                gate = gate_ref[src + r0:src + r0 + nr, :]
                for kc in range(ROW_TILE):
                    y = acc_ref[src + r0:src + r0 + nr, kc * LANES:(kc + 1) * LANES] * gate
                    ybuf[slot, kc, base + r0:base + r0 + nr, :] = y

        @pl.when(e == n_e - 1)
        def _():
            all_steps(lambda p, group: scatter(e + 1, slot, 0, p, group))
            wait_all(ybuf, ssem, other)
            wait_all(ybuf, ssem, slot)
            wait_all(xbuf, gsem, other)


def _per_step(cap):
    return -(-cap // FF_STEPS)


def _copy_tables(idx, qslot, n_rows):
    n_e, cap = idx.shape
    padded = _per_step(cap) * FF_STEPS
    n_pad = padded - cap
    idx_p = jnp.concatenate([idx, jnp.zeros((n_e, n_pad), I32)], axis=1)
    spare = n_rows + jnp.arange(padded + n_e * n_pad, dtype=I32)
    lead = spare[:padded][None, :]
    pad_rows = spare[padded:].reshape(n_e, n_pad)
    q_p = jnp.concatenate([lead, jnp.concatenate([qslot // ROW_TILE, pad_rows], axis=1)], axis=0) * ROW_TILE
    return idx_p.reshape(-1), q_p.reshape(-1), n_rows + padded + n_e * n_pad


def _moe(idxc, idxs, qc, qs, hc, hs, wrx, w_gate, w_up, w_down):
    capc, caps = idxc.shape[1], idxs.shape[1]
    rows = capc + caps
    tf = FF_TILE
    idxc, qc, zc_rows = _copy_tables(idxc, qc, N_EXPERTS * capc)
    idxs, qs, zs_rows = _copy_tables(idxs, qs, N_EXPERTS * caps)
    buf_rows = (_per_step(capc) + _per_step(caps)) * FF_STEPS
    any_spec = pl.BlockSpec(memory_space=pl.ANY)
    grid_spec = pltpu.PrefetchScalarGridSpec(
        num_scalar_prefetch=4,
        grid=(N_EXPERTS, FF_STEPS),
        in_specs=[any_spec, any_spec,
                  pl.BlockSpec((D_MODEL, LANES), lambda e, j, *_: (0, 0)),
                  pl.BlockSpec((1, D_MODEL, tf), lambda e, j, *_: (e, 0, j)),
                  pl.BlockSpec((1, D_MODEL, tf), lambda e, j, *_: (e, 0, j)),
                  pl.BlockSpec((1, tf, D_MODEL), lambda e, j, *_: (e, j, 0))],
        out_specs=[any_spec, any_spec],
        scratch_shapes=[pltpu.VMEM((2, ROW_TILE, buf_rows, LANES), F32),
                        pltpu.VMEM((2, ROW_TILE, buf_rows, LANES), F32),
                        pltpu.VMEM((rows, D_MODEL), BF16),
                        pltpu.VMEM((rows, LANES), F32),
                        pltpu.VMEM((rows, D_MODEL), F32),
                        pltpu.SemaphoreType.DMA((2,)),
                        pltpu.SemaphoreType.DMA((2,))])
    return pl.pallas_call(
        functools.partial(_moe_kernel, capc=capc, caps=caps),
        grid_spec=grid_spec,
        out_shape=[jax.ShapeDtypeStruct((zc_rows * ROW_TILE, LANES), F32),
                   jax.ShapeDtypeStruct((zs_rows * ROW_TILE, LANES), F32)],
        compiler_params=_cparams(("arbitrary", "arbitrary")),
        name="expert_ffn",
    )(idxc, idxs, qc, qs, hc, hs, wrx, w_gate, w_up, w_down)


Z_BUFFERS = 3


def _combine_kernel(clo_ref, chi_ref, z_ref, span_ref, x1_ref, mod_ref, g_ref, o_ref,
                    zbuf, acc_ref, sems, state, *, n_chunks):
    b = pl.program_id(0)
    chunk_rows = ROW_CHUNK * ROW_TILE

    @pl.when(b == 0)
    def _():
        state[0] = 0
        state[1] = 0

    def chunk_copy(c):
        src = z_ref.at[pl.ds(pl.multiple_of(c * chunk_rows, chunk_rows), chunk_rows), :]
        return pltpu.make_async_copy(src, zbuf.at[c % Z_BUFFERS], sems.at[c % Z_BUFFERS])

    acc_ref[...] = jnp.zeros_like(acc_ref)
    first_row = span_ref[:, 0:1] * LANES + span_ref[:, 1:2]
    end_row = span_ref[:, 2:3] * LANES + span_ref[:, 3:4]
    lane_row = lax.broadcasted_iota(I32, (TOK_BLOCK, ROW_CHUNK), 1)

    def body(c, carry):
        for _ in range(Z_BUFFERS):
            nxt = state[0]

            @pl.when(nxt <= jnp.minimum(c + Z_BUFFERS - 1, n_chunks - 1))
            def _():
                chunk_copy(nxt).start()
                state[0] = nxt + 1

        @pl.when(state[1] <= c)
        def _():
            chunk_copy(c).wait()
            state[1] = c + 1

        slot = c % Z_BUFFERS
        row = (c * ROW_CHUNK + lane_row).astype(F32)
        onehot = jnp.where((row >= first_row) & (row < end_row), 1.0, 0.0).astype(BF16)
        y = jnp.concatenate([zbuf[slot, pl.ds(kc, ROW_CHUNK, stride=ROW_TILE), :] for kc in range(ROW_TILE)],
                            axis=1)
        acc_ref[...] += _dot(onehot, y.astype(BF16))
        return carry

    lax.fori_loop(clo_ref[b], chi_ref[b], body, 0)
    gt2 = mod_ref[0][:, 5 * D_MODEL:6 * D_MODEL]
    o_ref[...] = x1_ref[...] + gt2 * _rms(acc_ref[...], g_ref[...], EPS)


def _combine(ranges, z, span, x1, mod3, mod_row, g_post_ffn):
    clo, chi = ranges
    t = x1.shape[0]
    tb = TOK_BLOCK
    grid_spec = pltpu.PrefetchScalarGridSpec(
        num_scalar_prefetch=2,
        grid=(t // tb,),
        in_specs=[pl.BlockSpec(memory_space=pl.ANY),
                  pl.BlockSpec((tb, LANES), lambda b, *_: (b, 0)),
                  pl.BlockSpec((tb, D_MODEL), lambda b, *_: (b, 0)),
                  pl.BlockSpec((1, 1, N_MOD * D_MODEL), lambda b, *_: (mod_row(b, tb), 0, 0)),
                  pl.BlockSpec((1, D_MODEL), lambda b, *_: (0, 0))],
        out_specs=pl.BlockSpec((tb, D_MODEL), lambda b, *_: (b, 0)),
        scratch_shapes=[pltpu.VMEM((Z_BUFFERS, ROW_CHUNK * ROW_TILE, LANES), F32),
                        pltpu.VMEM((tb, D_MODEL), F32),
                        pltpu.SemaphoreType.DMA((Z_BUFFERS,)),
                        pltpu.SMEM((2,), I32)])
    return pl.pallas_call(
        functools.partial(_combine_kernel, n_chunks=2 * t // ROW_CHUNK),
        grid_spec=grid_spec,
        out_shape=jax.ShapeDtypeStruct((t, D_MODEL), F32),
        compiler_params=_cparams(("arbitrary",)),
        name="combine",
    )(clo, chi, z, span, x1, mod3, g_post_ffn)


def _rope_tables(seq):
    half = HEAD_DIM // 4
    freqs = ROPE_THETA ** (-np.arange(half, dtype=np.float64) / half)
    s = np.arange(seq)
    row = (s // GRID_W)[:, None] * freqs[None, :]
    col = (s % GRID_W)[:, None] * freqs[None, :]
    ang = np.concatenate([row, row, col, col], axis=1)
    ang = np.tile(ang, (1, QK_W // HEAD_DIM))
    lane = np.arange(QK_W)
    sign = np.where((lane % 32) < 16, -1.0, 1.0)[None, :]
    return (jnp.asarray(np.cos(ang), dtype=F32), jnp.asarray(np.sin(ang) * sign, dtype=F32))


def _combine_ranges(rows, n_tok):
    step = TOK_BLOCK // LANES
    nb = n_tok // TOK_BLOCK
    lo = rows[0:nb * step:step, 0]
    hi = jnp.concatenate([lo[1:], jnp.full((1,), 2 * n_tok, I32)])
    return (lo // ROW_CHUNK).astype(I32), ((hi + ROW_CHUNK - 1) // ROW_CHUNK).astype(I32)


def kernel(x_prompt, x_sample, c, cache_k, cache_v, c_ctx, w_mod, b_mod, g_pre_mix, g_post_mix, g_pre_ffn, g_post_ffn, w_in, lam_q1, lam_k1, lam_q2, lam_k2, g_subln, w_proj_attn, w_proj_fourier, w_out, w_router, w_gate, w_up, w_down):
    assert w_mod.shape[0] == 1
    lam_init = 0.8 - 0.6 * math.exp(-0.3 * 0)
    bp, sp, _ = x_prompt.shape
    bs, ss, _ = x_sample.shape

    mod3 = _modulation(c_ctx, c, w_mod[0], b_mod)

    w_in_b = None
    wpa =w_proj_attn[0].astype(BF16)
    wpf = w_proj_fourier[0].astype(BF16)
    wout = w_out[0].astype(BF16)
    wrx = jnp.pad(w_router[0], ((0, 0), (0, LANES - N_EXPERTS))).astype(BF16)
    lam_p = jnp.concatenate([lam_q1, lam_k1, lam_q2, lam_k2], axis=0)

    groups = []
    for x, seq, positional, ctx in ((x_sample, ss, True, (cache_k, cache_v)),
                                    (x_prompt, sp, False, None)):
        nb = x.shape[0]
        t = nb * seq
        x2d = x.reshape(t, D_MODEL)
        if positional:
            mod_row = lambda i, tm, seq=seq: 1 + (i * tm) // seq
        else:
            mod_row = lambda i, tm: 0
        self_contained = ctx is None and not positional
        if self_contained:
            assert PRE_BLOCK % seq == 0
            pre = _ctx_mixer(x2d, mod3, mod_row, g_pre_mix, w_in_b, seq, (lam_p, g_subln, lam_init), tm=PRE_BLOCK)
            o, fm, ga, gf = pre[:4]
        else:
            q, k, v, f, ga, gf, w_in_b = _pre_mixer(x2d, mod3, mod_row, g_pre_mix, w_in[0],
                                                    _rope_tables(seq) if positional else None, seq, tm=ROW_BLOCK)
            o = _attention(lam_p, g_subln, q, k, v, ctx, seq, lam_init)
            fm = _fourier(f, seq)
        x1, h2t, aff_t = _post_mixer(o, fm, ga, gf, x2d, mod3, mod_row, g_post_mix, g_pre_ffn,
                                     wpa, wpf, wout, wrx)
        cap = 2 * t // N_EXPERTS
        assert t <= 1 << TOKEN_BITS
        posm, pack, span, ranges, rows = _route(aff_t, cap)
        idx, qslot = _slot_lists(ranges, posm, pack, cap)
        groups.append(dict(x1=x1, h2t=h2t, idx=idx, qslot=qslot, span=span, ranges=_combine_ranges(rows, t),
                           mod_row=mod_row, cache=pre[4:] if self_contained else None, shape=x.shape))

    gs_, gc = groups
    zc, zs = _moe(gc["idx"], gs_["idx"], gc["qslot"], gs_["qslot"], gc["h2t"], gs_["h2t"], wrx,
                  w_gate[0], w_up[0], w_down[0])
    outs = []
    for g, z in ((gc, zc), (gs_, zs)):
        out = _combine(g["ranges"], z, g["span"], g["x1"], mod3, g["mod_row"], g_post_ffn)
        outs.append(out.reshape(g["shape"]))
    new_k, new_v = gc["cache"]
    return (outs[0], outs[1], new_k, new_v)
```

```python
import functools
import math

import numpy as np
import jax
import jax.numpy as jnp
from jax import lax
from jax.experimental import pallas as pl
from jax.experimental.pallas import tpu as pltpu

F32 = jnp.float32
BF16 = jnp.bfloat16
I32 = jnp.int32

D_MODEL = 1024
N_HEADS = 6
HEAD_DIM = 64
V_DIM = 128
QK_W = 768
FOUR_W = 256
FOUR_G = 64
IN_W = 4608
N_EXPERTS = 16
D_FF = 2816
N_MOD = 6
GRID_W = 64
ROPE_THETA = 10000.0
EPS = 1e-6
SUBLN_EPS = 1e-5

LANES = 128
ROW_BLOCK = 256
PRE_BLOCK = 512
GATE_CHUNK = 256
POST_BLOCK = 1024
POST_SUB = 256
WEIGHT_ROWS = 128
WEIGHT_STAGES = 3
TOK_BLOCK = 256
ROW_CHUNK = 256
FF_TILE = 256
FF_STEPS = D_FF // FF_TILE
ROW_TILE = D_MODEL // LANES
TOKEN_BITS = 13
VMEM_LIMIT = 56 * 1024 * 1024


def _cparams(sem):
    return pltpu.CompilerParams(dimension_semantics=sem, vmem_limit_bytes=VMEM_LIMIT)


def _dot(a, b):
    return jnp.dot(a, b, preferred_element_type=F32)


def _rms(x, g, eps):
    return x * lax.rsqrt(jnp.mean(x * x, axis=-1, keepdims=True) + eps) * g


def _whole_vmem():
    return pl.BlockSpec(memory_space=pltpu.MemorySpace.VMEM)


def _cast_weight(w_hbm, wb_ref, stage_ref, sem):
    depth, rows = stage_ref.shape[0], stage_ref.shape[1]
    chunks = [pltpu.make_async_copy(w_hbm.at[pl.ds(r0, rows), :], stage_ref.at[n % depth], sem.at[n % depth])
              for n, r0 in enumerate(range(0, w_hbm.shape[0], rows))]
    for chunk in chunks[:depth - 1]:
        chunk.start()
    for n, chunk in enumerate(chunks):
        if n + depth - 1 < len(chunks):
            chunks[n + depth - 1].start()
        chunk.wait()
        wb_ref[n * rows:(n + 1) * rows, :] = stage_ref[n % depth].astype(BF16)


MOD_ROWS = 8


def _mod_kernel(cc_ref, c_ref, w_ref, b_ref, o_ref):
    pad = jnp.zeros((MOD_ROWS - 1 - c_ref.shape[0], D_MODEL), F32)
    c = jnp.concatenate([cc_ref[...], c_ref[...], pad], axis=0)
    s = c * jax.nn.sigmoid(c)
    res = _dot(s.astype(BF16), w_ref[...].astype(BF16)) + b_ref[...]
    for r in range(MOD_ROWS):
        o_ref[r] = res[r:r + 1, :]


def _modulation(c_ctx, c, w_mod, b_mod):
    tn = 512
    n = N_MOD * D_MODEL
    assert 1 + c.shape[0] <= MOD_ROWS
    return pl.pallas_call(
        _mod_kernel,
        grid=(n // tn,),
        in_specs=[pl.BlockSpec((1, D_MODEL), lambda j: (0, 0)),
                  pl.BlockSpec(c.shape, lambda j: (0, 0)),
                  pl.BlockSpec((D_MODEL, tn), lambda j: (0, j)),
                  pl.BlockSpec((1, tn), lambda j: (0, j))],
        out_specs=pl.BlockSpec((MOD_ROWS, 1, tn), lambda j: (0, 0, j)),
        out_shape=jax.ShapeDtypeStruct((MOD_ROWS, 1, n), F32),
        compiler_params=_cparams(("arbitrary",)),
        name="modulation",
    )(c_ctx[None, :], c, w_mod, b_mod)


def _diff_lambda(lp, lam_init):
    s1 = jnp.sum(lp[0:1] * lp[1:2], axis=-1, keepdims=True)
    s2 = jnp.sum(lp[2:3] * lp[3:4], axis=-1, keepdims=True)
    return jnp.exp(s1) - jnp.exp(s2) + lam_init


def _attention_weights(q, k):
    comp1 = lax.broadcasted_iota(I32, (1, V_DIM), 1) < HEAD_DIM
    qs = q * jnp.asarray(HEAD_DIM ** -0.5, BF16)
    zero = jnp.zeros_like(qs)

    def weights(qc):
        s = lax.dot_general(qc, k, (((1,), (1,)), ((), ())), preferred_element_type=F32)
        return jnp.exp(s - jnp.max(s, axis=-1, keepdims=True)).astype(BF16)

    return weights(jnp.where(comp1, qs, zero)), weights(jnp.where(comp1, zero, qs))


def _attention_output(weights, v, lam, g_subln, lam_init):
    v_ones = jnp.concatenate([v, jnp.ones_like(v)], axis=1)

    def attend(ex):
        ov = _dot(ex, v_ones)
        return ov[:, 0:V_DIM] / ov[:, V_DIM:2 * V_DIM]

    o = attend(weights[0]) - lam * attend(weights[1])
    return (_rms(o, g_subln, SUBLN_EPS) * (1.0 - lam_init)).astype(BF16)


def _rope(z, cos, sin_signed, first_half):
    fwd = pltpu.roll(z, QK_W - 16, axis=1)
    bwd = pltpu.roll(z, 16, axis=1)
    return z * cos + jnp.where(first_half, fwd, bwd) * sin_signed


def _pre_kernel(*refs, positional):
    it = iter(refs)
    x_ref, mod_ref, g_ref, w_hbm = next(it), next(it), next(it), next(it)
    if positional:
        cos_ref, sin_ref = next(it), next(it)
    q_ref, k_ref, v_ref, f_ref, ga_ref, gf_ref = (next(it) for _ in range(6))
    wb_hbm, w_ref, stage_ref, sem = next(it), next(it), next(it), next(it)
    weight_out = pltpu.make_async_copy(w_ref, wb_hbm, sem.at[WEIGHT_STAGES])

    @pl.when(pl.program_id(0) == 0)
    def _():
        _cast_weight(w_hbm, w_ref, stage_ref, sem)
        weight_out.start()

    @pl.when(pl.program_id(0) == pl.num_programs(0) - 1)
    def _():
        weight_out.wait()

    m = mod_ref[0]
    sh1 = m[:, 0:D_MODEL]
    sc1 = m[:, D_MODEL:2 * D_MODEL]
    h = _rms(x_ref[...], g_ref[...], EPS) * (1.0 + sc1) + sh1
    hb = h.astype(BF16)

    def proj(lo, hi):
        return _dot(hb, w_ref[:, lo:hi])

    zq = proj(0, QK_W)
    zk = proj(QK_W, 2 * QK_W)
    zv = proj(2 * QK_W, 3 * QK_W)
    if positional:
        lane = lax.broadcasted_iota(I32, (1, QK_W), 1)
        first_half = (lane % 32) < 16
        cos = cos_ref[...]
        sin_signed = sin_ref[...]
        zq = _rope(zq, cos, sin_signed, first_half)
        zk = _rope(zk, cos, sin_signed, first_half)
    q_ref[...] = zq.astype(BF16)
    k_ref[...] = zk.astype(BF16)
    v_ref[...] = zv.astype(BF16)
    f0 = 3 * QK_W
    g0 = f0 + FOUR_W
    f_ref[...] = proj(f0, g0)
    ga_ref[...] = jax.nn.sigmoid(proj(g0, g0 + D_MODEL)).astype(BF16)
    gf_ref[...] = jax.nn.sigmoid(proj(g0 + D_MODEL, IN_W)).astype(BF16)


def _ctx_kernel(x_ref, mod_ref, g_ref, w_ref, bc_ref, bs_ref, cs_ref, ss_ref, lam_ref, gs_ref,
                o_ref, f_ref, ga_ref, gf_ref, kc_ref, vc_ref, *, seq, lam_init):
    m = mod_ref[0]
    sh1 = m[:, 0:D_MODEL]
    sc1 = m[:, D_MODEL:2 * D_MODEL]
    hb = (_rms(x_ref[...], g_ref[...], EPS) * (1.0 + sc1) + sh1).astype(BF16)
    seqs = [slice(b * seq, (b + 1) * seq) for b in range(x_ref.shape[0] // seq)]
    lam = _diff_lambda(lam_ref[...], lam_init)
    f0 = 3 * QK_W
    g0 = f0 + FOUR_W

    def proj(lo, hi):
        return _dot(hb, w_ref[:, lo:hi])

    q, k, v = {}, {}, {}

    def qkv_chunk(store, cache_ref, lo, c):
        cols = slice(c * GATE_CHUNK, (c + 1) * GATE_CHUNK)
        z = proj(lo + cols.start, lo + cols.stop)
        for hd in range(cols.start // V_DIM, cols.stop // V_DIM):
            z_hd = z[:, hd * V_DIM - cols.start:(hd + 1) * V_DIM - cols.start]
            if cache_ref is not None:
                for b, rs in enumerate(seqs):
                    cache_ref[b, 0, hd] = z_hd[rs, :]
            store[hd] = z_hd.astype(BF16)

    def fourier_chunk():
        f = proj(f0, g0)
        for rs in seqs:
            f_ref[rs, :] = _dft_real(f[rs, :], bc_ref, bs_ref, cs_ref, ss_ref)

    def gate_chunk(ref, lo, c):
        cols = slice(c * GATE_CHUNK, (c + 1) * GATE_CHUNK)
        ref[:, cols] = jax.nn.sigmoid(proj(lo + cols.start, lo + cols.stop)).astype(BF16)

    for store, cache, lo in ((q, None, 0), (k, kc_ref, QK_W), (v, vc_ref, 2 * QK_W)):
        for c in range(QK_W // GATE_CHUNK):
            qkv_chunk(store, cache, lo, c)
    matmul_work = [fourier_chunk] + [functools.partial(gate_chunk, ref, lo, c)
                                     for ref, lo in ((ga_ref, g0), (gf_ref, g0 + D_MODEL))
                                     for c in range(D_MODEL // GATE_CHUNK)]
    for rs in seqs:
        for hd in range(N_HEADS):
            weights = _attention_weights(q[hd][rs, :], k[hd][rs, :])
            if matmul_work:
                matmul_work.pop(0)()
            o_ref[rs, hd * V_DIM:(hd + 1) * V_DIM] = _attention_output(weights, v[hd][rs, :], lam, gs_ref[...],
                                                                      lam_init)
    for work in matmul_work:
        work()


def _ctx_mixer(x2d, mod3, mod_row, g_pre, w_in_b, seq, attn_params, tm):
    t = x2d.shape[0]
    lam_p, g_subln, lam_init = attn_params
    assert tm % seq == 0 and t % tm == 0
    row = lambda i: (i, 0)
    const = lambda i: (0, 0)
    consts = _dft_consts(seq) + (lam_p, g_subln)
    nb = t // seq
    cshape = jax.ShapeDtypeStruct((nb, 1, N_HEADS, seq, V_DIM), F32)
    cspec = pl.BlockSpec((tm // seq, 1, N_HEADS, seq, V_DIM), lambda i: (i, 0, 0, 0, 0))
    return pl.pallas_call(
        functools.partial(_ctx_kernel, seq=seq, lam_init=lam_init),
        grid=(t // tm,),
        in_specs=[pl.BlockSpec((tm, D_MODEL), row),
                  pl.BlockSpec((1, 1, N_MOD * D_MODEL), lambda i: (mod_row(i, tm), 0, 0)),
                  pl.BlockSpec((1, D_MODEL), const),
                  _whole_vmem()] + [pl.BlockSpec(c.shape, const) for c in consts],
        out_specs=[pl.BlockSpec((tm, QK_W), row),
                   pl.BlockSpec((tm, FOUR_W), row),
                   pl.BlockSpec((tm, D_MODEL), row),
                   pl.BlockSpec((tm, D_MODEL), row),
                   cspec, cspec],
        out_shape=[jax.ShapeDtypeStruct((t, QK_W), BF16),
                   jax.ShapeDtypeStruct((t, FOUR_W), BF16),
                   jax.ShapeDtypeStruct((t, D_MODEL), BF16),
                   jax.ShapeDtypeStruct((t, D_MODEL), BF16),
                   cshape, cshape],
        compiler_params=_cparams(("arbitrary",)),
        name="ctx_mixer",
    )(x2d, mod3, g_pre, w_in_b, *consts)


def _pre_mixer(x2d, mod3, mod_row, g_pre, w_in, rope_tabs, seq, tm):
    t = x2d.shape[0]
    positional = rope_tabs is not None
    assert seq % tm == 0
    blocks_per_seq = seq // tm
    row = lambda i: (i, 0)
    any_spec = pl.BlockSpec(memory_space=pl.ANY)
    in_specs = [pl.BlockSpec((tm, D_MODEL), row),
                pl.BlockSpec((1, 1, N_MOD * D_MODEL), lambda i: (mod_row(i, tm), 0, 0)),
                pl.BlockSpec((1, D_MODEL), lambda i: (0, 0)),
                any_spec]
    args = [x2d, mod3, g_pre, w_in]
    if positional:
        in_specs += [pl.BlockSpec((tm, QK_W), lambda i: (i % blocks_per_seq, 0))] * 2
        args += list(rope_tabs)
    out_shape = [jax.ShapeDtypeStruct((t, QK_W), BF16)] * 3 + [
        jax.ShapeDtypeStruct((t, FOUR_W), F32),
        jax.ShapeDtypeStruct((t, D_MODEL), BF16),
        jax.ShapeDtypeStruct((t, D_MODEL), BF16),
        jax.ShapeDtypeStruct(w_in.shape, BF16)]
    out_specs = [pl.BlockSpec((tm, QK_W), row)] * 3 + [
        pl.BlockSpec((tm, FOUR_W), row),
        pl.BlockSpec((tm, D_MODEL), row),
        pl.BlockSpec((tm, D_MODEL), row),
        any_spec]
    return pl.pallas_call(
        functools.partial(_pre_kernel, positional=positional),
        grid=(t // tm,),
        in_specs=in_specs,
        out_specs=out_specs,
        out_shape=out_shape,
        scratch_shapes=[pltpu.VMEM(w_in.shape, BF16),
                        pltpu.VMEM((WEIGHT_STAGES, WEIGHT_ROWS, w_in.shape[1]), F32),
                        pltpu.SemaphoreType.DMA((WEIGHT_STAGES + 1,))],
        compiler_params=_cparams(("arbitrary",)),
        name="pre_mixer",
    )(*args)


def _attn_kernel(*refs, lam_init, has_ctx):
    it = iter(refs)
    lam_ref, gs_ref, q_ref, k_ref, v_ref = (next(it) for _ in range(5))
    if has_ctx:
        ck_ref, cv_ref = next(it), next(it)
    o_ref = next(it)

    lam = _diff_lambda(lam_ref[...], lam_init)

    def operand(ref, cache_ref, hd):
        x = ref[:, hd * V_DIM:(hd + 1) * V_DIM]
        return jnp.concatenate([cache_ref[0, 0, hd].astype(BF16), x], axis=0) if has_ctx else x

    def head_weights(hd):
        return _attention_weights(q_ref[:, hd * V_DIM:(hd + 1) * V_DIM], operand(k_ref, ck_ref if has_ctx else None, hd))

    weights = head_weights(0)
    for hd in range(N_HEADS):
        nxt = head_weights(hd + 1) if hd + 1 < N_HEADS else None
        v = operand(v_ref, cv_ref if has_ctx else None, hd)
        o_ref[:, hd * V_DIM:(hd + 1) * V_DIM] = _attention_output(weights, v, lam, gs_ref[...], lam_init)
        weights = nxt


def _attention(lam_p, g_subln, q, k, v, ctx, seq, lam_init):
    t = q.shape[0]
    tq = ROW_BLOCK
    qb = seq // tq
    has_ctx = ctx is not None
    in_specs = [pl.BlockSpec((4, HEAD_DIM), lambda b, i: (0, 0)),
                pl.BlockSpec((1, V_DIM), lambda b, i: (0, 0)),
                pl.BlockSpec((tq, QK_W), lambda b, i: (b * qb + i, 0)),
                pl.BlockSpec((seq, QK_W), lambda b, i: (b, 0)),
                pl.BlockSpec((seq, QK_W), lambda b, i: (b, 0))]
    args = [lam_p, g_subln, q, k, v]
    if has_ctx:
        past = ctx[0].shape[3]
        cspec = pl.BlockSpec((1, 1, N_HEADS, past, V_DIM), lambda b, i: (b, 0, 0, 0, 0))
        in_specs += [cspec, cspec]
        args += list(ctx)
    return pl.pallas_call(
        functools.partial(_attn_kernel, lam_init=lam_init, has_ctx=has_ctx),
        grid=(t // seq, qb),
        in_specs=in_specs,
        out_specs=pl.BlockSpec((tq, QK_W), lambda b, i: (b * qb + i, 0)),
        out_shape=jax.ShapeDtypeStruct((t, QK_W), BF16),
        compiler_params=_cparams(("arbitrary", "arbitrary")),
        name="diff_attention",
    )(*args)


def _dft_real(f, bc_ref, bs_ref, cs_ref, ss_ref):
    fb = f.astype(BF16)
    u = _dot(fb, bc_ref[...].astype(BF16)).astype(BF16)
    w = _dot(fb, bs_ref[...].astype(BF16)).astype(BF16)
    return (_dot(cs_ref[...].astype(BF16), u) - _dot(ss_ref[...].astype(BF16), w)).astype(BF16)


def _fourier_kernel(f_ref, bc_ref, bs_ref, cs_ref, ss_ref, o_ref):
    o_ref[...] = _dft_real(f_ref[...], bc_ref, bs_ref, cs_ref, ss_ref)


def _dft_consts(seq):
    c = np.arange(FOUR_G)
    ang_c = 2.0 * np.pi * ((c[:, None] * c[None, :]) % FOUR_G) / FOUR_G
    eye = np.eye(FOUR_W // FOUR_G)
    bc = np.kron(eye, np.cos(ang_c)) / math.sqrt(FOUR_G)
    bs = np.kron(eye, np.sin(ang_c)) / math.sqrt(FOUR_G)
    s = np.arange(seq)
    ang_s = 2.0 * np.pi * ((s[:, None] * s[None, :]) % seq) / seq
    cs = np.cos(ang_s) / math.sqrt(seq)
    ss = np.sin(ang_s) / math.sqrt(seq)
    return tuple(jnp.asarray(a, dtype=F32) for a in (bc, bs, cs, ss))


def _fourier(f, seq):
    t = f.shape[0]
    bc, bs, cs, ss = _dft_consts(seq)
    const = lambda b: (0, 0)
    return pl.pallas_call(
        _fourier_kernel,
        grid=(t // seq,),
        in_specs=[pl.BlockSpec((seq, FOUR_W), lambda b: (b, 0)),
                  pl.BlockSpec((FOUR_W, FOUR_W), const),
                  pl.BlockSpec((FOUR_W, FOUR_W), const),
                  pl.BlockSpec((seq, seq), const),
                  pl.BlockSpec((seq, seq), const)],
        out_specs=pl.BlockSpec((seq, FOUR_W), lambda b: (b, 0)),
        out_shape=jax.ShapeDtypeStruct((t, FOUR_W), BF16),
        compiler_params=_cparams(("arbitrary",)),
        name="fourier_mix",
    )(f, bc, bs, cs, ss)


def _post_kernel(o_ref, fm_ref, ga_ref, gf_ref, x_ref, mod_ref, gpost_ref, gffn_ref,
                 wpa_ref, wpf_ref, wout_ref, wrx_ref,
                 x1_ref, h2t_ref, afft_ref):
    m = mod_ref[0]
    gt1 = m[:, 2 * D_MODEL:3 * D_MODEL]
    sh2 = m[:, 3 * D_MODEL:4 * D_MODEL]
    sc2 = m[:, 4 * D_MODEL:5 * D_MODEL]
    subs = [slice(r0, r0 + POST_SUB) for r0 in range(0, o_ref.shape[0], POST_SUB)]
    ab = [(_dot(o_ref[rs, :], wpa_ref[...]), _dot(fm_ref[rs, :], wpf_ref[...])) for rs in subs]
    merged = [(ga_ref[rs, :] * a + gf_ref[rs, :] * b).astype(BF16) for rs, (a, b) in zip(subs, ab)]
    ys = [_dot(mg, wout_ref[...]) for mg in merged]
    h2s = []
    for rs, y in zip(subs, ys):
        x1 = x_ref[rs, :] + gt1 * _rms(y, gpost_ref[...], EPS)
        x1_ref[rs, :] = x1
        h2s.append(_rms(x1, gffn_ref[...], EPS) * (1.0 + sc2) + sh2)
    logits = [_dot(h2.astype(BF16), wrx_ref[...]) for h2 in h2s]
    for rs, h2, lg in zip(subs, h2s, logits):
        lt = lg.T[0:N_EXPERTS]
        et = jnp.exp(lt - jnp.max(lt, axis=0, keepdims=True))
        aff = et / jnp.sum(et, axis=0, keepdims=True)
        for u in range(POST_SUB // LANES):
            afft_ref[:, rs.start // LANES + u, :] = aff[:, u * LANES:(u + 1) * LANES]
        for kc in range(ROW_TILE):
            h2t_ref[pl.ds(rs.start * ROW_TILE + kc, POST_SUB, stride=ROW_TILE), :] = h2[:, kc * LANES:(kc + 1) * LANES]


def _post_mixer(o, fm, ga, gf, x2d, mod3, mod_row, g_post, g_ffn, wpa, wpf, wout, wrx):
    t = x2d.shape[0]
    tm = POST_BLOCK
    row = lambda i: (i, 0)
    const = lambda i: (0, 0)
    return pl.pallas_call(
        _post_kernel,
        grid=(t // tm,),
        in_specs=[pl.BlockSpec((tm, QK_W), row),
                  pl.BlockSpec((tm, FOUR_W), row),
                  pl.BlockSpec((tm, D_MODEL), row),
                  pl.BlockSpec((tm, D_MODEL), row),
                  pl.BlockSpec((tm, D_MODEL), row),
                  pl.BlockSpec((1, 1, N_MOD * D_MODEL), lambda i: (mod_row(i, tm), 0, 0)),
                  pl.BlockSpec((1, D_MODEL), const),
                  pl.BlockSpec((1, D_MODEL), const),
                  pl.BlockSpec((QK_W, D_MODEL), const),
                  pl.BlockSpec((FOUR_W, D_MODEL), const),
                  pl.BlockSpec((D_MODEL, D_MODEL), const),
                  pl.BlockSpec((D_MODEL, LANES), const)],
        out_specs=[pl.BlockSpec((tm, D_MODEL), row),
                   pl.BlockSpec((tm * ROW_TILE, LANES), row),
                   pl.BlockSpec((N_EXPERTS, tm // LANES, LANES), lambda i: (0, i, 0))],
        out_shape=[jax.ShapeDtypeStruct((t, D_MODEL), F32),
                   jax.ShapeDtypeStruct((t * ROW_TILE, LANES), F32),
                   jax.ShapeDtypeStruct((N_EXPERTS, t // LANES, LANES), F32)],
        compiler_params=_cparams(("arbitrary",)),
        name="post_mixer",
    )(o, fm, ga, gf, x2d, mod3, g_post, g_ffn, wpa, wpf, wout, wrx)


RANGE_ROWS = 8


def _route_kernel(aff_ref, posm_ref, pack_ref, span_ref, ranges_ref, rows_ref, *, cap, n_tok):
    aff = aff_ref[...]
    nt = n_tok // LANES
    capf = float(cap)

    def count_ge(v):
        return jnp.sum(jnp.where(aff >= v, 1.0, 0.0), axis=(1, 2), keepdims=True)

    def search(i, thr):
        cand = thr | jnp.left_shift(jnp.int32(1), 30 - i)
        return jnp.where(count_ge(pltpu.bitcast(cand, F32)) >= capf, cand, thr)

    thr = lax.fori_loop(0, 31, search, jnp.zeros((N_EXPERTS, 1, 1), I32))
    lo = pltpu.bitcast(thr, F32)
    hi = pltpu.bitcast(thr + 1, F32)

    def refine(i, c):
        lo, hi = c
        mid = lo + (hi - lo) * 0.5
        ok = count_ge(mid) >= capf
        return jnp.where(ok, mid, lo), jnp.where(ok, hi, mid)

    lo, hi = lax.fori_loop(0, 12, refine, (lo, hi))
    gt = aff >= hi
    eq = (aff >= lo) & (aff < hi)
    n_tie = capf - jnp.sum(jnp.where(gt, 1.0, 0.0), axis=(1, 2), keepdims=True)

    sq0 = lax.broadcasted_iota(I32, (LANES, LANES), 0)
    sq1 = lax.broadcasted_iota(I32, (LANES, LANES), 1)
    along_total = jnp.concatenate([jnp.where(sq0 <= sq1, 1.0, 0.0), jnp.ones((LANES, LANES), F32)],
                                  axis=1).astype(BF16)
    m = N_EXPERTS * nt
    r0 = lax.broadcasted_iota(I32, (m, m), 0)
    r1 = lax.broadcasted_iota(I32, (m, m), 1)
    earlier = jnp.where((r0 // nt == r1 // nt) & (r1 < r0), 1.0, 0.0).astype(BF16)
    lane = lax.broadcasted_iota(I32, (1, LANES), 1)
    token = lax.broadcasted_iota(I32, (nt, LANES), 0) * LANES + lane

    def tile_counts(x):
        both = _dot(x.reshape(m, LANES).astype(BF16), along_total)
        total = both[:, LANES:]
        before = _dot(earlier, total.astype(BF16))
        shape = (N_EXPERTS, nt, LANES)
        return both[:, :LANES].reshape(shape), total.reshape(shape), before.reshape(shape)

    eq_f = jnp.where(eq, 1.0, 0.0)
    eq_along, _, eq_before = tile_counts(eq_f)
    sel = jnp.where(gt, 1.0, jnp.where(eq_along + eq_before <= n_tie, eq_f, 0.0))
    sel_along, sel_total, sel_before = tile_counts(sel)
    posm_ref[...] = jnp.where(sel > 0.5, sel_along + sel_before - sel, -1.0).astype(I32)

    cnt = jnp.sum(sel, axis=0)
    rows_before = jnp.sum(sel_before, axis=0)
    tok_start = _dot(cnt.astype(BF16), along_total[:, :LANES]) - cnt + rows_before
    k = jnp.zeros((nt, LANES), F32)
    slots_before = jnp.zeros((nt, LANES), F32)
    slots_here = jnp.zeros((nt, LANES), F32)
    for e in range(N_EXPERTS):
        pack_ref[e] = (tok_start + k).astype(I32) * (1 << TOKEN_BITS) + token
        k = k + sel[e]
        slots_before = jnp.where(lane == e, sel_before[e], slots_before)
        slots_here = jnp.where(lane == e, sel_total[e], slots_here)

    eye = jnp.where(sq0 == sq1, 1.0, 0.0).astype(BF16)
    tok_end = tok_start + cnt
    parts = []
    for v in (tok_start, tok_end):
        high = jnp.floor(v * (1.0 / LANES))
        parts += [high, v - high * LANES]
    for c in range(nt):
        rows4 = jnp.zeros((LANES, LANES), F32)
        for r, part in enumerate(parts):
            rows4 = jnp.where(sq0 == r, jnp.broadcast_to(part[c:c + 1, :], (LANES, LANES)), rows4)
        span_ref[c * LANES:(c + 1) * LANES, :] = lax.dot_general(
            eye, rows4.astype(BF16), (((1,), (1,)), ((), ())), preferred_element_type=F32)

    ranges_ref[...] = jnp.zeros_like(ranges_ref)
    for s in range(cap // LANES):
        done = jnp.where(slots_before + slots_here <= float(s * LANES), 1.0, 0.0)
        begun = jnp.where(slots_before < float((s + 1) * LANES), 1.0, 0.0)
        ranges_ref[s:s + 1, :] = jnp.sum(done, axis=0, keepdims=True).astype(I32)
        ranges_ref[RANGE_ROWS + s:RANGE_ROWS + s + 1, :] = jnp.sum(begun, axis=0, keepdims=True).astype(I32)
    rows_ref[...] = rows_before.astype(I32)


def _route(aff, cap):
    n_e, nt, _ = aff.shape
    n_tok = nt * LANES
    assert cap // LANES <= RANGE_ROWS
    return pl.pallas_call(
        functools.partial(_route_kernel, cap=cap, n_tok=n_tok),
        out_shape=[jax.ShapeDtypeStruct((n_e, nt, LANES), I32),
                   jax.ShapeDtypeStruct((n_e, nt, LANES), I32),
                   jax.ShapeDtypeStruct((n_tok, LANES), F32),
                   jax.ShapeDtypeStruct((2 * RANGE_ROWS, LANES), I32),
                   jax.ShapeDtypeStruct((nt, LANES), I32)],
        compiler_params=pltpu.CompilerParams(vmem_limit_bytes=VMEM_LIMIT),
        name="route",
    )(aff)


def _slots_kernel(ranges_ref, posm_ref, pack_ref, idx_ref, qslot_ref, acc_ref):
    e = pl.program_id(0)
    sub = lax.broadcasted_iota(I32, (LANES, LANES), 0)
    eye = sub == lax.broadcasted_iota(I32, (LANES, LANES), 1)
    n_tiles = idx_ref.shape[1]

    for s in range(n_tiles):
        slot = sub + s * LANES

        def body(c, acc, slot=slot):
            hit = posm_ref[e, pl.ds(c, 1), :] == slot
            return acc + jnp.where(hit, pack_ref[e, pl.ds(c, 1), :], 0)

        acc_ref[s] = lax.fori_loop(ranges_ref[s, e], ranges_ref[RANGE_ROWS + s, e], body,
                                   jnp.zeros((LANES, LANES), I32))

    def as_row(part):
        col = jnp.sum(part.astype(F32), axis=1, keepdims=True)
        return jnp.sum(jnp.where(eye, col, 0.0), axis=0, keepdims=True).astype(I32)

    for s in range(n_tiles):
        acc = acc_ref[s]
        idx_ref[0, s:s + 1, :] = as_row(acc & ((1 << TOKEN_BITS) - 1)) * ROW_TILE
        qslot_ref[0, s:s + 1, :] = as_row(lax.shift_right_logical(acc, TOKEN_BITS)) * ROW_TILE


def _slot_lists(ranges, posm4, qdst4, cap):
    ns = cap // LANES
    grid_spec = pltpu.PrefetchScalarGridSpec(
        num_scalar_prefetch=1,
        grid=(N_EXPERTS,),
        in_specs=[_whole_vmem(), _whole_vmem()],
        out_specs=[pl.BlockSpec((1, ns, LANES), lambda e, *_: (e, 0, 0)),
                   pl.BlockSpec((1, ns, LANES), lambda e, *_: (e, 0, 0))],
        scratch_shapes=[pltpu.VMEM((ns, LANES, LANES), I32)])
    idx, qslot = pl.pallas_call(
        _slots_kernel,
        grid_spec=grid_spec,
        out_shape=[jax.ShapeDtypeStruct((N_EXPERTS, ns, LANES), I32),
                   jax.ShapeDtypeStruct((N_EXPERTS, ns, LANES), I32)],
        compiler_params=_cparams(("arbitrary",)),
        name="slot_lists",
    )(ranges, posm4, qdst4)
    return idx.reshape(N_EXPERTS, cap), qslot.reshape(N_EXPERTS, cap)


def _moe_kernel(idxc_ref, idxs_ref, qc_ref, qs_ref,
                hc_ref, hs_ref, wr_ref, wg_ref, wu_ref, wd_ref, zc_ref, zs_ref,
                xbuf, ybuf, xb_ref, gate_ref, acc_ref, gsem, ssem, *, capc, caps):
    e = pl.program_id(0)
    j = pl.program_id(1)
    n_e = pl.num_programs(0)
    n_j = FF_STEPS
    slot = e % 2
    other = 1 - slot
    rows = capc + caps
    gc, gs = _per_step(capc), _per_step(caps)
    groups = ((hc_ref, idxc_ref, zc_ref, qc_ref, gc, 0), (hs_ref, idxs_ref, zs_ref, qs_ref, gs, gc * n_j))

    def tile(ref, first_sublane):
        return ref.at[pl.ds(pl.multiple_of(first_sublane, ROW_TILE), ROW_TILE), :]

    def gather(ex, sl, step, i, group):
        h_ref, idx_ref, _, _, per_step, base = group
        p = step * per_step + i
        src = tile(h_ref, idx_ref[ex * (per_step * n_j) + p])
        pltpu.make_async_copy(src, xbuf.at[sl, :, base + p, :], gsem.at[sl]).start()

    def scatter(table_row, sl, step, i, group):
        _, _, z_ref, q_ref, per_step, base = group
        p = step * per_step + i
        dst = tile(z_ref, q_ref[table_row * (per_step * n_j) + p])
        pltpu.make_async_copy(ybuf.at[sl, :, base + p, :], dst, ssem.at[sl]).start()

    def all_steps(fn):
        for group in groups:
            def body(p, carry, group=group):
                fn(p, group)
                return carry
            lax.fori_loop(0, group[4] * n_j, body, 0, unroll=8)

    def wait_all(buf, sem, sl):
        pltpu.make_async_copy(buf.at[sl], buf.at[sl], sem.at[sl]).wait()

    @pl.when((e == 0) & (j == 0))
    def _():
        ybuf[...] = jnp.zeros_like(ybuf)
        all_steps(lambda p, group: gather(0, 0, 0, p, group))

    @pl.when(j == 0)
    def _():
        wait_all(xbuf, gsem, slot)
        for base, n, dst in ((0, capc, 0), (gc * n_j, caps, capc)):
            for kc in range(ROW_TILE):
                xb_ref[dst:dst + n, kc * LANES:(kc + 1) * LANES] = xbuf[slot, kc, base:base + n, :].astype(BF16)
        acc_ref[...] = jnp.zeros_like(acc_ref)
        logits = _dot(xb_ref[...], wr_ref[...])
        lane = lax.broadcasted_iota(I32, (1, LANES), 1)
        is_expert = lane < N_EXPERTS
        ex = jnp.exp(logits - jnp.max(jnp.where(is_expert, logits, -jnp.inf), axis=-1, keepdims=True))
        mine = jnp.sum(jnp.where(lane == e, ex, 0.0), axis=-1, keepdims=True)
        gate = mine / jnp.sum(jnp.where(is_expert, ex, 0.0), axis=-1, keepdims=True)
        gate_ref[...] = jnp.broadcast_to(gate, gate_ref.shape)

    nxt = jnp.minimum(e + 1, n_e - 1)
    for group in groups:
        for i in range(group[4]):
            gather(nxt, other, j, i, group)
            scatter(e, other, j, i, group)

    x = xb_ref[...]
    g = _dot(x, wg_ref[0].astype(BF16))
    u = _dot(x, wu_ref[0].astype(BF16))
    hid = (g * jax.nn.sigmoid(g) * u).astype(BF16)
    acc_ref[...] += _dot(hid, wd_ref[0].astype(BF16))

    @pl.when(j == n_j - 1)
    def _():
        @pl.when(e >= 1)
        def _():
            wait_all(ybuf, ssem, slot)

        for base, n, src in ((0, capc, 0), (gc * n_j, caps, capc)):
            for r0 in range(0, n, LANES):
                nr = min(LANES, n - r0)
                gate = gate_ref[src + r0:src + r0 + nr, :]
                for kc in range(ROW_TILE):
                    y = acc_ref[src + r0:src + r0 + nr, kc * LANES:(kc + 1) * LANES] * gate
                    ybuf[slot, kc, base + r0:base + r0 + nr, :] = y

        @pl.when(e == n_e - 1)
        def _():
            all_steps(lambda p, group: scatter(e + 1, slot, 0, p, group))
            wait_all(ybuf, ssem, other)
            wait_all(ybuf, ssem, slot)
            wait_all(xbuf, gsem, other)


def _per_step(cap):
    return -(-cap // FF_STEPS)


def _copy_tables(idx, qslot, n_rows):
    n_e, cap = idx.shape
    padded = _per_step(cap) * FF_STEPS
    n_pad = padded - cap
    idx_p = jnp.concatenate([idx, jnp.zeros((n_e, n_pad), I32)], axis=1)
    spare = n_rows + jnp.arange(padded + n_e * n_pad, dtype=I32)
    lead = spare[:padded][None, :]
    pad_rows = spare[padded:].reshape(n_e, n_pad)
    q_p = jnp.concatenate([lead, jnp.concatenate([qslot // ROW_TILE, pad_rows], axis=1)], axis=0) * ROW_TILE
    return idx_p.reshape(-1), q_p.reshape(-1), n_rows + padded + n_e * n_pad


def _moe(idxc, idxs, qc, qs, hc, hs, wrx, w_gate, w_up, w_down):
    capc, caps = idxc.shape[1], idxs.shape[1]
    rows = capc + caps
    tf = FF_TILE
    idxc, qc, zc_rows = _copy_tables(idxc, qc, N_EXPERTS * capc)
    idxs, qs, zs_rows = _copy_tables(idxs, qs, N_EXPERTS * caps)
    buf_rows = (_per_step(capc) + _per_step(caps)) * FF_STEPS
    any_spec = pl.BlockSpec(memory_space=pl.ANY)
    grid_spec = pltpu.PrefetchScalarGridSpec(
        num_scalar_prefetch=4,
        grid=(N_EXPERTS, FF_STEPS),
        in_specs=[any_spec, any_spec,
                  pl.BlockSpec((D_MODEL, LANES), lambda e, j, *_: (0, 0)),
                  pl.BlockSpec((1, D_MODEL, tf), lambda e, j, *_: (e, 0, j)),
                  pl.BlockSpec((1, D_MODEL, tf), lambda e, j, *_: (e, 0, j)),
                  pl.BlockSpec((1, tf, D_MODEL), lambda e, j, *_: (e, j, 0))],
        out_specs=[any_spec, any_spec],
        scratch_shapes=[pltpu.VMEM((2, ROW_TILE, buf_rows, LANES), F32),
                        pltpu.VMEM((2, ROW_TILE, buf_rows, LANES), F32),
                        pltpu.VMEM((rows, D_MODEL), BF16),
                        pltpu.VMEM((rows, LANES), F32),
                        pltpu.VMEM((rows, D_MODEL), F32),
                        pltpu.SemaphoreType.DMA((2,)),
                        pltpu.SemaphoreType.DMA((2,))])
    return pl.pallas_call(
        functools.partial(_moe_kernel, capc=capc, caps=caps),
        grid_spec=grid_spec,
        out_shape=[jax.ShapeDtypeStruct((zc_rows * ROW_TILE, LANES), F32),
                   jax.ShapeDtypeStruct((zs_rows * ROW_TILE, LANES), F32)],
        compiler_params=_cparams(("arbitrary", "arbitrary")),
        name="expert_ffn",
    )(idxc, idxs, qc, qs, hc, hs, wrx, w_gate, w_up, w_down)


Z_BUFFERS = 3


def _combine_kernel(clo_ref, chi_ref, z_ref, span_ref, x1_ref, mod_ref, g_ref, o_ref,
                    zbuf, acc_ref, sems, state, *, n_chunks):
    b = pl.program_id(0)
    chunk_rows = ROW_CHUNK * ROW_TILE

    @pl.when(b == 0)
    def _():
        state[0] = 0
        state[1] = 0

    def chunk_copy(c):
        src = z_ref.at[pl.ds(pl.multiple_of(c * chunk_rows, chunk_rows), chunk_rows), :]
        return pltpu.make_async_copy(src, zbuf.at[c % Z_BUFFERS], sems.at[c % Z_BUFFERS])

    acc_ref[...] = jnp.zeros_like(acc_ref)
    first_row = span_ref[:, 0:1] * LANES + span_ref[:, 1:2]
    end_row = span_ref[:, 2:3] * LANES + span_ref[:, 3:4]
    lane_row = lax.broadcasted_iota(I32, (TOK_BLOCK, ROW_CHUNK), 1)

    def body(c, carry):
        for _ in range(Z_BUFFERS):
            nxt = state[0]

            @pl.when(nxt <= jnp.minimum(c + Z_BUFFERS - 1, n_chunks - 1))
            def _():
                chunk_copy(nxt).start()
                state[0] = nxt + 1

        @pl.when(state[1] <= c)
        def _():
            chunk_copy(c).wait()
            state[1] = c + 1

        slot = c % Z_BUFFERS
        row = (c * ROW_CHUNK + lane_row).astype(F32)
        onehot = jnp.where((row >= first_row) & (row < end_row), 1.0, 0.0).astype(BF16)
        y = jnp.concatenate([zbuf[slot, pl.ds(kc, ROW_CHUNK, stride=ROW_TILE), :] for kc in range(ROW_TILE)],
                            axis=1)
        acc_ref[...] += _dot(onehot, y.astype(BF16))
        return carry

    lax.fori_loop(clo_ref[b], chi_ref[b], body, 0)
    gt2 = mod_ref[0][:, 5 * D_MODEL:6 * D_MODEL]
    o_ref[...] = x1_ref[...] + gt2 * _rms(acc_ref[...], g_ref[...], EPS)


def _combine(ranges, z, span, x1, mod3, mod_row, g_post_ffn):
    clo, chi = ranges
    t = x1.shape[0]
    tb = TOK_BLOCK
    grid_spec = pltpu.PrefetchScalarGridSpec(
        num_scalar_prefetch=2,
        grid=(t // tb,),
        in_specs=[pl.BlockSpec(memory_space=pl.ANY),
                  pl.BlockSpec((tb, LANES), lambda b, *_: (b, 0)),
                  pl.BlockSpec((tb, D_MODEL), lambda b, *_: (b, 0)),
                  pl.BlockSpec((1, 1, N_MOD * D_MODEL), lambda b, *_: (mod_row(b, tb), 0, 0)),
                  pl.BlockSpec((1, D_MODEL), lambda b, *_: (0, 0))],
        out_specs=pl.BlockSpec((tb, D_MODEL), lambda b, *_: (b, 0)),
        scratch_shapes=[pltpu.VMEM((Z_BUFFERS, ROW_CHUNK * ROW_TILE, LANES), F32),
                        pltpu.VMEM((tb, D_MODEL), F32),
                        pltpu.SemaphoreType.DMA((Z_BUFFERS,)),
                        pltpu.SMEM((2,), I32)])
    return pl.pallas_call(
        functools.partial(_combine_kernel, n_chunks=2 * t // ROW_CHUNK),
        grid_spec=grid_spec,
        out_shape=jax.ShapeDtypeStruct((t, D_MODEL), F32),
        compiler_params=_cparams(("arbitrary",)),
        name="combine",
    )(clo, chi, z, span, x1, mod3, g_post_ffn)


def _rope_tables(seq):
    half = HEAD_DIM // 4
    freqs = ROPE_THETA ** (-np.arange(half, dtype=np.float64) / half)
    s = np.arange(seq)
    row = (s // GRID_W)[:, None] * freqs[None, :]
    col = (s % GRID_W)[:, None] * freqs[None, :]
    ang = np.concatenate([row, row, col, col], axis=1)
    ang = np.tile(ang, (1, QK_W // HEAD_DIM))
    lane = np.arange(QK_W)
    sign = np.where((lane % 32) < 16, -1.0, 1.0)[None, :]
    return (jnp.asarray(np.cos(ang), dtype=F32), jnp.asarray(np.sin(ang) * sign, dtype=F32))


def _combine_ranges(rows, n_tok):
    step = TOK_BLOCK // LANES
    nb = n_tok // TOK_BLOCK
    lo = rows[0:nb * step:step, 0]
    hi = jnp.concatenate([lo[1:], jnp.full((1,), 2 * n_tok, I32)])
    return (lo // ROW_CHUNK).astype(I32), ((hi + ROW_CHUNK - 1) // ROW_CHUNK).astype(I32)


def kernel(x_prompt, x_sample, c, cache_k, cache_v, c_ctx, w_mod, b_mod, g_pre_mix, g_post_mix, g_pre_ffn, g_post_ffn, w_in, lam_q1, lam_k1, lam_q2, lam_k2, g_subln, w_proj_attn, w_proj_fourier, w_out, w_router, w_gate, w_up, w_down):
    assert w_mod.shape[0] == 1
    lam_init = 0.8 - 0.6 * math.exp(-0.3 * 0)
    bp, sp, _ = x_prompt.shape
    bs, ss, _ = x_sample.shape

    mod3 = _modulation(c_ctx, c, w_mod[0], b_mod)

    w_in_b = None
    wpa =w_proj_attn[0].astype(BF16)
    wpf = w_proj_fourier[0].astype(BF16)
    wout = w_out[0].astype(BF16)
    wrx = jnp.pad(w_router[0], ((0, 0), (0, LANES - N_EXPERTS))).astype(BF16)
    lam_p = jnp.concatenate([lam_q1, lam_k1, lam_q2, lam_k2], axis=0)

    groups = []
    for x, seq, positional, ctx in ((x_sample, ss, True, (cache_k, cache_v)),
                                    (x_prompt, sp, False, None)):
        nb = x.shape[0]
        t = nb * seq
        x2d = x.reshape(t, D_MODEL)
        if positional:
            mod_row = lambda i, tm, seq=seq: 1 + (i * tm) // seq
        else:
            mod_row = lambda i, tm: 0
        self_contained = ctx is None and not positional
        if self_contained:
            assert PRE_BLOCK % seq == 0
            pre = _ctx_mixer(x2d, mod3, mod_row, g_pre_mix, w_in_b, seq, (lam_p, g_subln, lam_init), tm=PRE_BLOCK)
            o, fm, ga, gf = pre[:4]
        else:
            q, k, v, f, ga, gf, w_in_b = _pre_mixer(x2d, mod3, mod_row, g_pre_mix, w_in[0],
                                                    _rope_tables(seq) if positional else None, seq, tm=ROW_BLOCK)
            o = _attention(lam_p, g_subln, q, k, v, ctx, seq, lam_init)
            fm = _fourier(f, seq)
        x1, h2t, aff_t = _post_mixer(o, fm, ga, gf, x2d, mod3, mod_row, g_post_mix, g_pre_ffn,
                                     wpa, wpf, wout, wrx)
        cap = 2 * t // N_EXPERTS
        assert t <= 1 << TOKEN_BITS
        posm, pack, span, ranges, rows = _route(aff_t, cap)
        idx, qslot = _slot_lists(ranges, posm, pack, cap)
        groups.append(dict(x1=x1, h2t=h2t, idx=idx, qslot=qslot, span=span, ranges=_combine_ranges(rows, t),
                           mod_row=mod_row, cache=pre[4:] if self_contained else None, shape=x.shape))

    gs_, gc = groups
    zc, zs = _moe(gc["idx"], gs_["idx"], gc["qslot"], gs_["qslot"], gc["h2t"], gs_["h2t"], wrx,
                  w_gate[0], w_up[0], w_down[0])
    outs = []
    for g, z in ((gc, zc), (gs_, zs)):
        out = _combine(g["ranges"], z, g["span"], g["x1"], mod3, g["mod_row"], g_post_ffn)
        outs.append(out.reshape(g["shape"]))
    new_k, new_v = gc["cache"]
    return (outs[0], outs[1], new_k, new_v)
```

```python
import functools
import math

import numpy as np
import jax
import jax.numpy as jnp
from jax import lax
from jax.experimental import pallas as pl
from jax.experimental.pallas import tpu as pltpu

F32 = jnp.float32
BF16 = jnp.bfloat16
I32 = jnp.int32

D_MODEL = 1024
N_HEADS = 6
HEAD_DIM = 64
V_DIM = 128
QK_W = 768
FOUR_W = 256
FOUR_G = 64
IN_W = 4608
N_EXPERTS = 16
D_FF = 2816
N_MOD = 6
GRID_W = 64
ROPE_THETA = 10000.0
EPS = 1e-6
SUBLN_EPS = 1e-5

LANES = 128
ROW_BLOCK = 256
PRE_BLOCK = 512
GATE_CHUNK = 256
POST_BLOCK = 1024
POST_SUB = 256
WEIGHT_ROWS = 128
WEIGHT_STAGES = 4
TOK_BLOCK = 256
ROW_CHUNK = 256
FF_TILE = 256
FF_STEPS = D_FF // FF_TILE
ROW_TILE = D_MODEL // LANES
TOKEN_BITS = 13
VMEM_LIMIT = 56 * 1024 * 1024


def _cparams(sem):
    return pltpu.CompilerParams(dimension_semantics=sem, vmem_limit_bytes=VMEM_LIMIT)


def _dot(a, b):
    return jnp.dot(a, b, preferred_element_type=F32)


def _rms(x, g, eps):
    return x * lax.rsqrt(jnp.mean(x * x, axis=-1, keepdims=True) + eps) * g


def _whole_vmem():
    return pl.BlockSpec(memory_space=pltpu.MemorySpace.VMEM)


def _cast_weight(w_hbm, wb_ref, stage_ref, sem):
    depth, rows = stage_ref.shape[0], stage_ref.shape[1]
    chunks = [pltpu.make_async_copy(w_hbm.at[pl.ds(r0, rows), :], stage_ref.at[n % depth], sem.at[n % depth])
              for n, r0 in enumerate(range(0, w_hbm.shape[0], rows))]
    for chunk in chunks[:depth - 1]:
        chunk.start()
    for n, chunk in enumerate(chunks):
        if n + depth - 1 < len(chunks):
            chunks[n + depth - 1].start()
        chunk.wait()
        wb_ref[n * rows:(n + 1) * rows, :] = stage_ref[n % depth].astype(BF16)


MOD_ROWS = 8


def _mod_kernel(cc_ref, c_ref, w_ref, b_ref, o_ref):
    pad = jnp.zeros((MOD_ROWS - 1 - c_ref.shape[0], D_MODEL), F32)
    c = jnp.concatenate([cc_ref[...], c_ref[...], pad], axis=0)
    s = c * jax.nn.sigmoid(c)
    res = _dot(s.astype(BF16), w_ref[...].astype(BF16)) + b_ref[...]
    for r in range(MOD_ROWS):
        o_ref[r] = res[r:r + 1, :]


def _modulation(c_ctx, c, w_mod, b_mod):
    tn = 1024
    n = N_MOD * D_MODEL
    assert 1 + c.shape[0] <= MOD_ROWS
    return pl.pallas_call(
        _mod_kernel,
        grid=(n // tn,),
        in_specs=[pl.BlockSpec((1, D_MODEL), lambda j: (0, 0)),
                  pl.BlockSpec(c.shape, lambda j: (0, 0)),
                  pl.BlockSpec((D_MODEL, tn), lambda j: (0, j)),
                  pl.BlockSpec((1, tn), lambda j: (0, j))],
        out_specs=pl.BlockSpec((MOD_ROWS, 1, tn), lambda j: (0, 0, j)),
        out_shape=jax.ShapeDtypeStruct((MOD_ROWS, 1, n), F32),
        compiler_params=_cparams(("arbitrary",)),
        name="modulation",
    )(c_ctx[None, :], c, w_mod, b_mod)


def _diff_lambda(lp, lam_init):
    s1 = jnp.sum(lp[0:1] * lp[1:2], axis=-1, keepdims=True)
    s2 = jnp.sum(lp[2:3] * lp[3:4], axis=-1, keepdims=True)
    return jnp.exp(s1) - jnp.exp(s2) + lam_init


def _attention_weights(q, k):
    comp1 = lax.broadcasted_iota(I32, (1, V_DIM), 1) < HEAD_DIM
    qs = q * jnp.asarray(HEAD_DIM ** -0.5, BF16)
    zero = jnp.zeros_like(qs)

    def weights(qc):
        s = lax.dot_general(qc, k, (((1,), (1,)), ((), ())), preferred_element_type=F32)
        return jnp.exp(s - jnp.max(s, axis=-1, keepdims=True)).astype(BF16)

    return weights(jnp.where(comp1, qs, zero)), weights(jnp.where(comp1, zero, qs))


def _attention_output(weights, v, lam, g_subln, lam_init):
    v_ones = jnp.concatenate([v, jnp.ones_like(v)], axis=1)

    def attend(ex):
        ov = _dot(ex, v_ones)
        return ov[:, 0:V_DIM] / ov[:, V_DIM:2 * V_DIM]

    o = attend(weights[0]) - lam * attend(weights[1])
    return (_rms(o, g_subln, SUBLN_EPS) * (1.0 - lam_init)).astype(BF16)


def _rope(z, cos, sin_signed, first_half):
    fwd = pltpu.roll(z, QK_W - 16, axis=1)
    bwd = pltpu.roll(z, 16, axis=1)
    return z * cos + jnp.where(first_half, fwd, bwd) * sin_signed


def _pre_kernel(*refs, positional):
    it = iter(refs)
    x_ref, mod_ref, g_ref, w_hbm = next(it), next(it), next(it), next(it)
    if positional:
        cos_ref, sin_ref = next(it), next(it)
    q_ref, k_ref, v_ref, f_ref, ga_ref, gf_ref = (next(it) for _ in range(6))
    wb_hbm, w_ref, stage_ref, sem = next(it), next(it), next(it), next(it)
    weight_out = pltpu.make_async_copy(w_ref, wb_hbm, sem.at[WEIGHT_STAGES])

    @pl.when(pl.program_id(0) == 0)
    def _():
        _cast_weight(w_hbm, w_ref, stage_ref, sem)
        weight_out.start()

    @pl.when(pl.program_id(0) == pl.num_programs(0) - 1)
    def _():
        weight_out.wait()

    m = mod_ref[0]
    sh1 = m[:, 0:D_MODEL]
    sc1 = m[:, D_MODEL:2 * D_MODEL]
    h = _rms(x_ref[...], g_ref[...], EPS) * (1.0 + sc1) + sh1
    hb = h.astype(BF16)

    def proj(lo, hi):
        return _dot(hb, w_ref[:, lo:hi])

    zq = proj(0, QK_W)
    zk = proj(QK_W, 2 * QK_W)
    zv = proj(2 * QK_W, 3 * QK_W)
    if positional:
        lane = lax.broadcasted_iota(I32, (1, QK_W), 1)
        first_half = (lane % 32) < 16
        cos = cos_ref[...]
        sin_signed = sin_ref[...]
        zq = _rope(zq, cos, sin_signed, first_half)
        zk = _rope(zk, cos, sin_signed, first_half)
    q_ref[...] = zq.astype(BF16)
    k_ref[...] = zk.astype(BF16)
    v_ref[...] = zv.astype(BF16)
    f0 = 3 * QK_W
    g0 = f0 + FOUR_W
    f_ref[...] = proj(f0, g0)
    ga_ref[...] = jax.nn.sigmoid(proj(g0, g0 + D_MODEL)).astype(BF16)
    gf_ref[...] = jax.nn.sigmoid(proj(g0 + D_MODEL, IN_W)).astype(BF16)


def _ctx_kernel(x_ref, mod_ref, g_ref, w_ref, bc_ref, bs_ref, cs_ref, ss_ref, lam_ref, gs_ref,
                o_ref, f_ref, ga_ref, gf_ref, kc_ref, vc_ref, *, seq, lam_init):
    m = mod_ref[0]
    sh1 = m[:, 0:D_MODEL]
    sc1 = m[:, D_MODEL:2 * D_MODEL]
    hb = (_rms(x_ref[...], g_ref[...], EPS) * (1.0 + sc1) + sh1).astype(BF16)
    seqs = [slice(b * seq, (b + 1) * seq) for b in range(x_ref.shape[0] // seq)]
    lam = _diff_lambda(lam_ref[...], lam_init)
    f0 = 3 * QK_W
    g0 = f0 + FOUR_W

    def proj(lo, hi):
        return _dot(hb, w_ref[:, lo:hi])

    q, k, v = {}, {}, {}

    def qkv_chunk(store, cache_ref, lo, c):
        cols = slice(c * GATE_CHUNK, (c + 1) * GATE_CHUNK)
        z = proj(lo + cols.start, lo + cols.stop)
        for hd in range(cols.start // V_DIM, cols.stop // V_DIM):
            z_hd = z[:, hd * V_DIM - cols.start:(hd + 1) * V_DIM - cols.start]
            if cache_ref is not None:
                for b, rs in enumerate(seqs):
                    cache_ref[b, 0, hd] = z_hd[rs, :]
            store[hd] = z_hd.astype(BF16)

    def fourier_chunk():
        f = proj(f0, g0)
        for rs in seqs:
            f_ref[rs, :] = _dft_real(f[rs, :], bc_ref, bs_ref, cs_ref, ss_ref)

    def gate_chunk(ref, lo, c):
        cols = slice(c * GATE_CHUNK, (c + 1) * GATE_CHUNK)
        ref[:, cols] = jax.nn.sigmoid(proj(lo + cols.start, lo + cols.stop)).astype(BF16)

    for store, cache, lo in ((q, None, 0), (k, kc_ref, QK_W), (v, vc_ref, 2 * QK_W)):
        for c in range(QK_W // GATE_CHUNK):
            qkv_chunk(store, cache, lo, c)
    matmul_work = [fourier_chunk] + [functools.partial(gate_chunk, ref, lo, c)
                                     for ref, lo in ((ga_ref, g0), (gf_ref, g0 + D_MODEL))
                                     for c in range(D_MODEL // GATE_CHUNK)]
    for rs in seqs:
        for hd in range(N_HEADS):
            weights = _attention_weights(q[hd][rs, :], k[hd][rs, :])
            if matmul_work:
                matmul_work.pop(0)()
            o_ref[rs, hd * V_DIM:(hd + 1) * V_DIM] = _attention_output(weights, v[hd][rs, :], lam, gs_ref[...],
                                                                      lam_init)
    for work in matmul_work:
        work()


def _ctx_mixer(x2d, mod3, mod_row, g_pre, w_in_b, seq, attn_params, tm):
    t = x2d.shape[0]
    lam_p, g_subln, lam_init = attn_params
    assert tm % seq == 0 and t % tm == 0
    row = lambda i: (i, 0)
    const = lambda i: (0, 0)
    consts = _dft_consts(seq) + (lam_p, g_subln)
    nb = t // seq
    cshape = jax.ShapeDtypeStruct((nb, 1, N_HEADS, seq, V_DIM), F32)
    cspec = pl.BlockSpec((tm // seq, 1, N_HEADS, seq, V_DIM), lambda i: (i, 0, 0, 0, 0))
    return pl.pallas_call(
        functools.partial(_ctx_kernel, seq=seq, lam_init=lam_init),
        grid=(t // tm,),
        in_specs=[pl.BlockSpec((tm, D_MODEL), row),
                  pl.BlockSpec((1, 1, N_MOD * D_MODEL), lambda i: (mod_row(i, tm), 0, 0)),
                  pl.BlockSpec((1, D_MODEL), const),
                  _whole_vmem()] + [pl.BlockSpec(c.shape, const) for c in consts],
        out_specs=[pl.BlockSpec((tm, QK_W), row),
                   pl.BlockSpec((tm, FOUR_W), row),
                   pl.BlockSpec((tm, D_MODEL), row),
                   pl.BlockSpec((tm, D_MODEL), row),
                   cspec, cspec],
        out_shape=[jax.ShapeDtypeStruct((t, QK_W), BF16),
                   jax.ShapeDtypeStruct((t, FOUR_W), BF16),
                   jax.ShapeDtypeStruct((t, D_MODEL), BF16),
                   jax.ShapeDtypeStruct((t, D_MODEL), BF16),
                   cshape, cshape],
        compiler_params=_cparams(("arbitrary",)),
        name="ctx_mixer",
    )(x2d, mod3, g_pre, w_in_b, *consts)


def _pre_mixer(x2d, mod3, mod_row, g_pre, w_in, rope_tabs, seq, tm):
    t = x2d.shape[0]
    positional = rope_tabs is not None
    assert seq % tm == 0
    blocks_per_seq = seq // tm
    row = lambda i: (i, 0)
    any_spec = pl.BlockSpec(memory_space=pl.ANY)
    in_specs = [pl.BlockSpec((tm, D_MODEL), row),
                pl.BlockSpec((1, 1, N_MOD * D_MODEL), lambda i: (mod_row(i, tm), 0, 0)),
                pl.BlockSpec((1, D_MODEL), lambda i: (0, 0)),
                any_spec]
    args = [x2d, mod3, g_pre, w_in]
    if positional:
        in_specs += [pl.BlockSpec((tm, QK_W), lambda i: (i % blocks_per_seq, 0))] * 2
        args += list(rope_tabs)
    out_shape = [jax.ShapeDtypeStruct((t, QK_W), BF16)] * 3 + [
        jax.ShapeDtypeStruct((t, FOUR_W), F32),
        jax.ShapeDtypeStruct((t, D_MODEL), BF16),
        jax.ShapeDtypeStruct((t, D_MODEL), BF16),
        jax.ShapeDtypeStruct(w_in.shape, BF16)]
    out_specs = [pl.BlockSpec((tm, QK_W), row)] * 3 + [
        pl.BlockSpec((tm, FOUR_W), row),
        pl.BlockSpec((tm, D_MODEL), row),
        pl.BlockSpec((tm, D_MODEL), row),
        any_spec]
    return pl.pallas_call(
        functools.partial(_pre_kernel, positional=positional),
        grid=(t // tm,),
        in_specs=in_specs,
        out_specs=out_specs,
        out_shape=out_shape,
        scratch_shapes=[pltpu.VMEM(w_in.shape, BF16),
                        pltpu.VMEM((WEIGHT_STAGES, WEIGHT_ROWS, w_in.shape[1]), F32),
                        pltpu.SemaphoreType.DMA((WEIGHT_STAGES + 1,))],
        compiler_params=_cparams(("arbitrary",)),
        name="pre_mixer",
    )(*args)


def _attn_kernel(*refs, lam_init, has_ctx):
    it = iter(refs)
    lam_ref, gs_ref, q_ref, k_ref, v_ref = (next(it) for _ in range(5))
    if has_ctx:
        ck_ref, cv_ref = next(it), next(it)
    o_ref = next(it)

    lam = _diff_lambda(lam_ref[...], lam_init)

    def operand(ref, cache_ref, hd):
        x = ref[:, hd * V_DIM:(hd + 1) * V_DIM]
        return jnp.concatenate([cache_ref[0, 0, hd].astype(BF16), x], axis=0) if has_ctx else x

    def head_weights(hd):
        return _attention_weights(q_ref[:, hd * V_DIM:(hd + 1) * V_DIM], operand(k_ref, ck_ref if has_ctx else None, hd))

    weights = head_weights(0)
    for hd in range(N_HEADS):
        nxt = head_weights(hd + 1) if hd + 1 < N_HEADS else None
        v = operand(v_ref, cv_ref if has_ctx else None, hd)
        o_ref[:, hd * V_DIM:(hd + 1) * V_DIM] = _attention_output(weights, v, lam, gs_ref[...], lam_init)
        weights = nxt


def _attention(lam_p, g_subln, q, k, v, ctx, seq, lam_init):
    t = q.shape[0]
    tq = ROW_BLOCK
    qb = seq // tq
    has_ctx = ctx is not None
    in_specs = [pl.BlockSpec((4, HEAD_DIM), lambda b, i: (0, 0)),
                pl.BlockSpec((1, V_DIM), lambda b, i: (0, 0)),
                pl.BlockSpec((tq, QK_W), lambda b, i: (b * qb + i, 0)),
                pl.BlockSpec((seq, QK_W), lambda b, i: (b, 0)),
                pl.BlockSpec((seq, QK_W), lambda b, i: (b, 0))]
    args = [lam_p, g_subln, q, k, v]
    if has_ctx:
        past = ctx[0].shape[3]
        cspec = pl.BlockSpec((1, 1, N_HEADS, past, V_DIM), lambda b, i: (b, 0, 0, 0, 0))
        in_specs += [cspec, cspec]
        args += list(ctx)
    return pl.pallas_call(
        functools.partial(_attn_kernel, lam_init=lam_init, has_ctx=has_ctx),
        grid=(t // seq, qb),
        in_specs=in_specs,
        out_specs=pl.BlockSpec((tq, QK_W), lambda b, i: (b * qb + i, 0)),
        out_shape=jax.ShapeDtypeStruct((t, QK_W), BF16),
        compiler_params=_cparams(("arbitrary", "arbitrary")),
        name="diff_attention",
    )(*args)


def _dft_real(f, bc_ref, bs_ref, cs_ref, ss_ref):
    fb = f.astype(BF16)
    u = _dot(fb, bc_ref[...].astype(BF16)).astype(BF16)
    w = _dot(fb, bs_ref[...].astype(BF16)).astype(BF16)
    return (_dot(cs_ref[...].astype(BF16), u) - _dot(ss_ref[...].astype(BF16), w)).astype(BF16)


def _fourier_kernel(f_ref, bc_ref, bs_ref, cs_ref, ss_ref, o_ref):
    o_ref[...] = _dft_real(f_ref[...], bc_ref, bs_ref, cs_ref, ss_ref)


def _dft_consts(seq):
    c = np.arange(FOUR_G)
    ang_c = 2.0 * np.pi * ((c[:, None] * c[None, :]) % FOUR_G) / FOUR_G
    eye = np.eye(FOUR_W // FOUR_G)
    bc = np.kron(eye, np.cos(ang_c)) / math.sqrt(FOUR_G)
    bs = np.kron(eye, np.sin(ang_c)) / math.sqrt(FOUR_G)
    s = np.arange(seq)
    ang_s = 2.0 * np.pi * ((s[:, None] * s[None, :]) % seq) / seq
    cs = np.cos(ang_s) / math.sqrt(seq)
    ss = np.sin(ang_s) / math.sqrt(seq)
    return tuple(jnp.asarray(a, dtype=F32) for a in (bc, bs, cs, ss))


def _fourier(f, seq):
    t = f.shape[0]
    bc, bs, cs, ss = _dft_consts(seq)
    const = lambda b: (0, 0)
    return pl.pallas_call(
        _fourier_kernel,
        grid=(t // seq,),
        in_specs=[pl.BlockSpec((seq, FOUR_W), lambda b: (b, 0)),
                  pl.BlockSpec((FOUR_W, FOUR_W), const),
                  pl.BlockSpec((FOUR_W, FOUR_W), const),
                  pl.BlockSpec((seq, seq), const),
                  pl.BlockSpec((seq, seq), const)],
        out_specs=pl.BlockSpec((seq, FOUR_W), lambda b: (b, 0)),
        out_shape=jax.ShapeDtypeStruct((t, FOUR_W), BF16),
        compiler_params=_cparams(("arbitrary",)),
        name="fourier_mix",
    )(f, bc, bs, cs, ss)


def _post_kernel(o_ref, fm_ref, ga_ref, gf_ref, x_ref, mod_ref, gpost_ref, gffn_ref,
                 wpa_ref, wpf_ref, wout_ref, wrx_ref,
                 x1_ref, h2t_ref, afft_ref):
    m = mod_ref[0]
    gt1 = m[:, 2 * D_MODEL:3 * D_MODEL]
    sh2 = m[:, 3 * D_MODEL:4 * D_MODEL]
    sc2 = m[:, 4 * D_MODEL:5 * D_MODEL]
    subs = [slice(r0, r0 + POST_SUB) for r0 in range(0, o_ref.shape[0], POST_SUB)]
    ab = [(_dot(o_ref[rs, :], wpa_ref[...]), _dot(fm_ref[rs, :], wpf_ref[...])) for rs in subs]
    merged = [(ga_ref[rs, :] * a + gf_ref[rs, :] * b).astype(BF16) for rs, (a, b) in zip(subs, ab)]
    ys = [_dot(mg, wout_ref[...]) for mg in merged]
    h2s = []
    for rs, y in zip(subs, ys):
        x1 = x_ref[rs, :] + gt1 * _rms(y, gpost_ref[...], EPS)
        x1_ref[rs, :] = x1
        h2s.append(_rms(x1, gffn_ref[...], EPS) * (1.0 + sc2) + sh2)
    logits = [_dot(h2.astype(BF16), wrx_ref[...]) for h2 in h2s]
    for rs, h2, lg in zip(subs, h2s, logits):
        lt = lg.T[0:N_EXPERTS]
        et = jnp.exp(lt - jnp.max(lt, axis=0, keepdims=True))
        aff = et / jnp.sum(et, axis=0, keepdims=True)
        for u in range(POST_SUB // LANES):
            afft_ref[:, rs.start // LANES + u, :] = aff[:, u * LANES:(u + 1) * LANES]
        for kc in range(ROW_TILE):
            h2t_ref[pl.ds(rs.start * ROW_TILE + kc, POST_SUB, stride=ROW_TILE), :] = h2[:, kc * LANES:(kc + 1) * LANES]


def _post_mixer(o, fm, ga, gf, x2d, mod3, mod_row, g_post, g_ffn, wpa, wpf, wout, wrx):
    t = x2d.shape[0]
    tm = POST_BLOCK
    row = lambda i: (i, 0)
    const = lambda i: (0, 0)
    return pl.pallas_call(
        _post_kernel,
        grid=(t // tm,),
        in_specs=[pl.BlockSpec((tm, QK_W), row),
                  pl.BlockSpec((tm, FOUR_W), row),
                  pl.BlockSpec((tm, D_MODEL), row),
                  pl.BlockSpec((tm, D_MODEL), row),
                  pl.BlockSpec((tm, D_MODEL), row),
                  pl.BlockSpec((1, 1, N_MOD * D_MODEL), lambda i: (mod_row(i, tm), 0, 0)),
                  pl.BlockSpec((1, D_MODEL), const),
                  pl.BlockSpec((1, D_MODEL), const),
                  pl.BlockSpec((QK_W, D_MODEL), const),
                  pl.BlockSpec((FOUR_W, D_MODEL), const),
                  pl.BlockSpec((D_MODEL, D_MODEL), const),
                  pl.BlockSpec((D_MODEL, LANES), const)],
        out_specs=[pl.BlockSpec((tm, D_MODEL), row),
                   pl.BlockSpec((tm * ROW_TILE, LANES), row),
                   pl.BlockSpec((N_EXPERTS, tm // LANES, LANES), lambda i: (0, i, 0))],
        out_shape=[jax.ShapeDtypeStruct((t, D_MODEL), F32),
                   jax.ShapeDtypeStruct((t * ROW_TILE, LANES), F32),
                   jax.ShapeDtypeStruct((N_EXPERTS, t // LANES, LANES), F32)],
        compiler_params=_cparams(("arbitrary",)),
        name="post_mixer",
    )(o, fm, ga, gf, x2d, mod3, g_post, g_ffn, wpa, wpf, wout, wrx)


RANGE_ROWS = 8


def _route_kernel(aff_ref, posm_ref, pack_ref, span_ref, ranges_ref, rows_ref, *, cap, n_tok):
    aff = aff_ref[...]
    nt = n_tok // LANES
    capf = float(cap)

    def count_ge(v):
        return jnp.sum(jnp.where(aff >= v, 1.0, 0.0), axis=(1, 2), keepdims=True)

    def search(i, thr):
        cand = thr | jnp.left_shift(jnp.int32(1), 30 - i)
        return jnp.where(count_ge(pltpu.bitcast(cand, F32)) >= capf, cand, thr)

    thr = lax.fori_loop(0, 31, search, jnp.zeros((N_EXPERTS, 1, 1), I32))
    lo = pltpu.bitcast(thr, F32)
    hi = pltpu.bitcast(thr + 1, F32)

    def refine(i, c):
        lo, hi = c
        mid = lo + (hi - lo) * 0.5
        ok = count_ge(mid) >= capf
        return jnp.where(ok, mid, lo), jnp.where(ok, hi, mid)

    lo, hi = lax.fori_loop(0, 12, refine, (lo, hi))
    gt = aff >= hi
    eq = (aff >= lo) & (aff < hi)
    n_tie = capf - jnp.sum(jnp.where(gt, 1.0, 0.0), axis=(1, 2), keepdims=True)

    sq0 = lax.broadcasted_iota(I32, (LANES, LANES), 0)
    sq1 = lax.broadcasted_iota(I32, (LANES, LANES), 1)
    along_total = jnp.concatenate([jnp.where(sq0 <= sq1, 1.0, 0.0), jnp.ones((LANES, LANES), F32)],
                                  axis=1).astype(BF16)
    m = N_EXPERTS * nt
    r0 = lax.broadcasted_iota(I32, (m, m), 0)
    r1 = lax.broadcasted_iota(I32, (m, m), 1)
    earlier = jnp.where((r0 // nt == r1 // nt) & (r1 < r0), 1.0, 0.0).astype(BF16)
    lane = lax.broadcasted_iota(I32, (1, LANES), 1)
    token = lax.broadcasted_iota(I32, (nt, LANES), 0) * LANES + lane

    def tile_counts(x):
        both = _dot(x.reshape(m, LANES).astype(BF16), along_total)
        total = both[:, LANES:]
        before = _dot(earlier, total.astype(BF16))
        shape = (N_EXPERTS, nt, LANES)
        return both[:, :LANES].reshape(shape), total.reshape(shape), before.reshape(shape)

    eq_f = jnp.where(eq, 1.0, 0.0)
    eq_along, _, eq_before = tile_counts(eq_f)
    sel = jnp.where(gt, 1.0, jnp.where(eq_along + eq_before <= n_tie, eq_f, 0.0))
    sel_along, sel_total, sel_before = tile_counts(sel)
    posm_ref[...] = jnp.where(sel > 0.5, sel_along + sel_before - sel, -1.0).astype(I32)

    cnt = jnp.sum(sel, axis=0)
    rows_before = jnp.sum(sel_before, axis=0)
    tok_start = _dot(cnt.astype(BF16), along_total[:, :LANES]) - cnt + rows_before
    k = jnp.zeros((nt, LANES), F32)
    slots_before = jnp.zeros((nt, LANES), F32)
    slots_here = jnp.zeros((nt, LANES), F32)
    for e in range(N_EXPERTS):
        pack_ref[e] = (tok_start + k).astype(I32) * (1 << TOKEN_BITS) + token
        k = k + sel[e]
        slots_before = jnp.where(lane == e, sel_before[e], slots_before)
        slots_here = jnp.where(lane == e, sel_total[e], slots_here)

    eye = jnp.where(sq0 == sq1, 1.0, 0.0).astype(BF16)
    tok_end = tok_start + cnt
    parts = []
    for v in (tok_start, tok_end):
        high = jnp.floor(v * (1.0 / LANES))
        parts += [high, v - high * LANES]
    for c in range(nt):
        rows4 = jnp.zeros((LANES, LANES), F32)
        for r, part in enumerate(parts):
            rows4 = jnp.where(sq0 == r, jnp.broadcast_to(part[c:c + 1, :], (LANES, LANES)), rows4)
        span_ref[c * LANES:(c + 1) * LANES, :] = lax.dot_general(
            eye, rows4.astype(BF16), (((1,), (1,)), ((), ())), preferred_element_type=F32)

    ranges_ref[...] = jnp.zeros_like(ranges_ref)
    for s in range(cap // LANES):
        done = jnp.where(slots_before + slots_here <= float(s * LANES), 1.0, 0.0)
        begun = jnp.where(slots_before < float((s + 1) * LANES), 1.0, 0.0)
        ranges_ref[s:s + 1, :] = jnp.sum(done, axis=0, keepdims=True).astype(I32)
        ranges_ref[RANGE_ROWS + s:RANGE_ROWS + s + 1, :] = jnp.sum(begun, axis=0, keepdims=True).astype(I32)
    rows_ref[...] = rows_before.astype(I32)


def _route(aff, cap):
    n_e, nt, _ = aff.shape
    n_tok = nt * LANES
    assert cap // LANES <= RANGE_ROWS
    return pl.pallas_call(
        functools.partial(_route_kernel, cap=cap, n_tok=n_tok),
        out_shape=[jax.ShapeDtypeStruct((n_e, nt, LANES), I32),
                   jax.ShapeDtypeStruct((n_e, nt, LANES), I32),
                   jax.ShapeDtypeStruct((n_tok, LANES), F32),
                   jax.ShapeDtypeStruct((2 * RANGE_ROWS, LANES), I32),
                   jax.ShapeDtypeStruct((nt, LANES), I32)],
        compiler_params=pltpu.CompilerParams(vmem_limit_bytes=VMEM_LIMIT),
        name="route",
    )(aff)


def _slots_kernel(ranges_ref, posm_ref, pack_ref, idx_ref, qslot_ref, acc_ref):
    e = pl.program_id(0)
    sub = lax.broadcasted_iota(I32, (LANES, LANES), 0)
    eye = sub == lax.broadcasted_iota(I32, (LANES, LANES), 1)
    n_tiles = idx_ref.shape[1]

    for s in range(n_tiles):
        slot = sub + s * LANES

        def body(c, acc, slot=slot):
            hit = posm_ref[e, pl.ds(c, 1), :] == slot
            return acc + jnp.where(hit, pack_ref[e, pl.ds(c, 1), :], 0)

        acc_ref[s] = lax.fori_loop(ranges_ref[s, e], ranges_ref[RANGE_ROWS + s, e], body,
                                   jnp.zeros((LANES, LANES), I32))

    def as_row(part):
        col = jnp.sum(part.astype(F32), axis=1, keepdims=True)
        return jnp.sum(jnp.where(eye, col, 0.0), axis=0, keepdims=True).astype(I32)

    for s in range(n_tiles):
        acc = acc_ref[s]
        idx_ref[0, s:s + 1, :] = as_row(acc & ((1 << TOKEN_BITS) - 1)) * ROW_TILE
        qslot_ref[0, s:s + 1, :] = as_row(lax.shift_right_logical(acc, TOKEN_BITS)) * ROW_TILE


def _slot_lists(ranges, posm4, qdst4, cap):
    ns = cap // LANES
    grid_spec = pltpu.PrefetchScalarGridSpec(
        num_scalar_prefetch=1,
        grid=(N_EXPERTS,),
        in_specs=[_whole_vmem(), _whole_vmem()],
        out_specs=[pl.BlockSpec((1, ns, LANES), lambda e, *_: (e, 0, 0)),
                   pl.BlockSpec((1, ns, LANES), lambda e, *_: (e, 0, 0))],
        scratch_shapes=[pltpu.VMEM((ns, LANES, LANES), I32)])
    idx, qslot = pl.pallas_call(
        _slots_kernel,
        grid_spec=grid_spec,
        out_shape=[jax.ShapeDtypeStruct((N_EXPERTS, ns, LANES), I32),
                   jax.ShapeDtypeStruct((N_EXPERTS, ns, LANES), I32)],
        compiler_params=_cparams(("arbitrary",)),
        name="slot_lists",
    )(ranges, posm4, qdst4)
    return idx.reshape(N_EXPERTS, cap), qslot.reshape(N_EXPERTS, cap)


def _moe_kernel(idxc_ref, idxs_ref, qc_ref, qs_ref,
                hc_ref, hs_ref, wr_ref, wg_ref, wu_ref, wd_ref, zc_ref, zs_ref,
                xbuf, ybuf, xb_ref, gate_ref, acc_ref, gsem, ssem, *, capc, caps):
    e = pl.program_id(0)
    j = pl.program_id(1)
    n_e = pl.num_programs(0)
    n_j = FF_STEPS
    slot = e % 2
    other = 1 - slot
    rows = capc + caps
    gc, gs = _per_step(capc), _per_step(caps)
    groups = ((hc_ref, idxc_ref, zc_ref, qc_ref, gc, 0), (hs_ref, idxs_ref, zs_ref, qs_ref, gs, gc * n_j))

    def tile(ref, first_sublane):
        return ref.at[pl.ds(pl.multiple_of(first_sublane, ROW_TILE), ROW_TILE), :]

    def gather(ex, sl, step, i, group):
        h_ref, idx_ref, _, _, per_step, base = group
        p = step * per_step + i
        src = tile(h_ref, idx_ref[ex * (per_step * n_j) + p])
        pltpu.make_async_copy(src, xbuf.at[sl, :, base + p, :], gsem.at[sl]).start()

    def scatter(table_row, sl, step, i, group):
        _, _, z_ref, q_ref, per_step, base = group
        p = step * per_step + i
        dst = tile(z_ref, q_ref[table_row * (per_step * n_j) + p])
        pltpu.make_async_copy(ybuf.at[sl, :, base + p, :], dst, ssem.at[sl]).start()

    def all_steps(fn):
        for group in groups:
            def body(p, carry, group=group):
                fn(p, group)
                return carry
            lax.fori_loop(0, group[4] * n_j, body, 0, unroll=8)

    def wait_all(buf, sem, sl):
        pltpu.make_async_copy(buf.at[sl], buf.at[sl], sem.at[sl]).wait()

    @pl.when((e == 0) & (j == 0))
    def _():
        ybuf[...] = jnp.zeros_like(ybuf)
        all_steps(lambda p, group: gather(0, 0, 0, p, group))

    @pl.when(j == 0)
    def _():
        wait_all(xbuf, gsem, slot)
        for base, n, dst in ((0, capc, 0), (gc * n_j, caps, capc)):
            for kc in range(ROW_TILE):
                xb_ref[dst:dst + n, kc * LANES:(kc + 1) * LANES] = xbuf[slot, kc, base:base + n, :].astype(BF16)
        acc_ref[...] = jnp.zeros_like(acc_ref)
        logits = _dot(xb_ref[...], wr_ref[...])
        lane = lax.broadcasted_iota(I32, (1, LANES), 1)
        is_expert = lane < N_EXPERTS
        ex = jnp.exp(logits - jnp.max(jnp.where(is_expert, logits, -jnp.inf), axis=-1, keepdims=True))
        mine = jnp.sum(jnp.where(lane == e, ex, 0.0), axis=-1, keepdims=True)
        gate = mine / jnp.sum(jnp.where(is_expert, ex, 0.0), axis=-1, keepdims=True)
        gate_ref[...] = jnp.broadcast_to(gate, gate_ref.shape)

    nxt = jnp.minimum(e + 1, n_e - 1)
    for group in groups:
        for i in range(group[4]):
            gather(nxt, other, j, i, group)
            scatter(e, other, j, i, group)

    x = xb_ref[...]
    g = _dot(x, wg_ref[0].astype(BF16))
    u = _dot(x, wu_ref[0].astype(BF16))
    hid = (g * jax.nn.sigmoid(g) * u).astype(BF16)
    acc_ref[...] += _dot(hid, wd_ref[0].astype(BF16))

    @pl.when(j == n_j - 1)
    def _():
        @pl.when(e >= 1)
        def _():
            wait_all(ybuf, ssem, slot)

        for base, n, src in ((0, capc, 0), (gc * n_j, caps, capc)):
            for r0 in range(0, n, LANES):
                nr = min(LANES, n - r0)
                gate = gate_ref[src + r0:src + r0 + nr, :]
                for kc in range(ROW_TILE):
                    y = acc_ref[src + r0:src + r0 + nr, kc * LANES:(kc + 1) * LANES] * gate
                    ybuf[slot, kc, base + r0:base + r0 + nr, :] = y

        @pl.when(e == n_e - 1)
        def _():
            all_steps(lambda p, group: scatter(e + 1, slot, 0, p, group))
            wait_all(ybuf, ssem, other)
            wait_all(ybuf, ssem, slot)
            wait_all(xbuf, gsem, other)


def _per_step(cap):
    return -(-cap // FF_STEPS)


def _copy_tables(idx, qslot, n_rows):
    n_e, cap = idx.shape
    padded = _per_step(cap) * FF_STEPS
    n_pad = padded - cap
    idx_p = jnp.concatenate([idx, jnp.zeros((n_e, n_pad), I32)], axis=1)
    spare = n_rows + jnp.arange(padded + n_e * n_pad, dtype=I32)
    lead = spare[:padded][None, :]
    pad_rows = spare[padded:].reshape(n_e, n_pad)
    q_p = jnp.concatenate([lead, jnp.concatenate([qslot // ROW_TILE, pad_rows], axis=1)], axis=0) * ROW_TILE
    return idx_p.reshape(-1), q_p.reshape(-1), n_rows + padded + n_e * n_pad


def _moe(idxc, idxs, qc, qs, hc, hs, wrx, w_gate, w_up, w_down):
    capc, caps = idxc.shape[1], idxs.shape[1]
    rows = capc + caps
    tf = FF_TILE
    idxc, qc, zc_rows = _copy_tables(idxc, qc, N_EXPERTS * capc)
    idxs, qs, zs_rows = _copy_tables(idxs, qs, N_EXPERTS * caps)
    buf_rows = (_per_step(capc) + _per_step(caps)) * FF_STEPS
    any_spec = pl.BlockSpec(memory_space=pl.ANY)
    grid_spec = pltpu.PrefetchScalarGridSpec(
        num_scalar_prefetch=4,
        grid=(N_EXPERTS, FF_STEPS),
        in_specs=[any_spec, any_spec,
                  pl.BlockSpec((D_MODEL, LANES), lambda e, j, *_: (0, 0)),
                  pl.BlockSpec((1, D_MODEL, tf), lambda e, j, *_: (e, 0, j)),
                  pl.BlockSpec((1, D_MODEL, tf), lambda e, j, *_: (e, 0, j)),
                  pl.BlockSpec((1, tf, D_MODEL), lambda e, j, *_: (e, j, 0))],
        out_specs=[any_spec, any_spec],
        scratch_shapes=[pltpu.VMEM((2, ROW_TILE, buf_rows, LANES), F32),
                        pltpu.VMEM((2, ROW_TILE, buf_rows, LANES), F32),
                        pltpu.VMEM((rows, D_MODEL), BF16),
                        pltpu.VMEM((rows, LANES), F32),
                        pltpu.VMEM((rows, D_MODEL), F32),
                        pltpu.SemaphoreType.DMA((2,)),
                        pltpu.SemaphoreType.DMA((2,))])
    return pl.pallas_call(
        functools.partial(_moe_kernel, capc=capc, caps=caps),
        grid_spec=grid_spec,
        out_shape=[jax.ShapeDtypeStruct((zc_rows * ROW_TILE, LANES), F32),
                   jax.ShapeDtypeStruct((zs_rows * ROW_TILE, LANES), F32)],
        compiler_params=_cparams(("arbitrary", "arbitrary")),
        name="expert_ffn",
    )(idxc, idxs, qc, qs, hc, hs, wrx, w_gate, w_up, w_down)


Z_BUFFERS = 3


def _combine_kernel(clo_ref, chi_ref, z_ref, span_ref, x1_ref, mod_ref, g_ref, o_ref,
                    zbuf, acc_ref, sems, state, *, n_chunks):
    b = pl.program_id(0)
    chunk_rows = ROW_CHUNK * ROW_TILE

    @pl.when(b == 0)
    def _():
        state[0] = 0
        state[1] = 0

    def chunk_copy(c):
        src = z_ref.at[pl.ds(pl.multiple_of(c * chunk_rows, chunk_rows), chunk_rows), :]
        return pltpu.make_async_copy(src, zbuf.at[c % Z_BUFFERS], sems.at[c % Z_BUFFERS])

    acc_ref[...] = jnp.zeros_like(acc_ref)
    first_row = span_ref[:, 0:1] * LANES + span_ref[:, 1:2]
    end_row = span_ref[:, 2:3] * LANES + span_ref[:, 3:4]
    lane_row = lax.broadcasted_iota(I32, (TOK_BLOCK, ROW_CHUNK), 1)

    def body(c, carry):
        for _ in range(Z_BUFFERS):
            nxt = state[0]

            @pl.when(nxt <= jnp.minimum(c + Z_BUFFERS - 1, n_chunks - 1))
            def _():
                chunk_copy(nxt).start()
                state[0] = nxt + 1

        @pl.when(state[1] <= c)
        def _():
            chunk_copy(c).wait()
            state[1] = c + 1

        slot = c % Z_BUFFERS
        row = (c * ROW_CHUNK + lane_row).astype(F32)
        onehot = jnp.where((row >= first_row) & (row < end_row), 1.0, 0.0).astype(BF16)
        y = jnp.concatenate([zbuf[slot, pl.ds(kc, ROW_CHUNK, stride=ROW_TILE), :] for kc in range(ROW_TILE)],
                            axis=1)
        acc_ref[...] += _dot(onehot, y.astype(BF16))
        return carry

    lax.fori_loop(clo_ref[b], chi_ref[b], body, 0)
    gt2 = mod_ref[0][:, 5 * D_MODEL:6 * D_MODEL]
    o_ref[...] = x1_ref[...] + gt2 * _rms(acc_ref[...], g_ref[...], EPS)


def _combine(ranges, z, span, x1, mod3, mod_row, g_post_ffn):
    clo, chi = ranges
    t = x1.shape[0]
    tb = TOK_BLOCK
    grid_spec = pltpu.PrefetchScalarGridSpec(
        num_scalar_prefetch=2,
        grid=(t // tb,),
        in_specs=[pl.BlockSpec(memory_space=pl.ANY),
                  pl.BlockSpec((tb, LANES), lambda b, *_: (b, 0)),
                  pl.BlockSpec((tb, D_MODEL), lambda b, *_: (b, 0)),
                  pl.BlockSpec((1, 1, N_MOD * D_MODEL), lambda b, *_: (mod_row(b, tb), 0, 0)),
                  pl.BlockSpec((1, D_MODEL), lambda b, *_: (0, 0))],
        out_specs=pl.BlockSpec((tb, D_MODEL), lambda b, *_: (b, 0)),
        scratch_shapes=[pltpu.VMEM((Z_BUFFERS, ROW_CHUNK * ROW_TILE, LANES), F32),
                        pltpu.VMEM((tb, D_MODEL), F32),
                        pltpu.SemaphoreType.DMA((Z_BUFFERS,)),
                        pltpu.SMEM((2,), I32)])
    return pl.pallas_call(
        functools.partial(_combine_kernel, n_chunks=2 * t // ROW_CHUNK),
        grid_spec=grid_spec,
        out_shape=jax.ShapeDtypeStruct((t, D_MODEL), F32),
        compiler_params=_cparams(("arbitrary",)),
        name="combine",
    )(clo, chi, z, span, x1, mod3, g_post_ffn)


def _rope_tables(seq):
    half = HEAD_DIM // 4
    freqs = ROPE_THETA ** (-np.arange(half, dtype=np.float64) / half)
    s = np.arange(seq)
    row = (s // GRID_W)[:, None] * freqs[None, :]
    col = (s % GRID_W)[:, None] * freqs[None, :]
    ang = np.concatenate([row, row, col, col], axis=1)
    ang = np.tile(ang, (1, QK_W // HEAD_DIM))
    lane = np.arange(QK_W)
    sign = np.where((lane % 32) < 16, -1.0, 1.0)[None, :]
    return (jnp.asarray(np.cos(ang), dtype=F32), jnp.asarray(np.sin(ang) * sign, dtype=F32))


def _combine_ranges(rows, n_tok):
    step = TOK_BLOCK // LANES
    nb = n_tok // TOK_BLOCK
    lo = rows[0:nb * step:step, 0]
    hi = jnp.concatenate([lo[1:], jnp.full((1,), 2 * n_tok, I32)])
    return (lo // ROW_CHUNK).astype(I32), ((hi + ROW_CHUNK - 1) // ROW_CHUNK).astype(I32)


def kernel(x_prompt, x_sample, c, cache_k, cache_v, c_ctx, w_mod, b_mod, g_pre_mix, g_post_mix, g_pre_ffn, g_post_ffn, w_in, lam_q1, lam_k1, lam_q2, lam_k2, g_subln, w_proj_attn, w_proj_fourier, w_out, w_router, w_gate, w_up, w_down):
    assert w_mod.shape[0] == 1
    lam_init = 0.8 - 0.6 * math.exp(-0.3 * 0)
    bp, sp, _ = x_prompt.shape
    bs, ss, _ = x_sample.shape

    mod3 = _modulation(c_ctx, c, w_mod[0], b_mod)

    w_in_b = None
    wpa =w_proj_attn[0].astype(BF16)
    wpf = w_proj_fourier[0].astype(BF16)
    wout = w_out[0].astype(BF16)
    wrx = jnp.pad(w_router[0], ((0, 0), (0, LANES - N_EXPERTS))).astype(BF16)
    lam_p = jnp.concatenate([lam_q1, lam_k1, lam_q2, lam_k2], axis=0)

    groups = []
    for x, seq, positional, ctx in ((x_sample, ss, True, (cache_k, cache_v)),
                                    (x_prompt, sp, False, None)):
        nb = x.shape[0]
        t = nb * seq
        x2d = x.reshape(t, D_MODEL)
        if positional:
            mod_row = lambda i, tm, seq=seq: 1 + (i * tm) // seq
        else:
            mod_row = lambda i, tm: 0
        self_contained = ctx is None and not positional
        if self_contained:
            assert PRE_BLOCK % seq == 0
            pre = _ctx_mixer(x2d, mod3, mod_row, g_pre_mix, w_in_b, seq, (lam_p, g_subln, lam_init), tm=PRE_BLOCK)
            o, fm, ga, gf = pre[:4]
        else:
            q, k, v, f, ga, gf, w_in_b = _pre_mixer(x2d, mod3, mod_row, g_pre_mix, w_in[0],
                                                    _rope_tables(seq) if positional else None, seq, tm=ROW_BLOCK)
            o = _attention(lam_p, g_subln, q, k, v, ctx, seq, lam_init)
            fm = _fourier(f, seq)
        x1, h2t, aff_t = _post_mixer(o, fm, ga, gf, x2d, mod3, mod_row, g_post_mix, g_pre_ffn,
                                     wpa, wpf, wout, wrx)
        cap = 2 * t // N_EXPERTS
        assert t <= 1 << TOKEN_BITS
        posm, pack, span, ranges, rows = _route(aff_t, cap)
        idx, qslot = _slot_lists(ranges, posm, pack, cap)
        groups.append(dict(x1=x1, h2t=h2t, idx=idx, qslot=qslot, span=span, ranges=_combine_ranges(rows, t),
                           mod_row=mod_row, cache=pre[4:] if self_contained else None, shape=x.shape))

    gs_, gc = groups
    zc, zs = _moe(gc["idx"], gs_["idx"], gc["qslot"], gs_["qslot"], gc["h2t"], gs_["h2t"], wrx,
                  w_gate[0], w_up[0], w_down[0])
    outs = []
    for g, z in ((gc, zc), (gs_, zs)):
        out = _combine(g["ranges"], z, g["span"], g["x1"], mod3, g["mod_row"], g_post_ffn)
        outs.append(out.reshape(g["shape"]))
    new_k, new_v = gc["cache"]
    return (outs[0], outs[1], new_k, new_v)
```

```python
import functools
import math

import numpy as np
import jax
import jax.numpy as jnp
from jax import lax
from jax.experimental import pallas as pl
from jax.experimental.pallas import tpu as pltpu

F32 = jnp.float32
BF16 = jnp.bfloat16
I32 = jnp.int32

D_MODEL = 1024
N_HEADS = 6
HEAD_DIM = 64
V_DIM = 128
QK_W = 768
FOUR_W = 256
FOUR_G = 64
IN_W = 4608
N_EXPERTS = 16
D_FF = 2816
N_MOD = 6
GRID_W = 64
ROPE_THETA = 10000.0
EPS = 1e-6
SUBLN_EPS = 1e-5

LANES = 128
ROW_BLOCK = 256
PRE_BLOCK = 512
GATE_CHUNK = 256
POST_BLOCK = 1024
POST_SUB = 256
WEIGHT_ROWS = 128
WEIGHT_STAGES = 3
TOK_BLOCK = 256
ROW_CHUNK = 256
FF_TILE = 256
FF_STEPS = D_FF // FF_TILE
ROW_TILE = D_MODEL // LANES
TOKEN_BITS = 13
VMEM_LIMIT = 56 * 1024 * 1024


def _cparams(sem):
    return pltpu.CompilerParams(dimension_semantics=sem, vmem_limit_bytes=VMEM_LIMIT)


def _dot(a, b):
    return jnp.dot(a, b, preferred_element_type=F32)


def _rms(x, g, eps):
    return x * lax.rsqrt(jnp.mean(x * x, axis=-1, keepdims=True) + eps) * g


def _whole_vmem():
    return pl.BlockSpec(memory_space=pltpu.MemorySpace.VMEM)


def _cast_weight(w_hbm, wb_ref, stage_ref, sem):
    depth, rows = stage_ref.shape[0], stage_ref.shape[1]
    chunks = [pltpu.make_async_copy(w_hbm.at[pl.ds(r0, rows), :], stage_ref.at[n % depth], sem.at[n % depth])
              for n, r0 in enumerate(range(0, w_hbm.shape[0], rows))]
    for chunk in chunks[:depth - 1]:
        chunk.start()
    for n, chunk in enumerate(chunks):
        if n + depth - 1 < len(chunks):
            chunks[n + depth - 1].start()
        chunk.wait()
        wb_ref[n * rows:(n + 1) * rows, :] = stage_ref[n % depth].astype(BF16)


MOD_ROWS = 8


def _mod_kernel(cc_ref, c_ref, wa_ref, wb_ref, b_ref, o_ref):
    pad = jnp.zeros((MOD_ROWS - 1 - c_ref.shape[0], D_MODEL), F32)
    c = jnp.concatenate([cc_ref[...], c_ref[...], pad], axis=0)
    s = (c * jax.nn.sigmoid(c)).astype(BF16)
    res = jnp.concatenate([_dot(s, wa_ref[...].astype(BF16)), _dot(s, wb_ref[...].astype(BF16))], axis=1) + b_ref[...]
    for r in range(MOD_ROWS):
        o_ref[r] = res[r:r + 1, :]


def _modulation(c_ctx, c, w_mod, b_mod):
    tn = 1024
    n = N_MOD * D_MODEL
    assert 1 + c.shape[0] <= MOD_ROWS
    return pl.pallas_call(
        _mod_kernel,
        grid=(n // tn,),
        in_specs=[pl.BlockSpec((1, D_MODEL), lambda j: (0, 0)),
                  pl.BlockSpec(c.shape, lambda j: (0, 0)),
                  pl.BlockSpec((D_MODEL, tn // 2), lambda j: (0, 2 * j)),
                  pl.BlockSpec((D_MODEL, tn // 2), lambda j: (0, 2 * j + 1)),
                  pl.BlockSpec((1, tn), lambda j: (0, j))],
        out_specs=pl.BlockSpec((MOD_ROWS, 1, tn), lambda j: (0, 0, j)),
        out_shape=jax.ShapeDtypeStruct((MOD_ROWS, 1, n), F32),
        compiler_params=_cparams(("arbitrary",)),
        name="modulation",
    )(c_ctx[None, :], c, w_mod, w_mod, b_mod)


def _diff_lambda(lp, lam_init):
    s1 = jnp.sum(lp[0:1] * lp[1:2], axis=-1, keepdims=True)
    s2 = jnp.sum(lp[2:3] * lp[3:4], axis=-1, keepdims=True)
    return jnp.exp(s1) - jnp.exp(s2) + lam_init


def _attention_weights(q, k):
    comp1 = lax.broadcasted_iota(I32, (1, V_DIM), 1) < HEAD_DIM
    qs = q * jnp.asarray(HEAD_DIM ** -0.5, BF16)
    zero = jnp.zeros_like(qs)

    def weights(qc):
        s = lax.dot_general(qc, k, (((1,), (1,)), ((), ())), preferred_element_type=F32)
        return jnp.exp(s - jnp.max(s, axis=-1, keepdims=True)).astype(BF16)

    return weights(jnp.where(comp1, qs, zero)), weights(jnp.where(comp1, zero, qs))


def _attention_output(weights, v, lam, g_subln, lam_init):
    v_ones = jnp.concatenate([v, jnp.ones_like(v)], axis=1)

    def attend(ex):
        ov = _dot(ex, v_ones)
        return ov[:, 0:V_DIM] / ov[:, V_DIM:2 * V_DIM]

    o = attend(weights[0]) - lam * attend(weights[1])
    return (_rms(o, g_subln, SUBLN_EPS) * (1.0 - lam_init)).astype(BF16)


def _rope(z, cos, sin_signed, first_half):
    fwd = pltpu.roll(z, QK_W - 16, axis=1)
    bwd = pltpu.roll(z, 16, axis=1)
    return z * cos + jnp.where(first_half, fwd, bwd) * sin_signed


def _pre_kernel(*refs, positional):
    it = iter(refs)
    x_ref, mod_ref, g_ref, w_hbm = next(it), next(it), next(it), next(it)
    if positional:
        cos_ref, sin_ref = next(it), next(it)
    q_ref, k_ref, v_ref, f_ref, ga_ref, gf_ref = (next(it) for _ in range(6))
    wb_hbm, w_ref, stage_ref, sem = next(it), next(it), next(it), next(it)
    weight_out = pltpu.make_async_copy(w_ref, wb_hbm, sem.at[WEIGHT_STAGES])

    @pl.when(pl.program_id(0) == 0)
    def _():
        _cast_weight(w_hbm, w_ref, stage_ref, sem)
        weight_out.start()

    @pl.when(pl.program_id(0) == pl.num_programs(0) - 1)
    def _():
        weight_out.wait()

    m = mod_ref[0]
    sh1 = m[:, 0:D_MODEL]
    sc1 = m[:, D_MODEL:2 * D_MODEL]
    h = _rms(x_ref[...], g_ref[...], EPS) * (1.0 + sc1) + sh1
    hb = h.astype(BF16)

    def proj(lo, hi):
        return _dot(hb, w_ref[:, lo:hi])

    zq = proj(0, QK_W)
    zk = proj(QK_W, 2 * QK_W)
    zv = proj(2 * QK_W, 3 * QK_W)
    if positional:
        lane = lax.broadcasted_iota(I32, (1, QK_W), 1)
        first_half = (lane % 32) < 16
        cos = cos_ref[...]
        sin_signed = sin_ref[...]
        zq = _rope(zq, cos, sin_signed, first_half)
        zk = _rope(zk, cos, sin_signed, first_half)
    q_ref[...] = zq.astype(BF16)
    k_ref[...] = zk.astype(BF16)
    v_ref[...] = zv.astype(BF16)
    f0 = 3 * QK_W
    g0 = f0 + FOUR_W
    f_ref[...] = proj(f0, g0)
    ga_ref[...] = jax.nn.sigmoid(proj(g0, g0 + D_MODEL)).astype(BF16)
    gf_ref[...] = jax.nn.sigmoid(proj(g0 + D_MODEL, IN_W)).astype(BF16)


def _ctx_kernel(x_ref, mod_ref, g_ref, w_ref, bc_ref, bs_ref, cs_ref, ss_ref, lam_ref, gs_ref,
                o_ref, f_ref, ga_ref, gf_ref, kc_ref, vc_ref, *, seq, lam_init):
    m = mod_ref[0]
    sh1 = m[:, 0:D_MODEL]
    sc1 = m[:, D_MODEL:2 * D_MODEL]
    hb = (_rms(x_ref[...], g_ref[...], EPS) * (1.0 + sc1) + sh1).astype(BF16)
    seqs = [slice(b * seq, (b + 1) * seq) for b in range(x_ref.shape[0] // seq)]
    lam = _diff_lambda(lam_ref[...], lam_init)
    f0 = 3 * QK_W
    g0 = f0 + FOUR_W

    def proj(lo, hi):
        return _dot(hb, w_ref[:, lo:hi])

    q, k, v = {}, {}, {}

    def qkv_chunk(store, cache_ref, lo, c):
        cols = slice(c * GATE_CHUNK, (c + 1) * GATE_CHUNK)
        z = proj(lo + cols.start, lo + cols.stop)
        for hd in range(cols.start // V_DIM, cols.stop // V_DIM):
            z_hd = z[:, hd * V_DIM - cols.start:(hd + 1) * V_DIM - cols.start]
            if cache_ref is not None:
                for b, rs in enumerate(seqs):
                    cache_ref[b, 0, hd] = z_hd[rs, :]
            store[hd] = z_hd.astype(BF16)

    def fourier_chunk():
        f = proj(f0, g0)
        for rs in seqs:
            f_ref[rs, :] = _dft_real(f[rs, :], bc_ref, bs_ref, cs_ref, ss_ref)

    def gate_chunk(ref, lo, c):
        cols = slice(c * GATE_CHUNK, (c + 1) * GATE_CHUNK)
        ref[:, cols] = jax.nn.sigmoid(proj(lo + cols.start, lo + cols.stop)).astype(BF16)

    for store, cache, lo in ((q, None, 0), (k, kc_ref, QK_W), (v, vc_ref, 2 * QK_W)):
        for c in range(QK_W // GATE_CHUNK):
            qkv_chunk(store, cache, lo, c)
    matmul_work = [fourier_chunk] + [functools.partial(gate_chunk, ref, lo, c)
                                     for ref, lo in ((ga_ref, g0), (gf_ref, g0 + D_MODEL))
                                     for c in range(D_MODEL // GATE_CHUNK)]
    for rs in seqs:
        for hd in range(N_HEADS):
            weights = _attention_weights(q[hd][rs, :], k[hd][rs, :])
            if matmul_work:
                matmul_work.pop(0)()
            o_ref[rs, hd * V_DIM:(hd + 1) * V_DIM] = _attention_output(weights, v[hd][rs, :], lam, gs_ref[...],
                                                                      lam_init)
    for work in matmul_work:
        work()


def _ctx_mixer(x2d, mod3, mod_row, g_pre, w_in_b, seq, attn_params, tm):
    t = x2d.shape[0]
    lam_p, g_subln, lam_init = attn_params
    assert tm % seq == 0 and t % tm == 0
    row = lambda i: (i, 0)
    const = lambda i: (0, 0)
    consts = _dft_consts(seq) + (lam_p, g_subln)
    nb = t // seq
    cshape = jax.ShapeDtypeStruct((nb, 1, N_HEADS, seq, V_DIM), F32)
    cspec = pl.BlockSpec((tm // seq, 1, N_HEADS, seq, V_DIM), lambda i: (i, 0, 0, 0, 0))
    return pl.pallas_call(
        functools.partial(_ctx_kernel, seq=seq, lam_init=lam_init),
        grid=(t // tm,),
        in_specs=[pl.BlockSpec((tm, D_MODEL), row),
                  pl.BlockSpec((1, 1, N_MOD * D_MODEL), lambda i: (mod_row(i, tm), 0, 0)),
                  pl.BlockSpec((1, D_MODEL), const),
                  _whole_vmem()] + [pl.BlockSpec(c.shape, const) for c in consts],
        out_specs=[pl.BlockSpec((tm, QK_W), row),
                   pl.BlockSpec((tm, FOUR_W), row),
                   pl.BlockSpec((tm, D_MODEL), row),
                   pl.BlockSpec((tm, D_MODEL), row),
                   cspec, cspec],
        out_shape=[jax.ShapeDtypeStruct((t, QK_W), BF16),
                   jax.ShapeDtypeStruct((t, FOUR_W), BF16),
                   jax.ShapeDtypeStruct((t, D_MODEL), BF16),
                   jax.ShapeDtypeStruct((t, D_MODEL), BF16),
                   cshape, cshape],
        compiler_params=_cparams(("arbitrary",)),
        name="ctx_mixer",
    )(x2d, mod3, g_pre, w_in_b, *consts)


def _pre_mixer(x2d, mod3, mod_row, g_pre, w_in, rope_tabs, seq, tm):
    t = x2d.shape[0]
    positional = rope_tabs is not None
    assert seq % tm == 0
    blocks_per_seq = seq // tm
    row = lambda i: (i, 0)
    any_spec = pl.BlockSpec(memory_space=pl.ANY)
    in_specs = [pl.BlockSpec((tm, D_MODEL), row),
                pl.BlockSpec((1, 1, N_MOD * D_MODEL), lambda i: (mod_row(i, tm), 0, 0)),
                pl.BlockSpec((1, D_MODEL), lambda i: (0, 0)),
                any_spec]
    args = [x2d, mod3, g_pre, w_in]
    if positional:
        in_specs += [pl.BlockSpec((tm, QK_W), lambda i: (i % blocks_per_seq, 0))] * 2
        args += list(rope_tabs)
    out_shape = [jax.ShapeDtypeStruct((t, QK_W), BF16)] * 3 + [
        jax.ShapeDtypeStruct((t, FOUR_W), F32),
        jax.ShapeDtypeStruct((t, D_MODEL), BF16),
        jax.ShapeDtypeStruct((t, D_MODEL), BF16),
        jax.ShapeDtypeStruct(w_in.shape, BF16)]
    out_specs = [pl.BlockSpec((tm, QK_W), row)] * 3 + [
        pl.BlockSpec((tm, FOUR_W), row),
        pl.BlockSpec((tm, D_MODEL), row),
        pl.BlockSpec((tm, D_MODEL), row),
        any_spec]
    return pl.pallas_call(
        functools.partial(_pre_kernel, positional=positional),
        grid=(t // tm,),
        in_specs=in_specs,
        out_specs=out_specs,
        out_shape=out_shape,
        scratch_shapes=[pltpu.VMEM(w_in.shape, BF16),
                        pltpu.VMEM((WEIGHT_STAGES, WEIGHT_ROWS, w_in.shape[1]), F32),
                        pltpu.SemaphoreType.DMA((WEIGHT_STAGES + 1,))],
        compiler_params=_cparams(("arbitrary",)),
        name="pre_mixer",
    )(*args)


def _attn_kernel(*refs, lam_init, has_ctx):
    it = iter(refs)
    lam_ref, gs_ref, q_ref, k_ref, v_ref = (next(it) for _ in range(5))
    if has_ctx:
        ck_ref, cv_ref = next(it), next(it)
    o_ref = next(it)

    lam = _diff_lambda(lam_ref[...], lam_init)

    def operand(ref, cache_ref, hd):
        x = ref[:, hd * V_DIM:(hd + 1) * V_DIM]
        return jnp.concatenate([cache_ref[0, 0, hd].astype(BF16), x], axis=0) if has_ctx else x

    def head_weights(hd):
        return _attention_weights(q_ref[:, hd * V_DIM:(hd + 1) * V_DIM], operand(k_ref, ck_ref if has_ctx else None, hd))

    weights = head_weights(0)
    for hd in range(N_HEADS):
        nxt = head_weights(hd + 1) if hd + 1 < N_HEADS else None
        v = operand(v_ref, cv_ref if has_ctx else None, hd)
        o_ref[:, hd * V_DIM:(hd + 1) * V_DIM] = _attention_output(weights, v, lam, gs_ref[...], lam_init)
        weights = nxt


def _attention(lam_p, g_subln, q, k, v, ctx, seq, lam_init):
    t = q.shape[0]
    tq = ROW_BLOCK
    qb = seq // tq
    has_ctx = ctx is not None
    in_specs = [pl.BlockSpec((4, HEAD_DIM), lambda b, i: (0, 0)),
                pl.BlockSpec((1, V_DIM), lambda b, i: (0, 0)),
                pl.BlockSpec((tq, QK_W), lambda b, i: (b * qb + i, 0)),
                pl.BlockSpec((seq, QK_W), lambda b, i: (b, 0)),
                pl.BlockSpec((seq, QK_W), lambda b, i: (b, 0))]
    args = [lam_p, g_subln, q, k, v]
    if has_ctx:
        past = ctx[0].shape[3]
        cspec = pl.BlockSpec((1, 1, N_HEADS, past, V_DIM), lambda b, i: (b, 0, 0, 0, 0))
        in_specs += [cspec, cspec]
        args += list(ctx)
    return pl.pallas_call(
        functools.partial(_attn_kernel, lam_init=lam_init, has_ctx=has_ctx),
        grid=(t // seq, qb),
        in_specs=in_specs,
        out_specs=pl.BlockSpec((tq, QK_W), lambda b, i: (b * qb + i, 0)),
        out_shape=jax.ShapeDtypeStruct((t, QK_W), BF16),
        compiler_params=_cparams(("arbitrary", "arbitrary")),
        name="diff_attention",
    )(*args)


def _dft_real(f, bc_ref, bs_ref, cs_ref, ss_ref):
    fb = f.astype(BF16)
    u = _dot(fb, bc_ref[...].astype(BF16)).astype(BF16)
    w = _dot(fb, bs_ref[...].astype(BF16)).astype(BF16)
    return (_dot(cs_ref[...].astype(BF16), u) - _dot(ss_ref[...].astype(BF16), w)).astype(BF16)


def _fourier_kernel(f_ref, bc_ref, bs_ref, cs_ref, ss_ref, o_ref):
    o_ref[...] = _dft_real(f_ref[...], bc_ref, bs_ref, cs_ref, ss_ref)


def _dft_consts(seq):
    c = np.arange(FOUR_G)
    ang_c = 2.0 * np.pi * ((c[:, None] * c[None, :]) % FOUR_G) / FOUR_G
    eye = np.eye(FOUR_W // FOUR_G)
    bc = np.kron(eye, np.cos(ang_c)) / math.sqrt(FOUR_G)
    bs = np.kron(eye, np.sin(ang_c)) / math.sqrt(FOUR_G)
    s = np.arange(seq)
    ang_s = 2.0 * np.pi * ((s[:, None] * s[None, :]) % seq) / seq
    cs = np.cos(ang_s) / math.sqrt(seq)
    ss = np.sin(ang_s) / math.sqrt(seq)
    return tuple(jnp.asarray(a, dtype=F32) for a in (bc, bs, cs, ss))


def _fourier(f, seq):
    t = f.shape[0]
    bc, bs, cs, ss = _dft_consts(seq)
    const = lambda b: (0, 0)
    return pl.pallas_call(
        _fourier_kernel,
        grid=(t // seq,),
        in_specs=[pl.BlockSpec((seq, FOUR_W), lambda b: (b, 0)),
                  pl.BlockSpec((FOUR_W, FOUR_W), const),
                  pl.BlockSpec((FOUR_W, FOUR_W), const),
                  pl.BlockSpec((seq, seq), const),
                  pl.BlockSpec((seq, seq), const)],
        out_specs=pl.BlockSpec((seq, FOUR_W), lambda b: (b, 0)),
        out_shape=jax.ShapeDtypeStruct((t, FOUR_W), BF16),
        compiler_params=_cparams(("arbitrary",)),
        name="fourier_mix",
    )(f, bc, bs, cs, ss)


def _post_kernel(o_ref, fm_ref, ga_ref, gf_ref, x_ref, mod_ref, gpost_ref, gffn_ref,
                 wpa_ref, wpf_ref, wout_ref, wrx_ref,
                 x1_ref, h2t_ref, afft_ref):
    m = mod_ref[0]
    gt1 = m[:, 2 * D_MODEL:3 * D_MODEL]
    sh2 = m[:, 3 * D_MODEL:4 * D_MODEL]
    sc2 = m[:, 4 * D_MODEL:5 * D_MODEL]
    subs = [slice(r0, r0 + POST_SUB) for r0 in range(0, o_ref.shape[0], POST_SUB)]
    ab = [(_dot(o_ref[rs, :], wpa_ref[...]), _dot(fm_ref[rs, :], wpf_ref[...])) for rs in subs]
    merged = [(ga_ref[rs, :] * a + gf_ref[rs, :] * b).astype(BF16) for rs, (a, b) in zip(subs, ab)]
    ys = [_dot(mg, wout_ref[...]) for mg in merged]
    h2s = []
    for rs, y in zip(subs, ys):
        x1 = x_ref[rs, :] + gt1 * _rms(y, gpost_ref[...], EPS)
        x1_ref[rs, :] = x1
        h2s.append(_rms(x1, gffn_ref[...], EPS) * (1.0 + sc2) + sh2)
    logits = [_dot(h2.astype(BF16), wrx_ref[...]) for h2 in h2s]
    for rs, h2, lg in zip(subs, h2s, logits):
        lt = lg.T[0:N_EXPERTS]
        et = jnp.exp(lt - jnp.max(lt, axis=0, keepdims=True))
        aff = et / jnp.sum(et, axis=0, keepdims=True)
        for u in range(POST_SUB // LANES):
            afft_ref[:, rs.start // LANES + u, :] = aff[:, u * LANES:(u + 1) * LANES]
        for kc in range(ROW_TILE):
            h2t_ref[pl.ds(rs.start * ROW_TILE + kc, POST_SUB, stride=ROW_TILE), :] = h2[:, kc * LANES:(kc + 1) * LANES]


def _post_mixer(o, fm, ga, gf, x2d, mod3, mod_row, g_post, g_ffn, wpa, wpf, wout, wrx):
    t = x2d.shape[0]
    tm = POST_BLOCK
    row = lambda i: (i, 0)
    const = lambda i: (0, 0)
    return pl.pallas_call(
        _post_kernel,
        grid=(t // tm,),
        in_specs=[pl.BlockSpec((tm, QK_W), row),
                  pl.BlockSpec((tm, FOUR_W), row),
                  pl.BlockSpec((tm, D_MODEL), row),
                  pl.BlockSpec((tm, D_MODEL), row),
                  pl.BlockSpec((tm, D_MODEL), row),
                  pl.BlockSpec((1, 1, N_MOD * D_MODEL), lambda i: (mod_row(i, tm), 0, 0)),
                  pl.BlockSpec((1, D_MODEL), const),
                  pl.BlockSpec((1, D_MODEL), const),
                  pl.BlockSpec((QK_W, D_MODEL), const),
                  pl.BlockSpec((FOUR_W, D_MODEL), const),
                  pl.BlockSpec((D_MODEL, D_MODEL), const),
                  pl.BlockSpec((D_MODEL, LANES), const)],
        out_specs=[pl.BlockSpec((tm, D_MODEL), row),
                   pl.BlockSpec((tm * ROW_TILE, LANES), row),
                   pl.BlockSpec((N_EXPERTS, tm // LANES, LANES), lambda i: (0, i, 0))],
        out_shape=[jax.ShapeDtypeStruct((t, D_MODEL), F32),
                   jax.ShapeDtypeStruct((t * ROW_TILE, LANES), F32),
                   jax.ShapeDtypeStruct((N_EXPERTS, t // LANES, LANES), F32)],
        compiler_params=_cparams(("arbitrary",)),
        name="post_mixer",
    )(o, fm, ga, gf, x2d, mod3, g_post, g_ffn, wpa, wpf, wout, wrx)


RANGE_ROWS = 8


def _route_kernel(aff_ref, posm_ref, pack_ref, span_ref, ranges_ref, rows_ref, *, cap, n_tok):
    aff = aff_ref[...]
    nt = n_tok // LANES
    capf = float(cap)

    def count_ge(v):
        return jnp.sum(jnp.where(aff >= v, 1.0, 0.0), axis=(1, 2), keepdims=True)

    def search(i, thr):
        cand = thr | jnp.left_shift(jnp.int32(1), 30 - i)
        return jnp.where(count_ge(pltpu.bitcast(cand, F32)) >= capf, cand, thr)

    thr = lax.fori_loop(0, 31, search, jnp.zeros((N_EXPERTS, 1, 1), I32))
    lo = pltpu.bitcast(thr, F32)
    hi = pltpu.bitcast(thr + 1, F32)

    def refine(i, c):
        lo, hi = c
        mid = lo + (hi - lo) * 0.5
        ok = count_ge(mid) >= capf
        return jnp.where(ok, mid, lo), jnp.where(ok, hi, mid)

    lo, hi = lax.fori_loop(0, 12, refine, (lo, hi))
    gt = aff >= hi
    eq = (aff >= lo) & (aff < hi)
    n_tie = capf - jnp.sum(jnp.where(gt, 1.0, 0.0), axis=(1, 2), keepdims=True)

    sq0 = lax.broadcasted_iota(I32, (LANES, LANES), 0)
    sq1 = lax.broadcasted_iota(I32, (LANES, LANES), 1)
    along_total = jnp.concatenate([jnp.where(sq0 <= sq1, 1.0, 0.0), jnp.ones((LANES, LANES), F32)],
                                  axis=1).astype(BF16)
    m = N_EXPERTS * nt
    r0 = lax.broadcasted_iota(I32, (m, m), 0)
    r1 = lax.broadcasted_iota(I32, (m, m), 1)
    earlier = jnp.where((r0 // nt == r1 // nt) & (r1 < r0), 1.0, 0.0).astype(BF16)
    lane = lax.broadcasted_iota(I32, (1, LANES), 1)
    token = lax.broadcasted_iota(I32, (nt, LANES), 0) * LANES + lane

    def tile_counts(x):
        both = _dot(x.reshape(m, LANES).astype(BF16), along_total)
        total = both[:, LANES:]
        before = _dot(earlier, total.astype(BF16))
        shape = (N_EXPERTS, nt, LANES)
        return both[:, :LANES].reshape(shape), total.reshape(shape), before.reshape(shape)

    eq_f = jnp.where(eq, 1.0, 0.0)
    eq_along, _, eq_before = tile_counts(eq_f)
    sel = jnp.where(gt, 1.0, jnp.where(eq_along + eq_before <= n_tie, eq_f, 0.0))
    sel_along, sel_total, sel_before = tile_counts(sel)
    posm_ref[...] = jnp.where(sel > 0.5, sel_along + sel_before - sel, -1.0).astype(I32)

    cnt = jnp.sum(sel, axis=0)
    rows_before = jnp.sum(sel_before, axis=0)
    tok_start = _dot(cnt.astype(BF16), along_total[:, :LANES]) - cnt + rows_before
    k = jnp.zeros((nt, LANES), F32)
    slots_before = jnp.zeros((nt, LANES), F32)
    slots_here = jnp.zeros((nt, LANES), F32)
    for e in range(N_EXPERTS):
        pack_ref[e] = (tok_start + k).astype(I32) * (1 << TOKEN_BITS) + token
        k = k + sel[e]
        slots_before = jnp.where(lane == e, sel_before[e], slots_before)
        slots_here = jnp.where(lane == e, sel_total[e], slots_here)

    eye = jnp.where(sq0 == sq1, 1.0, 0.0).astype(BF16)
    tok_end = tok_start + cnt
    parts = []
    for v in (tok_start, tok_end):
        high = jnp.floor(v * (1.0 / LANES))
        parts += [high, v - high * LANES]
    for c in range(nt):
        rows4 = jnp.zeros((LANES, LANES), F32)
        for r, part in enumerate(parts):
            rows4 = jnp.where(sq0 == r, jnp.broadcast_to(part[c:c + 1, :], (LANES, LANES)), rows4)
        span_ref[c * LANES:(c + 1) * LANES, :] = lax.dot_general(
            eye, rows4.astype(BF16), (((1,), (1,)), ((), ())), preferred_element_type=F32)

    ranges_ref[...] = jnp.zeros_like(ranges_ref)
    for s in range(cap // LANES):
        done = jnp.where(slots_before + slots_here <= float(s * LANES), 1.0, 0.0)
        begun = jnp.where(slots_before < float((s + 1) * LANES), 1.0, 0.0)
        ranges_ref[s:s + 1, :] = jnp.sum(done, axis=0, keepdims=True).astype(I32)
        ranges_ref[RANGE_ROWS + s:RANGE_ROWS + s + 1, :] = jnp.sum(begun, axis=0, keepdims=True).astype(I32)
    rows_ref[...] = rows_before.astype(I32)


def _route(aff, cap):
    n_e, nt, _ = aff.shape
    n_tok = nt * LANES
    assert cap // LANES <= RANGE_ROWS
    return pl.pallas_call(
        functools.partial(_route_kernel, cap=cap, n_tok=n_tok),
        out_shape=[jax.ShapeDtypeStruct((n_e, nt, LANES), I32),
                   jax.ShapeDtypeStruct((n_e, nt, LANES), I32),
                   jax.ShapeDtypeStruct((n_tok, LANES), F32),
                   jax.ShapeDtypeStruct((2 * RANGE_ROWS, LANES), I32),
                   jax.ShapeDtypeStruct((nt, LANES), I32)],
        compiler_params=pltpu.CompilerParams(vmem_limit_bytes=VMEM_LIMIT),
        name="route",
    )(aff)


def _slots_kernel(ranges_ref, posm_ref, pack_ref, idx_ref, qslot_ref, acc_ref):
    e = pl.program_id(0)
    sub = lax.broadcasted_iota(I32, (LANES, LANES), 0)
    eye = sub == lax.broadcasted_iota(I32, (LANES, LANES), 1)
    n_tiles = idx_ref.shape[1]

    for s in range(n_tiles):
        slot = sub + s * LANES

        def body(c, acc, slot=slot):
            hit = posm_ref[e, pl.ds(c, 1), :] == slot
            return acc + jnp.where(hit, pack_ref[e, pl.ds(c, 1), :], 0)

        acc_ref[s] = lax.fori_loop(ranges_ref[s, e], ranges_ref[RANGE_ROWS + s, e], body,
                                   jnp.zeros((LANES, LANES), I32))

    def as_row(part):
        col = jnp.sum(part.astype(F32), axis=1, keepdims=True)
        return jnp.sum(jnp.where(eye, col, 0.0), axis=0, keepdims=True).astype(I32)

    for s in range(n_tiles):
        acc = acc_ref[s]
        idx_ref[0, s:s + 1, :] = as_row(acc & ((1 << TOKEN_BITS) - 1)) * ROW_TILE
        qslot_ref[0, s:s + 1, :] = as_row(lax.shift_right_logical(acc, TOKEN_BITS)) * ROW_TILE


def _slot_lists(ranges, posm4, qdst4, cap):
    ns = cap // LANES
    grid_spec = pltpu.PrefetchScalarGridSpec(
        num_scalar_prefetch=1,
        grid=(N_EXPERTS,),
        in_specs=[_whole_vmem(), _whole_vmem()],
        out_specs=[pl.BlockSpec((1, ns, LANES), lambda e, *_: (e, 0, 0)),
                   pl.BlockSpec((1, ns, LANES), lambda e, *_: (e, 0, 0))],
        scratch_shapes=[pltpu.VMEM((ns, LANES, LANES), I32)])
    idx, qslot = pl.pallas_call(
        _slots_kernel,
        grid_spec=grid_spec,
        out_shape=[jax.ShapeDtypeStruct((N_EXPERTS, ns, LANES), I32),
                   jax.ShapeDtypeStruct((N_EXPERTS, ns, LANES), I32)],
        compiler_params=_cparams(("arbitrary",)),
        name="slot_lists",
    )(ranges, posm4, qdst4)
    return idx.reshape(N_EXPERTS, cap), qslot.reshape(N_EXPERTS, cap)


def _moe_kernel(idxc_ref, idxs_ref, qc_ref, qs_ref,
                hc_ref, hs_ref, wr_ref, wg_ref, wu_ref, wd_ref, zc_ref, zs_ref,
                xbuf, ybuf, xb_ref, gate_ref, acc_ref, gsem, ssem, *, capc, caps):
    e = pl.program_id(0)
    j = pl.program_id(1)
    n_e = pl.num_programs(0)
    n_j = FF_STEPS
    slot = e % 2
    other = 1 - slot
    rows = capc + caps
    gc, gs = _per_step(capc), _per_step(caps)
    groups = ((hc_ref, idxc_ref, zc_ref, qc_ref, gc, 0), (hs_ref, idxs_ref, zs_ref, qs_ref, gs, gc * n_j))

    def tile(ref, first_sublane):
        return ref.at[pl.ds(pl.multiple_of(first_sublane, ROW_TILE), ROW_TILE), :]

    def gather(ex, sl, step, i, group):
        h_ref, idx_ref, _, _, per_step, base = group
        p = step * per_step + i
        src = tile(h_ref, idx_ref[ex * (per_step * n_j) + p])
        pltpu.make_async_copy(src, xbuf.at[sl, :, base + p, :], gsem.at[sl]).start()

    def scatter(table_row, sl, step, i, group):
        _, _, z_ref, q_ref, per_step, base = group
        p = step * per_step + i
        dst = tile(z_ref, q_ref[table_row * (per_step * n_j) + p])
        pltpu.make_async_copy(ybuf.at[sl, :, base + p, :], dst, ssem.at[sl]).start()

    def all_steps(fn):
        for group in groups:
            def body(p, carry, group=group):
                fn(p, group)
                return carry
            lax.fori_loop(0, group[4] * n_j, body, 0, unroll=8)

    def wait_all(buf, sem, sl):
        pltpu.make_async_copy(buf.at[sl], buf.at[sl], sem.at[sl]).wait()

    @pl.when((e == 0) & (j == 0))
    def _():
        ybuf[...] = jnp.zeros_like(ybuf)
        all_steps(lambda p, group: gather(0, 0, 0, p, group))

    @pl.when(j == 0)
    def _():
        wait_all(xbuf, gsem, slot)
        for base, n, dst in ((0, capc, 0), (gc * n_j, caps, capc)):
            for kc in range(ROW_TILE):
                xb_ref[dst:dst + n, kc * LANES:(kc + 1) * LANES] = xbuf[slot, kc, base:base + n, :].astype(BF16)
        acc_ref[...] = jnp.zeros_like(acc_ref)
        logits = _dot(xb_ref[...], wr_ref[...])
        lane = lax.broadcasted_iota(I32, (1, LANES), 1)
        is_expert = lane < N_EXPERTS
        ex = jnp.exp(logits - jnp.max(jnp.where(is_expert, logits, -jnp.inf), axis=-1, keepdims=True))
        mine = jnp.sum(jnp.where(lane == e, ex, 0.0), axis=-1, keepdims=True)
        gate = mine / jnp.sum(jnp.where(is_expert, ex, 0.0), axis=-1, keepdims=True)
        gate_ref[...] = jnp.broadcast_to(gate, gate_ref.shape)

    nxt = jnp.minimum(e + 1, n_e - 1)
    for group in groups:
        for i in range(group[4]):
            gather(nxt, other, j, i, group)
            scatter(e, other, j, i, group)

    x = xb_ref[...]
    g = _dot(x, wg_ref[0].astype(BF16))
    u = _dot(x, wu_ref[0].astype(BF16))
    hid = (g * jax.nn.sigmoid(g) * u).astype(BF16)
    acc_ref[...] += _dot(hid, wd_ref[0].astype(BF16))

    @pl.when(j == n_j - 1)
    def _():
        @pl.when(e >= 1)
        def _():
            wait_all(ybuf, ssem, slot)

        for base, n, src in ((0, capc, 0), (gc * n_j, caps, capc)):
            for r0 in range(0, n, LANES):
                nr = min(LANES, n - r0)
                gate = gate_ref[src + r0:src + r0 + nr, :]
                for kc in range(ROW_TILE):
                    y = acc_ref[src + r0:src + r0 + nr, kc * LANES:(kc + 1) * LANES] * gate
                    ybuf[slot, kc, base + r0:base + r0 + nr, :] = y

        @pl.when(e == n_e - 1)
        def _():
            all_steps(lambda p, group: scatter(e + 1, slot, 0, p, group))
            wait_all(ybuf, ssem, other)
            wait_all(ybuf, ssem, slot)
            wait_all(xbuf, gsem, other)


def _per_step(cap):
    return -(-cap // FF_STEPS)


def _copy_tables(idx, qslot, n_rows):
    n_e, cap = idx.shape
    padded = _per_step(cap) * FF_STEPS
    n_pad = padded - cap
    idx_p = jnp.concatenate([idx, jnp.zeros((n_e, n_pad), I32)], axis=1)
    spare = n_rows + jnp.arange(padded + n_e * n_pad, dtype=I32)
    lead = spare[:padded][None, :]
    pad_rows = spare[padded:].reshape(n_e, n_pad)
    q_p = jnp.concatenate([lead, jnp.concatenate([qslot // ROW_TILE, pad_rows], axis=1)], axis=0) * ROW_TILE
    return idx_p.reshape(-1), q_p.reshape(-1), n_rows + padded + n_e * n_pad


def _moe(idxc, idxs, qc, qs, hc, hs, wrx, w_gate, w_up, w_down):
    capc, caps = idxc.shape[1], idxs.shape[1]
    rows = capc + caps
    tf = FF_TILE
    idxc, qc, zc_rows = _copy_tables(idxc, qc, N_EXPERTS * capc)
    idxs, qs, zs_rows = _copy_tables(idxs, qs, N_EXPERTS * caps)
    buf_rows = (_per_step(capc) + _per_step(caps)) * FF_STEPS
    any_spec = pl.BlockSpec(memory_space=pl.ANY)
    grid_spec = pltpu.PrefetchScalarGridSpec(
        num_scalar_prefetch=4,
        grid=(N_EXPERTS, FF_STEPS),
        in_specs=[any_spec, any_spec,
                  pl.BlockSpec((D_MODEL, LANES), lambda e, j, *_: (0, 0)),
                  pl.BlockSpec((1, D_MODEL, tf), lambda e, j, *_: (e, 0, j)),
                  pl.BlockSpec((1, D_MODEL, tf), lambda e, j, *_: (e, 0, j)),
                  pl.BlockSpec((1, tf, D_MODEL), lambda e, j, *_: (e, j, 0))],
        out_specs=[any_spec, any_spec],
        scratch_shapes=[pltpu.VMEM((2, ROW_TILE, buf_rows, LANES), F32),
                        pltpu.VMEM((2, ROW_TILE, buf_rows, LANES), F32),
                        pltpu.VMEM((rows, D_MODEL), BF16),
                        pltpu.VMEM((rows, LANES), F32),
                        pltpu.VMEM((rows, D_MODEL), F32),
                        pltpu.SemaphoreType.DMA((2,)),
                        pltpu.SemaphoreType.DMA((2,))])
    return pl.pallas_call(
        functools.partial(_moe_kernel, capc=capc, caps=caps),
        grid_spec=grid_spec,
        out_shape=[jax.ShapeDtypeStruct((zc_rows * ROW_TILE, LANES), F32),
                   jax.ShapeDtypeStruct((zs_rows * ROW_TILE, LANES), F32)],
        compiler_params=_cparams(("arbitrary", "arbitrary")),
        name="expert_ffn",
    )(idxc, idxs, qc, qs, hc, hs, wrx, w_gate, w_up, w_down)


Z_BUFFERS = 3


def _combine_kernel(clo_ref, chi_ref, z_ref, span_ref, x1_ref, mod_ref, g_ref, o_ref,
                    zbuf, acc_ref, sems, state, *, n_chunks):
    b = pl.program_id(0)
    chunk_rows = ROW_CHUNK * ROW_TILE

    @pl.when(b == 0)
    def _():
        state[0] = 0
        state[1] = 0

    def chunk_copy(c):
        src = z_ref.at[pl.ds(pl.multiple_of(c * chunk_rows, chunk_rows), chunk_rows), :]
        return pltpu.make_async_copy(src, zbuf.at[c % Z_BUFFERS], sems.at[c % Z_BUFFERS])

    acc_ref[...] = jnp.zeros_like(acc_ref)
    first_row = span_ref[:, 0:1] * LANES + span_ref[:, 1:2]
    end_row = span_ref[:, 2:3] * LANES + span_ref[:, 3:4]
    lane_row = lax.broadcasted_iota(I32, (TOK_BLOCK, ROW_CHUNK), 1)

    def body(c, carry):
        for _ in range(Z_BUFFERS):
            nxt = state[0]

            @pl.when(nxt <= jnp.minimum(c + Z_BUFFERS - 1, n_chunks - 1))
            def _():
                chunk_copy(nxt).start()
                state[0] = nxt + 1

        @pl.when(state[1] <= c)
        def _():
            chunk_copy(c).wait()
            state[1] = c + 1

        slot = c % Z_BUFFERS
        row = (c * ROW_CHUNK + lane_row).astype(F32)
        onehot = jnp.where((row >= first_row) & (row < end_row), 1.0, 0.0).astype(BF16)
        y = jnp.concatenate([zbuf[slot, pl.ds(kc, ROW_CHUNK, stride=ROW_TILE), :] for kc in range(ROW_TILE)],
                            axis=1)
        acc_ref[...] += _dot(onehot, y.astype(BF16))
        return carry

    lax.fori_loop(clo_ref[b], chi_ref[b], body, 0)
    gt2 = mod_ref[0][:, 5 * D_MODEL:6 * D_MODEL]
    o_ref[...] = x1_ref[...] + gt2 * _rms(acc_ref[...], g_ref[...], EPS)


def _combine(ranges, z, span, x1, mod3, mod_row, g_post_ffn):
    clo, chi = ranges
    t = x1.shape[0]
    tb = TOK_BLOCK
    grid_spec = pltpu.PrefetchScalarGridSpec(
        num_scalar_prefetch=2,
        grid=(t // tb,),
        in_specs=[pl.BlockSpec(memory_space=pl.ANY),
                  pl.BlockSpec((tb, LANES), lambda b, *_: (b, 0)),
                  pl.BlockSpec((tb, D_MODEL), lambda b, *_: (b, 0)),
                  pl.BlockSpec((1, 1, N_MOD * D_MODEL), lambda b, *_: (mod_row(b, tb), 0, 0)),
                  pl.BlockSpec((1, D_MODEL), lambda b, *_: (0, 0))],
        out_specs=pl.BlockSpec((tb, D_MODEL), lambda b, *_: (b, 0)),
        scratch_shapes=[pltpu.VMEM((Z_BUFFERS, ROW_CHUNK * ROW_TILE, LANES), F32),
                        pltpu.VMEM((tb, D_MODEL), F32),
                        pltpu.SemaphoreType.DMA((Z_BUFFERS,)),
                        pltpu.SMEM((2,), I32)])
    return pl.pallas_call(
        functools.partial(_combine_kernel, n_chunks=2 * t // ROW_CHUNK),
        grid_spec=grid_spec,
        out_shape=jax.ShapeDtypeStruct((t, D_MODEL), F32),
        compiler_params=_cparams(("arbitrary",)),
        name="combine",
    )(clo, chi, z, span, x1, mod3, g_post_ffn)


def _rope_tables(seq):
    half = HEAD_DIM // 4
    freqs = ROPE_THETA ** (-np.arange(half, dtype=np.float64) / half)
    s = np.arange(seq)
    row = (s // GRID_W)[:, None] * freqs[None, :]
    col = (s % GRID_W)[:, None] * freqs[None, :]
    ang = np.concatenate([row, row, col, col], axis=1)
    ang = np.tile(ang, (1, QK_W // HEAD_DIM))
    lane = np.arange(QK_W)
    sign = np.where((lane % 32) < 16, -1.0, 1.0)[None, :]
    return (jnp.asarray(np.cos(ang), dtype=F32), jnp.asarray(np.sin(ang) * sign, dtype=F32))


def _combine_ranges(rows, n_tok):
    step = TOK_BLOCK // LANES
    nb = n_tok // TOK_BLOCK
    lo = rows[0:nb * step:step, 0]
    hi = jnp.concatenate([lo[1:], jnp.full((1,), 2 * n_tok, I32)])
    return (lo // ROW_CHUNK).astype(I32), ((hi + ROW_CHUNK - 1) // ROW_CHUNK).astype(I32)


def kernel(x_prompt, x_sample, c, cache_k, cache_v, c_ctx, w_mod, b_mod, g_pre_mix, g_post_mix, g_pre_ffn, g_post_ffn, w_in, lam_q1, lam_k1, lam_q2, lam_k2, g_subln, w_proj_attn, w_proj_fourier, w_out, w_router, w_gate, w_up, w_down):
    assert w_mod.shape[0] == 1
    lam_init = 0.8 - 0.6 * math.exp(-0.3 * 0)
    bp, sp, _ = x_prompt.shape
    bs, ss, _ = x_sample.shape

    mod3 = _modulation(c_ctx, c, w_mod[0], b_mod)

    w_in_b = None
    wpa =w_proj_attn[0].astype(BF16)
    wpf = w_proj_fourier[0].astype(BF16)
    wout = w_out[0].astype(BF16)
    wrx = jnp.pad(w_router[0], ((0, 0), (0, LANES - N_EXPERTS))).astype(BF16)
    lam_p = jnp.concatenate([lam_q1, lam_k1, lam_q2, lam_k2], axis=0)

    groups = []
    for x, seq, positional, ctx in ((x_sample, ss, True, (cache_k, cache_v)),
                                    (x_prompt, sp, False, None)):
        nb = x.shape[0]
        t = nb * seq
        x2d = x.reshape(t, D_MODEL)
        if positional:
            mod_row = lambda i, tm, seq=seq: 1 + (i * tm) // seq
        else:
            mod_row = lambda i, tm: 0
        self_contained = ctx is None and not positional
        if self_contained:
            assert PRE_BLOCK % seq == 0
            pre = _ctx_mixer(x2d, mod3, mod_row, g_pre_mix, w_in_b, seq, (lam_p, g_subln, lam_init), tm=PRE_BLOCK)
            o, fm, ga, gf = pre[:4]
        else:
            q, k, v, f, ga, gf, w_in_b = _pre_mixer(x2d, mod3, mod_row, g_pre_mix, w_in[0],
                                                    _rope_tables(seq) if positional else None, seq, tm=ROW_BLOCK)
            o = _attention(lam_p, g_subln, q, k, v, ctx, seq, lam_init)
            fm = _fourier(f, seq)
        x1, h2t, aff_t = _post_mixer(o, fm, ga, gf, x2d, mod3, mod_row, g_post_mix, g_pre_ffn,
                                     wpa, wpf, wout, wrx)
        cap = 2 * t // N_EXPERTS
        assert t <= 1 << TOKEN_BITS
        posm, pack, span, ranges, rows = _route(aff_t, cap)
        idx, qslot = _slot_lists(ranges, posm, pack, cap)
        groups.append(dict(x1=x1, h2t=h2t, idx=idx, qslot=qslot, span=span, ranges=_combine_ranges(rows, t),
                           mod_row=mod_row, cache=pre[4:] if self_contained else None, shape=x.shape))

    gs_, gc = groups
    zc, zs = _moe(gc["idx"], gs_["idx"], gc["qslot"], gs_["qslot"], gc["h2t"], gs_["h2t"], wrx,
                  w_gate[0], w_up[0], w_down[0])
    outs = []
    for g, z in ((gc, zc), (gs_, zs)):
        out = _combine(g["ranges"], z, g["span"], g["x1"], mod3, g["mod_row"], g_post_ffn)
        outs.append(out.reshape(g["shape"]))
    new_k, new_v = gc["cache"]
    return (outs[0], outs[1], new_k, new_v)
```
